```python
import math
import jax
import jax.numpy as jnp
from jax import lax
import numpy as np

D_MODEL = 2048
BATCH = 16
SEQ = 256
DEPTH = 2
DEC_BATCH = 8
DEC_SEQ = 1024
PAST_LEN = 256

GRID_W = 64
EPS = 1e-6
N_BRANCH = 4
BRANCH_W = 1024
N_MOD = 6

D_RNN = 1024
LRU_BLOCKS = 8
LRU_BS = D_RNN // LRU_BLOCKS
LRU_CONV = 4
LRU_C = 8.0

RET_HEADS = 4
RET_DK = 128
RET_DV = 256
RET_CHUNK = 64

SSD_HEADS = 16
SSD_P = 64
SSD_N = 128
SSD_GROUPS = 2
SSD_CONV = 4
SSD_CHUNK = 64
D_SSD = SSD_HEADS * SSD_P
SSD_CONV_CH = D_SSD + 2 * SSD_GROUPS * SSD_N

NA_HEADS = 8
NA_HD = 128
NA_W = NA_HEADS * NA_HD
NA_WR = 8
NA_WC = 16
ROPE_BASE = 10000.0
Q_BLOCK = 128

D_FF = 5632
FFN_CONV = 3

IN_SIZES = (D_RNN, D_RNN,
            RET_HEADS * RET_DK, RET_HEADS * RET_DK, RET_HEADS * RET_DV, RET_HEADS * RET_DV,
            D_SSD, SSD_CONV_CH, 2 * SSD_HEADS,
            NA_W, NA_W, NA_W)
D_IN = sum(IN_SIZES)
IN_OFFSETS = tuple(int(s) for s in np.cumsum(IN_SIZES)[:-1])

kernel_name = "hybrid_diffusion_prefix_trunk_step"


def rmsnorm(x, g):
    xf = x.astype(jnp.float32)
    y = xf * lax.rsqrt(jnp.mean(xf * xf, axis=-1, keepdims=True) + EPS)
    return (y * g).astype(x.dtype)


def _rev(t):
    return jnp.flip(t, axis=1)


def dwconv(x, w, b, pad_left):
    K = w.shape[0]
    L = x.shape[1]
    xp = jnp.pad(x, ((0, 0), (pad_left, K - 1 - pad_left), (0, 0)))
    out = b
    for kk in range(K):
        out = out + xp[:, kk:kk + L] * w[kk]
    return out


def chunked_linear_recurrence(q, k, v, log_a, s0, chunk):
    f32 = jnp.float32
    b, L, h, dk = q.shape
    dv = v.shape[-1]
    nc = L // chunk
    qc = q.astype(f32).reshape(b, nc, chunk, h, dk)
    kc = k.astype(f32).reshape(b, nc, chunk, h, dk)
    vc = v.astype(f32).reshape(b, nc, chunk, h, dv)
    cum = jnp.cumsum(log_a.astype(f32).reshape(b, nc, chunk, h), axis=2)
    causal = jnp.tril(jnp.ones((chunk, chunk), dtype=bool))
    seg = cum[:, :, :, None, :] - cum[:, :, None, :, :]
    decay = jnp.exp(jnp.where(causal[None, None, :, :, None], seg, -jnp.inf))
    scores = jnp.einsum('bcthd,bcshd->bctsh', qc, kc) * decay
    y_intra = jnp.einsum('bctsh,bcshv->bcthv', scores, vc)
    tail = jnp.exp(cum[:, :, -1:, :] - cum)
    chunk_states = jnp.einsum('bcshd,bcsh,bcshv->bchdv', kc, tail, vc)
    chunk_decay = jnp.exp(cum[:, :, -1, :])

    def step(S, inp):
        st, dec = inp
        return dec[..., None, None] * S + st, S

    s_final, s_in = lax.scan(step, s0.astype(f32),
                             (chunk_states.swapaxes(0, 1), chunk_decay.swapaxes(0, 1)))
    s_in = s_in.swapaxes(0, 1)
    y_inter = jnp.einsum('bcthd,bchdv->bcthv', qc * jnp.exp(cum)[..., None], s_in)
    return (y_intra + y_inter).reshape(b, L, h, dv), s_final


def _lin_combine(e1, e2):
    a1, b1 = e1
    a2, b2 = e2
    return a1 * a2, a2 * b1 + b2


def rglru_direction(xc, wa, ba, wx, bx, lam, h0, reverse):
    f32 = jnp.float32
    b, L, c = xc.shape
    xb = xc.reshape(b, L, LRU_BLOCKS, LRU_BS)
    r = jax.nn.sigmoid(jnp.einsum('blnd,nde->blne', xb, wa).reshape(b, L, c) + ba)
    i = jax.nn.sigmoid(jnp.einsum('blnd,nde->blne', xb, wx).reshape(b, L, c) + bx)
    log_a = (-LRU_C * r * jax.nn.softplus(-lam)).astype(f32)
    u = jnp.sqrt(-jnp.expm1(2.0 * log_a)) * (i * xc).astype(f32)
    if reverse:
        log_a, u = _rev(log_a), _rev(u)
    a_cum, h = lax.associative_scan(_lin_combine, (jnp.exp(log_a), u), axis=1)
    h = h + a_cum * h0.astype(f32)[:, None, :]
    h_final = h[:, -1]
    if reverse:
        h = _rev(h)
    return h, h_final


def lru_branch(x_in, gate_in, p, h0_f, h0_b):
    xc = dwconv(x_in, p["lru_conv_w"], p["lru_conv_b"], LRU_CONV // 2)
    h_f, s_f = rglru_direction(xc, p["lru_wa"][0], p["lru_ba"][0], p["lru_wx"][0],
                               p["lru_bx"][0], p["lru_lambda"][0], h0_f, False)
    h_b, s_b = rglru_direction(xc, p["lru_wa"][1], p["lru_ba"][1], p["lru_wx"][1],
                               p["lru_bx"][1], p["lru_lambda"][1], h0_b, True)
    y = (h_f + h_b).astype(x_in.dtype) * jax.nn.gelu(gate_in)
    return y, s_f, s_b


def retention_branch(q, k, v, g, p, s0_f, s0_b):
    f32 = jnp.float32
    b, L, _ = q.shape
    q = q.reshape(b, L, RET_HEADS, RET_DK)
    k = k.reshape(b, L, RET_HEADS, RET_DK) * (RET_DK ** -0.5)
    v = v.reshape(b, L, RET_HEADS, RET_DV)
    hh = jnp.arange(RET_HEADS, dtype=f32)
    la_f = jnp.broadcast_to(jnp.log1p(-jnp.exp2(-5.0 - hh)), (b, L, RET_HEADS))
    la_b = jnp.broadcast_to(jnp.log1p(-jnp.exp2(-5.5 - hh)), (b, L, RET_HEADS))
    y_f, s_f = chunked_linear_recurrence(q, k, v, la_f, s0_f, RET_CHUNK)
    y_b, s_b = chunked_linear_recurrence(_rev(q), _rev(k), _rev(v), la_b, s0_b, RET_CHUNK)
    y = y_f + _rev(y_b)
    mu = jnp.mean(y, axis=-1, keepdims=True)
    var = jnp.mean(jnp.square(y - mu), axis=-1, keepdims=True)
    y = (y - mu) * lax.rsqrt(var + EPS) * p["ret_gn_g"].reshape(RET_HEADS, RET_DV)
    y = y.reshape(b, L, RET_HEADS * RET_DV) * jax.nn.silu(g.astype(f32))
    return y.astype(g.dtype), s_f, s_b


def ssd_branch(z, xbc, dt_raw, p, s0_f, s0_b):
    f32 = jnp.float32
    b, L, _ = z.shape
    xbc = jax.nn.silu(dwconv(xbc, p["ssd_conv_w"], p["ssd_conv_b"], SSD_CONV // 2))
    xs, Bm, Cm = jnp.split(xbc, [D_SSD, D_SSD + SSD_GROUPS * SSD_N], axis=-1)
    rep = SSD_HEADS // SSD_GROUPS
    x = xs.reshape(b, L, SSD_HEADS, SSD_P).astype(f32)
    Bh = jnp.repeat(Bm.reshape(b, L, SSD_GROUPS, SSD_N), rep, axis=2)
    Ch = jnp.repeat(Cm.reshape(b, L, SSD_GROUPS, SSD_N), rep, axis=2)
    A = -jnp.exp(p["ssd_a_log"].astype(f32))
    dt = jax.nn.softplus(dt_raw.reshape(b, L, 2, SSD_HEADS).astype(f32)
                         + p["ssd_dt_bias"].astype(f32))
    y_f, s_f = chunked_linear_recurrence(Ch, Bh, x * dt[:, :, 0, :, None],
                                         dt[:, :, 0] * A[0], s0_f, SSD_CHUNK)
    y_b, s_b = chunked_linear_recurrence(_rev(Ch), _rev(Bh), _rev(x * dt[:, :, 1, :, None]),
                                         _rev(dt[:, :, 1] * A[1]), s0_b, SSD_CHUNK)
    y = y_f + _rev(y_b) + x * p["ssd_d"].astype(f32)[:, None]
    y = y.reshape(b, L, D_SSD) * jax.nn.silu(z.astype(f32))
    return rmsnorm(y, p["ssd_norm_g"]).astype(z.dtype), s_f, s_b


def rope_2d(x):
    f32 = jnp.float32
    b, L, h, d = x.shape
    t = jnp.arange(L)
    half = d // 2
    inv = ROPE_BASE ** (-jnp.arange(half // 2, dtype=f32) / (half // 2))

    def rotate(xa, pos):
        ang = pos.astype(f32)[:, None] * inv
        cos = jnp.cos(ang)[None, :, None, :]
        sin = jnp.sin(ang)[None, :, None, :]
        x1, x2 = jnp.split(xa.astype(f32), 2, axis=-1)
        return jnp.concatenate([x1 * cos - x2 * sin, x1 * sin + x2 * cos], axis=-1)

    out = jnp.concatenate([rotate(x[..., :half], t // GRID_W), rotate(x[..., half:], t % GRID_W)], axis=-1)
    return out.astype(x.dtype)


def context_attention(q, k, v):
    b, L, h, d = q.shape
    nb = L // Q_BLOCK
    qb = q.reshape(b, nb, Q_BLOCK, h, d).swapaxes(0, 1)
    scale = d ** -0.5

    def one(qi):
        s = jnp.einsum('bqhd,bkhd->bhqk', qi, k).astype(jnp.float32) * scale
        pr = jax.nn.softmax(s, axis=-1).astype(v.dtype)
        return jnp.einsum('bhqk,bkhd->bqhd', pr, v)

    o = lax.map(one, qb)
    return o.swapaxes(0, 1).reshape(b, L, h, d)


def neighbourhood_attention(q, k, v, k_ctx, v_ctx, rpb):
    b, T, h, d = q.shape
    rows = T // GRID_W
    wr = min(NA_WR, rows)
    nk = wr * GRID_W
    r = jnp.arange(rows)
    row_start = jnp.clip(r - wr // 2, 0, rows - wr)
    row_idx = row_start[:, None] + jnp.arange(wr)[None, :]
    kg = k.reshape(b, rows, GRID_W, h, d)[:, row_idx].reshape(b, rows, nk, h, d)
    vg = v.reshape(b, rows, GRID_W, h, d)[:, row_idx].reshape(b, rows, nk, h, d)
    qg = q.reshape(b, rows, GRID_W, h, d)
    scale = d ** -0.5
    cq = jnp.arange(GRID_W)
    col_start = jnp.clip(cq - NA_WC // 2, 0, GRID_W - NA_WC)
    kc = jnp.tile(jnp.arange(GRID_W), wr)
    valid = (kc[None, :] >= col_start[:, None]) & (kc[None, :] < col_start[:, None] + NA_WC)
    row_off = jnp.repeat(row_idx - r[:, None], GRID_W, axis=1)
    col_off = jnp.clip(kc[None, :] - cq[:, None], 1 - NA_WC, NA_WC - 1)
    bias = rpb[:, row_off[:, None, :] + NA_WR - 1, col_off[None, :, :] + NA_WC - 1]
    s_loc = jnp.einsum('brqhd,brkhd->bhrqk', qg, kg).astype(jnp.float32) * scale + bias[None].astype(jnp.float32)
    s_loc = jnp.where(valid[None, None, None], s_loc, -1e30)
    s_ctx = jnp.einsum('brqhd,bchd->bhrqc', qg, k_ctx).astype(jnp.float32) * scale
    pr = jax.nn.softmax(jnp.concatenate([s_loc, s_ctx], axis=-1), axis=-1).astype(v.dtype)
    o = (jnp.einsum('bhrqk,brkhd->brqhd', pr[..., :nk], vg)
         + jnp.einsum('bhrqc,bchd->brqhd', pr[..., nk:], v_ctx))
    return o.reshape(b, T, h, d)


def mixer(xn, p, init, ctx_kv):
    b, L, _ = xn.shape
    (lru_x, lru_g, ret_q, ret_k, ret_v, ret_g, ssd_z, ssd_xbc, ssd_dt,
     na_q, na_k, na_v) = jnp.split(xn @ p["w_in"], IN_OFFSETS, axis=-1)
    y_lru, lru_f, lru_b = lru_branch(lru_x, lru_g, p, init[0], init[1])
    y_ret, ret_f, ret_b = retention_branch(ret_q, ret_k, ret_v, ret_g, p, init[2], init[3])
    y_ssd, ssd_f, ssd_b = ssd_branch(ssd_z, ssd_xbc, ssd_dt, p, init[4], init[5])
    q = rmsnorm(na_q.reshape(b, L, NA_HEADS, NA_HD), p["na_q_g"])
    k = rmsnorm(na_k.reshape(b, L, NA_HEADS, NA_HD), p["na_k_g"])
    v = na_v.reshape(b, L, NA_HEADS, NA_HD)
    if ctx_kv is None:
        y_na = context_attention(q, k, v)
        kv = (k, v)
    else:
        y_na = neighbourhood_attention(rope_2d(q), rope_2d(k), v, ctx_kv[0], ctx_kv[1], p["na_rpb"])
        kv = ctx_kv
    y_na = y_na.reshape(b, L, NA_W)
    gates = jax.nn.sigmoid(xn @ p["w_gate"] + p["b_gate"]).reshape(b, L, N_BRANCH, D_MODEL)
    branches = jnp.stack([y_lru, y_ret, y_ssd, y_na], axis=2)
    proj = jnp.einsum('blnc,ncd->blnd', branches, p["w_branch"])
    out = jnp.sum(gates * proj, axis=2) @ p["w_out"]
    return out, (kv[0], kv[1], lru_f, lru_b, ret_f, ret_b, ssd_f, ssd_b)


def conv_ffn(xn, p):
    a, v = jnp.split(xn @ p["ffn_w_up"], 2, axis=-1)
    a = dwconv(a, p["ffn_conv_w"], p["ffn_conv_b"], FFN_CONV // 2)
    return (jax.nn.gelu(a) * v) @ p["ffn_w_down"]


def block(x, cond, p, init, ctx_kv):
    sh_a, sc_a, g_a, sh_f, sc_f, g_f = jnp.split(jax.nn.silu(cond) @ p["w_ada"] + p["b_ada"], N_MOD, axis=-1)
    xn = rmsnorm(x, p["norm1_g"]) * (1.0 + sc_a) + sh_a
    y, states = mixer(xn, p, init, ctx_kv)
    x = x + g_a * y
    xn = rmsnorm(x, p["norm2_g"]) * (1.0 + sc_f) + sh_f
    x = x + g_f * conv_ffn(xn, p)
    return x, states


def setup_inputs(seed: int = 0) -> dict:
    key = jax.random.key(seed)
    ks = iter(jax.random.split(key, 64))
    f32 = jnp.float32

    def nrm(shape, scale=1.0):
        return jax.random.normal(next(ks), shape, f32) * scale

    def gain(shape):
        return 1.0 + nrm(shape, 0.05)

    def unif(shape, lo, hi):
        return jax.random.uniform(next(ks), shape, f32, lo, hi)

    D = D_MODEL
    a0 = unif((DEPTH, 2, D_RNN), 0.9, 0.999)
    dt0 = jnp.exp(unif((DEPTH, 2, SSD_HEADS), math.log(1e-3), math.log(1e-1)))
    return {
        "x_prompt": nrm((BATCH, SEQ, D)),
        "x_sample": nrm((DEC_BATCH, DEC_SEQ, D)),
        "cache_na_k": nrm((DEC_BATCH, DEPTH, PAST_LEN, NA_HEADS, NA_HD)),
        "cache_na_v": nrm((DEC_BATCH, DEPTH, PAST_LEN, NA_HEADS, NA_HD)),
        "state_lru_f": nrm((DEC_BATCH, DEPTH, D_RNN), 0.5),
        "state_lru_b": nrm((DEC_BATCH, DEPTH, D_RNN), 0.5),
        "state_ret_f": nrm((DEC_BATCH, DEPTH, RET_HEADS, RET_DK, RET_DV), 0.5),
        "state_ret_b": nrm((DEC_BATCH, DEPTH, RET_HEADS, RET_DK, RET_DV), 0.5),
        "state_ssd_f": nrm((DEC_BATCH, DEPTH, SSD_HEADS, SSD_N, SSD_P), 0.5),
        "state_ssd_b": nrm((DEC_BATCH, DEPTH, SSD_HEADS, SSD_N, SSD_P), 0.5),
        "c": nrm((DEC_BATCH, D)),
        "c_ctx": nrm((D,)),
        "norm1_g": gain((DEPTH, D)),
        "norm2_g": gain((DEPTH, D)),
        "w_ada": nrm((DEPTH, D, N_MOD * D), 0.5 * D ** -0.5),
        "b_ada": nrm((DEPTH, N_MOD * D), 0.02),
        "w_in": nrm((DEPTH, D, D_IN), D ** -0.5),
        "w_gate": nrm((DEPTH, D, N_BRANCH * D), D ** -0.5),
        "b_gate": nrm((DEPTH, N_BRANCH * D), 0.02),
        "w_branch": nrm((DEPTH, N_BRANCH, BRANCH_W, D), BRANCH_W ** -0.5),
        "w_out": nrm((DEPTH, D, D), D ** -0.5),
        "lru_conv_w": nrm((DEPTH, LRU_CONV, D_RNN), LRU_CONV ** -0.5),
        "lru_conv_b": nrm((DEPTH, D_RNN), 0.02),
        "lru_wa": nrm((DEPTH, 2, LRU_BLOCKS, LRU_BS, LRU_BS), LRU_BS ** -0.5),
        "lru_ba": nrm((DEPTH, 2, D_RNN), 0.02),
        "lru_wx": nrm((DEPTH, 2, LRU_BLOCKS, LRU_BS, LRU_BS), LRU_BS ** -0.5),
        "lru_bx": nrm((DEPTH, 2, D_RNN), 0.02),
        "lru_lambda": jnp.log(a0) - jnp.log1p(-a0),
        "ret_gn_g": gain((DEPTH, RET_HEADS * RET_DV)),
        "ssd_conv_w": nrm((DEPTH, SSD_CONV, SSD_CONV_CH), SSD_CONV ** -0.5),
        "ssd_conv_b": nrm((DEPTH, SSD_CONV_CH), 0.02),
        "ssd_a_log": jnp.log(unif((DEPTH, 2, SSD_HEADS), 1.0, 16.0)),
        "ssd_dt_bias": dt0 + jnp.log(-jnp.expm1(-dt0)),
        "ssd_d": gain((DEPTH, SSD_HEADS)),
        "ssd_norm_g": gain((DEPTH, D_SSD)),
        "na_q_g": gain((DEPTH, NA_HD)),
        "na_k_g": gain((DEPTH, NA_HD)),
        "na_rpb": nrm((DEPTH, NA_HEADS, 2 * NA_WR - 1, 2 * NA_WC - 1), 0.1),
        "ffn_w_up": nrm((DEPTH, D, 2 * D_FF), D ** -0.5),
        "ffn_conv_w": nrm((DEPTH, FFN_CONV, D_FF), FFN_CONV ** -0.5),
        "ffn_conv_b": nrm((DEPTH, D_FF), 0.02),
        "ffn_w_down": nrm((DEPTH, D_FF, D), D_FF ** -0.5),
    }


def reference(x_prompt, x_sample, cache_na_k, cache_na_v, state_lru_f, state_lru_b,
              state_ret_f, state_ret_b, state_ssd_f, state_ssd_b, c, c_ctx,
              norm1_g, norm2_g, w_ada, b_ada, w_in, w_gate, b_gate, w_branch, w_out,
              lru_conv_w, lru_conv_b, lru_wa, lru_ba, lru_wx, lru_bx, lru_lambda,
              ret_gn_g, ssd_conv_w, ssd_conv_b, ssd_a_log, ssd_dt_bias, ssd_d, ssd_norm_g,
              na_q_g, na_k_g, na_rpb, ffn_w_up, ffn_conv_w, ffn_conv_b, ffn_w_down):
    f32 = jnp.float32
    layers = [dict(norm1_g=norm1_g[l], norm2_g=norm2_g[l], w_ada=w_ada[l], b_ada=b_ada[l],
                   w_in=w_in[l], w_gate=w_gate[l], b_gate=b_gate[l], w_branch=w_branch[l], w_out=w_out[l],
                   lru_conv_w=lru_conv_w[l], lru_conv_b=lru_conv_b[l], lru_wa=lru_wa[l], lru_ba=lru_ba[l],
                   lru_wx=lru_wx[l], lru_bx=lru_bx[l], lru_lambda=lru_lambda[l], ret_gn_g=ret_gn_g[l],
                   ssd_conv_w=ssd_conv_w[l], ssd_conv_b=ssd_conv_b[l], ssd_a_log=ssd_a_log[l],
                   ssd_dt_bias=ssd_dt_bias[l], ssd_d=ssd_d[l], ssd_norm_g=ssd_norm_g[l],
                   na_q_g=na_q_g[l], na_k_g=na_k_g[l], na_rpb=na_rpb[l], ffn_w_up=ffn_w_up[l],
                   ffn_conv_w=ffn_conv_w[l], ffn_conv_b=ffn_conv_b[l], ffn_w_down=ffn_w_down[l])
              for l in range(DEPTH)]

    xp = x_prompt
    bp = xp.shape[0]
    cond_ctx = c_ctx[None, None, :]
    zero_init = (jnp.zeros((bp, D_RNN), f32), jnp.zeros((bp, D_RNN), f32),
                 jnp.zeros((bp, RET_HEADS, RET_DK, RET_DV), f32), jnp.zeros((bp, RET_HEADS, RET_DK, RET_DV), f32),
                 jnp.zeros((bp, SSD_HEADS, SSD_N, SSD_P), f32), jnp.zeros((bp, SSD_HEADS, SSD_N, SSD_P), f32))
    per_layer = []
    for l in range(DEPTH):
        xp, st = block(xp, cond_ctx, layers[l], zero_init, None)
        per_layer.append(st)
    y_prompt = xp
    new_na_k = jnp.stack([s[0] for s in per_layer], axis=1)
    new_na_v = jnp.stack([s[1] for s in per_layer], axis=1)
    new_lru_f = jnp.stack([s[2] for s in per_layer], axis=1)
    new_lru_b = jnp.stack([s[3] for s in per_layer], axis=1)
    new_ret_f = jnp.stack([s[4] for s in per_layer], axis=1)
    new_ret_b = jnp.stack([s[5] for s in per_layer], axis=1)
    new_ssd_f = jnp.stack([s[6] for s in per_layer], axis=1)
    new_ssd_b = jnp.stack([s[7] for s in per_layer], axis=1)

    xs = x_sample
    cond = c[:, None, :]
    for l in range(DEPTH):
        init = (state_lru_f[:, l], state_lru_b[:, l], state_ret_f[:, l], state_ret_b[:, l],
                state_ssd_f[:, l], state_ssd_b[:, l])
        xs, _ = block(xs, cond, layers[l], init, (cache_na_k[:, l], cache_na_v[:, l]))
    y_sample = xs

    return (y_prompt, y_sample, new_na_k, new_na_v, new_lru_f, new_lru_b,
            new_ret_f, new_ret_b, new_ssd_f, new_ssd_b)
```

```python
import functools
import math

import jax
import jax.numpy as jnp
import numpy as np
from jax import lax
from jax.experimental import pallas as pl
from jax.experimental.pallas import tpu as pltpu

D_MODEL = 2048
DEPTH = 2
GRID_W = 64
EPS = 1e-6
N_BRANCH = 4
BRANCH_W = 1024
N_MOD = 6
D_RNN = 1024
LRU_BLOCKS = 8
LRU_BS = D_RNN // LRU_BLOCKS
LRU_CONV = 4
LRU_C = 8.0
RET_HEADS = 4
RET_DK = 128
RET_DV = 256
RET_CHUNK = 64
SSD_HEADS = 16
SSD_P = 64
SSD_N = 128
SSD_GROUPS = 2
SSD_CONV = 4
SSD_CHUNK = 64
D_SSD = SSD_HEADS * SSD_P
SSD_CONV_CH = D_SSD + 2 * SSD_GROUPS * SSD_N
NA_HEADS = 8
NA_HD = 128
NA_W = NA_HEADS * NA_HD
NA_WR = 8
NA_WC = 16
ROPE_BASE = 10000.0
Q_BLOCK = 128
D_FF = 5632
FFN_CONV = 3
IN_SIZES = (D_RNN, D_RNN,
            RET_HEADS * RET_DK, RET_HEADS * RET_DK, RET_HEADS * RET_DV, RET_HEADS * RET_DV,
            D_SSD, SSD_CONV_CH, 2 * SSD_HEADS,
            NA_W, NA_W, NA_W)
D_IN = sum(IN_SIZES)
IN_OFFSETS = tuple(int(s) for s in np.cumsum(IN_SIZES)[:-1])

V7X_VMEM_LIMIT_BYTES = 56 * 1024 * 1024

f32 = jnp.float32
bf16 = jnp.bfloat16


def _mm_kernel(x_ref, w_ref, o_ref):
    o_ref[...] = jnp.dot(x_ref[...], w_ref[...], preferred_element_type=f32).astype(o_ref.dtype)


def _pick_tile(n, pref):
    t = pref
    while n % t:
        t //= 2
    return t


def matmul(x, w, out_dtype=f32, tm=1024, tn=512):
    m, k = x.shape
    n = w.shape[1]
    n_pad = (-n) % 128
    if n_pad:
        w = jnp.pad(w, ((0, 0), (0, n_pad)))
    m_pad = (-m) % 16
    if m_pad:
        x = jnp.pad(x, ((0, m_pad), (0, 0)))
    mp, np_ = m + m_pad, n + n_pad
    tm = _pick_tile(mp, tm) if mp >= tm else mp
    tn = _pick_tile(np_, tn)
    out = pl.pallas_call(
        _mm_kernel,
        grid=(mp // tm, np_ // tn),
        in_specs=[pl.BlockSpec((tm, k), lambda i, j: (i, 0)),
                  pl.BlockSpec((k, tn), lambda i, j: (0, j))],
        out_specs=pl.BlockSpec((tm, tn), lambda i, j: (i, j)),
        out_shape=jax.ShapeDtypeStruct((mp, np_), out_dtype),
        compiler_params=pltpu.CompilerParams(
            dimension_semantics=("arbitrary", "arbitrary"),
            vmem_limit_bytes=V7X_VMEM_LIMIT_BYTES),
    )(x.astype(bf16), w.astype(bf16))
    return out[:m, :n]


def mm3(x, w):
    b, l, k = x.shape
    return matmul(x.reshape(b * l, k), w).reshape(b, l, w.shape[1])


def rmsnorm(x, g):
    xf = x.astype(f32)
    y = xf * lax.rsqrt(jnp.mean(xf * xf, axis=-1, keepdims=True) + EPS)
    return (y * g).astype(x.dtype)


def _rev(t):
    return jnp.flip(t, axis=1)


def dwconv(x, w, b, pad_left):
    K = w.shape[0]
    L = x.shape[1]
    xp = jnp.pad(x, ((0, 0), (pad_left, K - 1 - pad_left), (0, 0)))
    out = b
    for kk in range(K):
        out = out + xp[:, kk:kk + L] * w[kk]
    return out


def chunked_linear_recurrence(q, k, v, log_a, s0, chunk):
    b, L, h, dk = q.shape
    dv = v.shape[-1]
    nc = L // chunk
    qc = q.astype(f32).reshape(b, nc, chunk, h, dk)
    kc = k.astype(f32).reshape(b, nc, chunk, h, dk)
    vc = v.astype(f32).reshape(b, nc, chunk, h, dv)
    cum = jnp.cumsum(log_a.astype(f32).reshape(b, nc, chunk, h), axis=2)
    causal = jnp.tril(jnp.ones((chunk, chunk), dtype=bool))
    seg = cum[:, :, :, None, :] - cum[:, :, None, :, :]
    decay = jnp.exp(jnp.where(causal[None, None, :, :, None], seg, -jnp.inf))
    scores = jnp.einsum('bcthd,bcshd->bctsh', qc, kc) * decay
    y_intra = jnp.einsum('bctsh,bcshv->bcthv', scores, vc)
    tail = jnp.exp(cum[:, :, -1:, :] - cum)
    chunk_states = jnp.einsum('bcshd,bcsh,bcshv->bchdv', kc, tail, vc)
    chunk_decay = jnp.exp(cum[:, :, -1, :])

    def step(S, inp):
        st, dec = inp
        return dec[..., None, None] * S + st, S

    s_final, s_in = lax.scan(step, s0.astype(f32),
                             (chunk_states.swapaxes(0, 1), chunk_decay.swapaxes(0, 1)))
    s_in = s_in.swapaxes(0, 1)
    y_inter = jnp.einsum('bcthd,bchdv->bcthv', qc * jnp.exp(cum)[..., None], s_in)
    return (y_intra + y_inter).reshape(b, L, h, dv), s_final


def _lin_combine(e1, e2):
    a1, b1 = e1
    a2, b2 = e2
    return a1 * a2, a2 * b1 + b2


def rglru_direction(xc, wa, ba, wx, bx, lam, h0, reverse):
    b, L, c = xc.shape
    xb = xc.reshape(b, L, LRU_BLOCKS, LRU_BS)
    r = jax.nn.sigmoid(jnp.einsum('blnd,nde->blne', xb, wa).reshape(b, L, c) + ba)
    i = jax.nn.sigmoid(jnp.einsum('blnd,nde->blne', xb, wx).reshape(b, L, c) + bx)
    log_a = (-LRU_C * r * jax.nn.softplus(-lam)).astype(f32)
    u = jnp.sqrt(-jnp.expm1(2.0 * log_a)) * (i * xc).astype(f32)
    if reverse:
        log_a, u = _rev(log_a), _rev(u)
    a_cum, h = lax.associative_scan(_lin_combine, (jnp.exp(log_a), u), axis=1)
    h = h + a_cum * h0.astype(f32)[:, None, :]
    h_final = h[:, -1]
    if reverse:
        h = _rev(h)
    return h, h_final


def lru_branch(x_in, gate_in, p, h0_f, h0_b):
    xc = dwconv(x_in, p["lru_conv_w"], p["lru_conv_b"], LRU_CONV // 2)
    h_f, s_f = rglru_direction(xc, p["lru_wa"][0], p["lru_ba"][0], p["lru_wx"][0],
                               p["lru_bx"][0], p["lru_lambda"][0], h0_f, False)
    h_b, s_b = rglru_direction(xc, p["lru_wa"][1], p["lru_ba"][1], p["lru_wx"][1],
                               p["lru_bx"][1], p["lru_lambda"][1], h0_b, True)
    y = (h_f + h_b).astype(x_in.dtype) * jax.nn.gelu(gate_in)
    return y, s_f, s_b


def retention_branch(q, k, v, g, p, s0_f, s0_b):
    b, L, _ = q.shape
    q = q.reshape(b, L, RET_HEADS, RET_DK)
    k = k.reshape(b, L, RET_HEADS, RET_DK) * (RET_DK ** -0.5)
    v = v.reshape(b, L, RET_HEADS, RET_DV)
    hh = jnp.arange(RET_HEADS, dtype=f32)
    la_f = jnp.broadcast_to(jnp.log1p(-jnp.exp2(-5.0 - hh)), (b, L, RET_HEADS))
    la_b = jnp.broadcast_to(jnp.log1p(-jnp.exp2(-5.5 - hh)), (b, L, RET_HEADS))
    y_f, s_f = chunked_linear_recurrence(q, k, v, la_f, s0_f, RET_CHUNK)
    y_b, s_b = chunked_linear_recurrence(_rev(q), _rev(k), _rev(v), la_b, s0_b, RET_CHUNK)
    y = y_f + _rev(y_b)
    mu = jnp.mean(y, axis=-1, keepdims=True)
    var = jnp.mean(jnp.square(y - mu), axis=-1, keepdims=True)
    y = (y - mu) * lax.rsqrt(var + EPS) * p["ret_gn_g"].reshape(RET_HEADS, RET_DV)
    y = y.reshape(b, L, RET_HEADS * RET_DV) * jax.nn.silu(g.astype(f32))
    return y.astype(g.dtype), s_f, s_b


def ssd_branch(z, xbc, dt_raw, p, s0_f, s0_b):
    b, L, _ = z.shape
    xbc = jax.nn.silu(dwconv(xbc, p["ssd_conv_w"], p["ssd_conv_b"], SSD_CONV // 2))
    xs, Bm, Cm = jnp.split(xbc, [D_SSD, D_SSD + SSD_GROUPS * SSD_N], axis=-1)
    rep = SSD_HEADS // SSD_GROUPS
    x = xs.reshape(b, L, SSD_HEADS, SSD_P).astype(f32)
    Bh = jnp.repeat(Bm.reshape(b, L, SSD_GROUPS, SSD_N), rep, axis=2)
    Ch = jnp.repeat(Cm.reshape(b, L, SSD_GROUPS, SSD_N), rep, axis=2)
    A = -jnp.exp(p["ssd_a_log"].astype(f32))
    dt = jax.nn.softplus(dt_raw.reshape(b, L, 2, SSD_HEADS).astype(f32)
                         + p["ssd_dt_bias"].astype(f32))
    y_f, s_f = chunked_linear_recurrence(Ch, Bh, x * dt[:, :, 0, :, None],
                                         dt[:, :, 0] * A[0], s0_f, SSD_CHUNK)
    y_b, s_b = chunked_linear_recurrence(_rev(Ch), _rev(Bh), _rev(x * dt[:, :, 1, :, None]),
                                         _rev(dt[:, :, 1] * A[1]), s0_b, SSD_CHUNK)
    y = y_f + _rev(y_b) + x * p["ssd_d"].astype(f32)[:, None]
    y = y.reshape(b, L, D_SSD) * jax.nn.silu(z.astype(f32))
    return rmsnorm(y, p["ssd_norm_g"]).astype(z.dtype), s_f, s_b


def rope_2d(x):
    b, L, h, d = x.shape
    t = jnp.arange(L)
    half = d // 2
    inv = ROPE_BASE ** (-jnp.arange(half // 2, dtype=f32) / (half // 2))

    def rotate(xa, pos):
        ang = pos.astype(f32)[:, None] * inv
        cos = jnp.cos(ang)[None, :, None, :]
        sin = jnp.sin(ang)[None, :, None, :]
        x1, x2 = jnp.split(xa.astype(f32), 2, axis=-1)
        return jnp.concatenate([x1 * cos - x2 * sin, x1 * sin + x2 * cos], axis=-1)

    out = jnp.concatenate([rotate(x[..., :half], t // GRID_W), rotate(x[..., half:], t % GRID_W)], axis=-1)
    return out.astype(x.dtype)


def context_attention(q, k, v):
    b, L, h, d = q.shape
    scale = d ** -0.5
    s = jnp.einsum('bqhd,bkhd->bhqk', q, k).astype(f32) * scale
    pr = jax.nn.softmax(s, axis=-1).astype(v.dtype)
    return jnp.einsum('bhqk,bkhd->bqhd', pr, v)


def neighbourhood_attention(q, k, v, k_ctx, v_ctx, rpb):
    b, T, h, d = q.shape
    rows = T // GRID_W
    wr = min(NA_WR, rows)
    nk = wr * GRID_W
    r = jnp.arange(rows)
    row_start = jnp.clip(r - wr // 2, 0, rows - wr)
    row_idx = row_start[:, None] + jnp.arange(wr)[None, :]
    kg = k.reshape(b, rows, GRID_W, h, d)[:, row_idx].reshape(b, rows, nk, h, d)
    vg = v.reshape(b, rows, GRID_W, h, d)[:, row_idx].reshape(b, rows, nk, h, d)
    qg = q.reshape(b, rows, GRID_W, h, d)
    scale = d ** -0.5
    cq = jnp.arange(GRID_W)
    col_start = jnp.clip(cq - NA_WC // 2, 0, GRID_W - NA_WC)
    kc = jnp.tile(jnp.arange(GRID_W), wr)
    valid = (kc[None, :] >= col_start[:, None]) & (kc[None, :] < col_start[:, None] + NA_WC)
    row_off = jnp.repeat(row_idx - r[:, None], GRID_W, axis=1)
    col_off = jnp.clip(kc[None, :] - cq[:, None], 1 - NA_WC, NA_WC - 1)
    bias = rpb[:, row_off[:, None, :] + NA_WR - 1, col_off[None, :, :] + NA_WC - 1]
    s_loc = jnp.einsum('brqhd,brkhd->bhrqk', qg, kg).astype(f32) * scale + bias[None].astype(f32)
    s_loc = jnp.where(valid[None, None, None], s_loc, -1e30)
    s_ctx = jnp.einsum('brqhd,bchd->bhrqc', qg, k_ctx).astype(f32) * scale
    pr = jax.nn.softmax(jnp.concatenate([s_loc, s_ctx], axis=-1), axis=-1).astype(v.dtype)
    o = (jnp.einsum('bhrqk,brkhd->brqhd', pr[..., :nk], vg)
         + jnp.einsum('bhrqc,bchd->brqhd', pr[..., nk:], v_ctx))
    return o.reshape(b, T, h, d)


def mixer(xn, p, init, ctx_kv):
    b, L, _ = xn.shape
    (lru_x, lru_g, ret_q, ret_k, ret_v, ret_g, ssd_z, ssd_xbc, ssd_dt,
     na_q, na_k, na_v) = jnp.split(mm3(xn, p["w_in"]), IN_OFFSETS, axis=-1)
    y_lru, lru_f, lru_b = lru_branch(lru_x, lru_g, p, init[0], init[1])
    y_ret, ret_f, ret_b = retention_branch(ret_q, ret_k, ret_v, ret_g, p, init[2], init[3])
    y_ssd, ssd_f, ssd_b = ssd_branch(ssd_z, ssd_xbc, ssd_dt, p, init[4], init[5])
    q = rmsnorm(na_q.reshape(b, L, NA_HEADS, NA_HD), p["na_q_g"])
    k = rmsnorm(na_k.reshape(b, L, NA_HEADS, NA_HD), p["na_k_g"])
    v = na_v.reshape(b, L, NA_HEADS, NA_HD)
    if ctx_kv is None:
        y_na = context_attention(q, k, v)
        kv = (k, v)
    else:
        y_na = neighbourhood_attention(rope_2d(q), rope_2d(k), v, ctx_kv[0], ctx_kv[1], p["na_rpb"])
        kv = ctx_kv
    y_na = y_na.reshape(b, L, NA_W)
    gates = jax.nn.sigmoid(mm3(xn, p["w_gate"]) + p["b_gate"]).reshape(b, L, N_BRANCH, D_MODEL)
    merged = 0.0
    for n, y in enumerate((y_lru, y_ret, y_ssd, y_na)):
        merged = merged + gates[:, :, n] * mm3(y, p["w_branch"][n])
    out = mm3(merged, p["w_out"])
    return out, (kv[0], kv[1], lru_f, lru_b, ret_f, ret_b, ssd_f, ssd_b)


def conv_ffn(xn, p):
    a, v = jnp.split(mm3(xn, p["ffn_w_up"]), 2, axis=-1)
    a = dwconv(a, p["ffn_conv_w"], p["ffn_conv_b"], FFN_CONV // 2)
    return mm3(jax.nn.gelu(a) * v, p["ffn_w_down"])


def block(x, cond, p, init, ctx_kv):
    nb = cond.shape[0]
    mod = matmul(jax.nn.silu(cond).reshape(nb, D_MODEL), p["w_ada"]).reshape(nb, 1, -1) + p["b_ada"]
    sh_a, sc_a, g_a, sh_f, sc_f, g_f = jnp.split(mod, N_MOD, axis=-1)
    xn = rmsnorm(x, p["norm1_g"]) * (1.0 + sc_a) + sh_a
    y, states = mixer(xn, p, init, ctx_kv)
    x = x + g_a * y
    xn = rmsnorm(x, p["norm2_g"]) * (1.0 + sc_f) + sh_f
    x = x + g_f * conv_ffn(xn, p)
    return x, states


def kernel(x_prompt, x_sample, cache_na_k, cache_na_v, state_lru_f, state_lru_b,
           state_ret_f, state_ret_b, state_ssd_f, state_ssd_b, c, c_ctx,
           norm1_g, norm2_g, w_ada, b_ada, w_in, w_gate, b_gate, w_branch, w_out,
           lru_conv_w, lru_conv_b, lru_wa, lru_ba, lru_wx, lru_bx, lru_lambda,
           ret_gn_g, ssd_conv_w, ssd_conv_b, ssd_a_log, ssd_dt_bias, ssd_d, ssd_norm_g,
           na_q_g, na_k_g, na_rpb, ffn_w_up, ffn_conv_w, ffn_conv_b, ffn_w_down):
    layers = [dict(norm1_g=norm1_g[l], norm2_g=norm2_g[l], w_ada=w_ada[l], b_ada=b_ada[l],
                   w_in=w_in[l], w_gate=w_gate[l], b_gate=b_gate[l], w_branch=w_branch[l], w_out=w_out[l],
                   lru_conv_w=lru_conv_w[l], lru_conv_b=lru_conv_b[l], lru_wa=lru_wa[l], lru_ba=lru_ba[l],
                   lru_wx=lru_wx[l], lru_bx=lru_bx[l], lru_lambda=lru_lambda[l], ret_gn_g=ret_gn_g[l],
                   ssd_conv_w=ssd_conv_w[l], ssd_conv_b=ssd_conv_b[l], ssd_a_log=ssd_a_log[l],
                   ssd_dt_bias=ssd_dt_bias[l], ssd_d=ssd_d[l], ssd_norm_g=ssd_norm_g[l],
                   na_q_g=na_q_g[l], na_k_g=na_k_g[l], na_rpb=na_rpb[l], ffn_w_up=ffn_w_up[l],
                   ffn_conv_w=ffn_conv_w[l], ffn_conv_b=ffn_conv_b[l], ffn_w_down=ffn_w_down[l])
              for l in range(DEPTH)]

    xp = x_prompt
    bp = xp.shape[0]
    cond_ctx = c_ctx[None, None, :]
    zero_init = (jnp.zeros((bp, D_RNN), f32), jnp.zeros((bp, D_RNN), f32),
                 jnp.zeros((bp, RET_HEADS, RET_DK, RET_DV), f32), jnp.zeros((bp, RET_HEADS, RET_DK, RET_DV), f32),
                 jnp.zeros((bp, SSD_HEADS, SSD_N, SSD_P), f32), jnp.zeros((bp, SSD_HEADS, SSD_N, SSD_P), f32))
    per_layer = []
    for l in range(DEPTH):
        xp, st = block(xp, cond_ctx, layers[l], zero_init, None)
        per_layer.append(st)
    outs = [jnp.stack([s[i] for s in per_layer], axis=1) for i in range(8)]

    xs = x_sample
    cond = c[:, None, :]
    for l in range(DEPTH):
        init = (state_lru_f[:, l], state_lru_b[:, l], state_ret_f[:, l], state_ret_b[:, l],
                state_ssd_f[:, l], state_ssd_b[:, l])
        xs, _ = block(xs, cond, layers[l], init, (cache_na_k[:, l], cache_na_v[:, l]))

    return (xp, xs, *outs)
```

```python
import functools
import math

import jax
import jax.numpy as jnp
import numpy as np
from jax import lax
from jax.experimental import pallas as pl
from jax.experimental.pallas import tpu as pltpu

D_MODEL = 2048
BATCH = 16
SEQ = 256
DEPTH = 2
DEC_BATCH = 8
DEC_SEQ = 1024
PAST_LEN = 256
GRID_W = 64
EPS = 1e-6
N_BRANCH = 4
BRANCH_W = 1024
N_MOD = 6
D_RNN = 1024
LRU_BLOCKS = 8
LRU_BS = D_RNN // LRU_BLOCKS
LRU_CONV = 4
LRU_C = 8.0
RET_HEADS = 4
RET_DK = 128
RET_DV = 256
SSD_HEADS = 16
SSD_P = 64
SSD_N = 128
SSD_GROUPS = 2
SSD_CONV = 4
D_SSD = SSD_HEADS * SSD_P
SSD_CONV_CH = D_SSD + 2 * SSD_GROUPS * SSD_N
NA_HEADS = 8
NA_HD = 128
NA_W = NA_HEADS * NA_HD
NA_WR = 8
NA_WC = 16
ROPE_BASE = 10000.0
D_FF = 5632
FFN_CONV = 3
IN_SIZES = (D_RNN, D_RNN,
            RET_HEADS * RET_DK, RET_HEADS * RET_DK, RET_HEADS * RET_DV, RET_HEADS * RET_DV,
            D_SSD, SSD_CONV_CH, 2 * SSD_HEADS,
            NA_W, NA_W, NA_W)
D_IN = sum(IN_SIZES)
IN_OFFSETS = tuple(int(s) for s in np.cumsum(IN_SIZES)[:-1])

V7X_LANES = 128
V7X_SUBLANES = 8
V7X_VMEM_LIMIT_BYTES = 56 * 1024 * 1024

TM = 1024
T_SAMPLE = DEC_BATCH * DEC_SEQ
T_PROMPT = BATCH * SEQ
T_ALL = T_SAMPLE + T_PROMPT
N_SAMPLE_TILES = T_SAMPLE // TM
N_TILES = T_ALL // TM
N_COND = DEC_BATCH + 1
N_COND_PAD = 16

SSD_HPG = SSD_HEADS // SSD_GROUPS
OFF_LRU_X = 0
OFF_LRU_G = OFF_LRU_X + D_RNN
OFF_RET_Q = OFF_LRU_G + D_RNN
OFF_RET_K = OFF_RET_Q + RET_HEADS * RET_DK
OFF_RET_V = OFF_RET_K + RET_HEADS * RET_DK
OFF_RET_G = OFF_RET_V + RET_HEADS * RET_DV
OFF_SSD_Z = OFF_RET_G + RET_HEADS * RET_DV
OFF_NA_Q = OFF_SSD_Z + D_SSD
OFF_NA_K = OFF_NA_Q + NA_W
OFF_NA_V = OFF_NA_K + NA_W
OFF_SSD_XBC = OFF_NA_V + NA_W
OFF_SSD_DT = OFF_SSD_XBC + SSD_CONV_CH
PROJ_TN = 512
D_IN_PAD = -(-(OFF_SSD_DT + SSD_GROUPS * V7X_LANES) // PROJ_TN) * PROJ_TN

CHUNK = 256

f32 = jnp.float32
bf16 = jnp.bfloat16

_ARB = "arbitrary"


def _params(n_axes):
    return pltpu.CompilerParams(dimension_semantics=(_ARB,) * n_axes,
                                vmem_limit_bytes=V7X_VMEM_LIMIT_BYTES)


def _mod_spec(k):
    return pl.BlockSpec((1, 1, D_MODEL), lambda i, j: (jnp.minimum(i, N_SAMPLE_TILES), 0, k))


def _dot(a, b):
    return jnp.dot(a, b, preferred_element_type=f32)


def _dot_nt(a, b):
    return lax.dot_general(a, b, (((1,), (1,)), ((), ())), preferred_element_type=f32)


def _sigmoid(x):
    return 1.0 / (1.0 + jnp.exp(-x))


def _silu(x):
    return x * _sigmoid(x)


def _gelu_tanh(x):
    return 0.5 * x * (1.0 + jnp.tanh(math.sqrt(2.0 / math.pi) * (x + 0.044715 * (x * x * x))))


def _softplus(x):
    return jnp.maximum(x, 0.0) + jnp.log1p(jnp.exp(-jnp.abs(x)))


ADA_TN = 1024


def _ada_kernel(c_ref, w_ref, b_ref, o_ref):
    c = _silu(c_ref[...]).astype(bf16)
    o_ref[0] = _dot(c, w_ref[0].astype(bf16)) + b_ref[0]


def ada_modulation(cond, w_ada, b_ada):
    n = N_MOD * D_MODEL
    return pl.pallas_call(
        _ada_kernel,
        grid=(DEPTH, n // ADA_TN),
        in_specs=[pl.BlockSpec((N_COND_PAD, D_MODEL), lambda l, j: (0, 0)),
                  pl.BlockSpec((1, D_MODEL, ADA_TN), lambda l, j: (l, 0, j)),
                  pl.BlockSpec((1, 1, ADA_TN), lambda l, j: (l, 0, j))],
        out_specs=pl.BlockSpec((1, N_COND_PAD, ADA_TN), lambda l, j: (l, 0, j)),
        out_shape=jax.ShapeDtypeStruct((DEPTH, N_COND_PAD, n), f32),
        compiler_params=_params(2),
        name="ada_modulation",
    )(cond, w_ada, b_ada.reshape(DEPTH, 1, n))


NORM_ROWS = 128


def _modulated_norm(x_ref, g_ref, sc_ref, sh_ref, xn_ref):
    g = g_ref[...]
    sc = 1.0 + sc_ref[0]
    sh = sh_ref[0]

    def body(r, carry):
        rows = pl.ds(pl.multiple_of(r * NORM_ROWS, NORM_ROWS), NORM_ROWS)
        x = x_ref[rows, :]
        y = x * lax.rsqrt(jnp.mean(x * x, axis=-1, keepdims=True) + EPS)
        xn_ref[rows, :] = ((y * g) * sc + sh).astype(bf16)
        return carry

    lax.fori_loop(0, TM // NORM_ROWS, body, 0)


def _in_proj_kernel(x_ref, g_ref, sc_ref, sh_ref, w_ref, o_ref, xn_ref):
    @pl.when(pl.program_id(1) == 0)
    def _():
        _modulated_norm(x_ref, g_ref, sc_ref, sh_ref, xn_ref)

    o_ref[...] = _dot(xn_ref[...], w_ref[...])


def in_projection(x, g, mod, w):
    n = w.shape[1]
    return pl.pallas_call(
        _in_proj_kernel,
        grid=(N_TILES, n // PROJ_TN),
        in_specs=[pl.BlockSpec((TM, D_MODEL), lambda i, j: (i, 0)),
                  pl.BlockSpec((1, D_MODEL), lambda i, j: (0, 0)),
                  _mod_spec(1), _mod_spec(0),
                  pl.BlockSpec((D_MODEL, PROJ_TN), lambda i, j: (0, j))],
        out_specs=[pl.BlockSpec((TM, PROJ_TN), lambda i, j: (i, j)),
                   pl.BlockSpec((TM, D_MODEL), lambda i, j: (i, 0))],
        out_shape=[jax.ShapeDtypeStruct((T_ALL, n), f32),
                   jax.ShapeDtypeStruct((T_ALL, D_MODEL), bf16)],
        compiler_params=_params(2),
        name="in_projection",
    )(x, g, mod, mod, w)


MERGE_TN = 256


def _merge_kernel(xn_ref, y0_ref, y1_ref, y2_ref, y3_ref, wg_ref, bg_ref, wb_ref, o_ref):
    xn = xn_ref[...]
    acc = None
    for n, y_ref in enumerate((y0_ref, y1_ref, y2_ref, y3_ref)):
        gate = _sigmoid(_dot(xn, wg_ref[n]) + bg_ref[n])
        term = gate * _dot(y_ref[...], wb_ref[n])
        acc = term if acc is None else acc + term
    o_ref[...] = acc.astype(bf16)


def merge_branches(xn, ys, w_gate, b_gate, w_branch):
    y_spec = pl.BlockSpec((TM, BRANCH_W), lambda i, j: (i, 0))
    return pl.pallas_call(
        _merge_kernel,
        grid=(N_TILES, D_MODEL // MERGE_TN),
        in_specs=[pl.BlockSpec((TM, D_MODEL), lambda i, j: (i, 0)),
                  y_spec, y_spec, y_spec, y_spec,
                  pl.BlockSpec((N_BRANCH, D_MODEL, MERGE_TN), lambda i, j: (0, 0, j)),
                  pl.BlockSpec((N_BRANCH, 1, MERGE_TN), lambda i, j: (0, 0, j)),
                  pl.BlockSpec((N_BRANCH, BRANCH_W, MERGE_TN), lambda i, j: (0, 0, j))],
        out_specs=pl.BlockSpec((TM, MERGE_TN), lambda i, j: (i, j)),
        out_shape=jax.ShapeDtypeStruct((T_ALL, D_MODEL), bf16),
        compiler_params=_params(2),
        name="merge_branches",
    )(xn, *ys, w_gate, b_gate, w_branch)


RES_TN = 512


def _residual_kernel(a_ref, w_ref, x_ref, gv_ref, o_ref):
    o_ref[...] = x_ref[...] + gv_ref[0] * _dot(a_ref[...], w_ref[...])


def residual_projection(a, w, x, mod, k_mod):
    kdim = a.shape[1]
    return pl.pallas_call(
        _residual_kernel,
        grid=(N_TILES, D_MODEL // RES_TN),
        in_specs=[pl.BlockSpec((TM, kdim), lambda i, j: (i, 0)),
                  pl.BlockSpec((kdim, RES_TN), lambda i, j: (0, j)),
                  pl.BlockSpec((TM, RES_TN), lambda i, j: (i, j)),
                  pl.BlockSpec((1, 1, RES_TN),
                               lambda i, j: (jnp.minimum(i, N_SAMPLE_TILES), 0,
                                             k_mod * (D_MODEL // RES_TN) + j))],
        out_specs=pl.BlockSpec((TM, RES_TN), lambda i, j: (i, j)),
        out_shape=jax.ShapeDtypeStruct((T_ALL, D_MODEL), f32),
        compiler_params=_params(2),
        name="residual_projection",
    )(a, w, x, mod)


FFN_TN = 512


def _ffn_up_kernel(x_ref, g_ref, sc_ref, sh_ref, wa_ref, wv_ref, cw_ref, cb_ref, o_ref, xn_ref, pad_ref):
    i = pl.program_id(0)

    @pl.when(pl.program_id(1) == 0)
    def _():
        _modulated_norm(x_ref, g_ref, sc_ref, sh_ref, xn_ref)
        zeros = jnp.zeros((V7X_SUBLANES, FFN_TN), f32)
        pad_ref[pl.ds(0, V7X_SUBLANES), :] = zeros
        pad_ref[pl.ds(V7X_SUBLANES + TM, V7X_SUBLANES), :] = zeros

    xn = xn_ref[...]
    a = _dot(xn, wa_ref[...])
    pad_ref[pl.ds(V7X_SUBLANES, TM), :] = a
    seq_len = jnp.where(i < N_SAMPLE_TILES, DEC_SEQ, SEQ)
    pos = lax.broadcasted_iota(jnp.int32, (TM, 1), 0) & (seq_len - 1)
    prev = jnp.where(pos == 0, 0.0, pad_ref[pl.ds(V7X_SUBLANES - 1, TM), :])
    nxt = jnp.where(pos == seq_len - 1, 0.0, pad_ref[pl.ds(V7X_SUBLANES + 1, TM), :])
    cw = cw_ref[...]
    conv = cb_ref[...] + prev * cw[0:1, :] + a * cw[1:2, :] + nxt * cw[2:3, :]
    o_ref[...] = (_gelu_tanh(conv) * _dot(xn, wv_ref[...])).astype(bf16)


def ffn_up(x, g, mod, w_up, conv_w, conv_b):
    nj = D_FF // FFN_TN
    return pl.pallas_call(
        _ffn_up_kernel,
        grid=(N_TILES, nj),
        in_specs=[pl.BlockSpec((TM, D_MODEL), lambda i, j: (i, 0)),
                  pl.BlockSpec((1, D_MODEL), lambda i, j: (0, 0)),
                  _mod_spec(4), _mod_spec(3),
                  pl.BlockSpec((D_MODEL, FFN_TN), lambda i, j: (0, j)),
                  pl.BlockSpec((D_MODEL, FFN_TN), lambda i, j: (0, nj + j)),
                  pl.BlockSpec((FFN_CONV, FFN_TN), lambda i, j: (0, j)),
                  pl.BlockSpec((1, FFN_TN), lambda i, j: (0, j))],
        out_specs=pl.BlockSpec((TM, FFN_TN), lambda i, j: (i, j)),
        out_shape=jax.ShapeDtypeStruct((T_ALL, D_FF), bf16),
        scratch_shapes=[pltpu.VMEM((TM, D_MODEL), bf16),
                        pltpu.VMEM((TM + 2 * V7X_SUBLANES, FFN_TN), f32)],
        compiler_params=_params(2),
        name="ffn_up",
    )(x, g, mod, mod, w_up, w_up, conv_w, conv_b)


LRU_CB = 512
LRU_NB = LRU_CB // LRU_BS
SCAN_UNROLL = 8


def _lru_kernel(*refs, seq_len, has_init):
    if has_init:
        (x_ref, g_ref, cw_ref, cb_ref, w4_ref, b4_ref, lam_ref, h0f_ref, h0b_ref,
         y_ref, pad_ref, af_ref, uf_ref, ab_ref, ub_ref, hf_ref, hb_ref) = refs
    else:
        (x_ref, g_ref, cw_ref, cb_ref, w4_ref, b4_ref, lam_ref,
         y_ref, sf_ref, sb_ref, pad_ref, af_ref, uf_ref, ab_ref, ub_ref, hf_ref, hb_ref) = refs
    L = seq_len
    S = V7X_SUBLANES
    zeros = jnp.zeros((S, LRU_CB), f32)
    pad_ref[pl.ds(0, S), :] = zeros
    pad_ref[pl.ds(S + L, S), :] = zeros
    pad_ref[pl.ds(S, L), :] = x_ref[...]
    cw = cw_ref[...]
    xc = cb_ref[...]
    for kk in range(LRU_CONV):
        xc = xc + pad_ref[pl.ds(S + kk - LRU_CONV // 2, L), :] * cw[kk:kk + 1, :]
    sp = _softplus(-lam_ref[...])
    for n in range(LRU_NB):
        cols = slice(n * LRU_BS, (n + 1) * LRU_BS)
        xcn = xc[:, cols]
        z = _dot(xcn.astype(bf16), w4_ref[n]) + b4_ref[n]
        for d, (a_ref, u_ref) in enumerate(((af_ref, uf_ref), (ab_ref, ub_ref))):
            r = _sigmoid(z[:, (2 * d) * LRU_BS:(2 * d + 1) * LRU_BS])
            ig = _sigmoid(z[:, (2 * d + 1) * LRU_BS:(2 * d + 2) * LRU_BS])
            log_a = (-LRU_C) * r * sp[d:d + 1, cols]
            a = jnp.exp(log_a)
            a_ref[:, cols] = a
            u_ref[:, cols] = jnp.sqrt(-jnp.tanh(log_a) * (a * a + 1.0)) * (ig * xcn)

    if has_init:
        h0f = h0f_ref[0]
        h0b = h0b_ref[0]
    else:
        h0f = jnp.zeros((1, LRU_CB), f32)
        h0b = jnp.zeros((1, LRU_CB), f32)

    def step(t, carry):
        hf, hb = carry
        tb = L - 1 - t
        hf = af_ref[pl.ds(t, 1), :] * hf + uf_ref[pl.ds(t, 1), :]
        hb = ab_ref[pl.ds(tb, 1), :] * hb + ub_ref[pl.ds(tb, 1), :]
        hf_ref[pl.ds(t, 1), :] = hf
        hb_ref[pl.ds(tb, 1), :] = hb
        return hf, hb

    hf, hb = lax.fori_loop(0, L, step, (h0f, h0b), unroll=SCAN_UNROLL)
    if not has_init:
        sf_ref[0] = hf
        sb_ref[0] = hb
    y_ref[...] = ((hf_ref[...] + hb_ref[...]) * _gelu_tanh(g_ref[...])).astype(bf16)


def lru_branch(proj, row0, n_seq, seq_len, cw, cb, w4, b4, lam, init):
    has_init = init is not None
    ncb = D_RNN // LRU_CB
    rb0 = row0 // seq_len
    in_specs = [pl.BlockSpec((seq_len, LRU_CB), lambda s, c: (rb0 + s, OFF_LRU_X // LRU_CB + c)),
                pl.BlockSpec((seq_len, LRU_CB), lambda s, c: (rb0 + s, OFF_LRU_G // LRU_CB + c)),
                pl.BlockSpec((LRU_CONV, LRU_CB), lambda s, c: (0, c)),
                pl.BlockSpec((1, LRU_CB), lambda s, c: (0, c)),
                pl.BlockSpec((LRU_NB, LRU_BS, 4 * LRU_BS), lambda s, c: (c, 0, 0)),
                pl.BlockSpec((LRU_NB, 1, 4 * LRU_BS), lambda s, c: (c, 0, 0)),
                pl.BlockSpec((2, LRU_CB), lambda s, c: (0, c))]
    args = [proj, proj, cw, cb, w4, b4, lam]
    state_spec = pl.BlockSpec((1, 1, LRU_CB), lambda s, c: (s, 0, c))
    y_shape = jax.ShapeDtypeStruct((n_seq * seq_len, D_RNN), bf16)
    y_spec = pl.BlockSpec((seq_len, LRU_CB), lambda s, c: (s, c))
    if has_init:
        in_specs += [state_spec, state_spec]
        args += [init[0].reshape(n_seq, 1, D_RNN), init[1].reshape(n_seq, 1, D_RNN)]
        out_specs, out_shape = y_spec, y_shape
    else:
        st = jax.ShapeDtypeStruct((n_seq, 1, D_RNN), f32)
        out_specs, out_shape = [y_spec, state_spec, state_spec], [y_shape, st, st]
    scratch = [pltpu.VMEM((seq_len + 2 * V7X_SUBLANES, LRU_CB), f32)] + \
              [pltpu.VMEM((seq_len, LRU_CB), f32) for _ in range(6)]
    return pl.pallas_call(
        functools.partial(_lru_kernel, seq_len=seq_len, has_init=has_init),
        grid=(n_seq, ncb), in_specs=in_specs, out_specs=out_specs, out_shape=out_shape,
        scratch_shapes=scratch, compiler_params=_params(2), name="lru_branch",
    )(*args)


def _ret_kernel(*refs, seq_len, has_init):
    if has_init:
        (la_ref, q_ref, k_ref, v_ref, g_ref, gn_ref, s0f_ref, s0b_ref, y_ref, acc_ref) = refs
    else:
        (la_ref, q_ref, k_ref, v_ref, g_ref, gn_ref, y_ref, sf_ref, sb_ref, acc_ref) = refs
    h = pl.program_id(1)
    T = min(CHUNK, seq_len)
    nc = seq_len // T
    la_f = la_ref[h, 0]
    la_b = la_ref[h, 1]
    tt = lax.broadcasted_iota(jnp.int32, (T, T), 0)
    ss = lax.broadcasted_iota(jnp.int32, (T, T), 1)
    diff = (tt - ss).astype(f32)
    dsum = (jnp.where(tt >= ss, jnp.exp(la_f * diff), 0.0)
            + jnp.where(ss >= tt, jnp.exp(-la_b * diff), 0.0))
    tcol = lax.broadcasted_iota(jnp.int32, (T, 1), 0).astype(f32)
    scale = RET_DK ** -0.5

    def chunk(c):
        rows = pl.ds(c * T, T)
        q = q_ref[rows, :]
        ks = k_ref[rows, :] * scale
        v = v_ref[rows, :]
        return rows, q, ks, v

    if has_init:
        s_f = s0f_ref[0, 0]
    else:
        s_f = None
    for c in range(nc):
        rows, q, ks, v = chunk(c)
        scores = _dot_nt(q.astype(bf16), ks.astype(bf16)) * dsum
        y = _dot(scores.astype(bf16), v.astype(bf16))
        if s_f is not None:
            y = y + _dot((q * jnp.exp(la_f * (tcol + 1.0))).astype(bf16), s_f.astype(bf16))
        acc_ref[rows, :] = y
        if c < nc - 1 or not has_init:
            upd = _dot(ks.T.astype(bf16), (v * jnp.exp(la_f * (T - 1.0 - tcol))).astype(bf16))
            s_f = upd if s_f is None else jnp.exp(la_f * T) * s_f + upd
    if has_init:
        s_b = s0b_ref[0, 0]
    else:
        s_b = None
    for c in reversed(range(nc)):
        rows, q, ks, v = chunk(c)
        if s_b is not None:
            acc_ref[rows, :] += _dot((q * jnp.exp(la_b * (T - tcol))).astype(bf16), s_b.astype(bf16))
        if c > 0 or not has_init:
            upd = _dot(ks.T.astype(bf16), (v * jnp.exp(la_b * tcol)).astype(bf16))
            s_b = upd if s_b is None else jnp.exp(la_b * T) * s_b + upd
    if not has_init:
        sf_ref[0, 0] = s_f
        sb_ref[0, 0] = s_b
    y = acc_ref[...]
    mu = jnp.mean(y, axis=-1, keepdims=True)
    yc = y - mu
    var = jnp.mean(yc * yc, axis=-1, keepdims=True)
    y = yc * lax.rsqrt(var + EPS) * gn_ref[...]
    y_ref[...] = (y * _silu(g_ref[...])).astype(bf16)


def retention_branch(proj, row0, n_seq, seq_len, la, gn, init):
    has_init = init is not None
    rb0 = row0 // seq_len
    in_specs = [pl.BlockSpec(memory_space=pltpu.SMEM),
                pl.BlockSpec((seq_len, RET_DK), lambda s, h: (rb0 + s, OFF_RET_Q // RET_DK + h)),
                pl.BlockSpec((seq_len, RET_DK), lambda s, h: (rb0 + s, OFF_RET_K // RET_DK + h)),
                pl.BlockSpec((seq_len, RET_DV), lambda s, h: (rb0 + s, OFF_RET_V // RET_DV + h)),
                pl.BlockSpec((seq_len, RET_DV), lambda s, h: (rb0 + s, OFF_RET_G // RET_DV + h)),
                pl.BlockSpec((1, RET_DV), lambda s, h: (0, h))]
    args = [la, proj, proj, proj, proj, gn]
    state_spec = pl.BlockSpec((1, 1, RET_DK, RET_DV), lambda s, h: (s, h, 0, 0))
    y_shape = jax.ShapeDtypeStruct((n_seq * seq_len, RET_HEADS * RET_DV), bf16)
    y_spec = pl.BlockSpec((seq_len, RET_DV), lambda s, h: (s, h))
    if has_init:
        in_specs += [state_spec, state_spec]
        args += [init[0], init[1]]
        out_specs, out_shape = y_spec, y_shape
    else:
        st = jax.ShapeDtypeStruct((n_seq, RET_HEADS, RET_DK, RET_DV), f32)
        out_specs, out_shape = [y_spec, state_spec, state_spec], [y_shape, st, st]
    return pl.pallas_call(
        functools.partial(_ret_kernel, seq_len=seq_len, has_init=has_init),
        grid=(n_seq, RET_HEADS), in_specs=in_specs, out_specs=out_specs, out_shape=out_shape,
        scratch_shapes=[pltpu.VMEM((seq_len, RET_DV), f32)],
        compiler_params=_params(2), name="retention_branch",
    )(*args)


SSD_GW = SSD_HPG * SSD_P


def _split3(x):
    h1 = x.astype(bf16)
    r1 = x - h1.astype(f32)
    h2 = r1.astype(bf16)
    h3 = (r1 - h2.astype(f32)).astype(bf16)
    return h1, h2, h3


def _dot_exact_rhs(m, x):
    h1, h2, h3 = _split3(x)
    return _dot(m, h1) + _dot(m, h2) + _dot(m, h3)


def _dot_exact_lhs(x, m):
    h1, h2, h3 = _split3(x)
    return _dot(h1, m) + _dot(h2, m) + _dot(h3, m)


def _conv4_silu(src_ref, pad_ref, cw_ref, cb_ref, L, width):
    S = V7X_SUBLANES
    zeros = jnp.zeros((S, width), f32)
    pad_ref[pl.ds(0, S), pl.ds(0, width)] = zeros
    pad_ref[pl.ds(S + L, S), pl.ds(0, width)] = zeros
    pad_ref[pl.ds(S, L), pl.ds(0, width)] = src_ref[...]
    cw = cw_ref[...]
    out = cb_ref[...]
    for kk in range(SSD_CONV):
        out = out + pad_ref[pl.ds(S + kk - SSD_CONV // 2, L), pl.ds(0, width)] * cw[kk:kk + 1, :]
    return _silu(out)


def _ssd_kernel(*refs, seq_len, has_init):
    if has_init:
        (z_ref, x_ref, b_ref, c_ref, dt_ref, cwx_ref, cbx_ref, cwb_ref, cbb_ref, cwc_ref, cbc_ref,
         prm_ref, dvec_ref, ng_ref, s0f_ref, s0b_ref,
         y_ref, pad_ref, xs_ref, bs_ref, cs_ref, acc_ref, rb_ref, yn_ref) = refs
    else:
        (z_ref, x_ref, b_ref, c_ref, dt_ref, cwx_ref, cbx_ref, cwb_ref, cbb_ref, cwc_ref, cbc_ref,
         prm_ref, dvec_ref, ng_ref,
         y_ref, sf_ref, sb_ref, pad_ref, xs_ref, bs_ref, cs_ref, acc_ref, rb_ref, yn_ref) = refs
    g = pl.program_id(1)
    L = seq_len
    T = min(CHUNK, L)
    nc = L // T
    H = SSD_HPG
    xs_ref[...] = _conv4_silu(x_ref, pad_ref, cwx_ref, cbx_ref, L, SSD_GW)
    bs_ref[...] = _conv4_silu(b_ref, pad_ref, cwb_ref, cbb_ref, L, SSD_N)
    cs_ref[...] = _conv4_silu(c_ref, pad_ref, cwc_ref, cbc_ref, L, SSD_N)
    prm = prm_ref[0]
    a_neg = -jnp.exp(prm[1:2, :])
    tt = lax.broadcasted_iota(jnp.int32, (T, T), 0)
    ss = lax.broadcasted_iota(jnp.int32, (T, T), 1)
    lower = tt >= ss
    upper = ss >= tt
    tri_l = jnp.where(lower, 1.0, 0.0).astype(bf16)
    tri_u = jnp.where(upper, 1.0, 0.0).astype(bf16)
    er = lax.broadcasted_iota(jnp.int32, (V7X_LANES, SSD_GW), 0)
    ec = lax.broadcasted_iota(jnp.int32, (V7X_LANES, SSD_GW), 1) // SSD_P
    exp_f = jnp.where(er == ec, 1.0, 0.0).astype(bf16)
    exp_b = jnp.where(er == ec + H, 1.0, 0.0).astype(bf16)

    def expand(w, e):
        hi = w.astype(bf16)
        lo = (w - hi.astype(f32)).astype(bf16)
        return _dot(hi, e) + _dot(lo, e)

    def chunk_terms(c):
        rows = pl.ds(c * T, T)
        dt = _softplus(dt_ref[rows, :] + prm[0:1, :])
        da = dt * a_neg
        return rows, dt, da

    s_f = s0f_ref[0, 0] if has_init else None
    for c in range(nc):
        rows, dt, da = chunk_terms(c)
        cum = _dot_exact_rhs(tri_l, da)
        rsum = _dot_exact_rhs(tri_u, da)
        rb_ref[rows, :] = rsum
        da_t = da.T
        cum_t = _dot_exact_lhs(da_t, tri_u)
        rsum_t = _dot_exact_lhs(da_t, tri_l)
        dt_t = dt.T
        bmat = bs_ref[rows, :]
        cmat = cs_ref[rows, :]
        xmat = xs_ref[rows, :]
        gmat = _dot_nt(cmat.astype(bf16), bmat.astype(bf16))
        for hh in range(H):
            lf = jnp.where(lower, jnp.exp(cum[:, hh:hh + 1] - cum_t[hh:hh + 1, :]), 0.0)
            lb = jnp.where(upper, jnp.exp(rsum[:, H + hh:H + hh + 1] - rsum_t[H + hh:H + hh + 1, :]), 0.0)
            m = gmat * (lf * dt_t[hh:hh + 1, :] + lb * dt_t[H + hh:H + hh + 1, :])
            xh = xmat[:, hh * SSD_P:(hh + 1) * SSD_P]
            acc_ref[rows, pl.ds(hh * SSD_P, SSD_P)] = _dot(m.astype(bf16), xh.astype(bf16))
        ecum = jnp.exp(cum)
        if s_f is not None:
            acc_ref[rows, :] += _dot(cmat.astype(bf16), s_f.astype(bf16)) * expand(ecum, exp_f)
        if c < nc - 1 or not has_init:
            tail = jnp.exp(cum[T - 1:T, :] - cum) * dt
            xw = (xmat * expand(tail, exp_f)).astype(bf16)
            upd = _dot(bmat.T.astype(bf16), xw)
            if s_f is None:
                s_f = upd
            else:
                s_f = s_f * expand(jnp.broadcast_to(ecum[T - 1:T, :], (V7X_SUBLANES, V7X_LANES)),
                                   exp_f)[0:1, :] + upd
    s_b = s0b_ref[0, 0] if has_init else None
    for c in reversed(range(nc)):
        rows, dt, da = chunk_terms(c)
        rsum = rb_ref[rows, :]
        ers = jnp.exp(rsum)
        bmat = bs_ref[rows, :]
        cmat = cs_ref[rows, :]
        xmat = xs_ref[rows, :]
        if s_b is not None:
            acc_ref[rows, :] += _dot(cmat.astype(bf16), s_b.astype(bf16)) * expand(ers, exp_b)
        if c > 0 or not has_init:
            tail = jnp.exp(rsum[0:1, :] - rsum) * dt
            xw = (xmat * expand(tail, exp_b)).astype(bf16)
            upd = _dot(bmat.T.astype(bf16), xw)
            if s_b is None:
                s_b = upd
            else:
                s_b = s_b * expand(jnp.broadcast_to(ers[0:1, :], (V7X_SUBLANES, V7X_LANES)),
                                   exp_b)[0:1, :] + upd
    if not has_init:
        for hh in range(H):
            sf_ref[0, hh] = s_f[:, hh * SSD_P:(hh + 1) * SSD_P]
            sb_ref[0, hh] = s_b[:, hh * SSD_P:(hh + 1) * SSD_P]
    yg = (acc_ref[...] + xs_ref[...] * dvec_ref[...]) * _silu(z_ref[...])
    yn_ref[g] = yg

    @pl.when(g == SSD_GROUPS - 1)
    def _():
        ssq = None
        for gg in range(SSD_GROUPS):
            y = yn_ref[gg]
            s = jnp.sum(y * y, axis=-1, keepdims=True)
            ssq = s if ssq is None else ssq + s
        inv = lax.rsqrt(ssq * (1.0 / D_SSD) + EPS)
        for gg in range(SSD_GROUPS):
            cols = pl.ds(gg * SSD_GW, SSD_GW)
            y_ref[:, cols] = (yn_ref[gg] * inv * ng_ref[:, cols]).astype(bf16)


def ssd_branch(proj, row0, n_seq, seq_len, conv_w, conv_b, prm, dvec, ng, init):
    has_init = init is not None
    rb0 = row0 // seq_len
    xoff = OFF_SSD_XBC
    boff = OFF_SSD_XBC + D_SSD
    coff = boff + SSD_GROUPS * SSD_N
    in_specs = [pl.BlockSpec((seq_len, SSD_GW), lambda s, g: (rb0 + s, OFF_SSD_Z // SSD_GW + g)),
                pl.BlockSpec((seq_len, SSD_GW), lambda s, g: (rb0 + s, xoff // SSD_GW + g)),
                pl.BlockSpec((seq_len, SSD_N), lambda s, g: (rb0 + s, boff // SSD_N + g)),
                pl.BlockSpec((seq_len, SSD_N), lambda s, g: (rb0 + s, coff // SSD_N + g)),
                pl.BlockSpec((seq_len, V7X_LANES), lambda s, g: (rb0 + s, OFF_SSD_DT // V7X_LANES + g)),
                pl.BlockSpec((SSD_CONV, SSD_GW), lambda s, g: (0, g)),
                pl.BlockSpec((1, SSD_GW), lambda s, g: (0, g)),
                pl.BlockSpec((SSD_CONV, SSD_N), lambda s, g: (0, D_SSD // SSD_N + g)),
                pl.BlockSpec((1, SSD_N), lambda s, g: (0, D_SSD // SSD_N + g)),
                pl.BlockSpec((SSD_CONV, SSD_N), lambda s, g: (0, D_SSD // SSD_N + SSD_GROUPS + g)),
                pl.BlockSpec((1, SSD_N), lambda s, g: (0, D_SSD // SSD_N + SSD_GROUPS + g)),
                pl.BlockSpec((1, V7X_SUBLANES, V7X_LANES), lambda s, g: (g, 0, 0)),
                pl.BlockSpec((1, SSD_GW), lambda s, g: (0, g)),
                pl.BlockSpec((1, D_SSD), lambda s, g: (0, 0))]
    args = [proj, proj, proj, proj, proj, conv_w, conv_b, conv_w, conv_b, conv_w, conv_b, prm, dvec, ng]
    y_shape = jax.ShapeDtypeStruct((n_seq * seq_len, D_SSD), bf16)
    y_spec = pl.BlockSpec((seq_len, D_SSD), lambda s, g: (s, 0))
    if has_init:
        sin_spec = pl.BlockSpec((1, 1, SSD_N, SSD_GW), lambda s, g: (s, g, 0, 0))
        in_specs += [sin_spec, sin_spec]
        args += [init[0], init[1]]
        out_specs, out_shape = y_spec, y_shape
    else:
        st_spec = pl.BlockSpec((1, SSD_HPG, SSD_N, SSD_P), lambda s, g: (s, g, 0, 0))
        st = jax.ShapeDtypeStruct((n_seq, SSD_HEADS, SSD_N, SSD_P), f32)
        out_specs, out_shape = [y_spec, st_spec, st_spec], [y_shape, st, st]
    scratch = [pltpu.VMEM((seq_len + 2 * V7X_SUBLANES, SSD_GW), f32),
               pltpu.VMEM((seq_len, SSD_GW), f32),
               pltpu.VMEM((seq_len, SSD_N), f32),
               pltpu.VMEM((seq_len, SSD_N), f32),
               pltpu.VMEM((seq_len, SSD_GW), f32),
               pltpu.VMEM((seq_len, V7X_LANES), f32),
               pltpu.VMEM((SSD_GROUPS, seq_len, SSD_GW), f32)]
    return pl.pallas_call(
        functools.partial(_ssd_kernel, seq_len=seq_len, has_init=has_init),
        grid=(n_seq, SSD_GROUPS), in_specs=in_specs, out_specs=out_specs, out_shape=out_shape,
        scratch_shapes=scratch, compiler_params=_params(2), name="ssd_branch",
    )(*args)


def _head_rmsnorm(x, g):
    return x * lax.rsqrt(jnp.mean(x * x, axis=-1, keepdims=True) + EPS) * g


def _ctx_attn_kernel(q_ref, k_ref, v_ref, qg_ref, kg_ref, y_ref, ko_ref):
    scale = NA_HD ** -0.5
    for h in range(NA_HEADS):
        cols = pl.ds(h * NA_HD, NA_HD)
        q = _head_rmsnorm(q_ref[:, cols], qg_ref[...])
        k = _head_rmsnorm(k_ref[:, cols], kg_ref[...])
        ko_ref[:, cols] = k
        s = _dot_nt(q.astype(bf16), k.astype(bf16)) * scale
        p = jnp.exp(s - jnp.max(s, axis=-1, keepdims=True))
        o = _dot(p.astype(bf16), v_ref[:, cols].astype(bf16)) / jnp.sum(p, axis=-1, keepdims=True)
        y_ref[:, cols] = o.astype(bf16)


def context_attention(proj, row0, qg, kg):
    rb0 = row0 // SEQ
    spec = lambda off: pl.BlockSpec((SEQ, NA_W), lambda s: (rb0 + s, off // NA_W))
    gspec = pl.BlockSpec((1, NA_HD), lambda s: (0, 0))
    out_spec = pl.BlockSpec((SEQ, NA_W), lambda s: (s, 0))
    return pl.pallas_call(
        _ctx_attn_kernel,
        grid=(BATCH,),
        in_specs=[spec(OFF_NA_Q), spec(OFF_NA_K), spec(OFF_NA_V), gspec, gspec],
        out_specs=[out_spec, out_spec],
        out_shape=[jax.ShapeDtypeStruct((T_PROMPT, NA_W), bf16),
                   jax.ShapeDtypeStruct((T_PROMPT, NA_W), f32)],
        compiler_params=_params(1), name="context_attention",
    )(proj, proj, proj, qg, kg)


NA_ROWS = DEC_SEQ // GRID_W
NA_NK = NA_WR * GRID_W


def _na_row_start(r):
    return min(max(r - NA_WR // 2, 0), NA_ROWS - NA_WR)


def _rope(x, cos, sin_signed):
    lane = lax.broadcasted_iota(jnp.int32, x.shape, 1)
    quarter = NA_HD // 4
    swapped = jnp.where((lane & (2 * quarter - 1)) < quarter,
                        pltpu.roll(x, NA_HD - quarter, axis=1), pltpu.roll(x, quarter, axis=1))
    return x * cos + swapped * sin_signed


def _na_attn_kernel(q_ref, k_ref, v_ref, kc_ref, vc_ref, qg_ref, kg_ref, cos_ref, sin_ref,
                    bias_ref, valid_ref, y_ref, qs_ref, ks_ref):
    scale = NA_HD ** -0.5
    cos = cos_ref[...]
    sin = sin_ref[...]
    qs_ref[...] = _rope(_head_rmsnorm(q_ref[...], qg_ref[...]), cos, sin).astype(bf16)
    ks_ref[...] = _rope(_head_rmsnorm(k_ref[...], kg_ref[...]), cos, sin).astype(bf16)
    kc = kc_ref[0, 0].astype(bf16)
    vc = vc_ref[0, 0].astype(bf16)
    valid = valid_ref[...] > 0.0
    for r in range(NA_ROWS):
        rs = _na_row_start(r)
        q = qs_ref[pl.ds(r * GRID_W, GRID_W), :]
        kw = ks_ref[pl.ds(rs * GRID_W, NA_NK), :]
        vw = v_ref[pl.ds(rs * GRID_W, NA_NK), :].astype(bf16)
        s_loc = _dot_nt(q, kw) * scale + bias_ref[0, r - rs]
        s_loc = jnp.where(valid, s_loc, -1e30)
        s_ctx = _dot_nt(q, kc) * scale
        m = jnp.maximum(jnp.max(s_loc, axis=-1, keepdims=True), jnp.max(s_ctx, axis=-1, keepdims=True))
        p_loc = jnp.exp(s_loc - m)
        p_ctx = jnp.exp(s_ctx - m)
        denom = jnp.sum(p_loc, axis=-1, keepdims=True) + jnp.sum(p_ctx, axis=-1, keepdims=True)
        o = (_dot(p_loc.astype(bf16), vw) + _dot(p_ctx.astype(bf16), vc)) / denom
        y_ref[pl.ds(r * GRID_W, GRID_W), :] = o.astype(bf16)


def neighbourhood_attention(proj, cache_k, cache_v, layer, qg, kg, cos, sin, bias, valid):
    spec = lambda off: pl.BlockSpec((DEC_SEQ, NA_HD), lambda b, h: (b, off // NA_HD + h))
    cspec = pl.BlockSpec((1, 1, PAST_LEN, NA_HD), lambda b, h: (b, layer, 0, h))
    gspec = pl.BlockSpec((1, NA_HD), lambda b, h: (0, 0))
    tspec = pl.BlockSpec((DEC_SEQ, NA_HD), lambda b, h: (0, 0))
    return pl.pallas_call(
        _na_attn_kernel,
        grid=(DEC_BATCH, NA_HEADS),
        in_specs=[spec(OFF_NA_Q), spec(OFF_NA_K), spec(OFF_NA_V), cspec, cspec, gspec, gspec,
                  tspec, tspec,
                  pl.BlockSpec((1, NA_WR, GRID_W, NA_NK), lambda b, h: (h, 0, 0, 0)),
                  pl.BlockSpec((GRID_W, NA_NK), lambda b, h: (0, 0))],
        out_specs=pl.BlockSpec((DEC_SEQ, NA_HD), lambda b, h: (b, h)),
        out_shape=jax.ShapeDtypeStruct((T_SAMPLE, NA_W), bf16),
        scratch_shapes=[pltpu.VMEM((DEC_SEQ, NA_HD), bf16), pltpu.VMEM((DEC_SEQ, NA_HD), bf16)],
        compiler_params=_params(2), name="neighbourhood_attention",
    )(proj, proj, proj, cache_k, cache_v, qg, kg, cos, sin, bias, valid)


def _reorder_w_in(w_in):
    xbc0, dt0, na0 = IN_OFFSETS[6], IN_OFFSETS[7], IN_OFFSETS[8]
    pieces = [w_in[:, :xbc0], w_in[:, na0:], w_in[:, xbc0:dt0]]
    dt = w_in[:, dt0:na0]
    for gidx in range(SSD_GROUPS):
        pieces.append(dt[:, gidx * SSD_HPG:(gidx + 1) * SSD_HPG])
        pieces.append(dt[:, SSD_HEADS + gidx * SSD_HPG:SSD_HEADS + (gidx + 1) * SSD_HPG])
        pieces.append(jnp.zeros((D_MODEL, V7X_LANES - 2 * SSD_HPG), w_in.dtype))
    used = OFF_SSD_DT + SSD_GROUPS * V7X_LANES
    pieces.append(jnp.zeros((D_MODEL, D_IN_PAD - used), w_in.dtype))
    return jnp.concatenate(pieces, axis=1).astype(bf16)


def _group_lanes(v):
    rows = []
    for gidx in range(SSD_GROUPS):
        sl = slice(gidx * SSD_HPG, (gidx + 1) * SSD_HPG)
        rows.append(jnp.concatenate([v[0, sl], v[1, sl], jnp.zeros((V7X_LANES - 2 * SSD_HPG,), f32)]))
    return jnp.stack(rows)


def _rope_tables():
    t = np.arange(DEC_SEQ)
    quarter = NA_HD // 4
    inv = ROPE_BASE ** (-np.arange(quarter, dtype=np.float32) / quarter)
    ang_r = (t // GRID_W).astype(np.float32)[:, None] * inv
    ang_c = (t % GRID_W).astype(np.float32)[:, None] * inv
    cos = np.concatenate([np.cos(ang_r), np.cos(ang_r), np.cos(ang_c), np.cos(ang_c)], axis=1)
    sin = np.concatenate([-np.sin(ang_r), np.sin(ang_r), -np.sin(ang_c), np.sin(ang_c)], axis=1)
    return jnp.asarray(cos, f32), jnp.asarray(sin, f32)


def _na_tables(rpb):
    cq = np.arange(GRID_W)
    kc = np.tile(np.arange(GRID_W), NA_WR)
    col_start = np.clip(cq - NA_WC // 2, 0, GRID_W - NA_WC)
    valid = (kc[None, :] >= col_start[:, None]) & (kc[None, :] < col_start[:, None] + NA_WC)
    col_off = np.clip(kc[None, :] - cq[:, None], 1 - NA_WC, NA_WC - 1) + NA_WC - 1
    krow = np.repeat(np.arange(NA_WR), GRID_W)
    tables = []
    for d in range(NA_WR):
        row_off = krow - d + NA_WR - 1
        tables.append(rpb[:, row_off[None, :], col_off])
    return jnp.stack(tables, axis=1).astype(f32), jnp.asarray(valid, f32)


def kernel(x_prompt, x_sample, cache_na_k, cache_na_v, state_lru_f, state_lru_b,
           state_ret_f, state_ret_b, state_ssd_f, state_ssd_b, c, c_ctx,
           norm1_g, norm2_g, w_ada, b_ada, w_in, w_gate, b_gate, w_branch, w_out,
           lru_conv_w, lru_conv_b, lru_wa, lru_ba, lru_wx, lru_bx, lru_lambda,
           ret_gn_g, ssd_conv_w, ssd_conv_b, ssd_a_log, ssd_dt_bias, ssd_d, ssd_norm_g,
           na_q_g, na_k_g, na_rpb, ffn_w_up, ffn_conv_w, ffn_conv_b, ffn_w_down):
    x = jnp.concatenate([x_sample.reshape(T_SAMPLE, D_MODEL), x_prompt.reshape(T_PROMPT, D_MODEL)], axis=0)
    cond = jnp.concatenate([c, c_ctx[None, :], jnp.zeros((N_COND_PAD - N_COND, D_MODEL), f32)], axis=0)
    mod_all = ada_modulation(cond, w_ada, b_ada)
    cos, sin = _rope_tables()
    hh = jnp.arange(RET_HEADS, dtype=f32)
    ret_la = jnp.stack([jnp.log1p(-jnp.exp2(-5.0 - hh)), jnp.log1p(-jnp.exp2(-5.5 - hh))], axis=1)
    cache_k = cache_na_k.reshape(DEC_BATCH, DEPTH, PAST_LEN, NA_W)
    cache_v = cache_na_v.reshape(DEC_BATCH, DEPTH, PAST_LEN, NA_W)

    new = {k: [] for k in ("k", "v", "lru_f", "lru_b", "ret_f", "ret_b", "ssd_f", "ssd_b")}
    for l in range(DEPTH):
        mod = mod_all[l].reshape(N_COND_PAD, 1, N_MOD * D_MODEL)
        proj, xn = in_projection(x, norm1_g[l][None, :], mod, _reorder_w_in(w_in[l]))

        w4 = jnp.concatenate([lru_wa[l, 0], lru_wx[l, 0], lru_wa[l, 1], lru_wx[l, 1]], axis=-1).astype(bf16)
        b4 = jnp.concatenate([lru_ba[l, 0].reshape(LRU_BLOCKS, 1, LRU_BS), lru_bx[l, 0].reshape(LRU_BLOCKS, 1, LRU_BS),
                              lru_ba[l, 1].reshape(LRU_BLOCKS, 1, LRU_BS), lru_bx[l, 1].reshape(LRU_BLOCKS, 1, LRU_BS)],
                             axis=-1)
        lru_args = (lru_conv_w[l], lru_conv_b[l][None, :], w4, b4, lru_lambda[l])
        y_lru_s = lru_branch(proj, 0, DEC_BATCH, DEC_SEQ, *lru_args, (state_lru_f[:, l], state_lru_b[:, l]))
        y_lru_p, lf, lb = lru_branch(proj, T_SAMPLE, BATCH, SEQ, *lru_args, None)
        new["lru_f"].append(lf.reshape(BATCH, D_RNN))
        new["lru_b"].append(lb.reshape(BATCH, D_RNN))

        gn = ret_gn_g[l][None, :]
        y_ret_s = retention_branch(proj, 0, DEC_BATCH, DEC_SEQ, ret_la, gn,
                                   (state_ret_f[:, l], state_ret_b[:, l]))
        y_ret_p, rf, rb = retention_branch(proj, T_SAMPLE, BATCH, SEQ, ret_la, gn, None)
        new["ret_f"].append(rf)
        new["ret_b"].append(rb)

        prm = jnp.stack([_group_lanes(ssd_dt_bias[l]), _group_lanes(ssd_a_log[l])], axis=1)
        prm = jnp.concatenate([prm, jnp.zeros((SSD_GROUPS, V7X_SUBLANES - 2, V7X_LANES), f32)], axis=1)
        dvec = jnp.repeat(ssd_d[l], SSD_P)[None, :]
        ssd_args = (ssd_conv_w[l], ssd_conv_b[l][None, :], prm, dvec, ssd_norm_g[l][None, :])

        def to_group_layout(s):
            s = s.reshape(DEC_BATCH, SSD_GROUPS, SSD_HPG, SSD_N, SSD_P)
            return s.transpose(0, 1, 3, 2, 4).reshape(DEC_BATCH, SSD_GROUPS, SSD_N, SSD_GW)

        y_ssd_s = ssd_branch(proj, 0, DEC_BATCH, DEC_SEQ, *ssd_args,
                             (to_group_layout(state_ssd_f[:, l]), to_group_layout(state_ssd_b[:, l])))
        y_ssd_p, sf, sb = ssd_branch(proj, T_SAMPLE, BATCH, SEQ, *ssd_args, None)
        new["ssd_f"].append(sf)
        new["ssd_b"].append(sb)

        qg = na_q_g[l][None, :]
        kg = na_k_g[l][None, :]
        y_na_p, k_new = context_attention(proj, T_SAMPLE, qg, kg)
        new["k"].append(k_new.reshape(BATCH, SEQ, NA_HEADS, NA_HD))
        new["v"].append(proj[T_SAMPLE:, OFF_NA_V:OFF_NA_V + NA_W].reshape(BATCH, SEQ, NA_HEADS, NA_HD))
        bias, valid = _na_tables(na_rpb[l])
        y_na_s = neighbourhood_attention(proj, cache_k, cache_v, l, qg, kg, cos, sin, bias, valid)

        ys = [jnp.concatenate([a, b], axis=0) for a, b in
              ((y_lru_s, y_lru_p), (y_ret_s, y_ret_p), (y_ssd_s, y_ssd_p), (y_na_s, y_na_p))]
        merged = merge_branches(xn, ys,
                                w_gate[l].reshape(D_MODEL, N_BRANCH, D_MODEL).transpose(1, 0, 2).astype(bf16),
                                b_gate[l].reshape(N_BRANCH, 1, D_MODEL),
                                w_branch[l].astype(bf16))
        x = residual_projection(merged, w_out[l].astype(bf16), x, mod, 2)
        hmid = ffn_up(x, norm2_g[l][None, :], mod, ffn_w_up[l].astype(bf16), ffn_conv_w[l], ffn_conv_b[l][None, :])
        x = residual_projection(hmid, ffn_w_down[l].astype(bf16), x, mod, 5)

    y_sample = x[:T_SAMPLE].reshape(DEC_BATCH, DEC_SEQ, D_MODEL)
    y_prompt = x[T_SAMPLE:].reshape(BATCH, SEQ, D_MODEL)
    st = lambda k: jnp.stack(new[k], axis=1)
    return (y_prompt, y_sample, st("k"), st("v"), st("lru_f"), st("lru_b"),
            st("ret_f"), st("ret_b"), st("ssd_f"), st("ssd_b"))
```

```python
import functools
import math

import jax
import jax.numpy as jnp
import numpy as np
from jax import lax
from jax.experimental import pallas as pl
from jax.experimental.pallas import tpu as pltpu

D_MODEL = 2048
BATCH = 16
SEQ = 256
DEPTH = 2
DEC_BATCH = 8
DEC_SEQ = 1024
PAST_LEN = 256
GRID_W = 64
EPS = 1e-6
N_BRANCH = 4
BRANCH_W = 1024
N_MOD = 6
D_RNN = 1024
LRU_BLOCKS = 8
LRU_BS = D_RNN // LRU_BLOCKS
LRU_CONV = 4
LRU_C = 8.0
RET_HEADS = 4
RET_DK = 128
RET_DV = 256
SSD_HEADS = 16
SSD_P = 64
SSD_N = 128
SSD_GROUPS = 2
SSD_CONV = 4
D_SSD = SSD_HEADS * SSD_P
SSD_CONV_CH = D_SSD + 2 * SSD_GROUPS * SSD_N
NA_HEADS = 8
NA_HD = 128
NA_W = NA_HEADS * NA_HD
NA_WR = 8
NA_WC = 16
ROPE_BASE = 10000.0
D_FF = 5632
FFN_CONV = 3
IN_SIZES = (D_RNN, D_RNN,
            RET_HEADS * RET_DK, RET_HEADS * RET_DK, RET_HEADS * RET_DV, RET_HEADS * RET_DV,
            D_SSD, SSD_CONV_CH, 2 * SSD_HEADS,
            NA_W, NA_W, NA_W)
D_IN = sum(IN_SIZES)
IN_OFFSETS = tuple(int(s) for s in np.cumsum(IN_SIZES)[:-1])

V7X_LANES = 128
V7X_SUBLANES = 8
V7X_VMEM_LIMIT_BYTES = 56 * 1024 * 1024

TM = 1024
T_SAMPLE = DEC_BATCH * DEC_SEQ
T_PROMPT = BATCH * SEQ
T_ALL = T_SAMPLE + T_PROMPT
N_SAMPLE_TILES = T_SAMPLE // TM
N_TILES = T_ALL // TM
N_COND = DEC_BATCH + 1
N_COND_PAD = 16

SSD_HPG = SSD_HEADS // SSD_GROUPS
OFF_LRU_X = 0
OFF_LRU_G = OFF_LRU_X + D_RNN
OFF_RET_Q = OFF_LRU_G + D_RNN
OFF_RET_K = OFF_RET_Q + RET_HEADS * RET_DK
OFF_RET_V = OFF_RET_K + RET_HEADS * RET_DK
OFF_RET_G = OFF_RET_V + RET_HEADS * RET_DV
OFF_SSD_Z = OFF_RET_G + RET_HEADS * RET_DV
OFF_NA_Q = OFF_SSD_Z + D_SSD
OFF_NA_K = OFF_NA_Q + NA_W
OFF_NA_V = OFF_NA_K + NA_W
OFF_SSD_XBC = OFF_NA_V + NA_W
OFF_SSD_DT = OFF_SSD_XBC + SSD_CONV_CH
PROJ_TN = 512
D_IN_PAD = -(-(OFF_SSD_DT + SSD_GROUPS * V7X_LANES) // PROJ_TN) * PROJ_TN

CHUNK = 256

f32 = jnp.float32
bf16 = jnp.bfloat16

_ARB = "arbitrary"


def _params(n_axes):
    return pltpu.CompilerParams(dimension_semantics=(_ARB,) * n_axes,
                                vmem_limit_bytes=V7X_VMEM_LIMIT_BYTES)


def _mod_spec(k):
    return pl.BlockSpec((1, 1, D_MODEL), lambda i, j: (jnp.minimum(i, N_SAMPLE_TILES), 0, k))


def _dot(a, b):
    return jnp.dot(a, b, preferred_element_type=f32)


def _dot_nt(a, b):
    return lax.dot_general(a, b, (((1,), (1,)), ((), ())), preferred_element_type=f32)


def _sigmoid(x):
    return 1.0 / (1.0 + jnp.exp(-x))


def _silu(x):
    return x * _sigmoid(x)


def _gelu_tanh(x):
    return 0.5 * x * (1.0 + jnp.tanh(math.sqrt(2.0 / math.pi) * (x + 0.044715 * (x * x * x))))


def _softplus(x):
    return jnp.maximum(x, 0.0) + jnp.log1p(jnp.exp(-jnp.abs(x)))


ADA_TN = 1024


def _ada_kernel(c_ref, w_ref, b_ref, o_ref):
    c = _silu(c_ref[...]).astype(bf16)
    o_ref[0] = _dot(c, w_ref[0].astype(bf16)) + b_ref[0]


def ada_modulation(cond, w_ada, b_ada):
    n = N_MOD * D_MODEL
    return pl.pallas_call(
        _ada_kernel,
        grid=(DEPTH, n // ADA_TN),
        in_specs=[pl.BlockSpec((N_COND_PAD, D_MODEL), lambda l, j: (0, 0)),
                  pl.BlockSpec((1, D_MODEL, ADA_TN), lambda l, j: (l, 0, j)),
                  pl.BlockSpec((1, 1, ADA_TN), lambda l, j: (l, 0, j))],
        out_specs=pl.BlockSpec((1, N_COND_PAD, ADA_TN), lambda l, j: (l, 0, j)),
        out_shape=jax.ShapeDtypeStruct((DEPTH, N_COND_PAD, n), f32),
        compiler_params=_params(2),
        name="ada_modulation",
    )(cond, w_ada, b_ada.reshape(DEPTH, 1, n))


NORM_ROWS = 128


def _modulated_norm(x_ref, g_ref, sc_ref, sh_ref, xn_ref):
    g = g_ref[...]
    sc = 1.0 + sc_ref[0]
    sh = sh_ref[0]

    def body(r, carry):
        rows = pl.ds(pl.multiple_of(r * NORM_ROWS, NORM_ROWS), NORM_ROWS)
        x = x_ref[rows, :]
        y = x * lax.rsqrt(jnp.mean(x * x, axis=-1, keepdims=True) + EPS)
        xn_ref[rows, :] = ((y * g) * sc + sh).astype(bf16)
        return carry

    lax.fori_loop(0, TM // NORM_ROWS, body, 0)


def _in_proj_kernel(x_ref, g_ref, sc_ref, sh_ref, w_ref, o_ref, xn_ref):
    @pl.when(pl.program_id(1) == 0)
    def _():
        _modulated_norm(x_ref, g_ref, sc_ref, sh_ref, xn_ref)

    o_ref[...] = _dot(xn_ref[...], w_ref[...])


def in_projection(x, g, mod, w):
    n = w.shape[1]
    return pl.pallas_call(
        _in_proj_kernel,
        grid=(N_TILES, n // PROJ_TN),
        in_specs=[pl.BlockSpec((TM, D_MODEL), lambda i, j: (i, 0)),
                  pl.BlockSpec((1, D_MODEL), lambda i, j: (0, 0)),
                  _mod_spec(1), _mod_spec(0),
                  pl.BlockSpec((D_MODEL, PROJ_TN), lambda i, j: (0, j))],
        out_specs=[pl.BlockSpec((TM, PROJ_TN), lambda i, j: (i, j)),
                   pl.BlockSpec((TM, D_MODEL), lambda i, j: (i, 0))],
        out_shape=[jax.ShapeDtypeStruct((T_ALL, n), f32),
                   jax.ShapeDtypeStruct((T_ALL, D_MODEL), bf16)],
        compiler_params=_params(2),
        name="in_projection",
    )(x, g, mod, mod, w)


MERGE_TN = 256


def _merge_kernel(xn_ref, *refs):
    y_refs, wg_refs, bg_refs = refs[0:4], refs[4:8], refs[8:12]
    wb_ref, o_ref = refs[12], refs[13]
    xn = xn_ref[...]
    acc = None
    for n in range(N_BRANCH):
        gate = _sigmoid(_dot(xn, wg_refs[n][...]) + bg_refs[n][...])
        term = gate * _dot(y_refs[n][...], wb_ref[n])
        acc = term if acc is None else acc + term
    o_ref[...] = acc.astype(bf16)


def merge_branches(xn, ys, w_gate, b_gate, w_branch):
    nj = D_MODEL // MERGE_TN
    y_spec = pl.BlockSpec((TM, BRANCH_W), lambda i, j: (i, 0))
    wg_specs = [pl.BlockSpec((D_MODEL, MERGE_TN), lambda i, j, n=n: (0, n * nj + j)) for n in range(N_BRANCH)]
    bg_specs = [pl.BlockSpec((1, MERGE_TN), lambda i, j, n=n: (0, n * nj + j)) for n in range(N_BRANCH)]
    return pl.pallas_call(
        _merge_kernel,
        grid=(N_TILES, nj),
        in_specs=[pl.BlockSpec((TM, D_MODEL), lambda i, j: (i, 0)),
                  y_spec, y_spec, y_spec, y_spec, *wg_specs, *bg_specs,
                  pl.BlockSpec((N_BRANCH, BRANCH_W, MERGE_TN), lambda i, j: (0, 0, j))],
        out_specs=pl.BlockSpec((TM, MERGE_TN), lambda i, j: (i, j)),
        out_shape=jax.ShapeDtypeStruct((T_ALL, D_MODEL), bf16),
        compiler_params=_params(2),
        name="merge_branches",
    )(xn, *ys, *([w_gate] * N_BRANCH), *([b_gate] * N_BRANCH), w_branch)


RES_TN = 512


def _residual_kernel(a_ref, w_ref, x_ref, gv_ref, o_ref):
    o_ref[...] = x_ref[...] + gv_ref[0] * _dot(a_ref[...], w_ref[...])


def residual_projection(a, w, x, mod, k_mod):
    kdim = a.shape[1]
    return pl.pallas_call(
        _residual_kernel,
        grid=(N_TILES, D_MODEL // RES_TN),
        in_specs=[pl.BlockSpec((TM, kdim), lambda i, j: (i, 0)),
                  pl.BlockSpec((kdim, RES_TN), lambda i, j: (0, j)),
                  pl.BlockSpec((TM, RES_TN), lambda i, j: (i, j)),
                  pl.BlockSpec((1, 1, RES_TN),
                               lambda i, j: (jnp.minimum(i, N_SAMPLE_TILES), 0,
                                             k_mod * (D_MODEL // RES_TN) + j))],
        out_specs=pl.BlockSpec((TM, RES_TN), lambda i, j: (i, j)),
        out_shape=jax.ShapeDtypeStruct((T_ALL, D_MODEL), f32),
        compiler_params=_params(2),
        name="residual_projection",
    )(a, w, x, mod)


FFN_TN = 512


def _ffn_up_kernel(x_ref, g_ref, sc_ref, sh_ref, wa_ref, wv_ref, cw_ref, cb_ref, o_ref, xn_ref, pad_ref):
    i = pl.program_id(0)

    @pl.when(pl.program_id(1) == 0)
    def _():
        _modulated_norm(x_ref, g_ref, sc_ref, sh_ref, xn_ref)
        zeros = jnp.zeros((V7X_SUBLANES, FFN_TN), f32)
        pad_ref[pl.ds(0, V7X_SUBLANES), :] = zeros
        pad_ref[pl.ds(V7X_SUBLANES + TM, V7X_SUBLANES), :] = zeros

    xn = xn_ref[...]
    a = _dot(xn, wa_ref[...])
    pad_ref[pl.ds(V7X_SUBLANES, TM), :] = a
    seq_len = jnp.where(i < N_SAMPLE_TILES, DEC_SEQ, SEQ)
    pos = lax.broadcasted_iota(jnp.int32, (TM, 1), 0) & (seq_len - 1)
    prev = jnp.where(pos == 0, 0.0, pad_ref[pl.ds(V7X_SUBLANES - 1, TM), :])
    nxt = jnp.where(pos == seq_len - 1, 0.0, pad_ref[pl.ds(V7X_SUBLANES + 1, TM), :])
    cw = cw_ref[...]
    conv = cb_ref[...] + prev * cw[0:1, :] + a * cw[1:2, :] + nxt * cw[2:3, :]
    o_ref[...] = (_gelu_tanh(conv) * _dot(xn, wv_ref[...])).astype(bf16)


def ffn_up(x, g, mod, w_up, conv_w, conv_b):
    nj = D_FF // FFN_TN
    return pl.pallas_call(
        _ffn_up_kernel,
        grid=(N_TILES, nj),
        in_specs=[pl.BlockSpec((TM, D_MODEL), lambda i, j: (i, 0)),
                  pl.BlockSpec((1, D_MODEL), lambda i, j: (0, 0)),
                  _mod_spec(4), _mod_spec(3),
                  pl.BlockSpec((D_MODEL, FFN_TN), lambda i, j: (0, j)),
                  pl.BlockSpec((D_MODEL, FFN_TN), lambda i, j: (0, nj + j)),
                  pl.BlockSpec((FFN_CONV, FFN_TN), lambda i, j: (0, j)),
                  pl.BlockSpec((1, FFN_TN), lambda i, j: (0, j))],
        out_specs=pl.BlockSpec((TM, FFN_TN), lambda i, j: (i, j)),
        out_shape=jax.ShapeDtypeStruct((T_ALL, D_FF), bf16),
        scratch_shapes=[pltpu.VMEM((TM, D_MODEL), bf16),
                        pltpu.VMEM((TM + 2 * V7X_SUBLANES, FFN_TN), f32)],
        compiler_params=_params(2),
        name="ffn_up",
    )(x, g, mod, mod, w_up, w_up, conv_w, conv_b)


def _drop_first_ref(body, _buf_ref, *refs):
    body(*refs)


def _branch_call(body, y_buf, grid, in_specs, args, out_specs, out_shape, scratch, name):
    aliases = {}
    if y_buf is not None:
        body = functools.partial(_drop_first_ref, body)
        in_specs = [pl.BlockSpec(memory_space=pl.ANY)] + list(in_specs)
        args = [y_buf] + list(args)
        aliases = {0: 0}
    return pl.pallas_call(
        body, grid=grid, in_specs=in_specs, out_specs=out_specs, out_shape=out_shape,
        scratch_shapes=scratch, input_output_aliases=aliases,
        compiler_params=_params(len(grid)), name=name,
    )(*args)


_BRANCH_BUF = jax.ShapeDtypeStruct((T_ALL, BRANCH_W), bf16)


LRU_CB = 512
LRU_NB = LRU_CB // LRU_BS
SCAN_UNROLL = 8


def _lru_kernel(*refs, seq_len, has_init):
    if has_init:
        (x_ref, g_ref, cw_ref, cb_ref, w4_ref, b4_ref, lam_ref, h0f_ref, h0b_ref,
         y_ref, pad_ref, af_ref, uf_ref, ab_ref, ub_ref, hf_ref, hb_ref) = refs
    else:
        (x_ref, g_ref, cw_ref, cb_ref, w4_ref, b4_ref, lam_ref,
         y_ref, sf_ref, sb_ref, pad_ref, af_ref, uf_ref, ab_ref, ub_ref, hf_ref, hb_ref) = refs
    L = seq_len
    S = V7X_SUBLANES
    zeros = jnp.zeros((S, LRU_CB), f32)
    pad_ref[pl.ds(0, S), :] = zeros
    pad_ref[pl.ds(S + L, S), :] = zeros
    pad_ref[pl.ds(S, L), :] = x_ref[...]
    cw = cw_ref[...]
    xc = cb_ref[...]
    for kk in range(LRU_CONV):
        xc = xc + pad_ref[pl.ds(S + kk - LRU_CONV // 2, L), :] * cw[kk:kk + 1, :]
    sp = _softplus(-lam_ref[...])
    for n in range(LRU_NB):
        cols = slice(n * LRU_BS, (n + 1) * LRU_BS)
        xcn = xc[:, cols]
        z = _dot(xcn.astype(bf16), w4_ref[n]) + b4_ref[n]
        for d, (a_ref, u_ref) in enumerate(((af_ref, uf_ref), (ab_ref, ub_ref))):
            r = _sigmoid(z[:, (2 * d) * LRU_BS:(2 * d + 1) * LRU_BS])
            ig = _sigmoid(z[:, (2 * d + 1) * LRU_BS:(2 * d + 2) * LRU_BS])
            log_a = (-LRU_C) * r * sp[d:d + 1, cols]
            a = jnp.exp(log_a)
            a_ref[:, cols] = a
            u_ref[:, cols] = jnp.sqrt(-jnp.tanh(log_a) * (a * a + 1.0)) * (ig * xcn)

    if has_init:
        h0f = h0f_ref[0]
        h0b = h0b_ref[0]
    else:
        h0f = jnp.zeros((1, LRU_CB), f32)
        h0b = jnp.zeros((1, LRU_CB), f32)

    def step(t, carry):
        hf, hb = carry
        tb = L - 1 - t
        hf = af_ref[pl.ds(t, 1), :] * hf + uf_ref[pl.ds(t, 1), :]
        hb = ab_ref[pl.ds(tb, 1), :] * hb + ub_ref[pl.ds(tb, 1), :]
        hf_ref[pl.ds(t, 1), :] = hf
        hb_ref[pl.ds(tb, 1), :] = hb
        return hf, hb

    hf, hb = lax.fori_loop(0, L, step, (h0f, h0b), unroll=SCAN_UNROLL)
    if not has_init:
        sf_ref[0] = hf
        sb_ref[0] = hb
    y_ref[...] = ((hf_ref[...] + hb_ref[...]) * _gelu_tanh(g_ref[...])).astype(bf16)


def lru_branch(proj, y_buf, row0, n_seq, seq_len, cw, cb, w4, b4, lam, init):
    has_init = init is not None
    ncb = D_RNN // LRU_CB
    rb0 = row0 // seq_len
    in_specs = [pl.BlockSpec((seq_len, LRU_CB), lambda s, c: (rb0 + s, OFF_LRU_X // LRU_CB + c)),
                pl.BlockSpec((seq_len, LRU_CB), lambda s, c: (rb0 + s, OFF_LRU_G // LRU_CB + c)),
                pl.BlockSpec((LRU_CONV, LRU_CB), lambda s, c: (0, c)),
                pl.BlockSpec((1, LRU_CB), lambda s, c: (0, c)),
                pl.BlockSpec((LRU_NB, LRU_BS, 4 * LRU_BS), lambda s, c: (c, 0, 0)),
                pl.BlockSpec((LRU_NB, 1, 4 * LRU_BS), lambda s, c: (c, 0, 0)),
                pl.BlockSpec((2, LRU_CB), lambda s, c: (0, c))]
    args = [proj, proj, cw, cb, w4, b4, lam]
    state_spec = pl.BlockSpec((1, 1, LRU_CB), lambda s, c: (s, 0, c))
    y_shape = _BRANCH_BUF
    y_spec = pl.BlockSpec((seq_len, LRU_CB), lambda s, c: (rb0 + s, c))
    if has_init:
        in_specs += [state_spec, state_spec]
        args += [init[0].reshape(n_seq, 1, D_RNN), init[1].reshape(n_seq, 1, D_RNN)]
        out_specs, out_shape = y_spec, y_shape
    else:
        st = jax.ShapeDtypeStruct((n_seq, 1, D_RNN), f32)
        out_specs, out_shape = [y_spec, state_spec, state_spec], [y_shape, st, st]
    scratch = [pltpu.VMEM((seq_len + 2 * V7X_SUBLANES, LRU_CB), f32)] + \
              [pltpu.VMEM((seq_len, LRU_CB), f32) for _ in range(6)]
    return _branch_call(functools.partial(_lru_kernel, seq_len=seq_len, has_init=has_init), y_buf,
                        (n_seq, ncb), in_specs, args, out_specs, out_shape, scratch, "lru_branch")


def _ret_kernel(*refs, seq_len, has_init):
    if has_init:
        (la_ref, q_ref, k_ref, v_ref, g_ref, gn_ref, s0f_ref, s0b_ref, y_ref, acc_ref) = refs
    else:
        (la_ref, q_ref, k_ref, v_ref, g_ref, gn_ref, y_ref, sf_ref, sb_ref, acc_ref) = refs
    h = pl.program_id(1)
    T = min(CHUNK, seq_len)
    nc = seq_len // T
    la_f = la_ref[h, 0]
    la_b = la_ref[h, 1]
    tt = lax.broadcasted_iota(jnp.int32, (T, T), 0)
    ss = lax.broadcasted_iota(jnp.int32, (T, T), 1)
    diff = (tt - ss).astype(f32)
    dsum = (jnp.where(tt >= ss, jnp.exp(la_f * diff), 0.0)
            + jnp.where(ss >= tt, jnp.exp(-la_b * diff), 0.0))
    tcol = lax.broadcasted_iota(jnp.int32, (T, 1), 0).astype(f32)
    scale = RET_DK ** -0.5

    def chunk(c):
        rows = pl.ds(c * T, T)
        q = q_ref[rows, :]
        ks = k_ref[rows, :] * scale
        v = v_ref[rows, :]
        return rows, q, ks, v

    if has_init:
        s_f = s0f_ref[0, 0]
    else:
        s_f = None
    for c in range(nc):
        rows, q, ks, v = chunk(c)
        scores = _dot_nt(q.astype(bf16), ks.astype(bf16)) * dsum
        y = _dot(scores.astype(bf16), v.astype(bf16))
        if s_f is not None:
            y = y + _dot((q * jnp.exp(la_f * (tcol + 1.0))).astype(bf16), s_f.astype(bf16))
        acc_ref[rows, :] = y
        if c < nc - 1 or not has_init:
            upd = _dot(ks.T.astype(bf16), (v * jnp.exp(la_f * (T - 1.0 - tcol))).astype(bf16))
            s_f = upd if s_f is None else jnp.exp(la_f * T) * s_f + upd
    if has_init:
        s_b = s0b_ref[0, 0]
    else:
        s_b = None
    for c in reversed(range(nc)):
        rows, q, ks, v = chunk(c)
        if s_b is not None:
            acc_ref[rows, :] += _dot((q * jnp.exp(la_b * (T - tcol))).astype(bf16), s_b.astype(bf16))
        if c > 0 or not has_init:
            upd = _dot(ks.T.astype(bf16), (v * jnp.exp(la_b * tcol)).astype(bf16))
            s_b = upd if s_b is None else jnp.exp(la_b * T) * s_b + upd
    if not has_init:
        sf_ref[0, 0] = s_f
        sb_ref[0, 0] = s_b
    y = acc_ref[...]
    mu = jnp.mean(y, axis=-1, keepdims=True)
    yc = y - mu
    var = jnp.mean(yc * yc, axis=-1, keepdims=True)
    y = yc * lax.rsqrt(var + EPS) * gn_ref[...]
    y_ref[...] = (y * _silu(g_ref[...])).astype(bf16)


def retention_branch(proj, y_buf, row0, n_seq, seq_len, la, gn, init):
    has_init = init is not None
    rb0 = row0 // seq_len
    in_specs = [pl.BlockSpec(memory_space=pltpu.SMEM),
                pl.BlockSpec((seq_len, RET_DK), lambda s, h: (rb0 + s, OFF_RET_Q // RET_DK + h)),
                pl.BlockSpec((seq_len, RET_DK), lambda s, h: (rb0 + s, OFF_RET_K // RET_DK + h)),
                pl.BlockSpec((seq_len, RET_DV), lambda s, h: (rb0 + s, OFF_RET_V // RET_DV + h)),
                pl.BlockSpec((seq_len, RET_DV), lambda s, h: (rb0 + s, OFF_RET_G // RET_DV + h)),
                pl.BlockSpec((1, RET_DV), lambda s, h: (0, h))]
    args = [la, proj, proj, proj, proj, gn]
    state_spec = pl.BlockSpec((1, 1, RET_DK, RET_DV), lambda s, h: (s, h, 0, 0))
    y_shape = _BRANCH_BUF
    y_spec = pl.BlockSpec((seq_len, RET_DV), lambda s, h: (rb0 + s, h))
    if has_init:
        in_specs += [state_spec, state_spec]
        args += [init[0], init[1]]
        out_specs, out_shape = y_spec, y_shape
    else:
        st = jax.ShapeDtypeStruct((n_seq, RET_HEADS, RET_DK, RET_DV), f32)
        out_specs, out_shape = [y_spec, state_spec, state_spec], [y_shape, st, st]
    return _branch_call(functools.partial(_ret_kernel, seq_len=seq_len, has_init=has_init), y_buf,
                        (n_seq, RET_HEADS), in_specs, args, out_specs, out_shape,
                        [pltpu.VMEM((seq_len, RET_DV), f32)], "retention_branch")


SSD_GW = SSD_HPG * SSD_P


def _split3(x):
    h1 = x.astype(bf16)
    r1 = x - h1.astype(f32)
    h2 = r1.astype(bf16)
    h3 = (r1 - h2.astype(f32)).astype(bf16)
    return h1, h2, h3


def _dot_exact_rhs(m, x):
    h1, h2, h3 = _split3(x)
    return _dot(m, h1) + _dot(m, h2) + _dot(m, h3)


def _dot_exact_lhs(x, m):
    h1, h2, h3 = _split3(x)
    return _dot(h1, m) + _dot(h2, m) + _dot(h3, m)


def _conv4_silu(src_ref, pad_ref, cw_ref, cb_ref, L, width):
    S = V7X_SUBLANES
    zeros = jnp.zeros((S, width), f32)
    pad_ref[pl.ds(0, S), pl.ds(0, width)] = zeros
    pad_ref[pl.ds(S + L, S), pl.ds(0, width)] = zeros
    pad_ref[pl.ds(S, L), pl.ds(0, width)] = src_ref[...]
    cw = cw_ref[...]
    out = cb_ref[...]
    for kk in range(SSD_CONV):
        out = out + pad_ref[pl.ds(S + kk - SSD_CONV // 2, L), pl.ds(0, width)] * cw[kk:kk + 1, :]
    return _silu(out)


def _ssd_kernel(*refs, seq_len, has_init):
    if has_init:
        (z_ref, x_ref, b_ref, c_ref, dt_ref, cwx_ref, cbx_ref, cwb_ref, cbb_ref, cwc_ref, cbc_ref,
         prm_ref, dvec_ref, ng_ref, s0f_ref, s0b_ref,
         y_ref, pad_ref, xs_ref, bs_ref, cs_ref, acc_ref, rb_ref, yn_ref) = refs
    else:
        (z_ref, x_ref, b_ref, c_ref, dt_ref, cwx_ref, cbx_ref, cwb_ref, cbb_ref, cwc_ref, cbc_ref,
         prm_ref, dvec_ref, ng_ref,
         y_ref, sf_ref, sb_ref, pad_ref, xs_ref, bs_ref, cs_ref, acc_ref, rb_ref, yn_ref) = refs
    g = pl.program_id(1)
    L = seq_len
    T = min(CHUNK, L)
    nc = L // T
    H = SSD_HPG
    xs_ref[...] = _conv4_silu(x_ref, pad_ref, cwx_ref, cbx_ref, L, SSD_GW)
    bs_ref[...] = _conv4_silu(b_ref, pad_ref, cwb_ref, cbb_ref, L, SSD_N)
    cs_ref[...] = _conv4_silu(c_ref, pad_ref, cwc_ref, cbc_ref, L, SSD_N)
    prm = prm_ref[0]
    a_neg = -jnp.exp(prm[1:2, :])
    tt = lax.broadcasted_iota(jnp.int32, (T, T), 0)
    ss = lax.broadcasted_iota(jnp.int32, (T, T), 1)
    lower = tt >= ss
    upper = ss >= tt
    tri_l = jnp.where(lower, 1.0, 0.0).astype(bf16)
    tri_u = jnp.where(upper, 1.0, 0.0).astype(bf16)
    er = lax.broadcasted_iota(jnp.int32, (V7X_LANES, SSD_GW), 0)
    ec = lax.broadcasted_iota(jnp.int32, (V7X_LANES, SSD_GW), 1) // SSD_P
    exp_f = jnp.where(er == ec, 1.0, 0.0).astype(bf16)
    exp_b = jnp.where(er == ec + H, 1.0, 0.0).astype(bf16)

    def expand(w, e):
        hi = w.astype(bf16)
        lo = (w - hi.astype(f32)).astype(bf16)
        return _dot(hi, e) + _dot(lo, e)

    def chunk_terms(c):
        rows = pl.ds(c * T, T)
        dt = _softplus(dt_ref[rows, :] + prm[0:1, :])
        da = dt * a_neg
        return rows, dt, da

    s_f = s0f_ref[0, 0] if has_init else None
    for c in range(nc):
        rows, dt, da = chunk_terms(c)
        cum = _dot_exact_rhs(tri_l, da)
        rsum = _dot_exact_rhs(tri_u, da)
        rb_ref[rows, :] = rsum
        da_t = da.T
        cum_t = _dot_exact_lhs(da_t, tri_u)
        rsum_t = _dot_exact_lhs(da_t, tri_l)
        dt_t = dt.T
        bmat = bs_ref[rows, :]
        cmat = cs_ref[rows, :]
        xmat = xs_ref[rows, :]
        gmat = _dot_nt(cmat.astype(bf16), bmat.astype(bf16))
        for hh in range(H):
            lf = jnp.where(lower, jnp.exp(cum[:, hh:hh + 1] - cum_t[hh:hh + 1, :]), 0.0)
            lb = jnp.where(upper, jnp.exp(rsum[:, H + hh:H + hh + 1] - rsum_t[H + hh:H + hh + 1, :]), 0.0)
            m = gmat * (lf * dt_t[hh:hh + 1, :] + lb * dt_t[H + hh:H + hh + 1, :])
            xh = xmat[:, hh * SSD_P:(hh + 1) * SSD_P]
            acc_ref[rows, pl.ds(hh * SSD_P, SSD_P)] = _dot(m.astype(bf16), xh.astype(bf16))
        ecum = jnp.exp(cum)
        if s_f is not None:
            acc_ref[rows, :] += _dot(cmat.astype(bf16), s_f.astype(bf16)) * expand(ecum, exp_f)
        if c < nc - 1 or not has_init:
            tail = jnp.exp(cum[T - 1:T, :] - cum) * dt
            xw = (xmat * expand(tail, exp_f)).astype(bf16)
            upd = _dot(bmat.T.astype(bf16), xw)
            if s_f is None:
                s_f = upd
            else:
                s_f = s_f * expand(jnp.broadcast_to(ecum[T - 1:T, :], (V7X_SUBLANES, V7X_LANES)),
                                   exp_f)[0:1, :] + upd
    s_b = s0b_ref[0, 0] if has_init else None
    for c in reversed(range(nc)):
        rows, dt, da = chunk_terms(c)
        rsum = rb_ref[rows, :]
        ers = jnp.exp(rsum)
        bmat = bs_ref[rows, :]
        cmat = cs_ref[rows, :]
        xmat = xs_ref[rows, :]
        if s_b is not None:
            acc_ref[rows, :] += _dot(cmat.astype(bf16), s_b.astype(bf16)) * expand(ers, exp_b)
        if c > 0 or not has_init:
            tail = jnp.exp(rsum[0:1, :] - rsum) * dt
            xw = (xmat * expand(tail, exp_b)).astype(bf16)
            upd = _dot(bmat.T.astype(bf16), xw)
            if s_b is None:
                s_b = upd
            else:
                s_b = s_b * expand(jnp.broadcast_to(ers[0:1, :], (V7X_SUBLANES, V7X_LANES)),
                                   exp_b)[0:1, :] + upd
    if not has_init:
        for hh in range(H):
            sf_ref[0, hh] = s_f[:, hh * SSD_P:(hh + 1) * SSD_P]
            sb_ref[0, hh] = s_b[:, hh * SSD_P:(hh + 1) * SSD_P]
    yg = (acc_ref[...] + xs_ref[...] * dvec_ref[...]) * _silu(z_ref[...])
    yn_ref[g] = yg

    @pl.when(g == SSD_GROUPS - 1)
    def _():
        ssq = None
        for gg in range(SSD_GROUPS):
            y = yn_ref[gg]
            s = jnp.sum(y * y, axis=-1, keepdims=True)
            ssq = s if ssq is None else ssq + s
        inv = lax.rsqrt(ssq * (1.0 / D_SSD) + EPS)
        for gg in range(SSD_GROUPS):
            cols = pl.ds(gg * SSD_GW, SSD_GW)
            y_ref[:, cols] = (yn_ref[gg] * inv * ng_ref[:, cols]).astype(bf16)


def ssd_branch(proj, y_buf, row0, n_seq, seq_len, conv_w, conv_b, prm, dvec, ng, init):
    has_init = init is not None
    rb0 = row0 // seq_len
    xoff = OFF_SSD_XBC
    boff = OFF_SSD_XBC + D_SSD
    coff = boff + SSD_GROUPS * SSD_N
    in_specs = [pl.BlockSpec((seq_len, SSD_GW), lambda s, g: (rb0 + s, OFF_SSD_Z // SSD_GW + g)),
                pl.BlockSpec((seq_len, SSD_GW), lambda s, g: (rb0 + s, xoff // SSD_GW + g)),
                pl.BlockSpec((seq_len, SSD_N), lambda s, g: (rb0 + s, boff // SSD_N + g)),
                pl.BlockSpec((seq_len, SSD_N), lambda s, g: (rb0 + s, coff // SSD_N + g)),
                pl.BlockSpec((seq_len, V7X_LANES), lambda s, g: (rb0 + s, OFF_SSD_DT // V7X_LANES + g)),
                pl.BlockSpec((SSD_CONV, SSD_GW), lambda s, g: (0, g)),
                pl.BlockSpec((1, SSD_GW), lambda s, g: (0, g)),
                pl.BlockSpec((SSD_CONV, SSD_N), lambda s, g: (0, D_SSD // SSD_N + g)),
                pl.BlockSpec((1, SSD_N), lambda s, g: (0, D_SSD // SSD_N + g)),
                pl.BlockSpec((SSD_CONV, SSD_N), lambda s, g: (0, D_SSD // SSD_N + SSD_GROUPS + g)),
                pl.BlockSpec((1, SSD_N), lambda s, g: (0, D_SSD // SSD_N + SSD_GROUPS + g)),
                pl.BlockSpec((1, V7X_SUBLANES, V7X_LANES), lambda s, g: (g, 0, 0)),
                pl.BlockSpec((1, SSD_GW), lambda s, g: (0, g)),
                pl.BlockSpec((1, D_SSD), lambda s, g: (0, 0))]
    args = [proj, proj, proj, proj, proj, conv_w, conv_b, conv_w, conv_b, conv_w, conv_b, prm, dvec, ng]
    y_shape = _BRANCH_BUF
    y_spec = pl.BlockSpec((seq_len, D_SSD), lambda s, g: (rb0 + s, 0))
    if has_init:
        sin_spec = pl.BlockSpec((1, 1, SSD_N, SSD_GW), lambda s, g: (s, g, 0, 0))
        in_specs += [sin_spec, sin_spec]
        args += [init[0], init[1]]
        out_specs, out_shape = y_spec, y_shape
    else:
        st_spec = pl.BlockSpec((1, SSD_HPG, SSD_N, SSD_P), lambda s, g: (s, g, 0, 0))
        st = jax.ShapeDtypeStruct((n_seq, SSD_HEADS, SSD_N, SSD_P), f32)
        out_specs, out_shape = [y_spec, st_spec, st_spec], [y_shape, st, st]
    scratch = [pltpu.VMEM((seq_len + 2 * V7X_SUBLANES, SSD_GW), f32),
               pltpu.VMEM((seq_len, SSD_GW), f32),
               pltpu.VMEM((seq_len, SSD_N), f32),
               pltpu.VMEM((seq_len, SSD_N), f32),
               pltpu.VMEM((seq_len, SSD_GW), f32),
               pltpu.VMEM((seq_len, V7X_LANES), f32),
               pltpu.VMEM((SSD_GROUPS, seq_len, SSD_GW), f32)]
    return _branch_call(functools.partial(_ssd_kernel, seq_len=seq_len, has_init=has_init), y_buf,
                        (n_seq, SSD_GROUPS), in_specs, args, out_specs, out_shape, scratch, "ssd_branch")


def _head_rmsnorm(x, g):
    return x * lax.rsqrt(jnp.mean(x * x, axis=-1, keepdims=True) + EPS) * g


def _ctx_attn_kernel(q_ref, k_ref, v_ref, qg_ref, kg_ref, y_ref, ko_ref):
    scale = NA_HD ** -0.5
    for h in range(NA_HEADS):
        cols = pl.ds(h * NA_HD, NA_HD)
        q = _head_rmsnorm(q_ref[:, cols], qg_ref[...])
        k = _head_rmsnorm(k_ref[:, cols], kg_ref[...])
        ko_ref[:, cols] = k
        s = _dot_nt(q.astype(bf16), k.astype(bf16)) * scale
        p = jnp.exp(s - jnp.max(s, axis=-1, keepdims=True))
        o = _dot(p.astype(bf16), v_ref[:, cols].astype(bf16)) / jnp.sum(p, axis=-1, keepdims=True)
        y_ref[:, cols] = o.astype(bf16)


def context_attention(proj, y_buf, row0, qg, kg):
    rb0 = row0 // SEQ
    spec = lambda off: pl.BlockSpec((SEQ, NA_W), lambda s: (rb0 + s, off // NA_W))
    gspec = pl.BlockSpec((1, NA_HD), lambda s: (0, 0))
    return _branch_call(
        _ctx_attn_kernel, y_buf, (BATCH,),
        [spec(OFF_NA_Q), spec(OFF_NA_K), spec(OFF_NA_V), gspec, gspec], [proj, proj, proj, qg, kg],
        [pl.BlockSpec((SEQ, NA_W), lambda s: (rb0 + s, 0)), pl.BlockSpec((SEQ, NA_W), lambda s: (s, 0))],
        [_BRANCH_BUF, jax.ShapeDtypeStruct((T_PROMPT, NA_W), f32)], [], "context_attention")


NA_ROWS = DEC_SEQ // GRID_W
NA_NK = NA_WR * GRID_W


def _na_row_start(r):
    return min(max(r - NA_WR // 2, 0), NA_ROWS - NA_WR)


def _rope(x, cos, sin_signed):
    lane = lax.broadcasted_iota(jnp.int32, x.shape, 1)
    quarter = NA_HD // 4
    swapped = jnp.where((lane & (2 * quarter - 1)) < quarter,
                        pltpu.roll(x, NA_HD - quarter, axis=1), pltpu.roll(x, quarter, axis=1))
    return x * cos + swapped * sin_signed


def _na_attn_kernel(q_ref, k_ref, v_ref, kc_ref, vc_ref, qg_ref, kg_ref, cos_ref, sin_ref,
                    bias_ref, valid_ref, y_ref, qs_ref, ks_ref):
    scale = NA_HD ** -0.5
    cos = cos_ref[...]
    sin = sin_ref[...]
    qs_ref[...] = _rope(_head_rmsnorm(q_ref[...], qg_ref[...]), cos, sin).astype(bf16)
    ks_ref[...] = _rope(_head_rmsnorm(k_ref[...], kg_ref[...]), cos, sin).astype(bf16)
    kc = kc_ref[0, 0].astype(bf16)
    vc = vc_ref[0, 0].astype(bf16)
    valid = valid_ref[...] > 0.0
    for r in range(NA_ROWS):
        rs = _na_row_start(r)
        q = qs_ref[pl.ds(r * GRID_W, GRID_W), :]
        kw = ks_ref[pl.ds(rs * GRID_W, NA_NK), :]
        vw = v_ref[pl.ds(rs * GRID_W, NA_NK), :].astype(bf16)
        s_loc = _dot_nt(q, kw) * scale + bias_ref[0, r - rs]
        s_loc = jnp.where(valid, s_loc, -1e30)
        s_ctx = _dot_nt(q, kc) * scale
        m = jnp.maximum(jnp.max(s_loc, axis=-1, keepdims=True), jnp.max(s_ctx, axis=-1, keepdims=True))
        p_loc = jnp.exp(s_loc - m)
        p_ctx = jnp.exp(s_ctx - m)
        denom = jnp.sum(p_loc, axis=-1, keepdims=True) + jnp.sum(p_ctx, axis=-1, keepdims=True)
        o = (_dot(p_loc.astype(bf16), vw) + _dot(p_ctx.astype(bf16), vc)) / denom
        y_ref[pl.ds(r * GRID_W, GRID_W), :] = o.astype(bf16)


def neighbourhood_attention(proj, cache_k, cache_v, layer, qg, kg, cos, sin, bias, valid):
    spec = lambda off: pl.BlockSpec((DEC_SEQ, NA_HD), lambda b, h: (b, off // NA_HD + h))
    cspec = pl.BlockSpec((1, 1, PAST_LEN, NA_HD), lambda b, h: (b, layer, 0, h))
    gspec = pl.BlockSpec((1, NA_HD), lambda b, h: (0, 0))
    tspec = pl.BlockSpec((DEC_SEQ, NA_HD), lambda b, h: (0, 0))
    return pl.pallas_call(
        _na_attn_kernel,
        grid=(DEC_BATCH, NA_HEADS),
        in_specs=[spec(OFF_NA_Q), spec(OFF_NA_K), spec(OFF_NA_V), cspec, cspec, gspec, gspec,
                  tspec, tspec,
                  pl.BlockSpec((1, NA_WR, GRID_W, NA_NK), lambda b, h: (h, 0, 0, 0)),
                  pl.BlockSpec((GRID_W, NA_NK), lambda b, h: (0, 0))],
        out_specs=pl.BlockSpec((DEC_SEQ, NA_HD), lambda b, h: (b, h)),
        out_shape=_BRANCH_BUF,
        scratch_shapes=[pltpu.VMEM((DEC_SEQ, NA_HD), bf16), pltpu.VMEM((DEC_SEQ, NA_HD), bf16)],
        compiler_params=_params(2), name="neighbourhood_attention",
    )(proj, proj, proj, cache_k, cache_v, qg, kg, cos, sin, bias, valid)


def _reorder_w_in(w_in):
    xbc0, dt0, na0 = IN_OFFSETS[6], IN_OFFSETS[7], IN_OFFSETS[8]
    pieces = [w_in[:, :xbc0], w_in[:, na0:], w_in[:, xbc0:dt0]]
    dt = w_in[:, dt0:na0]
    for gidx in range(SSD_GROUPS):
        pieces.append(dt[:, gidx * SSD_HPG:(gidx + 1) * SSD_HPG])
        pieces.append(dt[:, SSD_HEADS + gidx * SSD_HPG:SSD_HEADS + (gidx + 1) * SSD_HPG])
        pieces.append(jnp.zeros((D_MODEL, V7X_LANES - 2 * SSD_HPG), w_in.dtype))
    used = OFF_SSD_DT + SSD_GROUPS * V7X_LANES
    pieces.append(jnp.zeros((D_MODEL, D_IN_PAD - used), w_in.dtype))
    return jnp.concatenate(pieces, axis=1).astype(bf16)


def _group_lanes(v):
    rows = []
    for gidx in range(SSD_GROUPS):
        sl = slice(gidx * SSD_HPG, (gidx + 1) * SSD_HPG)
        rows.append(jnp.concatenate([v[0, sl], v[1, sl], jnp.zeros((V7X_LANES - 2 * SSD_HPG,), f32)]))
    return jnp.stack(rows)


def _rope_tables():
    t = np.arange(DEC_SEQ)
    quarter = NA_HD // 4
    inv = ROPE_BASE ** (-np.arange(quarter, dtype=np.float32) / quarter)
    ang_r = (t // GRID_W).astype(np.float32)[:, None] * inv
    ang_c = (t % GRID_W).astype(np.float32)[:, None] * inv
    cos = np.concatenate([np.cos(ang_r), np.cos(ang_r), np.cos(ang_c), np.cos(ang_c)], axis=1)
    sin = np.concatenate([-np.sin(ang_r), np.sin(ang_r), -np.sin(ang_c), np.sin(ang_c)], axis=1)
    return jnp.asarray(cos, f32), jnp.asarray(sin, f32)


def _na_tables(rpb):
    cq = np.arange(GRID_W)
    kc = np.tile(np.arange(GRID_W), NA_WR)
    col_start = np.clip(cq - NA_WC // 2, 0, GRID_W - NA_WC)
    valid = (kc[None, :] >= col_start[:, None]) & (kc[None, :] < col_start[:, None] + NA_WC)
    col_off = np.clip(cq[None, :] - cq[:, None], 1 - NA_WC, NA_WC - 1) + NA_WC - 1
    onehot = (col_off[None, :, :] == np.arange(2 * NA_WC - 1)[:, None, None]).astype(np.float32)
    toep = jnp.einsum('hic,cqk->hiqk', rpb.astype(f32), jnp.asarray(onehot), precision=lax.Precision.HIGHEST)
    tables = []
    for d in range(NA_WR):
        rows = toep[:, NA_WR - 1 - d:2 * NA_WR - 1 - d]
        tables.append(rows.transpose(0, 2, 1, 3).reshape(NA_HEADS, GRID_W, NA_NK))
    return jnp.stack(tables, axis=1), jnp.asarray(valid, f32)


def kernel(x_prompt, x_sample, cache_na_k, cache_na_v, state_lru_f, state_lru_b,
           state_ret_f, state_ret_b, state_ssd_f, state_ssd_b, c, c_ctx,
           norm1_g, norm2_g, w_ada, b_ada, w_in, w_gate, b_gate, w_branch, w_out,
           lru_conv_w, lru_conv_b, lru_wa, lru_ba, lru_wx, lru_bx, lru_lambda,
           ret_gn_g, ssd_conv_w, ssd_conv_b, ssd_a_log, ssd_dt_bias, ssd_d, ssd_norm_g,
           na_q_g, na_k_g, na_rpb, ffn_w_up, ffn_conv_w, ffn_conv_b, ffn_w_down):
    x = jnp.concatenate([x_sample.reshape(T_SAMPLE, D_MODEL), x_prompt.reshape(T_PROMPT, D_MODEL)], axis=0)
    cond = jnp.concatenate([c, c_ctx[None, :], jnp.zeros((N_COND_PAD - N_COND, D_MODEL), f32)], axis=0)
    mod_all = ada_modulation(cond, w_ada, b_ada)
    cos, sin = _rope_tables()
    hh = jnp.arange(RET_HEADS, dtype=f32)
    ret_la = jnp.stack([jnp.log1p(-jnp.exp2(-5.0 - hh)), jnp.log1p(-jnp.exp2(-5.5 - hh))], axis=1)
    cache_k = cache_na_k.reshape(DEC_BATCH, DEPTH, PAST_LEN, NA_W)
    cache_v = cache_na_v.reshape(DEC_BATCH, DEPTH, PAST_LEN, NA_W)

    new = {k: [] for k in ("k", "v", "lru_f", "lru_b", "ret_f", "ret_b", "ssd_f", "ssd_b")}
    for l in range(DEPTH):
        mod = mod_all[l].reshape(N_COND_PAD, 1, N_MOD * D_MODEL)
        proj, xn = in_projection(x, norm1_g[l][None, :], mod, _reorder_w_in(w_in[l]))

        w4 = jnp.concatenate([lru_wa[l, 0], lru_wx[l, 0], lru_wa[l, 1], lru_wx[l, 1]], axis=-1).astype(bf16)
        b4 = jnp.concatenate([lru_ba[l, 0].reshape(LRU_BLOCKS, 1, LRU_BS), lru_bx[l, 0].reshape(LRU_BLOCKS, 1, LRU_BS),
                              lru_ba[l, 1].reshape(LRU_BLOCKS, 1, LRU_BS), lru_bx[l, 1].reshape(LRU_BLOCKS, 1, LRU_BS)],
                             axis=-1)
        lru_args = (lru_conv_w[l], lru_conv_b[l][None, :], w4, b4, lru_lambda[l])
        y_lru = lru_branch(proj, None, 0, DEC_BATCH, DEC_SEQ, *lru_args, (state_lru_f[:, l], state_lru_b[:, l]))
        y_lru, lf, lb = lru_branch(proj, y_lru, T_SAMPLE, BATCH, SEQ, *lru_args, None)
        new["lru_f"].append(lf.reshape(BATCH, D_RNN))
        new["lru_b"].append(lb.reshape(BATCH, D_RNN))

        gn = ret_gn_g[l][None, :]
        y_ret = retention_branch(proj, None, 0, DEC_BATCH, DEC_SEQ, ret_la, gn,
                                 (state_ret_f[:, l], state_ret_b[:, l]))
        y_ret, rf, rb = retention_branch(proj, y_ret, T_SAMPLE, BATCH, SEQ, ret_la, gn, None)
        new["ret_f"].append(rf)
        new["ret_b"].append(rb)

        prm = jnp.stack([_group_lanes(ssd_dt_bias[l]), _group_lanes(ssd_a_log[l])], axis=1)
        prm = jnp.concatenate([prm, jnp.zeros((SSD_GROUPS, V7X_SUBLANES - 2, V7X_LANES), f32)], axis=1)
        dvec = jnp.repeat(ssd_d[l], SSD_P)[None, :]
        ssd_args = (ssd_conv_w[l], ssd_conv_b[l][None, :], prm, dvec, ssd_norm_g[l][None, :])

        def to_group_layout(s):
            s = s.reshape(DEC_BATCH, SSD_GROUPS, SSD_HPG, SSD_N, SSD_P)
            return s.transpose(0, 1, 3, 2, 4).reshape(DEC_BATCH, SSD_GROUPS, SSD_N, SSD_GW)

        y_ssd = ssd_branch(proj, None, 0, DEC_BATCH, DEC_SEQ, *ssd_args,
                           (to_group_layout(state_ssd_f[:, l]), to_group_layout(state_ssd_b[:, l])))
        y_ssd, sf, sb = ssd_branch(proj, y_ssd, T_SAMPLE, BATCH, SEQ, *ssd_args, None)
        new["ssd_f"].append(sf)
        new["ssd_b"].append(sb)

        qg = na_q_g[l][None, :]
        kg = na_k_g[l][None, :]
        bias, valid = _na_tables(na_rpb[l])
        y_na = neighbourhood_attention(proj, cache_k, cache_v, l, qg, kg, cos, sin, bias, valid)
        y_na, k_new = context_attention(proj, y_na, T_SAMPLE, qg, kg)
        new["k"].append(k_new.reshape(BATCH, SEQ, NA_HEADS, NA_HD))
        new["v"].append(proj[T_SAMPLE:, OFF_NA_V:OFF_NA_V + NA_W].reshape(BATCH, SEQ, NA_HEADS, NA_HD))

        merged = merge_branches(xn, (y_lru, y_ret, y_ssd, y_na),
                                w_gate[l].astype(bf16), b_gate[l][None, :],
                                w_branch[l].astype(bf16))
        x = residual_projection(merged, w_out[l].astype(bf16), x, mod, 2)
        hmid = ffn_up(x, norm2_g[l][None, :], mod, ffn_w_up[l].astype(bf16), ffn_conv_w[l], ffn_conv_b[l][None, :])
        x = residual_projection(hmid, ffn_w_down[l].astype(bf16), x, mod, 5)

    y_sample = x[:T_SAMPLE].reshape(DEC_BATCH, DEC_SEQ, D_MODEL)
    y_prompt = x[T_SAMPLE:].reshape(BATCH, SEQ, D_MODEL)
    st = lambda k: jnp.stack(new[k], axis=1)
    return (y_prompt, y_sample, st("k"), st("v"), st("lru_f"), st("lru_b"),
            st("ret_f"), st("ret_b"), st("ssd_f"), st("ssd_b"))
```

```python
import functools
import math

import jax
import jax.numpy as jnp
import numpy as np
from jax import lax
from jax.experimental import pallas as pl
from jax.experimental.pallas import tpu as pltpu

D_MODEL = 2048
BATCH = 16
SEQ = 256
DEPTH = 2
DEC_BATCH = 8
DEC_SEQ = 1024
PAST_LEN = 256
GRID_W = 64
EPS = 1e-6
N_BRANCH = 4
BRANCH_W = 1024
N_MOD = 6
D_RNN = 1024
LRU_BLOCKS = 8
LRU_BS = D_RNN // LRU_BLOCKS
LRU_CONV = 4
LRU_C = 8.0
RET_HEADS = 4
RET_DK = 128
RET_DV = 256
SSD_HEADS = 16
SSD_P = 64
SSD_N = 128
SSD_GROUPS = 2
SSD_CONV = 4
D_SSD = SSD_HEADS * SSD_P
SSD_CONV_CH = D_SSD + 2 * SSD_GROUPS * SSD_N
NA_HEADS = 8
NA_HD = 128
NA_W = NA_HEADS * NA_HD
NA_WR = 8
NA_WC = 16
ROPE_BASE = 10000.0
D_FF = 5632
FFN_CONV = 3
IN_SIZES = (D_RNN, D_RNN,
            RET_HEADS * RET_DK, RET_HEADS * RET_DK, RET_HEADS * RET_DV, RET_HEADS * RET_DV,
            D_SSD, SSD_CONV_CH, 2 * SSD_HEADS,
            NA_W, NA_W, NA_W)
D_IN = sum(IN_SIZES)
IN_OFFSETS = tuple(int(s) for s in np.cumsum(IN_SIZES)[:-1])

V7X_LANES = 128
V7X_SUBLANES = 8
V7X_VMEM_LIMIT_BYTES = 56 * 1024 * 1024

TM = 1024
T_SAMPLE = DEC_BATCH * DEC_SEQ
T_PROMPT = BATCH * SEQ
T_ALL = T_SAMPLE + T_PROMPT
N_SAMPLE_TILES = T_SAMPLE // TM
N_TILES = T_ALL // TM
N_COND = DEC_BATCH + 1
N_COND_PAD = 16

SSD_HPG = SSD_HEADS // SSD_GROUPS
OFF_LRU_X = 0
OFF_LRU_G = OFF_LRU_X + D_RNN
OFF_RET_Q = OFF_LRU_G + D_RNN
OFF_RET_K = OFF_RET_Q + RET_HEADS * RET_DK
OFF_RET_V = OFF_RET_K + RET_HEADS * RET_DK
OFF_RET_G = OFF_RET_V + RET_HEADS * RET_DV
OFF_SSD_Z = OFF_RET_G + RET_HEADS * RET_DV
OFF_NA_Q = OFF_SSD_Z + D_SSD
OFF_NA_K = OFF_NA_Q + NA_W
OFF_NA_V = OFF_NA_K + NA_W
OFF_SSD_XBC = OFF_NA_V + NA_W
OFF_SSD_DT = OFF_SSD_XBC + SSD_CONV_CH
PROJ_TN = 512
D_IN_PAD = -(-(OFF_SSD_DT + SSD_GROUPS * V7X_LANES) // PROJ_TN) * PROJ_TN

RET_CHUNK = 256
SSD_CHUNK = 256

f32 = jnp.float32
bf16 = jnp.bfloat16

_ARB = "arbitrary"


def _params(n_axes):
    return pltpu.CompilerParams(dimension_semantics=(_ARB,) * n_axes,
                                vmem_limit_bytes=V7X_VMEM_LIMIT_BYTES)


def _mod_spec(k, tile0=0):
    return pl.BlockSpec((1, 1, D_MODEL), lambda i, j: (jnp.minimum(tile0 + i, N_SAMPLE_TILES), 0, k))


def _dot(a, b):
    return jnp.dot(a, b, preferred_element_type=f32)


def _dot_nt(a, b):
    return lax.dot_general(a, b, (((1,), (1,)), ((), ())), preferred_element_type=f32)


def _sigmoid(x):
    return 0.5 * jnp.tanh(0.5 * x) + 0.5


def _silu(x):
    return x * _sigmoid(x)


def _gelu_tanh(x):
    return 0.5 * x * (1.0 + jnp.tanh(math.sqrt(2.0 / math.pi) * (x + 0.044715 * (x * x * x))))


def _softplus(x):
    return jnp.maximum(x, 0.0) + jnp.log1p(jnp.exp(-jnp.abs(x)))


def _drop_refs(body, n, *refs):
    body(*refs[n:])


def _inplace_call(body, bufs, grid, in_specs, args, out_specs, out_shape, scratch, name):
    held = [(k, b) for k, b in enumerate(bufs) if b is not None]
    if held:
        body = functools.partial(_drop_refs, body, len(held))
        in_specs = [pl.BlockSpec(memory_space=pl.ANY)] * len(held) + list(in_specs)
        args = [b for _, b in held] + list(args)
    aliases = {pos: k for pos, (k, _) in enumerate(held)}
    return pl.pallas_call(
        body, grid=grid, in_specs=in_specs, out_specs=out_specs, out_shape=out_shape,
        scratch_shapes=scratch, input_output_aliases=aliases,
        compiler_params=_params(len(grid)), name=name,
    )(*args)


_BRANCH_BUF = jax.ShapeDtypeStruct((T_ALL, BRANCH_W), bf16)


ADA_TN = 1024


def _ada_kernel(c_ref, w_ref, b_ref, o_ref):
    c = _silu(c_ref[...]).astype(bf16)
    o_ref[0] = _dot(c, w_ref[0].astype(bf16)) + b_ref[0]


def ada_modulation(cond, w_ada, b_ada):
    n = N_MOD * D_MODEL
    return pl.pallas_call(
        _ada_kernel,
        grid=(DEPTH, n // ADA_TN),
        in_specs=[pl.BlockSpec((N_COND_PAD, D_MODEL), lambda l, j: (0, 0)),
                  pl.BlockSpec((1, D_MODEL, ADA_TN), lambda l, j: (l, 0, j)),
                  pl.BlockSpec((1, 1, ADA_TN), lambda l, j: (l, 0, j))],
        out_specs=pl.BlockSpec((1, N_COND_PAD, ADA_TN), lambda l, j: (l, 0, j)),
        out_shape=jax.ShapeDtypeStruct((DEPTH, N_COND_PAD, n), f32),
        compiler_params=_params(2),
        name="ada_modulation",
    )(cond, w_ada, b_ada.reshape(DEPTH, 1, n))


NORM_ROWS = 128


def _modulated_norm(x_ref, g_ref, sc_ref, sh_ref, xn_ref):
    g = g_ref[...]
    sc = 1.0 + sc_ref[0]
    sh = sh_ref[0]

    def body(r, carry):
        rows = pl.ds(pl.multiple_of(r * NORM_ROWS, NORM_ROWS), NORM_ROWS)
        x = x_ref[rows, :]
        y = x * lax.rsqrt(jnp.mean(x * x, axis=-1, keepdims=True) + EPS)
        xn_ref[rows, :] = ((y * g) * sc + sh).astype(bf16)
        return carry

    lax.fori_loop(0, TM // NORM_ROWS, body, 0)


def _in_proj_kernel(x_ref, g_ref, sc_ref, sh_ref, w_ref, o_ref, xn_ref):
    @pl.when(pl.program_id(1) == 0)
    def _():
        _modulated_norm(x_ref, g_ref, sc_ref, sh_ref, xn_ref)

    o_ref[...] = _dot(xn_ref[...], w_ref[0]).astype(bf16)


def in_projection(x, x_tile0, tile0, n_tiles, g, mod, w, layer, bufs=(None, None)):
    n = w.shape[2]
    return _inplace_call(
        _in_proj_kernel, bufs, (n_tiles, n // PROJ_TN),
        [pl.BlockSpec((TM, D_MODEL), lambda i, j: (x_tile0 + i, 0)),
         pl.BlockSpec((1, D_MODEL), lambda i, j: (0, 0)),
         _mod_spec(1, tile0), _mod_spec(0, tile0),
         pl.BlockSpec((1, D_MODEL, PROJ_TN), lambda i, j: (layer, 0, j))],
        [x, g, mod, mod, w],
        [pl.BlockSpec((TM, PROJ_TN), lambda i, j: (tile0 + i, j)),
         pl.BlockSpec((TM, D_MODEL), lambda i, j: (tile0 + i, 0))],
        [jax.ShapeDtypeStruct((T_ALL, n), bf16), jax.ShapeDtypeStruct((T_ALL, D_MODEL), bf16)],
        [], "in_projection")


MERGE_TN = 256


def _merge_kernel(xn_ref, *refs):
    y_refs, wg_refs, bg_refs = refs[0:4], refs[4:8], refs[8:12]
    wb_ref, o_ref = refs[12], refs[13]
    xn = xn_ref[...]
    acc = None
    for n in range(N_BRANCH):
        gate = _sigmoid(_dot(xn, wg_refs[n][0]) + bg_refs[n][...])
        term = gate * _dot(y_refs[n][...], wb_ref[0, n])
        acc = term if acc is None else acc + term
    o_ref[...] = acc.astype(bf16)


def merge_branches(xn, ys, w_gate, b_gate, w_branch, layer):
    nj = D_MODEL // MERGE_TN
    y_spec = pl.BlockSpec((TM, BRANCH_W), lambda i, j: (i, 0))
    wg_specs = [pl.BlockSpec((1, D_MODEL, MERGE_TN), lambda i, j, n=n: (layer, 0, n * nj + j))
                for n in range(N_BRANCH)]
    bg_specs = [pl.BlockSpec((1, MERGE_TN), lambda i, j, n=n: (0, n * nj + j)) for n in range(N_BRANCH)]
    return pl.pallas_call(
        _merge_kernel,
        grid=(N_TILES, nj),
        in_specs=[pl.BlockSpec((TM, D_MODEL), lambda i, j: (i, 0)),
                  y_spec, y_spec, y_spec, y_spec, *wg_specs, *bg_specs,
                  pl.BlockSpec((1, N_BRANCH, BRANCH_W, MERGE_TN), lambda i, j: (layer, 0, 0, j))],
        out_specs=pl.BlockSpec((TM, MERGE_TN), lambda i, j: (i, j)),
        out_shape=jax.ShapeDtypeStruct((T_ALL, D_MODEL), bf16),
        compiler_params=_params(2),
        name="merge_branches",
    )(xn, *ys, *([w_gate] * N_BRANCH), *([b_gate] * N_BRANCH), w_branch)


RES_TN = 512


def _residual_kernel(a_ref, w_ref, x_ref, gv_ref, o_ref):
    o_ref[...] = x_ref[...] + gv_ref[0] * _dot(a_ref[...], w_ref[0])


def residual_projection(a, w, layer, x, mod, k_mod, tile0=0, n_tiles=N_TILES, x_tile0=None, out_tile0=None,
                        out_rows=T_ALL, buf=None):
    kdim = a.shape[1]
    x_tile0 = tile0 if x_tile0 is None else x_tile0
    out_tile0 = tile0 if out_tile0 is None else out_tile0
    return _inplace_call(
        _residual_kernel, (buf,), (n_tiles, D_MODEL // RES_TN),
        [pl.BlockSpec((TM, kdim), lambda i, j: (tile0 + i, 0)),
         pl.BlockSpec((1, kdim, RES_TN), lambda i, j: (layer, 0, j)),
         pl.BlockSpec((TM, RES_TN), lambda i, j: (x_tile0 + i, j)),
         pl.BlockSpec((1, 1, RES_TN),
                      lambda i, j: (jnp.minimum(tile0 + i, N_SAMPLE_TILES), 0, k_mod * (D_MODEL // RES_TN) + j))],
        [a, w, x, mod],
        pl.BlockSpec((TM, RES_TN), lambda i, j: (out_tile0 + i, j)),
        jax.ShapeDtypeStruct((out_rows, D_MODEL), f32), [], "residual_projection")


FFN_TN = 512


def _ffn_up_kernel(x_ref, g_ref, sc_ref, sh_ref, wa_ref, wv_ref, cw_ref, cb_ref, o_ref, xn_ref, pad_ref):
    i = pl.program_id(0)

    @pl.when(pl.program_id(1) == 0)
    def _():
        _modulated_norm(x_ref, g_ref, sc_ref, sh_ref, xn_ref)
        zeros = jnp.zeros((V7X_SUBLANES, FFN_TN), f32)
        pad_ref[pl.ds(0, V7X_SUBLANES), :] = zeros
        pad_ref[pl.ds(V7X_SUBLANES + TM, V7X_SUBLANES), :] = zeros

    xn = xn_ref[...]
    a = _dot(xn, wa_ref[0])
    pad_ref[pl.ds(V7X_SUBLANES, TM), :] = a
    seq_len = jnp.where(i < N_SAMPLE_TILES, DEC_SEQ, SEQ)
    pos = lax.broadcasted_iota(jnp.int32, (TM, 1), 0) & (seq_len - 1)
    prev = jnp.where(pos == 0, 0.0, pad_ref[pl.ds(V7X_SUBLANES - 1, TM), :])
    nxt = jnp.where(pos == seq_len - 1, 0.0, pad_ref[pl.ds(V7X_SUBLANES + 1, TM), :])
    cw = cw_ref[...]
    conv = cb_ref[...] + prev * cw[0:1, :] + a * cw[1:2, :] + nxt * cw[2:3, :]
    o_ref[...] = (_gelu_tanh(conv) * _dot(xn, wv_ref[0])).astype(bf16)


def ffn_up(x, g, mod, w_up, layer, conv_w, conv_b):
    nj = D_FF // FFN_TN
    return pl.pallas_call(
        _ffn_up_kernel,
        grid=(N_TILES, nj),
        in_specs=[pl.BlockSpec((TM, D_MODEL), lambda i, j: (i, 0)),
                  pl.BlockSpec((1, D_MODEL), lambda i, j: (0, 0)),
                  _mod_spec(4), _mod_spec(3),
                  pl.BlockSpec((1, D_MODEL, FFN_TN), lambda i, j: (layer, 0, j)),
                  pl.BlockSpec((1, D_MODEL, FFN_TN), lambda i, j: (layer, 0, nj + j)),
                  pl.BlockSpec((FFN_CONV, FFN_TN), lambda i, j: (0, j)),
                  pl.BlockSpec((1, FFN_TN), lambda i, j: (0, j))],
        out_specs=pl.BlockSpec((TM, FFN_TN), lambda i, j: (i, j)),
        out_shape=jax.ShapeDtypeStruct((T_ALL, D_FF), bf16),
        scratch_shapes=[pltpu.VMEM((TM, D_MODEL), bf16),
                        pltpu.VMEM((TM + 2 * V7X_SUBLANES, FFN_TN), f32)],
        compiler_params=_params(2),
        name="ffn_up",
    )(x, g, mod, mod, w_up, w_up, conv_w, conv_b)


LRU_CB = 512
LRU_NB = LRU_CB // LRU_BS
SCAN_UNROLL = 8


def _lru_kernel(*refs, seq_len, has_init):
    if has_init:
        (x_ref, g_ref, cw_ref, cb_ref, w4_ref, b4_ref, lam_ref, h0f_ref, h0b_ref,
         y_ref, pad_ref, af_ref, uf_ref, ab_ref, ub_ref, hf_ref, hb_ref) = refs
    else:
        (x_ref, g_ref, cw_ref, cb_ref, w4_ref, b4_ref, lam_ref,
         y_ref, sf_ref, sb_ref, pad_ref, af_ref, uf_ref, ab_ref, ub_ref, hf_ref, hb_ref) = refs
    L = seq_len
    S = V7X_SUBLANES
    zeros = jnp.zeros((S, LRU_CB), f32)
    pad_ref[pl.ds(0, S), :] = zeros
    pad_ref[pl.ds(S + L, S), :] = zeros
    pad_ref[pl.ds(S, L), :] = x_ref[...].astype(f32)
    cw = cw_ref[...]
    xc = cb_ref[...]
    for kk in range(LRU_CONV):
        xc = xc + pad_ref[pl.ds(S + kk - LRU_CONV // 2, L), :] * cw[kk:kk + 1, :]
    sp = _softplus(-lam_ref[...])
    for n in range(LRU_NB):
        cols = slice(n * LRU_BS, (n + 1) * LRU_BS)
        xcn = xc[:, cols]
        z = _dot(xcn.astype(bf16), w4_ref[n]) + b4_ref[n]
        for d, (a_ref, u_ref) in enumerate(((af_ref, uf_ref), (ab_ref, ub_ref))):
            r = _sigmoid(z[:, (2 * d) * LRU_BS:(2 * d + 1) * LRU_BS])
            ig = _sigmoid(z[:, (2 * d + 1) * LRU_BS:(2 * d + 2) * LRU_BS])
            log_a = (-LRU_C) * r * sp[d:d + 1, cols]
            a = jnp.exp(log_a)
            a_ref[:, cols] = a
            u_ref[:, cols] = jnp.sqrt(-jnp.tanh(log_a) * (a * a + 1.0)) * (ig * xcn)

    if has_init:
        h0f = h0f_ref[0, 0]
        h0b = h0b_ref[0, 0]
    else:
        h0f = jnp.zeros((1, LRU_CB), f32)
        h0b = jnp.zeros((1, LRU_CB), f32)

    def step(t, carry):
        hf, hb = carry
        tb = L - 1 - t
        hf = af_ref[pl.ds(t, 1), :] * hf + uf_ref[pl.ds(t, 1), :]
        hb = ab_ref[pl.ds(tb, 1), :] * hb + ub_ref[pl.ds(tb, 1), :]
        hf_ref[pl.ds(t, 1), :] = hf
        hb_ref[pl.ds(tb, 1), :] = hb
        return hf, hb

    hf, hb = lax.fori_loop(0, L, step, (h0f, h0b), unroll=SCAN_UNROLL)
    if not has_init:
        sf_ref[0, 0] = hf
        sb_ref[0, 0] = hb
    y_ref[...] = ((hf_ref[...] + hb_ref[...]) * _gelu_tanh(g_ref[...].astype(f32))).astype(bf16)


def lru_branch(proj, bufs, row0, n_seq, seq_len, cw, cb, w4, b4, lam, init, layer):
    has_init = init is not None
    ncb = D_RNN // LRU_CB
    rb0 = row0 // seq_len
    in_specs = [pl.BlockSpec((seq_len, LRU_CB), lambda s, c: (rb0 + s, OFF_LRU_X // LRU_CB + c)),
                pl.BlockSpec((seq_len, LRU_CB), lambda s, c: (rb0 + s, OFF_LRU_G // LRU_CB + c)),
                pl.BlockSpec((LRU_CONV, LRU_CB), lambda s, c: (0, c)),
                pl.BlockSpec((1, LRU_CB), lambda s, c: (0, c)),
                pl.BlockSpec((LRU_NB, LRU_BS, 4 * LRU_BS), lambda s, c: (c, 0, 0)),
                pl.BlockSpec((LRU_NB, 1, 4 * LRU_BS), lambda s, c: (c, 0, 0)),
                pl.BlockSpec((2, LRU_CB), lambda s, c: (0, c))]
    args = [proj, proj, cw, cb, w4, b4, lam]
    state_spec = pl.BlockSpec((1, 1, 1, LRU_CB), lambda s, c: (s, layer, 0, c))
    y_spec = pl.BlockSpec((seq_len, LRU_CB), lambda s, c: (rb0 + s, c))
    if has_init:
        in_specs += [state_spec, state_spec]
        args += [init[0], init[1]]
        out_specs, out_shape = y_spec, _BRANCH_BUF
    else:
        st = jax.ShapeDtypeStruct((n_seq, DEPTH, 1, D_RNN), f32)
        out_specs, out_shape = [y_spec, state_spec, state_spec], [_BRANCH_BUF, st, st]
    scratch = [pltpu.VMEM((seq_len + 2 * V7X_SUBLANES, LRU_CB), f32)] + \
              [pltpu.VMEM((seq_len, LRU_CB), f32) for _ in range(6)]
    return _inplace_call(functools.partial(_lru_kernel, seq_len=seq_len, has_init=has_init), bufs,
                         (n_seq, ncb), in_specs, args, out_specs, out_shape, scratch, "lru_branch")


def _ret_kernel(*refs, seq_len, has_init):
    if has_init:
        (la_ref, q_ref, k_ref, v_ref, g_ref, gn_ref, s0f_ref, s0b_ref, y_ref, acc_ref) = refs
    else:
        (la_ref, q_ref, k_ref, v_ref, g_ref, gn_ref, y_ref, sf_ref, sb_ref, acc_ref) = refs
    h = pl.program_id(1)
    T = min(RET_CHUNK, seq_len)
    nc = seq_len // T
    la_f = la_ref[h, 0]
    la_b = la_ref[h, 1]
    tt = lax.broadcasted_iota(jnp.int32, (T, T), 0)
    ss = lax.broadcasted_iota(jnp.int32, (T, T), 1)
    diff = (tt - ss).astype(f32)
    dsum = (jnp.where(tt >= ss, jnp.exp(la_f * diff), 0.0)
            + jnp.where(ss >= tt, jnp.exp(-la_b * diff), 0.0))
    tcol = lax.broadcasted_iota(jnp.int32, (T, 1), 0).astype(f32)
    scale = RET_DK ** -0.5

    def chunk(c):
        rows = pl.ds(c * T, T)
        q = q_ref[rows, :].astype(f32)
        ks = k_ref[rows, :].astype(f32) * scale
        v = v_ref[rows, :].astype(f32)
        return rows, q, ks, v

    s_f = s0f_ref[0, 0, 0] if has_init else None
    for c in range(nc):
        rows, q, ks, v = chunk(c)
        scores = _dot_nt(q.astype(bf16), ks.astype(bf16)) * dsum
        y = _dot(scores.astype(bf16), v.astype(bf16))
        if s_f is not None:
            y = y + _dot((q * jnp.exp(la_f * (tcol + 1.0))).astype(bf16), s_f.astype(bf16))
        acc_ref[rows, :] = y
        if c < nc - 1 or not has_init:
            upd = _dot(ks.T.astype(bf16), (v * jnp.exp(la_f * (T - 1.0 - tcol))).astype(bf16))
            s_f = upd if s_f is None else jnp.exp(la_f * T) * s_f + upd
    s_b = s0b_ref[0, 0, 0] if has_init else None
    for c in reversed(range(nc)):
        rows, q, ks, v = chunk(c)
        if s_b is not None:
            acc_ref[rows, :] += _dot((q * jnp.exp(la_b * (T - tcol))).astype(bf16), s_b.astype(bf16))
        if c > 0 or not has_init:
            upd = _dot(ks.T.astype(bf16), (v * jnp.exp(la_b * tcol)).astype(bf16))
            s_b = upd if s_b is None else jnp.exp(la_b * T) * s_b + upd
    if not has_init:
        sf_ref[0, 0, 0] = s_f
        sb_ref[0, 0, 0] = s_b
    y = acc_ref[...]
    mu = jnp.mean(y, axis=-1, keepdims=True)
    yc = y - mu
    var = jnp.mean(yc * yc, axis=-1, keepdims=True)
    y = yc * lax.rsqrt(var + EPS) * gn_ref[...]
    y_ref[...] = (y * _silu(g_ref[...].astype(f32))).astype(bf16)


def retention_branch(proj, bufs, row0, n_seq, seq_len, la, gn, init, layer):
    has_init = init is not None
    rb0 = row0 // seq_len
    in_specs = [pl.BlockSpec(memory_space=pltpu.SMEM),
                pl.BlockSpec((seq_len, RET_DK), lambda s, h: (rb0 + s, OFF_RET_Q // RET_DK + h)),
                pl.BlockSpec((seq_len, RET_DK), lambda s, h: (rb0 + s, OFF_RET_K // RET_DK + h)),
                pl.BlockSpec((seq_len, RET_DV), lambda s, h: (rb0 + s, OFF_RET_V // RET_DV + h)),
                pl.BlockSpec((seq_len, RET_DV), lambda s, h: (rb0 + s, OFF_RET_G // RET_DV + h)),
                pl.BlockSpec((1, RET_DV), lambda s, h: (0, h))]
    args = [la, proj, proj, proj, proj, gn]
    state_spec = pl.BlockSpec((1, 1, 1, RET_DK, RET_DV), lambda s, h: (s, layer, h, 0, 0))
    y_spec = pl.BlockSpec((seq_len, RET_DV), lambda s, h: (rb0 + s, h))
    if has_init:
        in_specs += [state_spec, state_spec]
        args += [init[0], init[1]]
        out_specs, out_shape = y_spec, _BRANCH_BUF
    else:
        st = jax.ShapeDtypeStruct((n_seq, DEPTH, RET_HEADS, RET_DK, RET_DV), f32)
        out_specs, out_shape = [y_spec, state_spec, state_spec], [_BRANCH_BUF, st, st]
    return _inplace_call(functools.partial(_ret_kernel, seq_len=seq_len, has_init=has_init), bufs,
                         (n_seq, RET_HEADS), in_specs, args, out_specs, out_shape,
                         [pltpu.VMEM((seq_len, RET_DV), f32)], "retention_branch")


SSD_GW = SSD_HPG * SSD_P


def _split3(x):
    h1 = x.astype(bf16)
    r1 = x - h1.astype(f32)
    h2 = r1.astype(bf16)
    h3 = (r1 - h2.astype(f32)).astype(bf16)
    return h1, h2, h3


def _dot_exact_rhs(m, x):
    h1, h2, h3 = _split3(x)
    return _dot(m, h1) + _dot(m, h2) + _dot(m, h3)


def _dot_exact_lhs(x, m):
    h1, h2, h3 = _split3(x)
    return _dot(h1, m) + _dot(h2, m) + _dot(h3, m)


def _conv4_silu(src_ref, pad_ref, cw_ref, cb_ref, L, width):
    S = V7X_SUBLANES
    zeros = jnp.zeros((S, width), f32)
    pad_ref[pl.ds(0, S), pl.ds(0, width)] = zeros
    pad_ref[pl.ds(S + L, S), pl.ds(0, width)] = zeros
    pad_ref[pl.ds(S, L), pl.ds(0, width)] = src_ref[...].astype(f32)
    cw = cw_ref[...]
    out = cb_ref[...]
    for kk in range(SSD_CONV):
        out = out + pad_ref[pl.ds(S + kk - SSD_CONV // 2, L), pl.ds(0, width)] * cw[kk:kk + 1, :]
    return _silu(out)


def _heads_to_lanes(s_ref, st_ref):
    for hh in range(SSD_HPG):
        st_ref[:, pl.ds(hh * SSD_P, SSD_P)] = s_ref[0, 0, hh]
    return st_ref[...]


def _ssd_kernel(*refs, seq_len, has_init):
    if has_init:
        (z_ref, x_ref, b_ref, c_ref, dt_ref, cwx_ref, cbx_ref, cwb_ref, cbb_ref, cwc_ref, cbc_ref,
         prm_ref, dvec_ref, ng_ref, s0f_ref, s0b_ref,
         y_ref, pad_ref, xs_ref, bs_ref, cs_ref, acc_ref, rb_ref, yn_ref, st_ref) = refs
    else:
        (z_ref, x_ref, b_ref, c_ref, dt_ref, cwx_ref, cbx_ref, cwb_ref, cbb_ref, cwc_ref, cbc_ref,
         prm_ref, dvec_ref, ng_ref,
         y_ref, sf_ref, sb_ref, pad_ref, xs_ref, bs_ref, cs_ref, acc_ref, rb_ref, yn_ref, st_ref) = refs
    g = pl.program_id(1)
    L = seq_len
    T = min(SSD_CHUNK, L)
    nc = L // T
    H = SSD_HPG
    xs_ref[...] = _conv4_silu(x_ref, pad_ref, cwx_ref, cbx_ref, L, SSD_GW)
    bs_ref[...] = _conv4_silu(b_ref, pad_ref, cwb_ref, cbb_ref, L, SSD_N)
    cs_ref[...] = _conv4_silu(c_ref, pad_ref, cwc_ref, cbc_ref, L, SSD_N)
    prm = prm_ref[0]
    a_neg = -jnp.exp(prm[1:2, :])
    tt = lax.broadcasted_iota(jnp.int32, (T, T), 0)
    ss = lax.broadcasted_iota(jnp.int32, (T, T), 1)
    lower = tt >= ss
    upper = ss >= tt
    tri_l = jnp.where(lower, 1.0, 0.0).astype(bf16)
    tri_u = jnp.where(upper, 1.0, 0.0).astype(bf16)
    er = lax.broadcasted_iota(jnp.int32, (V7X_LANES, SSD_GW), 0)
    ec = lax.broadcasted_iota(jnp.int32, (V7X_LANES, SSD_GW), 1) // SSD_P
    exp_f = jnp.where(er == ec, 1.0, 0.0).astype(bf16)
    exp_b = jnp.where(er == ec + H, 1.0, 0.0).astype(bf16)

    def expand(w, e):
        hi = w.astype(bf16)
        lo = (w - hi.astype(f32)).astype(bf16)
        return _dot(hi, e) + _dot(lo, e)

    def chunk_terms(c):
        rows = pl.ds(c * T, T)
        dt = _softplus(dt_ref[rows, :].astype(f32) + prm[0:1, :])
        da = dt * a_neg
        return rows, dt, da

    s_f = _heads_to_lanes(s0f_ref, st_ref) if has_init else None
    for c in range(nc):
        rows, dt, da = chunk_terms(c)
        cum = _dot_exact_rhs(tri_l, da)
        rsum = _dot_exact_rhs(tri_u, da)
        rb_ref[rows, :] = rsum
        da_t = da.T
        cum_t = _dot_exact_lhs(da_t, tri_u)
        rsum_t = _dot_exact_lhs(da_t, tri_l)
        dt_t = dt.T
        bmat = bs_ref[rows, :]
        cmat = cs_ref[rows, :]
        xmat = xs_ref[rows, :]
        gmat = _dot_nt(cmat.astype(bf16), bmat.astype(bf16))
        for hh in range(H):
            lf = jnp.where(lower, jnp.exp(cum[:, hh:hh + 1] - cum_t[hh:hh + 1, :]), 0.0)
            lb = jnp.where(upper, jnp.exp(rsum[:, H + hh:H + hh + 1] - rsum_t[H + hh:H + hh + 1, :]), 0.0)
            m = gmat * (lf * dt_t[hh:hh + 1, :] + lb * dt_t[H + hh:H + hh + 1, :])
            xh = xmat[:, hh * SSD_P:(hh + 1) * SSD_P]
            acc_ref[rows, pl.ds(hh * SSD_P, SSD_P)] = _dot(m.astype(bf16), xh.astype(bf16))
        ecum = jnp.exp(cum)
        if s_f is not None:
            acc_ref[rows, :] += _dot(cmat.astype(bf16), s_f.astype(bf16)) * expand(ecum, exp_f)
        if c < nc - 1 or not has_init:
            tail = jnp.exp(cum[T - 1:T, :] - cum) * dt
            xw = (xmat * expand(tail, exp_f)).astype(bf16)
            upd = _dot(bmat.T.astype(bf16), xw)
            if s_f is None:
                s_f = upd
            else:
                s_f = s_f * expand(jnp.broadcast_to(ecum[T - 1:T, :], (V7X_SUBLANES, V7X_LANES)),
                                   exp_f)[0:1, :] + upd
    s_b = _heads_to_lanes(s0b_ref, st_ref) if has_init else None
    for c in reversed(range(nc)):
        rows, dt, da = chunk_terms(c)
        rsum = rb_ref[rows, :]
        ers = jnp.exp(rsum)
        bmat = bs_ref[rows, :]
        cmat = cs_ref[rows, :]
        xmat = xs_ref[rows, :]
        if s_b is not None:
            acc_ref[rows, :] += _dot(cmat.astype(bf16), s_b.astype(bf16)) * expand(ers, exp_b)
        if c > 0 or not has_init:
            tail = jnp.exp(rsum[0:1, :] - rsum) * dt
            xw = (xmat * expand(tail, exp_b)).astype(bf16)
            upd = _dot(bmat.T.astype(bf16), xw)
            if s_b is None:
                s_b = upd
            else:
                s_b = s_b * expand(jnp.broadcast_to(ers[0:1, :], (V7X_SUBLANES, V7X_LANES)),
                                   exp_b)[0:1, :] + upd
    if not has_init:
        for hh in range(H):
            sf_ref[0, 0, hh] = s_f[:, hh * SSD_P:(hh + 1) * SSD_P]
            sb_ref[0, 0, hh] = s_b[:, hh * SSD_P:(hh + 1) * SSD_P]
    yg = (acc_ref[...] + xs_ref[...] * dvec_ref[...]) * _silu(z_ref[...].astype(f32))
    yn_ref[g] = yg

    @pl.when(g == SSD_GROUPS - 1)
    def _():
        ssq = None
        for gg in range(SSD_GROUPS):
            y = yn_ref[gg]
            s = jnp.sum(y * y, axis=-1, keepdims=True)
            ssq = s if ssq is None else ssq + s
        inv = lax.rsqrt(ssq * (1.0 / D_SSD) + EPS)
        for gg in range(SSD_GROUPS):
            cols = pl.ds(gg * SSD_GW, SSD_GW)
            y_ref[:, cols] = (yn_ref[gg] * inv * ng_ref[:, cols]).astype(bf16)


def ssd_branch(proj, bufs, row0, n_seq, seq_len, conv_w, conv_b, prm, dvec, ng, init, layer):
    has_init = init is not None
    rb0 = row0 // seq_len
    xoff = OFF_SSD_XBC
    boff = OFF_SSD_XBC + D_SSD
    coff = boff + SSD_GROUPS * SSD_N
    in_specs = [pl.BlockSpec((seq_len, SSD_GW), lambda s, g: (rb0 + s, OFF_SSD_Z // SSD_GW + g)),
                pl.BlockSpec((seq_len, SSD_GW), lambda s, g: (rb0 + s, xoff // SSD_GW + g)),
                pl.BlockSpec((seq_len, SSD_N), lambda s, g: (rb0 + s, boff // SSD_N + g)),
                pl.BlockSpec((seq_len, SSD_N), lambda s, g: (rb0 + s, coff // SSD_N + g)),
                pl.BlockSpec((seq_len, V7X_LANES), lambda s, g: (rb0 + s, OFF_SSD_DT // V7X_LANES + g)),
                pl.BlockSpec((SSD_CONV, SSD_GW), lambda s, g: (0, g)),
                pl.BlockSpec((1, SSD_GW), lambda s, g: (0, g)),
                pl.BlockSpec((SSD_CONV, SSD_N), lambda s, g: (0, D_SSD // SSD_N + g)),
                pl.BlockSpec((1, SSD_N), lambda s, g: (0, D_SSD // SSD_N + g)),
                pl.BlockSpec((SSD_CONV, SSD_N), lambda s, g: (0, D_SSD // SSD_N + SSD_GROUPS + g)),
                pl.BlockSpec((1, SSD_N), lambda s, g: (0, D_SSD // SSD_N + SSD_GROUPS + g)),
                pl.BlockSpec((1, V7X_SUBLANES, V7X_LANES), lambda s, g: (g, 0, 0)),
                pl.BlockSpec((1, SSD_GW), lambda s, g: (0, g)),
                pl.BlockSpec((1, D_SSD), lambda s, g: (0, 0))]
    args = [proj, proj, proj, proj, proj, conv_w, conv_b, conv_w, conv_b, conv_w, conv_b, prm, dvec, ng]
    state_spec = pl.BlockSpec((1, 1, SSD_HPG, SSD_N, SSD_P), lambda s, g: (s, layer, g, 0, 0))
    y_spec = pl.BlockSpec((seq_len, D_SSD), lambda s, g: (rb0 + s, 0))
    if has_init:
        in_specs += [state_spec, state_spec]
        args += [init[0], init[1]]
        out_specs, out_shape = y_spec, _BRANCH_BUF
    else:
        st = jax.ShapeDtypeStruct((n_seq, DEPTH, SSD_HEADS, SSD_N, SSD_P), f32)
        out_specs, out_shape = [y_spec, state_spec, state_spec], [_BRANCH_BUF, st, st]
    scratch = [pltpu.VMEM((seq_len + 2 * V7X_SUBLANES, SSD_GW), f32),
               pltpu.VMEM((seq_len, SSD_GW), f32),
               pltpu.VMEM((seq_len, SSD_N), f32),
               pltpu.VMEM((seq_len, SSD_N), f32),
               pltpu.VMEM((seq_len, SSD_GW), f32),
               pltpu.VMEM((seq_len, V7X_LANES), f32),
               pltpu.VMEM((SSD_GROUPS, seq_len, SSD_GW), f32),
               pltpu.VMEM((SSD_N, SSD_GW), f32)]
    return _inplace_call(functools.partial(_ssd_kernel, seq_len=seq_len, has_init=has_init), bufs,
                         (n_seq, SSD_GROUPS), in_specs, args, out_specs, out_shape, scratch, "ssd_branch")


def _head_rmsnorm(x, g):
    return x * lax.rsqrt(jnp.mean(x * x, axis=-1, keepdims=True) + EPS) * g


def _ctx_attn_kernel(q_ref, k_ref, v_ref, qg_ref, kg_ref, y_ref, ko_ref, vo_ref):
    scale = NA_HD ** -0.5
    vo_ref[0, 0] = v_ref[...].astype(f32)
    for h in range(NA_HEADS):
        cols = pl.ds(h * NA_HD, NA_HD)
        q = _head_rmsnorm(q_ref[:, cols].astype(f32), qg_ref[...])
        k = _head_rmsnorm(k_ref[:, cols].astype(f32), kg_ref[...])
        ko_ref[0, 0, :, cols] = k
        s = _dot_nt(q.astype(bf16), k.astype(bf16)) * scale
        p = jnp.exp(s - jnp.max(s, axis=-1, keepdims=True))
        o = _dot(p.astype(bf16), v_ref[:, cols]) / jnp.sum(p, axis=-1, keepdims=True)
        y_ref[:, cols] = o.astype(bf16)


def context_attention(proj, bufs, row0, qg, kg, layer):
    rb0 = row0 // SEQ
    spec = lambda off: pl.BlockSpec((SEQ, NA_W), lambda s: (rb0 + s, off // NA_W))
    gspec = pl.BlockSpec((1, NA_HD), lambda s: (0, 0))
    kv_spec = pl.BlockSpec((1, 1, SEQ, NA_W), lambda s: (s, layer, 0, 0))
    kv_shape = jax.ShapeDtypeStruct((BATCH, DEPTH, SEQ, NA_W), f32)
    return _inplace_call(
        _ctx_attn_kernel, bufs, (BATCH,),
        [spec(OFF_NA_Q), spec(OFF_NA_K), spec(OFF_NA_V), gspec, gspec], [proj, proj, proj, qg, kg],
        [pl.BlockSpec((SEQ, NA_W), lambda s: (rb0 + s, 0)), kv_spec, kv_spec],
        [_BRANCH_BUF, kv_shape, kv_shape], [], "context_attention")


NA_ROWS = DEC_SEQ // GRID_W
NA_NK = NA_WR * GRID_W


def _na_row_start(r):
    return min(max(r - NA_WR // 2, 0), NA_ROWS - NA_WR)


def _rope(x, cos, sin_signed):
    lane = lax.broadcasted_iota(jnp.int32, x.shape, 1)
    quarter = NA_HD // 4
    swapped = jnp.where((lane & (2 * quarter - 1)) < quarter,
                        pltpu.roll(x, NA_HD - quarter, axis=1), pltpu.roll(x, quarter, axis=1))
    return x * cos + swapped * sin_signed


def _na_row_groups():
    groups, r = [], 0
    while r < NA_ROWS:
        n = 1
        while r + n < NA_ROWS and _na_row_start(r + n) == _na_row_start(r):
            n += 1
        groups.append((r, n, _na_row_start(r)))
        r += n
    return groups


def _na_attn_kernel(q_ref, k_ref, v_ref, kc_ref, vc_ref, qg_ref, kg_ref, cos_ref, sin_ref,
                    bias_ref, valid_ref, y_ref, qs_ref, ks_ref, oc_ref, mc_ref, lc_ref):
    scale = NA_HD ** -0.5
    cos = cos_ref[...]
    sin = sin_ref[...]
    qs_ref[...] = _rope(_head_rmsnorm(q_ref[...].astype(f32), qg_ref[...]), cos, sin).astype(bf16)
    ks_ref[...] = _rope(_head_rmsnorm(k_ref[...].astype(f32), kg_ref[...]), cos, sin).astype(bf16)
    s_ctx = _dot_nt(qs_ref[...], kc_ref[0, 0].astype(bf16)) * scale
    m_ctx = jnp.max(s_ctx, axis=-1, keepdims=True)
    p_ctx = jnp.exp(s_ctx - m_ctx)
    mc_ref[...] = m_ctx
    lc_ref[...] = jnp.sum(p_ctx, axis=-1, keepdims=True)
    oc_ref[...] = _dot(p_ctx.astype(bf16), vc_ref[0, 0].astype(bf16))
    valid = valid_ref[...] > 0.0
    for r0, n, rs in _na_row_groups():
        rows = pl.ds(r0 * GRID_W, n * GRID_W)
        d0 = r0 - rs
        kw = ks_ref[pl.ds(rs * GRID_W, NA_NK), :]
        vw = v_ref[pl.ds(rs * GRID_W, NA_NK), :]
        s = (_dot_nt(qs_ref[rows, :], kw) * scale).reshape(n, GRID_W, NA_NK) + bias_ref[0, d0:d0 + n]
        s = jnp.where(valid[None], s, -1e30).reshape(n * GRID_W, NA_NK)
        m_loc = jnp.max(s, axis=-1, keepdims=True)
        m_ctx = mc_ref[rows, :]
        m = jnp.maximum(m_loc, m_ctx)
        p = jnp.exp(s - m)
        a_ctx = jnp.exp(m_ctx - m)
        denom = jnp.sum(p, axis=-1, keepdims=True) + a_ctx * lc_ref[rows, :]
        o = (_dot(p.astype(bf16), vw) + a_ctx * oc_ref[rows, :]) / denom
        y_ref[rows, :] = o.astype(bf16)


def neighbourhood_attention(proj, cache_k, cache_v, layer, qg, kg, cos, sin, bias, valid):
    spec = lambda off: pl.BlockSpec((DEC_SEQ, NA_HD), lambda b, h: (b, off // NA_HD + h))
    cspec = pl.BlockSpec((1, 1, PAST_LEN, NA_HD), lambda b, h: (b, layer, 0, h))
    gspec = pl.BlockSpec((1, NA_HD), lambda b, h: (0, 0))
    tspec = pl.BlockSpec((DEC_SEQ, NA_HD), lambda b, h: (0, 0))
    return pl.pallas_call(
        _na_attn_kernel,
        grid=(DEC_BATCH, NA_HEADS),
        in_specs=[spec(OFF_NA_Q), spec(OFF_NA_K), spec(OFF_NA_V), cspec, cspec, gspec, gspec,
                  tspec, tspec,
                  pl.BlockSpec((1, NA_WR, GRID_W, NA_NK), lambda b, h: (h, 0, 0, 0)),
                  pl.BlockSpec((GRID_W, NA_NK), lambda b, h: (0, 0))],
        out_specs=pl.BlockSpec((DEC_SEQ, NA_HD), lambda b, h: (b, h)),
        out_shape=_BRANCH_BUF,
        scratch_shapes=[pltpu.VMEM((DEC_SEQ, NA_HD), bf16), pltpu.VMEM((DEC_SEQ, NA_HD), bf16),
                        pltpu.VMEM((DEC_SEQ, NA_HD), f32), pltpu.VMEM((DEC_SEQ, 1), f32),
                        pltpu.VMEM((DEC_SEQ, 1), f32)],
        compiler_params=_params(2), name="neighbourhood_attention",
    )(proj, proj, proj, cache_k, cache_v, qg, kg, cos, sin, bias, valid)


def _reorder_w_in(w_in):
    xbc0, dt0, na0 = IN_OFFSETS[6], IN_OFFSETS[7], IN_OFFSETS[8]
    w_in = w_in.astype(bf16)
    pieces = [w_in[..., :xbc0], w_in[..., na0:], w_in[..., xbc0:dt0]]
    dt = w_in[..., dt0:na0]
    for gidx in range(SSD_GROUPS):
        pieces.append(dt[..., gidx * SSD_HPG:(gidx + 1) * SSD_HPG])
        pieces.append(dt[..., SSD_HEADS + gidx * SSD_HPG:SSD_HEADS + (gidx + 1) * SSD_HPG])
        pieces.append(jnp.zeros((DEPTH, D_MODEL, V7X_LANES - 2 * SSD_HPG), bf16))
    used = OFF_SSD_DT + SSD_GROUPS * V7X_LANES
    pieces.append(jnp.zeros((DEPTH, D_MODEL, D_IN_PAD - used), bf16))
    return jnp.concatenate(pieces, axis=-1)


def _group_lanes(v):
    rows = []
    for gidx in range(SSD_GROUPS):
        sl = slice(gidx * SSD_HPG, (gidx + 1) * SSD_HPG)
        rows.append(jnp.concatenate([v[0, sl], v[1, sl], jnp.zeros((V7X_LANES - 2 * SSD_HPG,), f32)]))
    return jnp.stack(rows)


def _rope_tables():
    t = np.arange(DEC_SEQ)
    quarter = NA_HD // 4
    inv = ROPE_BASE ** (-np.arange(quarter, dtype=np.float32) / quarter)
    ang_r = (t // GRID_W).astype(np.float32)[:, None] * inv
    ang_c = (t % GRID_W).astype(np.float32)[:, None] * inv
    cos = np.concatenate([np.cos(ang_r), np.cos(ang_r), np.cos(ang_c), np.cos(ang_c)], axis=1)
    sin = np.concatenate([-np.sin(ang_r), np.sin(ang_r), -np.sin(ang_c), np.sin(ang_c)], axis=1)
    return jnp.asarray(cos, f32), jnp.asarray(sin, f32)


def _na_tables(rpb):
    cq = np.arange(GRID_W)
    kc = np.tile(np.arange(GRID_W), NA_WR)
    col_start = np.clip(cq - NA_WC // 2, 0, GRID_W - NA_WC)
    valid = (kc[None, :] >= col_start[:, None]) & (kc[None, :] < col_start[:, None] + NA_WC)
    col_off = np.clip(cq[None, :] - cq[:, None], 1 - NA_WC, NA_WC - 1) + NA_WC - 1
    onehot = (col_off[None, :, :] == np.arange(2 * NA_WC - 1)[:, None, None]).astype(np.float32)
    toep = jnp.einsum('hic,cqk->hiqk', rpb.astype(f32), jnp.asarray(onehot), precision=lax.Precision.HIGHEST)
    tables = []
    for d in range(NA_WR):
        rows = toep[:, NA_WR - 1 - d:2 * NA_WR - 1 - d]
        tables.append(rows.transpose(0, 2, 1, 3).reshape(NA_HEADS, GRID_W, NA_NK))
    return jnp.stack(tables, axis=1), jnp.asarray(valid, f32)


def kernel(x_prompt, x_sample, cache_na_k, cache_na_v, state_lru_f, state_lru_b,
           state_ret_f, state_ret_b, state_ssd_f, state_ssd_b, c, c_ctx,
           norm1_g, norm2_g, w_ada, b_ada, w_in, w_gate, b_gate, w_branch, w_out,
           lru_conv_w, lru_conv_b, lru_wa, lru_ba, lru_wx, lru_bx, lru_lambda,
           ret_gn_g, ssd_conv_w, ssd_conv_b, ssd_a_log, ssd_dt_bias, ssd_d, ssd_norm_g,
           na_q_g, na_k_g, na_rpb, ffn_w_up, ffn_conv_w, ffn_conv_b, ffn_w_down):
    xs0 = x_sample.reshape(T_SAMPLE, D_MODEL)
    xp0 = x_prompt.reshape(T_PROMPT, D_MODEL)
    n_ptiles = N_TILES - N_SAMPLE_TILES
    cond = jnp.concatenate([c, c_ctx[None, :], jnp.zeros((N_COND_PAD - N_COND, D_MODEL), f32)], axis=0)
    mod_all = ada_modulation(cond, w_ada, b_ada)
    cos, sin = _rope_tables()
    hh = jnp.arange(RET_HEADS, dtype=f32)
    ret_la = jnp.stack([jnp.log1p(-jnp.exp2(-5.0 - hh)), jnp.log1p(-jnp.exp2(-5.5 - hh))], axis=1)
    cache_k = cache_na_k.reshape(DEC_BATCH, DEPTH, PAST_LEN, NA_W)
    cache_v = cache_na_v.reshape(DEC_BATCH, DEPTH, PAST_LEN, NA_W)
    lru_init = (state_lru_f.reshape(DEC_BATCH, DEPTH, 1, D_RNN), state_lru_b.reshape(DEC_BATCH, DEPTH, 1, D_RNN))

    w_in_b = _reorder_w_in(w_in)
    w_gate_b, w_branch_b, w_out_b = w_gate.astype(bf16), w_branch.astype(bf16), w_out.astype(bf16)
    w_up_b, w_down_b = ffn_w_up.astype(bf16), ffn_w_down.astype(bf16)

    new = {k: None for k in ("k", "v", "lru_f", "lru_b", "ret_f", "ret_b", "ssd_f", "ssd_b")}
    for l in range(DEPTH):
        mod = mod_all[l].reshape(N_COND_PAD, 1, N_MOD * D_MODEL)
        g1 = norm1_g[l][None, :]
        if l == 0:
            proj, xn = in_projection(xs0, 0, 0, N_SAMPLE_TILES, g1, mod, w_in_b, l)
            proj, xn = in_projection(xp0, 0, N_SAMPLE_TILES, n_ptiles, g1, mod, w_in_b, l, (proj, xn))
        else:
            proj, xn = in_projection(x, 0, 0, N_TILES, g1, mod, w_in_b, l)

        w4 = jnp.concatenate([lru_wa[l, 0], lru_wx[l, 0], lru_wa[l, 1], lru_wx[l, 1]], axis=-1).astype(bf16)
        b4 = jnp.concatenate([lru_ba[l, 0].reshape(LRU_BLOCKS, 1, LRU_BS), lru_bx[l, 0].reshape(LRU_BLOCKS, 1, LRU_BS),
                              lru_ba[l, 1].reshape(LRU_BLOCKS, 1, LRU_BS), lru_bx[l, 1].reshape(LRU_BLOCKS, 1, LRU_BS)],
                             axis=-1)
        lru_args = (lru_conv_w[l], lru_conv_b[l][None, :], w4, b4, lru_lambda[l])
        y_lru = lru_branch(proj, (None,), 0, DEC_BATCH, DEC_SEQ, *lru_args, lru_init, l)
        y_lru, new["lru_f"], new["lru_b"] = lru_branch(
            proj, (y_lru, new["lru_f"], new["lru_b"]), T_SAMPLE, BATCH, SEQ, *lru_args, None, l)

        gn = ret_gn_g[l][None, :]
        y_ret = retention_branch(proj, (None,), 0, DEC_BATCH, DEC_SEQ, ret_la, gn, (state_ret_f, state_ret_b), l)
        y_ret, new["ret_f"], new["ret_b"] = retention_branch(
            proj, (y_ret, new["ret_f"], new["ret_b"]), T_SAMPLE, BATCH, SEQ, ret_la, gn, None, l)

        prm = jnp.stack([_group_lanes(ssd_dt_bias[l]), _group_lanes(ssd_a_log[l])], axis=1)
        prm = jnp.concatenate([prm, jnp.zeros((SSD_GROUPS, V7X_SUBLANES - 2, V7X_LANES), f32)], axis=1)
        dvec = jnp.repeat(ssd_d[l], SSD_P)[None, :]
        ssd_args = (ssd_conv_w[l], ssd_conv_b[l][None, :], prm, dvec, ssd_norm_g[l][None, :])
        y_ssd = ssd_branch(proj, (None,), 0, DEC_BATCH, DEC_SEQ, *ssd_args, (state_ssd_f, state_ssd_b), l)
        y_ssd, new["ssd_f"], new["ssd_b"] = ssd_branch(
            proj, (y_ssd, new["ssd_f"], new["ssd_b"]), T_SAMPLE, BATCH, SEQ, *ssd_args, None, l)

        qg = na_q_g[l][None, :]
        kg = na_k_g[l][None, :]
        bias, valid = _na_tables(na_rpb[l])
        y_na = neighbourhood_attention(proj, cache_k, cache_v, l, qg, kg, cos, sin, bias, valid)
        y_na, new["k"], new["v"] = context_attention(proj, (y_na, new["k"], new["v"]), T_SAMPLE, qg, kg, l)

        merged = merge_branches(xn, (y_lru, y_ret, y_ssd, y_na), w_gate_b, b_gate[l][None, :], w_branch_b, l)
        if l == 0:
            x = residual_projection(merged, w_out_b, l, xs0, mod, 2, 0, N_SAMPLE_TILES)
            x = residual_projection(merged, w_out_b, l, xp0, mod, 2, N_SAMPLE_TILES, n_ptiles, x_tile0=0, buf=x)
        else:
            x = residual_projection(merged, w_out_b, l, x, mod, 2)
        hmid = ffn_up(x, norm2_g[l][None, :], mod, w_up_b, l, ffn_conv_w[l], ffn_conv_b[l][None, :])
        if l < DEPTH - 1:
            x = residual_projection(hmid, w_down_b, l, x, mod, 5)
        else:
            y_sample = residual_projection(hmid, w_down_b, l, x, mod, 5, 0, N_SAMPLE_TILES, out_rows=T_SAMPLE)
            y_prompt = residual_projection(hmid, w_down_b, l, x, mod, 5, N_SAMPLE_TILES, n_ptiles,
                                           out_tile0=0, out_rows=T_PROMPT)

    kv_shape = (BATCH, DEPTH, SEQ, NA_HEADS, NA_HD)
    return (y_prompt.reshape(BATCH, SEQ, D_MODEL), y_sample.reshape(DEC_BATCH, DEC_SEQ, D_MODEL),
            new["k"].reshape(kv_shape), new["v"].reshape(kv_shape),
            new["lru_f"].reshape(BATCH, DEPTH, D_RNN), new["lru_b"].reshape(BATCH, DEPTH, D_RNN),
            new["ret_f"], new["ret_b"], new["ssd_f"], new["ssd_b"])
```

```python
import functools
import math

import jax
import jax.numpy as jnp
import numpy as np
from jax import lax
from jax.experimental import pallas as pl
from jax.experimental.pallas import tpu as pltpu

D_MODEL = 2048
BATCH = 16
SEQ = 256
DEPTH = 2
DEC_BATCH = 8
DEC_SEQ = 1024
PAST_LEN = 256
GRID_W = 64
EPS = 1e-6
N_BRANCH = 4
BRANCH_W = 1024
N_MOD = 6
D_RNN = 1024
LRU_BLOCKS = 8
LRU_BS = D_RNN // LRU_BLOCKS
LRU_CONV = 4
LRU_C = 8.0
RET_HEADS = 4
RET_DK = 128
RET_DV = 256
SSD_HEADS = 16
SSD_P = 64
SSD_N = 128
SSD_GROUPS = 2
SSD_CONV = 4
D_SSD = SSD_HEADS * SSD_P
SSD_CONV_CH = D_SSD + 2 * SSD_GROUPS * SSD_N
NA_HEADS = 8
NA_HD = 128
NA_W = NA_HEADS * NA_HD
NA_WR = 8
NA_WC = 16
ROPE_BASE = 10000.0
D_FF = 5632
FFN_CONV = 3
IN_SIZES = (D_RNN, D_RNN,
            RET_HEADS * RET_DK, RET_HEADS * RET_DK, RET_HEADS * RET_DV, RET_HEADS * RET_DV,
            D_SSD, SSD_CONV_CH, 2 * SSD_HEADS,
            NA_W, NA_W, NA_W)
D_IN = sum(IN_SIZES)
IN_OFFSETS = tuple(int(s) for s in np.cumsum(IN_SIZES)[:-1])

V7X_LANES = 128
V7X_SUBLANES = 8
V7X_VMEM_LIMIT_BYTES = 56 * 1024 * 1024

TM = 1024
T_SAMPLE = DEC_BATCH * DEC_SEQ
T_PROMPT = BATCH * SEQ
T_ALL = T_SAMPLE + T_PROMPT
N_SAMPLE_TILES = T_SAMPLE // TM
N_TILES = T_ALL // TM
N_COND = DEC_BATCH + 1
N_COND_PAD = 16

SSD_HPG = SSD_HEADS // SSD_GROUPS
OFF_LRU_X = 0
OFF_LRU_G = OFF_LRU_X + D_RNN
OFF_RET_Q = OFF_LRU_G + D_RNN
OFF_RET_K = OFF_RET_Q + RET_HEADS * RET_DK
OFF_RET_V = OFF_RET_K + RET_HEADS * RET_DK
OFF_RET_G = OFF_RET_V + RET_HEADS * RET_DV
OFF_SSD_Z = OFF_RET_G + RET_HEADS * RET_DV
OFF_SSD_XBC = OFF_SSD_Z + D_SSD
OFF_NA_Q = OFF_SSD_XBC + SSD_CONV_CH
OFF_NA_K = OFF_NA_Q + NA_W
OFF_NA_V = OFF_NA_K + NA_W
OFF_SSD_DT = OFF_NA_V + NA_W
PROJ_TN = 512
D_IN_PAD = -(-(OFF_SSD_DT + SSD_GROUPS * V7X_LANES) // PROJ_TN) * PROJ_TN
N_HEAD_TILES = OFF_NA_Q // PROJ_TN

RET_CHUNK = 256
SSD_CHUNK = 256

f32 = jnp.float32
bf16 = jnp.bfloat16

_ARB = "arbitrary"


def _params(n_axes):
    return pltpu.CompilerParams(dimension_semantics=(_ARB,) * n_axes,
                                vmem_limit_bytes=V7X_VMEM_LIMIT_BYTES)


def _mod_spec(k, tile0=0):
    return pl.BlockSpec((1, 1, D_MODEL), lambda i, j: (jnp.minimum(tile0 + i, N_SAMPLE_TILES), 0, k))


def _dot(a, b):
    return jnp.dot(a, b, preferred_element_type=f32)


def _dot_nt(a, b):
    return lax.dot_general(a, b, (((1,), (1,)), ((), ())), preferred_element_type=f32)


def _sigmoid(x):
    return 0.5 * jnp.tanh(0.5 * x) + 0.5


def _silu(x):
    return x * _sigmoid(x)


def _gelu_tanh(x):
    return 0.5 * x * (1.0 + jnp.tanh(math.sqrt(2.0 / math.pi) * (x + 0.044715 * (x * x * x))))


def _softplus(x):
    return jnp.maximum(x, 0.0) + jnp.log1p(jnp.exp(-jnp.abs(x)))


def _drop_refs(body, n, *refs):
    body(*refs[n:])


def _inplace_call(body, bufs, grid, in_specs, args, out_specs, out_shape, scratch, name):
    held = [(k, b) for k, b in enumerate(bufs) if b is not None]
    if held:
        body = functools.partial(_drop_refs, body, len(held))
        in_specs = [pl.BlockSpec(memory_space=pl.ANY)] * len(held) + list(in_specs)
        args = [b for _, b in held] + list(args)
    aliases = {pos: k for pos, (k, _) in enumerate(held)}
    return pl.pallas_call(
        body, grid=grid, in_specs=in_specs, out_specs=out_specs, out_shape=out_shape,
        scratch_shapes=scratch, input_output_aliases=aliases,
        compiler_params=_params(len(grid)), name=name,
    )(*args)


_BRANCH_BUF = jax.ShapeDtypeStruct((T_ALL, BRANCH_W), bf16)


ADA_TN = 1024


def _ada_kernel(c_ref, w_ref, b_ref, o_ref):
    c = _silu(c_ref[...]).astype(bf16)
    o_ref[0] = _dot(c, w_ref[0].astype(bf16)) + b_ref[0]


def ada_modulation(cond, w_ada, b_ada):
    n = N_MOD * D_MODEL
    return pl.pallas_call(
        _ada_kernel,
        grid=(DEPTH, n // ADA_TN),
        in_specs=[pl.BlockSpec((N_COND_PAD, D_MODEL), lambda l, j: (0, 0)),
                  pl.BlockSpec((1, D_MODEL, ADA_TN), lambda l, j: (l, 0, j)),
                  pl.BlockSpec((1, 1, ADA_TN), lambda l, j: (l, 0, j))],
        out_specs=pl.BlockSpec((1, N_COND_PAD, ADA_TN), lambda l, j: (l, 0, j)),
        out_shape=jax.ShapeDtypeStruct((DEPTH, N_COND_PAD, n), f32),
        compiler_params=_params(2),
        name="ada_modulation",
    )(cond, w_ada, b_ada.reshape(DEPTH, 1, n))


NORM_ROWS = 128


def _modulated_norm(x_ref, g_ref, sc_ref, sh_ref, xn_ref):
    g = g_ref[...]
    sc = 1.0 + sc_ref[0]
    sh = sh_ref[0]

    def body(r, carry):
        rows = pl.ds(pl.multiple_of(r * NORM_ROWS, NORM_ROWS), NORM_ROWS)
        x = x_ref[rows, :]
        y = x * lax.rsqrt(jnp.mean(x * x, axis=-1, keepdims=True) + EPS)
        xn_ref[rows, :] = ((y * g) * sc + sh).astype(bf16)
        return carry

    lax.fori_loop(0, TM // NORM_ROWS, body, 0)


def _in_proj_kernel(x_ref, g_ref, sc_ref, sh_ref, wh_ref, wt_ref, o_ref, xn_ref):
    j = pl.program_id(1)

    @pl.when(j == 0)
    def _():
        _modulated_norm(x_ref, g_ref, sc_ref, sh_ref, xn_ref)

    @pl.when(j < N_HEAD_TILES)
    def _():
        o_ref[...] = _dot(xn_ref[...], wh_ref[0]).astype(bf16)

    @pl.when(j >= N_HEAD_TILES)
    def _():
        o_ref[...] = _dot(xn_ref[...], wt_ref[0]).astype(bf16)


def in_projection(x, x_tile0, tile0, n_tiles, g, mod, w_head, w_tail, layer, bufs=(None, None)):
    n = D_IN_PAD
    n_tail = w_tail.shape[2] // PROJ_TN
    return _inplace_call(
        _in_proj_kernel, bufs, (n_tiles, n // PROJ_TN),
        [pl.BlockSpec((TM, D_MODEL), lambda i, j: (x_tile0 + i, 0)),
         pl.BlockSpec((1, D_MODEL), lambda i, j: (0, 0)),
         _mod_spec(1, tile0), _mod_spec(0, tile0),
         pl.BlockSpec((1, D_MODEL, PROJ_TN), lambda i, j: (layer, 0, jnp.minimum(j, N_HEAD_TILES - 1))),
         pl.BlockSpec((1, D_MODEL, PROJ_TN),
                      lambda i, j: (layer, 0, jnp.clip(j - N_HEAD_TILES, 0, n_tail - 1)))],
        [x, g, mod, mod, w_head, w_tail],
        [pl.BlockSpec((TM, PROJ_TN), lambda i, j: (tile0 + i, j)),
         pl.BlockSpec((TM, D_MODEL), lambda i, j: (tile0 + i, 0))],
        [jax.ShapeDtypeStruct((T_ALL, n), bf16), jax.ShapeDtypeStruct((T_ALL, D_MODEL), bf16)],
        [], "in_projection")


MERGE_TN = 256


def _merge_kernel(xn_ref, *refs):
    y_refs, wg_refs, bg_refs = refs[0:4], refs[4:8], refs[8:12]
    wb_ref, o_ref = refs[12], refs[13]
    xn = xn_ref[...]
    acc = None
    for n in range(N_BRANCH):
        gate = _sigmoid(_dot(xn, wg_refs[n][0]) + bg_refs[n][...])
        term = gate * _dot(y_refs[n][...], wb_ref[0, n])
        acc = term if acc is None else acc + term
    o_ref[...] = acc.astype(bf16)


def merge_branches(xn, ys, w_gate, b_gate, w_branch, layer):
    nj = D_MODEL // MERGE_TN
    y_spec = pl.BlockSpec((TM, BRANCH_W), lambda i, j: (i, 0))
    wg_specs = [pl.BlockSpec((1, D_MODEL, MERGE_TN), lambda i, j, n=n: (layer, 0, n * nj + j))
                for n in range(N_BRANCH)]
    bg_specs = [pl.BlockSpec((1, MERGE_TN), lambda i, j, n=n: (0, n * nj + j)) for n in range(N_BRANCH)]
    return pl.pallas_call(
        _merge_kernel,
        grid=(N_TILES, nj),
        in_specs=[pl.BlockSpec((TM, D_MODEL), lambda i, j: (i, 0)),
                  y_spec, y_spec, y_spec, y_spec, *wg_specs, *bg_specs,
                  pl.BlockSpec((1, N_BRANCH, BRANCH_W, MERGE_TN), lambda i, j: (layer, 0, 0, j))],
        out_specs=pl.BlockSpec((TM, MERGE_TN), lambda i, j: (i, j)),
        out_shape=jax.ShapeDtypeStruct((T_ALL, D_MODEL), bf16),
        compiler_params=_params(2),
        name="merge_branches",
    )(xn, *ys, *([w_gate] * N_BRANCH), *([b_gate] * N_BRANCH), w_branch)


RES_TN = 512


def _residual_kernel(a_ref, w_ref, x_ref, gv_ref, o_ref):
    o_ref[...] = x_ref[...] + gv_ref[0] * _dot(a_ref[...], w_ref[0])


def residual_projection(a, w, layer, x, mod, k_mod, tile0=0, n_tiles=N_TILES, x_tile0=None, out_tile0=None,
                        out_rows=T_ALL, buf=None):
    kdim = a.shape[1]
    x_tile0 = tile0 if x_tile0 is None else x_tile0
    out_tile0 = tile0 if out_tile0 is None else out_tile0
    return _inplace_call(
        _residual_kernel, (buf,), (n_tiles, D_MODEL // RES_TN),
        [pl.BlockSpec((TM, kdim), lambda i, j: (tile0 + i, 0)),
         pl.BlockSpec((1, kdim, RES_TN), lambda i, j: (layer, 0, j)),
         pl.BlockSpec((TM, RES_TN), lambda i, j: (x_tile0 + i, j)),
         pl.BlockSpec((1, 1, RES_TN),
                      lambda i, j: (jnp.minimum(tile0 + i, N_SAMPLE_TILES), 0, k_mod * (D_MODEL // RES_TN) + j))],
        [a, w, x, mod],
        pl.BlockSpec((TM, RES_TN), lambda i, j: (out_tile0 + i, j)),
        jax.ShapeDtypeStruct((out_rows, D_MODEL), f32), [], "residual_projection")


FFN_TN = 512


def _ffn_up_kernel(x_ref, g_ref, sc_ref, sh_ref, wa_ref, wv_ref, cw_ref, cb_ref, o_ref, xn_ref, pad_ref):
    i = pl.program_id(0)

    @pl.when(pl.program_id(1) == 0)
    def _():
        _modulated_norm(x_ref, g_ref, sc_ref, sh_ref, xn_ref)
        zeros = jnp.zeros((V7X_SUBLANES, FFN_TN), f32)
        pad_ref[pl.ds(0, V7X_SUBLANES), :] = zeros
        pad_ref[pl.ds(V7X_SUBLANES + TM, V7X_SUBLANES), :] = zeros

    xn = xn_ref[...]
    a = _dot(xn, wa_ref[0])
    pad_ref[pl.ds(V7X_SUBLANES, TM), :] = a
    seq_len = jnp.where(i < N_SAMPLE_TILES, DEC_SEQ, SEQ)
    pos = lax.broadcasted_iota(jnp.int32, (TM, 1), 0) & (seq_len - 1)
    prev = jnp.where(pos == 0, 0.0, pad_ref[pl.ds(V7X_SUBLANES - 1, TM), :])
    nxt = jnp.where(pos == seq_len - 1, 0.0, pad_ref[pl.ds(V7X_SUBLANES + 1, TM), :])
    cw = cw_ref[...]
    conv = cb_ref[...] + prev * cw[0:1, :] + a * cw[1:2, :] + nxt * cw[2:3, :]
    o_ref[...] = (_gelu_tanh(conv) * _dot(xn, wv_ref[0])).astype(bf16)


def ffn_up(x, g, mod, w_up, layer, conv_w, conv_b):
    nj = D_FF // FFN_TN
    return pl.pallas_call(
        _ffn_up_kernel,
        grid=(N_TILES, nj),
        in_specs=[pl.BlockSpec((TM, D_MODEL), lambda i, j: (i, 0)),
                  pl.BlockSpec((1, D_MODEL), lambda i, j: (0, 0)),
                  _mod_spec(4), _mod_spec(3),
                  pl.BlockSpec((1, D_MODEL, FFN_TN), lambda i, j: (layer, 0, j)),
                  pl.BlockSpec((1, D_MODEL, FFN_TN), lambda i, j: (layer, 0, nj + j)),
                  pl.BlockSpec((FFN_CONV, FFN_TN), lambda i, j: (0, j)),
                  pl.BlockSpec((1, FFN_TN), lambda i, j: (0, j))],
        out_specs=pl.BlockSpec((TM, FFN_TN), lambda i, j: (i, j)),
        out_shape=jax.ShapeDtypeStruct((T_ALL, D_FF), bf16),
        scratch_shapes=[pltpu.VMEM((TM, D_MODEL), bf16),
                        pltpu.VMEM((TM + 2 * V7X_SUBLANES, FFN_TN), f32)],
        compiler_params=_params(2),
        name="ffn_up",
    )(x, g, mod, mod, w_up, w_up, conv_w, conv_b)


LRU_CB = 512
LRU_NB = LRU_CB // LRU_BS
SCAN_UNROLL = 8


def _lru_kernel(*refs, seq_len, has_init):
    if has_init:
        (x_ref, g_ref, cw_ref, cb_ref, w4_ref, b4_ref, lam_ref, h0f_ref, h0b_ref,
         y_ref, pad_ref, af_ref, uf_ref, ab_ref, ub_ref, hf_ref, hb_ref) = refs
    else:
        (x_ref, g_ref, cw_ref, cb_ref, w4_ref, b4_ref, lam_ref,
         y_ref, sf_ref, sb_ref, pad_ref, af_ref, uf_ref, ab_ref, ub_ref, hf_ref, hb_ref) = refs
    L = seq_len
    S = V7X_SUBLANES
    zeros = jnp.zeros((S, LRU_CB), f32)
    pad_ref[pl.ds(0, S), :] = zeros
    pad_ref[pl.ds(S + L, S), :] = zeros
    pad_ref[pl.ds(S, L), :] = x_ref[...].astype(f32)
    cw = cw_ref[...]
    xc = cb_ref[...]
    for kk in range(LRU_CONV):
        xc = xc + pad_ref[pl.ds(S + kk - LRU_CONV // 2, L), :] * cw[kk:kk + 1, :]
    sp = _softplus(-lam_ref[...])
    for n in range(LRU_NB):
        cols = slice(n * LRU_BS, (n + 1) * LRU_BS)
        xcn = xc[:, cols]
        z = _dot(xcn.astype(bf16), w4_ref[n]) + b4_ref[n]
        for d, (a_ref, u_ref) in enumerate(((af_ref, uf_ref), (ab_ref, ub_ref))):
            r = _sigmoid(z[:, (2 * d) * LRU_BS:(2 * d + 1) * LRU_BS])
            ig = _sigmoid(z[:, (2 * d + 1) * LRU_BS:(2 * d + 2) * LRU_BS])
            log_a = (-LRU_C) * r * sp[d:d + 1, cols]
            a = jnp.exp(log_a)
            a_ref[:, cols] = a
            u_ref[:, cols] = jnp.sqrt(-jnp.tanh(log_a) * (a * a + 1.0)) * (ig * xcn)

    if has_init:
        h0f = h0f_ref[0, 0]
        h0b = h0b_ref[0, 0]
    else:
        h0f = jnp.zeros((1, LRU_CB), f32)
        h0b = jnp.zeros((1, LRU_CB), f32)

    def step(t, carry):
        hf, hb = carry
        tb = L - 1 - t
        hf = af_ref[pl.ds(t, 1), :] * hf + uf_ref[pl.ds(t, 1), :]
        hb = ab_ref[pl.ds(tb, 1), :] * hb + ub_ref[pl.ds(tb, 1), :]
        hf_ref[pl.ds(t, 1), :] = hf
        hb_ref[pl.ds(tb, 1), :] = hb
        return hf, hb

    hf, hb = lax.fori_loop(0, L, step, (h0f, h0b), unroll=SCAN_UNROLL)
    if not has_init:
        sf_ref[0, 0] = hf
        sb_ref[0, 0] = hb
    y_ref[...] = ((hf_ref[...] + hb_ref[...]) * _gelu_tanh(g_ref[...].astype(f32))).astype(bf16)


def lru_branch(proj, bufs, row0, n_seq, seq_len, cw, cb, w4, b4, lam, init, layer):
    has_init = init is not None
    ncb = D_RNN // LRU_CB
    rb0 = row0 // seq_len
    in_specs = [pl.BlockSpec((seq_len, LRU_CB), lambda s, c: (rb0 + s, OFF_LRU_X // LRU_CB + c)),
                pl.BlockSpec((seq_len, LRU_CB), lambda s, c: (rb0 + s, OFF_LRU_G // LRU_CB + c)),
                pl.BlockSpec((LRU_CONV, LRU_CB), lambda s, c: (0, c)),
                pl.BlockSpec((1, LRU_CB), lambda s, c: (0, c)),
                pl.BlockSpec((LRU_NB, LRU_BS, 4 * LRU_BS), lambda s, c: (c, 0, 0)),
                pl.BlockSpec((LRU_NB, 1, 4 * LRU_BS), lambda s, c: (c, 0, 0)),
                pl.BlockSpec((2, LRU_CB), lambda s, c: (0, c))]
    args = [proj, proj, cw, cb, w4, b4, lam]
    state_spec = pl.BlockSpec((1, 1, 1, LRU_CB), lambda s, c: (s, layer, 0, c))
    y_spec = pl.BlockSpec((seq_len, LRU_CB), lambda s, c: (rb0 + s, c))
    if has_init:
        in_specs += [state_spec, state_spec]
        args += [init[0], init[1]]
        out_specs, out_shape = y_spec, _BRANCH_BUF
    else:
        st = jax.ShapeDtypeStruct((n_seq, DEPTH, 1, D_RNN), f32)
        out_specs, out_shape = [y_spec, state_spec, state_spec], [_BRANCH_BUF, st, st]
    scratch = [pltpu.VMEM((seq_len + 2 * V7X_SUBLANES, LRU_CB), f32)] + \
              [pltpu.VMEM((seq_len, LRU_CB), f32) for _ in range(6)]
    return _inplace_call(functools.partial(_lru_kernel, seq_len=seq_len, has_init=has_init), bufs,
                         (n_seq, ncb), in_specs, args, out_specs, out_shape, scratch, "lru_branch")


def _ret_kernel(*refs, seq_len, has_init):
    if has_init:
        (la_ref, q_ref, k_ref, v_ref, g_ref, gn_ref, s0f_ref, s0b_ref, y_ref, acc_ref) = refs
    else:
        (la_ref, q_ref, k_ref, v_ref, g_ref, gn_ref, y_ref, sf_ref, sb_ref, acc_ref) = refs
    h = pl.program_id(1)
    T = min(RET_CHUNK, seq_len)
    nc = seq_len // T
    la_f = la_ref[h, 0]
    la_b = la_ref[h, 1]
    tt = lax.broadcasted_iota(jnp.int32, (T, T), 0)
    ss = lax.broadcasted_iota(jnp.int32, (T, T), 1)
    diff = (tt - ss).astype(f32)
    dsum = (jnp.where(tt >= ss, jnp.exp(la_f * diff), 0.0)
            + jnp.where(ss >= tt, jnp.exp(-la_b * diff), 0.0))
    tcol = lax.broadcasted_iota(jnp.int32, (T, 1), 0).astype(f32)
    scale = RET_DK ** -0.5

    def chunk(c):
        rows = pl.ds(c * T, T)
        q = q_ref[rows, :].astype(f32)
        ks = k_ref[rows, :].astype(f32) * scale
        v = v_ref[rows, :].astype(f32)
        return rows, q, ks, v

    s_f = s0f_ref[0, 0, 0] if has_init else None
    for c in range(nc):
        rows, q, ks, v = chunk(c)
        scores = _dot_nt(q.astype(bf16), ks.astype(bf16)) * dsum
        y = _dot(scores.astype(bf16), v.astype(bf16))
        if s_f is not None:
            y = y + _dot((q * jnp.exp(la_f * (tcol + 1.0))).astype(bf16), s_f.astype(bf16))
        acc_ref[rows, :] = y
        if c < nc - 1 or not has_init:
            upd = _dot(ks.T.astype(bf16), (v * jnp.exp(la_f * (T - 1.0 - tcol))).astype(bf16))
            s_f = upd if s_f is None else jnp.exp(la_f * T) * s_f + upd
    s_b = s0b_ref[0, 0, 0] if has_init else None
    for c in reversed(range(nc)):
        rows, q, ks, v = chunk(c)
        if s_b is not None:
            acc_ref[rows, :] += _dot((q * jnp.exp(la_b * (T - tcol))).astype(bf16), s_b.astype(bf16))
        if c > 0 or not has_init:
            upd = _dot(ks.T.astype(bf16), (v * jnp.exp(la_b * tcol)).astype(bf16))
            s_b = upd if s_b is None else jnp.exp(la_b * T) * s_b + upd
    if not has_init:
        sf_ref[0, 0, 0] = s_f
        sb_ref[0, 0, 0] = s_b
    y = acc_ref[...]
    mu = jnp.mean(y, axis=-1, keepdims=True)
    yc = y - mu
    var = jnp.mean(yc * yc, axis=-1, keepdims=True)
    y = yc * lax.rsqrt(var + EPS) * gn_ref[...]
    y_ref[...] = (y * _silu(g_ref[...].astype(f32))).astype(bf16)


def retention_branch(proj, bufs, row0, n_seq, seq_len, la, gn, init, layer):
    has_init = init is not None
    rb0 = row0 // seq_len
    in_specs = [pl.BlockSpec(memory_space=pltpu.SMEM),
                pl.BlockSpec((seq_len, RET_DK), lambda s, h: (rb0 + s, OFF_RET_Q // RET_DK + h)),
                pl.BlockSpec((seq_len, RET_DK), lambda s, h: (rb0 + s, OFF_RET_K // RET_DK + h)),
                pl.BlockSpec((seq_len, RET_DV), lambda s, h: (rb0 + s, OFF_RET_V // RET_DV + h)),
                pl.BlockSpec((seq_len, RET_DV), lambda s, h: (rb0 + s, OFF_RET_G // RET_DV + h)),
                pl.BlockSpec((1, RET_DV), lambda s, h: (0, h))]
    args = [la, proj, proj, proj, proj, gn]
    state_spec = pl.BlockSpec((1, 1, 1, RET_DK, RET_DV), lambda s, h: (s, layer, h, 0, 0))
    y_spec = pl.BlockSpec((seq_len, RET_DV), lambda s, h: (rb0 + s, h))
    if has_init:
        in_specs += [state_spec, state_spec]
        args += [init[0], init[1]]
        out_specs, out_shape = y_spec, _BRANCH_BUF
    else:
        st = jax.ShapeDtypeStruct((n_seq, DEPTH, RET_HEADS, RET_DK, RET_DV), f32)
        out_specs, out_shape = [y_spec, state_spec, state_spec], [_BRANCH_BUF, st, st]
    return _inplace_call(functools.partial(_ret_kernel, seq_len=seq_len, has_init=has_init), bufs,
                         (n_seq, RET_HEADS), in_specs, args, out_specs, out_shape,
                         [pltpu.VMEM((seq_len, RET_DV), f32)], "retention_branch")


SSD_GW = SSD_HPG * SSD_P


def _split3(x):
    h1 = x.astype(bf16)
    r1 = x - h1.astype(f32)
    h2 = r1.astype(bf16)
    h3 = (r1 - h2.astype(f32)).astype(bf16)
    return h1, h2, h3


def _dot_exact_rhs(m, x):
    h1, h2, h3 = _split3(x)
    return _dot(m, h1) + _dot(m, h2) + _dot(m, h3)


def _dot_exact_lhs(x, m):
    h1, h2, h3 = _split3(x)
    return _dot(h1, m) + _dot(h2, m) + _dot(h3, m)


def _conv4_silu(src_ref, pad_ref, cw_ref, cb_ref, L, width):
    S = V7X_SUBLANES
    zeros = jnp.zeros((S, width), f32)
    pad_ref[pl.ds(0, S), pl.ds(0, width)] = zeros
    pad_ref[pl.ds(S + L, S), pl.ds(0, width)] = zeros
    pad_ref[pl.ds(S, L), pl.ds(0, width)] = src_ref[...].astype(f32)
    cw = cw_ref[...]
    out = cb_ref[...]
    for kk in range(SSD_CONV):
        out = out + pad_ref[pl.ds(S + kk - SSD_CONV // 2, L), pl.ds(0, width)] * cw[kk:kk + 1, :]
    return _silu(out)


def _heads_to_lanes(s_ref, st_ref):
    for hh in range(SSD_HPG):
        st_ref[:, pl.ds(hh * SSD_P, SSD_P)] = s_ref[0, 0, hh]
    return st_ref[...]


def _ssd_kernel(*refs, seq_len, has_init):
    if has_init:
        (z_ref, x_ref, b_ref, c_ref, dt_ref, cwx_ref, cbx_ref, cwb_ref, cbb_ref, cwc_ref, cbc_ref,
         prm_ref, dvec_ref, ng_ref, s0f_ref, s0b_ref,
         y_ref, pad_ref, xs_ref, bs_ref, cs_ref, acc_ref, rb_ref, yn_ref, st_ref) = refs
    else:
        (z_ref, x_ref, b_ref, c_ref, dt_ref, cwx_ref, cbx_ref, cwb_ref, cbb_ref, cwc_ref, cbc_ref,
         prm_ref, dvec_ref, ng_ref,
         y_ref, sf_ref, sb_ref, pad_ref, xs_ref, bs_ref, cs_ref, acc_ref, rb_ref, yn_ref, st_ref) = refs
    g = pl.program_id(1)
    L = seq_len
    T = min(SSD_CHUNK, L)
    nc = L // T
    H = SSD_HPG
    xs_ref[...] = _conv4_silu(x_ref, pad_ref, cwx_ref, cbx_ref, L, SSD_GW)
    bs_ref[...] = _conv4_silu(b_ref, pad_ref, cwb_ref, cbb_ref, L, SSD_N)
    cs_ref[...] = _conv4_silu(c_ref, pad_ref, cwc_ref, cbc_ref, L, SSD_N)
    prm = prm_ref[0]
    a_neg = -jnp.exp(prm[1:2, :])
    tt = lax.broadcasted_iota(jnp.int32, (T, T), 0)
    ss = lax.broadcasted_iota(jnp.int32, (T, T), 1)
    lower = tt >= ss
    upper = ss >= tt
    tri_l = jnp.where(lower, 1.0, 0.0).astype(bf16)
    tri_u = jnp.where(upper, 1.0, 0.0).astype(bf16)
    er = lax.broadcasted_iota(jnp.int32, (V7X_LANES, SSD_GW), 0)
    ec = lax.broadcasted_iota(jnp.int32, (V7X_LANES, SSD_GW), 1) // SSD_P
    exp_f = jnp.where(er == ec, 1.0, 0.0).astype(bf16)
    exp_b = jnp.where(er == ec + H, 1.0, 0.0).astype(bf16)

    def expand(w, e):
        hi = w.astype(bf16)
        lo = (w - hi.astype(f32)).astype(bf16)
        return _dot(hi, e) + _dot(lo, e)

    def chunk_terms(c):
        rows = pl.ds(c * T, T)
        dt = _softplus(dt_ref[rows, :].astype(f32) + prm[0:1, :])
        da = dt * a_neg
        return rows, dt, da

    s_f = _heads_to_lanes(s0f_ref, st_ref) if has_init else None
    for c in range(nc):
        rows, dt, da = chunk_terms(c)
        cum = _dot_exact_rhs(tri_l, da)
        rsum = _dot_exact_rhs(tri_u, da)
        rb_ref[rows, :] = rsum
        da_t = da.T
        cum_t = _dot_exact_lhs(da_t, tri_u)
        rsum_t = _dot_exact_lhs(da_t, tri_l)
        dt_t = dt.T
        bmat = bs_ref[rows, :]
        cmat = cs_ref[rows, :]
        xmat = xs_ref[rows, :]
        gmat = _dot_nt(cmat.astype(bf16), bmat.astype(bf16))
        for hh in range(H):
            lf = jnp.where(lower, jnp.exp(cum[:, hh:hh + 1] - cum_t[hh:hh + 1, :]), 0.0)
            lb = jnp.where(upper, jnp.exp(rsum[:, H + hh:H + hh + 1] - rsum_t[H + hh:H + hh + 1, :]), 0.0)
            m = gmat * (lf * dt_t[hh:hh + 1, :] + lb * dt_t[H + hh:H + hh + 1, :])
            xh = xmat[:, hh * SSD_P:(hh + 1) * SSD_P]
            acc_ref[rows, pl.ds(hh * SSD_P, SSD_P)] = _dot(m.astype(bf16), xh.astype(bf16))
        ecum = jnp.exp(cum)
        if s_f is not None:
            acc_ref[rows, :] += _dot(cmat.astype(bf16), s_f.astype(bf16)) * expand(ecum, exp_f)
        if c < nc - 1 or not has_init:
            tail = jnp.exp(cum[T - 1:T, :] - cum) * dt
            xw = (xmat * expand(tail, exp_f)).astype(bf16)
            upd = _dot(bmat.T.astype(bf16), xw)
            if s_f is None:
                s_f = upd
            else:
                s_f = s_f * expand(jnp.broadcast_to(ecum[T - 1:T, :], (V7X_SUBLANES, V7X_LANES)),
                                   exp_f)[0:1, :] + upd
    s_b = _heads_to_lanes(s0b_ref, st_ref) if has_init else None
    for c in reversed(range(nc)):
        rows, dt, da = chunk_terms(c)
        rsum = rb_ref[rows, :]
        ers = jnp.exp(rsum)
        bmat = bs_ref[rows, :]
        cmat = cs_ref[rows, :]
        xmat = xs_ref[rows, :]
        if s_b is not None:
            acc_ref[rows, :] += _dot(cmat.astype(bf16), s_b.astype(bf16)) * expand(ers, exp_b)
        if c > 0 or not has_init:
            tail = jnp.exp(rsum[0:1, :] - rsum) * dt
            xw = (xmat * expand(tail, exp_b)).astype(bf16)
            upd = _dot(bmat.T.astype(bf16), xw)
            if s_b is None:
                s_b = upd
            else:
                s_b = s_b * expand(jnp.broadcast_to(ers[0:1, :], (V7X_SUBLANES, V7X_LANES)),
                                   exp_b)[0:1, :] + upd
    if not has_init:
        for hh in range(H):
            sf_ref[0, 0, hh] = s_f[:, hh * SSD_P:(hh + 1) * SSD_P]
            sb_ref[0, 0, hh] = s_b[:, hh * SSD_P:(hh + 1) * SSD_P]
    yg = (acc_ref[...] + xs_ref[...] * dvec_ref[...]) * _silu(z_ref[...].astype(f32))
    yn_ref[g] = yg

    @pl.when(g == SSD_GROUPS - 1)
    def _():
        ssq = None
        for gg in range(SSD_GROUPS):
            y = yn_ref[gg]
            s = jnp.sum(y * y, axis=-1, keepdims=True)
            ssq = s if ssq is None else ssq + s
        inv = lax.rsqrt(ssq * (1.0 / D_SSD) + EPS)
        for gg in range(SSD_GROUPS):
            cols = pl.ds(gg * SSD_GW, SSD_GW)
            y_ref[:, cols] = (yn_ref[gg] * inv * ng_ref[:, cols]).astype(bf16)


def ssd_branch(proj, bufs, row0, n_seq, seq_len, conv_w, conv_b, prm, dvec, ng, init, layer):
    has_init = init is not None
    rb0 = row0 // seq_len
    xoff = OFF_SSD_XBC
    boff = OFF_SSD_XBC + D_SSD
    coff = boff + SSD_GROUPS * SSD_N
    in_specs = [pl.BlockSpec((seq_len, SSD_GW), lambda s, g: (rb0 + s, OFF_SSD_Z // SSD_GW + g)),
                pl.BlockSpec((seq_len, SSD_GW), lambda s, g: (rb0 + s, xoff // SSD_GW + g)),
                pl.BlockSpec((seq_len, SSD_N), lambda s, g: (rb0 + s, boff // SSD_N + g)),
                pl.BlockSpec((seq_len, SSD_N), lambda s, g: (rb0 + s, coff // SSD_N + g)),
                pl.BlockSpec((seq_len, V7X_LANES), lambda s, g: (rb0 + s, OFF_SSD_DT // V7X_LANES + g)),
                pl.BlockSpec((SSD_CONV, SSD_GW), lambda s, g: (0, g)),
                pl.BlockSpec((1, SSD_GW), lambda s, g: (0, g)),
                pl.BlockSpec((SSD_CONV, SSD_N), lambda s, g: (0, D_SSD // SSD_N + g)),
                pl.BlockSpec((1, SSD_N), lambda s, g: (0, D_SSD // SSD_N + g)),
                pl.BlockSpec((SSD_CONV, SSD_N), lambda s, g: (0, D_SSD // SSD_N + SSD_GROUPS + g)),
                pl.BlockSpec((1, SSD_N), lambda s, g: (0, D_SSD // SSD_N + SSD_GROUPS + g)),
                pl.BlockSpec((1, V7X_SUBLANES, V7X_LANES), lambda s, g: (g, 0, 0)),
                pl.BlockSpec((1, SSD_GW), lambda s, g: (0, g)),
                pl.BlockSpec((1, D_SSD), lambda s, g: (0, 0))]
    args = [proj, proj, proj, proj, proj, conv_w, conv_b, conv_w, conv_b, conv_w, conv_b, prm, dvec, ng]
    state_spec = pl.BlockSpec((1, 1, SSD_HPG, SSD_N, SSD_P), lambda s, g: (s, layer, g, 0, 0))
    y_spec = pl.BlockSpec((seq_len, D_SSD), lambda s, g: (rb0 + s, 0))
    if has_init:
        in_specs += [state_spec, state_spec]
        args += [init[0], init[1]]
        out_specs, out_shape = y_spec, _BRANCH_BUF
    else:
        st = jax.ShapeDtypeStruct((n_seq, DEPTH, SSD_HEADS, SSD_N, SSD_P), f32)
        out_specs, out_shape = [y_spec, state_spec, state_spec], [_BRANCH_BUF, st, st]
    scratch = [pltpu.VMEM((seq_len + 2 * V7X_SUBLANES, SSD_GW), f32),
               pltpu.VMEM((seq_len, SSD_GW), f32),
               pltpu.VMEM((seq_len, SSD_N), f32),
               pltpu.VMEM((seq_len, SSD_N), f32),
               pltpu.VMEM((seq_len, SSD_GW), f32),
               pltpu.VMEM((seq_len, V7X_LANES), f32),
               pltpu.VMEM((SSD_GROUPS, seq_len, SSD_GW), f32),
               pltpu.VMEM((SSD_N, SSD_GW), f32)]
    return _inplace_call(functools.partial(_ssd_kernel, seq_len=seq_len, has_init=has_init), bufs,
                         (n_seq, SSD_GROUPS), in_specs, args, out_specs, out_shape, scratch, "ssd_branch")


def _head_rmsnorm(x, g):
    return x * lax.rsqrt(jnp.mean(x * x, axis=-1, keepdims=True) + EPS) * g


CTX_HB = 4
CTX_W = CTX_HB * NA_HD


def _ctx_attn_kernel(q_ref, k_ref, v_ref, qg_ref, kg_ref, y_ref, ko_ref, vo_ref):
    scale = NA_HD ** -0.5
    vo_ref[0, 0] = v_ref[...].astype(f32)
    for h in range(CTX_HB):
        cols = pl.ds(h * NA_HD, NA_HD)
        q = _head_rmsnorm(q_ref[:, cols].astype(f32), qg_ref[...])
        k = _head_rmsnorm(k_ref[:, cols].astype(f32), kg_ref[...])
        ko_ref[0, 0, :, cols] = k
        s = _dot_nt(q.astype(bf16), k.astype(bf16)) * scale
        p = jnp.exp(s - jnp.max(s, axis=-1, keepdims=True))
        o = _dot(p.astype(bf16), v_ref[:, cols]) / jnp.sum(p, axis=-1, keepdims=True)
        y_ref[:, cols] = o.astype(bf16)


def context_attention(proj, bufs, row0, qg, kg, layer):
    rb0 = row0 // SEQ
    spec = lambda off: pl.BlockSpec((SEQ, CTX_W), lambda s, hb: (rb0 + s, off // CTX_W + hb))
    gspec = pl.BlockSpec((1, NA_HD), lambda s, hb: (0, 0))
    kv_spec = pl.BlockSpec((1, 1, SEQ, CTX_W), lambda s, hb: (s, layer, 0, hb))
    kv_shape = jax.ShapeDtypeStruct((BATCH, DEPTH, SEQ, NA_W), f32)
    return _inplace_call(
        _ctx_attn_kernel, bufs, (BATCH, NA_HEADS // CTX_HB),
        [spec(OFF_NA_Q), spec(OFF_NA_K), spec(OFF_NA_V), gspec, gspec], [proj, proj, proj, qg, kg],
        [pl.BlockSpec((SEQ, CTX_W), lambda s, hb: (rb0 + s, hb)), kv_spec, kv_spec],
        [_BRANCH_BUF, kv_shape, kv_shape], [], "context_attention")


NA_ROWS = DEC_SEQ // GRID_W
NA_NK = NA_WR * GRID_W


def _na_row_start(r):
    return min(max(r - NA_WR // 2, 0), NA_ROWS - NA_WR)


def _rope(x, cos, sin_signed):
    lane = lax.broadcasted_iota(jnp.int32, x.shape, 1)
    quarter = NA_HD // 4
    swapped = jnp.where((lane & (2 * quarter - 1)) < quarter,
                        pltpu.roll(x, NA_HD - quarter, axis=1), pltpu.roll(x, quarter, axis=1))
    return x * cos + swapped * sin_signed


def _na_row_groups():
    groups, r = [], 0
    while r < NA_ROWS:
        n = 1
        while r + n < NA_ROWS and _na_row_start(r + n) == _na_row_start(r):
            n += 1
        groups.append((r, n, _na_row_start(r)))
        r += n
    return groups


def _na_attn_kernel(q_ref, k_ref, v_ref, kc_ref, vc_ref, qg_ref, kg_ref, cos_ref, sin_ref,
                    bias_ref, valid_ref, y_ref, qs_ref, ks_ref, s_ref, p_ref, oc_ref, w_ref):
    scale = NA_HD ** -0.5
    cos = cos_ref[...]
    sin = sin_ref[...]
    qs_ref[...] = _rope(_head_rmsnorm(q_ref[...].astype(f32), qg_ref[...]), cos, sin).astype(bf16)
    ks_ref[...] = _rope(_head_rmsnorm(k_ref[...].astype(f32), kg_ref[...]), cos, sin).astype(bf16)
    groups = _na_row_groups()
    valid = valid_ref[...] > 0.0
    for r0, n, rs in groups:
        rows = pl.ds(r0 * GRID_W, n * GRID_W)
        d0 = r0 - rs
        kw = ks_ref[pl.ds(rs * GRID_W, NA_NK), :]
        s = (_dot_nt(qs_ref[rows, :], kw) * scale).reshape(n, GRID_W, NA_NK) + bias_ref[0, d0:d0 + n]
        s_ref[rows, :] = jnp.where(valid[None], s, -1e30).reshape(n * GRID_W, NA_NK)
    s_ctx = _dot_nt(qs_ref[...], kc_ref[0, 0].astype(bf16)) * scale
    s_loc = s_ref[...]
    m = jnp.maximum(jnp.max(s_loc, axis=-1, keepdims=True), jnp.max(s_ctx, axis=-1, keepdims=True))
    p_loc = jnp.exp(s_loc - m)
    p_ctx = jnp.exp(s_ctx - m)
    w_ref[...] = 1.0 / (jnp.sum(p_loc, axis=-1, keepdims=True) + jnp.sum(p_ctx, axis=-1, keepdims=True))
    p_ref[...] = p_loc.astype(bf16)
    oc_ref[...] = _dot(p_ctx.astype(bf16), vc_ref[0, 0].astype(bf16))
    for r0, n, rs in groups:
        rows = pl.ds(r0 * GRID_W, n * GRID_W)
        vw = v_ref[pl.ds(rs * GRID_W, NA_NK), :]
        y_ref[rows, :] = ((_dot(p_ref[rows, :], vw) + oc_ref[rows, :]) * w_ref[rows, :]).astype(bf16)


def neighbourhood_attention(proj, cache_k, cache_v, layer, qg, kg, cos, sin, bias, valid):
    spec = lambda off: pl.BlockSpec((DEC_SEQ, NA_HD), lambda b, h: (b, off // NA_HD + h))
    cspec = pl.BlockSpec((1, 1, PAST_LEN, NA_HD), lambda b, h: (b, layer, 0, h))
    gspec = pl.BlockSpec((1, NA_HD), lambda b, h: (0, 0))
    tspec = pl.BlockSpec((DEC_SEQ, NA_HD), lambda b, h: (0, 0))
    return pl.pallas_call(
        _na_attn_kernel,
        grid=(DEC_BATCH, NA_HEADS),
        in_specs=[spec(OFF_NA_Q), spec(OFF_NA_K), spec(OFF_NA_V), cspec, cspec, gspec, gspec,
                  tspec, tspec,
                  pl.BlockSpec((1, NA_WR, GRID_W, NA_NK), lambda b, h: (h, 0, 0, 0)),
                  pl.BlockSpec((GRID_W, NA_NK), lambda b, h: (0, 0))],
        out_specs=pl.BlockSpec((DEC_SEQ, NA_HD), lambda b, h: (b, h)),
        out_shape=_BRANCH_BUF,
        scratch_shapes=[pltpu.VMEM((DEC_SEQ, NA_HD), bf16), pltpu.VMEM((DEC_SEQ, NA_HD), bf16),
                        pltpu.VMEM((DEC_SEQ, NA_NK), f32), pltpu.VMEM((DEC_SEQ, NA_NK), bf16),
                        pltpu.VMEM((DEC_SEQ, NA_HD), f32), pltpu.VMEM((DEC_SEQ, 1), f32)],
        compiler_params=_params(2), name="neighbourhood_attention",
    )(proj, proj, proj, cache_k, cache_v, qg, kg, cos, sin, bias, valid)


def _split_w_in(w_in):
    dt0, na0 = IN_OFFSETS[7], IN_OFFSETS[8]
    head = w_in[..., :dt0].astype(bf16)
    pieces = [w_in[..., na0:].astype(bf16)]
    dt = w_in[..., dt0:na0].astype(bf16)
    for gidx in range(SSD_GROUPS):
        pieces.append(dt[..., gidx * SSD_HPG:(gidx + 1) * SSD_HPG])
        pieces.append(dt[..., SSD_HEADS + gidx * SSD_HPG:SSD_HEADS + (gidx + 1) * SSD_HPG])
        pieces.append(jnp.zeros((DEPTH, D_MODEL, V7X_LANES - 2 * SSD_HPG), bf16))
    used = OFF_SSD_DT + SSD_GROUPS * V7X_LANES
    pieces.append(jnp.zeros((DEPTH, D_MODEL, D_IN_PAD - used), bf16))
    return head, jnp.concatenate(pieces, axis=-1)


def _group_lanes(v):
    rows = []
    for gidx in range(SSD_GROUPS):
        sl = slice(gidx * SSD_HPG, (gidx + 1) * SSD_HPG)
        rows.append(jnp.concatenate([v[0, sl], v[1, sl], jnp.zeros((V7X_LANES - 2 * SSD_HPG,), f32)]))
    return jnp.stack(rows)


def _rope_tables():
    t = np.arange(DEC_SEQ)
    quarter = NA_HD // 4
    inv = ROPE_BASE ** (-np.arange(quarter, dtype=np.float32) / quarter)
    ang_r = (t // GRID_W).astype(np.float32)[:, None] * inv
    ang_c = (t % GRID_W).astype(np.float32)[:, None] * inv
    cos = np.concatenate([np.cos(ang_r), np.cos(ang_r), np.cos(ang_c), np.cos(ang_c)], axis=1)
    sin = np.concatenate([-np.sin(ang_r), np.sin(ang_r), -np.sin(ang_c), np.sin(ang_c)], axis=1)
    return jnp.asarray(cos, f32), jnp.asarray(sin, f32)


def _na_tables(rpb):
    cq = np.arange(GRID_W)
    kc = np.tile(np.arange(GRID_W), NA_WR)
    col_start = np.clip(cq - NA_WC // 2, 0, GRID_W - NA_WC)
    valid = (kc[None, :] >= col_start[:, None]) & (kc[None, :] < col_start[:, None] + NA_WC)
    col_off = np.clip(cq[None, :] - cq[:, None], 1 - NA_WC, NA_WC - 1) + NA_WC - 1
    onehot = (col_off[None, :, :] == np.arange(2 * NA_WC - 1)[:, None, None]).astype(np.float32)
    toep = jnp.einsum('hic,cqk->hiqk', rpb.astype(f32), jnp.asarray(onehot), precision=lax.Precision.HIGHEST)
    tables = []
    for d in range(NA_WR):
        rows = toep[:, NA_WR - 1 - d:2 * NA_WR - 1 - d]
        tables.append(rows.transpose(0, 2, 1, 3).reshape(NA_HEADS, GRID_W, NA_NK))
    return jnp.stack(tables, axis=1), jnp.asarray(valid, f32)


def kernel(x_prompt, x_sample, cache_na_k, cache_na_v, state_lru_f, state_lru_b,
           state_ret_f, state_ret_b, state_ssd_f, state_ssd_b, c, c_ctx,
           norm1_g, norm2_g, w_ada, b_ada, w_in, w_gate, b_gate, w_branch, w_out,
           lru_conv_w, lru_conv_b, lru_wa, lru_ba, lru_wx, lru_bx, lru_lambda,
           ret_gn_g, ssd_conv_w, ssd_conv_b, ssd_a_log, ssd_dt_bias, ssd_d, ssd_norm_g,
           na_q_g, na_k_g, na_rpb, ffn_w_up, ffn_conv_w, ffn_conv_b, ffn_w_down):
    xs0 = x_sample.reshape(T_SAMPLE, D_MODEL)
    xp0 = x_prompt.reshape(T_PROMPT, D_MODEL)
    n_ptiles = N_TILES - N_SAMPLE_TILES
    cond = jnp.concatenate([c, c_ctx[None, :], jnp.zeros((N_COND_PAD - N_COND, D_MODEL), f32)], axis=0)
    mod_all = ada_modulation(cond, w_ada, b_ada)
    cos, sin = _rope_tables()
    hh = jnp.arange(RET_HEADS, dtype=f32)
    ret_la = jnp.stack([jnp.log1p(-jnp.exp2(-5.0 - hh)), jnp.log1p(-jnp.exp2(-5.5 - hh))], axis=1)
    cache_k = cache_na_k.reshape(DEC_BATCH, DEPTH, PAST_LEN, NA_W)
    cache_v = cache_na_v.reshape(DEC_BATCH, DEPTH, PAST_LEN, NA_W)
    lru_init = (state_lru_f.reshape(DEC_BATCH, DEPTH, 1, D_RNN), state_lru_b.reshape(DEC_BATCH, DEPTH, 1, D_RNN))

    w_head_b, w_tail_b = _split_w_in(w_in)
    w_gate_b, w_branch_b, w_out_b = w_gate.astype(bf16), w_branch.astype(bf16), w_out.astype(bf16)
    w_up_b, w_down_b = ffn_w_up.astype(bf16), ffn_w_down.astype(bf16)

    new = {k: None for k in ("k", "v", "lru_f", "lru_b", "ret_f", "ret_b", "ssd_f", "ssd_b")}
    for l in range(DEPTH):
        mod = mod_all[l].reshape(N_COND_PAD, 1, N_MOD * D_MODEL)
        g1 = norm1_g[l][None, :]
        if l == 0:
            proj, xn = in_projection(xs0, 0, 0, N_SAMPLE_TILES, g1, mod, w_head_b, w_tail_b, l)
            proj, xn = in_projection(xp0, 0, N_SAMPLE_TILES, n_ptiles, g1, mod, w_head_b, w_tail_b, l, (proj, xn))
        else:
            proj, xn = in_projection(x, 0, 0, N_TILES, g1, mod, w_head_b, w_tail_b, l)

        w4 = jnp.concatenate([lru_wa[l, 0], lru_wx[l, 0], lru_wa[l, 1], lru_wx[l, 1]], axis=-1).astype(bf16)
        b4 = jnp.concatenate([lru_ba[l, 0].reshape(LRU_BLOCKS, 1, LRU_BS), lru_bx[l, 0].reshape(LRU_BLOCKS, 1, LRU_BS),
                              lru_ba[l, 1].reshape(LRU_BLOCKS, 1, LRU_BS), lru_bx[l, 1].reshape(LRU_BLOCKS, 1, LRU_BS)],
                             axis=-1)
        lru_args = (lru_conv_w[l], lru_conv_b[l][None, :], w4, b4, lru_lambda[l])
        y_lru = lru_branch(proj, (None,), 0, DEC_BATCH, DEC_SEQ, *lru_args, lru_init, l)
        y_lru, new["lru_f"], new["lru_b"] = lru_branch(
            proj, (y_lru, new["lru_f"], new["lru_b"]), T_SAMPLE, BATCH, SEQ, *lru_args, None, l)

        gn = ret_gn_g[l][None, :]
        y_ret = retention_branch(proj, (None,), 0, DEC_BATCH, DEC_SEQ, ret_la, gn, (state_ret_f, state_ret_b), l)
        y_ret, new["ret_f"], new["ret_b"] = retention_branch(
            proj, (y_ret, new["ret_f"], new["ret_b"]), T_SAMPLE, BATCH, SEQ, ret_la, gn, None, l)

        prm = jnp.stack([_group_lanes(ssd_dt_bias[l]), _group_lanes(ssd_a_log[l])], axis=1)
        prm = jnp.concatenate([prm, jnp.zeros((SSD_GROUPS, V7X_SUBLANES - 2, V7X_LANES), f32)], axis=1)
        dvec = jnp.repeat(ssd_d[l], SSD_P)[None, :]
        ssd_args = (ssd_conv_w[l], ssd_conv_b[l][None, :], prm, dvec, ssd_norm_g[l][None, :])
        y_ssd = ssd_branch(proj, (None,), 0, DEC_BATCH, DEC_SEQ, *ssd_args, (state_ssd_f, state_ssd_b), l)
        y_ssd, new["ssd_f"], new["ssd_b"] = ssd_branch(
            proj, (y_ssd, new["ssd_f"], new["ssd_b"]), T_SAMPLE, BATCH, SEQ, *ssd_args, None, l)

        qg = na_q_g[l][None, :]
        kg = na_k_g[l][None, :]
        bias, valid = _na_tables(na_rpb[l])
        y_na = neighbourhood_attention(proj, cache_k, cache_v, l, qg, kg, cos, sin, bias, valid)
        y_na, new["k"], new["v"] = context_attention(proj, (y_na, new["k"], new["v"]), T_SAMPLE, qg, kg, l)

        merged = merge_branches(xn, (y_lru, y_ret, y_ssd, y_na), w_gate_b, b_gate[l][None, :], w_branch_b, l)
        if l == 0:
            x = residual_projection(merged, w_out_b, l, xs0, mod, 2, 0, N_SAMPLE_TILES)
            x = residual_projection(merged, w_out_b, l, xp0, mod, 2, N_SAMPLE_TILES, n_ptiles, x_tile0=0, buf=x)
        else:
            x = residual_projection(merged, w_out_b, l, x, mod, 2)
        hmid = ffn_up(x, norm2_g[l][None, :], mod, w_up_b, l, ffn_conv_w[l], ffn_conv_b[l][None, :])
        if l < DEPTH - 1:
            x = residual_projection(hmid, w_down_b, l, x, mod, 5)
        else:
            y_sample = residual_projection(hmid, w_down_b, l, x, mod, 5, 0, N_SAMPLE_TILES, out_rows=T_SAMPLE)
            y_prompt = residual_projection(hmid, w_down_b, l, x, mod, 5, N_SAMPLE_TILES, n_ptiles,
                                           out_tile0=0, out_rows=T_PROMPT)

    kv_shape = (BATCH, DEPTH, SEQ, NA_HEADS, NA_HD)
    return (y_prompt.reshape(BATCH, SEQ, D_MODEL), y_sample.reshape(DEC_BATCH, DEC_SEQ, D_MODEL),
            new["k"].reshape(kv_shape), new["v"].reshape(kv_shape),
            new["lru_f"].reshape(BATCH, DEPTH, D_RNN), new["lru_b"].reshape(BATCH, DEPTH, D_RNN),
            new["ret_f"], new["ret_b"], new["ssd_f"], new["ssd_b"])
```

```python
import functools
import math

import jax
import jax.numpy as jnp
import numpy as np
from jax import lax
from jax.experimental import pallas as pl
from jax.experimental.pallas import tpu as pltpu

D_MODEL = 2048
BATCH = 16
SEQ = 256
DEPTH = 2
DEC_BATCH = 8
DEC_SEQ = 1024
PAST_LEN = 256
GRID_W = 64
EPS = 1e-6
N_BRANCH = 4
BRANCH_W = 1024
N_MOD = 6
D_RNN = 1024
LRU_BLOCKS = 8
LRU_BS = D_RNN // LRU_BLOCKS
LRU_CONV = 4
LRU_C = 8.0
RET_HEADS = 4
RET_DK = 128
RET_DV = 256
SSD_HEADS = 16
SSD_P = 64
SSD_N = 128
SSD_GROUPS = 2
SSD_CONV = 4
D_SSD = SSD_HEADS * SSD_P
SSD_CONV_CH = D_SSD + 2 * SSD_GROUPS * SSD_N
NA_HEADS = 8
NA_HD = 128
NA_W = NA_HEADS * NA_HD
NA_WR = 8
NA_WC = 16
ROPE_BASE = 10000.0
D_FF = 5632
FFN_CONV = 3
IN_SIZES = (D_RNN, D_RNN,
            RET_HEADS * RET_DK, RET_HEADS * RET_DK, RET_HEADS * RET_DV, RET_HEADS * RET_DV,
            D_SSD, SSD_CONV_CH, 2 * SSD_HEADS,
            NA_W, NA_W, NA_W)
D_IN = sum(IN_SIZES)
IN_OFFSETS = tuple(int(s) for s in np.cumsum(IN_SIZES)[:-1])

V7X_LANES = 128
V7X_SUBLANES = 8
V7X_VMEM_LIMIT_BYTES = 56 * 1024 * 1024

TM = 1024
T_SAMPLE = DEC_BATCH * DEC_SEQ
T_PROMPT = BATCH * SEQ
T_ALL = T_SAMPLE + T_PROMPT
N_SAMPLE_TILES = T_SAMPLE // TM
N_TILES = T_ALL // TM
N_COND = DEC_BATCH + 1
N_COND_PAD = 16

SSD_HPG = SSD_HEADS // SSD_GROUPS
OFF_LRU_X = 0
OFF_LRU_G = OFF_LRU_X + D_RNN
OFF_RET_Q = OFF_LRU_G + D_RNN
OFF_RET_K = OFF_RET_Q + RET_HEADS * RET_DK
OFF_RET_V = OFF_RET_K + RET_HEADS * RET_DK
OFF_RET_G = OFF_RET_V + RET_HEADS * RET_DV
OFF_SSD_Z = OFF_RET_G + RET_HEADS * RET_DV
OFF_SSD_XBC = OFF_SSD_Z + D_SSD
OFF_NA_Q = OFF_SSD_XBC + SSD_CONV_CH
OFF_NA_K = OFF_NA_Q + NA_W
OFF_NA_V = OFF_NA_K + NA_W
OFF_SSD_DT = OFF_NA_V + NA_W
PROJ_TN = 512
D_IN_PAD = -(-(OFF_SSD_DT + SSD_GROUPS * V7X_LANES) // PROJ_TN) * PROJ_TN
N_HEAD_TILES = OFF_NA_Q // PROJ_TN

RET_CHUNK = 256
SSD_CHUNK = 256
SSD_TB = 128

f32 = jnp.float32
bf16 = jnp.bfloat16

_ARB = "arbitrary"


def _params(n_axes):
    return pltpu.CompilerParams(dimension_semantics=(_ARB,) * n_axes,
                                vmem_limit_bytes=V7X_VMEM_LIMIT_BYTES)


def _mod_spec(k, tile0=0):
    return pl.BlockSpec((1, 1, D_MODEL), lambda i, j: (jnp.minimum(tile0 + i, N_SAMPLE_TILES), 0, k))


def _dot(a, b):
    return jnp.dot(a, b, preferred_element_type=f32)


def _dot_nt(a, b):
    return lax.dot_general(a, b, (((1,), (1,)), ((), ())), preferred_element_type=f32)


def _sigmoid(x):
    return 0.5 * jnp.tanh(0.5 * x) + 0.5


def _silu(x):
    return x * _sigmoid(x)


def _gelu_tanh(x):
    return 0.5 * x * (1.0 + jnp.tanh(math.sqrt(2.0 / math.pi) * (x + 0.044715 * (x * x * x))))


def _softplus(x):
    return jnp.maximum(x, 0.0) + jnp.log1p(jnp.exp(-jnp.abs(x)))


def _drop_refs(body, n, *refs):
    body(*refs[n:])


def _inplace_call(body, bufs, grid, in_specs, args, out_specs, out_shape, scratch, name):
    held = [(k, b) for k, b in enumerate(bufs) if b is not None]
    if held:
        body = functools.partial(_drop_refs, body, len(held))
        in_specs = [pl.BlockSpec(memory_space=pl.ANY)] * len(held) + list(in_specs)
        args = [b for _, b in held] + list(args)
    aliases = {pos: k for pos, (k, _) in enumerate(held)}
    return pl.pallas_call(
        body, grid=grid, in_specs=in_specs, out_specs=out_specs, out_shape=out_shape,
        scratch_shapes=scratch, input_output_aliases=aliases,
        compiler_params=_params(len(grid)), name=name,
    )(*args)


_BRANCH_BUF = jax.ShapeDtypeStruct((T_ALL, BRANCH_W), bf16)


ADA_TN = 1024


def _ada_kernel(c_ref, w_ref, b_ref, o_ref):
    c = _silu(c_ref[...]).astype(bf16)
    o_ref[0] = _dot(c, w_ref[0].astype(bf16)) + b_ref[0]


def ada_modulation(cond, w_ada, b_ada):
    n = N_MOD * D_MODEL
    return pl.pallas_call(
        _ada_kernel,
        grid=(DEPTH, n // ADA_TN),
        in_specs=[pl.BlockSpec((N_COND_PAD, D_MODEL), lambda l, j: (0, 0)),
                  pl.BlockSpec((1, D_MODEL, ADA_TN), lambda l, j: (l, 0, j)),
                  pl.BlockSpec((1, 1, ADA_TN), lambda l, j: (l, 0, j))],
        out_specs=pl.BlockSpec((1, N_COND_PAD, ADA_TN), lambda l, j: (l, 0, j)),
        out_shape=jax.ShapeDtypeStruct((DEPTH, N_COND_PAD, n), f32),
        compiler_params=_params(2),
        name="ada_modulation",
    )(cond, w_ada, b_ada.reshape(DEPTH, 1, n))


NORM_ROWS = 128


def _modulated_norm(x_ref, g_ref, sc_ref, sh_ref, xn_ref):
    g = g_ref[...]
    sc = 1.0 + sc_ref[0]
    sh = sh_ref[0]

    def body(r, carry):
        rows = pl.ds(pl.multiple_of(r * NORM_ROWS, NORM_ROWS), NORM_ROWS)
        x = x_ref[rows, :]
        y = x * lax.rsqrt(jnp.mean(x * x, axis=-1, keepdims=True) + EPS)
        xn_ref[rows, :] = ((y * g) * sc + sh).astype(bf16)
        return carry

    lax.fori_loop(0, TM // NORM_ROWS, body, 0)


def _in_proj_kernel(x_ref, g_ref, sc_ref, sh_ref, wh_ref, wt_ref, o_ref, xn_ref):
    j = pl.program_id(1)

    @pl.when(j == 0)
    def _():
        _modulated_norm(x_ref, g_ref, sc_ref, sh_ref, xn_ref)

    @pl.when(j < N_HEAD_TILES)
    def _():
        o_ref[...] = _dot(xn_ref[...], wh_ref[0]).astype(bf16)

    @pl.when(j >= N_HEAD_TILES)
    def _():
        o_ref[...] = _dot(xn_ref[...], wt_ref[0]).astype(bf16)


def in_projection(x, x_tile0, tile0, n_tiles, g, mod, w_head, w_tail, layer, bufs=(None, None)):
    n = D_IN_PAD
    n_tail = w_tail.shape[2] // PROJ_TN
    return _inplace_call(
        _in_proj_kernel, bufs, (n_tiles, n // PROJ_TN),
        [pl.BlockSpec((TM, D_MODEL), lambda i, j: (x_tile0 + i, 0)),
         pl.BlockSpec((1, D_MODEL), lambda i, j: (0, 0)),
         _mod_spec(1, tile0), _mod_spec(0, tile0),
         pl.BlockSpec((1, D_MODEL, PROJ_TN), lambda i, j: (layer, 0, jnp.minimum(j, N_HEAD_TILES - 1))),
         pl.BlockSpec((1, D_MODEL, PROJ_TN),
                      lambda i, j: (layer, 0, jnp.clip(j - N_HEAD_TILES, 0, n_tail - 1)))],
        [x, g, mod, mod, w_head, w_tail],
        [pl.BlockSpec((TM, PROJ_TN), lambda i, j: (tile0 + i, j)),
         pl.BlockSpec((TM, D_MODEL), lambda i, j: (tile0 + i, 0))],
        [jax.ShapeDtypeStruct((T_ALL, n), bf16), jax.ShapeDtypeStruct((T_ALL, D_MODEL), bf16)],
        [], "in_projection")


MERGE_TN = 256


def _merge_kernel(xn_ref, *refs):
    y_refs, wg_refs, bg_refs = refs[0:4], refs[4:8], refs[8:12]
    wb_ref, o_ref = refs[12], refs[13]
    xn = xn_ref[...]
    acc = None
    for n in range(N_BRANCH):
        gate = _sigmoid(_dot(xn, wg_refs[n][0].astype(bf16)) + bg_refs[n][...])
        term = gate * _dot(y_refs[n][...], wb_ref[0, n])
        acc = term if acc is None else acc + term
    o_ref[...] = acc.astype(bf16)


def merge_branches(xn, ys, w_gate, b_gate, w_branch, layer):
    nj = D_MODEL // MERGE_TN
    y_spec = pl.BlockSpec((TM, BRANCH_W), lambda i, j: (i, 0))
    wg_specs = [pl.BlockSpec((1, D_MODEL, MERGE_TN), lambda i, j, n=n: (layer, 0, n * nj + j))
                for n in range(N_BRANCH)]
    bg_specs = [pl.BlockSpec((1, MERGE_TN), lambda i, j, n=n: (0, n * nj + j)) for n in range(N_BRANCH)]
    return pl.pallas_call(
        _merge_kernel,
        grid=(N_TILES, nj),
        in_specs=[pl.BlockSpec((TM, D_MODEL), lambda i, j: (i, 0)),
                  y_spec, y_spec, y_spec, y_spec, *wg_specs, *bg_specs,
                  pl.BlockSpec((1, N_BRANCH, BRANCH_W, MERGE_TN), lambda i, j: (layer, 0, 0, j))],
        out_specs=pl.BlockSpec((TM, MERGE_TN), lambda i, j: (i, j)),
        out_shape=jax.ShapeDtypeStruct((T_ALL, D_MODEL), bf16),
        compiler_params=_params(2),
        name="merge_branches",
    )(xn, *ys, *([w_gate] * N_BRANCH), *([b_gate] * N_BRANCH), w_branch)


RES_TN = 512


def _residual_kernel(a_ref, w_ref, x_ref, gv_ref, o_ref):
    o_ref[...] = x_ref[...] + gv_ref[0] * _dot(a_ref[...], w_ref[0])


def residual_projection(a, w, layer, x, mod, k_mod, tile0=0, n_tiles=N_TILES, x_tile0=None, out_tile0=None,
                        out_rows=T_ALL, buf=None):
    kdim = a.shape[1]
    x_tile0 = tile0 if x_tile0 is None else x_tile0
    out_tile0 = tile0 if out_tile0 is None else out_tile0
    return _inplace_call(
        _residual_kernel, (buf,), (n_tiles, D_MODEL // RES_TN),
        [pl.BlockSpec((TM, kdim), lambda i, j: (tile0 + i, 0)),
         pl.BlockSpec((1, kdim, RES_TN), lambda i, j: (layer, 0, j)),
         pl.BlockSpec((TM, RES_TN), lambda i, j: (x_tile0 + i, j)),
         pl.BlockSpec((1, 1, RES_TN),
                      lambda i, j: (jnp.minimum(tile0 + i, N_SAMPLE_TILES), 0, k_mod * (D_MODEL // RES_TN) + j))],
        [a, w, x, mod],
        pl.BlockSpec((TM, RES_TN), lambda i, j: (out_tile0 + i, j)),
        jax.ShapeDtypeStruct((out_rows, D_MODEL), f32), [], "residual_projection")


FFN_TN = 512


def _ffn_up_kernel(x_ref, g_ref, sc_ref, sh_ref, wa_ref, wv_ref, cw_ref, cb_ref, o_ref, xn_ref, pad_ref):
    i = pl.program_id(0)

    @pl.when(pl.program_id(1) == 0)
    def _():
        _modulated_norm(x_ref, g_ref, sc_ref, sh_ref, xn_ref)
        zeros = jnp.zeros((V7X_SUBLANES, FFN_TN), f32)
        pad_ref[pl.ds(0, V7X_SUBLANES), :] = zeros
        pad_ref[pl.ds(V7X_SUBLANES + TM, V7X_SUBLANES), :] = zeros

    xn = xn_ref[...]
    a = _dot(xn, wa_ref[0].astype(bf16))
    pad_ref[pl.ds(V7X_SUBLANES, TM), :] = a
    seq_len = jnp.where(i < N_SAMPLE_TILES, DEC_SEQ, SEQ)
    pos = lax.broadcasted_iota(jnp.int32, (TM, 1), 0) & (seq_len - 1)
    prev = jnp.where(pos == 0, 0.0, pad_ref[pl.ds(V7X_SUBLANES - 1, TM), :])
    nxt = jnp.where(pos == seq_len - 1, 0.0, pad_ref[pl.ds(V7X_SUBLANES + 1, TM), :])
    cw = cw_ref[...]
    conv = cb_ref[...] + prev * cw[0:1, :] + a * cw[1:2, :] + nxt * cw[2:3, :]
    o_ref[...] = (_gelu_tanh(conv) * _dot(xn, wv_ref[0].astype(bf16))).astype(bf16)


def ffn_up(x, g, mod, w_up, layer, conv_w, conv_b):
    nj = D_FF // FFN_TN
    return pl.pallas_call(
        _ffn_up_kernel,
        grid=(N_TILES, nj),
        in_specs=[pl.BlockSpec((TM, D_MODEL), lambda i, j: (i, 0)),
                  pl.BlockSpec((1, D_MODEL), lambda i, j: (0, 0)),
                  _mod_spec(4), _mod_spec(3),
                  pl.BlockSpec((1, D_MODEL, FFN_TN), lambda i, j: (layer, 0, j)),
                  pl.BlockSpec((1, D_MODEL, FFN_TN), lambda i, j: (layer, 0, nj + j)),
                  pl.BlockSpec((FFN_CONV, FFN_TN), lambda i, j: (0, j)),
                  pl.BlockSpec((1, FFN_TN), lambda i, j: (0, j))],
        out_specs=pl.BlockSpec((TM, FFN_TN), lambda i, j: (i, j)),
        out_shape=jax.ShapeDtypeStruct((T_ALL, D_FF), bf16),
        scratch_shapes=[pltpu.VMEM((TM, D_MODEL), bf16),
                        pltpu.VMEM((TM + 2 * V7X_SUBLANES, FFN_TN), f32)],
        compiler_params=_params(2),
        name="ffn_up",
    )(x, g, mod, mod, w_up, w_up, conv_w, conv_b)


LRU_CB = 512
LRU_NB = LRU_CB // LRU_BS


def _lru_kernel(*refs, seq_len, has_init):
    if has_init:
        (x_ref, g_ref, cw_ref, cb_ref, w4_ref, b4_ref, lam_ref, h0f_ref, h0b_ref,
         y_ref, pad_ref, af_ref, uf_ref, ab_ref, ub_ref, hf_ref, hb_ref) = refs
    else:
        (x_ref, g_ref, cw_ref, cb_ref, w4_ref, b4_ref, lam_ref,
         y_ref, sf_ref, sb_ref, pad_ref, af_ref, uf_ref, ab_ref, ub_ref, hf_ref, hb_ref) = refs
    L = seq_len
    S = V7X_SUBLANES
    zeros = jnp.zeros((S, LRU_CB), f32)
    pad_ref[pl.ds(0, S), :] = zeros
    pad_ref[pl.ds(S + L, S), :] = zeros
    pad_ref[pl.ds(S, L), :] = x_ref[...].astype(f32)
    cw = cw_ref[...]
    xc = cb_ref[...]
    for kk in range(LRU_CONV):
        xc = xc + pad_ref[pl.ds(S + kk - LRU_CONV // 2, L), :] * cw[kk:kk + 1, :]
    sp = _softplus(-lam_ref[...])
    for n in range(LRU_NB):
        cols = slice(n * LRU_BS, (n + 1) * LRU_BS)
        xcn = xc[:, cols]
        z = _dot(xcn.astype(bf16), w4_ref[n]) + b4_ref[n]
        for d, (a_ref, u_ref) in enumerate(((af_ref, uf_ref), (ab_ref, ub_ref))):
            r = _sigmoid(z[:, (2 * d) * LRU_BS:(2 * d + 1) * LRU_BS])
            ig = _sigmoid(z[:, (2 * d + 1) * LRU_BS:(2 * d + 2) * LRU_BS])
            log_a = (-LRU_C) * r * sp[d:d + 1, cols]
            a = jnp.exp(log_a)
            a_ref[:, cols] = a
            gain2 = -jnp.tanh(log_a) * (a * a + 1.0)
            gain = jnp.where(gain2 > 0.0, gain2 * lax.rsqrt(gain2), 0.0)
            u_ref[:, cols] = gain * (ig * xcn)

    if has_init:
        h0f = h0f_ref[0, 0]
        h0b = h0b_ref[0, 0]
    else:
        h0f = jnp.zeros((1, LRU_CB), f32)
        h0b = jnp.zeros((1, LRU_CB), f32)

    def step(i, carry):
        hf, hb = carry
        base_f = pl.multiple_of(i * S, S)
        base_b = pl.multiple_of(L - S - i * S, S)
        for r in range(S):
            tf = pl.ds(base_f + r, 1)
            tb = pl.ds(base_b + (S - 1 - r), 1)
            hf = af_ref[tf, :] * hf + uf_ref[tf, :]
            hb = ab_ref[tb, :] * hb + ub_ref[tb, :]
            hf_ref[tf, :] = hf
            hb_ref[tb, :] = hb
        return hf, hb

    hf, hb = lax.fori_loop(0, L // S, step, (h0f, h0b))
    if not has_init:
        sf_ref[0, 0] = hf
        sb_ref[0, 0] = hb
    y_ref[...] = ((hf_ref[...] + hb_ref[...]) * _gelu_tanh(g_ref[...].astype(f32))).astype(bf16)


def lru_branch(proj, bufs, row0, n_seq, seq_len, cw, cb, w4, b4, lam, init, layer):
    has_init = init is not None
    ncb = D_RNN // LRU_CB
    rb0 = row0 // seq_len
    in_specs = [pl.BlockSpec((seq_len, LRU_CB), lambda s, c: (rb0 + s, OFF_LRU_X // LRU_CB + c)),
                pl.BlockSpec((seq_len, LRU_CB), lambda s, c: (rb0 + s, OFF_LRU_G // LRU_CB + c)),
                pl.BlockSpec((LRU_CONV, LRU_CB), lambda s, c: (0, c)),
                pl.BlockSpec((1, LRU_CB), lambda s, c: (0, c)),
                pl.BlockSpec((LRU_NB, LRU_BS, 4 * LRU_BS), lambda s, c: (c, 0, 0)),
                pl.BlockSpec((LRU_NB, 1, 4 * LRU_BS), lambda s, c: (c, 0, 0)),
                pl.BlockSpec((2, LRU_CB), lambda s, c: (0, c))]
    args = [proj, proj, cw, cb, w4, b4, lam]
    state_spec = pl.BlockSpec((1, 1, 1, LRU_CB), lambda s, c: (s, layer, 0, c))
    y_spec = pl.BlockSpec((seq_len, LRU_CB), lambda s, c: (rb0 + s, c))
    if has_init:
        in_specs += [state_spec, state_spec]
        args += [init[0], init[1]]
        out_specs, out_shape = y_spec, _BRANCH_BUF
    else:
        st = jax.ShapeDtypeStruct((n_seq, DEPTH, 1, D_RNN), f32)
        out_specs, out_shape = [y_spec, state_spec, state_spec], [_BRANCH_BUF, st, st]
    scratch = [pltpu.VMEM((seq_len + 2 * V7X_SUBLANES, LRU_CB), f32)] + \
              [pltpu.VMEM((seq_len, LRU_CB), f32) for _ in range(6)]
    return _inplace_call(functools.partial(_lru_kernel, seq_len=seq_len, has_init=has_init), bufs,
                         (n_seq, ncb), in_specs, args, out_specs, out_shape, scratch, "lru_branch")


def _ret_kernel(*refs, seq_len, has_init):
    if has_init:
        (la_ref, q_ref, k_ref, v_ref, g_ref, gn_ref, s0f_ref, s0b_ref, y_ref, acc_ref) = refs
    else:
        (la_ref, q_ref, k_ref, v_ref, g_ref, gn_ref, y_ref, sf_ref, sb_ref, acc_ref) = refs
    h = pl.program_id(1)
    T = min(RET_CHUNK, seq_len)
    nc = seq_len // T
    la_f = la_ref[h, 0]
    la_b = la_ref[h, 1]
    tt = lax.broadcasted_iota(jnp.int32, (T, T), 0)
    ss = lax.broadcasted_iota(jnp.int32, (T, T), 1)
    diff = (tt - ss).astype(f32)
    dsum = (jnp.where(tt >= ss, jnp.exp(la_f * diff), 0.0)
            + jnp.where(ss >= tt, jnp.exp(-la_b * diff), 0.0))
    tcol = lax.broadcasted_iota(jnp.int32, (T, 1), 0).astype(f32)
    scale = RET_DK ** -0.5

    def chunk(c):
        rows = pl.ds(c * T, T)
        q = q_ref[rows, :].astype(f32)
        ks = k_ref[rows, :].astype(f32) * scale
        v = v_ref[rows, :].astype(f32)
        return rows, q, ks, v

    s_f = s0f_ref[0, 0, 0] if has_init else None
    for c in range(nc):
        rows, q, ks, v = chunk(c)
        scores = _dot_nt(q.astype(bf16), ks.astype(bf16)) * dsum
        y = _dot(scores.astype(bf16), v.astype(bf16))
        if s_f is not None:
            y = y + _dot((q * jnp.exp(la_f * (tcol + 1.0))).astype(bf16), s_f.astype(bf16))
        acc_ref[rows, :] = y
        if c < nc - 1 or not has_init:
            upd = _dot(ks.T.astype(bf16), (v * jnp.exp(la_f * (T - 1.0 - tcol))).astype(bf16))
            s_f = upd if s_f is None else jnp.exp(la_f * T) * s_f + upd
    s_b = s0b_ref[0, 0, 0] if has_init else None
    for c in reversed(range(nc)):
        rows, q, ks, v = chunk(c)
        if s_b is not None:
            acc_ref[rows, :] += _dot((q * jnp.exp(la_b * (T - tcol))).astype(bf16), s_b.astype(bf16))
        if c > 0 or not has_init:
            upd = _dot(ks.T.astype(bf16), (v * jnp.exp(la_b * tcol)).astype(bf16))
            s_b = upd if s_b is None else jnp.exp(la_b * T) * s_b + upd
    if not has_init:
        sf_ref[0, 0, 0] = s_f
        sb_ref[0, 0, 0] = s_b
    y = acc_ref[...]
    mu = jnp.mean(y, axis=-1, keepdims=True)
    yc = y - mu
    var = jnp.mean(yc * yc, axis=-1, keepdims=True)
    y = yc * lax.rsqrt(var + EPS) * gn_ref[...]
    y_ref[...] = (y * _silu(g_ref[...].astype(f32))).astype(bf16)


def retention_branch(proj, bufs, row0, n_seq, seq_len, la, gn, init, layer):
    has_init = init is not None
    rb0 = row0 // seq_len
    in_specs = [pl.BlockSpec(memory_space=pltpu.SMEM),
                pl.BlockSpec((seq_len, RET_DK), lambda s, h: (rb0 + s, OFF_RET_Q // RET_DK + h)),
                pl.BlockSpec((seq_len, RET_DK), lambda s, h: (rb0 + s, OFF_RET_K // RET_DK + h)),
                pl.BlockSpec((seq_len, RET_DV), lambda s, h: (rb0 + s, OFF_RET_V // RET_DV + h)),
                pl.BlockSpec((seq_len, RET_DV), lambda s, h: (rb0 + s, OFF_RET_G // RET_DV + h)),
                pl.BlockSpec((1, RET_DV), lambda s, h: (0, h))]
    args = [la, proj, proj, proj, proj, gn]
    state_spec = pl.BlockSpec((1, 1, 1, RET_DK, RET_DV), lambda s, h: (s, layer, h, 0, 0))
    y_spec = pl.BlockSpec((seq_len, RET_DV), lambda s, h: (rb0 + s, h))
    if has_init:
        in_specs += [state_spec, state_spec]
        args += [init[0], init[1]]
        out_specs, out_shape = y_spec, _BRANCH_BUF
    else:
        st = jax.ShapeDtypeStruct((n_seq, DEPTH, RET_HEADS, RET_DK, RET_DV), f32)
        out_specs, out_shape = [y_spec, state_spec, state_spec], [_BRANCH_BUF, st, st]
    return _inplace_call(functools.partial(_ret_kernel, seq_len=seq_len, has_init=has_init), bufs,
                         (n_seq, RET_HEADS), in_specs, args, out_specs, out_shape,
                         [pltpu.VMEM((seq_len, RET_DV), f32)], "retention_branch")


SSD_GW = SSD_HPG * SSD_P


def _split3(x):
    h1 = x.astype(bf16)
    r1 = x - h1.astype(f32)
    h2 = r1.astype(bf16)
    h3 = (r1 - h2.astype(f32)).astype(bf16)
    return h1, h2, h3


def _dot_exact_rhs(m, x):
    h1, h2, h3 = _split3(x)
    return _dot(m, h1) + _dot(m, h2) + _dot(m, h3)


def _dot_exact_lhs(x, m):
    h1, h2, h3 = _split3(x)
    return _dot(h1, m) + _dot(h2, m) + _dot(h3, m)


def _conv4_silu(src_ref, pad_ref, cw_ref, cb_ref, L, width):
    S = V7X_SUBLANES
    zeros = jnp.zeros((S, width), f32)
    pad_ref[pl.ds(0, S), pl.ds(0, width)] = zeros
    pad_ref[pl.ds(S + L, S), pl.ds(0, width)] = zeros
    pad_ref[pl.ds(S, L), pl.ds(0, width)] = src_ref[...].astype(f32)
    cw = cw_ref[...]
    out = cb_ref[...]
    for kk in range(SSD_CONV):
        out = out + pad_ref[pl.ds(S + kk - SSD_CONV // 2, L), pl.ds(0, width)] * cw[kk:kk + 1, :]
    return _silu(out)


def _heads_to_lanes(s_ref, st_ref):
    for hh in range(SSD_HPG):
        st_ref[:, pl.ds(hh * SSD_P, SSD_P)] = s_ref[0, 0, hh]
    return st_ref[...]


def _ssd_kernel(*refs, seq_len, has_init):
    if has_init:
        (z_ref, x_ref, b_ref, c_ref, dt_ref, cwx_ref, cbx_ref, cwb_ref, cbb_ref, cwc_ref, cbc_ref,
         prm_ref, dvec_ref, ng_ref, s0f_ref, s0b_ref,
         y_ref, pad_ref, xs_ref, bs_ref, cs_ref, acc_ref, rb_ref, yn_ref, st_ref) = refs
    else:
        (z_ref, x_ref, b_ref, c_ref, dt_ref, cwx_ref, cbx_ref, cwb_ref, cbb_ref, cwc_ref, cbc_ref,
         prm_ref, dvec_ref, ng_ref,
         y_ref, sf_ref, sb_ref, pad_ref, xs_ref, bs_ref, cs_ref, acc_ref, rb_ref, yn_ref, st_ref) = refs
    g = pl.program_id(1)
    L = seq_len
    T = min(SSD_CHUNK, L)
    nc = L // T
    H = SSD_HPG
    xs_ref[...] = _conv4_silu(x_ref, pad_ref, cwx_ref, cbx_ref, L, SSD_GW)
    bs_ref[...] = _conv4_silu(b_ref, pad_ref, cwb_ref, cbb_ref, L, SSD_N)
    cs_ref[...] = _conv4_silu(c_ref, pad_ref, cwc_ref, cbc_ref, L, SSD_N)
    prm = prm_ref[0]
    a_neg = -jnp.exp(prm[1:2, :])
    tt = lax.broadcasted_iota(jnp.int32, (T, T), 0)
    ss = lax.broadcasted_iota(jnp.int32, (T, T), 1)
    lower = tt >= ss
    upper = ss >= tt
    tri_l = jnp.where(lower, 1.0, 0.0).astype(bf16)
    tri_u = jnp.where(upper, 1.0, 0.0).astype(bf16)
    lower_b = lower[:SSD_TB, :SSD_TB]
    upper_b = upper[:SSD_TB, :SSD_TB]
    er = lax.broadcasted_iota(jnp.int32, (V7X_LANES, SSD_GW), 0)
    ec = lax.broadcasted_iota(jnp.int32, (V7X_LANES, SSD_GW), 1) // SSD_P
    exp_f = jnp.where(er == ec, 1.0, 0.0).astype(bf16)
    exp_b = jnp.where(er == ec + H, 1.0, 0.0).astype(bf16)

    def expand(w, e):
        hi = w.astype(bf16)
        lo = (w - hi.astype(f32)).astype(bf16)
        return _dot(hi, e) + _dot(lo, e)

    def chunk_terms(c):
        rows = pl.ds(c * T, T)
        dt = _softplus(dt_ref[rows, :].astype(f32) + prm[0:1, :])
        da = dt * a_neg
        return rows, dt, da

    s_f = _heads_to_lanes(s0f_ref, st_ref) if has_init else None
    for c in range(nc):
        rows, dt, da = chunk_terms(c)
        cum = _dot_exact_rhs(tri_l, da)
        rsum = _dot_exact_rhs(tri_u, da)
        rb_ref[rows, :] = rsum
        da_t = da.T
        cum_t = _dot_exact_lhs(da_t, tri_u)
        rsum_t = _dot_exact_lhs(da_t, tri_l)
        dt_t = dt.T
        bmat = bs_ref[rows, :]
        cmat = cs_ref[rows, :]
        xmat = xs_ref[rows, :]
        gmat = _dot_nt(cmat.astype(bf16), bmat.astype(bf16))
        for hh in range(H):
            cf, cf_t, dtf_t = cum[:, hh:hh + 1], cum_t[hh:hh + 1, :], dt_t[hh:hh + 1, :]
            rb, rb_t, dtb_t = rsum[:, H + hh:H + hh + 1], rsum_t[H + hh:H + hh + 1, :], dt_t[H + hh:H + hh + 1, :]
            block_rows = []
            for bi in range(T // SSD_TB):
                rr = slice(bi * SSD_TB, (bi + 1) * SSD_TB)
                blocks = []
                for bj in range(T // SSD_TB):
                    cc = slice(bj * SSD_TB, (bj + 1) * SSD_TB)
                    if bi > bj:
                        blk = jnp.exp(cf[rr] - cf_t[:, cc]) * dtf_t[:, cc]
                    elif bi < bj:
                        blk = jnp.exp(rb[rr] - rb_t[:, cc]) * dtb_t[:, cc]
                    else:
                        blk = (jnp.where(lower_b, jnp.exp(cf[rr] - cf_t[:, cc]), 0.0) * dtf_t[:, cc]
                               + jnp.where(upper_b, jnp.exp(rb[rr] - rb_t[:, cc]), 0.0) * dtb_t[:, cc])
                    blocks.append(gmat[rr, cc] * blk)
                block_rows.append(jnp.concatenate(blocks, axis=1))
            m = jnp.concatenate(block_rows, axis=0)
            xh = xmat[:, hh * SSD_P:(hh + 1) * SSD_P]
            acc_ref[rows, pl.ds(hh * SSD_P, SSD_P)] = _dot(m.astype(bf16), xh.astype(bf16))
        ecum = jnp.exp(cum)
        if s_f is not None:
            acc_ref[rows, :] += _dot(cmat.astype(bf16), s_f.astype(bf16)) * expand(ecum, exp_f)
        if c < nc - 1 or not has_init:
            tail = jnp.exp(cum[T - 1:T, :] - cum) * dt
            xw = (xmat * expand(tail, exp_f)).astype(bf16)
            upd = _dot(bmat.T.astype(bf16), xw)
            if s_f is None:
                s_f = upd
            else:
                s_f = s_f * expand(jnp.broadcast_to(ecum[T - 1:T, :], (V7X_SUBLANES, V7X_LANES)),
                                   exp_f)[0:1, :] + upd
    s_b = _heads_to_lanes(s0b_ref, st_ref) if has_init else None
    for c in reversed(range(nc)):
        rows, dt, da = chunk_terms(c)
        rsum = rb_ref[rows, :]
        ers = jnp.exp(rsum)
        bmat = bs_ref[rows, :]
        cmat = cs_ref[rows, :]
        xmat = xs_ref[rows, :]
        if s_b is not None:
            acc_ref[rows, :] += _dot(cmat.astype(bf16), s_b.astype(bf16)) * expand(ers, exp_b)
        if c > 0 or not has_init:
            tail = jnp.exp(rsum[0:1, :] - rsum) * dt
            xw = (xmat * expand(tail, exp_b)).astype(bf16)
            upd = _dot(bmat.T.astype(bf16), xw)
            if s_b is None:
                s_b = upd
            else:
                s_b = s_b * expand(jnp.broadcast_to(ers[0:1, :], (V7X_SUBLANES, V7X_LANES)),
                                   exp_b)[0:1, :] + upd
    if not has_init:
        for hh in range(H):
            sf_ref[0, 0, hh] = s_f[:, hh * SSD_P:(hh + 1) * SSD_P]
            sb_ref[0, 0, hh] = s_b[:, hh * SSD_P:(hh + 1) * SSD_P]
    yg = (acc_ref[...] + xs_ref[...] * dvec_ref[...]) * _silu(z_ref[...].astype(f32))
    yn_ref[g] = yg

    @pl.when(g == SSD_GROUPS - 1)
    def _():
        ssq = None
        for gg in range(SSD_GROUPS):
            y = yn_ref[gg]
            s = jnp.sum(y * y, axis=-1, keepdims=True)
            ssq = s if ssq is None else ssq + s
        inv = lax.rsqrt(ssq * (1.0 / D_SSD) + EPS)
        for gg in range(SSD_GROUPS):
            cols = pl.ds(gg * SSD_GW, SSD_GW)
            y_ref[:, cols] = (yn_ref[gg] * inv * ng_ref[:, cols]).astype(bf16)


def ssd_branch(proj, bufs, row0, n_seq, seq_len, conv_w, conv_b, prm, dvec, ng, init, layer):
    has_init = init is not None
    rb0 = row0 // seq_len
    xoff = OFF_SSD_XBC
    boff = OFF_SSD_XBC + D_SSD
    coff = boff + SSD_GROUPS * SSD_N
    in_specs = [pl.BlockSpec((seq_len, SSD_GW), lambda s, g: (rb0 + s, OFF_SSD_Z // SSD_GW + g)),
                pl.BlockSpec((seq_len, SSD_GW), lambda s, g: (rb0 + s, xoff // SSD_GW + g)),
                pl.BlockSpec((seq_len, SSD_N), lambda s, g: (rb0 + s, boff // SSD_N + g)),
                pl.BlockSpec((seq_len, SSD_N), lambda s, g: (rb0 + s, coff // SSD_N + g)),
                pl.BlockSpec((seq_len, V7X_LANES), lambda s, g: (rb0 + s, OFF_SSD_DT // V7X_LANES + g)),
                pl.BlockSpec((SSD_CONV, SSD_GW), lambda s, g: (0, g)),
                pl.BlockSpec((1, SSD_GW), lambda s, g: (0, g)),
                pl.BlockSpec((SSD_CONV, SSD_N), lambda s, g: (0, D_SSD // SSD_N + g)),
                pl.BlockSpec((1, SSD_N), lambda s, g: (0, D_SSD // SSD_N + g)),
                pl.BlockSpec((SSD_CONV, SSD_N), lambda s, g: (0, D_SSD // SSD_N + SSD_GROUPS + g)),
                pl.BlockSpec((1, SSD_N), lambda s, g: (0, D_SSD // SSD_N + SSD_GROUPS + g)),
                pl.BlockSpec((1, V7X_SUBLANES, V7X_LANES), lambda s, g: (g, 0, 0)),
                pl.BlockSpec((1, SSD_GW), lambda s, g: (0, g)),
                pl.BlockSpec((1, D_SSD), lambda s, g: (0, 0))]
    args = [proj, proj, proj, proj, proj, conv_w, conv_b, conv_w, conv_b, conv_w, conv_b, prm, dvec, ng]
    state_spec = pl.BlockSpec((1, 1, SSD_HPG, SSD_N, SSD_P), lambda s, g: (s, layer, g, 0, 0))
    y_spec = pl.BlockSpec((seq_len, D_SSD), lambda s, g: (rb0 + s, 0))
    if has_init:
        in_specs += [state_spec, state_spec]
        args += [init[0], init[1]]
        out_specs, out_shape = y_spec, _BRANCH_BUF
    else:
        st = jax.ShapeDtypeStruct((n_seq, DEPTH, SSD_HEADS, SSD_N, SSD_P), f32)
        out_specs, out_shape = [y_spec, state_spec, state_spec], [_BRANCH_BUF, st, st]
    scratch = [pltpu.VMEM((seq_len + 2 * V7X_SUBLANES, SSD_GW), f32),
               pltpu.VMEM((seq_len, SSD_GW), f32),
               pltpu.VMEM((seq_len, SSD_N), f32),
               pltpu.VMEM((seq_len, SSD_N), f32),
               pltpu.VMEM((seq_len, SSD_GW), f32),
               pltpu.VMEM((seq_len, V7X_LANES), f32),
               pltpu.VMEM((SSD_GROUPS, seq_len, SSD_GW), f32),
               pltpu.VMEM((SSD_N, SSD_GW), f32)]
    return _inplace_call(functools.partial(_ssd_kernel, seq_len=seq_len, has_init=has_init), bufs,
                         (n_seq, SSD_GROUPS), in_specs, args, out_specs, out_shape, scratch, "ssd_branch")


def _head_rmsnorm(x, g):
    return x * lax.rsqrt(jnp.mean(x * x, axis=-1, keepdims=True) + EPS) * g


CTX_HB = 4
CTX_W = CTX_HB * NA_HD


def _ctx_attn_kernel(q_ref, k_ref, v_ref, qg_ref, kg_ref, y_ref, ko_ref, vo_ref):
    scale = NA_HD ** -0.5
    vo_ref[0, 0] = v_ref[...].astype(f32)
    for h in range(CTX_HB):
        cols = pl.ds(h * NA_HD, NA_HD)
        q = _head_rmsnorm(q_ref[:, cols].astype(f32), qg_ref[...])
        k = _head_rmsnorm(k_ref[:, cols].astype(f32), kg_ref[...])
        ko_ref[0, 0, :, cols] = k
        s = _dot_nt(q.astype(bf16), k.astype(bf16)) * scale
        p = jnp.exp(s - jnp.max(s, axis=-1, keepdims=True))
        o = _dot(p.astype(bf16), v_ref[:, cols]) / jnp.sum(p, axis=-1, keepdims=True)
        y_ref[:, cols] = o.astype(bf16)


def context_attention(proj, bufs, row0, qg, kg, layer):
    rb0 = row0 // SEQ
    spec = lambda off: pl.BlockSpec((SEQ, CTX_W), lambda s, hb: (rb0 + s, off // CTX_W + hb))
    gspec = pl.BlockSpec((1, NA_HD), lambda s, hb: (0, 0))
    kv_spec = pl.BlockSpec((1, 1, SEQ, CTX_W), lambda s, hb: (s, layer, 0, hb))
    kv_shape = jax.ShapeDtypeStruct((BATCH, DEPTH, SEQ, NA_W), f32)
    return _inplace_call(
        _ctx_attn_kernel, bufs, (BATCH, NA_HEADS // CTX_HB),
        [spec(OFF_NA_Q), spec(OFF_NA_K), spec(OFF_NA_V), gspec, gspec], [proj, proj, proj, qg, kg],
        [pl.BlockSpec((SEQ, CTX_W), lambda s, hb: (rb0 + s, hb)), kv_spec, kv_spec],
        [_BRANCH_BUF, kv_shape, kv_shape], [], "context_attention")


NA_ROWS = DEC_SEQ // GRID_W
NA_NK = NA_WR * GRID_W


def _na_row_start(r):
    return min(max(r - NA_WR // 2, 0), NA_ROWS - NA_WR)


def _rope(x, cos, sin_signed):
    lane = lax.broadcasted_iota(jnp.int32, x.shape, 1)
    quarter = NA_HD // 4
    swapped = jnp.where((lane & (2 * quarter - 1)) < quarter,
                        pltpu.roll(x, NA_HD - quarter, axis=1), pltpu.roll(x, quarter, axis=1))
    return x * cos + swapped * sin_signed


def _na_row_groups():
    groups, r = [], 0
    while r < NA_ROWS:
        n = 1
        while r + n < NA_ROWS and _na_row_start(r + n) == _na_row_start(r):
            n += 1
        groups.append((r, n, _na_row_start(r)))
        r += n
    return groups


def _na_attn_kernel(q_ref, k_ref, v_ref, kc_ref, vc_ref, qg_ref, kg_ref, cos_ref, sin_ref,
                    bias_ref, valid_ref, y_ref, qs_ref, ks_ref, s_ref, p_ref, oc_ref, w_ref):
    scale = NA_HD ** -0.5
    cos = cos_ref[...]
    sin = sin_ref[...]
    qs_ref[...] = _rope(_head_rmsnorm(q_ref[...].astype(f32), qg_ref[...]), cos, sin).astype(bf16)
    ks_ref[...] = _rope(_head_rmsnorm(k_ref[...].astype(f32), kg_ref[...]), cos, sin).astype(bf16)
    groups = _na_row_groups()
    valid = valid_ref[...] > 0.0
    for r0, n, rs in groups:
        rows = pl.ds(r0 * GRID_W, n * GRID_W)
        d0 = r0 - rs
        kw = ks_ref[pl.ds(rs * GRID_W, NA_NK), :]
        s = (_dot_nt(qs_ref[rows, :], kw) * scale).reshape(n, GRID_W, NA_NK) + bias_ref[0, d0:d0 + n]
        s_ref[rows, :] = jnp.where(valid[None], s, -1e30).reshape(n * GRID_W, NA_NK)
    s_ctx = _dot_nt(qs_ref[...], kc_ref[0, 0].astype(bf16)) * scale
    s_loc = s_ref[...]
    m = jnp.maximum(jnp.max(s_loc, axis=-1, keepdims=True), jnp.max(s_ctx, axis=-1, keepdims=True))
    p_loc = jnp.exp(s_loc - m)
    p_ctx = jnp.exp(s_ctx - m)
    w_ref[...] = 1.0 / (jnp.sum(p_loc, axis=-1, keepdims=True) + jnp.sum(p_ctx, axis=-1, keepdims=True))
    p_ref[...] = p_loc.astype(bf16)
    oc_ref[...] = _dot(p_ctx.astype(bf16), vc_ref[0, 0].astype(bf16))
    for r0, n, rs in groups:
        rows = pl.ds(r0 * GRID_W, n * GRID_W)
        vw = v_ref[pl.ds(rs * GRID_W, NA_NK), :]
        y_ref[rows, :] = ((_dot(p_ref[rows, :], vw) + oc_ref[rows, :]) * w_ref[rows, :]).astype(bf16)


def neighbourhood_attention(proj, cache_k, cache_v, layer, qg, kg, cos, sin, bias, valid):
    spec = lambda off: pl.BlockSpec((DEC_SEQ, NA_HD), lambda b, h: (b, off // NA_HD + h))
    cspec = pl.BlockSpec((1, 1, PAST_LEN, NA_HD), lambda b, h: (b, layer, 0, h))
    gspec = pl.BlockSpec((1, NA_HD), lambda b, h: (0, 0))
    tspec = pl.BlockSpec((DEC_SEQ, NA_HD), lambda b, h: (0, 0))
    return pl.pallas_call(
        _na_attn_kernel,
        grid=(DEC_BATCH, NA_HEADS),
        in_specs=[spec(OFF_NA_Q), spec(OFF_NA_K), spec(OFF_NA_V), cspec, cspec, gspec, gspec,
                  tspec, tspec,
                  pl.BlockSpec((1, NA_WR, GRID_W, NA_NK), lambda b, h: (h, 0, 0, 0)),
                  pl.BlockSpec((GRID_W, NA_NK), lambda b, h: (0, 0))],
        out_specs=pl.BlockSpec((DEC_SEQ, NA_HD), lambda b, h: (b, h)),
        out_shape=_BRANCH_BUF,
        scratch_shapes=[pltpu.VMEM((DEC_SEQ, NA_HD), bf16), pltpu.VMEM((DEC_SEQ, NA_HD), bf16),
                        pltpu.VMEM((DEC_SEQ, NA_NK), f32), pltpu.VMEM((DEC_SEQ, NA_NK), bf16),
                        pltpu.VMEM((DEC_SEQ, NA_HD), f32), pltpu.VMEM((DEC_SEQ, 1), f32)],
        compiler_params=_params(2), name="neighbourhood_attention",
    )(proj, proj, proj, cache_k, cache_v, qg, kg, cos, sin, bias, valid)


def _split_w_in(w_in):
    dt0, na0 = IN_OFFSETS[7], IN_OFFSETS[8]
    head = w_in.astype(bf16)
    pieces = [head[..., na0:]]
    dt = head[..., dt0:na0]
    for gidx in range(SSD_GROUPS):
        pieces.append(dt[..., gidx * SSD_HPG:(gidx + 1) * SSD_HPG])
        pieces.append(dt[..., SSD_HEADS + gidx * SSD_HPG:SSD_HEADS + (gidx + 1) * SSD_HPG])
        pieces.append(jnp.zeros((DEPTH, D_MODEL, V7X_LANES - 2 * SSD_HPG), bf16))
    used = OFF_SSD_DT + SSD_GROUPS * V7X_LANES
    pieces.append(jnp.zeros((DEPTH, D_MODEL, D_IN_PAD - used), bf16))
    return head, jnp.concatenate(pieces, axis=-1)


def _group_lanes(v):
    rows = []
    for gidx in range(SSD_GROUPS):
        sl = slice(gidx * SSD_HPG, (gidx + 1) * SSD_HPG)
        rows.append(jnp.concatenate([v[0, sl], v[1, sl], jnp.zeros((V7X_LANES - 2 * SSD_HPG,), f32)]))
    return jnp.stack(rows)


def _rope_tables():
    t = np.arange(DEC_SEQ)
    quarter = NA_HD // 4
    inv = ROPE_BASE ** (-np.arange(quarter, dtype=np.float32) / quarter)
    ang_r = (t // GRID_W).astype(np.float32)[:, None] * inv
    ang_c = (t % GRID_W).astype(np.float32)[:, None] * inv
    cos = np.concatenate([np.cos(ang_r), np.cos(ang_r), np.cos(ang_c), np.cos(ang_c)], axis=1)
    sin = np.concatenate([-np.sin(ang_r), np.sin(ang_r), -np.sin(ang_c), np.sin(ang_c)], axis=1)
    return jnp.asarray(cos, f32), jnp.asarray(sin, f32)


def _na_tables(rpb):
    cq = np.arange(GRID_W)
    kc = np.tile(np.arange(GRID_W), NA_WR)
    col_start = np.clip(cq - NA_WC // 2, 0, GRID_W - NA_WC)
    valid = (kc[None, :] >= col_start[:, None]) & (kc[None, :] < col_start[:, None] + NA_WC)
    col_off = np.clip(cq[None, :] - cq[:, None], 1 - NA_WC, NA_WC - 1) + NA_WC - 1
    onehot = (col_off[None, :, :] == np.arange(2 * NA_WC - 1)[:, None, None]).astype(np.float32)
    toep = jnp.einsum('hic,cqk->hiqk', rpb.astype(f32), jnp.asarray(onehot), precision=lax.Precision.HIGHEST)
    tables = []
    for d in range(NA_WR):
        rows = toep[:, NA_WR - 1 - d:2 * NA_WR - 1 - d]
        tables.append(rows.transpose(0, 2, 1, 3).reshape(NA_HEADS, GRID_W, NA_NK))
    return jnp.stack(tables, axis=1), jnp.asarray(valid, f32)


def kernel(x_prompt, x_sample, cache_na_k, cache_na_v, state_lru_f, state_lru_b,
           state_ret_f, state_ret_b, state_ssd_f, state_ssd_b, c, c_ctx,
           norm1_g, norm2_g, w_ada, b_ada, w_in, w_gate, b_gate, w_branch, w_out,
           lru_conv_w, lru_conv_b, lru_wa, lru_ba, lru_wx, lru_bx, lru_lambda,
           ret_gn_g, ssd_conv_w, ssd_conv_b, ssd_a_log, ssd_dt_bias, ssd_d, ssd_norm_g,
           na_q_g, na_k_g, na_rpb, ffn_w_up, ffn_conv_w, ffn_conv_b, ffn_w_down):
    xs0 = x_sample.reshape(T_SAMPLE, D_MODEL)
    xp0 = x_prompt.reshape(T_PROMPT, D_MODEL)
    n_ptiles = N_TILES - N_SAMPLE_TILES
    cond = jnp.concatenate([c, c_ctx[None, :], jnp.zeros((N_COND_PAD - N_COND, D_MODEL), f32)], axis=0)
    mod_all = ada_modulation(cond, w_ada, b_ada)
    cos, sin = _rope_tables()
    hh = jnp.arange(RET_HEADS, dtype=f32)
    ret_la = jnp.stack([jnp.log1p(-jnp.exp2(-5.0 - hh)), jnp.log1p(-jnp.exp2(-5.5 - hh))], axis=1)
    cache_k = cache_na_k.reshape(DEC_BATCH, DEPTH, PAST_LEN, NA_W)
    cache_v = cache_na_v.reshape(DEC_BATCH, DEPTH, PAST_LEN, NA_W)
    lru_init = (state_lru_f.reshape(DEC_BATCH, DEPTH, 1, D_RNN), state_lru_b.reshape(DEC_BATCH, DEPTH, 1, D_RNN))

    w_head_b, w_tail_b = _split_w_in(w_in)
    w_branch_b, w_out_b, w_down_b = w_branch.astype(bf16), w_out.astype(bf16), ffn_w_down.astype(bf16)

    new = {k: None for k in ("k", "v", "lru_f", "lru_b", "ret_f", "ret_b", "ssd_f", "ssd_b")}
    for l in range(DEPTH):
        mod = mod_all[l].reshape(N_COND_PAD, 1, N_MOD * D_MODEL)
        g1 = norm1_g[l][None, :]
        if l == 0:
            proj, xn = in_projection(xs0, 0, 0, N_SAMPLE_TILES, g1, mod, w_head_b, w_tail_b, l)
            proj, xn = in_projection(xp0, 0, N_SAMPLE_TILES, n_ptiles, g1, mod, w_head_b, w_tail_b, l, (proj, xn))
        else:
            proj, xn = in_projection(x, 0, 0, N_TILES, g1, mod, w_head_b, w_tail_b, l)

        w4 = jnp.concatenate([lru_wa[l, 0], lru_wx[l, 0], lru_wa[l, 1], lru_wx[l, 1]], axis=-1).astype(bf16)
        b4 = jnp.concatenate([lru_ba[l, 0].reshape(LRU_BLOCKS, 1, LRU_BS), lru_bx[l, 0].reshape(LRU_BLOCKS, 1, LRU_BS),
                              lru_ba[l, 1].reshape(LRU_BLOCKS, 1, LRU_BS), lru_bx[l, 1].reshape(LRU_BLOCKS, 1, LRU_BS)],
                             axis=-1)
        lru_args = (lru_conv_w[l], lru_conv_b[l][None, :], w4, b4, lru_lambda[l])
        y_lru = lru_branch(proj, (None,), 0, DEC_BATCH, DEC_SEQ, *lru_args, lru_init, l)
        y_lru, new["lru_f"], new["lru_b"] = lru_branch(
            proj, (y_lru, new["lru_f"], new["lru_b"]), T_SAMPLE, BATCH, SEQ, *lru_args, None, l)

        gn = ret_gn_g[l][None, :]
        y_ret = retention_branch(proj, (None,), 0, DEC_BATCH, DEC_SEQ, ret_la, gn, (state_ret_f, state_ret_b), l)
        y_ret, new["ret_f"], new["ret_b"] = retention_branch(
            proj, (y_ret, new["ret_f"], new["ret_b"]), T_SAMPLE, BATCH, SEQ, ret_la, gn, None, l)

        prm = jnp.stack([_group_lanes(ssd_dt_bias[l]), _group_lanes(ssd_a_log[l])], axis=1)
        prm = jnp.concatenate([prm, jnp.zeros((SSD_GROUPS, V7X_SUBLANES - 2, V7X_LANES), f32)], axis=1)
        dvec = jnp.repeat(ssd_d[l], SSD_P)[None, :]
        ssd_args = (ssd_conv_w[l], ssd_conv_b[l][None, :], prm, dvec, ssd_norm_g[l][None, :])
        y_ssd = ssd_branch(proj, (None,), 0, DEC_BATCH, DEC_SEQ, *ssd_args, (state_ssd_f, state_ssd_b), l)
        y_ssd, new["ssd_f"], new["ssd_b"] = ssd_branch(
            proj, (y_ssd, new["ssd_f"], new["ssd_b"]), T_SAMPLE, BATCH, SEQ, *ssd_args, None, l)

        qg = na_q_g[l][None, :]
        kg = na_k_g[l][None, :]
        bias, valid = _na_tables(na_rpb[l])
        y_na = neighbourhood_attention(proj, cache_k, cache_v, l, qg, kg, cos, sin, bias, valid)
        y_na, new["k"], new["v"] = context_attention(proj, (y_na, new["k"], new["v"]), T_SAMPLE, qg, kg, l)

        merged = merge_branches(xn, (y_lru, y_ret, y_ssd, y_na), w_gate, b_gate[l][None, :], w_branch_b, l)
        if l == 0:
            x = residual_projection(merged, w_out_b, l, xs0, mod, 2, 0, N_SAMPLE_TILES)
            x = residual_projection(merged, w_out_b, l, xp0, mod, 2, N_SAMPLE_TILES, n_ptiles, x_tile0=0, buf=x)
        else:
            x = residual_projection(merged, w_out_b, l, x, mod, 2)
        hmid = ffn_up(x, norm2_g[l][None, :], mod, ffn_w_up, l, ffn_conv_w[l], ffn_conv_b[l][None, :])
        if l < DEPTH - 1:
            x = residual_projection(hmid, w_down_b, l, x, mod, 5)
        else:
            y_sample = residual_projection(hmid, w_down_b, l, x, mod, 5, 0, N_SAMPLE_TILES, out_rows=T_SAMPLE)
            y_prompt = residual_projection(hmid, w_down_b, l, x, mod, 5, N_SAMPLE_TILES, n_ptiles,
                                           out_tile0=0, out_rows=T_PROMPT)

    kv_shape = (BATCH, DEPTH, SEQ, NA_HEADS, NA_HD)
    return (y_prompt.reshape(BATCH, SEQ, D_MODEL), y_sample.reshape(DEC_BATCH, DEC_SEQ, D_MODEL),
            new["k"].reshape(kv_shape), new["v"].reshape(kv_shape),
            new["lru_f"].reshape(BATCH, DEPTH, D_RNN), new["lru_b"].reshape(BATCH, DEPTH, D_RNN),
            new["ret_f"], new["ret_b"], new["ssd_f"], new["ssd_b"])
```

```python
import functools
import math

import jax
import jax.numpy as jnp
import numpy as np
from jax import lax
from jax.experimental import pallas as pl
from jax.experimental.pallas import tpu as pltpu

D_MODEL = 2048
BATCH = 16
SEQ = 256
DEPTH = 2
DEC_BATCH = 8
DEC_SEQ = 1024
PAST_LEN = 256
GRID_W = 64
EPS = 1e-6
N_BRANCH = 4
BRANCH_W = 1024
N_MOD = 6
D_RNN = 1024
LRU_BLOCKS = 8
LRU_BS = D_RNN // LRU_BLOCKS
LRU_CONV = 4
LRU_C = 8.0
RET_HEADS = 4
RET_DK = 128
RET_DV = 256
SSD_HEADS = 16
SSD_P = 64
SSD_N = 128
SSD_GROUPS = 2
SSD_CONV = 4
D_SSD = SSD_HEADS * SSD_P
SSD_CONV_CH = D_SSD + 2 * SSD_GROUPS * SSD_N
NA_HEADS = 8
NA_HD = 128
NA_W = NA_HEADS * NA_HD
NA_WR = 8
NA_WC = 16
ROPE_BASE = 10000.0
D_FF = 5632
FFN_CONV = 3
IN_SIZES = (D_RNN, D_RNN,
            RET_HEADS * RET_DK, RET_HEADS * RET_DK, RET_HEADS * RET_DV, RET_HEADS * RET_DV,
            D_SSD, SSD_CONV_CH, 2 * SSD_HEADS,
            NA_W, NA_W, NA_W)
D_IN = sum(IN_SIZES)
IN_OFFSETS = tuple(int(s) for s in np.cumsum(IN_SIZES)[:-1])

V7X_LANES = 128
V7X_SUBLANES = 8
V7X_VMEM_LIMIT_BYTES = 56 * 1024 * 1024

TM = 1024
T_SAMPLE = DEC_BATCH * DEC_SEQ
T_PROMPT = BATCH * SEQ
T_ALL = T_SAMPLE + T_PROMPT
N_SAMPLE_TILES = T_SAMPLE // TM
N_TILES = T_ALL // TM
N_COND = DEC_BATCH + 1
N_COND_PAD = 16

SSD_HPG = SSD_HEADS // SSD_GROUPS
OFF_LRU_X = 0
OFF_LRU_G = OFF_LRU_X + D_RNN
OFF_RET_Q = OFF_LRU_G + D_RNN
OFF_RET_K = OFF_RET_Q + RET_HEADS * RET_DK
OFF_RET_V = OFF_RET_K + RET_HEADS * RET_DK
OFF_RET_G = OFF_RET_V + RET_HEADS * RET_DV
OFF_SSD_Z = OFF_RET_G + RET_HEADS * RET_DV
OFF_SSD_XBC = OFF_SSD_Z + D_SSD
OFF_NA_Q = OFF_SSD_XBC + SSD_CONV_CH
OFF_NA_K = OFF_NA_Q + NA_W
OFF_NA_V = OFF_NA_K + NA_W
OFF_SSD_DT = OFF_NA_V + NA_W
PROJ_TN = 1024
D_IN_PAD = -(-(OFF_SSD_DT + SSD_GROUPS * V7X_LANES) // PROJ_TN) * PROJ_TN
N_HEAD_TILES = OFF_NA_Q // PROJ_TN
HEAD_COLS = N_HEAD_TILES * PROJ_TN

RET_CHUNK = 256
SSD_CHUNK = 256
SSD_TB = 128

f32 = jnp.float32
bf16 = jnp.bfloat16

_ARB = "arbitrary"


def _params(n_axes):
    return pltpu.CompilerParams(dimension_semantics=(_ARB,) * n_axes,
                                vmem_limit_bytes=V7X_VMEM_LIMIT_BYTES)


def _mod_spec(k, tile0=0):
    return pl.BlockSpec((1, 1, D_MODEL), lambda i, j: (jnp.minimum(tile0 + i, N_SAMPLE_TILES), 0, k))


def _dot(a, b):
    return jnp.dot(a, b, preferred_element_type=f32)


def _dot_nt(a, b):
    return lax.dot_general(a, b, (((1,), (1,)), ((), ())), preferred_element_type=f32)


def _sigmoid(x):
    return 0.5 * jnp.tanh(0.5 * x) + 0.5


def _silu(x):
    return x * _sigmoid(x)


def _gelu_tanh(x):
    return 0.5 * x * (1.0 + jnp.tanh(math.sqrt(2.0 / math.pi) * (x + 0.044715 * (x * x * x))))


def _softplus(x):
    return jnp.maximum(x, 0.0) + jnp.log1p(jnp.exp(-jnp.abs(x)))


def _drop_refs(body, n, *refs):
    body(*refs[n:])


def _inplace_call(body, bufs, grid, in_specs, args, out_specs, out_shape, scratch, name):
    held = [(k, b) for k, b in enumerate(bufs) if b is not None]
    if held:
        body = functools.partial(_drop_refs, body, len(held))
        in_specs = [pl.BlockSpec(memory_space=pl.ANY)] * len(held) + list(in_specs)
        args = [b for _, b in held] + list(args)
    aliases = {pos: k for pos, (k, _) in enumerate(held)}
    return pl.pallas_call(
        body, grid=grid, in_specs=in_specs, out_specs=out_specs, out_shape=out_shape,
        scratch_shapes=scratch, input_output_aliases=aliases,
        compiler_params=_params(len(grid)), name=name,
    )(*args)


_BRANCH_BUF = jax.ShapeDtypeStruct((T_ALL, BRANCH_W), bf16)


ADA_TN = 1024


def _ada_kernel(c_ref, w_ref, b_ref, o_ref):
    c = _silu(c_ref[...]).astype(bf16)
    o_ref[0] = _dot(c, w_ref[0].astype(bf16)) + b_ref[0]


def ada_modulation(cond, w_ada, b_ada):
    n = N_MOD * D_MODEL
    return pl.pallas_call(
        _ada_kernel,
        grid=(DEPTH, n // ADA_TN),
        in_specs=[pl.BlockSpec((N_COND_PAD, D_MODEL), lambda l, j: (0, 0)),
                  pl.BlockSpec((1, D_MODEL, ADA_TN), lambda l, j: (l, 0, j)),
                  pl.BlockSpec((1, 1, ADA_TN), lambda l, j: (l, 0, j))],
        out_specs=pl.BlockSpec((1, N_COND_PAD, ADA_TN), lambda l, j: (l, 0, j)),
        out_shape=jax.ShapeDtypeStruct((DEPTH, N_COND_PAD, n), f32),
        compiler_params=_params(2),
        name="ada_modulation",
    )(cond, w_ada, b_ada.reshape(DEPTH, 1, n))


NORM_ROWS = 128


def _modulated_norm(x_ref, g_ref, sc_ref, sh_ref, xn_ref):
    g = g_ref[...]
    sc = 1.0 + sc_ref[0]
    sh = sh_ref[0]

    def body(r, carry):
        rows = pl.ds(pl.multiple_of(r * NORM_ROWS, NORM_ROWS), NORM_ROWS)
        x = x_ref[rows, :]
        y = x * lax.rsqrt(jnp.mean(x * x, axis=-1, keepdims=True) + EPS)
        xn_ref[rows, :] = ((y * g) * sc + sh).astype(bf16)
        return carry

    lax.fori_loop(0, TM // NORM_ROWS, body, 0)


def _in_proj_kernel(x_ref, g_ref, sc_ref, sh_ref, wh_ref, wt_ref, o_ref, xn_ref):
    j = pl.program_id(1)

    @pl.when(j == 0)
    def _():
        _modulated_norm(x_ref, g_ref, sc_ref, sh_ref, xn_ref)

    @pl.when(j < N_HEAD_TILES)
    def _():
        o_ref[...] = _dot(xn_ref[...], wh_ref[0]).astype(bf16)

    @pl.when(j >= N_HEAD_TILES)
    def _():
        o_ref[...] = _dot(xn_ref[...], wt_ref[0]).astype(bf16)


def in_projection(x, x_tile0, tile0, n_tiles, g, mod, w_head, w_tail, layer, bufs=(None, None)):
    n = D_IN_PAD
    n_tail = w_tail.shape[2] // PROJ_TN
    return _inplace_call(
        _in_proj_kernel, bufs, (n_tiles, n // PROJ_TN),
        [pl.BlockSpec((TM, D_MODEL), lambda i, j: (x_tile0 + i, 0)),
         pl.BlockSpec((1, D_MODEL), lambda i, j: (0, 0)),
         _mod_spec(1, tile0), _mod_spec(0, tile0),
         pl.BlockSpec((1, D_MODEL, PROJ_TN), lambda i, j: (layer, 0, jnp.minimum(j, N_HEAD_TILES - 1))),
         pl.BlockSpec((1, D_MODEL, PROJ_TN),
                      lambda i, j: (layer, 0, jnp.clip(j - N_HEAD_TILES, 0, n_tail - 1)))],
        [x, g, mod, mod, w_head, w_tail],
        [pl.BlockSpec((TM, PROJ_TN), lambda i, j: (tile0 + i, j)),
         pl.BlockSpec((TM, D_MODEL), lambda i, j: (tile0 + i, 0))],
        [jax.ShapeDtypeStruct((T_ALL, n), bf16), jax.ShapeDtypeStruct((T_ALL, D_MODEL), bf16)],
        [], "in_projection")


MERGE_TN = 256


def _merge_kernel(xn_ref, *refs):
    y_refs, wg_refs, bg_refs = refs[0:4], refs[4:8], refs[8:12]
    wb_ref, o_ref = refs[12], refs[13]
    xn = xn_ref[...]
    acc = None
    for n in range(N_BRANCH):
        gate = _sigmoid(_dot(xn, wg_refs[n][0].astype(bf16)) + bg_refs[n][...])
        term = gate * _dot(y_refs[n][...], wb_ref[0, n])
        acc = term if acc is None else acc + term
    o_ref[...] = acc.astype(bf16)


def merge_branches(xn, ys, w_gate, b_gate, w_branch, layer):
    nj = D_MODEL // MERGE_TN
    y_spec = pl.BlockSpec((TM, BRANCH_W), lambda i, j: (i, 0))
    wg_specs = [pl.BlockSpec((1, D_MODEL, MERGE_TN), lambda i, j, n=n: (layer, 0, n * nj + j))
                for n in range(N_BRANCH)]
    bg_specs = [pl.BlockSpec((1, MERGE_TN), lambda i, j, n=n: (0, n * nj + j)) for n in range(N_BRANCH)]
    return pl.pallas_call(
        _merge_kernel,
        grid=(N_TILES, nj),
        in_specs=[pl.BlockSpec((TM, D_MODEL), lambda i, j: (i, 0)),
                  y_spec, y_spec, y_spec, y_spec, *wg_specs, *bg_specs,
                  pl.BlockSpec((1, N_BRANCH, BRANCH_W, MERGE_TN), lambda i, j: (layer, 0, 0, j))],
        out_specs=pl.BlockSpec((TM, MERGE_TN), lambda i, j: (i, j)),
        out_shape=jax.ShapeDtypeStruct((T_ALL, D_MODEL), bf16),
        compiler_params=_params(2),
        name="merge_branches",
    )(xn, *ys, *([w_gate] * N_BRANCH), *([b_gate] * N_BRANCH), w_branch)


RES_TN = 512


def _residual_kernel(a_ref, w_ref, x_ref, gv_ref, o_ref):
    o_ref[...] = x_ref[...] + gv_ref[0] * _dot(a_ref[...], w_ref[0])


def residual_projection(a, w, layer, x, mod, k_mod, tile0=0, n_tiles=N_TILES, x_tile0=None, out_tile0=None,
                        out_rows=T_ALL, buf=None):
    kdim = a.shape[1]
    x_tile0 = tile0 if x_tile0 is None else x_tile0
    out_tile0 = tile0 if out_tile0 is None else out_tile0
    return _inplace_call(
        _residual_kernel, (buf,), (n_tiles, D_MODEL // RES_TN),
        [pl.BlockSpec((TM, kdim), lambda i, j: (tile0 + i, 0)),
         pl.BlockSpec((1, kdim, RES_TN), lambda i, j: (layer, 0, j)),
         pl.BlockSpec((TM, RES_TN), lambda i, j: (x_tile0 + i, j)),
         pl.BlockSpec((1, 1, RES_TN),
                      lambda i, j: (jnp.minimum(tile0 + i, N_SAMPLE_TILES), 0, k_mod * (D_MODEL // RES_TN) + j))],
        [a, w, x, mod],
        pl.BlockSpec((TM, RES_TN), lambda i, j: (out_tile0 + i, j)),
        jax.ShapeDtypeStruct((out_rows, D_MODEL), f32), [], "residual_projection")


FFN_TN = 512


def _ffn_up_kernel(x_ref, g_ref, sc_ref, sh_ref, wa_ref, wv_ref, cw_ref, cb_ref, o_ref, xn_ref, pad_ref):
    i = pl.program_id(0)

    @pl.when(pl.program_id(1) == 0)
    def _():
        _modulated_norm(x_ref, g_ref, sc_ref, sh_ref, xn_ref)
        zeros = jnp.zeros((V7X_SUBLANES, FFN_TN), f32)
        pad_ref[pl.ds(0, V7X_SUBLANES), :] = zeros
        pad_ref[pl.ds(V7X_SUBLANES + TM, V7X_SUBLANES), :] = zeros

    xn = xn_ref[...]
    a = _dot(xn, wa_ref[0].astype(bf16))
    pad_ref[pl.ds(V7X_SUBLANES, TM), :] = a
    seq_len = jnp.where(i < N_SAMPLE_TILES, DEC_SEQ, SEQ)
    pos = lax.broadcasted_iota(jnp.int32, (TM, 1), 0) & (seq_len - 1)
    prev = jnp.where(pos == 0, 0.0, pad_ref[pl.ds(V7X_SUBLANES - 1, TM), :])
    nxt = jnp.where(pos == seq_len - 1, 0.0, pad_ref[pl.ds(V7X_SUBLANES + 1, TM), :])
    cw = cw_ref[...]
    conv = cb_ref[...] + prev * cw[0:1, :] + a * cw[1:2, :] + nxt * cw[2:3, :]
    o_ref[...] = (_gelu_tanh(conv) * _dot(xn, wv_ref[0].astype(bf16))).astype(bf16)


def ffn_up(x, g, mod, w_up, layer, conv_w, conv_b):
    nj = D_FF // FFN_TN
    return pl.pallas_call(
        _ffn_up_kernel,
        grid=(N_TILES, nj),
        in_specs=[pl.BlockSpec((TM, D_MODEL), lambda i, j: (i, 0)),
                  pl.BlockSpec((1, D_MODEL), lambda i, j: (0, 0)),
                  _mod_spec(4), _mod_spec(3),
                  pl.BlockSpec((1, D_MODEL, FFN_TN), lambda i, j: (layer, 0, j)),
                  pl.BlockSpec((1, D_MODEL, FFN_TN), lambda i, j: (layer, 0, nj + j)),
                  pl.BlockSpec((FFN_CONV, FFN_TN), lambda i, j: (0, j)),
                  pl.BlockSpec((1, FFN_TN), lambda i, j: (0, j))],
        out_specs=pl.BlockSpec((TM, FFN_TN), lambda i, j: (i, j)),
        out_shape=jax.ShapeDtypeStruct((T_ALL, D_FF), bf16),
        scratch_shapes=[pltpu.VMEM((TM, D_MODEL), bf16),
                        pltpu.VMEM((TM + 2 * V7X_SUBLANES, FFN_TN), f32)],
        compiler_params=_params(2),
        name="ffn_up",
    )(x, g, mod, mod, w_up, w_up, conv_w, conv_b)


LRU_CB_LONG = 512
LRU_CB_SHORT = D_RNN


def _lru_kernel(*refs, seq_len, has_init, cbw):
    if has_init:
        (x_ref, g_ref, cw_ref, cb_ref, w4_ref, b4_ref, lam_ref, h0f_ref, h0b_ref,
         y_ref, pad_ref, af_ref, uf_ref, ab_ref, ub_ref, hf_ref, hb_ref) = refs
    else:
        (x_ref, g_ref, cw_ref, cb_ref, w4_ref, b4_ref, lam_ref,
         y_ref, sf_ref, sb_ref, pad_ref, af_ref, uf_ref, ab_ref, ub_ref, hf_ref, hb_ref) = refs
    L = seq_len
    S = V7X_SUBLANES
    zeros = jnp.zeros((S, cbw), f32)
    pad_ref[pl.ds(0, S), :] = zeros
    pad_ref[pl.ds(S + L, S), :] = zeros
    pad_ref[pl.ds(S, L), :] = x_ref[...].astype(f32)
    cw = cw_ref[...]
    xc = cb_ref[...]
    for kk in range(LRU_CONV):
        xc = xc + pad_ref[pl.ds(S + kk - LRU_CONV // 2, L), :] * cw[kk:kk + 1, :]
    sp = _softplus(-lam_ref[...])
    for n in range(cbw // LRU_BS):
        cols = slice(n * LRU_BS, (n + 1) * LRU_BS)
        xcn = xc[:, cols]
        z = _dot(xcn.astype(bf16), w4_ref[n]) + b4_ref[n]
        for d, (a_ref, u_ref) in enumerate(((af_ref, uf_ref), (ab_ref, ub_ref))):
            r = _sigmoid(z[:, (2 * d) * LRU_BS:(2 * d + 1) * LRU_BS])
            ig = _sigmoid(z[:, (2 * d + 1) * LRU_BS:(2 * d + 2) * LRU_BS])
            log_a = (-LRU_C) * r * sp[d:d + 1, cols]
            a = jnp.exp(log_a)
            a_ref[:, cols] = a
            gain2 = -jnp.tanh(log_a) * (a * a + 1.0)
            gain = jnp.where(gain2 > 0.0, gain2 * lax.rsqrt(gain2), 0.0)
            u_ref[:, cols] = gain * (ig * xcn)

    if has_init:
        h0f = h0f_ref[0, 0]
        h0b = h0b_ref[0, 0]
    else:
        h0f = jnp.zeros((1, cbw), f32)
        h0b = jnp.zeros((1, cbw), f32)

    def step(i, carry):
        hf, hb = carry
        base_f = pl.multiple_of(i * S, S)
        base_b = pl.multiple_of(L - S - i * S, S)
        for r in range(S):
            tf = pl.ds(base_f + r, 1)
            tb = pl.ds(base_b + (S - 1 - r), 1)
            hf = af_ref[tf, :] * hf + uf_ref[tf, :]
            hb = ab_ref[tb, :] * hb + ub_ref[tb, :]
            hf_ref[tf, :] = hf
            hb_ref[tb, :] = hb
        return hf, hb

    hf, hb = lax.fori_loop(0, L // S, step, (h0f, h0b))
    if not has_init:
        sf_ref[0, 0] = hf
        sb_ref[0, 0] = hb
    y_ref[...] = ((hf_ref[...] + hb_ref[...]) * _gelu_tanh(g_ref[...].astype(f32))).astype(bf16)


def lru_branch(proj, bufs, row0, n_seq, seq_len, cw, cb, w4, b4, lam, init, layer):
    has_init = init is not None
    cbw = LRU_CB_LONG if seq_len == DEC_SEQ else LRU_CB_SHORT
    ncb = D_RNN // cbw
    nb = cbw // LRU_BS
    rb0 = row0 // seq_len
    in_specs = [pl.BlockSpec((seq_len, cbw), lambda s, c: (rb0 + s, OFF_LRU_X // cbw + c)),
                pl.BlockSpec((seq_len, cbw), lambda s, c: (rb0 + s, OFF_LRU_G // cbw + c)),
                pl.BlockSpec((LRU_CONV, cbw), lambda s, c: (0, c)),
                pl.BlockSpec((1, cbw), lambda s, c: (0, c)),
                pl.BlockSpec((nb, LRU_BS, 4 * LRU_BS), lambda s, c: (c, 0, 0)),
                pl.BlockSpec((nb, 1, 4 * LRU_BS), lambda s, c: (c, 0, 0)),
                pl.BlockSpec((2, cbw), lambda s, c: (0, c))]
    args = [proj, proj, cw, cb, w4, b4, lam]
    state_spec = pl.BlockSpec((1, 1, 1, cbw), lambda s, c: (s, layer, 0, c))
    y_spec = pl.BlockSpec((seq_len, cbw), lambda s, c: (rb0 + s, c))
    if has_init:
        in_specs += [state_spec, state_spec]
        args += [init[0], init[1]]
        out_specs, out_shape = y_spec, _BRANCH_BUF
    else:
        st = jax.ShapeDtypeStruct((n_seq, DEPTH, 1, D_RNN), f32)
        out_specs, out_shape = [y_spec, state_spec, state_spec], [_BRANCH_BUF, st, st]
    scratch = [pltpu.VMEM((seq_len + 2 * V7X_SUBLANES, cbw), f32)] + \
              [pltpu.VMEM((seq_len, cbw), f32) for _ in range(6)]
    return _inplace_call(functools.partial(_lru_kernel, seq_len=seq_len, has_init=has_init, cbw=cbw), bufs,
                         (n_seq, ncb), in_specs, args, out_specs, out_shape, scratch, "lru_branch")


def _ret_kernel(*refs, seq_len, has_init):
    if has_init:
        (la_ref, q_ref, k_ref, v_ref, g_ref, gn_ref, s0f_ref, s0b_ref, y_ref, acc_ref) = refs
    else:
        (la_ref, q_ref, k_ref, v_ref, g_ref, gn_ref, y_ref, sf_ref, sb_ref, acc_ref) = refs
    T = min(RET_CHUNK, seq_len)
    nc = seq_len // T
    tt = lax.broadcasted_iota(jnp.int32, (T, T), 0)
    ss = lax.broadcasted_iota(jnp.int32, (T, T), 1)
    diff = (tt - ss).astype(f32)
    tcol = lax.broadcasted_iota(jnp.int32, (T, 1), 0).astype(f32)
    scale = RET_DK ** -0.5
    for h in range(RET_HEADS):
        la_f = la_ref[h, 0]
        la_b = la_ref[h, 1]
        kcols = pl.ds(h * RET_DK, RET_DK)
        vcols = pl.ds(h * RET_DV, RET_DV)
        dsum = (jnp.where(tt >= ss, jnp.exp(la_f * diff), 0.0)
                + jnp.where(ss >= tt, jnp.exp(-la_b * diff), 0.0))

        def chunk(c):
            rows = pl.ds(c * T, T)
            q = q_ref[rows, kcols].astype(f32)
            ks = k_ref[rows, kcols].astype(f32) * scale
            v = v_ref[rows, vcols].astype(f32)
            return rows, q, ks, v

        s_f = s0f_ref[0, 0, h] if has_init else None
        for c in range(nc):
            rows, q, ks, v = chunk(c)
            scores = _dot_nt(q.astype(bf16), ks.astype(bf16)) * dsum
            y = _dot(scores.astype(bf16), v.astype(bf16))
            if s_f is not None:
                y = y + _dot((q * jnp.exp(la_f * (tcol + 1.0))).astype(bf16), s_f.astype(bf16))
            acc_ref[rows, :] = y
            if c < nc - 1 or not has_init:
                upd = _dot(ks.T.astype(bf16), (v * jnp.exp(la_f * (T - 1.0 - tcol))).astype(bf16))
                s_f = upd if s_f is None else jnp.exp(la_f * T) * s_f + upd
        s_b = s0b_ref[0, 0, h] if has_init else None
        for c in reversed(range(nc)):
            rows, q, ks, v = chunk(c)
            if s_b is not None:
                acc_ref[rows, :] += _dot((q * jnp.exp(la_b * (T - tcol))).astype(bf16), s_b.astype(bf16))
            if c > 0 or not has_init:
                upd = _dot(ks.T.astype(bf16), (v * jnp.exp(la_b * tcol)).astype(bf16))
                s_b = upd if s_b is None else jnp.exp(la_b * T) * s_b + upd
        if not has_init:
            sf_ref[0, 0, h] = s_f
            sb_ref[0, 0, h] = s_b
        y = acc_ref[...]
        mu = jnp.mean(y, axis=-1, keepdims=True)
        yc = y - mu
        var = jnp.mean(yc * yc, axis=-1, keepdims=True)
        y = yc * lax.rsqrt(var + EPS) * gn_ref[:, vcols]
        y_ref[:, vcols] = (y * _silu(g_ref[:, vcols].astype(f32))).astype(bf16)


def retention_branch(proj, bufs, row0, n_seq, seq_len, la, gn, init, layer):
    has_init = init is not None
    rb0 = row0 // seq_len
    qk_w, vg_w = RET_HEADS * RET_DK, RET_HEADS * RET_DV
    in_specs = [pl.BlockSpec(memory_space=pltpu.SMEM),
                pl.BlockSpec((seq_len, qk_w), lambda s: (rb0 + s, OFF_RET_Q // qk_w)),
                pl.BlockSpec((seq_len, qk_w), lambda s: (rb0 + s, OFF_RET_K // qk_w)),
                pl.BlockSpec((seq_len, vg_w), lambda s: (rb0 + s, OFF_RET_V // vg_w)),
                pl.BlockSpec((seq_len, vg_w), lambda s: (rb0 + s, OFF_RET_G // vg_w)),
                pl.BlockSpec((1, vg_w), lambda s: (0, 0))]
    args = [la, proj, proj, proj, proj, gn]
    state_spec = pl.BlockSpec((1, 1, RET_HEADS, RET_DK, RET_DV), lambda s: (s, layer, 0, 0, 0))
    y_spec = pl.BlockSpec((seq_len, vg_w), lambda s: (rb0 + s, 0))
    if has_init:
        in_specs += [state_spec, state_spec]
        args += [init[0], init[1]]
        out_specs, out_shape = y_spec, _BRANCH_BUF
    else:
        st = jax.ShapeDtypeStruct((n_seq, DEPTH, RET_HEADS, RET_DK, RET_DV), f32)
        out_specs, out_shape = [y_spec, state_spec, state_spec], [_BRANCH_BUF, st, st]
    return _inplace_call(functools.partial(_ret_kernel, seq_len=seq_len, has_init=has_init), bufs,
                         (n_seq,), in_specs, args, out_specs, out_shape,
                         [pltpu.VMEM((seq_len, RET_DV), f32)], "retention_branch")


SSD_GW = SSD_HPG * SSD_P


def _split3(x):
    h1 = x.astype(bf16)
    r1 = x - h1.astype(f32)
    h2 = r1.astype(bf16)
    h3 = (r1 - h2.astype(f32)).astype(bf16)
    return h1, h2, h3


def _dot_exact_rhs(m, x):
    h1, h2, h3 = _split3(x)
    return _dot(m, h1) + _dot(m, h2) + _dot(m, h3)


def _dot_exact_lhs(x, m):
    h1, h2, h3 = _split3(x)
    return _dot(h1, m) + _dot(h2, m) + _dot(h3, m)


def _conv4_silu(src_ref, pad_ref, cw_ref, cb_ref, L, width):
    S = V7X_SUBLANES
    zeros = jnp.zeros((S, width), f32)
    pad_ref[pl.ds(0, S), pl.ds(0, width)] = zeros
    pad_ref[pl.ds(S + L, S), pl.ds(0, width)] = zeros
    pad_ref[pl.ds(S, L), pl.ds(0, width)] = src_ref[...].astype(f32)
    cw = cw_ref[...]
    out = cb_ref[...]
    for kk in range(SSD_CONV):
        out = out + pad_ref[pl.ds(S + kk - SSD_CONV // 2, L), pl.ds(0, width)] * cw[kk:kk + 1, :]
    return _silu(out)


def _heads_to_lanes(s_ref, st_ref):
    for hh in range(SSD_HPG):
        st_ref[:, pl.ds(hh * SSD_P, SSD_P)] = s_ref[0, 0, hh]
    return st_ref[...]


def _ssd_kernel(*refs, seq_len, has_init):
    if has_init:
        (z_ref, x_ref, b_ref, c_ref, dt_ref, cwx_ref, cbx_ref, cwb_ref, cbb_ref, cwc_ref, cbc_ref,
         prm_ref, dvec_ref, ng_ref, s0f_ref, s0b_ref,
         y_ref, pad_ref, xs_ref, bs_ref, cs_ref, acc_ref, rb_ref, yn_ref, st_ref) = refs
    else:
        (z_ref, x_ref, b_ref, c_ref, dt_ref, cwx_ref, cbx_ref, cwb_ref, cbb_ref, cwc_ref, cbc_ref,
         prm_ref, dvec_ref, ng_ref,
         y_ref, sf_ref, sb_ref, pad_ref, xs_ref, bs_ref, cs_ref, acc_ref, rb_ref, yn_ref, st_ref) = refs
    g = pl.program_id(1)
    L = seq_len
    T = min(SSD_CHUNK, L)
    nc = L // T
    H = SSD_HPG
    xs_ref[...] = _conv4_silu(x_ref, pad_ref, cwx_ref, cbx_ref, L, SSD_GW)
    bs_ref[...] = _conv4_silu(b_ref, pad_ref, cwb_ref, cbb_ref, L, SSD_N)
    cs_ref[...] = _conv4_silu(c_ref, pad_ref, cwc_ref, cbc_ref, L, SSD_N)
    prm = prm_ref[0]
    a_neg = -jnp.exp(prm[1:2, :])
    tt = lax.broadcasted_iota(jnp.int32, (T, T), 0)
    ss = lax.broadcasted_iota(jnp.int32, (T, T), 1)
    lower = tt >= ss
    upper = ss >= tt
    tri_l = jnp.where(lower, 1.0, 0.0).astype(bf16)
    tri_u = jnp.where(upper, 1.0, 0.0).astype(bf16)
    lower_b = lower[:SSD_TB, :SSD_TB]
    upper_b = upper[:SSD_TB, :SSD_TB]
    er = lax.broadcasted_iota(jnp.int32, (V7X_LANES, SSD_GW), 0)
    ec = lax.broadcasted_iota(jnp.int32, (V7X_LANES, SSD_GW), 1) // SSD_P
    exp_f = jnp.where(er == ec, 1.0, 0.0).astype(bf16)
    exp_b = jnp.where(er == ec + H, 1.0, 0.0).astype(bf16)

    def expand(w, e):
        hi = w.astype(bf16)
        lo = (w - hi.astype(f32)).astype(bf16)
        return _dot(hi, e) + _dot(lo, e)

    def chunk_terms(c):
        rows = pl.ds(c * T, T)
        dt = _softplus(dt_ref[rows, :].astype(f32) + prm[0:1, :])
        da = dt * a_neg
        return rows, dt, da

    s_f = _heads_to_lanes(s0f_ref, st_ref) if has_init else None
    for c in range(nc):
        rows, dt, da = chunk_terms(c)
        cum = _dot_exact_rhs(tri_l, da)
        rsum = _dot_exact_rhs(tri_u, da)
        rb_ref[rows, :] = rsum
        da_t = da.T
        cum_t = _dot_exact_lhs(da_t, tri_u)
        rsum_t = _dot_exact_lhs(da_t, tri_l)
        dt_t = dt.T
        bmat = bs_ref[rows, :]
        cmat = cs_ref[rows, :]
        xmat = xs_ref[rows, :]
        gmat = _dot_nt(cmat.astype(bf16), bmat.astype(bf16))
        for hh in range(H):
            cf, cf_t, dtf_t = cum[:, hh:hh + 1], cum_t[hh:hh + 1, :], dt_t[hh:hh + 1, :]
            rb, rb_t, dtb_t = rsum[:, H + hh:H + hh + 1], rsum_t[H + hh:H + hh + 1, :], dt_t[H + hh:H + hh + 1, :]
            block_rows = []
            for bi in range(T // SSD_TB):
                rr = slice(bi * SSD_TB, (bi + 1) * SSD_TB)
                blocks = []
                for bj in range(T // SSD_TB):
                    cc = slice(bj * SSD_TB, (bj + 1) * SSD_TB)
                    if bi > bj:
                        blk = jnp.exp(cf[rr] - cf_t[:, cc]) * dtf_t[:, cc]
                    elif bi < bj:
                        blk = jnp.exp(rb[rr] - rb_t[:, cc]) * dtb_t[:, cc]
                    else:
                        blk = (jnp.where(lower_b, jnp.exp(cf[rr] - cf_t[:, cc]), 0.0) * dtf_t[:, cc]
                               + jnp.where(upper_b, jnp.exp(rb[rr] - rb_t[:, cc]), 0.0) * dtb_t[:, cc])
                    blocks.append(gmat[rr, cc] * blk)
                block_rows.append(jnp.concatenate(blocks, axis=1))
            m = jnp.concatenate(block_rows, axis=0)
            xh = xmat[:, hh * SSD_P:(hh + 1) * SSD_P]
            acc_ref[rows, pl.ds(hh * SSD_P, SSD_P)] = _dot(m.astype(bf16), xh.astype(bf16))
        ecum = jnp.exp(cum)
        if s_f is not None:
            acc_ref[rows, :] += _dot(cmat.astype(bf16), s_f.astype(bf16)) * expand(ecum, exp_f)
        if c < nc - 1 or not has_init:
            tail = jnp.exp(cum[T - 1:T, :] - cum) * dt
            xw = (xmat * expand(tail, exp_f)).astype(bf16)
            upd = _dot(bmat.T.astype(bf16), xw)
            if s_f is None:
                s_f = upd
            else:
                s_f = s_f * expand(jnp.broadcast_to(ecum[T - 1:T, :], (V7X_SUBLANES, V7X_LANES)),
                                   exp_f)[0:1, :] + upd
    s_b = _heads_to_lanes(s0b_ref, st_ref) if has_init else None
    for c in reversed(range(nc)):
        rows, dt, da = chunk_terms(c)
        rsum = rb_ref[rows, :]
        ers = jnp.exp(rsum)
        bmat = bs_ref[rows, :]
        cmat = cs_ref[rows, :]
        xmat = xs_ref[rows, :]
        if s_b is not None:
            acc_ref[rows, :] += _dot(cmat.astype(bf16), s_b.astype(bf16)) * expand(ers, exp_b)
        if c > 0 or not has_init:
            tail = jnp.exp(rsum[0:1, :] - rsum) * dt
            xw = (xmat * expand(tail, exp_b)).astype(bf16)
            upd = _dot(bmat.T.astype(bf16), xw)
            if s_b is None:
                s_b = upd
            else:
                s_b = s_b * expand(jnp.broadcast_to(ers[0:1, :], (V7X_SUBLANES, V7X_LANES)),
                                   exp_b)[0:1, :] + upd
    if not has_init:
        for hh in range(H):
            sf_ref[0, 0, hh] = s_f[:, hh * SSD_P:(hh + 1) * SSD_P]
            sb_ref[0, 0, hh] = s_b[:, hh * SSD_P:(hh + 1) * SSD_P]
    yg = (acc_ref[...] + xs_ref[...] * dvec_ref[...]) * _silu(z_ref[...].astype(f32))
    yn_ref[g] = yg

    @pl.when(g == SSD_GROUPS - 1)
    def _():
        ssq = None
        for gg in range(SSD_GROUPS):
            y = yn_ref[gg]
            s = jnp.sum(y * y, axis=-1, keepdims=True)
            ssq = s if ssq is None else ssq + s
        inv = lax.rsqrt(ssq * (1.0 / D_SSD) + EPS)
        for gg in range(SSD_GROUPS):
            cols = pl.ds(gg * SSD_GW, SSD_GW)
            y_ref[:, cols] = (yn_ref[gg] * inv * ng_ref[:, cols]).astype(bf16)


def ssd_branch(proj, bufs, row0, n_seq, seq_len, conv_w, conv_b, prm, dvec, ng, init, layer):
    has_init = init is not None
    rb0 = row0 // seq_len
    xoff = OFF_SSD_XBC
    boff = OFF_SSD_XBC + D_SSD
    coff = boff + SSD_GROUPS * SSD_N
    in_specs = [pl.BlockSpec((seq_len, SSD_GW), lambda s, g: (rb0 + s, OFF_SSD_Z // SSD_GW + g)),
                pl.BlockSpec((seq_len, SSD_GW), lambda s, g: (rb0 + s, xoff // SSD_GW + g)),
                pl.BlockSpec((seq_len, SSD_N), lambda s, g: (rb0 + s, boff // SSD_N + g)),
                pl.BlockSpec((seq_len, SSD_N), lambda s, g: (rb0 + s, coff // SSD_N + g)),
                pl.BlockSpec((seq_len, V7X_LANES), lambda s, g: (rb0 + s, OFF_SSD_DT // V7X_LANES + g)),
                pl.BlockSpec((SSD_CONV, SSD_GW), lambda s, g: (0, g)),
                pl.BlockSpec((1, SSD_GW), lambda s, g: (0, g)),
                pl.BlockSpec((SSD_CONV, SSD_N), lambda s, g: (0, D_SSD // SSD_N + g)),
                pl.BlockSpec((1, SSD_N), lambda s, g: (0, D_SSD // SSD_N + g)),
                pl.BlockSpec((SSD_CONV, SSD_N), lambda s, g: (0, D_SSD // SSD_N + SSD_GROUPS + g)),
                pl.BlockSpec((1, SSD_N), lambda s, g: (0, D_SSD // SSD_N + SSD_GROUPS + g)),
                pl.BlockSpec((1, V7X_SUBLANES, V7X_LANES), lambda s, g: (g, 0, 0)),
                pl.BlockSpec((1, SSD_GW), lambda s, g: (0, g)),
                pl.BlockSpec((1, D_SSD), lambda s, g: (0, 0))]
    args = [proj, proj, proj, proj, proj, conv_w, conv_b, conv_w, conv_b, conv_w, conv_b, prm, dvec, ng]
    state_spec = pl.BlockSpec((1, 1, SSD_HPG, SSD_N, SSD_P), lambda s, g: (s, layer, g, 0, 0))
    y_spec = pl.BlockSpec((seq_len, D_SSD), lambda s, g: (rb0 + s, 0))
    if has_init:
        in_specs += [state_spec, state_spec]
        args += [init[0], init[1]]
        out_specs, out_shape = y_spec, _BRANCH_BUF
    else:
        st = jax.ShapeDtypeStruct((n_seq, DEPTH, SSD_HEADS, SSD_N, SSD_P), f32)
        out_specs, out_shape = [y_spec, state_spec, state_spec], [_BRANCH_BUF, st, st]
    scratch = [pltpu.VMEM((seq_len + 2 * V7X_SUBLANES, SSD_GW), f32),
               pltpu.VMEM((seq_len, SSD_GW), f32),
               pltpu.VMEM((seq_len, SSD_N), f32),
               pltpu.VMEM((seq_len, SSD_N), f32),
               pltpu.VMEM((seq_len, SSD_GW), f32),
               pltpu.VMEM((seq_len, V7X_LANES), f32),
               pltpu.VMEM((SSD_GROUPS, seq_len, SSD_GW), f32),
               pltpu.VMEM((SSD_N, SSD_GW), f32)]
    return _inplace_call(functools.partial(_ssd_kernel, seq_len=seq_len, has_init=has_init), bufs,
                         (n_seq, SSD_GROUPS), in_specs, args, out_specs, out_shape, scratch, "ssd_branch")


def _head_rmsnorm(x, g):
    return x * lax.rsqrt(jnp.mean(x * x, axis=-1, keepdims=True) + EPS) * g


CTX_HB = 4
CTX_W = CTX_HB * NA_HD


def _ctx_attn_kernel(q_ref, k_ref, v_ref, qg_ref, kg_ref, y_ref, ko_ref, vo_ref):
    scale = NA_HD ** -0.5
    vo_ref[0, 0] = v_ref[...].astype(f32)
    for h in range(CTX_HB):
        cols = pl.ds(h * NA_HD, NA_HD)
        q = _head_rmsnorm(q_ref[:, cols].astype(f32), qg_ref[...])
        k = _head_rmsnorm(k_ref[:, cols].astype(f32), kg_ref[...])
        ko_ref[0, 0, :, cols] = k
        s = _dot_nt(q.astype(bf16), k.astype(bf16)) * scale
        p = jnp.exp(s - jnp.max(s, axis=-1, keepdims=True))
        o = _dot(p.astype(bf16), v_ref[:, cols]) / jnp.sum(p, axis=-1, keepdims=True)
        y_ref[:, cols] = o.astype(bf16)


def context_attention(proj, bufs, row0, qg, kg, layer):
    rb0 = row0 // SEQ
    spec = lambda off: pl.BlockSpec((SEQ, CTX_W), lambda s, hb: (rb0 + s, off // CTX_W + hb))
    gspec = pl.BlockSpec((1, NA_HD), lambda s, hb: (0, 0))
    kv_spec = pl.BlockSpec((1, 1, SEQ, CTX_W), lambda s, hb: (s, layer, 0, hb))
    kv_shape = jax.ShapeDtypeStruct((BATCH, DEPTH, SEQ, NA_W), f32)
    return _inplace_call(
        _ctx_attn_kernel, bufs, (BATCH, NA_HEADS // CTX_HB),
        [spec(OFF_NA_Q), spec(OFF_NA_K), spec(OFF_NA_V), gspec, gspec], [proj, proj, proj, qg, kg],
        [pl.BlockSpec((SEQ, CTX_W), lambda s, hb: (rb0 + s, hb)), kv_spec, kv_spec],
        [_BRANCH_BUF, kv_shape, kv_shape], [], "context_attention")


NA_ROWS = DEC_SEQ // GRID_W
NA_NK = NA_WR * GRID_W


def _na_row_start(r):
    return min(max(r - NA_WR // 2, 0), NA_ROWS - NA_WR)


def _rope(x, cos, sin_signed):
    lane = lax.broadcasted_iota(jnp.int32, x.shape, 1)
    quarter = NA_HD // 4
    swapped = jnp.where((lane & (2 * quarter - 1)) < quarter,
                        pltpu.roll(x, NA_HD - quarter, axis=1), pltpu.roll(x, quarter, axis=1))
    return x * cos + swapped * sin_signed


def _na_row_groups():
    groups, r = [], 0
    while r < NA_ROWS:
        n = 1
        while r + n < NA_ROWS and _na_row_start(r + n) == _na_row_start(r):
            n += 1
        groups.append((r, n, _na_row_start(r)))
        r += n
    return groups


def _na_attn_kernel(q_ref, k_ref, v_ref, kc_ref, vc_ref, qg_ref, kg_ref, cos_ref, sin_ref,
                    bias_ref, valid_ref, y_ref, qs_ref, ks_ref, s_ref, p_ref, oc_ref, w_ref):
    scale = NA_HD ** -0.5
    cos = cos_ref[...]
    sin = sin_ref[...]
    qs_ref[...] = _rope(_head_rmsnorm(q_ref[...].astype(f32), qg_ref[...]), cos, sin).astype(bf16)
    ks_ref[...] = _rope(_head_rmsnorm(k_ref[...].astype(f32), kg_ref[...]), cos, sin).astype(bf16)
    groups = _na_row_groups()
    valid = valid_ref[...] > 0.0
    for r0, n, rs in groups:
        rows = pl.ds(r0 * GRID_W, n * GRID_W)
        d0 = r0 - rs
        kw = ks_ref[pl.ds(rs * GRID_W, NA_NK), :]
        s = (_dot_nt(qs_ref[rows, :], kw) * scale).reshape(n, GRID_W, NA_NK) + bias_ref[0, d0:d0 + n]
        s_ref[rows, :] = jnp.where(valid[None], s, -1e30).reshape(n * GRID_W, NA_NK)
    s_ctx = _dot_nt(qs_ref[...], kc_ref[0, 0].astype(bf16)) * scale
    s_loc = s_ref[...]
    m = jnp.maximum(jnp.max(s_loc, axis=-1, keepdims=True), jnp.max(s_ctx, axis=-1, keepdims=True))
    p_loc = jnp.exp(s_loc - m)
    p_ctx = jnp.exp(s_ctx - m)
    w_ref[...] = 1.0 / (jnp.sum(p_loc, axis=-1, keepdims=True) + jnp.sum(p_ctx, axis=-1, keepdims=True))
    p_ref[...] = p_loc.astype(bf16)
    oc_ref[...] = _dot(p_ctx.astype(bf16), vc_ref[0, 0].astype(bf16))
    for r0, n, rs in groups:
        rows = pl.ds(r0 * GRID_W, n * GRID_W)
        vw = v_ref[pl.ds(rs * GRID_W, NA_NK), :]
        y_ref[rows, :] = ((_dot(p_ref[rows, :], vw) + oc_ref[rows, :]) * w_ref[rows, :]).astype(bf16)


def neighbourhood_attention(proj, cache_k, cache_v, layer, qg, kg, cos, sin, bias, valid):
    spec = lambda off: pl.BlockSpec((DEC_SEQ, NA_HD), lambda b, h: (b, off // NA_HD + h))
    cspec = pl.BlockSpec((1, 1, PAST_LEN, NA_HD), lambda b, h: (b, layer, 0, h))
    gspec = pl.BlockSpec((1, NA_HD), lambda b, h: (0, 0))
    tspec = pl.BlockSpec((DEC_SEQ, NA_HD), lambda b, h: (0, 0))
    return pl.pallas_call(
        _na_attn_kernel,
        grid=(DEC_BATCH, NA_HEADS),
        in_specs=[spec(OFF_NA_Q), spec(OFF_NA_K), spec(OFF_NA_V), cspec, cspec, gspec, gspec,
                  tspec, tspec,
                  pl.BlockSpec((1, NA_WR, GRID_W, NA_NK), lambda b, h: (h, 0, 0, 0)),
                  pl.BlockSpec((GRID_W, NA_NK), lambda b, h: (0, 0))],
        out_specs=pl.BlockSpec((DEC_SEQ, NA_HD), lambda b, h: (b, h)),
        out_shape=_BRANCH_BUF,
        scratch_shapes=[pltpu.VMEM((DEC_SEQ, NA_HD), bf16), pltpu.VMEM((DEC_SEQ, NA_HD), bf16),
                        pltpu.VMEM((DEC_SEQ, NA_NK), f32), pltpu.VMEM((DEC_SEQ, NA_NK), bf16),
                        pltpu.VMEM((DEC_SEQ, NA_HD), f32), pltpu.VMEM((DEC_SEQ, 1), f32)],
        compiler_params=_params(2), name="neighbourhood_attention",
    )(proj, proj, proj, cache_k, cache_v, qg, kg, cos, sin, bias, valid)


def _split_w_in(w_in):
    dt0, na0 = IN_OFFSETS[7], IN_OFFSETS[8]
    head = w_in.astype(bf16)
    pieces = [head[..., HEAD_COLS:dt0], head[..., na0:]]
    dt = head[..., dt0:na0]
    for gidx in range(SSD_GROUPS):
        pieces.append(dt[..., gidx * SSD_HPG:(gidx + 1) * SSD_HPG])
        pieces.append(dt[..., SSD_HEADS + gidx * SSD_HPG:SSD_HEADS + (gidx + 1) * SSD_HPG])
        pieces.append(jnp.zeros((DEPTH, D_MODEL, V7X_LANES - 2 * SSD_HPG), bf16))
    used = OFF_SSD_DT + SSD_GROUPS * V7X_LANES
    pieces.append(jnp.zeros((DEPTH, D_MODEL, D_IN_PAD - used), bf16))
    return head, jnp.concatenate(pieces, axis=-1)


def _group_lanes(v):
    rows = []
    for gidx in range(SSD_GROUPS):
        sl = slice(gidx * SSD_HPG, (gidx + 1) * SSD_HPG)
        rows.append(jnp.concatenate([v[0, sl], v[1, sl], jnp.zeros((V7X_LANES - 2 * SSD_HPG,), f32)]))
    return jnp.stack(rows)


def _rope_tables():
    t = np.arange(DEC_SEQ)
    quarter = NA_HD // 4
    inv = ROPE_BASE ** (-np.arange(quarter, dtype=np.float32) / quarter)
    ang_r = (t // GRID_W).astype(np.float32)[:, None] * inv
    ang_c = (t % GRID_W).astype(np.float32)[:, None] * inv
    cos = np.concatenate([np.cos(ang_r), np.cos(ang_r), np.cos(ang_c), np.cos(ang_c)], axis=1)
    sin = np.concatenate([-np.sin(ang_r), np.sin(ang_r), -np.sin(ang_c), np.sin(ang_c)], axis=1)
    return jnp.asarray(cos, f32), jnp.asarray(sin, f32)


def _na_tables(rpb):
    cq = np.arange(GRID_W)
    kc = np.tile(np.arange(GRID_W), NA_WR)
    col_start = np.clip(cq - NA_WC // 2, 0, GRID_W - NA_WC)
    valid = (kc[None, :] >= col_start[:, None]) & (kc[None, :] < col_start[:, None] + NA_WC)
    col_off = np.clip(cq[None, :] - cq[:, None], 1 - NA_WC, NA_WC - 1) + NA_WC - 1
    onehot = (col_off[None, :, :] == np.arange(2 * NA_WC - 1)[:, None, None]).astype(np.float32)
    toep = jnp.einsum('hic,cqk->hiqk', rpb.astype(f32), jnp.asarray(onehot), precision=lax.Precision.HIGHEST)
    tables = []
    for d in range(NA_WR):
        rows = toep[:, NA_WR - 1 - d:2 * NA_WR - 1 - d]
        tables.append(rows.transpose(0, 2, 1, 3).reshape(NA_HEADS, GRID_W, NA_NK))
    return jnp.stack(tables, axis=1), jnp.asarray(valid, f32)


def kernel(x_prompt, x_sample, cache_na_k, cache_na_v, state_lru_f, state_lru_b,
           state_ret_f, state_ret_b, state_ssd_f, state_ssd_b, c, c_ctx,
           norm1_g, norm2_g, w_ada, b_ada, w_in, w_gate, b_gate, w_branch, w_out,
           lru_conv_w, lru_conv_b, lru_wa, lru_ba, lru_wx, lru_bx, lru_lambda,
           ret_gn_g, ssd_conv_w, ssd_conv_b, ssd_a_log, ssd_dt_bias, ssd_d, ssd_norm_g,
           na_q_g, na_k_g, na_rpb, ffn_w_up, ffn_conv_w, ffn_conv_b, ffn_w_down):
    xs0 = x_sample.reshape(T_SAMPLE, D_MODEL)
    xp0 = x_prompt.reshape(T_PROMPT, D_MODEL)
    n_ptiles = N_TILES - N_SAMPLE_TILES
    cond = jnp.concatenate([c, c_ctx[None, :], jnp.zeros((N_COND_PAD - N_COND, D_MODEL), f32)], axis=0)
    mod_all = ada_modulation(cond, w_ada, b_ada)
    cos, sin = _rope_tables()
    hh = jnp.arange(RET_HEADS, dtype=f32)
    ret_la = jnp.stack([jnp.log1p(-jnp.exp2(-5.0 - hh)), jnp.log1p(-jnp.exp2(-5.5 - hh))], axis=1)
    cache_k = cache_na_k.reshape(DEC_BATCH, DEPTH, PAST_LEN, NA_W)
    cache_v = cache_na_v.reshape(DEC_BATCH, DEPTH, PAST_LEN, NA_W)
    lru_init = (state_lru_f.reshape(DEC_BATCH, DEPTH, 1, D_RNN), state_lru_b.reshape(DEC_BATCH, DEPTH, 1, D_RNN))

    w_head_b, w_tail_b = _split_w_in(w_in)
    w_branch_b, w_out_b, w_down_b = w_branch.astype(bf16), w_out.astype(bf16), ffn_w_down.astype(bf16)

    new = {k: None for k in ("k", "v", "lru_f", "lru_b", "ret_f", "ret_b", "ssd_f", "ssd_b")}
    for l in range(DEPTH):
        mod = mod_all[l].reshape(N_COND_PAD, 1, N_MOD * D_MODEL)
        g1 = norm1_g[l][None, :]
        if l == 0:
            proj, xn = in_projection(xs0, 0, 0, N_SAMPLE_TILES, g1, mod, w_head_b, w_tail_b, l)
            proj, xn = in_projection(xp0, 0, N_SAMPLE_TILES, n_ptiles, g1, mod, w_head_b, w_tail_b, l, (proj, xn))
        else:
            proj, xn = in_projection(x, 0, 0, N_TILES, g1, mod, w_head_b, w_tail_b, l)

        w4 = jnp.concatenate([lru_wa[l, 0], lru_wx[l, 0], lru_wa[l, 1], lru_wx[l, 1]], axis=-1).astype(bf16)
        b4 = jnp.concatenate([lru_ba[l, 0].reshape(LRU_BLOCKS, 1, LRU_BS), lru_bx[l, 0].reshape(LRU_BLOCKS, 1, LRU_BS),
                              lru_ba[l, 1].reshape(LRU_BLOCKS, 1, LRU_BS), lru_bx[l, 1].reshape(LRU_BLOCKS, 1, LRU_BS)],
                             axis=-1)
        lru_args = (lru_conv_w[l], lru_conv_b[l][None, :], w4, b4, lru_lambda[l])
        y_lru = lru_branch(proj, (None,), 0, DEC_BATCH, DEC_SEQ, *lru_args, lru_init, l)
        y_lru, new["lru_f"], new["lru_b"] = lru_branch(
            proj, (y_lru, new["lru_f"], new["lru_b"]), T_SAMPLE, BATCH, SEQ, *lru_args, None, l)

        gn = ret_gn_g[l][None, :]
        y_ret = retention_branch(proj, (None,), 0, DEC_BATCH, DEC_SEQ, ret_la, gn, (state_ret_f, state_ret_b), l)
        y_ret, new["ret_f"], new["ret_b"] = retention_branch(
            proj, (y_ret, new["ret_f"], new["ret_b"]), T_SAMPLE, BATCH, SEQ, ret_la, gn, None, l)

        prm = jnp.stack([_group_lanes(ssd_dt_bias[l]), _group_lanes(ssd_a_log[l])], axis=1)
        prm = jnp.concatenate([prm, jnp.zeros((SSD_GROUPS, V7X_SUBLANES - 2, V7X_LANES), f32)], axis=1)
        dvec = jnp.repeat(ssd_d[l], SSD_P)[None, :]
        ssd_args = (ssd_conv_w[l], ssd_conv_b[l][None, :], prm, dvec, ssd_norm_g[l][None, :])
        y_ssd = ssd_branch(proj, (None,), 0, DEC_BATCH, DEC_SEQ, *ssd_args, (state_ssd_f, state_ssd_b), l)
        y_ssd, new["ssd_f"], new["ssd_b"] = ssd_branch(
            proj, (y_ssd, new["ssd_f"], new["ssd_b"]), T_SAMPLE, BATCH, SEQ, *ssd_args, None, l)

        qg = na_q_g[l][None, :]
        kg = na_k_g[l][None, :]
        bias, valid = _na_tables(na_rpb[l])
        y_na = neighbourhood_attention(proj, cache_k, cache_v, l, qg, kg, cos, sin, bias, valid)
        y_na, new["k"], new["v"] = context_attention(proj, (y_na, new["k"], new["v"]), T_SAMPLE, qg, kg, l)

        merged = merge_branches(xn, (y_lru, y_ret, y_ssd, y_na), w_gate, b_gate[l][None, :], w_branch_b, l)
        if l == 0:
            x = residual_projection(merged, w_out_b, l, xs0, mod, 2, 0, N_SAMPLE_TILES)
            x = residual_projection(merged, w_out_b, l, xp0, mod, 2, N_SAMPLE_TILES, n_ptiles, x_tile0=0, buf=x)
        else:
            x = residual_projection(merged, w_out_b, l, x, mod, 2)
        hmid = ffn_up(x, norm2_g[l][None, :], mod, ffn_w_up, l, ffn_conv_w[l], ffn_conv_b[l][None, :])
        if l < DEPTH - 1:
            x = residual_projection(hmid, w_down_b, l, x, mod, 5)
        else:
            y_sample = residual_projection(hmid, w_down_b, l, x, mod, 5, 0, N_SAMPLE_TILES, out_rows=T_SAMPLE)
            y_prompt = residual_projection(hmid, w_down_b, l, x, mod, 5, N_SAMPLE_TILES, n_ptiles,
                                           out_tile0=0, out_rows=T_PROMPT)

    kv_shape = (BATCH, DEPTH, SEQ, NA_HEADS, NA_HD)
    return (y_prompt.reshape(BATCH, SEQ, D_MODEL), y_sample.reshape(DEC_BATCH, DEC_SEQ, D_MODEL),
            new["k"].reshape(kv_shape), new["v"].reshape(kv_shape),
            new["lru_f"].reshape(BATCH, DEPTH, D_RNN), new["lru_b"].reshape(BATCH, DEPTH, D_RNN),
            new["ret_f"], new["ret_b"], new["ssd_f"], new["ssd_b"])
```

```python
import functools
import math

import jax
import jax.numpy as jnp
import numpy as np
from jax import lax
from jax.experimental import pallas as pl
from jax.experimental.pallas import tpu as pltpu

D_MODEL = 2048
BATCH = 16
SEQ = 256
DEPTH = 2
DEC_BATCH = 8
DEC_SEQ = 1024
PAST_LEN = 256
GRID_W = 64
EPS = 1e-6
N_BRANCH = 4
BRANCH_W = 1024
N_MOD = 6
D_RNN = 1024
LRU_BLOCKS = 8
LRU_BS = D_RNN // LRU_BLOCKS
LRU_CONV = 4
LRU_C = 8.0
RET_HEADS = 4
RET_DK = 128
RET_DV = 256
SSD_HEADS = 16
SSD_P = 64
SSD_N = 128
SSD_GROUPS = 2
SSD_CONV = 4
D_SSD = SSD_HEADS * SSD_P
SSD_CONV_CH = D_SSD + 2 * SSD_GROUPS * SSD_N
NA_HEADS = 8
NA_HD = 128
NA_W = NA_HEADS * NA_HD
NA_WR = 8
NA_WC = 16
ROPE_BASE = 10000.0
D_FF = 5632
FFN_CONV = 3
IN_SIZES = (D_RNN, D_RNN,
            RET_HEADS * RET_DK, RET_HEADS * RET_DK, RET_HEADS * RET_DV, RET_HEADS * RET_DV,
            D_SSD, SSD_CONV_CH, 2 * SSD_HEADS,
            NA_W, NA_W, NA_W)
D_IN = sum(IN_SIZES)
IN_OFFSETS = tuple(int(s) for s in np.cumsum(IN_SIZES)[:-1])

V7X_LANES = 128
V7X_SUBLANES = 8
V7X_VMEM_LIMIT_BYTES = 56 * 1024 * 1024

TM = 1024
T_SAMPLE = DEC_BATCH * DEC_SEQ
T_PROMPT = BATCH * SEQ
T_ALL = T_SAMPLE + T_PROMPT
N_SAMPLE_TILES = T_SAMPLE // TM
N_TILES = T_ALL // TM
N_COND = DEC_BATCH + 1
N_COND_PAD = 16

SSD_HPG = SSD_HEADS // SSD_GROUPS
OFF_LRU_X = 0
OFF_LRU_G = OFF_LRU_X + D_RNN
OFF_RET_Q = OFF_LRU_G + D_RNN
OFF_RET_K = OFF_RET_Q + RET_HEADS * RET_DK
OFF_RET_V = OFF_RET_K + RET_HEADS * RET_DK
OFF_RET_G = OFF_RET_V + RET_HEADS * RET_DV
OFF_SSD_Z = OFF_RET_G + RET_HEADS * RET_DV
OFF_SSD_XBC = OFF_SSD_Z + D_SSD
OFF_NA_Q = OFF_SSD_XBC + SSD_CONV_CH
OFF_NA_K = OFF_NA_Q + NA_W
OFF_NA_V = OFF_NA_K + NA_W
OFF_SSD_DT = OFF_NA_V + NA_W
PROJ_TN = 1024
D_IN_PAD = -(-(OFF_SSD_DT + SSD_GROUPS * V7X_LANES) // PROJ_TN) * PROJ_TN
N_HEAD_TILES = OFF_NA_Q // PROJ_TN
HEAD_COLS = N_HEAD_TILES * PROJ_TN

RET_CHUNK = 256
SSD_CHUNK = 256
SSD_TB = 128

f32 = jnp.float32
bf16 = jnp.bfloat16

_ARB = "arbitrary"


def _params(n_axes):
    return pltpu.CompilerParams(dimension_semantics=(_ARB,) * n_axes,
                                vmem_limit_bytes=V7X_VMEM_LIMIT_BYTES)


def _mod_spec(k, tile0=0):
    return pl.BlockSpec((1, 1, D_MODEL), lambda i, j: (jnp.minimum(tile0 + i, N_SAMPLE_TILES), 0, k))


def _dot(a, b):
    return jnp.dot(a, b, preferred_element_type=f32)


def _dot_nt(a, b):
    return lax.dot_general(a, b, (((1,), (1,)), ((), ())), preferred_element_type=f32)


def _sigmoid(x):
    return 0.5 * jnp.tanh(0.5 * x) + 0.5


def _silu(x):
    return x * _sigmoid(x)


def _gelu_tanh(x):
    return 0.5 * x * (1.0 + jnp.tanh(math.sqrt(2.0 / math.pi) * (x + 0.044715 * (x * x * x))))


def _softplus(x):
    return jnp.maximum(x, 0.0) + jnp.log1p(jnp.exp(-jnp.abs(x)))


def _drop_refs(body, n, *refs):
    body(*refs[n:])


def _inplace_call(body, bufs, grid, in_specs, args, out_specs, out_shape, scratch, name):
    held = [(k, b) for k, b in enumerate(bufs) if b is not None]
    if held:
        body = functools.partial(_drop_refs, body, len(held))
        in_specs = [pl.BlockSpec(memory_space=pl.ANY)] * len(held) + list(in_specs)
        args = [b for _, b in held] + list(args)
    aliases = {pos: k for pos, (k, _) in enumerate(held)}
    return pl.pallas_call(
        body, grid=grid, in_specs=in_specs, out_specs=out_specs, out_shape=out_shape,
        scratch_shapes=scratch, input_output_aliases=aliases,
        compiler_params=_params(len(grid)), name=name,
    )(*args)


_BRANCH_BUF = jax.ShapeDtypeStruct((T_ALL, BRANCH_W), bf16)


ADA_TN = 1024


def _ada_kernel(c_ref, w_ref, b_ref, o_ref):
    c = _silu(c_ref[...]).astype(bf16)
    o_ref[0] = _dot(c, w_ref[0].astype(bf16)) + b_ref[0]


def ada_modulation(cond, w_ada, b_ada):
    n = N_MOD * D_MODEL
    return pl.pallas_call(
        _ada_kernel,
        grid=(DEPTH, n // ADA_TN),
        in_specs=[pl.BlockSpec((N_COND_PAD, D_MODEL), lambda l, j: (0, 0)),
                  pl.BlockSpec((1, D_MODEL, ADA_TN), lambda l, j: (l, 0, j)),
                  pl.BlockSpec((1, 1, ADA_TN), lambda l, j: (l, 0, j))],
        out_specs=pl.BlockSpec((1, N_COND_PAD, ADA_TN), lambda l, j: (l, 0, j)),
        out_shape=jax.ShapeDtypeStruct((DEPTH, N_COND_PAD, n), f32),
        compiler_params=_params(2),
        name="ada_modulation",
    )(cond, w_ada, b_ada.reshape(DEPTH, 1, n))


NORM_ROWS = 128


def _modulated_norm(x_ref, g_ref, sc_ref, sh_ref, xn_ref):
    g = g_ref[...]
    sc = 1.0 + sc_ref[0]
    sh = sh_ref[0]

    def body(r, carry):
        rows = pl.ds(pl.multiple_of(r * NORM_ROWS, NORM_ROWS), NORM_ROWS)
        x = x_ref[rows, :]
        y = x * lax.rsqrt(jnp.mean(x * x, axis=-1, keepdims=True) + EPS)
        xn_ref[rows, :] = ((y * g) * sc + sh).astype(bf16)
        return carry

    lax.fori_loop(0, TM // NORM_ROWS, body, 0)


def _in_proj_kernel(x_ref, g_ref, sc_ref, sh_ref, wh_ref, wt_ref, o_ref, xn_ref):
    j = pl.program_id(1)

    @pl.when(j == 0)
    def _():
        _modulated_norm(x_ref, g_ref, sc_ref, sh_ref, xn_ref)

    @pl.when(j < N_HEAD_TILES)
    def _():
        o_ref[...] = _dot(xn_ref[...], wh_ref[0]).astype(bf16)

    @pl.when(j >= N_HEAD_TILES)
    def _():
        o_ref[...] = _dot(xn_ref[...], wt_ref[0]).astype(bf16)


def in_projection(x, x_tile0, tile0, n_tiles, g, mod, w_head, w_tail, layer, bufs=(None, None)):
    n = D_IN_PAD
    n_tail = w_tail.shape[2] // PROJ_TN
    return _inplace_call(
        _in_proj_kernel, bufs, (n_tiles, n // PROJ_TN),
        [pl.BlockSpec((TM, D_MODEL), lambda i, j: (x_tile0 + i, 0)),
         pl.BlockSpec((1, D_MODEL), lambda i, j: (0, 0)),
         _mod_spec(1, tile0), _mod_spec(0, tile0),
         pl.BlockSpec((1, D_MODEL, PROJ_TN), lambda i, j: (layer, 0, jnp.minimum(j, N_HEAD_TILES - 1))),
         pl.BlockSpec((1, D_MODEL, PROJ_TN),
                      lambda i, j: (layer, 0, jnp.clip(j - N_HEAD_TILES, 0, n_tail - 1)))],
        [x, g, mod, mod, w_head, w_tail],
        [pl.BlockSpec((TM, PROJ_TN), lambda i, j: (tile0 + i, j)),
         pl.BlockSpec((TM, D_MODEL), lambda i, j: (tile0 + i, 0))],
        [jax.ShapeDtypeStruct((T_ALL, n), bf16), jax.ShapeDtypeStruct((T_ALL, D_MODEL), bf16)],
        [], "in_projection")


MERGE_TN = 256


def _merge_kernel(xn_ref, *refs):
    y_refs, wg_refs, bg_refs = refs[0:4], refs[4:8], refs[8:12]
    wb_ref, o_ref = refs[12], refs[13]
    xn = xn_ref[...]
    acc = None
    for n in range(N_BRANCH):
        gate = _sigmoid(_dot(xn, wg_refs[n][0].astype(bf16)) + bg_refs[n][...])
        term = gate * _dot(y_refs[n][...], wb_ref[0, n])
        acc = term if acc is None else acc + term
    o_ref[...] = acc.astype(bf16)


def merge_branches(xn, ys, w_gate, b_gate, w_branch, layer):
    nj = D_MODEL // MERGE_TN
    y_spec = pl.BlockSpec((TM, BRANCH_W), lambda i, j: (i, 0))
    wg_specs = [pl.BlockSpec((1, D_MODEL, MERGE_TN), lambda i, j, n=n: (layer, 0, n * nj + j))
                for n in range(N_BRANCH)]
    bg_specs = [pl.BlockSpec((1, MERGE_TN), lambda i, j, n=n: (0, n * nj + j)) for n in range(N_BRANCH)]
    return pl.pallas_call(
        _merge_kernel,
        grid=(N_TILES, nj),
        in_specs=[pl.BlockSpec((TM, D_MODEL), lambda i, j: (i, 0)),
                  y_spec, y_spec, y_spec, y_spec, *wg_specs, *bg_specs,
                  pl.BlockSpec((1, N_BRANCH, BRANCH_W, MERGE_TN), lambda i, j: (layer, 0, 0, j))],
        out_specs=pl.BlockSpec((TM, MERGE_TN), lambda i, j: (i, j)),
        out_shape=jax.ShapeDtypeStruct((T_ALL, D_MODEL), bf16),
        compiler_params=_params(2),
        name="merge_branches",
    )(xn, *ys, *([w_gate] * N_BRANCH), *([b_gate] * N_BRANCH), w_branch)


RES_TN = 512


def _residual_kernel(a_ref, w_ref, x_ref, gv_ref, o_ref):
    o_ref[...] = x_ref[...] + gv_ref[0] * _dot(a_ref[...], w_ref[0])


def residual_projection(a, w, layer, x, mod, k_mod, tile0=0, n_tiles=N_TILES, x_tile0=None, out_tile0=None,
                        out_rows=T_ALL, buf=None):
    kdim = a.shape[1]
    x_tile0 = tile0 if x_tile0 is None else x_tile0
    out_tile0 = tile0 if out_tile0 is None else out_tile0
    return _inplace_call(
        _residual_kernel, (buf,), (n_tiles, D_MODEL // RES_TN),
        [pl.BlockSpec((TM, kdim), lambda i, j: (tile0 + i, 0)),
         pl.BlockSpec((1, kdim, RES_TN), lambda i, j: (layer, 0, j)),
         pl.BlockSpec((TM, RES_TN), lambda i, j: (x_tile0 + i, j)),
         pl.BlockSpec((1, 1, RES_TN),
                      lambda i, j: (jnp.minimum(tile0 + i, N_SAMPLE_TILES), 0, k_mod * (D_MODEL // RES_TN) + j))],
        [a, w, x, mod],
        pl.BlockSpec((TM, RES_TN), lambda i, j: (out_tile0 + i, j)),
        jax.ShapeDtypeStruct((out_rows, D_MODEL), f32), [], "residual_projection")


FFN_TN = 512


def _ffn_up_kernel(x_ref, g_ref, sc_ref, sh_ref, wa_ref, wv_ref, cw_ref, cb_ref, o_ref, xn_ref, pad_ref):
    i = pl.program_id(0)

    @pl.when(pl.program_id(1) == 0)
    def _():
        _modulated_norm(x_ref, g_ref, sc_ref, sh_ref, xn_ref)
        zeros = jnp.zeros((V7X_SUBLANES, FFN_TN), f32)
        pad_ref[pl.ds(0, V7X_SUBLANES), :] = zeros
        pad_ref[pl.ds(V7X_SUBLANES + TM, V7X_SUBLANES), :] = zeros

    xn = xn_ref[...]
    a = _dot(xn, wa_ref[0].astype(bf16))
    pad_ref[pl.ds(V7X_SUBLANES, TM), :] = a
    seq_len = jnp.where(i < N_SAMPLE_TILES, DEC_SEQ, SEQ)
    pos = lax.broadcasted_iota(jnp.int32, (TM, 1), 0) & (seq_len - 1)
    prev = jnp.where(pos == 0, 0.0, pad_ref[pl.ds(V7X_SUBLANES - 1, TM), :])
    nxt = jnp.where(pos == seq_len - 1, 0.0, pad_ref[pl.ds(V7X_SUBLANES + 1, TM), :])
    cw = cw_ref[...]
    conv = cb_ref[...] + prev * cw[0:1, :] + a * cw[1:2, :] + nxt * cw[2:3, :]
    o_ref[...] = (_gelu_tanh(conv) * _dot(xn, wv_ref[0].astype(bf16))).astype(bf16)


def ffn_up(x, g, mod, w_up, layer, conv_w, conv_b):
    nj = D_FF // FFN_TN
    return pl.pallas_call(
        _ffn_up_kernel,
        grid=(N_TILES, nj),
        in_specs=[pl.BlockSpec((TM, D_MODEL), lambda i, j: (i, 0)),
                  pl.BlockSpec((1, D_MODEL), lambda i, j: (0, 0)),
                  _mod_spec(4), _mod_spec(3),
                  pl.BlockSpec((1, D_MODEL, FFN_TN), lambda i, j: (layer, 0, j)),
                  pl.BlockSpec((1, D_MODEL, FFN_TN), lambda i, j: (layer, 0, nj + j)),
                  pl.BlockSpec((FFN_CONV, FFN_TN), lambda i, j: (0, j)),
                  pl.BlockSpec((1, FFN_TN), lambda i, j: (0, j))],
        out_specs=pl.BlockSpec((TM, FFN_TN), lambda i, j: (i, j)),
        out_shape=jax.ShapeDtypeStruct((T_ALL, D_FF), bf16),
        scratch_shapes=[pltpu.VMEM((TM, D_MODEL), bf16),
                        pltpu.VMEM((TM + 2 * V7X_SUBLANES, FFN_TN), f32)],
        compiler_params=_params(2),
        name="ffn_up",
    )(x, g, mod, mod, w_up, w_up, conv_w, conv_b)


LRU_CB_LONG = D_RNN
LRU_CB_SHORT = D_RNN


def _lru_kernel(*refs, seq_len, has_init, cbw):
    if has_init:
        (x_ref, g_ref, cw_ref, cb_ref, w4_ref, b4_ref, lam_ref, h0f_ref, h0b_ref,
         y_ref, pad_ref, af_ref, uf_ref, ab_ref, ub_ref, hf_ref, hb_ref) = refs
    else:
        (x_ref, g_ref, cw_ref, cb_ref, w4_ref, b4_ref, lam_ref,
         y_ref, sf_ref, sb_ref, pad_ref, af_ref, uf_ref, ab_ref, ub_ref, hf_ref, hb_ref) = refs
    L = seq_len
    S = V7X_SUBLANES
    zeros = jnp.zeros((S, cbw), f32)
    pad_ref[pl.ds(0, S), :] = zeros
    pad_ref[pl.ds(S + L, S), :] = zeros
    pad_ref[pl.ds(S, L), :] = x_ref[...].astype(f32)
    cw = cw_ref[...]
    xc = cb_ref[...]
    for kk in range(LRU_CONV):
        xc = xc + pad_ref[pl.ds(S + kk - LRU_CONV // 2, L), :] * cw[kk:kk + 1, :]
    sp = _softplus(-lam_ref[...])
    for n in range(cbw // LRU_BS):
        cols = slice(n * LRU_BS, (n + 1) * LRU_BS)
        xcn = xc[:, cols]
        z = _dot(xcn.astype(bf16), w4_ref[n]) + b4_ref[n]
        for d, (a_ref, u_ref) in enumerate(((af_ref, uf_ref), (ab_ref, ub_ref))):
            r = _sigmoid(z[:, (2 * d) * LRU_BS:(2 * d + 1) * LRU_BS])
            ig = _sigmoid(z[:, (2 * d + 1) * LRU_BS:(2 * d + 2) * LRU_BS])
            log_a = (-LRU_C) * r * sp[d:d + 1, cols]
            a = jnp.exp(log_a)
            a_ref[:, cols] = a
            gain2 = -jnp.tanh(log_a) * (a * a + 1.0)
            gain = jnp.where(gain2 > 0.0, gain2 * lax.rsqrt(gain2), 0.0)
            u_ref[:, cols] = gain * (ig * xcn)

    if has_init:
        h0f = h0f_ref[0, 0]
        h0b = h0b_ref[0, 0]
    else:
        h0f = jnp.zeros((1, cbw), f32)
        h0b = jnp.zeros((1, cbw), f32)

    def step(i, carry):
        hf, hb = carry
        base_f = pl.multiple_of(i * S, S)
        base_b = pl.multiple_of(L - S - i * S, S)
        for r in range(S):
            tf = pl.ds(base_f + r, 1)
            tb = pl.ds(base_b + (S - 1 - r), 1)
            hf = af_ref[tf, :] * hf + uf_ref[tf, :]
            hb = ab_ref[tb, :] * hb + ub_ref[tb, :]
            hf_ref[tf, :] = hf
            hb_ref[tb, :] = hb
        return hf, hb

    hf, hb = lax.fori_loop(0, L // S, step, (h0f, h0b))
    if not has_init:
        sf_ref[0, 0] = hf
        sb_ref[0, 0] = hb
    y_ref[...] = ((hf_ref[...] + hb_ref[...]) * _gelu_tanh(g_ref[...].astype(f32))).astype(bf16)


def lru_branch(proj, bufs, row0, n_seq, seq_len, cw, cb, w4, b4, lam, init, layer):
    has_init = init is not None
    cbw = LRU_CB_LONG if seq_len == DEC_SEQ else LRU_CB_SHORT
    ncb = D_RNN // cbw
    nb = cbw // LRU_BS
    rb0 = row0 // seq_len
    in_specs = [pl.BlockSpec((seq_len, cbw), lambda s, c: (rb0 + s, OFF_LRU_X // cbw + c)),
                pl.BlockSpec((seq_len, cbw), lambda s, c: (rb0 + s, OFF_LRU_G // cbw + c)),
                pl.BlockSpec((LRU_CONV, cbw), lambda s, c: (0, c)),
                pl.BlockSpec((1, cbw), lambda s, c: (0, c)),
                pl.BlockSpec((nb, LRU_BS, 4 * LRU_BS), lambda s, c: (c, 0, 0)),
                pl.BlockSpec((nb, 1, 4 * LRU_BS), lambda s, c: (c, 0, 0)),
                pl.BlockSpec((2, cbw), lambda s, c: (0, c))]
    args = [proj, proj, cw, cb, w4, b4, lam]
    state_spec = pl.BlockSpec((1, 1, 1, cbw), lambda s, c: (s, layer, 0, c))
    y_spec = pl.BlockSpec((seq_len, cbw), lambda s, c: (rb0 + s, c))
    if has_init:
        in_specs += [state_spec, state_spec]
        args += [init[0], init[1]]
        out_specs, out_shape = y_spec, _BRANCH_BUF
    else:
        st = jax.ShapeDtypeStruct((n_seq, DEPTH, 1, D_RNN), f32)
        out_specs, out_shape = [y_spec, state_spec, state_spec], [_BRANCH_BUF, st, st]
    scratch = [pltpu.VMEM((seq_len + 2 * V7X_SUBLANES, cbw), f32)] + \
              [pltpu.VMEM((seq_len, cbw), f32) for _ in range(6)]
    return _inplace_call(functools.partial(_lru_kernel, seq_len=seq_len, has_init=has_init, cbw=cbw), bufs,
                         (n_seq, ncb), in_specs, args, out_specs, out_shape, scratch, "lru_branch")


def _ret_kernel(*refs, seq_len, has_init):
    if has_init:
        (la_ref, q_ref, k_ref, v_ref, g_ref, gn_ref, s0f_ref, s0b_ref, y_ref, acc_ref) = refs
    else:
        (la_ref, q_ref, k_ref, v_ref, g_ref, gn_ref, y_ref, sf_ref, sb_ref, acc_ref) = refs
    T = min(RET_CHUNK, seq_len)
    nc = seq_len // T
    tt = lax.broadcasted_iota(jnp.int32, (T, T), 0)
    ss = lax.broadcasted_iota(jnp.int32, (T, T), 1)
    diff = (tt - ss).astype(f32)
    tcol = lax.broadcasted_iota(jnp.int32, (T, 1), 0).astype(f32)
    scale = RET_DK ** -0.5
    for h in range(RET_HEADS):
        la_f = la_ref[h, 0]
        la_b = la_ref[h, 1]
        kcols = pl.ds(h * RET_DK, RET_DK)
        vcols = pl.ds(h * RET_DV, RET_DV)
        dsum = (jnp.where(tt >= ss, jnp.exp(la_f * diff), 0.0)
                + jnp.where(ss >= tt, jnp.exp(-la_b * diff), 0.0))

        def chunk(c):
            rows = pl.ds(c * T, T)
            q = q_ref[rows, kcols].astype(f32)
            ks = k_ref[rows, kcols].astype(f32) * scale
            v = v_ref[rows, vcols].astype(f32)
            return rows, q, ks, v

        s_f = s0f_ref[0, 0, h] if has_init else None
        for c in range(nc):
            rows, q, ks, v = chunk(c)
            scores = _dot_nt(q.astype(bf16), ks.astype(bf16)) * dsum
            y = _dot(scores.astype(bf16), v.astype(bf16))
            if s_f is not None:
                y = y + _dot((q * jnp.exp(la_f * (tcol + 1.0))).astype(bf16), s_f.astype(bf16))
            acc_ref[rows, :] = y
            if c < nc - 1 or not has_init:
                upd = _dot(ks.T.astype(bf16), (v * jnp.exp(la_f * (T - 1.0 - tcol))).astype(bf16))
                s_f = upd if s_f is None else jnp.exp(la_f * T) * s_f + upd
        s_b = s0b_ref[0, 0, h] if has_init else None
        for c in reversed(range(nc)):
            rows, q, ks, v = chunk(c)
            if s_b is not None:
                acc_ref[rows, :] += _dot((q * jnp.exp(la_b * (T - tcol))).astype(bf16), s_b.astype(bf16))
            if c > 0 or not has_init:
                upd = _dot(ks.T.astype(bf16), (v * jnp.exp(la_b * tcol)).astype(bf16))
                s_b = upd if s_b is None else jnp.exp(la_b * T) * s_b + upd
        if not has_init:
            sf_ref[0, 0, h] = s_f
            sb_ref[0, 0, h] = s_b
        y = acc_ref[...]
        mu = jnp.mean(y, axis=-1, keepdims=True)
        yc = y - mu
        var = jnp.mean(yc * yc, axis=-1, keepdims=True)
        y = yc * lax.rsqrt(var + EPS) * gn_ref[:, vcols]
        y_ref[:, vcols] = (y * _silu(g_ref[:, vcols].astype(f32))).astype(bf16)


def retention_branch(proj, bufs, row0, n_seq, seq_len, la, gn, init, layer):
    has_init = init is not None
    rb0 = row0 // seq_len
    qk_w, vg_w = RET_HEADS * RET_DK, RET_HEADS * RET_DV
    in_specs = [pl.BlockSpec(memory_space=pltpu.SMEM),
                pl.BlockSpec((seq_len, qk_w), lambda s: (rb0 + s, OFF_RET_Q // qk_w)),
                pl.BlockSpec((seq_len, qk_w), lambda s: (rb0 + s, OFF_RET_K // qk_w)),
                pl.BlockSpec((seq_len, vg_w), lambda s: (rb0 + s, OFF_RET_V // vg_w)),
                pl.BlockSpec((seq_len, vg_w), lambda s: (rb0 + s, OFF_RET_G // vg_w)),
                pl.BlockSpec((1, vg_w), lambda s: (0, 0))]
    args = [la, proj, proj, proj, proj, gn]
    state_spec = pl.BlockSpec((1, 1, RET_HEADS, RET_DK, RET_DV), lambda s: (s, layer, 0, 0, 0))
    y_spec = pl.BlockSpec((seq_len, vg_w), lambda s: (rb0 + s, 0))
    if has_init:
        in_specs += [state_spec, state_spec]
        args += [init[0], init[1]]
        out_specs, out_shape = y_spec, _BRANCH_BUF
    else:
        st = jax.ShapeDtypeStruct((n_seq, DEPTH, RET_HEADS, RET_DK, RET_DV), f32)
        out_specs, out_shape = [y_spec, state_spec, state_spec], [_BRANCH_BUF, st, st]
    return _inplace_call(functools.partial(_ret_kernel, seq_len=seq_len, has_init=has_init), bufs,
                         (n_seq,), in_specs, args, out_specs, out_shape,
                         [pltpu.VMEM((seq_len, RET_DV), f32)], "retention_branch")


SSD_GW = SSD_HPG * SSD_P


def _split3(x):
    h1 = x.astype(bf16)
    r1 = x - h1.astype(f32)
    h2 = r1.astype(bf16)
    h3 = (r1 - h2.astype(f32)).astype(bf16)
    return h1, h2, h3


def _dot_exact_rhs(m, x):
    h1, h2, h3 = _split3(x)
    return _dot(m, h1) + _dot(m, h2) + _dot(m, h3)


def _dot_exact_lhs(x, m):
    h1, h2, h3 = _split3(x)
    return _dot(h1, m) + _dot(h2, m) + _dot(h3, m)


def _conv4_silu(src_ref, pad_ref, cw_ref, cb_ref, L, width):
    S = V7X_SUBLANES
    zeros = jnp.zeros((S, width), f32)
    pad_ref[pl.ds(0, S), pl.ds(0, width)] = zeros
    pad_ref[pl.ds(S + L, S), pl.ds(0, width)] = zeros
    pad_ref[pl.ds(S, L), pl.ds(0, width)] = src_ref[...].astype(f32)
    cw = cw_ref[...]
    out = cb_ref[...]
    for kk in range(SSD_CONV):
        out = out + pad_ref[pl.ds(S + kk - SSD_CONV // 2, L), pl.ds(0, width)] * cw[kk:kk + 1, :]
    return _silu(out)


def _heads_to_lanes(s_ref, st_ref):
    for hh in range(SSD_HPG):
        st_ref[:, pl.ds(hh * SSD_P, SSD_P)] = s_ref[0, 0, hh]
    return st_ref[...]


def _ssd_kernel(*refs, seq_len, has_init):
    if has_init:
        (z_ref, x_ref, b_ref, c_ref, dt_ref, cwx_ref, cbx_ref, cwb_ref, cbb_ref, cwc_ref, cbc_ref,
         prm_ref, dvec_ref, ng_ref, s0f_ref, s0b_ref,
         y_ref, pad_ref, xs_ref, bs_ref, cs_ref, acc_ref, rb_ref, yn_ref, st_ref) = refs
    else:
        (z_ref, x_ref, b_ref, c_ref, dt_ref, cwx_ref, cbx_ref, cwb_ref, cbb_ref, cwc_ref, cbc_ref,
         prm_ref, dvec_ref, ng_ref,
         y_ref, sf_ref, sb_ref, pad_ref, xs_ref, bs_ref, cs_ref, acc_ref, rb_ref, yn_ref, st_ref) = refs
    g = pl.program_id(1)
    L = seq_len
    T = min(SSD_CHUNK, L)
    nc = L // T
    H = SSD_HPG
    xs_ref[...] = _conv4_silu(x_ref, pad_ref, cwx_ref, cbx_ref, L, SSD_GW)
    bs_ref[...] = _conv4_silu(b_ref, pad_ref, cwb_ref, cbb_ref, L, SSD_N)
    cs_ref[...] = _conv4_silu(c_ref, pad_ref, cwc_ref, cbc_ref, L, SSD_N)
    prm = prm_ref[0]
    a_neg = -jnp.exp(prm[1:2, :])
    tt = lax.broadcasted_iota(jnp.int32, (T, T), 0)
    ss = lax.broadcasted_iota(jnp.int32, (T, T), 1)
    lower = tt >= ss
    upper = ss >= tt
    tri_l = jnp.where(lower, 1.0, 0.0).astype(bf16)
    tri_u = jnp.where(upper, 1.0, 0.0).astype(bf16)
    lower_b = lower[:SSD_TB, :SSD_TB]
    upper_b = upper[:SSD_TB, :SSD_TB]
    er = lax.broadcasted_iota(jnp.int32, (V7X_LANES, SSD_GW), 0)
    ec = lax.broadcasted_iota(jnp.int32, (V7X_LANES, SSD_GW), 1) // SSD_P
    exp_f = jnp.where(er == ec, 1.0, 0.0).astype(bf16)
    exp_b = jnp.where(er == ec + H, 1.0, 0.0).astype(bf16)

    def expand(w, e):
        hi = w.astype(bf16)
        lo = (w - hi.astype(f32)).astype(bf16)
        return _dot(hi, e) + _dot(lo, e)

    def chunk_terms(c):
        rows = pl.ds(c * T, T)
        dt = _softplus(dt_ref[rows, :].astype(f32) + prm[0:1, :])
        da = dt * a_neg
        return rows, dt, da

    s_f = _heads_to_lanes(s0f_ref, st_ref) if has_init else None
    for c in range(nc):
        rows, dt, da = chunk_terms(c)
        cum = _dot_exact_rhs(tri_l, da)
        rsum = _dot_exact_rhs(tri_u, da)
        rb_ref[rows, :] = rsum
        da_t = da.T
        cum_t = _dot_exact_lhs(da_t, tri_u)
        rsum_t = _dot_exact_lhs(da_t, tri_l)
        dt_t = dt.T
        bmat = bs_ref[rows, :]
        cmat = cs_ref[rows, :]
        xmat = xs_ref[rows, :]
        gmat = _dot_nt(cmat.astype(bf16), bmat.astype(bf16))
        for hh in range(H):
            cf, cf_t, dtf_t = cum[:, hh:hh + 1], cum_t[hh:hh + 1, :], dt_t[hh:hh + 1, :]
            rb, rb_t, dtb_t = rsum[:, H + hh:H + hh + 1], rsum_t[H + hh:H + hh + 1, :], dt_t[H + hh:H + hh + 1, :]
            block_rows = []
            for bi in range(T // SSD_TB):
                rr = slice(bi * SSD_TB, (bi + 1) * SSD_TB)
                blocks = []
                for bj in range(T // SSD_TB):
                    cc = slice(bj * SSD_TB, (bj + 1) * SSD_TB)
                    if bi > bj:
                        blk = jnp.exp(cf[rr] - cf_t[:, cc]) * dtf_t[:, cc]
                    elif bi < bj:
                        blk = jnp.exp(rb[rr] - rb_t[:, cc]) * dtb_t[:, cc]
                    else:
                        blk = (jnp.where(lower_b, jnp.exp(cf[rr] - cf_t[:, cc]), 0.0) * dtf_t[:, cc]
                               + jnp.where(upper_b, jnp.exp(rb[rr] - rb_t[:, cc]), 0.0) * dtb_t[:, cc])
                    blocks.append(gmat[rr, cc] * blk)
                block_rows.append(jnp.concatenate(blocks, axis=1))
            m = jnp.concatenate(block_rows, axis=0)
            xh = xmat[:, hh * SSD_P:(hh + 1) * SSD_P]
            acc_ref[rows, pl.ds(hh * SSD_P, SSD_P)] = _dot(m.astype(bf16), xh.astype(bf16))
        ecum = jnp.exp(cum)
        if s_f is not None:
            acc_ref[rows, :] += _dot(cmat.astype(bf16), s_f.astype(bf16)) * expand(ecum, exp_f)
        if c < nc - 1 or not has_init:
            tail = jnp.exp(cum[T - 1:T, :] - cum) * dt
            xw = (xmat * expand(tail, exp_f)).astype(bf16)
            upd = _dot(bmat.T.astype(bf16), xw)
            if s_f is None:
                s_f = upd
            else:
                s_f = s_f * expand(jnp.broadcast_to(ecum[T - 1:T, :], (V7X_SUBLANES, V7X_LANES)),
                                   exp_f)[0:1, :] + upd
    s_b = _heads_to_lanes(s0b_ref, st_ref) if has_init else None
    for c in reversed(range(nc)):
        rows, dt, da = chunk_terms(c)
        rsum = rb_ref[rows, :]
        ers = jnp.exp(rsum)
        bmat = bs_ref[rows, :]
        cmat = cs_ref[rows, :]
        xmat = xs_ref[rows, :]
        if s_b is not None:
            acc_ref[rows, :] += _dot(cmat.astype(bf16), s_b.astype(bf16)) * expand(ers, exp_b)
        if c > 0 or not has_init:
            tail = jnp.exp(rsum[0:1, :] - rsum) * dt
            xw = (xmat * expand(tail, exp_b)).astype(bf16)
            upd = _dot(bmat.T.astype(bf16), xw)
            if s_b is None:
                s_b = upd
            else:
                s_b = s_b * expand(jnp.broadcast_to(ers[0:1, :], (V7X_SUBLANES, V7X_LANES)),
                                   exp_b)[0:1, :] + upd
    if not has_init:
        for hh in range(H):
            sf_ref[0, 0, hh] = s_f[:, hh * SSD_P:(hh + 1) * SSD_P]
            sb_ref[0, 0, hh] = s_b[:, hh * SSD_P:(hh + 1) * SSD_P]
    yg = (acc_ref[...] + xs_ref[...] * dvec_ref[...]) * _silu(z_ref[...].astype(f32))
    yn_ref[g] = yg

    @pl.when(g == SSD_GROUPS - 1)
    def _():
        ssq = None
        for gg in range(SSD_GROUPS):
            y = yn_ref[gg]
            s = jnp.sum(y * y, axis=-1, keepdims=True)
            ssq = s if ssq is None else ssq + s
        inv = lax.rsqrt(ssq * (1.0 / D_SSD) + EPS)
        for gg in range(SSD_GROUPS):
            cols = pl.ds(gg * SSD_GW, SSD_GW)
            y_ref[:, cols] = (yn_ref[gg] * inv * ng_ref[:, cols]).astype(bf16)


def ssd_branch(proj, bufs, row0, n_seq, seq_len, conv_w, conv_b, prm, dvec, ng, init, layer):
    has_init = init is not None
    rb0 = row0 // seq_len
    xoff = OFF_SSD_XBC
    boff = OFF_SSD_XBC + D_SSD
    coff = boff + SSD_GROUPS * SSD_N
    in_specs = [pl.BlockSpec((seq_len, SSD_GW), lambda s, g: (rb0 + s, OFF_SSD_Z // SSD_GW + g)),
                pl.BlockSpec((seq_len, SSD_GW), lambda s, g: (rb0 + s, xoff // SSD_GW + g)),
                pl.BlockSpec((seq_len, SSD_N), lambda s, g: (rb0 + s, boff // SSD_N + g)),
                pl.BlockSpec((seq_len, SSD_N), lambda s, g: (rb0 + s, coff // SSD_N + g)),
                pl.BlockSpec((seq_len, V7X_LANES), lambda s, g: (rb0 + s, OFF_SSD_DT // V7X_LANES + g)),
                pl.BlockSpec((SSD_CONV, SSD_GW), lambda s, g: (0, g)),
                pl.BlockSpec((1, SSD_GW), lambda s, g: (0, g)),
                pl.BlockSpec((SSD_CONV, SSD_N), lambda s, g: (0, D_SSD // SSD_N + g)),
                pl.BlockSpec((1, SSD_N), lambda s, g: (0, D_SSD // SSD_N + g)),
                pl.BlockSpec((SSD_CONV, SSD_N), lambda s, g: (0, D_SSD // SSD_N + SSD_GROUPS + g)),
                pl.BlockSpec((1, SSD_N), lambda s, g: (0, D_SSD // SSD_N + SSD_GROUPS + g)),
                pl.BlockSpec((1, V7X_SUBLANES, V7X_LANES), lambda s, g: (g, 0, 0)),
                pl.BlockSpec((1, SSD_GW), lambda s, g: (0, g)),
                pl.BlockSpec((1, D_SSD), lambda s, g: (0, 0))]
    args = [proj, proj, proj, proj, proj, conv_w, conv_b, conv_w, conv_b, conv_w, conv_b, prm, dvec, ng]
    state_spec = pl.BlockSpec((1, 1, SSD_HPG, SSD_N, SSD_P), lambda s, g: (s, layer, g, 0, 0))
    y_spec = pl.BlockSpec((seq_len, D_SSD), lambda s, g: (rb0 + s, 0))
    if has_init:
        in_specs += [state_spec, state_spec]
        args += [init[0], init[1]]
        out_specs, out_shape = y_spec, _BRANCH_BUF
    else:
        st = jax.ShapeDtypeStruct((n_seq, DEPTH, SSD_HEADS, SSD_N, SSD_P), f32)
        out_specs, out_shape = [y_spec, state_spec, state_spec], [_BRANCH_BUF, st, st]
    scratch = [pltpu.VMEM((seq_len + 2 * V7X_SUBLANES, SSD_GW), f32),
               pltpu.VMEM((seq_len, SSD_GW), f32),
               pltpu.VMEM((seq_len, SSD_N), f32),
               pltpu.VMEM((seq_len, SSD_N), f32),
               pltpu.VMEM((seq_len, SSD_GW), f32),
               pltpu.VMEM((seq_len, V7X_LANES), f32),
               pltpu.VMEM((SSD_GROUPS, seq_len, SSD_GW), f32),
               pltpu.VMEM((SSD_N, SSD_GW), f32)]
    return _inplace_call(functools.partial(_ssd_kernel, seq_len=seq_len, has_init=has_init), bufs,
                         (n_seq, SSD_GROUPS), in_specs, args, out_specs, out_shape, scratch, "ssd_branch")


def _head_rmsnorm(x, g):
    return x * lax.rsqrt(jnp.mean(x * x, axis=-1, keepdims=True) + EPS) * g


CTX_HB = 4
CTX_W = CTX_HB * NA_HD


def _ctx_attn_kernel(q_ref, k_ref, v_ref, qg_ref, kg_ref, y_ref, ko_ref, vo_ref):
    scale = NA_HD ** -0.5
    vo_ref[0, 0] = v_ref[...].astype(f32)
    for h in range(CTX_HB):
        cols = pl.ds(h * NA_HD, NA_HD)
        q = _head_rmsnorm(q_ref[:, cols].astype(f32), qg_ref[...])
        k = _head_rmsnorm(k_ref[:, cols].astype(f32), kg_ref[...])
        ko_ref[0, 0, :, cols] = k
        s = _dot_nt(q.astype(bf16), k.astype(bf16)) * scale
        p = jnp.exp(s - jnp.max(s, axis=-1, keepdims=True))
        o = _dot(p.astype(bf16), v_ref[:, cols]) / jnp.sum(p, axis=-1, keepdims=True)
        y_ref[:, cols] = o.astype(bf16)


def context_attention(proj, bufs, row0, qg, kg, layer):
    rb0 = row0 // SEQ
    spec = lambda off: pl.BlockSpec((SEQ, CTX_W), lambda s, hb: (rb0 + s, off // CTX_W + hb))
    gspec = pl.BlockSpec((1, NA_HD), lambda s, hb: (0, 0))
    kv_spec = pl.BlockSpec((1, 1, SEQ, CTX_W), lambda s, hb: (s, layer, 0, hb))
    kv_shape = jax.ShapeDtypeStruct((BATCH, DEPTH, SEQ, NA_W), f32)
    return _inplace_call(
        _ctx_attn_kernel, bufs, (BATCH, NA_HEADS // CTX_HB),
        [spec(OFF_NA_Q), spec(OFF_NA_K), spec(OFF_NA_V), gspec, gspec], [proj, proj, proj, qg, kg],
        [pl.BlockSpec((SEQ, CTX_W), lambda s, hb: (rb0 + s, hb)), kv_spec, kv_spec],
        [_BRANCH_BUF, kv_shape, kv_shape], [], "context_attention")


NA_ROWS = DEC_SEQ // GRID_W
NA_NK = NA_WR * GRID_W


def _na_row_start(r):
    return min(max(r - NA_WR // 2, 0), NA_ROWS - NA_WR)


def _rope(x, cos, sin_signed):
    lane = lax.broadcasted_iota(jnp.int32, x.shape, 1)
    quarter = NA_HD // 4
    swapped = jnp.where((lane & (2 * quarter - 1)) < quarter,
                        pltpu.roll(x, NA_HD - quarter, axis=1), pltpu.roll(x, quarter, axis=1))
    return x * cos + swapped * sin_signed


def _na_row_groups():
    groups, r = [], 0
    while r < NA_ROWS:
        n = 1
        while r + n < NA_ROWS and _na_row_start(r + n) == _na_row_start(r):
            n += 1
        groups.append((r, n, _na_row_start(r)))
        r += n
    return groups


def _na_attn_kernel(q_ref, k_ref, v_ref, kc_ref, vc_ref, qg_ref, kg_ref, cos_ref, sin_ref,
                    bias_ref, valid_ref, y_ref, qs_ref, ks_ref, s_ref, p_ref, oc_ref, w_ref):
    scale = NA_HD ** -0.5
    cos = cos_ref[...]
    sin = sin_ref[...]
    qs_ref[...] = _rope(_head_rmsnorm(q_ref[...].astype(f32), qg_ref[...]), cos, sin).astype(bf16)
    ks_ref[...] = _rope(_head_rmsnorm(k_ref[...].astype(f32), kg_ref[...]), cos, sin).astype(bf16)
    groups = _na_row_groups()
    valid = valid_ref[...] > 0.0
    for r0, n, rs in groups:
        rows = pl.ds(r0 * GRID_W, n * GRID_W)
        d0 = r0 - rs
        kw = ks_ref[pl.ds(rs * GRID_W, NA_NK), :]
        s = (_dot_nt(qs_ref[rows, :], kw) * scale).reshape(n, GRID_W, NA_NK) + bias_ref[0, d0:d0 + n]
        s_ref[rows, :] = jnp.where(valid[None], s, -1e30).reshape(n * GRID_W, NA_NK)
    s_ctx = _dot_nt(qs_ref[...], kc_ref[0, 0].astype(bf16)) * scale
    s_loc = s_ref[...]
    m = jnp.maximum(jnp.max(s_loc, axis=-1, keepdims=True), jnp.max(s_ctx, axis=-1, keepdims=True))
    p_loc = jnp.exp(s_loc - m)
    p_ctx = jnp.exp(s_ctx - m)
    w_ref[...] = 1.0 / (jnp.sum(p_loc, axis=-1, keepdims=True) + jnp.sum(p_ctx, axis=-1, keepdims=True))
    p_ref[...] = p_loc.astype(bf16)
    oc_ref[...] = _dot(p_ctx.astype(bf16), vc_ref[0, 0].astype(bf16))
    for r0, n, rs in groups:
        rows = pl.ds(r0 * GRID_W, n * GRID_W)
        vw = v_ref[pl.ds(rs * GRID_W, NA_NK), :]
        y_ref[rows, :] = ((_dot(p_ref[rows, :], vw) + oc_ref[rows, :]) * w_ref[rows, :]).astype(bf16)


def neighbourhood_attention(proj, cache_k, cache_v, layer, qg, kg, cos, sin, bias, valid):
    spec = lambda off: pl.BlockSpec((DEC_SEQ, NA_HD), lambda b, h: (b, off // NA_HD + h))
    cspec = pl.BlockSpec((1, 1, PAST_LEN, NA_HD), lambda b, h: (b, layer, 0, h))
    gspec = pl.BlockSpec((1, NA_HD), lambda b, h: (0, 0))
    tspec = pl.BlockSpec((DEC_SEQ, NA_HD), lambda b, h: (0, 0))
    return pl.pallas_call(
        _na_attn_kernel,
        grid=(DEC_BATCH, NA_HEADS),
        in_specs=[spec(OFF_NA_Q), spec(OFF_NA_K), spec(OFF_NA_V), cspec, cspec, gspec, gspec,
                  tspec, tspec,
                  pl.BlockSpec((1, NA_WR, GRID_W, NA_NK), lambda b, h: (h, 0, 0, 0)),
                  pl.BlockSpec((GRID_W, NA_NK), lambda b, h: (0, 0))],
        out_specs=pl.BlockSpec((DEC_SEQ, NA_HD), lambda b, h: (b, h)),
        out_shape=_BRANCH_BUF,
        scratch_shapes=[pltpu.VMEM((DEC_SEQ, NA_HD), bf16), pltpu.VMEM((DEC_SEQ, NA_HD), bf16),
                        pltpu.VMEM((DEC_SEQ, NA_NK), f32), pltpu.VMEM((DEC_SEQ, NA_NK), bf16),
                        pltpu.VMEM((DEC_SEQ, NA_HD), f32), pltpu.VMEM((DEC_SEQ, 1), f32)],
        compiler_params=_params(2), name="neighbourhood_attention",
    )(proj, proj, proj, cache_k, cache_v, qg, kg, cos, sin, bias, valid)


def _split_w_in(w_in):
    dt0, na0 = IN_OFFSETS[7], IN_OFFSETS[8]
    head = w_in.astype(bf16)
    pieces = [head[..., HEAD_COLS:dt0], head[..., na0:]]
    dt = head[..., dt0:na0]
    for gidx in range(SSD_GROUPS):
        pieces.append(dt[..., gidx * SSD_HPG:(gidx + 1) * SSD_HPG])
        pieces.append(dt[..., SSD_HEADS + gidx * SSD_HPG:SSD_HEADS + (gidx + 1) * SSD_HPG])
        pieces.append(jnp.zeros((DEPTH, D_MODEL, V7X_LANES - 2 * SSD_HPG), bf16))
    used = OFF_SSD_DT + SSD_GROUPS * V7X_LANES
    pieces.append(jnp.zeros((DEPTH, D_MODEL, D_IN_PAD - used), bf16))
    return head, jnp.concatenate(pieces, axis=-1)


def _group_lanes(v):
    rows = []
    for gidx in range(SSD_GROUPS):
        sl = slice(gidx * SSD_HPG, (gidx + 1) * SSD_HPG)
        rows.append(jnp.concatenate([v[0, sl], v[1, sl], jnp.zeros((V7X_LANES - 2 * SSD_HPG,), f32)]))
    return jnp.stack(rows)


def _rope_tables():
    t = np.arange(DEC_SEQ)
    quarter = NA_HD // 4
    inv = ROPE_BASE ** (-np.arange(quarter, dtype=np.float32) / quarter)
    ang_r = (t // GRID_W).astype(np.float32)[:, None] * inv
    ang_c = (t % GRID_W).astype(np.float32)[:, None] * inv
    cos = np.concatenate([np.cos(ang_r), np.cos(ang_r), np.cos(ang_c), np.cos(ang_c)], axis=1)
    sin = np.concatenate([-np.sin(ang_r), np.sin(ang_r), -np.sin(ang_c), np.sin(ang_c)], axis=1)
    return jnp.asarray(cos, f32), jnp.asarray(sin, f32)


def _na_tables(rpb):
    cq = np.arange(GRID_W)
    kc = np.tile(np.arange(GRID_W), NA_WR)
    col_start = np.clip(cq - NA_WC // 2, 0, GRID_W - NA_WC)
    valid = (kc[None, :] >= col_start[:, None]) & (kc[None, :] < col_start[:, None] + NA_WC)
    col_off = np.clip(cq[None, :] - cq[:, None], 1 - NA_WC, NA_WC - 1) + NA_WC - 1
    onehot = (col_off[None, :, :] == np.arange(2 * NA_WC - 1)[:, None, None]).astype(np.float32)
    toep = jnp.einsum('hic,cqk->hiqk', rpb.astype(f32), jnp.asarray(onehot), precision=lax.Precision.HIGHEST)
    tables = []
    for d in range(NA_WR):
        rows = toep[:, NA_WR - 1 - d:2 * NA_WR - 1 - d]
        tables.append(rows.transpose(0, 2, 1, 3).reshape(NA_HEADS, GRID_W, NA_NK))
    return jnp.stack(tables, axis=1), jnp.asarray(valid, f32)


def kernel(x_prompt, x_sample, cache_na_k, cache_na_v, state_lru_f, state_lru_b,
           state_ret_f, state_ret_b, state_ssd_f, state_ssd_b, c, c_ctx,
           norm1_g, norm2_g, w_ada, b_ada, w_in, w_gate, b_gate, w_branch, w_out,
           lru_conv_w, lru_conv_b, lru_wa, lru_ba, lru_wx, lru_bx, lru_lambda,
           ret_gn_g, ssd_conv_w, ssd_conv_b, ssd_a_log, ssd_dt_bias, ssd_d, ssd_norm_g,
           na_q_g, na_k_g, na_rpb, ffn_w_up, ffn_conv_w, ffn_conv_b, ffn_w_down):
    xs0 = x_sample.reshape(T_SAMPLE, D_MODEL)
    xp0 = x_prompt.reshape(T_PROMPT, D_MODEL)
    n_ptiles = N_TILES - N_SAMPLE_TILES
    cond = jnp.concatenate([c, c_ctx[None, :], jnp.zeros((N_COND_PAD - N_COND, D_MODEL), f32)], axis=0)
    mod_all = ada_modulation(cond, w_ada, b_ada)
    cos, sin = _rope_tables()
    hh = jnp.arange(RET_HEADS, dtype=f32)
    ret_la = jnp.stack([jnp.log1p(-jnp.exp2(-5.0 - hh)), jnp.log1p(-jnp.exp2(-5.5 - hh))], axis=1)
    cache_k = cache_na_k.reshape(DEC_BATCH, DEPTH, PAST_LEN, NA_W)
    cache_v = cache_na_v.reshape(DEC_BATCH, DEPTH, PAST_LEN, NA_W)
    lru_init = (state_lru_f.reshape(DEC_BATCH, DEPTH, 1, D_RNN), state_lru_b.reshape(DEC_BATCH, DEPTH, 1, D_RNN))

    w_head_b, w_tail_b = _split_w_in(w_in)
    w_branch_b, w_out_b, w_down_b = w_branch.astype(bf16), w_out.astype(bf16), ffn_w_down.astype(bf16)

    new = {k: None for k in ("k", "v", "lru_f", "lru_b", "ret_f", "ret_b", "ssd_f", "ssd_b")}
    for l in range(DEPTH):
        mod = mod_all[l].reshape(N_COND_PAD, 1, N_MOD * D_MODEL)
        g1 = norm1_g[l][None, :]
        if l == 0:
            proj, xn = in_projection(xs0, 0, 0, N_SAMPLE_TILES, g1, mod, w_head_b, w_tail_b, l)
            proj, xn = in_projection(xp0, 0, N_SAMPLE_TILES, n_ptiles, g1, mod, w_head_b, w_tail_b, l, (proj, xn))
        else:
            proj, xn = in_projection(x, 0, 0, N_TILES, g1, mod, w_head_b, w_tail_b, l)

        w4 = jnp.concatenate([lru_wa[l, 0], lru_wx[l, 0], lru_wa[l, 1], lru_wx[l, 1]], axis=-1).astype(bf16)
        b4 = jnp.concatenate([lru_ba[l, 0].reshape(LRU_BLOCKS, 1, LRU_BS), lru_bx[l, 0].reshape(LRU_BLOCKS, 1, LRU_BS),
                              lru_ba[l, 1].reshape(LRU_BLOCKS, 1, LRU_BS), lru_bx[l, 1].reshape(LRU_BLOCKS, 1, LRU_BS)],
                             axis=-1)
        lru_args = (lru_conv_w[l], lru_conv_b[l][None, :], w4, b4, lru_lambda[l])
        y_lru = lru_branch(proj, (None,), 0, DEC_BATCH, DEC_SEQ, *lru_args, lru_init, l)
        y_lru, new["lru_f"], new["lru_b"] = lru_branch(
            proj, (y_lru, new["lru_f"], new["lru_b"]), T_SAMPLE, BATCH, SEQ, *lru_args, None, l)

        gn = ret_gn_g[l][None, :]
        y_ret = retention_branch(proj, (None,), 0, DEC_BATCH, DEC_SEQ, ret_la, gn, (state_ret_f, state_ret_b), l)
        y_ret, new["ret_f"], new["ret_b"] = retention_branch(
            proj, (y_ret, new["ret_f"], new["ret_b"]), T_SAMPLE, BATCH, SEQ, ret_la, gn, None, l)

        prm = jnp.stack([_group_lanes(ssd_dt_bias[l]), _group_lanes(ssd_a_log[l])], axis=1)
        prm = jnp.concatenate([prm, jnp.zeros((SSD_GROUPS, V7X_SUBLANES - 2, V7X_LANES), f32)], axis=1)
        dvec = jnp.repeat(ssd_d[l], SSD_P)[None, :]
        ssd_args = (ssd_conv_w[l], ssd_conv_b[l][None, :], prm, dvec, ssd_norm_g[l][None, :])
        y_ssd = ssd_branch(proj, (None,), 0, DEC_BATCH, DEC_SEQ, *ssd_args, (state_ssd_f, state_ssd_b), l)
        y_ssd, new["ssd_f"], new["ssd_b"] = ssd_branch(
            proj, (y_ssd, new["ssd_f"], new["ssd_b"]), T_SAMPLE, BATCH, SEQ, *ssd_args, None, l)

        qg = na_q_g[l][None, :]
        kg = na_k_g[l][None, :]
        bias, valid = _na_tables(na_rpb[l])
        y_na = neighbourhood_attention(proj, cache_k, cache_v, l, qg, kg, cos, sin, bias, valid)
        y_na, new["k"], new["v"] = context_attention(proj, (y_na, new["k"], new["v"]), T_SAMPLE, qg, kg, l)

        merged = merge_branches(xn, (y_lru, y_ret, y_ssd, y_na), w_gate, b_gate[l][None, :], w_branch_b, l)
        if l == 0:
            x = residual_projection(merged, w_out_b, l, xs0, mod, 2, 0, N_SAMPLE_TILES)
            x = residual_projection(merged, w_out_b, l, xp0, mod, 2, N_SAMPLE_TILES, n_ptiles, x_tile0=0, buf=x)
        else:
            x = residual_projection(merged, w_out_b, l, x, mod, 2)
        hmid = ffn_up(x, norm2_g[l][None, :], mod, ffn_w_up, l, ffn_conv_w[l], ffn_conv_b[l][None, :])
        if l < DEPTH - 1:
            x = residual_projection(hmid, w_down_b, l, x, mod, 5)
        else:
            y_sample = residual_projection(hmid, w_down_b, l, x, mod, 5, 0, N_SAMPLE_TILES, out_rows=T_SAMPLE)
            y_prompt = residual_projection(hmid, w_down_b, l, x, mod, 5, N_SAMPLE_TILES, n_ptiles,
                                           out_tile0=0, out_rows=T_PROMPT)

    kv_shape = (BATCH, DEPTH, SEQ, NA_HEADS, NA_HD)
    return (y_prompt.reshape(BATCH, SEQ, D_MODEL), y_sample.reshape(DEC_BATCH, DEC_SEQ, D_MODEL),
            new["k"].reshape(kv_shape), new["v"].reshape(kv_shape),
            new["lru_f"].reshape(BATCH, DEPTH, D_RNN), new["lru_b"].reshape(BATCH, DEPTH, D_RNN),
            new["ret_f"], new["ret_b"], new["ssd_f"], new["ssd_b"])
```

```python
import functools
import math

import jax
import jax.numpy as jnp
import numpy as np
from jax import lax
from jax.experimental import pallas as pl
from jax.experimental.pallas import tpu as pltpu

D_MODEL = 2048
BATCH = 16
SEQ = 256
DEPTH = 2
DEC_BATCH = 8
DEC_SEQ = 1024
PAST_LEN = 256
GRID_W = 64
EPS = 1e-6
N_BRANCH = 4
BRANCH_W = 1024
N_MOD = 6
D_RNN = 1024
LRU_BLOCKS = 8
LRU_BS = D_RNN // LRU_BLOCKS
LRU_CONV = 4
LRU_C = 8.0
RET_HEADS = 4
RET_DK = 128
RET_DV = 256
SSD_HEADS = 16
SSD_P = 64
SSD_N = 128
SSD_GROUPS = 2
SSD_CONV = 4
D_SSD = SSD_HEADS * SSD_P
SSD_CONV_CH = D_SSD + 2 * SSD_GROUPS * SSD_N
NA_HEADS = 8
NA_HD = 128
NA_W = NA_HEADS * NA_HD
NA_WR = 8
NA_WC = 16
ROPE_BASE = 10000.0
D_FF = 5632
FFN_CONV = 3
IN_SIZES = (D_RNN, D_RNN,
            RET_HEADS * RET_DK, RET_HEADS * RET_DK, RET_HEADS * RET_DV, RET_HEADS * RET_DV,
            D_SSD, SSD_CONV_CH, 2 * SSD_HEADS,
            NA_W, NA_W, NA_W)
D_IN = sum(IN_SIZES)
IN_OFFSETS = tuple(int(s) for s in np.cumsum(IN_SIZES)[:-1])

V7X_LANES = 128
V7X_SUBLANES = 8
V7X_VMEM_LIMIT_BYTES = 56 * 1024 * 1024

TM = 1024
T_SAMPLE = DEC_BATCH * DEC_SEQ
T_PROMPT = BATCH * SEQ
T_ALL = T_SAMPLE + T_PROMPT
N_SAMPLE_TILES = T_SAMPLE // TM
N_TILES = T_ALL // TM
N_COND = DEC_BATCH + 1
N_COND_PAD = 16

SSD_HPG = SSD_HEADS // SSD_GROUPS
OFF_LRU_X = 0
OFF_LRU_G = OFF_LRU_X + D_RNN
OFF_RET_Q = OFF_LRU_G + D_RNN
OFF_RET_K = OFF_RET_Q + RET_HEADS * RET_DK
OFF_RET_V = OFF_RET_K + RET_HEADS * RET_DK
OFF_RET_G = OFF_RET_V + RET_HEADS * RET_DV
OFF_SSD_Z = OFF_RET_G + RET_HEADS * RET_DV
OFF_SSD_XBC = OFF_SSD_Z + D_SSD
OFF_NA_Q = OFF_SSD_XBC + SSD_CONV_CH
OFF_NA_K = OFF_NA_Q + NA_W
OFF_NA_V = OFF_NA_K + NA_W
OFF_SSD_DT = OFF_NA_V + NA_W
PROJ_TN = 1024
D_IN_PAD = -(-(OFF_SSD_DT + SSD_GROUPS * V7X_LANES) // PROJ_TN) * PROJ_TN
N_HEAD_TILES = OFF_NA_Q // PROJ_TN
HEAD_COLS = N_HEAD_TILES * PROJ_TN

RET_CHUNK = 256
SSD_CHUNK = 256
SSD_TB = 128

f32 = jnp.float32
bf16 = jnp.bfloat16

_ARB = "arbitrary"


def _params(n_axes):
    return pltpu.CompilerParams(dimension_semantics=(_ARB,) * n_axes,
                                vmem_limit_bytes=V7X_VMEM_LIMIT_BYTES)


def _mod_spec(k, tile0=0):
    return pl.BlockSpec((1, 1, D_MODEL), lambda i, j: (jnp.minimum(tile0 + i, N_SAMPLE_TILES), 0, k))


def _dot(a, b):
    return jnp.dot(a, b, preferred_element_type=f32)


def _dot_nt(a, b):
    return lax.dot_general(a, b, (((1,), (1,)), ((), ())), preferred_element_type=f32)


def _sigmoid(x):
    return 0.5 * jnp.tanh(0.5 * x) + 0.5


def _silu(x):
    return x * _sigmoid(x)


def _gelu_tanh(x):
    return 0.5 * x * (1.0 + jnp.tanh(math.sqrt(2.0 / math.pi) * (x + 0.044715 * (x * x * x))))


def _softplus(x):
    return jnp.maximum(x, 0.0) + jnp.log1p(jnp.exp(-jnp.abs(x)))


def _drop_refs(body, n, *refs):
    body(*refs[n:])


def _inplace_call(body, bufs, grid, in_specs, args, out_specs, out_shape, scratch, name):
    held = [(k, b) for k, b in enumerate(bufs) if b is not None]
    if held:
        body = functools.partial(_drop_refs, body, len(held))
        in_specs = [pl.BlockSpec(memory_space=pl.ANY)] * len(held) + list(in_specs)
        args = [b for _, b in held] + list(args)
    aliases = {pos: k for pos, (k, _) in enumerate(held)}
    return pl.pallas_call(
        body, grid=grid, in_specs=in_specs, out_specs=out_specs, out_shape=out_shape,
        scratch_shapes=scratch, input_output_aliases=aliases,
        compiler_params=_params(len(grid)), name=name,
    )(*args)


_BRANCH_BUF = jax.ShapeDtypeStruct((T_ALL, BRANCH_W), bf16)


ADA_TN = 1024


def _ada_kernel(c_ref, w_ref, b_ref, o_ref):
    c = _silu(c_ref[...]).astype(bf16)
    o_ref[0] = _dot(c, w_ref[0].astype(bf16)) + b_ref[0]


def ada_modulation(cond, w_ada, b_ada):
    n = N_MOD * D_MODEL
    return pl.pallas_call(
        _ada_kernel,
        grid=(DEPTH, n // ADA_TN),
        in_specs=[pl.BlockSpec((N_COND_PAD, D_MODEL), lambda l, j: (0, 0)),
                  pl.BlockSpec((1, D_MODEL, ADA_TN), lambda l, j: (l, 0, j)),
                  pl.BlockSpec((1, 1, ADA_TN), lambda l, j: (l, 0, j))],
        out_specs=pl.BlockSpec((1, N_COND_PAD, ADA_TN), lambda l, j: (l, 0, j)),
        out_shape=jax.ShapeDtypeStruct((DEPTH, N_COND_PAD, n), f32),
        compiler_params=_params(2),
        name="ada_modulation",
    )(cond, w_ada, b_ada.reshape(DEPTH, 1, n))


NORM_ROWS = 128


def _modulated_norm(x_ref, g_ref, sc_ref, sh_ref, xn_ref):
    g = g_ref[...]
    sc = 1.0 + sc_ref[0]
    sh = sh_ref[0]

    def body(r, carry):
        rows = pl.ds(pl.multiple_of(r * NORM_ROWS, NORM_ROWS), NORM_ROWS)
        x = x_ref[rows, :]
        y = x * lax.rsqrt(jnp.mean(x * x, axis=-1, keepdims=True) + EPS)
        xn_ref[rows, :] = ((y * g) * sc + sh).astype(bf16)
        return carry

    lax.fori_loop(0, TM // NORM_ROWS, body, 0)


def _in_proj_kernel(x_ref, g_ref, sc_ref, sh_ref, wh_ref, wt_ref, o_ref, xn_ref):
    j = pl.program_id(1)

    @pl.when(j == 0)
    def _():
        _modulated_norm(x_ref, g_ref, sc_ref, sh_ref, xn_ref)

    @pl.when(j < N_HEAD_TILES)
    def _():
        o_ref[...] = _dot(xn_ref[...], wh_ref[0]).astype(bf16)

    @pl.when(j >= N_HEAD_TILES)
    def _():
        o_ref[...] = _dot(xn_ref[...], wt_ref[0]).astype(bf16)


def in_projection(x, x_tile0, tile0, n_tiles, g, mod, w_head, w_tail, layer, bufs=(None, None)):
    n = D_IN_PAD
    n_tail = w_tail.shape[2] // PROJ_TN
    return _inplace_call(
        _in_proj_kernel, bufs, (n_tiles, n // PROJ_TN),
        [pl.BlockSpec((TM, D_MODEL), lambda i, j: (x_tile0 + i, 0)),
         pl.BlockSpec((1, D_MODEL), lambda i, j: (0, 0)),
         _mod_spec(1, tile0), _mod_spec(0, tile0),
         pl.BlockSpec((1, D_MODEL, PROJ_TN), lambda i, j: (layer, 0, jnp.minimum(j, N_HEAD_TILES - 1))),
         pl.BlockSpec((1, D_MODEL, PROJ_TN),
                      lambda i, j: (layer, 0, jnp.clip(j - N_HEAD_TILES, 0, n_tail - 1)))],
        [x, g, mod, mod, w_head, w_tail],
        [pl.BlockSpec((TM, PROJ_TN), lambda i, j: (tile0 + i, j)),
         pl.BlockSpec((TM, D_MODEL), lambda i, j: (tile0 + i, 0))],
        [jax.ShapeDtypeStruct((T_ALL, n), bf16), jax.ShapeDtypeStruct((T_ALL, D_MODEL), bf16)],
        [], "in_projection")


MERGE_TN = 256


def _merge_kernel(xn_ref, *refs):
    y_refs, wg_refs, bg_refs = refs[0:4], refs[4:8], refs[8:12]
    wb_ref, o_ref = refs[12], refs[13]
    xn = xn_ref[...]
    acc = None
    for n in range(N_BRANCH):
        gate = _sigmoid(_dot(xn, wg_refs[n][0].astype(bf16)) + bg_refs[n][...])
        term = gate * _dot(y_refs[n][...], wb_ref[0, n])
        acc = term if acc is None else acc + term
    o_ref[...] = acc.astype(bf16)


def merge_branches(xn, ys, w_gate, b_gate, w_branch, layer):
    nj = D_MODEL // MERGE_TN
    y_spec = pl.BlockSpec((TM, BRANCH_W), lambda i, j: (i, 0))
    wg_specs = [pl.BlockSpec((1, D_MODEL, MERGE_TN), lambda i, j, n=n: (layer, 0, n * nj + j))
                for n in range(N_BRANCH)]
    bg_specs = [pl.BlockSpec((1, MERGE_TN), lambda i, j, n=n: (0, n * nj + j)) for n in range(N_BRANCH)]
    return pl.pallas_call(
        _merge_kernel,
        grid=(N_TILES, nj),
        in_specs=[pl.BlockSpec((TM, D_MODEL), lambda i, j: (i, 0)),
                  y_spec, y_spec, y_spec, y_spec, *wg_specs, *bg_specs,
                  pl.BlockSpec((1, N_BRANCH, BRANCH_W, MERGE_TN), lambda i, j: (layer, 0, 0, j))],
        out_specs=pl.BlockSpec((TM, MERGE_TN), lambda i, j: (i, j)),
        out_shape=jax.ShapeDtypeStruct((T_ALL, D_MODEL), bf16),
        compiler_params=_params(2),
        name="merge_branches",
    )(xn, *ys, *([w_gate] * N_BRANCH), *([b_gate] * N_BRANCH), w_branch)


RES_TN = 512
RES_TN_SHORT_K = 1024


def _residual_kernel(a_ref, w_ref, x_ref, gv_ref, o_ref):
    o_ref[...] = x_ref[...] + gv_ref[0] * _dot(a_ref[...], w_ref[0])


def residual_projection(a, w, layer, x, mod, k_mod, tile0=0, n_tiles=N_TILES, x_tile0=None, out_tile0=None,
                        out_rows=T_ALL, buf=None):
    kdim = a.shape[1]
    tn = RES_TN_SHORT_K if kdim == D_MODEL else RES_TN
    x_tile0 = tile0 if x_tile0 is None else x_tile0
    out_tile0 = tile0 if out_tile0 is None else out_tile0
    return _inplace_call(
        _residual_kernel, (buf,), (n_tiles, D_MODEL // tn),
        [pl.BlockSpec((TM, kdim), lambda i, j: (tile0 + i, 0)),
         pl.BlockSpec((1, kdim, tn), lambda i, j: (layer, 0, j)),
         pl.BlockSpec((TM, tn), lambda i, j: (x_tile0 + i, j)),
         pl.BlockSpec((1, 1, tn),
                      lambda i, j: (jnp.minimum(tile0 + i, N_SAMPLE_TILES), 0, k_mod * (D_MODEL // tn) + j))],
        [a, w, x, mod],
        pl.BlockSpec((TM, tn), lambda i, j: (out_tile0 + i, j)),
        jax.ShapeDtypeStruct((out_rows, D_MODEL), f32), [], "residual_projection")


FFN_TN = 512


def _ffn_up_kernel(x_ref, g_ref, sc_ref, sh_ref, wa_ref, wv_ref, cw_ref, cb_ref, o_ref, xn_ref, pad_ref):
    i = pl.program_id(0)

    @pl.when(pl.program_id(1) == 0)
    def _():
        _modulated_norm(x_ref, g_ref, sc_ref, sh_ref, xn_ref)
        zeros = jnp.zeros((V7X_SUBLANES, FFN_TN), f32)
        pad_ref[pl.ds(0, V7X_SUBLANES), :] = zeros
        pad_ref[pl.ds(V7X_SUBLANES + TM, V7X_SUBLANES), :] = zeros

    xn = xn_ref[...]
    a = _dot(xn, wa_ref[0].astype(bf16))
    pad_ref[pl.ds(V7X_SUBLANES, TM), :] = a
    seq_len = jnp.where(i < N_SAMPLE_TILES, DEC_SEQ, SEQ)
    pos = lax.broadcasted_iota(jnp.int32, (TM, 1), 0) & (seq_len - 1)
    prev = jnp.where(pos == 0, 0.0, pad_ref[pl.ds(V7X_SUBLANES - 1, TM), :])
    nxt = jnp.where(pos == seq_len - 1, 0.0, pad_ref[pl.ds(V7X_SUBLANES + 1, TM), :])
    cw = cw_ref[...]
    conv = cb_ref[...] + prev * cw[0:1, :] + a * cw[1:2, :] + nxt * cw[2:3, :]
    o_ref[...] = (_gelu_tanh(conv) * _dot(xn, wv_ref[0].astype(bf16))).astype(bf16)


def ffn_up(x, g, mod, w_up, layer, conv_w, conv_b):
    nj = D_FF // FFN_TN
    return pl.pallas_call(
        _ffn_up_kernel,
        grid=(N_TILES, nj),
        in_specs=[pl.BlockSpec((TM, D_MODEL), lambda i, j: (i, 0)),
                  pl.BlockSpec((1, D_MODEL), lambda i, j: (0, 0)),
                  _mod_spec(4), _mod_spec(3),
                  pl.BlockSpec((1, D_MODEL, FFN_TN), lambda i, j: (layer, 0, j)),
                  pl.BlockSpec((1, D_MODEL, FFN_TN), lambda i, j: (layer, 0, nj + j)),
                  pl.BlockSpec((FFN_CONV, FFN_TN), lambda i, j: (0, j)),
                  pl.BlockSpec((1, FFN_TN), lambda i, j: (0, j))],
        out_specs=pl.BlockSpec((TM, FFN_TN), lambda i, j: (i, j)),
        out_shape=jax.ShapeDtypeStruct((T_ALL, D_FF), bf16),
        scratch_shapes=[pltpu.VMEM((TM, D_MODEL), bf16),
                        pltpu.VMEM((TM + 2 * V7X_SUBLANES, FFN_TN), f32)],
        compiler_params=_params(2),
        name="ffn_up",
    )(x, g, mod, mod, w_up, w_up, conv_w, conv_b)


LRU_CB_LONG = D_RNN
LRU_CB_SHORT = D_RNN


def _lru_kernel(*refs, seq_len, has_init, cbw):
    if has_init:
        (x_ref, g_ref, cw_ref, cb_ref, w4_ref, b4_ref, lam_ref, h0f_ref, h0b_ref,
         y_ref, pad_ref, af_ref, uf_ref, ab_ref, ub_ref, hf_ref, hb_ref) = refs
    else:
        (x_ref, g_ref, cw_ref, cb_ref, w4_ref, b4_ref, lam_ref,
         y_ref, sf_ref, sb_ref, pad_ref, af_ref, uf_ref, ab_ref, ub_ref, hf_ref, hb_ref) = refs
    L = seq_len
    S = V7X_SUBLANES
    zeros = jnp.zeros((S, cbw), f32)
    pad_ref[pl.ds(0, S), :] = zeros
    pad_ref[pl.ds(S + L, S), :] = zeros
    pad_ref[pl.ds(S, L), :] = x_ref[...].astype(f32)
    cw = cw_ref[...]
    xc = cb_ref[...]
    for kk in range(LRU_CONV):
        xc = xc + pad_ref[pl.ds(S + kk - LRU_CONV // 2, L), :] * cw[kk:kk + 1, :]
    sp = _softplus(-lam_ref[...])
    for n in range(cbw // LRU_BS):
        cols = slice(n * LRU_BS, (n + 1) * LRU_BS)
        xcn = xc[:, cols]
        z = _dot(xcn.astype(bf16), w4_ref[n]) + b4_ref[n]
        for d, (a_ref, u_ref) in enumerate(((af_ref, uf_ref), (ab_ref, ub_ref))):
            r = _sigmoid(z[:, (2 * d) * LRU_BS:(2 * d + 1) * LRU_BS])
            ig = _sigmoid(z[:, (2 * d + 1) * LRU_BS:(2 * d + 2) * LRU_BS])
            log_a = (-LRU_C) * r * sp[d:d + 1, cols]
            a = jnp.exp(log_a)
            a_ref[:, cols] = a
            gain2 = -jnp.tanh(log_a) * (a * a + 1.0)
            gain = jnp.where(gain2 > 0.0, gain2 * lax.rsqrt(gain2), 0.0)
            u_ref[:, cols] = gain * (ig * xcn)

    if has_init:
        h0f = h0f_ref[0, 0]
        h0b = h0b_ref[0, 0]
    else:
        h0f = jnp.zeros((1, cbw), f32)
        h0b = jnp.zeros((1, cbw), f32)

    def step(i, carry):
        hf, hb = carry
        base_f = pl.multiple_of(i * S, S)
        base_b = pl.multiple_of(L - S - i * S, S)
        for r in range(S):
            tf = pl.ds(base_f + r, 1)
            tb = pl.ds(base_b + (S - 1 - r), 1)
            hf = af_ref[tf, :] * hf + uf_ref[tf, :]
            hb = ab_ref[tb, :] * hb + ub_ref[tb, :]
            hf_ref[tf, :] = hf
            hb_ref[tb, :] = hb
        return hf, hb

    hf, hb = lax.fori_loop(0, L // S, step, (h0f, h0b))
    if not has_init:
        sf_ref[0, 0] = hf
        sb_ref[0, 0] = hb
    y_ref[...] = ((hf_ref[...] + hb_ref[...]) * _gelu_tanh(g_ref[...].astype(f32))).astype(bf16)


def lru_branch(proj, bufs, row0, n_seq, seq_len, cw, cb, w4, b4, lam, init, layer):
    has_init = init is not None
    cbw = LRU_CB_LONG if seq_len == DEC_SEQ else LRU_CB_SHORT
    ncb = D_RNN // cbw
    nb = cbw // LRU_BS
    rb0 = row0 // seq_len
    in_specs = [pl.BlockSpec((seq_len, cbw), lambda s, c: (rb0 + s, OFF_LRU_X // cbw + c)),
                pl.BlockSpec((seq_len, cbw), lambda s, c: (rb0 + s, OFF_LRU_G // cbw + c)),
                pl.BlockSpec((LRU_CONV, cbw), lambda s, c: (0, c)),
                pl.BlockSpec((1, cbw), lambda s, c: (0, c)),
                pl.BlockSpec((nb, LRU_BS, 4 * LRU_BS), lambda s, c: (c, 0, 0)),
                pl.BlockSpec((nb, 1, 4 * LRU_BS), lambda s, c: (c, 0, 0)),
                pl.BlockSpec((2, cbw), lambda s, c: (0, c))]
    args = [proj, proj, cw, cb, w4, b4, lam]
    state_spec = pl.BlockSpec((1, 1, 1, cbw), lambda s, c: (s, layer, 0, c))
    y_spec = pl.BlockSpec((seq_len, cbw), lambda s, c: (rb0 + s, c))
    if has_init:
        in_specs += [state_spec, state_spec]
        args += [init[0], init[1]]
        out_specs, out_shape = y_spec, _BRANCH_BUF
    else:
        st = jax.ShapeDtypeStruct((n_seq, DEPTH, 1, D_RNN), f32)
        out_specs, out_shape = [y_spec, state_spec, state_spec], [_BRANCH_BUF, st, st]
    scratch = [pltpu.VMEM((seq_len + 2 * V7X_SUBLANES, cbw), f32)] + \
              [pltpu.VMEM((seq_len, cbw), f32) for _ in range(6)]
    return _inplace_call(functools.partial(_lru_kernel, seq_len=seq_len, has_init=has_init, cbw=cbw), bufs,
                         (n_seq, ncb), in_specs, args, out_specs, out_shape, scratch, "lru_branch")


def _ret_kernel(*refs, seq_len, has_init):
    if has_init:
        (la_ref, q_ref, k_ref, v_ref, g_ref, gn_ref, s0f_ref, s0b_ref, y_ref, acc_ref) = refs
    else:
        (la_ref, q_ref, k_ref, v_ref, g_ref, gn_ref, y_ref, sf_ref, sb_ref, acc_ref) = refs
    T = min(RET_CHUNK, seq_len)
    nc = seq_len // T
    tt = lax.broadcasted_iota(jnp.int32, (T, T), 0)
    ss = lax.broadcasted_iota(jnp.int32, (T, T), 1)
    diff = (tt - ss).astype(f32)
    tcol = lax.broadcasted_iota(jnp.int32, (T, 1), 0).astype(f32)
    scale = RET_DK ** -0.5
    for h in range(RET_HEADS):
        la_f = la_ref[h, 0]
        la_b = la_ref[h, 1]
        kcols = pl.ds(h * RET_DK, RET_DK)
        vcols = pl.ds(h * RET_DV, RET_DV)
        dsum = (jnp.where(tt >= ss, jnp.exp(la_f * diff), 0.0)
                + jnp.where(ss >= tt, jnp.exp(-la_b * diff), 0.0))

        def chunk(c):
            rows = pl.ds(c * T, T)
            q = q_ref[rows, kcols].astype(f32)
            ks = k_ref[rows, kcols].astype(f32) * scale
            v = v_ref[rows, vcols].astype(f32)
            return rows, q, ks, v

        s_f = s0f_ref[0, 0, h] if has_init else None
        for c in range(nc):
            rows, q, ks, v = chunk(c)
            scores = _dot_nt(q.astype(bf16), ks.astype(bf16)) * dsum
            y = _dot(scores.astype(bf16), v.astype(bf16))
            if s_f is not None:
                y = y + _dot((q * jnp.exp(la_f * (tcol + 1.0))).astype(bf16), s_f.astype(bf16))
            acc_ref[rows, :] = y
            if c < nc - 1 or not has_init:
                upd = _dot(ks.T.astype(bf16), (v * jnp.exp(la_f * (T - 1.0 - tcol))).astype(bf16))
                s_f = upd if s_f is None else jnp.exp(la_f * T) * s_f + upd
        s_b = s0b_ref[0, 0, h] if has_init else None
        for c in reversed(range(nc)):
            rows, q, ks, v = chunk(c)
            if s_b is not None:
                acc_ref[rows, :] += _dot((q * jnp.exp(la_b * (T - tcol))).astype(bf16), s_b.astype(bf16))
            if c > 0 or not has_init:
                upd = _dot(ks.T.astype(bf16), (v * jnp.exp(la_b * tcol)).astype(bf16))
                s_b = upd if s_b is None else jnp.exp(la_b * T) * s_b + upd
        if not has_init:
            sf_ref[0, 0, h] = s_f
            sb_ref[0, 0, h] = s_b
        y = acc_ref[...]
        mu = jnp.mean(y, axis=-1, keepdims=True)
        yc = y - mu
        var = jnp.mean(yc * yc, axis=-1, keepdims=True)
        y = yc * lax.rsqrt(var + EPS) * gn_ref[:, vcols]
        y_ref[:, vcols] = (y * _silu(g_ref[:, vcols].astype(f32))).astype(bf16)


def retention_branch(proj, bufs, row0, n_seq, seq_len, la, gn, init, layer):
    has_init = init is not None
    rb0 = row0 // seq_len
    qk_w, vg_w = RET_HEADS * RET_DK, RET_HEADS * RET_DV
    in_specs = [pl.BlockSpec(memory_space=pltpu.SMEM),
                pl.BlockSpec((seq_len, qk_w), lambda s: (rb0 + s, OFF_RET_Q // qk_w)),
                pl.BlockSpec((seq_len, qk_w), lambda s: (rb0 + s, OFF_RET_K // qk_w)),
                pl.BlockSpec((seq_len, vg_w), lambda s: (rb0 + s, OFF_RET_V // vg_w)),
                pl.BlockSpec((seq_len, vg_w), lambda s: (rb0 + s, OFF_RET_G // vg_w)),
                pl.BlockSpec((1, vg_w), lambda s: (0, 0))]
    args = [la, proj, proj, proj, proj, gn]
    state_spec = pl.BlockSpec((1, 1, RET_HEADS, RET_DK, RET_DV), lambda s: (s, layer, 0, 0, 0))
    y_spec = pl.BlockSpec((seq_len, vg_w), lambda s: (rb0 + s, 0))
    if has_init:
        in_specs += [state_spec, state_spec]
        args += [init[0], init[1]]
        out_specs, out_shape = y_spec, _BRANCH_BUF
    else:
        st = jax.ShapeDtypeStruct((n_seq, DEPTH, RET_HEADS, RET_DK, RET_DV), f32)
        out_specs, out_shape = [y_spec, state_spec, state_spec], [_BRANCH_BUF, st, st]
    return _inplace_call(functools.partial(_ret_kernel, seq_len=seq_len, has_init=has_init), bufs,
                         (n_seq,), in_specs, args, out_specs, out_shape,
                         [pltpu.VMEM((seq_len, RET_DV), f32)], "retention_branch")


SSD_GW = SSD_HPG * SSD_P


def _split3(x):
    h1 = x.astype(bf16)
    r1 = x - h1.astype(f32)
    h2 = r1.astype(bf16)
    h3 = (r1 - h2.astype(f32)).astype(bf16)
    return h1, h2, h3


def _dot_exact_rhs(m, x):
    h1, h2, h3 = _split3(x)
    return _dot(m, h1) + _dot(m, h2) + _dot(m, h3)


def _dot_exact_lhs(x, m):
    h1, h2, h3 = _split3(x)
    return _dot(h1, m) + _dot(h2, m) + _dot(h3, m)


def _conv4_silu(src_ref, pad_ref, cw_ref, cb_ref, L, width):
    S = V7X_SUBLANES
    zeros = jnp.zeros((S, width), f32)
    pad_ref[pl.ds(0, S), pl.ds(0, width)] = zeros
    pad_ref[pl.ds(S + L, S), pl.ds(0, width)] = zeros
    pad_ref[pl.ds(S, L), pl.ds(0, width)] = src_ref[...].astype(f32)
    cw = cw_ref[...]
    out = cb_ref[...]
    for kk in range(SSD_CONV):
        out = out + pad_ref[pl.ds(S + kk - SSD_CONV // 2, L), pl.ds(0, width)] * cw[kk:kk + 1, :]
    return _silu(out)


def _heads_to_lanes(s_ref, st_ref):
    for hh in range(SSD_HPG):
        st_ref[:, pl.ds(hh * SSD_P, SSD_P)] = s_ref[0, 0, hh]
    return st_ref[...]


def _ssd_kernel(*refs, seq_len, has_init):
    if has_init:
        (z_ref, x_ref, b_ref, c_ref, dt_ref, cwx_ref, cbx_ref, cwb_ref, cbb_ref, cwc_ref, cbc_ref,
         prm_ref, dvec_ref, ng_ref, s0f_ref, s0b_ref,
         y_ref, pad_ref, xs_ref, bs_ref, cs_ref, acc_ref, rb_ref, yn_ref, st_ref) = refs
    else:
        (z_ref, x_ref, b_ref, c_ref, dt_ref, cwx_ref, cbx_ref, cwb_ref, cbb_ref, cwc_ref, cbc_ref,
         prm_ref, dvec_ref, ng_ref,
         y_ref, sf_ref, sb_ref, pad_ref, xs_ref, bs_ref, cs_ref, acc_ref, rb_ref, yn_ref, st_ref) = refs
    g = pl.program_id(1)
    L = seq_len
    T = min(SSD_CHUNK, L)
    nc = L // T
    H = SSD_HPG
    xs_ref[...] = _conv4_silu(x_ref, pad_ref, cwx_ref, cbx_ref, L, SSD_GW)
    bs_ref[...] = _conv4_silu(b_ref, pad_ref, cwb_ref, cbb_ref, L, SSD_N)
    cs_ref[...] = _conv4_silu(c_ref, pad_ref, cwc_ref, cbc_ref, L, SSD_N)
    prm = prm_ref[0]
    a_neg = -jnp.exp(prm[1:2, :])
    tt = lax.broadcasted_iota(jnp.int32, (T, T), 0)
    ss = lax.broadcasted_iota(jnp.int32, (T, T), 1)
    lower = tt >= ss
    upper = ss >= tt
    tri_l = jnp.where(lower, 1.0, 0.0).astype(bf16)
    tri_u = jnp.where(upper, 1.0, 0.0).astype(bf16)
    lower_b = lower[:SSD_TB, :SSD_TB]
    upper_b = upper[:SSD_TB, :SSD_TB]
    er = lax.broadcasted_iota(jnp.int32, (V7X_LANES, SSD_GW), 0)
    ec = lax.broadcasted_iota(jnp.int32, (V7X_LANES, SSD_GW), 1) // SSD_P
    exp_f = jnp.where(er == ec, 1.0, 0.0).astype(bf16)
    exp_b = jnp.where(er == ec + H, 1.0, 0.0).astype(bf16)

    def expand(w, e):
        hi = w.astype(bf16)
        lo = (w - hi.astype(f32)).astype(bf16)
        return _dot(hi, e) + _dot(lo, e)

    def chunk_terms(c):
        rows = pl.ds(c * T, T)
        dt = _softplus(dt_ref[rows, :].astype(f32) + prm[0:1, :])
        da = dt * a_neg
        return rows, dt, da

    s_f = _heads_to_lanes(s0f_ref, st_ref) if has_init else None
    for c in range(nc):
        rows, dt, da = chunk_terms(c)
        cum = _dot_exact_rhs(tri_l, da)
        rsum = _dot_exact_rhs(tri_u, da)
        rb_ref[rows, :] = rsum
        da_t = da.T
        cum_t = _dot_exact_lhs(da_t, tri_u)
        rsum_t = _dot_exact_lhs(da_t, tri_l)
        dt_t = dt.T
        bmat = bs_ref[rows, :]
        cmat = cs_ref[rows, :]
        xmat = xs_ref[rows, :]
        gmat = _dot_nt(cmat.astype(bf16), bmat.astype(bf16))
        for hh in range(H):
            cf, cf_t, dtf_t = cum[:, hh:hh + 1], cum_t[hh:hh + 1, :], dt_t[hh:hh + 1, :]
            rb, rb_t, dtb_t = rsum[:, H + hh:H + hh + 1], rsum_t[H + hh:H + hh + 1, :], dt_t[H + hh:H + hh + 1, :]
            block_rows = []
            for bi in range(T // SSD_TB):
                rr = slice(bi * SSD_TB, (bi + 1) * SSD_TB)
                blocks = []
                for bj in range(T // SSD_TB):
                    cc = slice(bj * SSD_TB, (bj + 1) * SSD_TB)
                    if bi > bj:
                        blk = jnp.exp(cf[rr] - cf_t[:, cc]) * dtf_t[:, cc]
                    elif bi < bj:
                        blk = jnp.exp(rb[rr] - rb_t[:, cc]) * dtb_t[:, cc]
                    else:
                        blk = (jnp.where(lower_b, jnp.exp(cf[rr] - cf_t[:, cc]), 0.0) * dtf_t[:, cc]
                               + jnp.where(upper_b, jnp.exp(rb[rr] - rb_t[:, cc]), 0.0) * dtb_t[:, cc])
                    blocks.append(gmat[rr, cc] * blk)
                block_rows.append(jnp.concatenate(blocks, axis=1))
            m = jnp.concatenate(block_rows, axis=0)
            xh = xmat[:, hh * SSD_P:(hh + 1) * SSD_P]
            acc_ref[rows, pl.ds(hh * SSD_P, SSD_P)] = _dot(m.astype(bf16), xh.astype(bf16))
        ecum = jnp.exp(cum)
        if s_f is not None:
            acc_ref[rows, :] += _dot(cmat.astype(bf16), s_f.astype(bf16)) * expand(ecum, exp_f)
        if c < nc - 1 or not has_init:
            tail = jnp.exp(cum[T - 1:T, :] - cum) * dt
            xw = (xmat * expand(tail, exp_f)).astype(bf16)
            upd = _dot(bmat.T.astype(bf16), xw)
            if s_f is None:
                s_f = upd
            else:
                s_f = s_f * expand(jnp.broadcast_to(ecum[T - 1:T, :], (V7X_SUBLANES, V7X_LANES)),
                                   exp_f)[0:1, :] + upd
    s_b = _heads_to_lanes(s0b_ref, st_ref) if has_init else None
    for c in reversed(range(nc)):
        rows, dt, da = chunk_terms(c)
        rsum = rb_ref[rows, :]
        ers = jnp.exp(rsum)
        bmat = bs_ref[rows, :]
        cmat = cs_ref[rows, :]
        xmat = xs_ref[rows, :]
        if s_b is not None:
            acc_ref[rows, :] += _dot(cmat.astype(bf16), s_b.astype(bf16)) * expand(ers, exp_b)
        if c > 0 or not has_init:
            tail = jnp.exp(rsum[0:1, :] - rsum) * dt
            xw = (xmat * expand(tail, exp_b)).astype(bf16)
            upd = _dot(bmat.T.astype(bf16), xw)
            if s_b is None:
                s_b = upd
            else:
                s_b = s_b * expand(jnp.broadcast_to(ers[0:1, :], (V7X_SUBLANES, V7X_LANES)),
                                   exp_b)[0:1, :] + upd
    if not has_init:
        for hh in range(H):
            sf_ref[0, 0, hh] = s_f[:, hh * SSD_P:(hh + 1) * SSD_P]
            sb_ref[0, 0, hh] = s_b[:, hh * SSD_P:(hh + 1) * SSD_P]
    yg = (acc_ref[...] + xs_ref[...] * dvec_ref[...]) * _silu(z_ref[...].astype(f32))
    yn_ref[g] = yg

    @pl.when(g == SSD_GROUPS - 1)
    def _():
        ssq = None
        for gg in range(SSD_GROUPS):
            y = yn_ref[gg]
            s = jnp.sum(y * y, axis=-1, keepdims=True)
            ssq = s if ssq is None else ssq + s
        inv = lax.rsqrt(ssq * (1.0 / D_SSD) + EPS)
        for gg in range(SSD_GROUPS):
            cols = pl.ds(gg * SSD_GW, SSD_GW)
            y_ref[:, cols] = (yn_ref[gg] * inv * ng_ref[:, cols]).astype(bf16)


def ssd_branch(proj, bufs, row0, n_seq, seq_len, conv_w, conv_b, prm, dvec, ng, init, layer):
    has_init = init is not None
    rb0 = row0 // seq_len
    xoff = OFF_SSD_XBC
    boff = OFF_SSD_XBC + D_SSD
    coff = boff + SSD_GROUPS * SSD_N
    in_specs = [pl.BlockSpec((seq_len, SSD_GW), lambda s, g: (rb0 + s, OFF_SSD_Z // SSD_GW + g)),
                pl.BlockSpec((seq_len, SSD_GW), lambda s, g: (rb0 + s, xoff // SSD_GW + g)),
                pl.BlockSpec((seq_len, SSD_N), lambda s, g: (rb0 + s, boff // SSD_N + g)),
                pl.BlockSpec((seq_len, SSD_N), lambda s, g: (rb0 + s, coff // SSD_N + g)),
                pl.BlockSpec((seq_len, V7X_LANES), lambda s, g: (rb0 + s, OFF_SSD_DT // V7X_LANES + g)),
                pl.BlockSpec((SSD_CONV, SSD_GW), lambda s, g: (0, g)),
                pl.BlockSpec((1, SSD_GW), lambda s, g: (0, g)),
                pl.BlockSpec((SSD_CONV, SSD_N), lambda s, g: (0, D_SSD // SSD_N + g)),
                pl.BlockSpec((1, SSD_N), lambda s, g: (0, D_SSD // SSD_N + g)),
                pl.BlockSpec((SSD_CONV, SSD_N), lambda s, g: (0, D_SSD // SSD_N + SSD_GROUPS + g)),
                pl.BlockSpec((1, SSD_N), lambda s, g: (0, D_SSD // SSD_N + SSD_GROUPS + g)),
                pl.BlockSpec((1, V7X_SUBLANES, V7X_LANES), lambda s, g: (g, 0, 0)),
                pl.BlockSpec((1, SSD_GW), lambda s, g: (0, g)),
                pl.BlockSpec((1, D_SSD), lambda s, g: (0, 0))]
    args = [proj, proj, proj, proj, proj, conv_w, conv_b, conv_w, conv_b, conv_w, conv_b, prm, dvec, ng]
    state_spec = pl.BlockSpec((1, 1, SSD_HPG, SSD_N, SSD_P), lambda s, g: (s, layer, g, 0, 0))
    y_spec = pl.BlockSpec((seq_len, D_SSD), lambda s, g: (rb0 + s, 0))
    if has_init:
        in_specs += [state_spec, state_spec]
        args += [init[0], init[1]]
        out_specs, out_shape = y_spec, _BRANCH_BUF
    else:
        st = jax.ShapeDtypeStruct((n_seq, DEPTH, SSD_HEADS, SSD_N, SSD_P), f32)
        out_specs, out_shape = [y_spec, state_spec, state_spec], [_BRANCH_BUF, st, st]
    scratch = [pltpu.VMEM((seq_len + 2 * V7X_SUBLANES, SSD_GW), f32),
               pltpu.VMEM((seq_len, SSD_GW), f32),
               pltpu.VMEM((seq_len, SSD_N), f32),
               pltpu.VMEM((seq_len, SSD_N), f32),
               pltpu.VMEM((seq_len, SSD_GW), f32),
               pltpu.VMEM((seq_len, V7X_LANES), f32),
               pltpu.VMEM((SSD_GROUPS, seq_len, SSD_GW), f32),
               pltpu.VMEM((SSD_N, SSD_GW), f32)]
    return _inplace_call(functools.partial(_ssd_kernel, seq_len=seq_len, has_init=has_init), bufs,
                         (n_seq, SSD_GROUPS), in_specs, args, out_specs, out_shape, scratch, "ssd_branch")


def _head_rmsnorm(x, g):
    return x * lax.rsqrt(jnp.mean(x * x, axis=-1, keepdims=True) + EPS) * g


CTX_HB = 4
CTX_W = CTX_HB * NA_HD


def _ctx_attn_kernel(q_ref, k_ref, v_ref, qg_ref, kg_ref, y_ref, ko_ref, vo_ref):
    scale = NA_HD ** -0.5
    vo_ref[0, 0] = v_ref[...].astype(f32)
    for h in range(CTX_HB):
        cols = pl.ds(h * NA_HD, NA_HD)
        q = _head_rmsnorm(q_ref[:, cols].astype(f32), qg_ref[...])
        k = _head_rmsnorm(k_ref[:, cols].astype(f32), kg_ref[...])
        ko_ref[0, 0, :, cols] = k
        s = _dot_nt(q.astype(bf16), k.astype(bf16)) * scale
        p = jnp.exp(s - jnp.max(s, axis=-1, keepdims=True))
        o = _dot(p.astype(bf16), v_ref[:, cols]) / jnp.sum(p, axis=-1, keepdims=True)
        y_ref[:, cols] = o.astype(bf16)


def context_attention(proj, bufs, row0, qg, kg, layer):
    rb0 = row0 // SEQ
    spec = lambda off: pl.BlockSpec((SEQ, CTX_W), lambda s, hb: (rb0 + s, off // CTX_W + hb))
    gspec = pl.BlockSpec((1, NA_HD), lambda s, hb: (0, 0))
    kv_spec = pl.BlockSpec((1, 1, SEQ, CTX_W), lambda s, hb: (s, layer, 0, hb))
    kv_shape = jax.ShapeDtypeStruct((BATCH, DEPTH, SEQ, NA_W), f32)
    return _inplace_call(
        _ctx_attn_kernel, bufs, (BATCH, NA_HEADS // CTX_HB),
        [spec(OFF_NA_Q), spec(OFF_NA_K), spec(OFF_NA_V), gspec, gspec], [proj, proj, proj, qg, kg],
        [pl.BlockSpec((SEQ, CTX_W), lambda s, hb: (rb0 + s, hb)), kv_spec, kv_spec],
        [_BRANCH_BUF, kv_shape, kv_shape], [], "context_attention")


NA_ROWS = DEC_SEQ // GRID_W
NA_NK = NA_WR * GRID_W


def _na_row_start(r):
    return min(max(r - NA_WR // 2, 0), NA_ROWS - NA_WR)


def _rope(x, cos, sin_signed):
    lane = lax.broadcasted_iota(jnp.int32, x.shape, 1)
    quarter = NA_HD // 4
    swapped = jnp.where((lane & (2 * quarter - 1)) < quarter,
                        pltpu.roll(x, NA_HD - quarter, axis=1), pltpu.roll(x, quarter, axis=1))
    return x * cos + swapped * sin_signed


def _na_row_groups():
    groups, r = [], 0
    while r < NA_ROWS:
        n = 1
        while r + n < NA_ROWS and _na_row_start(r + n) == _na_row_start(r):
            n += 1
        groups.append((r, n, _na_row_start(r)))
        r += n
    return groups


def _na_attn_kernel(q_ref, k_ref, v_ref, kc_ref, vc_ref, qg_ref, kg_ref, cos_ref, sin_ref,
                    bias_ref, valid_ref, y_ref, qs_ref, ks_ref, s_ref, p_ref, oc_ref, w_ref):
    scale = NA_HD ** -0.5
    cos = cos_ref[...]
    sin = sin_ref[...]
    qs_ref[...] = _rope(_head_rmsnorm(q_ref[...].astype(f32), qg_ref[...]), cos, sin).astype(bf16)
    ks_ref[...] = _rope(_head_rmsnorm(k_ref[...].astype(f32), kg_ref[...]), cos, sin).astype(bf16)
    groups = _na_row_groups()
    valid = valid_ref[...] > 0.0
    for r0, n, rs in groups:
        rows = pl.ds(r0 * GRID_W, n * GRID_W)
        d0 = r0 - rs
        kw = ks_ref[pl.ds(rs * GRID_W, NA_NK), :]
        s = (_dot_nt(qs_ref[rows, :], kw) * scale).reshape(n, GRID_W, NA_NK) + bias_ref[0, d0:d0 + n]
        s_ref[rows, :] = jnp.where(valid[None], s, -1e30).reshape(n * GRID_W, NA_NK)
    s_ctx = _dot_nt(qs_ref[...], kc_ref[0, 0].astype(bf16)) * scale
    s_loc = s_ref[...]
    m = jnp.maximum(jnp.max(s_loc, axis=-1, keepdims=True), jnp.max(s_ctx, axis=-1, keepdims=True))
    p_loc = jnp.exp(s_loc - m)
    p_ctx = jnp.exp(s_ctx - m)
    w_ref[...] = 1.0 / (jnp.sum(p_loc, axis=-1, keepdims=True) + jnp.sum(p_ctx, axis=-1, keepdims=True))
    p_ref[...] = p_loc.astype(bf16)
    oc_ref[...] = _dot(p_ctx.astype(bf16), vc_ref[0, 0].astype(bf16))
    for r0, n, rs in groups:
        rows = pl.ds(r0 * GRID_W, n * GRID_W)
        vw = v_ref[pl.ds(rs * GRID_W, NA_NK), :]
        y_ref[rows, :] = ((_dot(p_ref[rows, :], vw) + oc_ref[rows, :]) * w_ref[rows, :]).astype(bf16)


def neighbourhood_attention(proj, cache_k, cache_v, layer, qg, kg, cos, sin, bias, valid):
    spec = lambda off: pl.BlockSpec((DEC_SEQ, NA_HD), lambda b, h: (b, off // NA_HD + h))
    cspec = pl.BlockSpec((1, 1, PAST_LEN, NA_HD), lambda b, h: (b, layer, 0, h))
    gspec = pl.BlockSpec((1, NA_HD), lambda b, h: (0, 0))
    tspec = pl.BlockSpec((DEC_SEQ, NA_HD), lambda b, h: (0, 0))
    return pl.pallas_call(
        _na_attn_kernel,
        grid=(DEC_BATCH, NA_HEADS),
        in_specs=[spec(OFF_NA_Q), spec(OFF_NA_K), spec(OFF_NA_V), cspec, cspec, gspec, gspec,
                  tspec, tspec,
                  pl.BlockSpec((1, NA_WR, GRID_W, NA_NK), lambda b, h: (h, 0, 0, 0)),
                  pl.BlockSpec((GRID_W, NA_NK), lambda b, h: (0, 0))],
        out_specs=pl.BlockSpec((DEC_SEQ, NA_HD), lambda b, h: (b, h)),
        out_shape=_BRANCH_BUF,
        scratch_shapes=[pltpu.VMEM((DEC_SEQ, NA_HD), bf16), pltpu.VMEM((DEC_SEQ, NA_HD), bf16),
                        pltpu.VMEM((DEC_SEQ, NA_NK), f32), pltpu.VMEM((DEC_SEQ, NA_NK), bf16),
                        pltpu.VMEM((DEC_SEQ, NA_HD), f32), pltpu.VMEM((DEC_SEQ, 1), f32)],
        compiler_params=_params(2), name="neighbourhood_attention",
    )(proj, proj, proj, cache_k, cache_v, qg, kg, cos, sin, bias, valid)


def _cast_kernel(w_ref, o_ref):
    o_ref[...] = w_ref[...].astype(bf16)


def _cast_head(w_in):
    spec = pl.BlockSpec((1, D_MODEL, PROJ_TN), lambda l, j: (l, 0, j))
    return pl.pallas_call(
        _cast_kernel, grid=(DEPTH, N_HEAD_TILES), in_specs=[spec], out_specs=spec,
        out_shape=jax.ShapeDtypeStruct((DEPTH, D_MODEL, HEAD_COLS), bf16),
        compiler_params=_params(2), name="cast_w_in_head",
    )(w_in)


def _split_w_in(w_in):
    dt0, na0 = IN_OFFSETS[7], IN_OFFSETS[8]
    head = _cast_head(w_in)
    pieces = [w_in[..., HEAD_COLS:dt0].astype(bf16), w_in[..., na0:].astype(bf16)]
    dt = w_in[..., dt0:na0].astype(bf16)
    for gidx in range(SSD_GROUPS):
        pieces.append(dt[..., gidx * SSD_HPG:(gidx + 1) * SSD_HPG])
        pieces.append(dt[..., SSD_HEADS + gidx * SSD_HPG:SSD_HEADS + (gidx + 1) * SSD_HPG])
        pieces.append(jnp.zeros((DEPTH, D_MODEL, V7X_LANES - 2 * SSD_HPG), bf16))
    used = OFF_SSD_DT + SSD_GROUPS * V7X_LANES
    pieces.append(jnp.zeros((DEPTH, D_MODEL, D_IN_PAD - used), bf16))
    return head, jnp.concatenate(pieces, axis=-1)


def _group_lanes(v):
    rows = []
    for gidx in range(SSD_GROUPS):
        sl = slice(gidx * SSD_HPG, (gidx + 1) * SSD_HPG)
        rows.append(jnp.concatenate([v[0, sl], v[1, sl], jnp.zeros((V7X_LANES - 2 * SSD_HPG,), f32)]))
    return jnp.stack(rows)


def _rope_tables():
    t = np.arange(DEC_SEQ)
    quarter = NA_HD // 4
    inv = ROPE_BASE ** (-np.arange(quarter, dtype=np.float32) / quarter)
    ang_r = (t // GRID_W).astype(np.float32)[:, None] * inv
    ang_c = (t % GRID_W).astype(np.float32)[:, None] * inv
    cos = np.concatenate([np.cos(ang_r), np.cos(ang_r), np.cos(ang_c), np.cos(ang_c)], axis=1)
    sin = np.concatenate([-np.sin(ang_r), np.sin(ang_r), -np.sin(ang_c), np.sin(ang_c)], axis=1)
    return jnp.asarray(cos, f32), jnp.asarray(sin, f32)


def _na_tables(rpb):
    cq = np.arange(GRID_W)
    kc = np.tile(np.arange(GRID_W), NA_WR)
    col_start = np.clip(cq - NA_WC // 2, 0, GRID_W - NA_WC)
    valid = (kc[None, :] >= col_start[:, None]) & (kc[None, :] < col_start[:, None] + NA_WC)
    col_off = np.clip(cq[None, :] - cq[:, None], 1 - NA_WC, NA_WC - 1) + NA_WC - 1
    onehot = (col_off[None, :, :] == np.arange(2 * NA_WC - 1)[:, None, None]).astype(np.float32)
    toep = jnp.einsum('hic,cqk->hiqk', rpb.astype(f32), jnp.asarray(onehot), precision=lax.Precision.HIGHEST)
    tables = []
    for d in range(NA_WR):
        rows = toep[:, NA_WR - 1 - d:2 * NA_WR - 1 - d]
        tables.append(rows.transpose(0, 2, 1, 3).reshape(NA_HEADS, GRID_W, NA_NK))
    return jnp.stack(tables, axis=1), jnp.asarray(valid, f32)


def kernel(x_prompt, x_sample, cache_na_k, cache_na_v, state_lru_f, state_lru_b,
           state_ret_f, state_ret_b, state_ssd_f, state_ssd_b, c, c_ctx,
           norm1_g, norm2_g, w_ada, b_ada, w_in, w_gate, b_gate, w_branch, w_out,
           lru_conv_w, lru_conv_b, lru_wa, lru_ba, lru_wx, lru_bx, lru_lambda,
           ret_gn_g, ssd_conv_w, ssd_conv_b, ssd_a_log, ssd_dt_bias, ssd_d, ssd_norm_g,
           na_q_g, na_k_g, na_rpb, ffn_w_up, ffn_conv_w, ffn_conv_b, ffn_w_down):
    xs0 = x_sample.reshape(T_SAMPLE, D_MODEL)
    xp0 = x_prompt.reshape(T_PROMPT, D_MODEL)
    n_ptiles = N_TILES - N_SAMPLE_TILES
    cond = jnp.concatenate([c, c_ctx[None, :], jnp.zeros((N_COND_PAD - N_COND, D_MODEL), f32)], axis=0)
    mod_all = ada_modulation(cond, w_ada, b_ada)
    cos, sin = _rope_tables()
    hh = jnp.arange(RET_HEADS, dtype=f32)
    ret_la = jnp.stack([jnp.log1p(-jnp.exp2(-5.0 - hh)), jnp.log1p(-jnp.exp2(-5.5 - hh))], axis=1)
    cache_k = cache_na_k.reshape(DEC_BATCH, DEPTH, PAST_LEN, NA_W)
    cache_v = cache_na_v.reshape(DEC_BATCH, DEPTH, PAST_LEN, NA_W)
    lru_init = (state_lru_f.reshape(DEC_BATCH, DEPTH, 1, D_RNN), state_lru_b.reshape(DEC_BATCH, DEPTH, 1, D_RNN))

    w_head_b, w_tail_b = _split_w_in(w_in)
    w_branch_b, w_out_b, w_down_b = w_branch.astype(bf16), w_out.astype(bf16), ffn_w_down.astype(bf16)

    new = {k: None for k in ("k", "v", "lru_f", "lru_b", "ret_f", "ret_b", "ssd_f", "ssd_b")}
    for l in range(DEPTH):
        mod = mod_all[l].reshape(N_COND_PAD, 1, N_MOD * D_MODEL)
        g1 = norm1_g[l][None, :]
        if l == 0:
            proj, xn = in_projection(xs0, 0, 0, N_SAMPLE_TILES, g1, mod, w_head_b, w_tail_b, l)
            proj, xn = in_projection(xp0, 0, N_SAMPLE_TILES, n_ptiles, g1, mod, w_head_b, w_tail_b, l, (proj, xn))
        else:
            proj, xn = in_projection(x, 0, 0, N_TILES, g1, mod, w_head_b, w_tail_b, l)

        w4 = jnp.concatenate([lru_wa[l, 0], lru_wx[l, 0], lru_wa[l, 1], lru_wx[l, 1]], axis=-1).astype(bf16)
        b4 = jnp.concatenate([lru_ba[l, 0].reshape(LRU_BLOCKS, 1, LRU_BS), lru_bx[l, 0].reshape(LRU_BLOCKS, 1, LRU_BS),
                              lru_ba[l, 1].reshape(LRU_BLOCKS, 1, LRU_BS), lru_bx[l, 1].reshape(LRU_BLOCKS, 1, LRU_BS)],
                             axis=-1)
        lru_args = (lru_conv_w[l], lru_conv_b[l][None, :], w4, b4, lru_lambda[l])
        y_lru = lru_branch(proj, (None,), 0, DEC_BATCH, DEC_SEQ, *lru_args, lru_init, l)
        y_lru, new["lru_f"], new["lru_b"] = lru_branch(
            proj, (y_lru, new["lru_f"], new["lru_b"]), T_SAMPLE, BATCH, SEQ, *lru_args, None, l)

        gn = ret_gn_g[l][None, :]
        y_ret = retention_branch(proj, (None,), 0, DEC_BATCH, DEC_SEQ, ret_la, gn, (state_ret_f, state_ret_b), l)
        y_ret, new["ret_f"], new["ret_b"] = retention_branch(
            proj, (y_ret, new["ret_f"], new["ret_b"]), T_SAMPLE, BATCH, SEQ, ret_la, gn, None, l)

        prm = jnp.stack([_group_lanes(ssd_dt_bias[l]), _group_lanes(ssd_a_log[l])], axis=1)
        prm = jnp.concatenate([prm, jnp.zeros((SSD_GROUPS, V7X_SUBLANES - 2, V7X_LANES), f32)], axis=1)
        dvec = jnp.repeat(ssd_d[l], SSD_P)[None, :]
        ssd_args = (ssd_conv_w[l], ssd_conv_b[l][None, :], prm, dvec, ssd_norm_g[l][None, :])
        y_ssd = ssd_branch(proj, (None,), 0, DEC_BATCH, DEC_SEQ, *ssd_args, (state_ssd_f, state_ssd_b), l)
        y_ssd, new["ssd_f"], new["ssd_b"] = ssd_branch(
            proj, (y_ssd, new["ssd_f"], new["ssd_b"]), T_SAMPLE, BATCH, SEQ, *ssd_args, None, l)

        qg = na_q_g[l][None, :]
        kg = na_k_g[l][None, :]
        bias, valid = _na_tables(na_rpb[l])
        y_na = neighbourhood_attention(proj, cache_k, cache_v, l, qg, kg, cos, sin, bias, valid)
        y_na, new["k"], new["v"] = context_attention(proj, (y_na, new["k"], new["v"]), T_SAMPLE, qg, kg, l)

        merged = merge_branches(xn, (y_lru, y_ret, y_ssd, y_na), w_gate, b_gate[l][None, :], w_branch_b, l)
        if l == 0:
            x = residual_projection(merged, w_out_b, l, xs0, mod, 2, 0, N_SAMPLE_TILES)
            x = residual_projection(merged, w_out_b, l, xp0, mod, 2, N_SAMPLE_TILES, n_ptiles, x_tile0=0, buf=x)
        else:
            x = residual_projection(merged, w_out_b, l, x, mod, 2)
        hmid = ffn_up(x, norm2_g[l][None, :], mod, ffn_w_up, l, ffn_conv_w[l], ffn_conv_b[l][None, :])
        if l < DEPTH - 1:
            x = residual_projection(hmid, w_down_b, l, x, mod, 5)
        else:
            y_sample = residual_projection(hmid, w_down_b, l, x, mod, 5, 0, N_SAMPLE_TILES, out_rows=T_SAMPLE)
            y_prompt = residual_projection(hmid, w_down_b, l, x, mod, 5, N_SAMPLE_TILES, n_ptiles,
                                           out_tile0=0, out_rows=T_PROMPT)

    kv_shape = (BATCH, DEPTH, SEQ, NA_HEADS, NA_HD)
    return (y_prompt.reshape(BATCH, SEQ, D_MODEL), y_sample.reshape(DEC_BATCH, DEC_SEQ, D_MODEL),
            new["k"].reshape(kv_shape), new["v"].reshape(kv_shape),
            new["lru_f"].reshape(BATCH, DEPTH, D_RNN), new["lru_b"].reshape(BATCH, DEPTH, D_RNN),
            new["ret_f"], new["ret_b"], new["ssd_f"], new["ssd_b"])
```

```python
import functools
import math

import jax
import jax.numpy as jnp
import numpy as np
from jax import lax
from jax.experimental import pallas as pl
from jax.experimental.pallas import tpu as pltpu

D_MODEL = 2048
BATCH = 16
SEQ = 256
DEPTH = 2
DEC_BATCH = 8
DEC_SEQ = 1024
PAST_LEN = 256
GRID_W = 64
EPS = 1e-6
N_BRANCH = 4
BRANCH_W = 1024
N_MOD = 6
D_RNN = 1024
LRU_BLOCKS = 8
LRU_BS = D_RNN // LRU_BLOCKS
LRU_CONV = 4
LRU_C = 8.0
RET_HEADS = 4
RET_DK = 128
RET_DV = 256
SSD_HEADS = 16
SSD_P = 64
SSD_N = 128
SSD_GROUPS = 2
SSD_CONV = 4
D_SSD = SSD_HEADS * SSD_P
SSD_CONV_CH = D_SSD + 2 * SSD_GROUPS * SSD_N
NA_HEADS = 8
NA_HD = 128
NA_W = NA_HEADS * NA_HD
NA_WR = 8
NA_WC = 16
ROPE_BASE = 10000.0
D_FF = 5632
FFN_CONV = 3
IN_SIZES = (D_RNN, D_RNN,
            RET_HEADS * RET_DK, RET_HEADS * RET_DK, RET_HEADS * RET_DV, RET_HEADS * RET_DV,
            D_SSD, SSD_CONV_CH, 2 * SSD_HEADS,
            NA_W, NA_W, NA_W)
D_IN = sum(IN_SIZES)
IN_OFFSETS = tuple(int(s) for s in np.cumsum(IN_SIZES)[:-1])

V7X_LANES = 128
V7X_SUBLANES = 8
V7X_VMEM_LIMIT_BYTES = 56 * 1024 * 1024

TM = 1024
T_SAMPLE = DEC_BATCH * DEC_SEQ
T_PROMPT = BATCH * SEQ
T_ALL = T_SAMPLE + T_PROMPT
N_SAMPLE_TILES = T_SAMPLE // TM
N_TILES = T_ALL // TM
N_COND = DEC_BATCH + 1
N_COND_PAD = 16

SSD_HPG = SSD_HEADS // SSD_GROUPS
OFF_LRU_X = 0
OFF_LRU_G = OFF_LRU_X + D_RNN
OFF_RET_Q = OFF_LRU_G + D_RNN
OFF_RET_K = OFF_RET_Q + RET_HEADS * RET_DK
OFF_RET_V = OFF_RET_K + RET_HEADS * RET_DK
OFF_RET_G = OFF_RET_V + RET_HEADS * RET_DV
OFF_SSD_Z = OFF_RET_G + RET_HEADS * RET_DV
OFF_SSD_XBC = OFF_SSD_Z + D_SSD
OFF_NA_Q = OFF_SSD_XBC + SSD_CONV_CH
OFF_NA_K = OFF_NA_Q + NA_W
OFF_NA_V = OFF_NA_K + NA_W
OFF_SSD_DT = OFF_NA_V + NA_W
PROJ_TN = 1024
D_IN_PAD = -(-(OFF_SSD_DT + SSD_GROUPS * V7X_LANES) // PROJ_TN) * PROJ_TN
N_HEAD_TILES = OFF_NA_Q // PROJ_TN
HEAD_COLS = N_HEAD_TILES * PROJ_TN

RET_CHUNK = 256
SSD_CHUNK = 256
SSD_TB = 128

f32 = jnp.float32
bf16 = jnp.bfloat16

_ARB = "arbitrary"


def _params(n_axes):
    return pltpu.CompilerParams(dimension_semantics=(_ARB,) * n_axes,
                                vmem_limit_bytes=V7X_VMEM_LIMIT_BYTES)


def _mod_spec(k, tile0=0):
    return pl.BlockSpec((1, 1, D_MODEL), lambda i, j: (jnp.minimum(tile0 + i, N_SAMPLE_TILES), 0, k))


def _dot(a, b):
    return jnp.dot(a, b, preferred_element_type=f32)


def _dot_nt(a, b):
    return lax.dot_general(a, b, (((1,), (1,)), ((), ())), preferred_element_type=f32)


def _sigmoid(x):
    return 0.5 * jnp.tanh(0.5 * x) + 0.5


def _silu(x):
    return x * _sigmoid(x)


def _gelu_tanh(x):
    return 0.5 * x * (1.0 + jnp.tanh(math.sqrt(2.0 / math.pi) * (x + 0.044715 * (x * x * x))))


def _softplus(x):
    return jnp.maximum(x, 0.0) + jnp.log1p(jnp.exp(-jnp.abs(x)))


def _drop_refs(body, n, *refs):
    body(*refs[n:])


def _inplace_call(body, bufs, grid, in_specs, args, out_specs, out_shape, scratch, name):
    held = [(k, b) for k, b in enumerate(bufs) if b is not None]
    if held:
        body = functools.partial(_drop_refs, body, len(held))
        in_specs = [pl.BlockSpec(memory_space=pl.ANY)] * len(held) + list(in_specs)
        args = [b for _, b in held] + list(args)
    aliases = {pos: k for pos, (k, _) in enumerate(held)}
    return pl.pallas_call(
        body, grid=grid, in_specs=in_specs, out_specs=out_specs, out_shape=out_shape,
        scratch_shapes=scratch, input_output_aliases=aliases,
        compiler_params=_params(len(grid)), name=name,
    )(*args)


_BRANCH_BUF = jax.ShapeDtypeStruct((T_ALL, BRANCH_W), bf16)


ADA_TN = 1024


def _ada_kernel(c_ref, w_ref, b_ref, o_ref):
    c = _silu(c_ref[...]).astype(bf16)
    o_ref[0] = _dot(c, w_ref[0].astype(bf16)) + b_ref[0]


def ada_modulation(cond, w_ada, b_ada):
    n = N_MOD * D_MODEL
    return pl.pallas_call(
        _ada_kernel,
        grid=(DEPTH, n // ADA_TN),
        in_specs=[pl.BlockSpec((N_COND_PAD, D_MODEL), lambda l, j: (0, 0)),
                  pl.BlockSpec((1, D_MODEL, ADA_TN), lambda l, j: (l, 0, j)),
                  pl.BlockSpec((1, 1, ADA_TN), lambda l, j: (l, 0, j))],
        out_specs=pl.BlockSpec((1, N_COND_PAD, ADA_TN), lambda l, j: (l, 0, j)),
        out_shape=jax.ShapeDtypeStruct((DEPTH, N_COND_PAD, n), f32),
        compiler_params=_params(2),
        name="ada_modulation",
    )(cond, w_ada, b_ada.reshape(DEPTH, 1, n))


NORM_ROWS = 128


def _modulated_norm(x_ref, g_ref, sc_ref, sh_ref, xn_ref):
    g = g_ref[...]
    sc = 1.0 + sc_ref[0]
    sh = sh_ref[0]

    def body(r, carry):
        rows = pl.ds(pl.multiple_of(r * NORM_ROWS, NORM_ROWS), NORM_ROWS)
        x = x_ref[rows, :]
        y = x * lax.rsqrt(jnp.mean(x * x, axis=-1, keepdims=True) + EPS)
        xn_ref[rows, :] = ((y * g) * sc + sh).astype(bf16)
        return carry

    lax.fori_loop(0, TM // NORM_ROWS, body, 0)


def _in_proj_kernel(x_ref, g_ref, sc_ref, sh_ref, wh_ref, wt_ref, o_ref, xn_ref):
    j = pl.program_id(1)

    @pl.when(j == 0)
    def _():
        _modulated_norm(x_ref, g_ref, sc_ref, sh_ref, xn_ref)

    @pl.when(j < N_HEAD_TILES)
    def _():
        o_ref[...] = _dot(xn_ref[...], wh_ref[0]).astype(bf16)

    @pl.when(j >= N_HEAD_TILES)
    def _():
        o_ref[...] = _dot(xn_ref[...], wt_ref[0]).astype(bf16)


def in_projection(x, x_tile0, tile0, n_tiles, g, mod, w_head, w_tail, layer, bufs=(None, None)):
    n = D_IN_PAD
    n_tail = w_tail.shape[2] // PROJ_TN
    return _inplace_call(
        _in_proj_kernel, bufs, (n_tiles, n // PROJ_TN),
        [pl.BlockSpec((TM, D_MODEL), lambda i, j: (x_tile0 + i, 0)),
         pl.BlockSpec((1, D_MODEL), lambda i, j: (0, 0)),
         _mod_spec(1, tile0), _mod_spec(0, tile0),
         pl.BlockSpec((1, D_MODEL, PROJ_TN), lambda i, j: (layer, 0, jnp.minimum(j, N_HEAD_TILES - 1))),
         pl.BlockSpec((1, D_MODEL, PROJ_TN),
                      lambda i, j: (layer, 0, jnp.clip(j - N_HEAD_TILES, 0, n_tail - 1)))],
        [x, g, mod, mod, w_head, w_tail],
        [pl.BlockSpec((TM, PROJ_TN), lambda i, j: (tile0 + i, j)),
         pl.BlockSpec((TM, D_MODEL), lambda i, j: (tile0 + i, 0))],
        [jax.ShapeDtypeStruct((T_ALL, n), bf16), jax.ShapeDtypeStruct((T_ALL, D_MODEL), bf16)],
        [], "in_projection")


MERGE_TN = 256


def _merge_kernel(xn_ref, *refs):
    y_refs, wg_refs, bg_refs = refs[0:4], refs[4:8], refs[8:12]
    wb_ref, o_ref = refs[12], refs[13]
    xn = xn_ref[...]
    acc = None
    for n in range(N_BRANCH):
        gate = _sigmoid(_dot(xn, wg_refs[n][0].astype(bf16)) + bg_refs[n][...])
        term = gate * _dot(y_refs[n][...], wb_ref[0, n])
        acc = term if acc is None else acc + term
    o_ref[...] = acc.astype(bf16)


def merge_branches(xn, ys, w_gate, b_gate, w_branch, layer):
    nj = D_MODEL // MERGE_TN
    y_spec = pl.BlockSpec((TM, BRANCH_W), lambda i, j: (i, 0))
    wg_specs = [pl.BlockSpec((1, D_MODEL, MERGE_TN), lambda i, j, n=n: (layer, 0, n * nj + j))
                for n in range(N_BRANCH)]
    bg_specs = [pl.BlockSpec((1, MERGE_TN), lambda i, j, n=n: (0, n * nj + j)) for n in range(N_BRANCH)]
    return pl.pallas_call(
        _merge_kernel,
        grid=(N_TILES, nj),
        in_specs=[pl.BlockSpec((TM, D_MODEL), lambda i, j: (i, 0)),
                  y_spec, y_spec, y_spec, y_spec, *wg_specs, *bg_specs,
                  pl.BlockSpec((1, N_BRANCH, BRANCH_W, MERGE_TN), lambda i, j: (layer, 0, 0, j))],
        out_specs=pl.BlockSpec((TM, MERGE_TN), lambda i, j: (i, j)),
        out_shape=jax.ShapeDtypeStruct((T_ALL, D_MODEL), bf16),
        compiler_params=_params(2),
        name="merge_branches",
    )(xn, *ys, *([w_gate] * N_BRANCH), *([b_gate] * N_BRANCH), w_branch)


RES_TN = 512
RES_TN_SHORT_K = 1024


def _residual_kernel(a_ref, w_ref, x_ref, gv_ref, o_ref):
    o_ref[...] = x_ref[...] + gv_ref[0] * _dot(a_ref[...], w_ref[0])


def residual_projection(a, w, layer, x, mod, k_mod, tile0=0, n_tiles=N_TILES, x_tile0=None, out_tile0=None,
                        out_rows=T_ALL, buf=None):
    kdim = a.shape[1]
    tn = RES_TN_SHORT_K if kdim == D_MODEL else RES_TN
    x_tile0 = tile0 if x_tile0 is None else x_tile0
    out_tile0 = tile0 if out_tile0 is None else out_tile0
    return _inplace_call(
        _residual_kernel, (buf,), (n_tiles, D_MODEL // tn),
        [pl.BlockSpec((TM, kdim), lambda i, j: (tile0 + i, 0)),
         pl.BlockSpec((1, kdim, tn), lambda i, j: (layer, 0, j)),
         pl.BlockSpec((TM, tn), lambda i, j: (x_tile0 + i, j)),
         pl.BlockSpec((1, 1, tn),
                      lambda i, j: (jnp.minimum(tile0 + i, N_SAMPLE_TILES), 0, k_mod * (D_MODEL // tn) + j))],
        [a, w, x, mod],
        pl.BlockSpec((TM, tn), lambda i, j: (out_tile0 + i, j)),
        jax.ShapeDtypeStruct((out_rows, D_MODEL), f32), [], "residual_projection")


FFN_TN = 512


def _ffn_up_kernel(x_ref, g_ref, sc_ref, sh_ref, wa_ref, wv_ref, cw_ref, cb_ref, o_ref, xn_ref, pad_ref):
    i = pl.program_id(0)

    @pl.when(pl.program_id(1) == 0)
    def _():
        _modulated_norm(x_ref, g_ref, sc_ref, sh_ref, xn_ref)
        zeros = jnp.zeros((V7X_SUBLANES, FFN_TN), f32)
        pad_ref[pl.ds(0, V7X_SUBLANES), :] = zeros
        pad_ref[pl.ds(V7X_SUBLANES + TM, V7X_SUBLANES), :] = zeros

    xn = xn_ref[...]
    a = _dot(xn, wa_ref[0].astype(bf16))
    pad_ref[pl.ds(V7X_SUBLANES, TM), :] = a
    seq_len = jnp.where(i < N_SAMPLE_TILES, DEC_SEQ, SEQ)
    pos = lax.broadcasted_iota(jnp.int32, (TM, 1), 0) & (seq_len - 1)
    prev = jnp.where(pos == 0, 0.0, pad_ref[pl.ds(V7X_SUBLANES - 1, TM), :])
    nxt = jnp.where(pos == seq_len - 1, 0.0, pad_ref[pl.ds(V7X_SUBLANES + 1, TM), :])
    cw = cw_ref[...]
    conv = cb_ref[...] + prev * cw[0:1, :] + a * cw[1:2, :] + nxt * cw[2:3, :]
    o_ref[...] = (_gelu_tanh(conv) * _dot(xn, wv_ref[0].astype(bf16))).astype(bf16)


def ffn_up(x, g, mod, w_up, layer, conv_w, conv_b):
    nj = D_FF // FFN_TN
    return pl.pallas_call(
        _ffn_up_kernel,
        grid=(N_TILES, nj),
        in_specs=[pl.BlockSpec((TM, D_MODEL), lambda i, j: (i, 0)),
                  pl.BlockSpec((1, D_MODEL), lambda i, j: (0, 0)),
                  _mod_spec(4), _mod_spec(3),
                  pl.BlockSpec((1, D_MODEL, FFN_TN), lambda i, j: (layer, 0, j)),
                  pl.BlockSpec((1, D_MODEL, FFN_TN), lambda i, j: (layer, 0, nj + j)),
                  pl.BlockSpec((FFN_CONV, FFN_TN), lambda i, j: (0, j)),
                  pl.BlockSpec((1, FFN_TN), lambda i, j: (0, j))],
        out_specs=pl.BlockSpec((TM, FFN_TN), lambda i, j: (i, j)),
        out_shape=jax.ShapeDtypeStruct((T_ALL, D_FF), bf16),
        scratch_shapes=[pltpu.VMEM((TM, D_MODEL), bf16),
                        pltpu.VMEM((TM + 2 * V7X_SUBLANES, FFN_TN), f32)],
        compiler_params=_params(2),
        name="ffn_up",
    )(x, g, mod, mod, w_up, w_up, conv_w, conv_b)


LRU_CB_LONG = D_RNN
LRU_CB_SHORT = D_RNN


def _lru_kernel(*refs, seq_len, has_init, cbw):
    if has_init:
        (x_ref, g_ref, cw_ref, cb_ref, w4_ref, b4_ref, lam_ref, h0f_ref, h0b_ref,
         y_ref, pad_ref, af_ref, uf_ref, ab_ref, ub_ref, hf_ref, hb_ref) = refs
    else:
        (x_ref, g_ref, cw_ref, cb_ref, w4_ref, b4_ref, lam_ref,
         y_ref, sf_ref, sb_ref, pad_ref, af_ref, uf_ref, ab_ref, ub_ref, hf_ref, hb_ref) = refs
    L = seq_len
    S = V7X_SUBLANES
    zeros = jnp.zeros((S, cbw), f32)
    pad_ref[pl.ds(0, S), :] = zeros
    pad_ref[pl.ds(S + L, S), :] = zeros
    pad_ref[pl.ds(S, L), :] = x_ref[...].astype(f32)
    cw = cw_ref[...]
    xc = cb_ref[...]
    for kk in range(LRU_CONV):
        xc = xc + pad_ref[pl.ds(S + kk - LRU_CONV // 2, L), :] * cw[kk:kk + 1, :]
    sp = _softplus(-lam_ref[...])
    for n in range(cbw // LRU_BS):
        cols = slice(n * LRU_BS, (n + 1) * LRU_BS)
        xcn = xc[:, cols]
        z = _dot(xcn.astype(bf16), w4_ref[n]) + b4_ref[n]
        for d, (a_ref, u_ref) in enumerate(((af_ref, uf_ref), (ab_ref, ub_ref))):
            r = _sigmoid(z[:, (2 * d) * LRU_BS:(2 * d + 1) * LRU_BS])
            ig = _sigmoid(z[:, (2 * d + 1) * LRU_BS:(2 * d + 2) * LRU_BS])
            log_a = (-LRU_C) * r * sp[d:d + 1, cols]
            a = jnp.exp(log_a)
            a_ref[:, cols] = a
            gain2 = -jnp.tanh(log_a) * (a * a + 1.0)
            gain = jnp.where(gain2 > 0.0, gain2 * lax.rsqrt(gain2), 0.0)
            u_ref[:, cols] = gain * (ig * xcn)

    if has_init:
        h0f = h0f_ref[0, 0]
        h0b = h0b_ref[0, 0]
    else:
        h0f = jnp.zeros((1, cbw), f32)
        h0b = jnp.zeros((1, cbw), f32)

    def step(i, carry):
        hf, hb = carry
        base_f = pl.multiple_of(i * S, S)
        base_b = pl.multiple_of(L - S - i * S, S)
        for r in range(S):
            tf = pl.ds(base_f + r, 1)
            tb = pl.ds(base_b + (S - 1 - r), 1)
            hf = af_ref[tf, :] * hf + uf_ref[tf, :]
            hb = ab_ref[tb, :] * hb + ub_ref[tb, :]
            hf_ref[tf, :] = hf
            hb_ref[tb, :] = hb
        return hf, hb

    hf, hb = lax.fori_loop(0, L // S, step, (h0f, h0b))
    if not has_init:
        sf_ref[0, 0] = hf
        sb_ref[0, 0] = hb
    y_ref[...] = ((hf_ref[...] + hb_ref[...]) * _gelu_tanh(g_ref[...].astype(f32))).astype(bf16)


def lru_branch(proj, bufs, row0, n_seq, seq_len, cw, cb, w4, b4, lam, init, layer):
    has_init = init is not None
    cbw = LRU_CB_LONG if seq_len == DEC_SEQ else LRU_CB_SHORT
    ncb = D_RNN // cbw
    nb = cbw // LRU_BS
    rb0 = row0 // seq_len
    in_specs = [pl.BlockSpec((seq_len, cbw), lambda s, c: (rb0 + s, OFF_LRU_X // cbw + c)),
                pl.BlockSpec((seq_len, cbw), lambda s, c: (rb0 + s, OFF_LRU_G // cbw + c)),
                pl.BlockSpec((LRU_CONV, cbw), lambda s, c: (0, c)),
                pl.BlockSpec((1, cbw), lambda s, c: (0, c)),
                pl.BlockSpec((nb, LRU_BS, 4 * LRU_BS), lambda s, c: (c, 0, 0)),
                pl.BlockSpec((nb, 1, 4 * LRU_BS), lambda s, c: (c, 0, 0)),
                pl.BlockSpec((2, cbw), lambda s, c: (0, c))]
    args = [proj, proj, cw, cb, w4, b4, lam]
    state_spec = pl.BlockSpec((1, 1, 1, cbw), lambda s, c: (s, layer, 0, c))
    y_spec = pl.BlockSpec((seq_len, cbw), lambda s, c: (rb0 + s, c))
    if has_init:
        in_specs += [state_spec, state_spec]
        args += [init[0], init[1]]
        out_specs, out_shape = y_spec, _BRANCH_BUF
    else:
        st = jax.ShapeDtypeStruct((n_seq, DEPTH, 1, D_RNN), f32)
        out_specs, out_shape = [y_spec, state_spec, state_spec], [_BRANCH_BUF, st, st]
    scratch = [pltpu.VMEM((seq_len + 2 * V7X_SUBLANES, cbw), f32)] + \
              [pltpu.VMEM((seq_len, cbw), f32) for _ in range(6)]
    return _inplace_call(functools.partial(_lru_kernel, seq_len=seq_len, has_init=has_init, cbw=cbw), bufs,
                         (n_seq, ncb), in_specs, args, out_specs, out_shape, scratch, "lru_branch")


def _ret_kernel(*refs, seq_len, has_init):
    if has_init:
        (la_ref, q_ref, k_ref, v_ref, g_ref, gn_ref, s0f_ref, s0b_ref, y_ref, acc_ref) = refs
    else:
        (la_ref, q_ref, k_ref, v_ref, g_ref, gn_ref, y_ref, sf_ref, sb_ref, acc_ref) = refs
    T = min(RET_CHUNK, seq_len)
    nc = seq_len // T
    tt = lax.broadcasted_iota(jnp.int32, (T, T), 0)
    ss = lax.broadcasted_iota(jnp.int32, (T, T), 1)
    diff = (tt - ss).astype(f32)
    tcol = lax.broadcasted_iota(jnp.int32, (T, 1), 0).astype(f32)
    scale = RET_DK ** -0.5
    for h in range(RET_HEADS):
        la_f = la_ref[h, 0]
        la_b = la_ref[h, 1]
        kcols = pl.ds(h * RET_DK, RET_DK)
        vcols = pl.ds(h * RET_DV, RET_DV)
        dsum = (jnp.where(tt >= ss, jnp.exp(la_f * diff), 0.0)
                + jnp.where(ss >= tt, jnp.exp(-la_b * diff), 0.0))

        def chunk(c):
            rows = pl.ds(c * T, T)
            q = q_ref[rows, kcols].astype(f32)
            ks = k_ref[rows, kcols].astype(f32) * scale
            v = v_ref[rows, vcols].astype(f32)
            return rows, q, ks, v

        s_f = s0f_ref[0, 0, h] if has_init else None
        for c in range(nc):
            rows, q, ks, v = chunk(c)
            scores = _dot_nt(q.astype(bf16), ks.astype(bf16)) * dsum
            y = _dot(scores.astype(bf16), v.astype(bf16))
            if s_f is not None:
                y = y + _dot((q * jnp.exp(la_f * (tcol + 1.0))).astype(bf16), s_f.astype(bf16))
            acc_ref[rows, :] = y
            if c < nc - 1 or not has_init:
                upd = _dot(ks.T.astype(bf16), (v * jnp.exp(la_f * (T - 1.0 - tcol))).astype(bf16))
                s_f = upd if s_f is None else jnp.exp(la_f * T) * s_f + upd
        s_b = s0b_ref[0, 0, h] if has_init else None
        for c in reversed(range(nc)):
            rows, q, ks, v = chunk(c)
            if s_b is not None:
                acc_ref[rows, :] += _dot((q * jnp.exp(la_b * (T - tcol))).astype(bf16), s_b.astype(bf16))
            if c > 0 or not has_init:
                upd = _dot(ks.T.astype(bf16), (v * jnp.exp(la_b * tcol)).astype(bf16))
                s_b = upd if s_b is None else jnp.exp(la_b * T) * s_b + upd
        if not has_init:
            sf_ref[0, 0, h] = s_f
            sb_ref[0, 0, h] = s_b
        y = acc_ref[...]
        mu = jnp.mean(y, axis=-1, keepdims=True)
        yc = y - mu
        var = jnp.mean(yc * yc, axis=-1, keepdims=True)
        y = yc * lax.rsqrt(var + EPS) * gn_ref[:, vcols]
        y_ref[:, vcols] = (y * _silu(g_ref[:, vcols].astype(f32))).astype(bf16)


def retention_branch(proj, bufs, row0, n_seq, seq_len, la, gn, init, layer):
    has_init = init is not None
    rb0 = row0 // seq_len
    qk_w, vg_w = RET_HEADS * RET_DK, RET_HEADS * RET_DV
    in_specs = [pl.BlockSpec(memory_space=pltpu.SMEM),
                pl.BlockSpec((seq_len, qk_w), lambda s: (rb0 + s, OFF_RET_Q // qk_w)),
                pl.BlockSpec((seq_len, qk_w), lambda s: (rb0 + s, OFF_RET_K // qk_w)),
                pl.BlockSpec((seq_len, vg_w), lambda s: (rb0 + s, OFF_RET_V // vg_w)),
                pl.BlockSpec((seq_len, vg_w), lambda s: (rb0 + s, OFF_RET_G // vg_w)),
                pl.BlockSpec((1, vg_w), lambda s: (0, 0))]
    args = [la, proj, proj, proj, proj, gn]
    state_spec = pl.BlockSpec((1, 1, RET_HEADS, RET_DK, RET_DV), lambda s: (s, layer, 0, 0, 0))
    y_spec = pl.BlockSpec((seq_len, vg_w), lambda s: (rb0 + s, 0))
    if has_init:
        in_specs += [state_spec, state_spec]
        args += [init[0], init[1]]
        out_specs, out_shape = y_spec, _BRANCH_BUF
    else:
        st = jax.ShapeDtypeStruct((n_seq, DEPTH, RET_HEADS, RET_DK, RET_DV), f32)
        out_specs, out_shape = [y_spec, state_spec, state_spec], [_BRANCH_BUF, st, st]
    return _inplace_call(functools.partial(_ret_kernel, seq_len=seq_len, has_init=has_init), bufs,
                         (n_seq,), in_specs, args, out_specs, out_shape,
                         [pltpu.VMEM((seq_len, RET_DV), f32)], "retention_branch")


SSD_GW = SSD_HPG * SSD_P


def _split3(x):
    h1 = x.astype(bf16)
    r1 = x - h1.astype(f32)
    h2 = r1.astype(bf16)
    h3 = (r1 - h2.astype(f32)).astype(bf16)
    return h1, h2, h3


def _dot_exact_rhs(m, x):
    h1, h2, h3 = _split3(x)
    return _dot(m, h1) + _dot(m, h2) + _dot(m, h3)


def _dot_exact_lhs(x, m):
    h1, h2, h3 = _split3(x)
    return _dot(h1, m) + _dot(h2, m) + _dot(h3, m)


def _conv4_silu(src_ref, pad_ref, cw_ref, cb_ref, L, width):
    S = V7X_SUBLANES
    zeros = jnp.zeros((S, width), f32)
    pad_ref[pl.ds(0, S), pl.ds(0, width)] = zeros
    pad_ref[pl.ds(S + L, S), pl.ds(0, width)] = zeros
    pad_ref[pl.ds(S, L), pl.ds(0, width)] = src_ref[...].astype(f32)
    cw = cw_ref[...]
    out = cb_ref[...]
    for kk in range(SSD_CONV):
        out = out + pad_ref[pl.ds(S + kk - SSD_CONV // 2, L), pl.ds(0, width)] * cw[kk:kk + 1, :]
    return _silu(out)


def _heads_to_lanes(s_ref, st_ref):
    for hh in range(SSD_HPG):
        st_ref[:, pl.ds(hh * SSD_P, SSD_P)] = s_ref[0, 0, hh]
    return st_ref[...]


def _ssd_kernel(*refs, seq_len, has_init):
    if has_init:
        (z_ref, x_ref, b_ref, c_ref, dt_ref, cwx_ref, cbx_ref, cwb_ref, cbb_ref, cwc_ref, cbc_ref,
         prm_ref, dvec_ref, ng_ref, s0f_ref, s0b_ref,
         y_ref, pad_ref, xs_ref, bs_ref, cs_ref, acc_ref, rb_ref, yn_ref, st_ref) = refs
    else:
        (z_ref, x_ref, b_ref, c_ref, dt_ref, cwx_ref, cbx_ref, cwb_ref, cbb_ref, cwc_ref, cbc_ref,
         prm_ref, dvec_ref, ng_ref,
         y_ref, sf_ref, sb_ref, pad_ref, xs_ref, bs_ref, cs_ref, acc_ref, rb_ref, yn_ref, st_ref) = refs
    g = pl.program_id(1)
    L = seq_len
    T = min(SSD_CHUNK, L)
    nc = L // T
    H = SSD_HPG
    xs_ref[...] = _conv4_silu(x_ref, pad_ref, cwx_ref, cbx_ref, L, SSD_GW)
    bs_ref[...] = _conv4_silu(b_ref, pad_ref, cwb_ref, cbb_ref, L, SSD_N)
    cs_ref[...] = _conv4_silu(c_ref, pad_ref, cwc_ref, cbc_ref, L, SSD_N)
    prm = prm_ref[0]
    a_neg = -jnp.exp(prm[1:2, :])
    tt = lax.broadcasted_iota(jnp.int32, (T, T), 0)
    ss = lax.broadcasted_iota(jnp.int32, (T, T), 1)
    lower = tt >= ss
    upper = ss >= tt
    tri_l = jnp.where(lower, 1.0, 0.0).astype(bf16)
    tri_u = jnp.where(upper, 1.0, 0.0).astype(bf16)
    lower_b = lower[:SSD_TB, :SSD_TB]
    upper_b = upper[:SSD_TB, :SSD_TB]
    er = lax.broadcasted_iota(jnp.int32, (V7X_LANES, SSD_GW), 0)
    ec = lax.broadcasted_iota(jnp.int32, (V7X_LANES, SSD_GW), 1) // SSD_P
    exp_f = jnp.where(er == ec, 1.0, 0.0).astype(bf16)
    exp_b = jnp.where(er == ec + H, 1.0, 0.0).astype(bf16)

    def expand(w, e):
        hi = w.astype(bf16)
        lo = (w - hi.astype(f32)).astype(bf16)
        return _dot(hi, e) + _dot(lo, e)

    def chunk_terms(c):
        rows = pl.ds(c * T, T)
        dt = _softplus(dt_ref[rows, :].astype(f32) + prm[0:1, :])
        da = dt * a_neg
        return rows, dt, da

    s_f = _heads_to_lanes(s0f_ref, st_ref) if has_init else None
    for c in range(nc):
        rows, dt, da = chunk_terms(c)
        cum = _dot_exact_rhs(tri_l, da)
        rsum = _dot_exact_rhs(tri_u, da)
        rb_ref[rows, :] = rsum
        da_t = da.T
        cum_t = _dot_exact_lhs(da_t, tri_u)
        rsum_t = _dot_exact_lhs(da_t, tri_l)
        dt_t = dt.T
        bmat = bs_ref[rows, :]
        cmat = cs_ref[rows, :]
        xmat = xs_ref[rows, :]
        gmat = _dot_nt(cmat.astype(bf16), bmat.astype(bf16))
        for hh in range(H):
            cf, cf_t, dtf_t = cum[:, hh:hh + 1], cum_t[hh:hh + 1, :], dt_t[hh:hh + 1, :]
            rb, rb_t, dtb_t = rsum[:, H + hh:H + hh + 1], rsum_t[H + hh:H + hh + 1, :], dt_t[H + hh:H + hh + 1, :]
            block_rows = []
            for bi in range(T // SSD_TB):
                rr = slice(bi * SSD_TB, (bi + 1) * SSD_TB)
                blocks = []
                for bj in range(T // SSD_TB):
                    cc = slice(bj * SSD_TB, (bj + 1) * SSD_TB)
                    if bi > bj:
                        blk = jnp.exp(cf[rr] - cf_t[:, cc]) * dtf_t[:, cc]
                    elif bi < bj:
                        blk = jnp.exp(rb[rr] - rb_t[:, cc]) * dtb_t[:, cc]
                    else:
                        blk = (jnp.where(lower_b, jnp.exp(cf[rr] - cf_t[:, cc]), 0.0) * dtf_t[:, cc]
                               + jnp.where(upper_b, jnp.exp(rb[rr] - rb_t[:, cc]), 0.0) * dtb_t[:, cc])
                    blocks.append(gmat[rr, cc] * blk)
                block_rows.append(jnp.concatenate(blocks, axis=1))
            m = jnp.concatenate(block_rows, axis=0)
            xh = xmat[:, hh * SSD_P:(hh + 1) * SSD_P]
            acc_ref[rows, pl.ds(hh * SSD_P, SSD_P)] = _dot(m.astype(bf16), xh.astype(bf16))
        ecum = jnp.exp(cum)
        if s_f is not None:
            acc_ref[rows, :] += _dot(cmat.astype(bf16), s_f.astype(bf16)) * expand(ecum, exp_f)
        if c < nc - 1 or not has_init:
            tail = jnp.exp(cum[T - 1:T, :] - cum) * dt
            xw = (xmat * expand(tail, exp_f)).astype(bf16)
            upd = _dot(bmat.T.astype(bf16), xw)
            if s_f is None:
                s_f = upd
            else:
                s_f = s_f * expand(jnp.broadcast_to(ecum[T - 1:T, :], (V7X_SUBLANES, V7X_LANES)),
                                   exp_f)[0:1, :] + upd
    s_b = _heads_to_lanes(s0b_ref, st_ref) if has_init else None
    for c in reversed(range(nc)):
        rows, dt, da = chunk_terms(c)
        rsum = rb_ref[rows, :]
        ers = jnp.exp(rsum)
        bmat = bs_ref[rows, :]
        cmat = cs_ref[rows, :]
        xmat = xs_ref[rows, :]
        if s_b is not None:
            acc_ref[rows, :] += _dot(cmat.astype(bf16), s_b.astype(bf16)) * expand(ers, exp_b)
        if c > 0 or not has_init:
            tail = jnp.exp(rsum[0:1, :] - rsum) * dt
            xw = (xmat * expand(tail, exp_b)).astype(bf16)
            upd = _dot(bmat.T.astype(bf16), xw)
            if s_b is None:
                s_b = upd
            else:
                s_b = s_b * expand(jnp.broadcast_to(ers[0:1, :], (V7X_SUBLANES, V7X_LANES)),
                                   exp_b)[0:1, :] + upd
    if not has_init:
        for hh in range(H):
            sf_ref[0, 0, hh] = s_f[:, hh * SSD_P:(hh + 1) * SSD_P]
            sb_ref[0, 0, hh] = s_b[:, hh * SSD_P:(hh + 1) * SSD_P]
    yg = (acc_ref[...] + xs_ref[...] * dvec_ref[...]) * _silu(z_ref[...].astype(f32))
    yn_ref[g] = yg

    @pl.when(g == SSD_GROUPS - 1)
    def _():
        ssq = None
        for gg in range(SSD_GROUPS):
            y = yn_ref[gg]
            s = jnp.sum(y * y, axis=-1, keepdims=True)
            ssq = s if ssq is None else ssq + s
        inv = lax.rsqrt(ssq * (1.0 / D_SSD) + EPS)
        for gg in range(SSD_GROUPS):
            cols = pl.ds(gg * SSD_GW, SSD_GW)
            y_ref[:, cols] = (yn_ref[gg] * inv * ng_ref[:, cols]).astype(bf16)


def ssd_branch(proj, bufs, row0, n_seq, seq_len, conv_w, conv_b, prm, dvec, ng, init, layer):
    has_init = init is not None
    rb0 = row0 // seq_len
    xoff = OFF_SSD_XBC
    boff = OFF_SSD_XBC + D_SSD
    coff = boff + SSD_GROUPS * SSD_N
    in_specs = [pl.BlockSpec((seq_len, SSD_GW), lambda s, g: (rb0 + s, OFF_SSD_Z // SSD_GW + g)),
                pl.BlockSpec((seq_len, SSD_GW), lambda s, g: (rb0 + s, xoff // SSD_GW + g)),
                pl.BlockSpec((seq_len, SSD_N), lambda s, g: (rb0 + s, boff // SSD_N + g)),
                pl.BlockSpec((seq_len, SSD_N), lambda s, g: (rb0 + s, coff // SSD_N + g)),
                pl.BlockSpec((seq_len, V7X_LANES), lambda s, g: (rb0 + s, OFF_SSD_DT // V7X_LANES + g)),
                pl.BlockSpec((SSD_CONV, SSD_GW), lambda s, g: (0, g)),
                pl.BlockSpec((1, SSD_GW), lambda s, g: (0, g)),
                pl.BlockSpec((SSD_CONV, SSD_N), lambda s, g: (0, D_SSD // SSD_N + g)),
                pl.BlockSpec((1, SSD_N), lambda s, g: (0, D_SSD // SSD_N + g)),
                pl.BlockSpec((SSD_CONV, SSD_N), lambda s, g: (0, D_SSD // SSD_N + SSD_GROUPS + g)),
                pl.BlockSpec((1, SSD_N), lambda s, g: (0, D_SSD // SSD_N + SSD_GROUPS + g)),
                pl.BlockSpec((1, V7X_SUBLANES, V7X_LANES), lambda s, g: (g, 0, 0)),
                pl.BlockSpec((1, SSD_GW), lambda s, g: (0, g)),
                pl.BlockSpec((1, D_SSD), lambda s, g: (0, 0))]
    args = [proj, proj, proj, proj, proj, conv_w, conv_b, conv_w, conv_b, conv_w, conv_b, prm, dvec, ng]
    state_spec = pl.BlockSpec((1, 1, SSD_HPG, SSD_N, SSD_P), lambda s, g: (s, layer, g, 0, 0))
    y_spec = pl.BlockSpec((seq_len, D_SSD), lambda s, g: (rb0 + s, 0))
    if has_init:
        in_specs += [state_spec, state_spec]
        args += [init[0], init[1]]
        out_specs, out_shape = y_spec, _BRANCH_BUF
    else:
        st = jax.ShapeDtypeStruct((n_seq, DEPTH, SSD_HEADS, SSD_N, SSD_P), f32)
        out_specs, out_shape = [y_spec, state_spec, state_spec], [_BRANCH_BUF, st, st]
    scratch = [pltpu.VMEM((seq_len + 2 * V7X_SUBLANES, SSD_GW), f32),
               pltpu.VMEM((seq_len, SSD_GW), f32),
               pltpu.VMEM((seq_len, SSD_N), f32),
               pltpu.VMEM((seq_len, SSD_N), f32),
               pltpu.VMEM((seq_len, SSD_GW), f32),
               pltpu.VMEM((seq_len, V7X_LANES), f32),
               pltpu.VMEM((SSD_GROUPS, seq_len, SSD_GW), f32),
               pltpu.VMEM((SSD_N, SSD_GW), f32)]
    return _inplace_call(functools.partial(_ssd_kernel, seq_len=seq_len, has_init=has_init), bufs,
                         (n_seq, SSD_GROUPS), in_specs, args, out_specs, out_shape, scratch, "ssd_branch")


def _head_rmsnorm(x, g):
    return x * lax.rsqrt(jnp.mean(x * x, axis=-1, keepdims=True) + EPS) * g


CTX_HB = 4
CTX_W = CTX_HB * NA_HD


def _ctx_attn_kernel(q_ref, k_ref, v_ref, qg_ref, kg_ref, y_ref, ko_ref, vo_ref):
    scale = NA_HD ** -0.5
    vo_ref[0, 0] = v_ref[...].astype(f32)
    for h in range(CTX_HB):
        cols = pl.ds(h * NA_HD, NA_HD)
        q = _head_rmsnorm(q_ref[:, cols].astype(f32), qg_ref[...])
        k = _head_rmsnorm(k_ref[:, cols].astype(f32), kg_ref[...])
        ko_ref[0, 0, :, cols] = k
        s = _dot_nt(q.astype(bf16), k.astype(bf16)) * scale
        p = jnp.exp(s - jnp.max(s, axis=-1, keepdims=True))
        o = _dot(p.astype(bf16), v_ref[:, cols]) / jnp.sum(p, axis=-1, keepdims=True)
        y_ref[:, cols] = o.astype(bf16)


def context_attention(proj, bufs, row0, qg, kg, layer):
    rb0 = row0 // SEQ
    spec = lambda off: pl.BlockSpec((SEQ, CTX_W), lambda s, hb: (rb0 + s, off // CTX_W + hb))
    gspec = pl.BlockSpec((1, NA_HD), lambda s, hb: (0, 0))
    kv_spec = pl.BlockSpec((1, 1, SEQ, CTX_W), lambda s, hb: (s, layer, 0, hb))
    kv_shape = jax.ShapeDtypeStruct((BATCH, DEPTH, SEQ, NA_W), f32)
    return _inplace_call(
        _ctx_attn_kernel, bufs, (BATCH, NA_HEADS // CTX_HB),
        [spec(OFF_NA_Q), spec(OFF_NA_K), spec(OFF_NA_V), gspec, gspec], [proj, proj, proj, qg, kg],
        [pl.BlockSpec((SEQ, CTX_W), lambda s, hb: (rb0 + s, hb)), kv_spec, kv_spec],
        [_BRANCH_BUF, kv_shape, kv_shape], [], "context_attention")


NA_ROWS = DEC_SEQ // GRID_W
NA_NK = NA_WR * GRID_W


def _na_row_start(r):
    return min(max(r - NA_WR // 2, 0), NA_ROWS - NA_WR)


def _rope(x, cos, sin_signed):
    lane = lax.broadcasted_iota(jnp.int32, x.shape, 1)
    quarter = NA_HD // 4
    swapped = jnp.where((lane & (2 * quarter - 1)) < quarter,
                        pltpu.roll(x, NA_HD - quarter, axis=1), pltpu.roll(x, quarter, axis=1))
    return x * cos + swapped * sin_signed


def _na_row_groups():
    groups, r = [], 0
    while r < NA_ROWS:
        n = 1
        while r + n < NA_ROWS and _na_row_start(r + n) == _na_row_start(r):
            n += 1
        groups.append((r, n, _na_row_start(r)))
        r += n
    return groups


def _na_attn_kernel(q_ref, k_ref, v_ref, kc_ref, vc_ref, qg_ref, kg_ref, cos_ref, sin_ref,
                    bias_ref, valid_ref, y_ref, qs_ref, ks_ref, s_ref, p_ref, oc_ref, w_ref):
    scale = NA_HD ** -0.5
    cos = cos_ref[...]
    sin = sin_ref[...]
    qs_ref[...] = _rope(_head_rmsnorm(q_ref[...].astype(f32), qg_ref[...]), cos, sin).astype(bf16)
    ks_ref[...] = _rope(_head_rmsnorm(k_ref[...].astype(f32), kg_ref[...]), cos, sin).astype(bf16)
    groups = _na_row_groups()
    valid = valid_ref[...] > 0.0
    for r0, n, rs in groups:
        rows = pl.ds(r0 * GRID_W, n * GRID_W)
        d0 = r0 - rs
        kw = ks_ref[pl.ds(rs * GRID_W, NA_NK), :]
        s = (_dot_nt(qs_ref[rows, :], kw) * scale).reshape(n, GRID_W, NA_NK) + bias_ref[0, d0:d0 + n]
        s_ref[rows, :] = jnp.where(valid[None], s, -1e30).reshape(n * GRID_W, NA_NK)
    s_ctx = _dot_nt(qs_ref[...], kc_ref[0, 0].astype(bf16)) * scale
    s_loc = s_ref[...]
    m = jnp.maximum(jnp.max(s_loc, axis=-1, keepdims=True), jnp.max(s_ctx, axis=-1, keepdims=True))
    p_loc = jnp.exp(s_loc - m)
    p_ctx = jnp.exp(s_ctx - m)
    w_ref[...] = 1.0 / (jnp.sum(p_loc, axis=-1, keepdims=True) + jnp.sum(p_ctx, axis=-1, keepdims=True))
    p_ref[...] = p_loc.astype(bf16)
    oc_ref[...] = _dot(p_ctx.astype(bf16), vc_ref[0, 0].astype(bf16))
    for r0, n, rs in groups:
        rows = pl.ds(r0 * GRID_W, n * GRID_W)
        vw = v_ref[pl.ds(rs * GRID_W, NA_NK), :]
        y_ref[rows, :] = ((_dot(p_ref[rows, :], vw) + oc_ref[rows, :]) * w_ref[rows, :]).astype(bf16)


def neighbourhood_attention(proj, cache_k, cache_v, layer, qg, kg, cos, sin, bias, valid):
    spec = lambda off: pl.BlockSpec((DEC_SEQ, NA_HD), lambda b, h: (b, off // NA_HD + h))
    cspec = pl.BlockSpec((1, 1, PAST_LEN, NA_HD), lambda b, h: (b, layer, 0, h))
    gspec = pl.BlockSpec((1, NA_HD), lambda b, h: (0, 0))
    tspec = pl.BlockSpec((DEC_SEQ, NA_HD), lambda b, h: (0, 0))
    return pl.pallas_call(
        _na_attn_kernel,
        grid=(DEC_BATCH, NA_HEADS),
        in_specs=[spec(OFF_NA_Q), spec(OFF_NA_K), spec(OFF_NA_V), cspec, cspec, gspec, gspec,
                  tspec, tspec,
                  pl.BlockSpec((1, NA_WR, GRID_W, NA_NK), lambda b, h: (h, 0, 0, 0)),
                  pl.BlockSpec((GRID_W, NA_NK), lambda b, h: (0, 0))],
        out_specs=pl.BlockSpec((DEC_SEQ, NA_HD), lambda b, h: (b, h)),
        out_shape=_BRANCH_BUF,
        scratch_shapes=[pltpu.VMEM((DEC_SEQ, NA_HD), bf16), pltpu.VMEM((DEC_SEQ, NA_HD), bf16),
                        pltpu.VMEM((DEC_SEQ, NA_NK), f32), pltpu.VMEM((DEC_SEQ, NA_NK), bf16),
                        pltpu.VMEM((DEC_SEQ, NA_HD), f32), pltpu.VMEM((DEC_SEQ, 1), f32)],
        compiler_params=_params(2), name="neighbourhood_attention",
    )(proj, proj, proj, cache_k, cache_v, qg, kg, cos, sin, bias, valid)


def _cast_kernel(w_ref, o_ref):
    o_ref[...] = w_ref[...].astype(bf16)


def _cast_head(w_in):
    spec = pl.BlockSpec((1, D_MODEL, PROJ_TN), lambda l, j: (l, 0, j))
    return pl.pallas_call(
        _cast_kernel, grid=(DEPTH, N_HEAD_TILES), in_specs=[spec], out_specs=spec,
        out_shape=jax.ShapeDtypeStruct((DEPTH, D_MODEL, HEAD_COLS), bf16),
        compiler_params=_params(2), name="cast_w_in_head",
    )(w_in)


def _split_w_in(w_in):
    dt0, na0 = IN_OFFSETS[7] - HEAD_COLS, IN_OFFSETS[8] - HEAD_COLS
    head = _cast_head(w_in)
    rest = lax.optimization_barrier(w_in[..., HEAD_COLS:].astype(bf16))
    pieces = [rest[..., :dt0], rest[..., na0:]]
    dt = rest[..., dt0:na0]
    for gidx in range(SSD_GROUPS):
        pieces.append(dt[..., gidx * SSD_HPG:(gidx + 1) * SSD_HPG])
        pieces.append(dt[..., SSD_HEADS + gidx * SSD_HPG:SSD_HEADS + (gidx + 1) * SSD_HPG])
        pieces.append(jnp.zeros((DEPTH, D_MODEL, V7X_LANES - 2 * SSD_HPG), bf16))
    used = OFF_SSD_DT + SSD_GROUPS * V7X_LANES
    pieces.append(jnp.zeros((DEPTH, D_MODEL, D_IN_PAD - used), bf16))
    return head, jnp.concatenate(pieces, axis=-1)


def _group_lanes(v):
    rows = []
    for gidx in range(SSD_GROUPS):
        sl = slice(gidx * SSD_HPG, (gidx + 1) * SSD_HPG)
        rows.append(jnp.concatenate([v[0, sl], v[1, sl], jnp.zeros((V7X_LANES - 2 * SSD_HPG,), f32)]))
    return jnp.stack(rows)


def _rope_tables():
    t = np.arange(DEC_SEQ)
    quarter = NA_HD // 4
    inv = ROPE_BASE ** (-np.arange(quarter, dtype=np.float32) / quarter)
    ang_r = (t // GRID_W).astype(np.float32)[:, None] * inv
    ang_c = (t % GRID_W).astype(np.float32)[:, None] * inv
    cos = np.concatenate([np.cos(ang_r), np.cos(ang_r), np.cos(ang_c), np.cos(ang_c)], axis=1)
    sin = np.concatenate([-np.sin(ang_r), np.sin(ang_r), -np.sin(ang_c), np.sin(ang_c)], axis=1)
    return jnp.asarray(cos, f32), jnp.asarray(sin, f32)


def _na_tables(rpb):
    cq = np.arange(GRID_W)
    kc = np.tile(np.arange(GRID_W), NA_WR)
    col_start = np.clip(cq - NA_WC // 2, 0, GRID_W - NA_WC)
    valid = (kc[None, :] >= col_start[:, None]) & (kc[None, :] < col_start[:, None] + NA_WC)
    col_off = np.clip(cq[None, :] - cq[:, None], 1 - NA_WC, NA_WC - 1) + NA_WC - 1
    onehot = (col_off[None, :, :] == np.arange(2 * NA_WC - 1)[:, None, None]).astype(np.float32)
    toep = jnp.einsum('hic,cqk->hiqk', rpb.astype(f32), jnp.asarray(onehot), precision=lax.Precision.HIGHEST)
    tables = []
    for d in range(NA_WR):
        rows = toep[:, NA_WR - 1 - d:2 * NA_WR - 1 - d]
        tables.append(rows.transpose(0, 2, 1, 3).reshape(NA_HEADS, GRID_W, NA_NK))
    return jnp.stack(tables, axis=1), jnp.asarray(valid, f32)


def kernel(x_prompt, x_sample, cache_na_k, cache_na_v, state_lru_f, state_lru_b,
           state_ret_f, state_ret_b, state_ssd_f, state_ssd_b, c, c_ctx,
           norm1_g, norm2_g, w_ada, b_ada, w_in, w_gate, b_gate, w_branch, w_out,
           lru_conv_w, lru_conv_b, lru_wa, lru_ba, lru_wx, lru_bx, lru_lambda,
           ret_gn_g, ssd_conv_w, ssd_conv_b, ssd_a_log, ssd_dt_bias, ssd_d, ssd_norm_g,
           na_q_g, na_k_g, na_rpb, ffn_w_up, ffn_conv_w, ffn_conv_b, ffn_w_down):
    xs0 = x_sample.reshape(T_SAMPLE, D_MODEL)
    xp0 = x_prompt.reshape(T_PROMPT, D_MODEL)
    n_ptiles = N_TILES - N_SAMPLE_TILES
    cond = jnp.concatenate([c, c_ctx[None, :], jnp.zeros((N_COND_PAD - N_COND, D_MODEL), f32)], axis=0)
    mod_all = ada_modulation(cond, w_ada, b_ada)
    cos, sin = _rope_tables()
    hh = jnp.arange(RET_HEADS, dtype=f32)
    ret_la = jnp.stack([jnp.log1p(-jnp.exp2(-5.0 - hh)), jnp.log1p(-jnp.exp2(-5.5 - hh))], axis=1)
    cache_k = cache_na_k.reshape(DEC_BATCH, DEPTH, PAST_LEN, NA_W)
    cache_v = cache_na_v.reshape(DEC_BATCH, DEPTH, PAST_LEN, NA_W)
    lru_init = (state_lru_f.reshape(DEC_BATCH, DEPTH, 1, D_RNN), state_lru_b.reshape(DEC_BATCH, DEPTH, 1, D_RNN))

    w_head_b, w_tail_b = _split_w_in(w_in)
    w_branch_b, w_out_b, w_down_b = w_branch.astype(bf16), w_out.astype(bf16), ffn_w_down.astype(bf16)

    new = {k: None for k in ("k", "v", "lru_f", "lru_b", "ret_f", "ret_b", "ssd_f", "ssd_b")}
    for l in range(DEPTH):
        mod = mod_all[l].reshape(N_COND_PAD, 1, N_MOD * D_MODEL)
        g1 = norm1_g[l][None, :]
        if l == 0:
            proj, xn = in_projection(xs0, 0, 0, N_SAMPLE_TILES, g1, mod, w_head_b, w_tail_b, l)
            proj, xn = in_projection(xp0, 0, N_SAMPLE_TILES, n_ptiles, g1, mod, w_head_b, w_tail_b, l, (proj, xn))
        else:
            proj, xn = in_projection(x, 0, 0, N_TILES, g1, mod, w_head_b, w_tail_b, l)

        w4 = jnp.concatenate([lru_wa[l, 0], lru_wx[l, 0], lru_wa[l, 1], lru_wx[l, 1]], axis=-1).astype(bf16)
        b4 = jnp.concatenate([lru_ba[l, 0].reshape(LRU_BLOCKS, 1, LRU_BS), lru_bx[l, 0].reshape(LRU_BLOCKS, 1, LRU_BS),
                              lru_ba[l, 1].reshape(LRU_BLOCKS, 1, LRU_BS), lru_bx[l, 1].reshape(LRU_BLOCKS, 1, LRU_BS)],
                             axis=-1)
        lru_args = (lru_conv_w[l], lru_conv_b[l][None, :], w4, b4, lru_lambda[l])
        y_lru = lru_branch(proj, (None,), 0, DEC_BATCH, DEC_SEQ, *lru_args, lru_init, l)
        y_lru, new["lru_f"], new["lru_b"] = lru_branch(
            proj, (y_lru, new["lru_f"], new["lru_b"]), T_SAMPLE, BATCH, SEQ, *lru_args, None, l)

        gn = ret_gn_g[l][None, :]
        y_ret = retention_branch(proj, (None,), 0, DEC_BATCH, DEC_SEQ, ret_la, gn, (state_ret_f, state_ret_b), l)
        y_ret, new["ret_f"], new["ret_b"] = retention_branch(
            proj, (y_ret, new["ret_f"], new["ret_b"]), T_SAMPLE, BATCH, SEQ, ret_la, gn, None, l)

        prm = jnp.stack([_group_lanes(ssd_dt_bias[l]), _group_lanes(ssd_a_log[l])], axis=1)
        prm = jnp.concatenate([prm, jnp.zeros((SSD_GROUPS, V7X_SUBLANES - 2, V7X_LANES), f32)], axis=1)
        dvec = jnp.repeat(ssd_d[l], SSD_P)[None, :]
        ssd_args = (ssd_conv_w[l], ssd_conv_b[l][None, :], prm, dvec, ssd_norm_g[l][None, :])
        y_ssd = ssd_branch(proj, (None,), 0, DEC_BATCH, DEC_SEQ, *ssd_args, (state_ssd_f, state_ssd_b), l)
        y_ssd, new["ssd_f"], new["ssd_b"] = ssd_branch(
            proj, (y_ssd, new["ssd_f"], new["ssd_b"]), T_SAMPLE, BATCH, SEQ, *ssd_args, None, l)

        qg = na_q_g[l][None, :]
        kg = na_k_g[l][None, :]
        bias, valid = _na_tables(na_rpb[l])
        y_na = neighbourhood_attention(proj, cache_k, cache_v, l, qg, kg, cos, sin, bias, valid)
        y_na, new["k"], new["v"] = context_attention(proj, (y_na, new["k"], new["v"]), T_SAMPLE, qg, kg, l)

        merged = merge_branches(xn, (y_lru, y_ret, y_ssd, y_na), w_gate, b_gate[l][None, :], w_branch_b, l)
        if l == 0:
            x = residual_projection(merged, w_out_b, l, xs0, mod, 2, 0, N_SAMPLE_TILES)
            x = residual_projection(merged, w_out_b, l, xp0, mod, 2, N_SAMPLE_TILES, n_ptiles, x_tile0=0, buf=x)
        else:
            x = residual_projection(merged, w_out_b, l, x, mod, 2)
        hmid = ffn_up(x, norm2_g[l][None, :], mod, ffn_w_up, l, ffn_conv_w[l], ffn_conv_b[l][None, :])
        if l < DEPTH - 1:
            x = residual_projection(hmid, w_down_b, l, x, mod, 5)
        else:
            y_sample = residual_projection(hmid, w_down_b, l, x, mod, 5, 0, N_SAMPLE_TILES, out_rows=T_SAMPLE)
            y_prompt = residual_projection(hmid, w_down_b, l, x, mod, 5, N_SAMPLE_TILES, n_ptiles,
                                           out_tile0=0, out_rows=T_PROMPT)

    kv_shape = (BATCH, DEPTH, SEQ, NA_HEADS, NA_HD)
    return (y_prompt.reshape(BATCH, SEQ, D_MODEL), y_sample.reshape(DEC_BATCH, DEC_SEQ, D_MODEL),
            new["k"].reshape(kv_shape), new["v"].reshape(kv_shape),
            new["lru_f"].reshape(BATCH, DEPTH, D_RNN), new["lru_b"].reshape(BATCH, DEPTH, D_RNN),
            new["ret_f"], new["ret_b"], new["ssd_f"], new["ssd_b"])
```

```python
import functools
import math

import jax
import jax.numpy as jnp
import numpy as np
from jax import lax
from jax.experimental import pallas as pl
from jax.experimental.pallas import tpu as pltpu

D_MODEL = 2048
BATCH = 16
SEQ = 256
DEPTH = 2
DEC_BATCH = 8
DEC_SEQ = 1024
PAST_LEN = 256
GRID_W = 64
EPS = 1e-6
N_BRANCH = 4
BRANCH_W = 1024
N_MOD = 6
D_RNN = 1024
LRU_BLOCKS = 8
LRU_BS = D_RNN // LRU_BLOCKS
LRU_CONV = 4
LRU_C = 8.0
RET_HEADS = 4
RET_DK = 128
RET_DV = 256
SSD_HEADS = 16
SSD_P = 64
SSD_N = 128
SSD_GROUPS = 2
SSD_CONV = 4
D_SSD = SSD_HEADS * SSD_P
SSD_CONV_CH = D_SSD + 2 * SSD_GROUPS * SSD_N
NA_HEADS = 8
NA_HD = 128
NA_W = NA_HEADS * NA_HD
NA_WR = 8
NA_WC = 16
ROPE_BASE = 10000.0
D_FF = 5632
FFN_CONV = 3
IN_SIZES = (D_RNN, D_RNN,
            RET_HEADS * RET_DK, RET_HEADS * RET_DK, RET_HEADS * RET_DV, RET_HEADS * RET_DV,
            D_SSD, SSD_CONV_CH, 2 * SSD_HEADS,
            NA_W, NA_W, NA_W)
D_IN = sum(IN_SIZES)
IN_OFFSETS = tuple(int(s) for s in np.cumsum(IN_SIZES)[:-1])

V7X_LANES = 128
V7X_SUBLANES = 8
V7X_VMEM_LIMIT_BYTES = 56 * 1024 * 1024

TM = 1024
T_SAMPLE = DEC_BATCH * DEC_SEQ
T_PROMPT = BATCH * SEQ
T_ALL = T_SAMPLE + T_PROMPT
N_SAMPLE_TILES = T_SAMPLE // TM
N_TILES = T_ALL // TM
N_COND = DEC_BATCH + 1
N_COND_PAD = 16

SSD_HPG = SSD_HEADS // SSD_GROUPS
OFF_LRU_X = 0
OFF_LRU_G = OFF_LRU_X + D_RNN
OFF_RET_Q = OFF_LRU_G + D_RNN
OFF_RET_K = OFF_RET_Q + RET_HEADS * RET_DK
OFF_RET_V = OFF_RET_K + RET_HEADS * RET_DK
OFF_RET_G = OFF_RET_V + RET_HEADS * RET_DV
OFF_SSD_Z = OFF_RET_G + RET_HEADS * RET_DV
OFF_SSD_XBC = OFF_SSD_Z + D_SSD
OFF_NA_Q = OFF_SSD_XBC + SSD_CONV_CH
OFF_NA_K = OFF_NA_Q + NA_W
OFF_NA_V = OFF_NA_K + NA_W
OFF_SSD_DT = OFF_NA_V + NA_W
PROJ_TN = 1024
D_IN_PAD = -(-(OFF_SSD_DT + SSD_GROUPS * V7X_LANES) // PROJ_TN) * PROJ_TN
N_HEAD_TILES = OFF_NA_Q // PROJ_TN
HEAD_COLS = N_HEAD_TILES * PROJ_TN

RET_CHUNK = 256
SSD_CHUNK = 256
SSD_TB = 128

f32 = jnp.float32
bf16 = jnp.bfloat16

_ARB = "arbitrary"


def _params(n_axes):
    return pltpu.CompilerParams(dimension_semantics=(_ARB,) * n_axes,
                                vmem_limit_bytes=V7X_VMEM_LIMIT_BYTES)


def _mod_spec(k, tile0=0):
    return pl.BlockSpec((1, 1, D_MODEL), lambda i, j: (jnp.minimum(tile0 + i, N_SAMPLE_TILES), 0, k))


def _dot(a, b):
    return jnp.dot(a, b, preferred_element_type=f32)


def _dot_nt(a, b):
    return lax.dot_general(a, b, (((1,), (1,)), ((), ())), preferred_element_type=f32)


def _sigmoid(x):
    return 0.5 * jnp.tanh(0.5 * x) + 0.5


def _silu(x):
    return x * _sigmoid(x)


def _gelu_tanh(x):
    return 0.5 * x * (1.0 + jnp.tanh(math.sqrt(2.0 / math.pi) * (x + 0.044715 * (x * x * x))))


def _softplus(x):
    return jnp.maximum(x, 0.0) + jnp.log1p(jnp.exp(-jnp.abs(x)))


def _drop_refs(body, n, *refs):
    body(*refs[n:])


def _inplace_call(body, bufs, grid, in_specs, args, out_specs, out_shape, scratch, name):
    held = [(k, b) for k, b in enumerate(bufs) if b is not None]
    if held:
        body = functools.partial(_drop_refs, body, len(held))
        in_specs = [pl.BlockSpec(memory_space=pl.ANY)] * len(held) + list(in_specs)
        args = [b for _, b in held] + list(args)
    aliases = {pos: k for pos, (k, _) in enumerate(held)}
    return pl.pallas_call(
        body, grid=grid, in_specs=in_specs, out_specs=out_specs, out_shape=out_shape,
        scratch_shapes=scratch, input_output_aliases=aliases,
        compiler_params=_params(len(grid)), name=name,
    )(*args)


_BRANCH_BUF = jax.ShapeDtypeStruct((T_ALL, BRANCH_W), bf16)


ADA_TN = 1024


def _ada_kernel(c_ref, w_ref, b_ref, o_ref):
    c = _silu(c_ref[...]).astype(bf16)
    o_ref[0] = _dot(c, w_ref[0].astype(bf16)) + b_ref[0]


def ada_modulation(cond, w_ada, b_ada):
    n = N_MOD * D_MODEL
    return pl.pallas_call(
        _ada_kernel,
        grid=(DEPTH, n // ADA_TN),
        in_specs=[pl.BlockSpec((N_COND_PAD, D_MODEL), lambda l, j: (0, 0)),
                  pl.BlockSpec((1, D_MODEL, ADA_TN), lambda l, j: (l, 0, j)),
                  pl.BlockSpec((1, 1, ADA_TN), lambda l, j: (l, 0, j))],
        out_specs=pl.BlockSpec((1, N_COND_PAD, ADA_TN), lambda l, j: (l, 0, j)),
        out_shape=jax.ShapeDtypeStruct((DEPTH, N_COND_PAD, n), f32),
        compiler_params=_params(2),
        name="ada_modulation",
    )(cond, w_ada, b_ada.reshape(DEPTH, 1, n))


NORM_ROWS = 128


def _modulated_norm(x_ref, g_ref, sc_ref, sh_ref, xn_ref):
    g = g_ref[...]
    sc = 1.0 + sc_ref[0]
    sh = sh_ref[0]

    def body(r, carry):
        rows = pl.ds(pl.multiple_of(r * NORM_ROWS, NORM_ROWS), NORM_ROWS)
        x = x_ref[rows, :]
        y = x * lax.rsqrt(jnp.mean(x * x, axis=-1, keepdims=True) + EPS)
        xn_ref[rows, :] = ((y * g) * sc + sh).astype(bf16)
        return carry

    lax.fori_loop(0, TM // NORM_ROWS, body, 0)


def _in_proj_kernel(x_ref, g_ref, sc_ref, sh_ref, wh_ref, wt_ref, o_ref, xn_ref):
    j = pl.program_id(1)

    @pl.when(j == 0)
    def _():
        _modulated_norm(x_ref, g_ref, sc_ref, sh_ref, xn_ref)

    @pl.when(j < N_HEAD_TILES)
    def _():
        o_ref[...] = _dot(xn_ref[...], wh_ref[0]).astype(bf16)

    @pl.when(j >= N_HEAD_TILES)
    def _():
        o_ref[...] = _dot(xn_ref[...], wt_ref[0]).astype(bf16)


def in_projection(x, x_tile0, tile0, n_tiles, g, mod, w_head, w_tail, layer, bufs=(None, None)):
    n = D_IN_PAD
    n_tail = w_tail.shape[2] // PROJ_TN
    return _inplace_call(
        _in_proj_kernel, bufs, (n_tiles, n // PROJ_TN),
        [pl.BlockSpec((TM, D_MODEL), lambda i, j: (x_tile0 + i, 0)),
         pl.BlockSpec((1, D_MODEL), lambda i, j: (0, 0)),
         _mod_spec(1, tile0), _mod_spec(0, tile0),
         pl.BlockSpec((1, D_MODEL, PROJ_TN), lambda i, j: (layer, 0, jnp.minimum(j, N_HEAD_TILES - 1))),
         pl.BlockSpec((1, D_MODEL, PROJ_TN),
                      lambda i, j: (layer, 0, jnp.clip(j - N_HEAD_TILES, 0, n_tail - 1)))],
        [x, g, mod, mod, w_head, w_tail],
        [pl.BlockSpec((TM, PROJ_TN), lambda i, j: (tile0 + i, j)),
         pl.BlockSpec((TM, D_MODEL), lambda i, j: (tile0 + i, 0))],
        [jax.ShapeDtypeStruct((T_ALL, n), bf16), jax.ShapeDtypeStruct((T_ALL, D_MODEL), bf16)],
        [], "in_projection")


MERGE_TN = 256


def _merge_kernel(xn_ref, *refs):
    y_refs, wg_refs, bg_refs = refs[0:4], refs[4:8], refs[8:12]
    wb_ref, o_ref = refs[12], refs[13]
    xn = xn_ref[...]
    acc = None
    for n in range(N_BRANCH):
        gate = _sigmoid(_dot(xn, wg_refs[n][0].astype(bf16)) + bg_refs[n][...])
        term = gate * _dot(y_refs[n][...], wb_ref[0, n])
        acc = term if acc is None else acc + term
    o_ref[...] = acc.astype(bf16)


def merge_branches(xn, ys, w_gate, b_gate, w_branch, layer):
    nj = D_MODEL // MERGE_TN
    y_spec = pl.BlockSpec((TM, BRANCH_W), lambda i, j: (i, 0))
    wg_specs = [pl.BlockSpec((1, D_MODEL, MERGE_TN), lambda i, j, n=n: (layer, 0, n * nj + j))
                for n in range(N_BRANCH)]
    bg_specs = [pl.BlockSpec((1, MERGE_TN), lambda i, j, n=n: (0, n * nj + j)) for n in range(N_BRANCH)]
    return pl.pallas_call(
        _merge_kernel,
        grid=(N_TILES, nj),
        in_specs=[pl.BlockSpec((TM, D_MODEL), lambda i, j: (i, 0)),
                  y_spec, y_spec, y_spec, y_spec, *wg_specs, *bg_specs,
                  pl.BlockSpec((1, N_BRANCH, BRANCH_W, MERGE_TN), lambda i, j: (layer, 0, 0, j))],
        out_specs=pl.BlockSpec((TM, MERGE_TN), lambda i, j: (i, j)),
        out_shape=jax.ShapeDtypeStruct((T_ALL, D_MODEL), bf16),
        compiler_params=_params(2),
        name="merge_branches",
    )(xn, *ys, *([w_gate] * N_BRANCH), *([b_gate] * N_BRANCH), w_branch)


RES_TN = 512
RES_TN_SHORT_K = 1024


def _residual_kernel(a_ref, w_ref, x_ref, gv_ref, o_ref):
    o_ref[...] = x_ref[...] + gv_ref[0] * _dot(a_ref[...], w_ref[0])


def residual_projection(a, w, layer, x, mod, k_mod, tile0=0, n_tiles=N_TILES, x_tile0=None, out_tile0=None,
                        out_rows=T_ALL, buf=None):
    kdim = a.shape[1]
    tn = RES_TN_SHORT_K if kdim == D_MODEL else RES_TN
    x_tile0 = tile0 if x_tile0 is None else x_tile0
    out_tile0 = tile0 if out_tile0 is None else out_tile0
    return _inplace_call(
        _residual_kernel, (buf,), (n_tiles, D_MODEL // tn),
        [pl.BlockSpec((TM, kdim), lambda i, j: (tile0 + i, 0)),
         pl.BlockSpec((1, kdim, tn), lambda i, j: (layer, 0, j)),
         pl.BlockSpec((TM, tn), lambda i, j: (x_tile0 + i, j)),
         pl.BlockSpec((1, 1, tn),
                      lambda i, j: (jnp.minimum(tile0 + i, N_SAMPLE_TILES), 0, k_mod * (D_MODEL // tn) + j))],
        [a, w, x, mod],
        pl.BlockSpec((TM, tn), lambda i, j: (out_tile0 + i, j)),
        jax.ShapeDtypeStruct((out_rows, D_MODEL), f32), [], "residual_projection")


FFN_TN = 512


def _ffn_up_kernel(x_ref, g_ref, sc_ref, sh_ref, wa_ref, wv_ref, cw_ref, cb_ref, o_ref, xn_ref, pad_ref):
    i = pl.program_id(0)

    @pl.when(pl.program_id(1) == 0)
    def _():
        _modulated_norm(x_ref, g_ref, sc_ref, sh_ref, xn_ref)
        zeros = jnp.zeros((V7X_SUBLANES, FFN_TN), f32)
        pad_ref[pl.ds(0, V7X_SUBLANES), :] = zeros
        pad_ref[pl.ds(V7X_SUBLANES + TM, V7X_SUBLANES), :] = zeros

    xn = xn_ref[...]
    a = _dot(xn, wa_ref[0].astype(bf16))
    pad_ref[pl.ds(V7X_SUBLANES, TM), :] = a
    seq_len = jnp.where(i < N_SAMPLE_TILES, DEC_SEQ, SEQ)
    pos = lax.broadcasted_iota(jnp.int32, (TM, 1), 0) & (seq_len - 1)
    prev = jnp.where(pos == 0, 0.0, pad_ref[pl.ds(V7X_SUBLANES - 1, TM), :])
    nxt = jnp.where(pos == seq_len - 1, 0.0, pad_ref[pl.ds(V7X_SUBLANES + 1, TM), :])
    cw = cw_ref[...]
    conv = cb_ref[...] + prev * cw[0:1, :] + a * cw[1:2, :] + nxt * cw[2:3, :]
    o_ref[...] = (_gelu_tanh(conv) * _dot(xn, wv_ref[0].astype(bf16))).astype(bf16)


def ffn_up(x, g, mod, w_up, layer, conv_w, conv_b):
    nj = D_FF // FFN_TN
    return pl.pallas_call(
        _ffn_up_kernel,
        grid=(N_TILES, nj),
        in_specs=[pl.BlockSpec((TM, D_MODEL), lambda i, j: (i, 0)),
                  pl.BlockSpec((1, D_MODEL), lambda i, j: (0, 0)),
                  _mod_spec(4), _mod_spec(3),
                  pl.BlockSpec((1, D_MODEL, FFN_TN), lambda i, j: (layer, 0, j)),
                  pl.BlockSpec((1, D_MODEL, FFN_TN), lambda i, j: (layer, 0, nj + j)),
                  pl.BlockSpec((FFN_CONV, FFN_TN), lambda i, j: (0, j)),
                  pl.BlockSpec((1, FFN_TN), lambda i, j: (0, j))],
        out_specs=pl.BlockSpec((TM, FFN_TN), lambda i, j: (i, j)),
        out_shape=jax.ShapeDtypeStruct((T_ALL, D_FF), bf16),
        scratch_shapes=[pltpu.VMEM((TM, D_MODEL), bf16),
                        pltpu.VMEM((TM + 2 * V7X_SUBLANES, FFN_TN), f32)],
        compiler_params=_params(2),
        name="ffn_up",
    )(x, g, mod, mod, w_up, w_up, conv_w, conv_b)


LRU_CB_LONG = D_RNN
LRU_CB_SHORT = D_RNN


def _lru_kernel(*refs, seq_len, has_init, cbw):
    if has_init:
        (x_ref, g_ref, cw_ref, cb_ref, w4_ref, b4_ref, lam_ref, h0f_ref, h0b_ref,
         y_ref, pad_ref, af_ref, uf_ref, ab_ref, ub_ref, hf_ref, hb_ref) = refs
    else:
        (x_ref, g_ref, cw_ref, cb_ref, w4_ref, b4_ref, lam_ref,
         y_ref, sf_ref, sb_ref, pad_ref, af_ref, uf_ref, ab_ref, ub_ref, hf_ref, hb_ref) = refs
    L = seq_len
    S = V7X_SUBLANES
    zeros = jnp.zeros((S, cbw), f32)
    pad_ref[pl.ds(0, S), :] = zeros
    pad_ref[pl.ds(S + L, S), :] = zeros
    pad_ref[pl.ds(S, L), :] = x_ref[...].astype(f32)
    cw = cw_ref[...]
    xc = cb_ref[...]
    for kk in range(LRU_CONV):
        xc = xc + pad_ref[pl.ds(S + kk - LRU_CONV // 2, L), :] * cw[kk:kk + 1, :]
    sp = _softplus(-lam_ref[...])
    for n in range(cbw // LRU_BS):
        cols = slice(n * LRU_BS, (n + 1) * LRU_BS)
        xcn = xc[:, cols]
        z = _dot(xcn.astype(bf16), w4_ref[n]) + b4_ref[n]
        for d, (a_ref, u_ref) in enumerate(((af_ref, uf_ref), (ab_ref, ub_ref))):
            r = _sigmoid(z[:, (2 * d) * LRU_BS:(2 * d + 1) * LRU_BS])
            ig = _sigmoid(z[:, (2 * d + 1) * LRU_BS:(2 * d + 2) * LRU_BS])
            log_a = (-LRU_C) * r * sp[d:d + 1, cols]
            a = jnp.exp(log_a)
            a_ref[:, cols] = a
            gain2 = -jnp.tanh(log_a) * (a * a + 1.0)
            gain = jnp.where(gain2 > 0.0, gain2 * lax.rsqrt(gain2), 0.0)
            u_ref[:, cols] = gain * (ig * xcn)

    if has_init:
        h0f = h0f_ref[0, 0]
        h0b = h0b_ref[0, 0]
    else:
        h0f = jnp.zeros((1, cbw), f32)
        h0b = jnp.zeros((1, cbw), f32)

    def step(i, carry):
        hf, hb = carry
        base_f = pl.multiple_of(i * S, S)
        base_b = pl.multiple_of(L - S - i * S, S)
        for r in range(S):
            tf = pl.ds(base_f + r, 1)
            tb = pl.ds(base_b + (S - 1 - r), 1)
            hf = af_ref[tf, :] * hf + uf_ref[tf, :]
            hb = ab_ref[tb, :] * hb + ub_ref[tb, :]
            hf_ref[tf, :] = hf
            hb_ref[tb, :] = hb
        return hf, hb

    hf, hb = lax.fori_loop(0, L // S, step, (h0f, h0b))
    if not has_init:
        sf_ref[0, 0] = hf
        sb_ref[0, 0] = hb
    y_ref[...] = ((hf_ref[...] + hb_ref[...]) * _gelu_tanh(g_ref[...].astype(f32))).astype(bf16)


def lru_branch(proj, bufs, row0, n_seq, seq_len, cw, cb, w4, b4, lam, init, layer):
    has_init = init is not None
    cbw = LRU_CB_LONG if seq_len == DEC_SEQ else LRU_CB_SHORT
    ncb = D_RNN // cbw
    nb = cbw // LRU_BS
    rb0 = row0 // seq_len
    in_specs = [pl.BlockSpec((seq_len, cbw), lambda s, c: (rb0 + s, OFF_LRU_X // cbw + c)),
                pl.BlockSpec((seq_len, cbw), lambda s, c: (rb0 + s, OFF_LRU_G // cbw + c)),
                pl.BlockSpec((LRU_CONV, cbw), lambda s, c: (0, c)),
                pl.BlockSpec((1, cbw), lambda s, c: (0, c)),
                pl.BlockSpec((nb, LRU_BS, 4 * LRU_BS), lambda s, c: (c, 0, 0)),
                pl.BlockSpec((nb, 1, 4 * LRU_BS), lambda s, c: (c, 0, 0)),
                pl.BlockSpec((2, cbw), lambda s, c: (0, c))]
    args = [proj, proj, cw, cb, w4, b4, lam]
    state_spec = pl.BlockSpec((1, 1, 1, cbw), lambda s, c: (s, layer, 0, c))
    y_spec = pl.BlockSpec((seq_len, cbw), lambda s, c: (rb0 + s, c))
    if has_init:
        in_specs += [state_spec, state_spec]
        args += [init[0], init[1]]
        out_specs, out_shape = y_spec, _BRANCH_BUF
    else:
        st = jax.ShapeDtypeStruct((n_seq, DEPTH, 1, D_RNN), f32)
        out_specs, out_shape = [y_spec, state_spec, state_spec], [_BRANCH_BUF, st, st]
    scratch = [pltpu.VMEM((seq_len + 2 * V7X_SUBLANES, cbw), f32)] + \
              [pltpu.VMEM((seq_len, cbw), f32) for _ in range(6)]
    return _inplace_call(functools.partial(_lru_kernel, seq_len=seq_len, has_init=has_init, cbw=cbw), bufs,
                         (n_seq, ncb), in_specs, args, out_specs, out_shape, scratch, "lru_branch")


def _ret_kernel(*refs, seq_len, has_init):
    if has_init:
        (la_ref, q_ref, k_ref, v_ref, g_ref, gn_ref, s0f_ref, s0b_ref, y_ref, acc_ref) = refs
    else:
        (la_ref, q_ref, k_ref, v_ref, g_ref, gn_ref, y_ref, sf_ref, sb_ref, acc_ref) = refs
    T = min(RET_CHUNK, seq_len)
    nc = seq_len // T
    tt = lax.broadcasted_iota(jnp.int32, (T, T), 0)
    ss = lax.broadcasted_iota(jnp.int32, (T, T), 1)
    diff = (tt - ss).astype(f32)
    tcol = lax.broadcasted_iota(jnp.int32, (T, 1), 0).astype(f32)
    scale = RET_DK ** -0.5
    for h in range(RET_HEADS):
        la_f = la_ref[h, 0]
        la_b = la_ref[h, 1]
        kcols = pl.ds(h * RET_DK, RET_DK)
        vcols = pl.ds(h * RET_DV, RET_DV)
        dsum = (jnp.where(tt >= ss, jnp.exp(la_f * diff), 0.0)
                + jnp.where(ss >= tt, jnp.exp(-la_b * diff), 0.0))

        def chunk(c):
            rows = pl.ds(c * T, T)
            q = q_ref[rows, kcols].astype(f32)
            ks = k_ref[rows, kcols].astype(f32) * scale
            v = v_ref[rows, vcols].astype(f32)
            return rows, q, ks, v

        s_f = s0f_ref[0, 0, h] if has_init else None
        for c in range(nc):
            rows, q, ks, v = chunk(c)
            scores = _dot_nt(q.astype(bf16), ks.astype(bf16)) * dsum
            y = _dot(scores.astype(bf16), v.astype(bf16))
            if s_f is not None:
                y = y + _dot((q * jnp.exp(la_f * (tcol + 1.0))).astype(bf16), s_f.astype(bf16))
            acc_ref[rows, :] = y
            if c < nc - 1 or not has_init:
                upd = _dot(ks.T.astype(bf16), (v * jnp.exp(la_f * (T - 1.0 - tcol))).astype(bf16))
                s_f = upd if s_f is None else jnp.exp(la_f * T) * s_f + upd
        s_b = s0b_ref[0, 0, h] if has_init else None
        for c in reversed(range(nc)):
            rows, q, ks, v = chunk(c)
            if s_b is not None:
                acc_ref[rows, :] += _dot((q * jnp.exp(la_b * (T - tcol))).astype(bf16), s_b.astype(bf16))
            if c > 0 or not has_init:
                upd = _dot(ks.T.astype(bf16), (v * jnp.exp(la_b * tcol)).astype(bf16))
                s_b = upd if s_b is None else jnp.exp(la_b * T) * s_b + upd
        if not has_init:
            sf_ref[0, 0, h] = s_f
            sb_ref[0, 0, h] = s_b
        y = acc_ref[...]
        mu = jnp.mean(y, axis=-1, keepdims=True)
        yc = y - mu
        var = jnp.mean(yc * yc, axis=-1, keepdims=True)
        y = yc * lax.rsqrt(var + EPS) * gn_ref[:, vcols]
        y_ref[:, vcols] = (y * _silu(g_ref[:, vcols].astype(f32))).astype(bf16)


def retention_branch(proj, bufs, row0, n_seq, seq_len, la, gn, init, layer):
    has_init = init is not None
    rb0 = row0 // seq_len
    qk_w, vg_w = RET_HEADS * RET_DK, RET_HEADS * RET_DV
    in_specs = [pl.BlockSpec(memory_space=pltpu.SMEM),
                pl.BlockSpec((seq_len, qk_w), lambda s: (rb0 + s, OFF_RET_Q // qk_w)),
                pl.BlockSpec((seq_len, qk_w), lambda s: (rb0 + s, OFF_RET_K // qk_w)),
                pl.BlockSpec((seq_len, vg_w), lambda s: (rb0 + s, OFF_RET_V // vg_w)),
                pl.BlockSpec((seq_len, vg_w), lambda s: (rb0 + s, OFF_RET_G // vg_w)),
                pl.BlockSpec((1, vg_w), lambda s: (0, 0))]
    args = [la, proj, proj, proj, proj, gn]
    state_spec = pl.BlockSpec((1, 1, RET_HEADS, RET_DK, RET_DV), lambda s: (s, layer, 0, 0, 0))
    y_spec = pl.BlockSpec((seq_len, vg_w), lambda s: (rb0 + s, 0))
    if has_init:
        in_specs += [state_spec, state_spec]
        args += [init[0], init[1]]
        out_specs, out_shape = y_spec, _BRANCH_BUF
    else:
        st = jax.ShapeDtypeStruct((n_seq, DEPTH, RET_HEADS, RET_DK, RET_DV), f32)
        out_specs, out_shape = [y_spec, state_spec, state_spec], [_BRANCH_BUF, st, st]
    return _inplace_call(functools.partial(_ret_kernel, seq_len=seq_len, has_init=has_init), bufs,
                         (n_seq,), in_specs, args, out_specs, out_shape,
                         [pltpu.VMEM((seq_len, RET_DV), f32)], "retention_branch")


SSD_GW = SSD_HPG * SSD_P


def _split3(x):
    h1 = x.astype(bf16)
    r1 = x - h1.astype(f32)
    h2 = r1.astype(bf16)
    h3 = (r1 - h2.astype(f32)).astype(bf16)
    return h1, h2, h3


def _dot_exact_rhs(m, x):
    h1, h2, h3 = _split3(x)
    return _dot(m, h1) + _dot(m, h2) + _dot(m, h3)


def _dot_exact_lhs(x, m):
    h1, h2, h3 = _split3(x)
    return _dot(h1, m) + _dot(h2, m) + _dot(h3, m)


def _conv4_silu(src_ref, pad_ref, cw_ref, cb_ref, L, width):
    S = V7X_SUBLANES
    zeros = jnp.zeros((S, width), f32)
    pad_ref[pl.ds(0, S), pl.ds(0, width)] = zeros
    pad_ref[pl.ds(S + L, S), pl.ds(0, width)] = zeros
    pad_ref[pl.ds(S, L), pl.ds(0, width)] = src_ref[...].astype(f32)
    cw = cw_ref[...]
    out = cb_ref[...]
    for kk in range(SSD_CONV):
        out = out + pad_ref[pl.ds(S + kk - SSD_CONV // 2, L), pl.ds(0, width)] * cw[kk:kk + 1, :]
    return _silu(out)


def _heads_to_lanes(s_ref, st_ref):
    for hh in range(SSD_HPG):
        st_ref[:, pl.ds(hh * SSD_P, SSD_P)] = s_ref[0, 0, hh]
    return st_ref[...]


def _ssd_kernel(*refs, seq_len, has_init):
    if has_init:
        (z_ref, x_ref, b_ref, c_ref, dt_ref, cwx_ref, cbx_ref, cwb_ref, cbb_ref, cwc_ref, cbc_ref,
         prm_ref, dvec_ref, ng_ref, s0f_ref, s0b_ref,
         y_ref, pad_ref, xs_ref, bs_ref, cs_ref, acc_ref, rb_ref, yn_ref, st_ref) = refs
    else:
        (z_ref, x_ref, b_ref, c_ref, dt_ref, cwx_ref, cbx_ref, cwb_ref, cbb_ref, cwc_ref, cbc_ref,
         prm_ref, dvec_ref, ng_ref,
         y_ref, sf_ref, sb_ref, pad_ref, xs_ref, bs_ref, cs_ref, acc_ref, rb_ref, yn_ref, st_ref) = refs
    g = pl.program_id(1)
    L = seq_len
    T = min(SSD_CHUNK, L)
    nc = L // T
    H = SSD_HPG
    xs_ref[...] = _conv4_silu(x_ref, pad_ref, cwx_ref, cbx_ref, L, SSD_GW)
    bs_ref[...] = _conv4_silu(b_ref, pad_ref, cwb_ref, cbb_ref, L, SSD_N)
    cs_ref[...] = _conv4_silu(c_ref, pad_ref, cwc_ref, cbc_ref, L, SSD_N)
    prm = prm_ref[0]
    a_neg = -jnp.exp(prm[1:2, :])
    tt = lax.broadcasted_iota(jnp.int32, (T, T), 0)
    ss = lax.broadcasted_iota(jnp.int32, (T, T), 1)
    lower = tt >= ss
    upper = ss >= tt
    tri_l = jnp.where(lower, 1.0, 0.0).astype(bf16)
    tri_u = jnp.where(upper, 1.0, 0.0).astype(bf16)
    lower_b = lower[:SSD_TB, :SSD_TB]
    upper_b = upper[:SSD_TB, :SSD_TB]
    er = lax.broadcasted_iota(jnp.int32, (V7X_LANES, SSD_GW), 0)
    ec = lax.broadcasted_iota(jnp.int32, (V7X_LANES, SSD_GW), 1) // SSD_P
    exp_f = jnp.where(er == ec, 1.0, 0.0).astype(bf16)
    exp_b = jnp.where(er == ec + H, 1.0, 0.0).astype(bf16)

    def expand(w, e):
        hi = w.astype(bf16)
        lo = (w - hi.astype(f32)).astype(bf16)
        return _dot(hi, e) + _dot(lo, e)

    def chunk_terms(c):
        rows = pl.ds(c * T, T)
        dt = _softplus(dt_ref[rows, :].astype(f32) + prm[0:1, :])
        da = dt * a_neg
        return rows, dt, da

    s_f = _heads_to_lanes(s0f_ref, st_ref) if has_init else None
    for c in range(nc):
        rows, dt, da = chunk_terms(c)
        cum = _dot_exact_rhs(tri_l, da)
        rsum = _dot_exact_rhs(tri_u, da)
        rb_ref[rows, :] = rsum
        da_t = da.T
        cum_t = _dot_exact_lhs(da_t, tri_u)
        rsum_t = _dot_exact_lhs(da_t, tri_l)
        dt_t = dt.T
        bmat = bs_ref[rows, :]
        cmat = cs_ref[rows, :]
        xmat = xs_ref[rows, :]
        gmat = _dot_nt(cmat.astype(bf16), bmat.astype(bf16))
        for hh in range(H):
            cf, cf_t, dtf_t = cum[:, hh:hh + 1], cum_t[hh:hh + 1, :], dt_t[hh:hh + 1, :]
            rb, rb_t, dtb_t = rsum[:, H + hh:H + hh + 1], rsum_t[H + hh:H + hh + 1, :], dt_t[H + hh:H + hh + 1, :]
            block_rows = []
            for bi in range(T // SSD_TB):
                rr = slice(bi * SSD_TB, (bi + 1) * SSD_TB)
                blocks = []
                for bj in range(T // SSD_TB):
                    cc = slice(bj * SSD_TB, (bj + 1) * SSD_TB)
                    if bi > bj:
                        blk = jnp.exp(cf[rr] - cf_t[:, cc]) * dtf_t[:, cc]
                    elif bi < bj:
                        blk = jnp.exp(rb[rr] - rb_t[:, cc]) * dtb_t[:, cc]
                    else:
                        blk = (jnp.where(lower_b, jnp.exp(cf[rr] - cf_t[:, cc]), 0.0) * dtf_t[:, cc]
                               + jnp.where(upper_b, jnp.exp(rb[rr] - rb_t[:, cc]), 0.0) * dtb_t[:, cc])
                    blocks.append(gmat[rr, cc] * blk)
                block_rows.append(jnp.concatenate(blocks, axis=1))
            m = jnp.concatenate(block_rows, axis=0)
            xh = xmat[:, hh * SSD_P:(hh + 1) * SSD_P]
            acc_ref[rows, pl.ds(hh * SSD_P, SSD_P)] = _dot(m.astype(bf16), xh.astype(bf16))
        ecum = jnp.exp(cum)
        if s_f is not None:
            acc_ref[rows, :] += _dot(cmat.astype(bf16), s_f.astype(bf16)) * expand(ecum, exp_f)
        if c < nc - 1 or not has_init:
            tail = jnp.exp(cum[T - 1:T, :] - cum) * dt
            xw = (xmat * expand(tail, exp_f)).astype(bf16)
            upd = _dot(bmat.T.astype(bf16), xw)
            if s_f is None:
                s_f = upd
            else:
                s_f = s_f * expand(jnp.broadcast_to(ecum[T - 1:T, :], (V7X_SUBLANES, V7X_LANES)),
                                   exp_f)[0:1, :] + upd
    s_b = _heads_to_lanes(s0b_ref, st_ref) if has_init else None
    for c in reversed(range(nc)):
        rows, dt, da = chunk_terms(c)
        rsum = rb_ref[rows, :]
        ers = jnp.exp(rsum)
        bmat = bs_ref[rows, :]
        cmat = cs_ref[rows, :]
        xmat = xs_ref[rows, :]
        if s_b is not None:
            acc_ref[rows, :] += _dot(cmat.astype(bf16), s_b.astype(bf16)) * expand(ers, exp_b)
        if c > 0 or not has_init:
            tail = jnp.exp(rsum[0:1, :] - rsum) * dt
            xw = (xmat * expand(tail, exp_b)).astype(bf16)
            upd = _dot(bmat.T.astype(bf16), xw)
            if s_b is None:
                s_b = upd
            else:
                s_b = s_b * expand(jnp.broadcast_to(ers[0:1, :], (V7X_SUBLANES, V7X_LANES)),
                                   exp_b)[0:1, :] + upd
    if not has_init:
        for hh in range(H):
            sf_ref[0, 0, hh] = s_f[:, hh * SSD_P:(hh + 1) * SSD_P]
            sb_ref[0, 0, hh] = s_b[:, hh * SSD_P:(hh + 1) * SSD_P]
    yg = (acc_ref[...] + xs_ref[...] * dvec_ref[...]) * _silu(z_ref[...].astype(f32))
    yn_ref[g] = yg

    @pl.when(g == SSD_GROUPS - 1)
    def _():
        ssq = None
        for gg in range(SSD_GROUPS):
            y = yn_ref[gg]
            s = jnp.sum(y * y, axis=-1, keepdims=True)
            ssq = s if ssq is None else ssq + s
        inv = lax.rsqrt(ssq * (1.0 / D_SSD) + EPS)
        for gg in range(SSD_GROUPS):
            cols = pl.ds(gg * SSD_GW, SSD_GW)
            y_ref[:, cols] = (yn_ref[gg] * inv * ng_ref[:, cols]).astype(bf16)


def ssd_branch(proj, bufs, row0, n_seq, seq_len, conv_w, conv_b, prm, dvec, ng, init, layer):
    has_init = init is not None
    rb0 = row0 // seq_len
    xoff = OFF_SSD_XBC
    boff = OFF_SSD_XBC + D_SSD
    coff = boff + SSD_GROUPS * SSD_N
    in_specs = [pl.BlockSpec((seq_len, SSD_GW), lambda s, g: (rb0 + s, OFF_SSD_Z // SSD_GW + g)),
                pl.BlockSpec((seq_len, SSD_GW), lambda s, g: (rb0 + s, xoff // SSD_GW + g)),
                pl.BlockSpec((seq_len, SSD_N), lambda s, g: (rb0 + s, boff // SSD_N + g)),
                pl.BlockSpec((seq_len, SSD_N), lambda s, g: (rb0 + s, coff // SSD_N + g)),
                pl.BlockSpec((seq_len, V7X_LANES), lambda s, g: (rb0 + s, OFF_SSD_DT // V7X_LANES + g)),
                pl.BlockSpec((SSD_CONV, SSD_GW), lambda s, g: (0, g)),
                pl.BlockSpec((1, SSD_GW), lambda s, g: (0, g)),
                pl.BlockSpec((SSD_CONV, SSD_N), lambda s, g: (0, D_SSD // SSD_N + g)),
                pl.BlockSpec((1, SSD_N), lambda s, g: (0, D_SSD // SSD_N + g)),
                pl.BlockSpec((SSD_CONV, SSD_N), lambda s, g: (0, D_SSD // SSD_N + SSD_GROUPS + g)),
                pl.BlockSpec((1, SSD_N), lambda s, g: (0, D_SSD // SSD_N + SSD_GROUPS + g)),
                pl.BlockSpec((1, V7X_SUBLANES, V7X_LANES), lambda s, g: (g, 0, 0)),
                pl.BlockSpec((1, SSD_GW), lambda s, g: (0, g)),
                pl.BlockSpec((1, D_SSD), lambda s, g: (0, 0))]
    args = [proj, proj, proj, proj, proj, conv_w, conv_b, conv_w, conv_b, conv_w, conv_b, prm, dvec, ng]
    state_spec = pl.BlockSpec((1, 1, SSD_HPG, SSD_N, SSD_P), lambda s, g: (s, layer, g, 0, 0))
    y_spec = pl.BlockSpec((seq_len, D_SSD), lambda s, g: (rb0 + s, 0))
    if has_init:
        in_specs += [state_spec, state_spec]
        args += [init[0], init[1]]
        out_specs, out_shape = y_spec, _BRANCH_BUF
    else:
        st = jax.ShapeDtypeStruct((n_seq, DEPTH, SSD_HEADS, SSD_N, SSD_P), f32)
        out_specs, out_shape = [y_spec, state_spec, state_spec], [_BRANCH_BUF, st, st]
    scratch = [pltpu.VMEM((seq_len + 2 * V7X_SUBLANES, SSD_GW), f32),
               pltpu.VMEM((seq_len, SSD_GW), f32),
               pltpu.VMEM((seq_len, SSD_N), f32),
               pltpu.VMEM((seq_len, SSD_N), f32),
               pltpu.VMEM((seq_len, SSD_GW), f32),
               pltpu.VMEM((seq_len, V7X_LANES), f32),
               pltpu.VMEM((SSD_GROUPS, seq_len, SSD_GW), f32),
               pltpu.VMEM((SSD_N, SSD_GW), f32)]
    return _inplace_call(functools.partial(_ssd_kernel, seq_len=seq_len, has_init=has_init), bufs,
                         (n_seq, SSD_GROUPS), in_specs, args, out_specs, out_shape, scratch, "ssd_branch")


def _head_rmsnorm(x, g):
    return x * lax.rsqrt(jnp.mean(x * x, axis=-1, keepdims=True) + EPS) * g


CTX_HB = 4
CTX_W = CTX_HB * NA_HD


def _ctx_attn_kernel(q_ref, k_ref, v_ref, qg_ref, kg_ref, y_ref, ko_ref, vo_ref):
    scale = NA_HD ** -0.5
    vo_ref[0, 0] = v_ref[...].astype(f32)
    for h in range(CTX_HB):
        cols = pl.ds(h * NA_HD, NA_HD)
        q = _head_rmsnorm(q_ref[:, cols].astype(f32), qg_ref[...])
        k = _head_rmsnorm(k_ref[:, cols].astype(f32), kg_ref[...])
        ko_ref[0, 0, :, cols] = k
        s = _dot_nt(q.astype(bf16), k.astype(bf16)) * scale
        p = jnp.exp(s - jnp.max(s, axis=-1, keepdims=True))
        o = _dot(p.astype(bf16), v_ref[:, cols]) / jnp.sum(p, axis=-1, keepdims=True)
        y_ref[:, cols] = o.astype(bf16)


def context_attention(proj, bufs, row0, qg, kg, layer):
    rb0 = row0 // SEQ
    spec = lambda off: pl.BlockSpec((SEQ, CTX_W), lambda s, hb: (rb0 + s, off // CTX_W + hb))
    gspec = pl.BlockSpec((1, NA_HD), lambda s, hb: (0, 0))
    kv_spec = pl.BlockSpec((1, 1, SEQ, CTX_W), lambda s, hb: (s, layer, 0, hb))
    kv_shape = jax.ShapeDtypeStruct((BATCH, DEPTH, SEQ, NA_W), f32)
    return _inplace_call(
        _ctx_attn_kernel, bufs, (BATCH, NA_HEADS // CTX_HB),
        [spec(OFF_NA_Q), spec(OFF_NA_K), spec(OFF_NA_V), gspec, gspec], [proj, proj, proj, qg, kg],
        [pl.BlockSpec((SEQ, CTX_W), lambda s, hb: (rb0 + s, hb)), kv_spec, kv_spec],
        [_BRANCH_BUF, kv_shape, kv_shape], [], "context_attention")


NA_ROWS = DEC_SEQ // GRID_W
NA_NK = NA_WR * GRID_W


def _na_row_start(r):
    return min(max(r - NA_WR // 2, 0), NA_ROWS - NA_WR)


def _rope(x, cos, sin_signed):
    lane = lax.broadcasted_iota(jnp.int32, x.shape, 1)
    quarter = NA_HD // 4
    swapped = jnp.where((lane & (2 * quarter - 1)) < quarter,
                        pltpu.roll(x, NA_HD - quarter, axis=1), pltpu.roll(x, quarter, axis=1))
    return x * cos + swapped * sin_signed


def _na_row_groups():
    groups, r = [], 0
    while r < NA_ROWS:
        n = 1
        while r + n < NA_ROWS and _na_row_start(r + n) == _na_row_start(r):
            n += 1
        groups.append((r, n, _na_row_start(r)))
        r += n
    return groups


def _na_attn_kernel(q_ref, k_ref, v_ref, kc_ref, vc_ref, qg_ref, kg_ref, cos_ref, sin_ref,
                    bias_ref, valid_ref, y_ref, qs_ref, ks_ref, s_ref, p_ref, oc_ref, w_ref):
    scale = NA_HD ** -0.5
    cos = cos_ref[...]
    sin = sin_ref[...]
    qs_ref[...] = _rope(_head_rmsnorm(q_ref[...].astype(f32), qg_ref[...]), cos, sin).astype(bf16)
    ks_ref[...] = _rope(_head_rmsnorm(k_ref[...].astype(f32), kg_ref[...]), cos, sin).astype(bf16)
    groups = _na_row_groups()
    valid = valid_ref[...] > 0.0
    for r0, n, rs in groups:
        rows = pl.ds(r0 * GRID_W, n * GRID_W)
        d0 = r0 - rs
        kw = ks_ref[pl.ds(rs * GRID_W, NA_NK), :]
        s = (_dot_nt(qs_ref[rows, :], kw) * scale).reshape(n, GRID_W, NA_NK) + bias_ref[0, d0:d0 + n]
        s_ref[rows, :] = jnp.where(valid[None], s, -1e30).reshape(n * GRID_W, NA_NK)
    s_ctx = _dot_nt(qs_ref[...], kc_ref[0, 0].astype(bf16)) * scale
    s_loc = s_ref[...]
    m = jnp.maximum(jnp.max(s_loc, axis=-1, keepdims=True), jnp.max(s_ctx, axis=-1, keepdims=True))
    p_loc = jnp.exp(s_loc - m)
    p_ctx = jnp.exp(s_ctx - m)
    w_ref[...] = 1.0 / (jnp.sum(p_loc, axis=-1, keepdims=True) + jnp.sum(p_ctx, axis=-1, keepdims=True))
    p_ref[...] = p_loc.astype(bf16)
    oc_ref[...] = _dot(p_ctx.astype(bf16), vc_ref[0, 0].astype(bf16))
    for r0, n, rs in groups:
        rows = pl.ds(r0 * GRID_W, n * GRID_W)
        vw = v_ref[pl.ds(rs * GRID_W, NA_NK), :]
        y_ref[rows, :] = ((_dot(p_ref[rows, :], vw) + oc_ref[rows, :]) * w_ref[rows, :]).astype(bf16)


def neighbourhood_attention(proj, cache_k, cache_v, layer, qg, kg, cos, sin, bias, valid):
    spec = lambda off: pl.BlockSpec((DEC_SEQ, NA_HD), lambda b, h: (b, off // NA_HD + h))
    cspec = pl.BlockSpec((1, 1, PAST_LEN, NA_HD), lambda b, h: (b, layer, 0, h))
    gspec = pl.BlockSpec((1, NA_HD), lambda b, h: (0, 0))
    tspec = pl.BlockSpec((DEC_SEQ, NA_HD), lambda b, h: (0, 0))
    return pl.pallas_call(
        _na_attn_kernel,
        grid=(DEC_BATCH, NA_HEADS),
        in_specs=[spec(OFF_NA_Q), spec(OFF_NA_K), spec(OFF_NA_V), cspec, cspec, gspec, gspec,
                  tspec, tspec,
                  pl.BlockSpec((1, NA_WR, GRID_W, NA_NK), lambda b, h: (h, 0, 0, 0)),
                  pl.BlockSpec((GRID_W, NA_NK), lambda b, h: (0, 0))],
        out_specs=pl.BlockSpec((DEC_SEQ, NA_HD), lambda b, h: (b, h)),
        out_shape=_BRANCH_BUF,
        scratch_shapes=[pltpu.VMEM((DEC_SEQ, NA_HD), bf16), pltpu.VMEM((DEC_SEQ, NA_HD), bf16),
                        pltpu.VMEM((DEC_SEQ, NA_NK), f32), pltpu.VMEM((DEC_SEQ, NA_NK), bf16),
                        pltpu.VMEM((DEC_SEQ, NA_HD), f32), pltpu.VMEM((DEC_SEQ, 1), f32)],
        compiler_params=_params(2), name="neighbourhood_attention",
    )(proj, proj, proj, cache_k, cache_v, qg, kg, cos, sin, bias, valid)


def _cast_kernel(w_ref, o_ref):
    o_ref[...] = w_ref[...].astype(bf16)


def _cast_head(w_in):
    spec = pl.BlockSpec((1, D_MODEL, PROJ_TN), lambda l, j: (l, 0, j))
    return pl.pallas_call(
        _cast_kernel, grid=(DEPTH, N_HEAD_TILES), in_specs=[spec], out_specs=spec,
        out_shape=jax.ShapeDtypeStruct((DEPTH, D_MODEL, HEAD_COLS), bf16),
        compiler_params=_params(2), name="cast_w_in_head",
    )(w_in)


TAIL_TN = 512
N_TAIL_TILES = (D_IN_PAD - HEAD_COLS) // TAIL_TN
N_QKV_TILES = 3 * NA_W // TAIL_TN
DT_W = 2 * SSD_HEADS


def _tail_kernel(a_ref, b_ref, o_ref):
    j = pl.program_id(1)
    lane = lax.broadcasted_iota(jnp.int32, (D_MODEL, TAIL_TN), 1)

    @pl.when(j == 0)
    def _():
        o_ref[0] = a_ref[0].astype(bf16)

    @pl.when((j >= 1) & (j <= N_QKV_TILES))
    def _():
        main = pltpu.roll(a_ref[0], TAIL_TN - DT_W, axis=1)
        nxt = pltpu.roll(b_ref[0], V7X_LANES - DT_W, axis=1)
        last_lane = lax.broadcasted_iota(jnp.int32, (D_MODEL, V7X_LANES), 1)
        last = jnp.where(last_lane < V7X_LANES - DT_W, main[:, TAIL_TN - V7X_LANES:], nxt)
        o_ref[0] = jnp.concatenate([main[:, :TAIL_TN - V7X_LANES], last], axis=1).astype(bf16)

    @pl.when(j == N_TAIL_TILES - 1)
    def _():
        a = a_ref[0]
        out = jnp.zeros((D_MODEL, TAIL_TN), f32)
        for gidx in range(SSD_GROUPS):
            base = gidx * V7X_LANES
            for src0, dst0 in ((gidx * SSD_HPG, base), (SSD_HEADS + gidx * SSD_HPG, base + SSD_HPG)):
                shift = (dst0 - src0) % TAIL_TN
                moved = a if shift == 0 else pltpu.roll(a, shift, axis=1)
                out = jnp.where((lane >= dst0) & (lane < dst0 + SSD_HPG), moved, out)
        o_ref[0] = out.astype(bf16)


def _split_w_in(w_in):
    dt0 = IN_OFFSETS[7]
    a_tile0, dt_tile, b_tile0 = HEAD_COLS // TAIL_TN, dt0 // TAIL_TN, dt0 // V7X_LANES
    b_per_a = TAIL_TN // V7X_LANES
    last_b = (D_IN - 1) // V7X_LANES
    tail = pl.pallas_call(
        _tail_kernel, grid=(DEPTH, N_TAIL_TILES),
        in_specs=[pl.BlockSpec((1, D_MODEL, TAIL_TN),
                               lambda l, j: (l, 0, jnp.where(j == N_TAIL_TILES - 1, dt_tile, a_tile0 + j))),
                  pl.BlockSpec((1, D_MODEL, V7X_LANES),
                               lambda l, j: (l, 0, jnp.minimum(b_tile0 + b_per_a * j, last_b)))],
        out_specs=pl.BlockSpec((1, D_MODEL, TAIL_TN), lambda l, j: (l, 0, j)),
        out_shape=jax.ShapeDtypeStruct((DEPTH, D_MODEL, D_IN_PAD - HEAD_COLS), bf16),
        compiler_params=_params(2), name="build_w_in_tail",
    )(w_in, w_in)
    return _cast_head(w_in), tail


def _group_lanes(v):
    rows = []
    for gidx in range(SSD_GROUPS):
        sl = slice(gidx * SSD_HPG, (gidx + 1) * SSD_HPG)
        rows.append(jnp.concatenate([v[0, sl], v[1, sl], jnp.zeros((V7X_LANES - 2 * SSD_HPG,), f32)]))
    return jnp.stack(rows)


def _rope_tables():
    t = np.arange(DEC_SEQ)
    quarter = NA_HD // 4
    inv = ROPE_BASE ** (-np.arange(quarter, dtype=np.float32) / quarter)
    ang_r = (t // GRID_W).astype(np.float32)[:, None] * inv
    ang_c = (t % GRID_W).astype(np.float32)[:, None] * inv
    cos = np.concatenate([np.cos(ang_r), np.cos(ang_r), np.cos(ang_c), np.cos(ang_c)], axis=1)
    sin = np.concatenate([-np.sin(ang_r), np.sin(ang_r), -np.sin(ang_c), np.sin(ang_c)], axis=1)
    return jnp.asarray(cos, f32), jnp.asarray(sin, f32)


def _na_tables(rpb):
    cq = np.arange(GRID_W)
    kc = np.tile(np.arange(GRID_W), NA_WR)
    col_start = np.clip(cq - NA_WC // 2, 0, GRID_W - NA_WC)
    valid = (kc[None, :] >= col_start[:, None]) & (kc[None, :] < col_start[:, None] + NA_WC)
    col_off = np.clip(cq[None, :] - cq[:, None], 1 - NA_WC, NA_WC - 1) + NA_WC - 1
    onehot = (col_off[None, :, :] == np.arange(2 * NA_WC - 1)[:, None, None]).astype(np.float32)
    toep = jnp.einsum('hic,cqk->hiqk', rpb.astype(f32), jnp.asarray(onehot), precision=lax.Precision.HIGHEST)
    tables = []
    for d in range(NA_WR):
        rows = toep[:, NA_WR - 1 - d:2 * NA_WR - 1 - d]
        tables.append(rows.transpose(0, 2, 1, 3).reshape(NA_HEADS, GRID_W, NA_NK))
    return jnp.stack(tables, axis=1), jnp.asarray(valid, f32)


def kernel(x_prompt, x_sample, cache_na_k, cache_na_v, state_lru_f, state_lru_b,
           state_ret_f, state_ret_b, state_ssd_f, state_ssd_b, c, c_ctx,
           norm1_g, norm2_g, w_ada, b_ada, w_in, w_gate, b_gate, w_branch, w_out,
           lru_conv_w, lru_conv_b, lru_wa, lru_ba, lru_wx, lru_bx, lru_lambda,
           ret_gn_g, ssd_conv_w, ssd_conv_b, ssd_a_log, ssd_dt_bias, ssd_d, ssd_norm_g,
           na_q_g, na_k_g, na_rpb, ffn_w_up, ffn_conv_w, ffn_conv_b, ffn_w_down):
    xs0 = x_sample.reshape(T_SAMPLE, D_MODEL)
    xp0 = x_prompt.reshape(T_PROMPT, D_MODEL)
    n_ptiles = N_TILES - N_SAMPLE_TILES
    cond = jnp.concatenate([c, c_ctx[None, :], jnp.zeros((N_COND_PAD - N_COND, D_MODEL), f32)], axis=0)
    mod_all = ada_modulation(cond, w_ada, b_ada)
    cos, sin = _rope_tables()
    hh = jnp.arange(RET_HEADS, dtype=f32)
    ret_la = jnp.stack([jnp.log1p(-jnp.exp2(-5.0 - hh)), jnp.log1p(-jnp.exp2(-5.5 - hh))], axis=1)
    cache_k = cache_na_k.reshape(DEC_BATCH, DEPTH, PAST_LEN, NA_W)
    cache_v = cache_na_v.reshape(DEC_BATCH, DEPTH, PAST_LEN, NA_W)
    lru_init = (state_lru_f.reshape(DEC_BATCH, DEPTH, 1, D_RNN), state_lru_b.reshape(DEC_BATCH, DEPTH, 1, D_RNN))

    w_head_b, w_tail_b = _split_w_in(w_in)
    w_branch_b, w_out_b, w_down_b = w_branch.astype(bf16), w_out.astype(bf16), ffn_w_down.astype(bf16)

    new = {k: None for k in ("k", "v", "lru_f", "lru_b", "ret_f", "ret_b", "ssd_f", "ssd_b")}
    for l in range(DEPTH):
        mod = mod_all[l].reshape(N_COND_PAD, 1, N_MOD * D_MODEL)
        g1 = norm1_g[l][None, :]
        if l == 0:
            proj, xn = in_projection(xs0, 0, 0, N_SAMPLE_TILES, g1, mod, w_head_b, w_tail_b, l)
            proj, xn = in_projection(xp0, 0, N_SAMPLE_TILES, n_ptiles, g1, mod, w_head_b, w_tail_b, l, (proj, xn))
        else:
            proj, xn = in_projection(x, 0, 0, N_TILES, g1, mod, w_head_b, w_tail_b, l)

        w4 = jnp.concatenate([lru_wa[l, 0], lru_wx[l, 0], lru_wa[l, 1], lru_wx[l, 1]], axis=-1).astype(bf16)
        b4 = jnp.concatenate([lru_ba[l, 0].reshape(LRU_BLOCKS, 1, LRU_BS), lru_bx[l, 0].reshape(LRU_BLOCKS, 1, LRU_BS),
                              lru_ba[l, 1].reshape(LRU_BLOCKS, 1, LRU_BS), lru_bx[l, 1].reshape(LRU_BLOCKS, 1, LRU_BS)],
                             axis=-1)
        lru_args = (lru_conv_w[l], lru_conv_b[l][None, :], w4, b4, lru_lambda[l])
        y_lru = lru_branch(proj, (None,), 0, DEC_BATCH, DEC_SEQ, *lru_args, lru_init, l)
        y_lru, new["lru_f"], new["lru_b"] = lru_branch(
            proj, (y_lru, new["lru_f"], new["lru_b"]), T_SAMPLE, BATCH, SEQ, *lru_args, None, l)

        gn = ret_gn_g[l][None, :]
        y_ret = retention_branch(proj, (None,), 0, DEC_BATCH, DEC_SEQ, ret_la, gn, (state_ret_f, state_ret_b), l)
        y_ret, new["ret_f"], new["ret_b"] = retention_branch(
            proj, (y_ret, new["ret_f"], new["ret_b"]), T_SAMPLE, BATCH, SEQ, ret_la, gn, None, l)

        prm = jnp.stack([_group_lanes(ssd_dt_bias[l]), _group_lanes(ssd_a_log[l])], axis=1)
        prm = jnp.concatenate([prm, jnp.zeros((SSD_GROUPS, V7X_SUBLANES - 2, V7X_LANES), f32)], axis=1)
        dvec = jnp.repeat(ssd_d[l], SSD_P)[None, :]
        ssd_args = (ssd_conv_w[l], ssd_conv_b[l][None, :], prm, dvec, ssd_norm_g[l][None, :])
        y_ssd = ssd_branch(proj, (None,), 0, DEC_BATCH, DEC_SEQ, *ssd_args, (state_ssd_f, state_ssd_b), l)
        y_ssd, new["ssd_f"], new["ssd_b"] = ssd_branch(
            proj, (y_ssd, new["ssd_f"], new["ssd_b"]), T_SAMPLE, BATCH, SEQ, *ssd_args, None, l)

        qg = na_q_g[l][None, :]
        kg = na_k_g[l][None, :]
        bias, valid = _na_tables(na_rpb[l])
        y_na = neighbourhood_attention(proj, cache_k, cache_v, l, qg, kg, cos, sin, bias, valid)
        y_na, new["k"], new["v"] = context_attention(proj, (y_na, new["k"], new["v"]), T_SAMPLE, qg, kg, l)

        merged = merge_branches(xn, (y_lru, y_ret, y_ssd, y_na), w_gate, b_gate[l][None, :], w_branch_b, l)
        if l == 0:
            x = residual_projection(merged, w_out_b, l, xs0, mod, 2, 0, N_SAMPLE_TILES)
            x = residual_projection(merged, w_out_b, l, xp0, mod, 2, N_SAMPLE_TILES, n_ptiles, x_tile0=0, buf=x)
        else:
            x = residual_projection(merged, w_out_b, l, x, mod, 2)
        hmid = ffn_up(x, norm2_g[l][None, :], mod, ffn_w_up, l, ffn_conv_w[l], ffn_conv_b[l][None, :])
        if l < DEPTH - 1:
            x = residual_projection(hmid, w_down_b, l, x, mod, 5)
        else:
            y_sample = residual_projection(hmid, w_down_b, l, x, mod, 5, 0, N_SAMPLE_TILES, out_rows=T_SAMPLE)
            y_prompt = residual_projection(hmid, w_down_b, l, x, mod, 5, N_SAMPLE_TILES, n_ptiles,
                                           out_tile0=0, out_rows=T_PROMPT)

    kv_shape = (BATCH, DEPTH, SEQ, NA_HEADS, NA_HD)
    return (y_prompt.reshape(BATCH, SEQ, D_MODEL), y_sample.reshape(DEC_BATCH, DEC_SEQ, D_MODEL),
            new["k"].reshape(kv_shape), new["v"].reshape(kv_shape),
            new["lru_f"].reshape(BATCH, DEPTH, D_RNN), new["lru_b"].reshape(BATCH, DEPTH, D_RNN),
            new["ret_f"], new["ret_b"], new["ssd_f"], new["ssd_b"])
```

```python
import functools
import math

import jax
import jax.numpy as jnp
import numpy as np
from jax import lax
from jax.experimental import pallas as pl
from jax.experimental.pallas import tpu as pltpu

D_MODEL = 2048
BATCH = 16
SEQ = 256
DEPTH = 2
DEC_BATCH = 8
DEC_SEQ = 1024
PAST_LEN = 256
GRID_W = 64
EPS = 1e-6
N_BRANCH = 4
BRANCH_W = 1024
N_MOD = 6
D_RNN = 1024
LRU_BLOCKS = 8
LRU_BS = D_RNN // LRU_BLOCKS
LRU_CONV = 4
LRU_C = 8.0
RET_HEADS = 4
RET_DK = 128
RET_DV = 256
SSD_HEADS = 16
SSD_P = 64
SSD_N = 128
SSD_GROUPS = 2
SSD_CONV = 4
D_SSD = SSD_HEADS * SSD_P
SSD_CONV_CH = D_SSD + 2 * SSD_GROUPS * SSD_N
NA_HEADS = 8
NA_HD = 128
NA_W = NA_HEADS * NA_HD
NA_WR = 8
NA_WC = 16
ROPE_BASE = 10000.0
D_FF = 5632
FFN_CONV = 3
IN_SIZES = (D_RNN, D_RNN,
            RET_HEADS * RET_DK, RET_HEADS * RET_DK, RET_HEADS * RET_DV, RET_HEADS * RET_DV,
            D_SSD, SSD_CONV_CH, 2 * SSD_HEADS,
            NA_W, NA_W, NA_W)
D_IN = sum(IN_SIZES)
IN_OFFSETS = tuple(int(s) for s in np.cumsum(IN_SIZES)[:-1])

V7X_LANES = 128
V7X_SUBLANES = 8
V7X_VMEM_LIMIT_BYTES = 56 * 1024 * 1024

TM = 1024
T_SAMPLE = DEC_BATCH * DEC_SEQ
T_PROMPT = BATCH * SEQ
T_ALL = T_SAMPLE + T_PROMPT
N_SAMPLE_TILES = T_SAMPLE // TM
N_TILES = T_ALL // TM
N_COND = DEC_BATCH + 1
N_COND_PAD = 16

SSD_HPG = SSD_HEADS // SSD_GROUPS
OFF_LRU_X = 0
OFF_LRU_G = OFF_LRU_X + D_RNN
OFF_RET_Q = OFF_LRU_G + D_RNN
OFF_RET_K = OFF_RET_Q + RET_HEADS * RET_DK
OFF_RET_V = OFF_RET_K + RET_HEADS * RET_DK
OFF_RET_G = OFF_RET_V + RET_HEADS * RET_DV
OFF_SSD_Z = OFF_RET_G + RET_HEADS * RET_DV
OFF_SSD_XBC = OFF_SSD_Z + D_SSD
OFF_NA_Q = OFF_SSD_XBC + SSD_CONV_CH
OFF_NA_K = OFF_NA_Q + NA_W
OFF_NA_V = OFF_NA_K + NA_W
OFF_SSD_DT = OFF_NA_V + NA_W
PROJ_TN = 1024
D_IN_PAD = -(-(OFF_SSD_DT + SSD_GROUPS * V7X_LANES) // PROJ_TN) * PROJ_TN
N_HEAD_TILES = OFF_NA_Q // PROJ_TN
HEAD_COLS = N_HEAD_TILES * PROJ_TN

RET_CHUNK = 256
SSD_CHUNK = 256
SSD_TB = 128

f32 = jnp.float32
bf16 = jnp.bfloat16

_ARB = "arbitrary"


def _params(n_axes):
    return pltpu.CompilerParams(dimension_semantics=(_ARB,) * n_axes,
                                vmem_limit_bytes=V7X_VMEM_LIMIT_BYTES)


def _mod_spec(k, tile0=0):
    return pl.BlockSpec((1, 1, D_MODEL), lambda i, j: (jnp.minimum(tile0 + i, N_SAMPLE_TILES), 0, k))


def _dot(a, b):
    return jnp.dot(a, b, preferred_element_type=f32)


def _dot_nt(a, b):
    return lax.dot_general(a, b, (((1,), (1,)), ((), ())), preferred_element_type=f32)


def _sigmoid(x):
    return 0.5 * jnp.tanh(0.5 * x) + 0.5


def _silu(x):
    return x * _sigmoid(x)


def _gelu_tanh(x):
    return 0.5 * x * (1.0 + jnp.tanh(math.sqrt(2.0 / math.pi) * (x + 0.044715 * (x * x * x))))


def _softplus(x):
    return jnp.maximum(x, 0.0) + jnp.log1p(jnp.exp(-jnp.abs(x)))


def _drop_refs(body, n, *refs):
    body(*refs[n:])


def _inplace_call(body, bufs, grid, in_specs, args, out_specs, out_shape, scratch, name):
    held = [(k, b) for k, b in enumerate(bufs) if b is not None]
    if held:
        body = functools.partial(_drop_refs, body, len(held))
        in_specs = [pl.BlockSpec(memory_space=pl.ANY)] * len(held) + list(in_specs)
        args = [b for _, b in held] + list(args)
    aliases = {pos: k for pos, (k, _) in enumerate(held)}
    return pl.pallas_call(
        body, grid=grid, in_specs=in_specs, out_specs=out_specs, out_shape=out_shape,
        scratch_shapes=scratch, input_output_aliases=aliases,
        compiler_params=_params(len(grid)), name=name,
    )(*args)


_BRANCH_BUF = jax.ShapeDtypeStruct((T_ALL, BRANCH_W), bf16)


ADA_TN = 1024


def _ada_kernel(c_ref, w_ref, b_ref, o_ref):
    c = _silu(c_ref[...]).astype(bf16)
    o_ref[0] = _dot(c, w_ref[0].astype(bf16)) + b_ref[0]


def ada_modulation(cond, w_ada, b_ada):
    n = N_MOD * D_MODEL
    return pl.pallas_call(
        _ada_kernel,
        grid=(DEPTH, n // ADA_TN),
        in_specs=[pl.BlockSpec((N_COND_PAD, D_MODEL), lambda l, j: (0, 0)),
                  pl.BlockSpec((1, D_MODEL, ADA_TN), lambda l, j: (l, 0, j)),
                  pl.BlockSpec((1, 1, ADA_TN), lambda l, j: (l, 0, j))],
        out_specs=pl.BlockSpec((1, N_COND_PAD, ADA_TN), lambda l, j: (l, 0, j)),
        out_shape=jax.ShapeDtypeStruct((DEPTH, N_COND_PAD, n), f32),
        compiler_params=_params(2),
        name="ada_modulation",
    )(cond, w_ada, b_ada.reshape(DEPTH, 1, n))


NORM_ROWS = 128


def _modulated_norm(x_ref, g_ref, sc_ref, sh_ref, xn_ref):
    g = g_ref[...]
    sc = 1.0 + sc_ref[0]
    sh = sh_ref[0]

    def body(r, carry):
        rows = pl.ds(pl.multiple_of(r * NORM_ROWS, NORM_ROWS), NORM_ROWS)
        x = x_ref[rows, :]
        y = x * lax.rsqrt(jnp.mean(x * x, axis=-1, keepdims=True) + EPS)
        xn_ref[rows, :] = ((y * g) * sc + sh).astype(bf16)
        return carry

    lax.fori_loop(0, TM // NORM_ROWS, body, 0)


def _in_proj_kernel(x_ref, g_ref, sc_ref, sh_ref, wh_ref, wt_ref, o_ref, xn_ref):
    j = pl.program_id(1)

    @pl.when(j == 0)
    def _():
        _modulated_norm(x_ref, g_ref, sc_ref, sh_ref, xn_ref)

    @pl.when(j < N_HEAD_TILES)
    def _():
        o_ref[...] = _dot(xn_ref[...], wh_ref[0]).astype(bf16)

    @pl.when(j >= N_HEAD_TILES)
    def _():
        o_ref[...] = _dot(xn_ref[...], wt_ref[0]).astype(bf16)


def in_projection(x, x_tile0, tile0, n_tiles, g, mod, w_head, w_tail, layer, bufs=(None, None)):
    n = D_IN_PAD
    n_tail = w_tail.shape[2] // PROJ_TN
    return _inplace_call(
        _in_proj_kernel, bufs, (n_tiles, n // PROJ_TN),
        [pl.BlockSpec((TM, D_MODEL), lambda i, j: (x_tile0 + i, 0)),
         pl.BlockSpec((1, D_MODEL), lambda i, j: (0, 0)),
         _mod_spec(1, tile0), _mod_spec(0, tile0),
         pl.BlockSpec((1, D_MODEL, PROJ_TN), lambda i, j: (layer, 0, jnp.minimum(j, N_HEAD_TILES - 1))),
         pl.BlockSpec((1, D_MODEL, PROJ_TN),
                      lambda i, j: (layer, 0, jnp.clip(j - N_HEAD_TILES, 0, n_tail - 1)))],
        [x, g, mod, mod, w_head, w_tail],
        [pl.BlockSpec((TM, PROJ_TN), lambda i, j: (tile0 + i, j)),
         pl.BlockSpec((TM, D_MODEL), lambda i, j: (tile0 + i, 0))],
        [jax.ShapeDtypeStruct((T_ALL, n), bf16), jax.ShapeDtypeStruct((T_ALL, D_MODEL), bf16)],
        [], "in_projection")


MERGE_TN = 256


def _merge_kernel(xn_ref, *refs):
    y_refs, wg_refs, bg_refs = refs[0:4], refs[4:8], refs[8:12]
    wb_ref, o_ref = refs[12], refs[13]
    xn = xn_ref[...]
    acc = None
    for n in range(N_BRANCH):
        gate = _sigmoid(_dot(xn, wg_refs[n][0].astype(bf16)) + bg_refs[n][...])
        term = gate * _dot(y_refs[n][...], wb_ref[0, n])
        acc = term if acc is None else acc + term
    o_ref[...] = acc.astype(bf16)


def merge_branches(xn, ys, w_gate, b_gate, w_branch, layer):
    nj = D_MODEL // MERGE_TN
    y_spec = pl.BlockSpec((TM, BRANCH_W), lambda i, j: (i, 0))
    wg_specs = [pl.BlockSpec((1, D_MODEL, MERGE_TN), lambda i, j, n=n: (layer, 0, n * nj + j))
                for n in range(N_BRANCH)]
    bg_specs = [pl.BlockSpec((1, MERGE_TN), lambda i, j, n=n: (0, n * nj + j)) for n in range(N_BRANCH)]
    return pl.pallas_call(
        _merge_kernel,
        grid=(N_TILES, nj),
        in_specs=[pl.BlockSpec((TM, D_MODEL), lambda i, j: (i, 0)),
                  y_spec, y_spec, y_spec, y_spec, *wg_specs, *bg_specs,
                  pl.BlockSpec((1, N_BRANCH, BRANCH_W, MERGE_TN), lambda i, j: (layer, 0, 0, j))],
        out_specs=pl.BlockSpec((TM, MERGE_TN), lambda i, j: (i, j)),
        out_shape=jax.ShapeDtypeStruct((T_ALL, D_MODEL), bf16),
        compiler_params=_params(2),
        name="merge_branches",
    )(xn, *ys, *([w_gate] * N_BRANCH), *([b_gate] * N_BRANCH), w_branch)


RES_TN = 512
RES_TN_SHORT_K = 1024


def _residual_kernel(a_ref, w_ref, x_ref, gv_ref, o_ref):
    o_ref[...] = x_ref[...] + gv_ref[0] * _dot(a_ref[...], w_ref[0])


def residual_projection(a, w, layer, x, mod, k_mod, tile0=0, n_tiles=N_TILES, x_tile0=None, out_tile0=None,
                        out_rows=T_ALL, buf=None):
    kdim = a.shape[1]
    tn = RES_TN_SHORT_K if kdim == D_MODEL else RES_TN
    x_tile0 = tile0 if x_tile0 is None else x_tile0
    out_tile0 = tile0 if out_tile0 is None else out_tile0
    return _inplace_call(
        _residual_kernel, (buf,), (n_tiles, D_MODEL // tn),
        [pl.BlockSpec((TM, kdim), lambda i, j: (tile0 + i, 0)),
         pl.BlockSpec((1, kdim, tn), lambda i, j: (layer, 0, j)),
         pl.BlockSpec((TM, tn), lambda i, j: (x_tile0 + i, j)),
         pl.BlockSpec((1, 1, tn),
                      lambda i, j: (jnp.minimum(tile0 + i, N_SAMPLE_TILES), 0, k_mod * (D_MODEL // tn) + j))],
        [a, w, x, mod],
        pl.BlockSpec((TM, tn), lambda i, j: (out_tile0 + i, j)),
        jax.ShapeDtypeStruct((out_rows, D_MODEL), f32), [], "residual_projection")


FFN_TN = 512


def _ffn_up_kernel(x_ref, g_ref, sc_ref, sh_ref, wa_ref, wv_ref, cw_ref, cb_ref, o_ref, xn_ref, pad_ref):
    i = pl.program_id(0)

    @pl.when(pl.program_id(1) == 0)
    def _():
        _modulated_norm(x_ref, g_ref, sc_ref, sh_ref, xn_ref)
        zeros = jnp.zeros((V7X_SUBLANES, FFN_TN), f32)
        pad_ref[pl.ds(0, V7X_SUBLANES), :] = zeros
        pad_ref[pl.ds(V7X_SUBLANES + TM, V7X_SUBLANES), :] = zeros

    xn = xn_ref[...]
    a = _dot(xn, wa_ref[0].astype(bf16))
    pad_ref[pl.ds(V7X_SUBLANES, TM), :] = a
    seq_len = jnp.where(i < N_SAMPLE_TILES, DEC_SEQ, SEQ)
    pos = lax.broadcasted_iota(jnp.int32, (TM, 1), 0) & (seq_len - 1)
    prev = jnp.where(pos == 0, 0.0, pad_ref[pl.ds(V7X_SUBLANES - 1, TM), :])
    nxt = jnp.where(pos == seq_len - 1, 0.0, pad_ref[pl.ds(V7X_SUBLANES + 1, TM), :])
    cw = cw_ref[...]
    conv = cb_ref[...] + prev * cw[0:1, :] + a * cw[1:2, :] + nxt * cw[2:3, :]
    o_ref[...] = (_gelu_tanh(conv) * _dot(xn, wv_ref[0].astype(bf16))).astype(bf16)


def ffn_up(x, g, mod, w_up, layer, conv_w, conv_b):
    nj = D_FF // FFN_TN
    return pl.pallas_call(
        _ffn_up_kernel,
        grid=(N_TILES, nj),
        in_specs=[pl.BlockSpec((TM, D_MODEL), lambda i, j: (i, 0)),
                  pl.BlockSpec((1, D_MODEL), lambda i, j: (0, 0)),
                  _mod_spec(4), _mod_spec(3),
                  pl.BlockSpec((1, D_MODEL, FFN_TN), lambda i, j: (layer, 0, j)),
                  pl.BlockSpec((1, D_MODEL, FFN_TN), lambda i, j: (layer, 0, nj + j)),
                  pl.BlockSpec((FFN_CONV, FFN_TN), lambda i, j: (0, j)),
                  pl.BlockSpec((1, FFN_TN), lambda i, j: (0, j))],
        out_specs=pl.BlockSpec((TM, FFN_TN), lambda i, j: (i, j)),
        out_shape=jax.ShapeDtypeStruct((T_ALL, D_FF), bf16),
        scratch_shapes=[pltpu.VMEM((TM, D_MODEL), bf16),
                        pltpu.VMEM((TM + 2 * V7X_SUBLANES, FFN_TN), f32)],
        compiler_params=_params(2),
        name="ffn_up",
    )(x, g, mod, mod, w_up, w_up, conv_w, conv_b)


LRU_CB_LONG = D_RNN
LRU_CB_SHORT = D_RNN


def _lru_kernel(*refs, seq_len, has_init, cbw):
    if has_init:
        (x_ref, g_ref, cw_ref, cb_ref, w4_ref, b4_ref, lam_ref, h0f_ref, h0b_ref,
         y_ref, pad_ref, af_ref, uf_ref, ab_ref, ub_ref, hf_ref, hb_ref) = refs
    else:
        (x_ref, g_ref, cw_ref, cb_ref, w4_ref, b4_ref, lam_ref,
         y_ref, sf_ref, sb_ref, pad_ref, af_ref, uf_ref, ab_ref, ub_ref, hf_ref, hb_ref) = refs
    L = seq_len
    S = V7X_SUBLANES
    zeros = jnp.zeros((S, cbw), f32)
    pad_ref[pl.ds(0, S), :] = zeros
    pad_ref[pl.ds(S + L, S), :] = zeros
    pad_ref[pl.ds(S, L), :] = x_ref[...].astype(f32)
    cw = cw_ref[...]
    xc = cb_ref[...]
    for kk in range(LRU_CONV):
        xc = xc + pad_ref[pl.ds(S + kk - LRU_CONV // 2, L), :] * cw[kk:kk + 1, :]
    sp = _softplus(-lam_ref[...])
    for n in range(cbw // LRU_BS):
        cols = slice(n * LRU_BS, (n + 1) * LRU_BS)
        xcn = xc[:, cols]
        z = _dot(xcn.astype(bf16), w4_ref[n]) + b4_ref[n]
        for d, (a_ref, u_ref) in enumerate(((af_ref, uf_ref), (ab_ref, ub_ref))):
            r = _sigmoid(z[:, (2 * d) * LRU_BS:(2 * d + 1) * LRU_BS])
            ig = _sigmoid(z[:, (2 * d + 1) * LRU_BS:(2 * d + 2) * LRU_BS])
            log_a = (-LRU_C) * r * sp[d:d + 1, cols]
            a = jnp.exp(log_a)
            a_ref[:, cols] = a
            gain2 = -jnp.tanh(log_a) * (a * a + 1.0)
            gain = jnp.where(gain2 > 0.0, gain2 * lax.rsqrt(gain2), 0.0)
            u_ref[:, cols] = gain * (ig * xcn)

    if has_init:
        h0f = h0f_ref[0, 0]
        h0b = h0b_ref[0, 0]
    else:
        h0f = jnp.zeros((1, cbw), f32)
        h0b = jnp.zeros((1, cbw), f32)

    def step(i, carry):
        hf, hb = carry
        base_f = pl.multiple_of(i * S, S)
        base_b = pl.multiple_of(L - S - i * S, S)
        for r in range(S):
            tf = pl.ds(base_f + r, 1)
            tb = pl.ds(base_b + (S - 1 - r), 1)
            hf = af_ref[tf, :] * hf + uf_ref[tf, :]
            hb = ab_ref[tb, :] * hb + ub_ref[tb, :]
            hf_ref[tf, :] = hf
            hb_ref[tb, :] = hb
        return hf, hb

    hf, hb = lax.fori_loop(0, L // S, step, (h0f, h0b))
    if not has_init:
        sf_ref[0, 0] = hf
        sb_ref[0, 0] = hb
    y_ref[...] = ((hf_ref[...] + hb_ref[...]) * _gelu_tanh(g_ref[...].astype(f32))).astype(bf16)


def lru_branch(proj, bufs, row0, n_seq, seq_len, cw, cb, w4, b4, lam, init, layer):
    has_init = init is not None
    cbw = LRU_CB_LONG if seq_len == DEC_SEQ else LRU_CB_SHORT
    ncb = D_RNN // cbw
    nb = cbw // LRU_BS
    rb0 = row0 // seq_len
    in_specs = [pl.BlockSpec((seq_len, cbw), lambda s, c: (rb0 + s, OFF_LRU_X // cbw + c)),
                pl.BlockSpec((seq_len, cbw), lambda s, c: (rb0 + s, OFF_LRU_G // cbw + c)),
                pl.BlockSpec((LRU_CONV, cbw), lambda s, c: (0, c)),
                pl.BlockSpec((1, cbw), lambda s, c: (0, c)),
                pl.BlockSpec((nb, LRU_BS, 4 * LRU_BS), lambda s, c: (c, 0, 0)),
                pl.BlockSpec((nb, 1, 4 * LRU_BS), lambda s, c: (c, 0, 0)),
                pl.BlockSpec((2, cbw), lambda s, c: (0, c))]
    args = [proj, proj, cw, cb, w4, b4, lam]
    state_spec = pl.BlockSpec((1, 1, 1, cbw), lambda s, c: (s, layer, 0, c))
    y_spec = pl.BlockSpec((seq_len, cbw), lambda s, c: (rb0 + s, c))
    if has_init:
        in_specs += [state_spec, state_spec]
        args += [init[0], init[1]]
        out_specs, out_shape = y_spec, _BRANCH_BUF
    else:
        st = jax.ShapeDtypeStruct((n_seq, DEPTH, 1, D_RNN), f32)
        out_specs, out_shape = [y_spec, state_spec, state_spec], [_BRANCH_BUF, st, st]
    scratch = [pltpu.VMEM((seq_len + 2 * V7X_SUBLANES, cbw), f32)] + \
              [pltpu.VMEM((seq_len, cbw), f32) for _ in range(6)]
    return _inplace_call(functools.partial(_lru_kernel, seq_len=seq_len, has_init=has_init, cbw=cbw), bufs,
                         (n_seq, ncb), in_specs, args, out_specs, out_shape, scratch, "lru_branch")


def _ret_kernel(*refs, seq_len, has_init):
    if has_init:
        (la_ref, q_ref, k_ref, v_ref, g_ref, gn_ref, s0f_ref, s0b_ref, y_ref, acc_ref) = refs
    else:
        (la_ref, q_ref, k_ref, v_ref, g_ref, gn_ref, y_ref, sf_ref, sb_ref, acc_ref) = refs
    T = min(RET_CHUNK, seq_len)
    nc = seq_len // T
    tt = lax.broadcasted_iota(jnp.int32, (T, T), 0)
    ss = lax.broadcasted_iota(jnp.int32, (T, T), 1)
    diff = (tt - ss).astype(f32)
    tcol = lax.broadcasted_iota(jnp.int32, (T, 1), 0).astype(f32)
    scale = RET_DK ** -0.5
    for h in range(RET_HEADS):
        la_f = la_ref[h, 0]
        la_b = la_ref[h, 1]
        kcols = pl.ds(h * RET_DK, RET_DK)
        vcols = pl.ds(h * RET_DV, RET_DV)
        dsum = (jnp.where(tt >= ss, jnp.exp(la_f * diff), 0.0)
                + jnp.where(ss >= tt, jnp.exp(-la_b * diff), 0.0))

        def chunk(c):
            rows = pl.ds(c * T, T)
            q = q_ref[rows, kcols].astype(f32)
            ks = k_ref[rows, kcols].astype(f32) * scale
            v = v_ref[rows, vcols].astype(f32)
            return rows, q, ks, v

        s_f = s0f_ref[0, 0, h] if has_init else None
        for c in range(nc):
            rows, q, ks, v = chunk(c)
            scores = _dot_nt(q.astype(bf16), ks.astype(bf16)) * dsum
            y = _dot(scores.astype(bf16), v.astype(bf16))
            if s_f is not None:
                y = y + _dot((q * jnp.exp(la_f * (tcol + 1.0))).astype(bf16), s_f.astype(bf16))
            acc_ref[rows, :] = y
            if c < nc - 1 or not has_init:
                upd = _dot(ks.T.astype(bf16), (v * jnp.exp(la_f * (T - 1.0 - tcol))).astype(bf16))
                s_f = upd if s_f is None else jnp.exp(la_f * T) * s_f + upd
        s_b = s0b_ref[0, 0, h] if has_init else None
        for c in reversed(range(nc)):
            rows, q, ks, v = chunk(c)
            if s_b is not None:
                acc_ref[rows, :] += _dot((q * jnp.exp(la_b * (T - tcol))).astype(bf16), s_b.astype(bf16))
            if c > 0 or not has_init:
                upd = _dot(ks.T.astype(bf16), (v * jnp.exp(la_b * tcol)).astype(bf16))
                s_b = upd if s_b is None else jnp.exp(la_b * T) * s_b + upd
        if not has_init:
            sf_ref[0, 0, h] = s_f
            sb_ref[0, 0, h] = s_b
        y = acc_ref[...]
        mu = jnp.mean(y, axis=-1, keepdims=True)
        yc = y - mu
        var = jnp.mean(yc * yc, axis=-1, keepdims=True)
        y = yc * lax.rsqrt(var + EPS) * gn_ref[:, vcols]
        y_ref[:, vcols] = (y * _silu(g_ref[:, vcols].astype(f32))).astype(bf16)


def retention_branch(proj, bufs, row0, n_seq, seq_len, la, gn, init, layer):
    has_init = init is not None
    rb0 = row0 // seq_len
    qk_w, vg_w = RET_HEADS * RET_DK, RET_HEADS * RET_DV
    in_specs = [pl.BlockSpec(memory_space=pltpu.SMEM),
                pl.BlockSpec((seq_len, qk_w), lambda s: (rb0 + s, OFF_RET_Q // qk_w)),
                pl.BlockSpec((seq_len, qk_w), lambda s: (rb0 + s, OFF_RET_K // qk_w)),
                pl.BlockSpec((seq_len, vg_w), lambda s: (rb0 + s, OFF_RET_V // vg_w)),
                pl.BlockSpec((seq_len, vg_w), lambda s: (rb0 + s, OFF_RET_G // vg_w)),
                pl.BlockSpec((1, vg_w), lambda s: (0, 0))]
    args = [la, proj, proj, proj, proj, gn]
    state_spec = pl.BlockSpec((1, 1, RET_HEADS, RET_DK, RET_DV), lambda s: (s, layer, 0, 0, 0))
    y_spec = pl.BlockSpec((seq_len, vg_w), lambda s: (rb0 + s, 0))
    if has_init:
        in_specs += [state_spec, state_spec]
        args += [init[0], init[1]]
        out_specs, out_shape = y_spec, _BRANCH_BUF
    else:
        st = jax.ShapeDtypeStruct((n_seq, DEPTH, RET_HEADS, RET_DK, RET_DV), f32)
        out_specs, out_shape = [y_spec, state_spec, state_spec], [_BRANCH_BUF, st, st]
    return _inplace_call(functools.partial(_ret_kernel, seq_len=seq_len, has_init=has_init), bufs,
                         (n_seq,), in_specs, args, out_specs, out_shape,
                         [pltpu.VMEM((seq_len, RET_DV), f32)], "retention_branch")


SSD_GW = SSD_HPG * SSD_P


def _split3(x):
    h1 = x.astype(bf16)
    r1 = x - h1.astype(f32)
    h2 = r1.astype(bf16)
    h3 = (r1 - h2.astype(f32)).astype(bf16)
    return h1, h2, h3


def _dot_exact_rhs(m, x):
    h1, h2, h3 = _split3(x)
    return _dot(m, h1) + _dot(m, h2) + _dot(m, h3)


def _dot_exact_lhs(x, m):
    h1, h2, h3 = _split3(x)
    return _dot(h1, m) + _dot(h2, m) + _dot(h3, m)


def _conv4_silu(src_ref, pad_ref, cw_ref, cb_ref, L, width):
    S = V7X_SUBLANES
    zeros = jnp.zeros((S, width), f32)
    pad_ref[pl.ds(0, S), pl.ds(0, width)] = zeros
    pad_ref[pl.ds(S + L, S), pl.ds(0, width)] = zeros
    pad_ref[pl.ds(S, L), pl.ds(0, width)] = src_ref[...].astype(f32)
    cw = cw_ref[...]
    out = cb_ref[...]
    for kk in range(SSD_CONV):
        out = out + pad_ref[pl.ds(S + kk - SSD_CONV // 2, L), pl.ds(0, width)] * cw[kk:kk + 1, :]
    return _silu(out)


def _heads_to_lanes(s_ref, st_ref):
    for hh in range(SSD_HPG):
        st_ref[:, pl.ds(hh * SSD_P, SSD_P)] = s_ref[0, 0, hh]
    return st_ref[...]


def _ssd_kernel(*refs, seq_len, has_init):
    if has_init:
        (z_ref, x_ref, b_ref, c_ref, dt_ref, cwx_ref, cbx_ref, cwb_ref, cbb_ref, cwc_ref, cbc_ref,
         prm_ref, dvec_ref, ng_ref, s0f_ref, s0b_ref,
         y_ref, pad_ref, xs_ref, bs_ref, cs_ref, acc_ref, rb_ref, yn_ref, st_ref) = refs
    else:
        (z_ref, x_ref, b_ref, c_ref, dt_ref, cwx_ref, cbx_ref, cwb_ref, cbb_ref, cwc_ref, cbc_ref,
         prm_ref, dvec_ref, ng_ref,
         y_ref, sf_ref, sb_ref, pad_ref, xs_ref, bs_ref, cs_ref, acc_ref, rb_ref, yn_ref, st_ref) = refs
    g = pl.program_id(1)
    L = seq_len
    T = min(SSD_CHUNK, L)
    nc = L // T
    H = SSD_HPG
    xs_ref[...] = _conv4_silu(x_ref, pad_ref, cwx_ref, cbx_ref, L, SSD_GW)
    bs_ref[...] = _conv4_silu(b_ref, pad_ref, cwb_ref, cbb_ref, L, SSD_N)
    cs_ref[...] = _conv4_silu(c_ref, pad_ref, cwc_ref, cbc_ref, L, SSD_N)
    prm = prm_ref[0]
    a_neg = -jnp.exp(prm[1:2, :])
    tt = lax.broadcasted_iota(jnp.int32, (T, T), 0)
    ss = lax.broadcasted_iota(jnp.int32, (T, T), 1)
    lower = tt >= ss
    upper = ss >= tt
    tri_l = jnp.where(lower, 1.0, 0.0).astype(bf16)
    tri_u = jnp.where(upper, 1.0, 0.0).astype(bf16)
    lower_b = lower[:SSD_TB, :SSD_TB]
    upper_b = upper[:SSD_TB, :SSD_TB]
    er = lax.broadcasted_iota(jnp.int32, (V7X_LANES, SSD_GW), 0)
    ec = lax.broadcasted_iota(jnp.int32, (V7X_LANES, SSD_GW), 1) // SSD_P
    exp_f = jnp.where(er == ec, 1.0, 0.0).astype(bf16)
    exp_b = jnp.where(er == ec + H, 1.0, 0.0).astype(bf16)

    def expand(w, e):
        hi = w.astype(bf16)
        lo = (w - hi.astype(f32)).astype(bf16)
        return _dot(hi, e) + _dot(lo, e)

    def chunk_terms(c):
        rows = pl.ds(c * T, T)
        dt = _softplus(dt_ref[rows, :].astype(f32) + prm[0:1, :])
        da = dt * a_neg
        return rows, dt, da

    s_f = _heads_to_lanes(s0f_ref, st_ref) if has_init else None
    for c in range(nc):
        rows, dt, da = chunk_terms(c)
        cum = _dot_exact_rhs(tri_l, da)
        rsum = _dot_exact_rhs(tri_u, da)
        rb_ref[rows, :] = rsum
        da_t = da.T
        cum_t = _dot_exact_lhs(da_t, tri_u)
        rsum_t = _dot_exact_lhs(da_t, tri_l)
        dt_t = dt.T
        bmat = bs_ref[rows, :]
        cmat = cs_ref[rows, :]
        xmat = xs_ref[rows, :]
        gmat = _dot_nt(cmat.astype(bf16), bmat.astype(bf16))
        for hh in range(H):
            cf, cf_t, dtf_t = cum[:, hh:hh + 1], cum_t[hh:hh + 1, :], dt_t[hh:hh + 1, :]
            rb, rb_t, dtb_t = rsum[:, H + hh:H + hh + 1], rsum_t[H + hh:H + hh + 1, :], dt_t[H + hh:H + hh + 1, :]
            block_rows = []
            for bi in range(T // SSD_TB):
                rr = slice(bi * SSD_TB, (bi + 1) * SSD_TB)
                blocks = []
                for bj in range(T // SSD_TB):
                    cc = slice(bj * SSD_TB, (bj + 1) * SSD_TB)
                    if bi > bj:
                        blk = jnp.exp(cf[rr] - cf_t[:, cc]) * dtf_t[:, cc]
                    elif bi < bj:
                        blk = jnp.exp(rb[rr] - rb_t[:, cc]) * dtb_t[:, cc]
                    else:
                        blk = (jnp.where(lower_b, jnp.exp(cf[rr] - cf_t[:, cc]), 0.0) * dtf_t[:, cc]
                               + jnp.where(upper_b, jnp.exp(rb[rr] - rb_t[:, cc]), 0.0) * dtb_t[:, cc])
                    blocks.append(gmat[rr, cc] * blk)
                block_rows.append(jnp.concatenate(blocks, axis=1))
            m = jnp.concatenate(block_rows, axis=0)
            xh = xmat[:, hh * SSD_P:(hh + 1) * SSD_P]
            acc_ref[rows, pl.ds(hh * SSD_P, SSD_P)] = _dot(m.astype(bf16), xh.astype(bf16))
        ecum = jnp.exp(cum)
        if s_f is not None:
            acc_ref[rows, :] += _dot(cmat.astype(bf16), s_f.astype(bf16)) * expand(ecum, exp_f)
        if c < nc - 1 or not has_init:
            tail = jnp.exp(cum[T - 1:T, :] - cum) * dt
            xw = (xmat * expand(tail, exp_f)).astype(bf16)
            upd = _dot(bmat.T.astype(bf16), xw)
            if s_f is None:
                s_f = upd
            else:
                s_f = s_f * expand(jnp.broadcast_to(ecum[T - 1:T, :], (V7X_SUBLANES, V7X_LANES)),
                                   exp_f)[0:1, :] + upd
    s_b = _heads_to_lanes(s0b_ref, st_ref) if has_init else None
    for c in reversed(range(nc)):
        rows, dt, da = chunk_terms(c)
        rsum = rb_ref[rows, :]
        ers = jnp.exp(rsum)
        bmat = bs_ref[rows, :]
        cmat = cs_ref[rows, :]
        xmat = xs_ref[rows, :]
        if s_b is not None:
            acc_ref[rows, :] += _dot(cmat.astype(bf16), s_b.astype(bf16)) * expand(ers, exp_b)
        if c > 0 or not has_init:
            tail = jnp.exp(rsum[0:1, :] - rsum) * dt
            xw = (xmat * expand(tail, exp_b)).astype(bf16)
            upd = _dot(bmat.T.astype(bf16), xw)
            if s_b is None:
                s_b = upd
            else:
                s_b = s_b * expand(jnp.broadcast_to(ers[0:1, :], (V7X_SUBLANES, V7X_LANES)),
                                   exp_b)[0:1, :] + upd
    if not has_init:
        for hh in range(H):
            sf_ref[0, 0, hh] = s_f[:, hh * SSD_P:(hh + 1) * SSD_P]
            sb_ref[0, 0, hh] = s_b[:, hh * SSD_P:(hh + 1) * SSD_P]
    yg = (acc_ref[...] + xs_ref[...] * dvec_ref[...]) * _silu(z_ref[...].astype(f32))
    yn_ref[g] = yg

    @pl.when(g == SSD_GROUPS - 1)
    def _():
        ssq = None
        for gg in range(SSD_GROUPS):
            y = yn_ref[gg]
            s = jnp.sum(y * y, axis=-1, keepdims=True)
            ssq = s if ssq is None else ssq + s
        inv = lax.rsqrt(ssq * (1.0 / D_SSD) + EPS)
        for gg in range(SSD_GROUPS):
            cols = pl.ds(gg * SSD_GW, SSD_GW)
            y_ref[:, cols] = (yn_ref[gg] * inv * ng_ref[:, cols]).astype(bf16)


def ssd_branch(proj, bufs, row0, n_seq, seq_len, conv_w, conv_b, prm, dvec, ng, init, layer):
    has_init = init is not None
    rb0 = row0 // seq_len
    xoff = OFF_SSD_XBC
    boff = OFF_SSD_XBC + D_SSD
    coff = boff + SSD_GROUPS * SSD_N
    in_specs = [pl.BlockSpec((seq_len, SSD_GW), lambda s, g: (rb0 + s, OFF_SSD_Z // SSD_GW + g)),
                pl.BlockSpec((seq_len, SSD_GW), lambda s, g: (rb0 + s, xoff // SSD_GW + g)),
                pl.BlockSpec((seq_len, SSD_N), lambda s, g: (rb0 + s, boff // SSD_N + g)),
                pl.BlockSpec((seq_len, SSD_N), lambda s, g: (rb0 + s, coff // SSD_N + g)),
                pl.BlockSpec((seq_len, V7X_LANES), lambda s, g: (rb0 + s, OFF_SSD_DT // V7X_LANES + g)),
                pl.BlockSpec((SSD_CONV, SSD_GW), lambda s, g: (0, g)),
                pl.BlockSpec((1, SSD_GW), lambda s, g: (0, g)),
                pl.BlockSpec((SSD_CONV, SSD_N), lambda s, g: (0, D_SSD // SSD_N + g)),
                pl.BlockSpec((1, SSD_N), lambda s, g: (0, D_SSD // SSD_N + g)),
                pl.BlockSpec((SSD_CONV, SSD_N), lambda s, g: (0, D_SSD // SSD_N + SSD_GROUPS + g)),
                pl.BlockSpec((1, SSD_N), lambda s, g: (0, D_SSD // SSD_N + SSD_GROUPS + g)),
                pl.BlockSpec((1, V7X_SUBLANES, V7X_LANES), lambda s, g: (g, 0, 0)),
                pl.BlockSpec((1, SSD_GW), lambda s, g: (0, g)),
                pl.BlockSpec((1, D_SSD), lambda s, g: (0, 0))]
    args = [proj, proj, proj, proj, proj, conv_w, conv_b, conv_w, conv_b, conv_w, conv_b, prm, dvec, ng]
    state_spec = pl.BlockSpec((1, 1, SSD_HPG, SSD_N, SSD_P), lambda s, g: (s, layer, g, 0, 0))
    y_spec = pl.BlockSpec((seq_len, D_SSD), lambda s, g: (rb0 + s, 0))
    if has_init:
        in_specs += [state_spec, state_spec]
        args += [init[0], init[1]]
        out_specs, out_shape = y_spec, _BRANCH_BUF
    else:
        st = jax.ShapeDtypeStruct((n_seq, DEPTH, SSD_HEADS, SSD_N, SSD_P), f32)
        out_specs, out_shape = [y_spec, state_spec, state_spec], [_BRANCH_BUF, st, st]
    scratch = [pltpu.VMEM((seq_len + 2 * V7X_SUBLANES, SSD_GW), f32),
               pltpu.VMEM((seq_len, SSD_GW), f32),
               pltpu.VMEM((seq_len, SSD_N), f32),
               pltpu.VMEM((seq_len, SSD_N), f32),
               pltpu.VMEM((seq_len, SSD_GW), f32),
               pltpu.VMEM((seq_len, V7X_LANES), f32),
               pltpu.VMEM((SSD_GROUPS, seq_len, SSD_GW), f32),
               pltpu.VMEM((SSD_N, SSD_GW), f32)]
    return _inplace_call(functools.partial(_ssd_kernel, seq_len=seq_len, has_init=has_init), bufs,
                         (n_seq, SSD_GROUPS), in_specs, args, out_specs, out_shape, scratch, "ssd_branch")


def _head_rmsnorm(x, g):
    return x * lax.rsqrt(jnp.mean(x * x, axis=-1, keepdims=True) + EPS) * g


CTX_HB = 4
CTX_W = CTX_HB * NA_HD


def _ctx_attn_kernel(q_ref, k_ref, v_ref, qg_ref, kg_ref, y_ref, ko_ref, vo_ref):
    scale = NA_HD ** -0.5
    vo_ref[0, 0] = v_ref[...].astype(f32)
    for h in range(CTX_HB):
        cols = pl.ds(h * NA_HD, NA_HD)
        q = _head_rmsnorm(q_ref[:, cols].astype(f32), qg_ref[...])
        k = _head_rmsnorm(k_ref[:, cols].astype(f32), kg_ref[...])
        ko_ref[0, 0, :, cols] = k
        s = _dot_nt(q.astype(bf16), k.astype(bf16)) * scale
        p = jnp.exp(s - jnp.max(s, axis=-1, keepdims=True))
        o = _dot(p.astype(bf16), v_ref[:, cols]) / jnp.sum(p, axis=-1, keepdims=True)
        y_ref[:, cols] = o.astype(bf16)


def context_attention(proj, bufs, row0, qg, kg, layer):
    rb0 = row0 // SEQ
    spec = lambda off: pl.BlockSpec((SEQ, CTX_W), lambda s, hb: (rb0 + s, off // CTX_W + hb))
    gspec = pl.BlockSpec((1, NA_HD), lambda s, hb: (0, 0))
    kv_spec = pl.BlockSpec((1, 1, SEQ, CTX_W), lambda s, hb: (s, layer, 0, hb))
    kv_shape = jax.ShapeDtypeStruct((BATCH, DEPTH, SEQ, NA_W), f32)
    return _inplace_call(
        _ctx_attn_kernel, bufs, (BATCH, NA_HEADS // CTX_HB),
        [spec(OFF_NA_Q), spec(OFF_NA_K), spec(OFF_NA_V), gspec, gspec], [proj, proj, proj, qg, kg],
        [pl.BlockSpec((SEQ, CTX_W), lambda s, hb: (rb0 + s, hb)), kv_spec, kv_spec],
        [_BRANCH_BUF, kv_shape, kv_shape], [], "context_attention")


NA_ROWS = DEC_SEQ // GRID_W
NA_NK = NA_WR * GRID_W


def _na_row_start(r):
    return min(max(r - NA_WR // 2, 0), NA_ROWS - NA_WR)


def _rope(x, cos, sin_signed):
    lane = lax.broadcasted_iota(jnp.int32, x.shape, 1)
    quarter = NA_HD // 4
    swapped = jnp.where((lane & (2 * quarter - 1)) < quarter,
                        pltpu.roll(x, NA_HD - quarter, axis=1), pltpu.roll(x, quarter, axis=1))
    return x * cos + swapped * sin_signed


def _na_row_groups():
    groups, r = [], 0
    while r < NA_ROWS:
        n = 1
        while r + n < NA_ROWS and _na_row_start(r + n) == _na_row_start(r):
            n += 1
        groups.append((r, n, _na_row_start(r)))
        r += n
    return groups


def _na_attn_kernel(q_ref, k_ref, v_ref, kc_ref, vc_ref, qg_ref, kg_ref, cos_ref, sin_ref,
                    bias_ref, valid_ref, y_ref, qs_ref, ks_ref, s_ref, p_ref, oc_ref, w_ref):
    scale = NA_HD ** -0.5
    cos = cos_ref[...]
    sin = sin_ref[...]
    qs_ref[...] = _rope(_head_rmsnorm(q_ref[...].astype(f32), qg_ref[...]), cos, sin).astype(bf16)
    ks_ref[...] = _rope(_head_rmsnorm(k_ref[...].astype(f32), kg_ref[...]), cos, sin).astype(bf16)
    groups = _na_row_groups()
    valid = valid_ref[...] > 0.0
    for r0, n, rs in groups:
        rows = pl.ds(r0 * GRID_W, n * GRID_W)
        d0 = r0 - rs
        kw = ks_ref[pl.ds(rs * GRID_W, NA_NK), :]
        s = (_dot_nt(qs_ref[rows, :], kw) * scale).reshape(n, GRID_W, NA_NK) + bias_ref[0, d0:d0 + n]
        s_ref[rows, :] = jnp.where(valid[None], s, -1e30).reshape(n * GRID_W, NA_NK)
    s_ctx = _dot_nt(qs_ref[...], kc_ref[0, 0].astype(bf16)) * scale
    s_loc = s_ref[...]
    m = jnp.maximum(jnp.max(s_loc, axis=-1, keepdims=True), jnp.max(s_ctx, axis=-1, keepdims=True))
    p_loc = jnp.exp(s_loc - m)
    p_ctx = jnp.exp(s_ctx - m)
    w_ref[...] = 1.0 / (jnp.sum(p_loc, axis=-1, keepdims=True) + jnp.sum(p_ctx, axis=-1, keepdims=True))
    p_ref[...] = p_loc.astype(bf16)
    oc_ref[...] = _dot(p_ctx.astype(bf16), vc_ref[0, 0].astype(bf16))
    for r0, n, rs in groups:
        rows = pl.ds(r0 * GRID_W, n * GRID_W)
        vw = v_ref[pl.ds(rs * GRID_W, NA_NK), :]
        y_ref[rows, :] = ((_dot(p_ref[rows, :], vw) + oc_ref[rows, :]) * w_ref[rows, :]).astype(bf16)


def neighbourhood_attention(proj, cache_k, cache_v, layer, qg, kg, cos, sin, bias, valid):
    spec = lambda off: pl.BlockSpec((DEC_SEQ, NA_HD), lambda b, h: (b, off // NA_HD + h))
    cspec = pl.BlockSpec((1, 1, PAST_LEN, NA_HD), lambda b, h: (b, layer, 0, h))
    gspec = pl.BlockSpec((1, NA_HD), lambda b, h: (0, 0))
    tspec = pl.BlockSpec((DEC_SEQ, NA_HD), lambda b, h: (0, 0))
    return pl.pallas_call(
        _na_attn_kernel,
        grid=(DEC_BATCH, NA_HEADS),
        in_specs=[spec(OFF_NA_Q), spec(OFF_NA_K), spec(OFF_NA_V), cspec, cspec, gspec, gspec,
                  tspec, tspec,
                  pl.BlockSpec((1, NA_WR, GRID_W, NA_NK), lambda b, h: (h, 0, 0, 0)),
                  pl.BlockSpec((GRID_W, NA_NK), lambda b, h: (0, 0))],
        out_specs=pl.BlockSpec((DEC_SEQ, NA_HD), lambda b, h: (b, h)),
        out_shape=_BRANCH_BUF,
        scratch_shapes=[pltpu.VMEM((DEC_SEQ, NA_HD), bf16), pltpu.VMEM((DEC_SEQ, NA_HD), bf16),
                        pltpu.VMEM((DEC_SEQ, NA_NK), f32), pltpu.VMEM((DEC_SEQ, NA_NK), bf16),
                        pltpu.VMEM((DEC_SEQ, NA_HD), f32), pltpu.VMEM((DEC_SEQ, 1), f32)],
        compiler_params=_params(2), name="neighbourhood_attention",
    )(proj, proj, proj, cache_k, cache_v, qg, kg, cos, sin, bias, valid)


TAIL_TN = 512
N_TAIL_TILES = (D_IN_PAD - HEAD_COLS) // TAIL_TN
N_QKV_TILES = 3 * NA_W // TAIL_TN
DT_W = 2 * SSD_HEADS


def _tail_kernel(a_ref, b_ref, o_ref):
    j = pl.program_id(1)
    lane = lax.broadcasted_iota(jnp.int32, (D_MODEL, TAIL_TN), 1)

    @pl.when(j == 0)
    def _():
        o_ref[0] = a_ref[0]

    @pl.when((j >= 1) & (j <= N_QKV_TILES))
    def _():
        main = pltpu.roll(a_ref[0].astype(f32), TAIL_TN - DT_W, axis=1)
        nxt = pltpu.roll(b_ref[0].astype(f32), V7X_LANES - DT_W, axis=1)
        last_lane = lax.broadcasted_iota(jnp.int32, (D_MODEL, V7X_LANES), 1)
        last = jnp.where(last_lane < V7X_LANES - DT_W, main[:, TAIL_TN - V7X_LANES:], nxt)
        o_ref[0] = jnp.concatenate([main[:, :TAIL_TN - V7X_LANES], last], axis=1).astype(bf16)

    @pl.when(j == N_TAIL_TILES - 1)
    def _():
        a = a_ref[0].astype(f32)
        out = jnp.zeros((D_MODEL, TAIL_TN), f32)
        for gidx in range(SSD_GROUPS):
            base = gidx * V7X_LANES
            for src0, dst0 in ((gidx * SSD_HPG, base), (SSD_HEADS + gidx * SSD_HPG, base + SSD_HPG)):
                shift = (dst0 - src0) % TAIL_TN
                moved = a if shift == 0 else pltpu.roll(a, shift, axis=1)
                out = jnp.where((lane >= dst0) & (lane < dst0 + SSD_HPG), moved, out)
        o_ref[0] = out.astype(bf16)


def _split_w_in(w_in):
    w_in = w_in.astype(bf16)
    dt0 = IN_OFFSETS[7]
    a_tile0, dt_tile, b_tile0 = HEAD_COLS // TAIL_TN, dt0 // TAIL_TN, dt0 // V7X_LANES
    b_per_a = TAIL_TN // V7X_LANES
    last_b = (D_IN - 1) // V7X_LANES
    tail = pl.pallas_call(
        _tail_kernel, grid=(DEPTH, N_TAIL_TILES),
        in_specs=[pl.BlockSpec((1, D_MODEL, TAIL_TN),
                               lambda l, j: (l, 0, jnp.where(j == N_TAIL_TILES - 1, dt_tile, a_tile0 + j))),
                  pl.BlockSpec((1, D_MODEL, V7X_LANES),
                               lambda l, j: (l, 0, jnp.minimum(b_tile0 + b_per_a * j, last_b)))],
        out_specs=pl.BlockSpec((1, D_MODEL, TAIL_TN), lambda l, j: (l, 0, j)),
        out_shape=jax.ShapeDtypeStruct((DEPTH, D_MODEL, D_IN_PAD - HEAD_COLS), bf16),
        compiler_params=_params(2), name="build_w_in_tail",
    )(w_in, w_in)
    return w_in, tail


def _group_lanes(v):
    rows = []
    for gidx in range(SSD_GROUPS):
        sl = slice(gidx * SSD_HPG, (gidx + 1) * SSD_HPG)
        rows.append(jnp.concatenate([v[0, sl], v[1, sl], jnp.zeros((V7X_LANES - 2 * SSD_HPG,), f32)]))
    return jnp.stack(rows)


def _rope_tables():
    t = np.arange(DEC_SEQ)
    quarter = NA_HD // 4
    inv = ROPE_BASE ** (-np.arange(quarter, dtype=np.float32) / quarter)
    ang_r = (t // GRID_W).astype(np.float32)[:, None] * inv
    ang_c = (t % GRID_W).astype(np.float32)[:, None] * inv
    cos = np.concatenate([np.cos(ang_r), np.cos(ang_r), np.cos(ang_c), np.cos(ang_c)], axis=1)
    sin = np.concatenate([-np.sin(ang_r), np.sin(ang_r), -np.sin(ang_c), np.sin(ang_c)], axis=1)
    return jnp.asarray(cos, f32), jnp.asarray(sin, f32)


def _na_tables(rpb):
    cq = np.arange(GRID_W)
    kc = np.tile(np.arange(GRID_W), NA_WR)
    col_start = np.clip(cq - NA_WC // 2, 0, GRID_W - NA_WC)
    valid = (kc[None, :] >= col_start[:, None]) & (kc[None, :] < col_start[:, None] + NA_WC)
    col_off = np.clip(cq[None, :] - cq[:, None], 1 - NA_WC, NA_WC - 1) + NA_WC - 1
    onehot = (col_off[None, :, :] == np.arange(2 * NA_WC - 1)[:, None, None]).astype(np.float32)
    toep = jnp.einsum('hic,cqk->hiqk', rpb.astype(f32), jnp.asarray(onehot), precision=lax.Precision.HIGHEST)
    tables = []
    for d in range(NA_WR):
        rows = toep[:, NA_WR - 1 - d:2 * NA_WR - 1 - d]
        tables.append(rows.transpose(0, 2, 1, 3).reshape(NA_HEADS, GRID_W, NA_NK))
    return jnp.stack(tables, axis=1), jnp.asarray(valid, f32)


def kernel(x_prompt, x_sample, cache_na_k, cache_na_v, state_lru_f, state_lru_b,
           state_ret_f, state_ret_b, state_ssd_f, state_ssd_b, c, c_ctx,
           norm1_g, norm2_g, w_ada, b_ada, w_in, w_gate, b_gate, w_branch, w_out,
           lru_conv_w, lru_conv_b, lru_wa, lru_ba, lru_wx, lru_bx, lru_lambda,
           ret_gn_g, ssd_conv_w, ssd_conv_b, ssd_a_log, ssd_dt_bias, ssd_d, ssd_norm_g,
           na_q_g, na_k_g, na_rpb, ffn_w_up, ffn_conv_w, ffn_conv_b, ffn_w_down):
    xs0 = x_sample.reshape(T_SAMPLE, D_MODEL)
    xp0 = x_prompt.reshape(T_PROMPT, D_MODEL)
    n_ptiles = N_TILES - N_SAMPLE_TILES
    cond = jnp.concatenate([c, c_ctx[None, :], jnp.zeros((N_COND_PAD - N_COND, D_MODEL), f32)], axis=0)
    mod_all = ada_modulation(cond, w_ada, b_ada)
    cos, sin = _rope_tables()
    hh = jnp.arange(RET_HEADS, dtype=f32)
    ret_la = jnp.stack([jnp.log1p(-jnp.exp2(-5.0 - hh)), jnp.log1p(-jnp.exp2(-5.5 - hh))], axis=1)
    cache_k = cache_na_k.reshape(DEC_BATCH, DEPTH, PAST_LEN, NA_W)
    cache_v = cache_na_v.reshape(DEC_BATCH, DEPTH, PAST_LEN, NA_W)
    lru_init = (state_lru_f.reshape(DEC_BATCH, DEPTH, 1, D_RNN), state_lru_b.reshape(DEC_BATCH, DEPTH, 1, D_RNN))

    w_head_b, w_tail_b = _split_w_in(w_in)
    w_branch_b, w_out_b, w_down_b = w_branch.astype(bf16), w_out.astype(bf16), ffn_w_down.astype(bf16)

    new = {k: None for k in ("k", "v", "lru_f", "lru_b", "ret_f", "ret_b", "ssd_f", "ssd_b")}
    for l in range(DEPTH):
        mod = mod_all[l].reshape(N_COND_PAD, 1, N_MOD * D_MODEL)
        g1 = norm1_g[l][None, :]
        if l == 0:
            proj, xn = in_projection(xs0, 0, 0, N_SAMPLE_TILES, g1, mod, w_head_b, w_tail_b, l)
            proj, xn = in_projection(xp0, 0, N_SAMPLE_TILES, n_ptiles, g1, mod, w_head_b, w_tail_b, l, (proj, xn))
        else:
            proj, xn = in_projection(x, 0, 0, N_TILES, g1, mod, w_head_b, w_tail_b, l)

        w4 = jnp.concatenate([lru_wa[l, 0], lru_wx[l, 0], lru_wa[l, 1], lru_wx[l, 1]], axis=-1).astype(bf16)
        b4 = jnp.concatenate([lru_ba[l, 0].reshape(LRU_BLOCKS, 1, LRU_BS), lru_bx[l, 0].reshape(LRU_BLOCKS, 1, LRU_BS),
                              lru_ba[l, 1].reshape(LRU_BLOCKS, 1, LRU_BS), lru_bx[l, 1].reshape(LRU_BLOCKS, 1, LRU_BS)],
                             axis=-1)
        lru_args = (lru_conv_w[l], lru_conv_b[l][None, :], w4, b4, lru_lambda[l])
        y_lru = lru_branch(proj, (None,), 0, DEC_BATCH, DEC_SEQ, *lru_args, lru_init, l)
        y_lru, new["lru_f"], new["lru_b"] = lru_branch(
            proj, (y_lru, new["lru_f"], new["lru_b"]), T_SAMPLE, BATCH, SEQ, *lru_args, None, l)

        gn = ret_gn_g[l][None, :]
        y_ret = retention_branch(proj, (None,), 0, DEC_BATCH, DEC_SEQ, ret_la, gn, (state_ret_f, state_ret_b), l)
        y_ret, new["ret_f"], new["ret_b"] = retention_branch(
            proj, (y_ret, new["ret_f"], new["ret_b"]), T_SAMPLE, BATCH, SEQ, ret_la, gn, None, l)

        prm = jnp.stack([_group_lanes(ssd_dt_bias[l]), _group_lanes(ssd_a_log[l])], axis=1)
        prm = jnp.concatenate([prm, jnp.zeros((SSD_GROUPS, V7X_SUBLANES - 2, V7X_LANES), f32)], axis=1)
        dvec = jnp.repeat(ssd_d[l], SSD_P)[None, :]
        ssd_args = (ssd_conv_w[l], ssd_conv_b[l][None, :], prm, dvec, ssd_norm_g[l][None, :])
        y_ssd = ssd_branch(proj, (None,), 0, DEC_BATCH, DEC_SEQ, *ssd_args, (state_ssd_f, state_ssd_b), l)
        y_ssd, new["ssd_f"], new["ssd_b"] = ssd_branch(
            proj, (y_ssd, new["ssd_f"], new["ssd_b"]), T_SAMPLE, BATCH, SEQ, *ssd_args, None, l)

        qg = na_q_g[l][None, :]
        kg = na_k_g[l][None, :]
        bias, valid = _na_tables(na_rpb[l])
        y_na = neighbourhood_attention(proj, cache_k, cache_v, l, qg, kg, cos, sin, bias, valid)
        y_na, new["k"], new["v"] = context_attention(proj, (y_na, new["k"], new["v"]), T_SAMPLE, qg, kg, l)

        merged = merge_branches(xn, (y_lru, y_ret, y_ssd, y_na), w_gate, b_gate[l][None, :], w_branch_b, l)
        if l == 0:
            x = residual_projection(merged, w_out_b, l, xs0, mod, 2, 0, N_SAMPLE_TILES)
            x = residual_projection(merged, w_out_b, l, xp0, mod, 2, N_SAMPLE_TILES, n_ptiles, x_tile0=0, buf=x)
        else:
            x = residual_projection(merged, w_out_b, l, x, mod, 2)
        hmid = ffn_up(x, norm2_g[l][None, :], mod, ffn_w_up, l, ffn_conv_w[l], ffn_conv_b[l][None, :])
        if l < DEPTH - 1:
            x = residual_projection(hmid, w_down_b, l, x, mod, 5)
        else:
            y_sample = residual_projection(hmid, w_down_b, l, x, mod, 5, 0, N_SAMPLE_TILES, out_rows=T_SAMPLE)
            y_prompt = residual_projection(hmid, w_down_b, l, x, mod, 5, N_SAMPLE_TILES, n_ptiles,
                                           out_tile0=0, out_rows=T_PROMPT)

    kv_shape = (BATCH, DEPTH, SEQ, NA_HEADS, NA_HD)
    return (y_prompt.reshape(BATCH, SEQ, D_MODEL), y_sample.reshape(DEC_BATCH, DEC_SEQ, D_MODEL),
            new["k"].reshape(kv_shape), new["v"].reshape(kv_shape),
            new["lru_f"].reshape(BATCH, DEPTH, D_RNN), new["lru_b"].reshape(BATCH, DEPTH, D_RNN),
            new["ret_f"], new["ret_b"], new["ssd_f"], new["ssd_b"])
```

```python
import functools
import math

import jax
import jax.numpy as jnp
import numpy as np
from jax import lax
from jax.experimental import pallas as pl
from jax.experimental.pallas import tpu as pltpu

D_MODEL = 2048
BATCH = 16
SEQ = 256
DEPTH = 2
DEC_BATCH = 8
DEC_SEQ = 1024
PAST_LEN = 256
GRID_W = 64
EPS = 1e-6
N_BRANCH = 4
BRANCH_W = 1024
N_MOD = 6
D_RNN = 1024
LRU_BLOCKS = 8
LRU_BS = D_RNN // LRU_BLOCKS
LRU_CONV = 4
LRU_C = 8.0
RET_HEADS = 4
RET_DK = 128
RET_DV = 256
SSD_HEADS = 16
SSD_P = 64
SSD_N = 128
SSD_GROUPS = 2
SSD_CONV = 4
D_SSD = SSD_HEADS * SSD_P
SSD_CONV_CH = D_SSD + 2 * SSD_GROUPS * SSD_N
NA_HEADS = 8
NA_HD = 128
NA_W = NA_HEADS * NA_HD
NA_WR = 8
NA_WC = 16
ROPE_BASE = 10000.0
D_FF = 5632
FFN_CONV = 3
IN_SIZES = (D_RNN, D_RNN,
            RET_HEADS * RET_DK, RET_HEADS * RET_DK, RET_HEADS * RET_DV, RET_HEADS * RET_DV,
            D_SSD, SSD_CONV_CH, 2 * SSD_HEADS,
            NA_W, NA_W, NA_W)
D_IN = sum(IN_SIZES)
IN_OFFSETS = tuple(int(s) for s in np.cumsum(IN_SIZES)[:-1])

V7X_LANES = 128
V7X_SUBLANES = 8
V7X_VMEM_LIMIT_BYTES = 56 * 1024 * 1024

TM = 1024
T_SAMPLE = DEC_BATCH * DEC_SEQ
T_PROMPT = BATCH * SEQ
T_ALL = T_SAMPLE + T_PROMPT
N_SAMPLE_TILES = T_SAMPLE // TM
N_TILES = T_ALL // TM
N_COND = DEC_BATCH + 1
N_COND_PAD = 16

SSD_HPG = SSD_HEADS // SSD_GROUPS
OFF_LRU_X = 0
OFF_LRU_G = OFF_LRU_X + D_RNN
OFF_RET_Q = OFF_LRU_G + D_RNN
OFF_RET_K = OFF_RET_Q + RET_HEADS * RET_DK
OFF_RET_V = OFF_RET_K + RET_HEADS * RET_DK
OFF_RET_G = OFF_RET_V + RET_HEADS * RET_DV
OFF_SSD_Z = OFF_RET_G + RET_HEADS * RET_DV
OFF_SSD_XBC = OFF_SSD_Z + D_SSD
OFF_NA_Q = OFF_SSD_XBC + SSD_CONV_CH
OFF_NA_K = OFF_NA_Q + NA_W
OFF_NA_V = OFF_NA_K + NA_W
OFF_SSD_DT = OFF_NA_V + NA_W
PROJ_TN = 1024
D_IN_PAD = -(-(OFF_SSD_DT + SSD_GROUPS * V7X_LANES) // PROJ_TN) * PROJ_TN
N_HEAD_TILES = OFF_NA_Q // PROJ_TN
HEAD_COLS = N_HEAD_TILES * PROJ_TN

RET_CHUNK = 256
SSD_CHUNK = 256
SSD_TB = 128

f32 = jnp.float32
bf16 = jnp.bfloat16

_ARB = "arbitrary"


def _params(n_axes):
    return pltpu.CompilerParams(dimension_semantics=(_ARB,) * n_axes,
                                vmem_limit_bytes=V7X_VMEM_LIMIT_BYTES)


def _mod_spec(k, tile0=0):
    return pl.BlockSpec((1, 1, D_MODEL), lambda i, j: (jnp.minimum(tile0 + i, N_SAMPLE_TILES), 0, k))


def _dot(a, b):
    return jnp.dot(a, b, preferred_element_type=f32)


def _dot_nt(a, b):
    return lax.dot_general(a, b, (((1,), (1,)), ((), ())), preferred_element_type=f32)


def _sigmoid(x):
    return 0.5 * jnp.tanh(0.5 * x) + 0.5


def _silu(x):
    return x * _sigmoid(x)


def _gelu_tanh(x):
    return 0.5 * x * (1.0 + jnp.tanh(math.sqrt(2.0 / math.pi) * (x + 0.044715 * (x * x * x))))


def _shift_rows(v, k):
    n = v.shape[0]
    row = lax.broadcasted_iota(jnp.int32, (n, 1), 0)
    if k == 1:
        return jnp.where(row == 0, 0.0, pltpu.roll(v, 1, axis=0))
    return jnp.where(row == n - 1, 0.0, pltpu.roll(v, n - 1, axis=0))


def _dwconv4(x, cw, cb):
    acc = _shift_rows(x * cw[0:1, :], 1) + x * cw[1:2, :]
    return cb + x * cw[2:3, :] + _shift_rows(acc, 1) + _shift_rows(x * cw[3:4, :], -1)


def _softplus(x):
    return jnp.maximum(x, 0.0) + jnp.log1p(jnp.exp(-jnp.abs(x)))


def _drop_refs(body, n, *refs):
    body(*refs[n:])


def _inplace_call(body, bufs, grid, in_specs, args, out_specs, out_shape, scratch, name):
    held = [(k, b) for k, b in enumerate(bufs) if b is not None]
    if held:
        body = functools.partial(_drop_refs, body, len(held))
        in_specs = [pl.BlockSpec(memory_space=pl.ANY)] * len(held) + list(in_specs)
        args = [b for _, b in held] + list(args)
    aliases = {pos: k for pos, (k, _) in enumerate(held)}
    return pl.pallas_call(
        body, grid=grid, in_specs=in_specs, out_specs=out_specs, out_shape=out_shape,
        scratch_shapes=scratch, input_output_aliases=aliases,
        compiler_params=_params(len(grid)), name=name,
    )(*args)


_BRANCH_BUF = jax.ShapeDtypeStruct((T_ALL, BRANCH_W), bf16)


ADA_TN = 1024


def _ada_kernel(c_ref, w_ref, b_ref, o_ref):
    c = _silu(c_ref[...]).astype(bf16)
    o_ref[0] = _dot(c, w_ref[0].astype(bf16)) + b_ref[0]


def ada_modulation(cond, w_ada, b_ada):
    n = N_MOD * D_MODEL
    return pl.pallas_call(
        _ada_kernel,
        grid=(DEPTH, n // ADA_TN),
        in_specs=[pl.BlockSpec((N_COND_PAD, D_MODEL), lambda l, j: (0, 0)),
                  pl.BlockSpec((1, D_MODEL, ADA_TN), lambda l, j: (l, 0, j)),
                  pl.BlockSpec((1, 1, ADA_TN), lambda l, j: (l, 0, j))],
        out_specs=pl.BlockSpec((1, N_COND_PAD, ADA_TN), lambda l, j: (l, 0, j)),
        out_shape=jax.ShapeDtypeStruct((DEPTH, N_COND_PAD, n), f32),
        compiler_params=_params(2),
        name="ada_modulation",
    )(cond, w_ada, b_ada.reshape(DEPTH, 1, n))


NORM_ROWS = 128


def _modulated_norm(x_ref, g_ref, sc_ref, sh_ref, xn_ref):
    g = g_ref[...]
    sc = 1.0 + sc_ref[0]
    sh = sh_ref[0]

    def body(r, carry):
        rows = pl.ds(pl.multiple_of(r * NORM_ROWS, NORM_ROWS), NORM_ROWS)
        x = x_ref[rows, :]
        y = x * lax.rsqrt(jnp.mean(x * x, axis=-1, keepdims=True) + EPS)
        xn_ref[rows, :] = ((y * g) * sc + sh).astype(bf16)
        return carry

    lax.fori_loop(0, TM // NORM_ROWS, body, 0)


def _in_proj_kernel(x_ref, g_ref, sc_ref, sh_ref, wh_ref, wt_ref, o_ref, xn_ref):
    j = pl.program_id(1)

    @pl.when(j == 0)
    def _():
        _modulated_norm(x_ref, g_ref, sc_ref, sh_ref, xn_ref)

    @pl.when(j < N_HEAD_TILES)
    def _():
        o_ref[...] = _dot(xn_ref[...], wh_ref[0]).astype(bf16)

    @pl.when(j >= N_HEAD_TILES)
    def _():
        o_ref[...] = _dot(xn_ref[...], wt_ref[0]).astype(bf16)


def in_projection(x, x_tile0, tile0, n_tiles, g, mod, w_head, w_tail, layer, bufs=(None, None)):
    n = D_IN_PAD
    n_tail = w_tail.shape[2] // PROJ_TN
    return _inplace_call(
        _in_proj_kernel, bufs, (n_tiles, n // PROJ_TN),
        [pl.BlockSpec((TM, D_MODEL), lambda i, j: (x_tile0 + i, 0)),
         pl.BlockSpec((1, D_MODEL), lambda i, j: (0, 0)),
         _mod_spec(1, tile0), _mod_spec(0, tile0),
         pl.BlockSpec((1, D_MODEL, PROJ_TN), lambda i, j: (layer, 0, jnp.minimum(j, N_HEAD_TILES - 1))),
         pl.BlockSpec((1, D_MODEL, PROJ_TN),
                      lambda i, j: (layer, 0, jnp.clip(j - N_HEAD_TILES, 0, n_tail - 1)))],
        [x, g, mod, mod, w_head, w_tail],
        [pl.BlockSpec((TM, PROJ_TN), lambda i, j: (tile0 + i, j)),
         pl.BlockSpec((TM, D_MODEL), lambda i, j: (tile0 + i, 0))],
        [jax.ShapeDtypeStruct((T_ALL, n), bf16), jax.ShapeDtypeStruct((T_ALL, D_MODEL), bf16)],
        [], "in_projection")


MERGE_TN = 256


def _merge_kernel(xn_ref, *refs):
    y_refs, wg_refs, bg_refs = refs[0:4], refs[4:8], refs[8:12]
    wb_ref, o_ref = refs[12], refs[13]
    xn = xn_ref[...]
    acc = None
    for n in range(N_BRANCH):
        gate = _sigmoid(_dot(xn, wg_refs[n][0].astype(bf16)) + bg_refs[n][...])
        term = gate * _dot(y_refs[n][...], wb_ref[0, n])
        acc = term if acc is None else acc + term
    o_ref[...] = acc.astype(bf16)


def merge_branches(xn, ys, w_gate, b_gate, w_branch, layer):
    nj = D_MODEL // MERGE_TN
    y_spec = pl.BlockSpec((TM, BRANCH_W), lambda i, j: (i, 0))
    wg_specs = [pl.BlockSpec((1, D_MODEL, MERGE_TN), lambda i, j, n=n: (layer, 0, n * nj + j))
                for n in range(N_BRANCH)]
    bg_specs = [pl.BlockSpec((1, MERGE_TN), lambda i, j, n=n: (0, n * nj + j)) for n in range(N_BRANCH)]
    return pl.pallas_call(
        _merge_kernel,
        grid=(N_TILES, nj),
        in_specs=[pl.BlockSpec((TM, D_MODEL), lambda i, j: (i, 0)),
                  y_spec, y_spec, y_spec, y_spec, *wg_specs, *bg_specs,
                  pl.BlockSpec((1, N_BRANCH, BRANCH_W, MERGE_TN), lambda i, j: (layer, 0, 0, j))],
        out_specs=pl.BlockSpec((TM, MERGE_TN), lambda i, j: (i, j)),
        out_shape=jax.ShapeDtypeStruct((T_ALL, D_MODEL), bf16),
        compiler_params=_params(2),
        name="merge_branches",
    )(xn, *ys, *([w_gate] * N_BRANCH), *([b_gate] * N_BRANCH), w_branch)


RES_TN = 512
RES_TN_SHORT_K = 1024


def _residual_kernel(a_ref, w_ref, x_ref, gv_ref, o_ref):
    o_ref[...] = x_ref[...] + gv_ref[0] * _dot(a_ref[...], w_ref[0])


def residual_projection(a, w, layer, x, mod, k_mod, tile0=0, n_tiles=N_TILES, x_tile0=None, out_tile0=None,
                        out_rows=T_ALL, buf=None):
    kdim = a.shape[1]
    tn = RES_TN_SHORT_K if kdim == D_MODEL else RES_TN
    x_tile0 = tile0 if x_tile0 is None else x_tile0
    out_tile0 = tile0 if out_tile0 is None else out_tile0
    return _inplace_call(
        _residual_kernel, (buf,), (n_tiles, D_MODEL // tn),
        [pl.BlockSpec((TM, kdim), lambda i, j: (tile0 + i, 0)),
         pl.BlockSpec((1, kdim, tn), lambda i, j: (layer, 0, j)),
         pl.BlockSpec((TM, tn), lambda i, j: (x_tile0 + i, j)),
         pl.BlockSpec((1, 1, tn),
                      lambda i, j: (jnp.minimum(tile0 + i, N_SAMPLE_TILES), 0, k_mod * (D_MODEL // tn) + j))],
        [a, w, x, mod],
        pl.BlockSpec((TM, tn), lambda i, j: (out_tile0 + i, j)),
        jax.ShapeDtypeStruct((out_rows, D_MODEL), f32), [], "residual_projection")


FFN_TN = 512


def _ffn_up_kernel(x_ref, g_ref, sc_ref, sh_ref, wa_ref, wv_ref, cw_ref, cb_ref, o_ref, xn_ref, pad_ref):
    i = pl.program_id(0)

    @pl.when(pl.program_id(1) == 0)
    def _():
        _modulated_norm(x_ref, g_ref, sc_ref, sh_ref, xn_ref)
        zeros = jnp.zeros((V7X_SUBLANES, FFN_TN), f32)
        pad_ref[pl.ds(0, V7X_SUBLANES), :] = zeros
        pad_ref[pl.ds(V7X_SUBLANES + TM, V7X_SUBLANES), :] = zeros

    xn = xn_ref[...]
    a = _dot(xn, wa_ref[0].astype(bf16))
    pad_ref[pl.ds(V7X_SUBLANES, TM), :] = a
    seq_len = jnp.where(i < N_SAMPLE_TILES, DEC_SEQ, SEQ)
    pos = lax.broadcasted_iota(jnp.int32, (TM, 1), 0) & (seq_len - 1)
    prev = jnp.where(pos == 0, 0.0, pad_ref[pl.ds(V7X_SUBLANES - 1, TM), :])
    nxt = jnp.where(pos == seq_len - 1, 0.0, pad_ref[pl.ds(V7X_SUBLANES + 1, TM), :])
    cw = cw_ref[...]
    conv = cb_ref[...] + prev * cw[0:1, :] + a * cw[1:2, :] + nxt * cw[2:3, :]
    o_ref[...] = (_gelu_tanh(conv) * _dot(xn, wv_ref[0].astype(bf16))).astype(bf16)


def ffn_up(x, g, mod, w_up, layer, conv_w, conv_b):
    nj = D_FF // FFN_TN
    return pl.pallas_call(
        _ffn_up_kernel,
        grid=(N_TILES, nj),
        in_specs=[pl.BlockSpec((TM, D_MODEL), lambda i, j: (i, 0)),
                  pl.BlockSpec((1, D_MODEL), lambda i, j: (0, 0)),
                  _mod_spec(4), _mod_spec(3),
                  pl.BlockSpec((1, D_MODEL, FFN_TN), lambda i, j: (layer, 0, j)),
                  pl.BlockSpec((1, D_MODEL, FFN_TN), lambda i, j: (layer, 0, nj + j)),
                  pl.BlockSpec((FFN_CONV, FFN_TN), lambda i, j: (0, j)),
                  pl.BlockSpec((1, FFN_TN), lambda i, j: (0, j))],
        out_specs=pl.BlockSpec((TM, FFN_TN), lambda i, j: (i, j)),
        out_shape=jax.ShapeDtypeStruct((T_ALL, D_FF), bf16),
        scratch_shapes=[pltpu.VMEM((TM, D_MODEL), bf16),
                        pltpu.VMEM((TM + 2 * V7X_SUBLANES, FFN_TN), f32)],
        compiler_params=_params(2),
        name="ffn_up",
    )(x, g, mod, mod, w_up, w_up, conv_w, conv_b)


LRU_CB_LONG = D_RNN
LRU_CB_SHORT = D_RNN


def _lru_kernel(*refs, seq_len, has_init, cbw):
    if has_init:
        (x_ref, g_ref, cw_ref, cb_ref, w4_ref, b4_ref, lam_ref, h0f_ref, h0b_ref,
         y_ref, af_ref, uf_ref, ab_ref, ub_ref, hf_ref, hb_ref) = refs
    else:
        (x_ref, g_ref, cw_ref, cb_ref, w4_ref, b4_ref, lam_ref,
         y_ref, sf_ref, sb_ref, af_ref, uf_ref, ab_ref, ub_ref, hf_ref, hb_ref) = refs
    L = seq_len
    S = V7X_SUBLANES
    xc = _dwconv4(x_ref[...].astype(f32), cw_ref[...], cb_ref[...])
    sp = _softplus(-lam_ref[...])
    for n in range(cbw // LRU_BS):
        cols = slice(n * LRU_BS, (n + 1) * LRU_BS)
        xcn = xc[:, cols]
        z = _dot(xcn.astype(bf16), w4_ref[n]) + b4_ref[n]
        for d, (a_ref, u_ref) in enumerate(((af_ref, uf_ref), (ab_ref, ub_ref))):
            r = _sigmoid(z[:, (2 * d) * LRU_BS:(2 * d + 1) * LRU_BS])
            ig = _sigmoid(z[:, (2 * d + 1) * LRU_BS:(2 * d + 2) * LRU_BS])
            log_a = (-LRU_C) * r * sp[d:d + 1, cols]
            a = jnp.exp(log_a)
            a_ref[:, cols] = a
            gain2 = -jnp.tanh(log_a) * (a * a + 1.0)
            gain = jnp.where(gain2 > 0.0, gain2 * lax.rsqrt(gain2), 0.0)
            u_ref[:, cols] = gain * (ig * xcn)

    if has_init:
        h0f = h0f_ref[0, 0]
        h0b = h0b_ref[0, 0]
    else:
        h0f = jnp.zeros((1, cbw), f32)
        h0b = jnp.zeros((1, cbw), f32)

    def step(i, carry):
        hf, hb = carry
        base_f = pl.multiple_of(i * S, S)
        base_b = pl.multiple_of(L - S - i * S, S)
        for r in range(S):
            tf = pl.ds(base_f + r, 1)
            tb = pl.ds(base_b + (S - 1 - r), 1)
            hf = af_ref[tf, :] * hf + uf_ref[tf, :]
            hb = ab_ref[tb, :] * hb + ub_ref[tb, :]
            hf_ref[tf, :] = hf
            hb_ref[tb, :] = hb
        return hf, hb

    hf, hb = lax.fori_loop(0, L // S, step, (h0f, h0b))
    if not has_init:
        sf_ref[0, 0] = hf
        sb_ref[0, 0] = hb
    y_ref[...] = ((hf_ref[...] + hb_ref[...]) * _gelu_tanh(g_ref[...].astype(f32))).astype(bf16)


def lru_branch(proj, bufs, row0, n_seq, seq_len, cw, cb, w4, b4, lam, init, layer):
    has_init = init is not None
    cbw = LRU_CB_LONG if seq_len == DEC_SEQ else LRU_CB_SHORT
    ncb = D_RNN // cbw
    nb = cbw // LRU_BS
    rb0 = row0 // seq_len
    in_specs = [pl.BlockSpec((seq_len, cbw), lambda s, c: (rb0 + s, OFF_LRU_X // cbw + c)),
                pl.BlockSpec((seq_len, cbw), lambda s, c: (rb0 + s, OFF_LRU_G // cbw + c)),
                pl.BlockSpec((LRU_CONV, cbw), lambda s, c: (0, c)),
                pl.BlockSpec((1, cbw), lambda s, c: (0, c)),
                pl.BlockSpec((nb, LRU_BS, 4 * LRU_BS), lambda s, c: (c, 0, 0)),
                pl.BlockSpec((nb, 1, 4 * LRU_BS), lambda s, c: (c, 0, 0)),
                pl.BlockSpec((2, cbw), lambda s, c: (0, c))]
    args = [proj, proj, cw, cb, w4, b4, lam]
    state_spec = pl.BlockSpec((1, 1, 1, cbw), lambda s, c: (s, layer, 0, c))
    y_spec = pl.BlockSpec((seq_len, cbw), lambda s, c: (rb0 + s, c))
    if has_init:
        in_specs += [state_spec, state_spec]
        args += [init[0], init[1]]
        out_specs, out_shape = y_spec, _BRANCH_BUF
    else:
        st = jax.ShapeDtypeStruct((n_seq, DEPTH, 1, D_RNN), f32)
        out_specs, out_shape = [y_spec, state_spec, state_spec], [_BRANCH_BUF, st, st]
    scratch = [pltpu.VMEM((seq_len, cbw), f32) for _ in range(6)]
    return _inplace_call(functools.partial(_lru_kernel, seq_len=seq_len, has_init=has_init, cbw=cbw), bufs,
                         (n_seq, ncb), in_specs, args, out_specs, out_shape, scratch, "lru_branch")


def _ret_kernel(*refs, seq_len, has_init):
    if has_init:
        (la_ref, q_ref, k_ref, v_ref, g_ref, gn_ref, s0f_ref, s0b_ref, y_ref, acc_ref) = refs
    else:
        (la_ref, q_ref, k_ref, v_ref, g_ref, gn_ref, y_ref, sf_ref, sb_ref, acc_ref) = refs
    T = min(RET_CHUNK, seq_len)
    nc = seq_len // T
    tt = lax.broadcasted_iota(jnp.int32, (T, T), 0)
    ss = lax.broadcasted_iota(jnp.int32, (T, T), 1)
    diff = (tt - ss).astype(f32)
    tcol = lax.broadcasted_iota(jnp.int32, (T, 1), 0).astype(f32)
    scale = RET_DK ** -0.5
    for h in range(RET_HEADS):
        la_f = la_ref[h, 0]
        la_b = la_ref[h, 1]
        kcols = pl.ds(h * RET_DK, RET_DK)
        vcols = pl.ds(h * RET_DV, RET_DV)
        dsum = (jnp.where(tt >= ss, jnp.exp(la_f * diff), 0.0)
                + jnp.where(ss >= tt, jnp.exp(-la_b * diff), 0.0))

        def chunk(c):
            rows = pl.ds(c * T, T)
            q = q_ref[rows, kcols].astype(f32)
            ks = k_ref[rows, kcols].astype(f32) * scale
            v = v_ref[rows, vcols].astype(f32)
            return rows, q, ks, v

        s_f = s0f_ref[0, 0, h] if has_init else None
        for c in range(nc):
            rows, q, ks, v = chunk(c)
            scores = _dot_nt(q.astype(bf16), ks.astype(bf16)) * dsum
            y = _dot(scores.astype(bf16), v.astype(bf16))
            if s_f is not None:
                y = y + _dot((q * jnp.exp(la_f * (tcol + 1.0))).astype(bf16), s_f.astype(bf16))
            acc_ref[rows, :] = y
            if c < nc - 1 or not has_init:
                upd = _dot(ks.T.astype(bf16), (v * jnp.exp(la_f * (T - 1.0 - tcol))).astype(bf16))
                s_f = upd if s_f is None else jnp.exp(la_f * T) * s_f + upd
        s_b = s0b_ref[0, 0, h] if has_init else None
        for c in reversed(range(nc)):
            rows, q, ks, v = chunk(c)
            if s_b is not None:
                acc_ref[rows, :] += _dot((q * jnp.exp(la_b * (T - tcol))).astype(bf16), s_b.astype(bf16))
            if c > 0 or not has_init:
                upd = _dot(ks.T.astype(bf16), (v * jnp.exp(la_b * tcol)).astype(bf16))
                s_b = upd if s_b is None else jnp.exp(la_b * T) * s_b + upd
        if not has_init:
            sf_ref[0, 0, h] = s_f
            sb_ref[0, 0, h] = s_b
        y = acc_ref[...]
        mu = jnp.mean(y, axis=-1, keepdims=True)
        yc = y - mu
        var = jnp.mean(yc * yc, axis=-1, keepdims=True)
        y = yc * lax.rsqrt(var + EPS) * gn_ref[:, vcols]
        y_ref[:, vcols] = (y * _silu(g_ref[:, vcols].astype(f32))).astype(bf16)


def retention_branch(proj, bufs, row0, n_seq, seq_len, la, gn, init, layer):
    has_init = init is not None
    rb0 = row0 // seq_len
    qk_w, vg_w = RET_HEADS * RET_DK, RET_HEADS * RET_DV
    in_specs = [pl.BlockSpec(memory_space=pltpu.SMEM),
                pl.BlockSpec((seq_len, qk_w), lambda s: (rb0 + s, OFF_RET_Q // qk_w)),
                pl.BlockSpec((seq_len, qk_w), lambda s: (rb0 + s, OFF_RET_K // qk_w)),
                pl.BlockSpec((seq_len, vg_w), lambda s: (rb0 + s, OFF_RET_V // vg_w)),
                pl.BlockSpec((seq_len, vg_w), lambda s: (rb0 + s, OFF_RET_G // vg_w)),
                pl.BlockSpec((1, vg_w), lambda s: (0, 0))]
    args = [la, proj, proj, proj, proj, gn]
    state_spec = pl.BlockSpec((1, 1, RET_HEADS, RET_DK, RET_DV), lambda s: (s, layer, 0, 0, 0))
    y_spec = pl.BlockSpec((seq_len, vg_w), lambda s: (rb0 + s, 0))
    if has_init:
        in_specs += [state_spec, state_spec]
        args += [init[0], init[1]]
        out_specs, out_shape = y_spec, _BRANCH_BUF
    else:
        st = jax.ShapeDtypeStruct((n_seq, DEPTH, RET_HEADS, RET_DK, RET_DV), f32)
        out_specs, out_shape = [y_spec, state_spec, state_spec], [_BRANCH_BUF, st, st]
    return _inplace_call(functools.partial(_ret_kernel, seq_len=seq_len, has_init=has_init), bufs,
                         (n_seq,), in_specs, args, out_specs, out_shape,
                         [pltpu.VMEM((seq_len, RET_DV), f32)], "retention_branch")


SSD_GW = SSD_HPG * SSD_P


def _split3(x):
    h1 = x.astype(bf16)
    r1 = x - h1.astype(f32)
    h2 = r1.astype(bf16)
    h3 = (r1 - h2.astype(f32)).astype(bf16)
    return h1, h2, h3


def _dot_exact_rhs(m, x):
    h1, h2, h3 = _split3(x)
    return _dot(m, h1) + _dot(m, h2) + _dot(m, h3)


def _dot_exact_lhs(x, m):
    h1, h2, h3 = _split3(x)
    return _dot(h1, m) + _dot(h2, m) + _dot(h3, m)


def _heads_to_lanes(s_ref, st_ref):
    for hh in range(SSD_HPG):
        st_ref[:, pl.ds(hh * SSD_P, SSD_P)] = s_ref[0, 0, hh]
    return st_ref[...]


def _ssd_kernel(*refs, seq_len, has_init):
    if has_init:
        (z_ref, x_ref, b_ref, c_ref, dt_ref, cwx_ref, cbx_ref, cwb_ref, cbb_ref, cwc_ref, cbc_ref,
         prm_ref, dvec_ref, ng_ref, s0f_ref, s0b_ref,
         y_ref, xs_ref, bs_ref, cs_ref, acc_ref, rb_ref, yn_ref, st_ref) = refs
    else:
        (z_ref, x_ref, b_ref, c_ref, dt_ref, cwx_ref, cbx_ref, cwb_ref, cbb_ref, cwc_ref, cbc_ref,
         prm_ref, dvec_ref, ng_ref,
         y_ref, sf_ref, sb_ref, xs_ref, bs_ref, cs_ref, acc_ref, rb_ref, yn_ref, st_ref) = refs
    g = pl.program_id(1)
    L = seq_len
    T = min(SSD_CHUNK, L)
    nc = L // T
    H = SSD_HPG
    xs_ref[...] = _silu(_dwconv4(x_ref[...].astype(f32), cwx_ref[...], cbx_ref[...]))
    bs_ref[...] = _silu(_dwconv4(b_ref[...].astype(f32), cwb_ref[...], cbb_ref[...]))
    cs_ref[...] = _silu(_dwconv4(c_ref[...].astype(f32), cwc_ref[...], cbc_ref[...]))
    prm = prm_ref[0]
    a_neg = -jnp.exp(prm[1:2, :])
    tt = lax.broadcasted_iota(jnp.int32, (T, T), 0)
    ss = lax.broadcasted_iota(jnp.int32, (T, T), 1)
    lower = tt >= ss
    upper = ss >= tt
    tri_l = jnp.where(lower, 1.0, 0.0).astype(bf16)
    tri_u = jnp.where(upper, 1.0, 0.0).astype(bf16)
    lower_b = lower[:SSD_TB, :SSD_TB]
    upper_b = upper[:SSD_TB, :SSD_TB]
    er = lax.broadcasted_iota(jnp.int32, (V7X_LANES, SSD_GW), 0)
    ec = lax.broadcasted_iota(jnp.int32, (V7X_LANES, SSD_GW), 1) // SSD_P
    exp_f = jnp.where(er == ec, 1.0, 0.0).astype(bf16)
    exp_b = jnp.where(er == ec + H, 1.0, 0.0).astype(bf16)

    def expand(w, e):
        hi = w.astype(bf16)
        lo = (w - hi.astype(f32)).astype(bf16)
        return _dot(hi, e) + _dot(lo, e)

    def chunk_terms(c):
        rows = pl.ds(c * T, T)
        dt = _softplus(dt_ref[rows, :].astype(f32) + prm[0:1, :])
        da = dt * a_neg
        return rows, dt, da

    s_f = _heads_to_lanes(s0f_ref, st_ref) if has_init else None
    for c in range(nc):
        rows, dt, da = chunk_terms(c)
        cum = _dot_exact_rhs(tri_l, da)
        rsum = _dot_exact_rhs(tri_u, da)
        rb_ref[rows, :] = rsum
        da_t = da.T
        cum_t = _dot_exact_lhs(da_t, tri_u)
        rsum_t = _dot_exact_lhs(da_t, tri_l)
        dt_t = dt.T
        bmat = bs_ref[rows, :]
        cmat = cs_ref[rows, :]
        xmat = xs_ref[rows, :]
        gmat = _dot_nt(cmat.astype(bf16), bmat.astype(bf16))
        for hh in range(H):
            cf, cf_t, dtf_t = cum[:, hh:hh + 1], cum_t[hh:hh + 1, :], dt_t[hh:hh + 1, :]
            rb, rb_t, dtb_t = rsum[:, H + hh:H + hh + 1], rsum_t[H + hh:H + hh + 1, :], dt_t[H + hh:H + hh + 1, :]
            block_rows = []
            for bi in range(T // SSD_TB):
                rr = slice(bi * SSD_TB, (bi + 1) * SSD_TB)
                blocks = []
                for bj in range(T // SSD_TB):
                    cc = slice(bj * SSD_TB, (bj + 1) * SSD_TB)
                    if bi > bj:
                        blk = jnp.exp(cf[rr] - cf_t[:, cc]) * dtf_t[:, cc]
                    elif bi < bj:
                        blk = jnp.exp(rb[rr] - rb_t[:, cc]) * dtb_t[:, cc]
                    else:
                        blk = (jnp.where(lower_b, jnp.exp(cf[rr] - cf_t[:, cc]), 0.0) * dtf_t[:, cc]
                               + jnp.where(upper_b, jnp.exp(rb[rr] - rb_t[:, cc]), 0.0) * dtb_t[:, cc])
                    blocks.append(gmat[rr, cc] * blk)
                block_rows.append(jnp.concatenate(blocks, axis=1))
            m = jnp.concatenate(block_rows, axis=0)
            xh = xmat[:, hh * SSD_P:(hh + 1) * SSD_P]
            acc_ref[rows, pl.ds(hh * SSD_P, SSD_P)] = _dot(m.astype(bf16), xh.astype(bf16))
        ecum = jnp.exp(cum)
        if s_f is not None:
            acc_ref[rows, :] += _dot(cmat.astype(bf16), s_f.astype(bf16)) * expand(ecum, exp_f)
        if c < nc - 1 or not has_init:
            tail = jnp.exp(cum[T - 1:T, :] - cum) * dt
            xw = (xmat * expand(tail, exp_f)).astype(bf16)
            upd = _dot(bmat.T.astype(bf16), xw)
            if s_f is None:
                s_f = upd
            else:
                s_f = s_f * expand(jnp.broadcast_to(ecum[T - 1:T, :], (V7X_SUBLANES, V7X_LANES)),
                                   exp_f)[0:1, :] + upd
    s_b = _heads_to_lanes(s0b_ref, st_ref) if has_init else None
    for c in reversed(range(nc)):
        rows, dt, da = chunk_terms(c)
        rsum = rb_ref[rows, :]
        ers = jnp.exp(rsum)
        bmat = bs_ref[rows, :]
        cmat = cs_ref[rows, :]
        xmat = xs_ref[rows, :]
        if s_b is not None:
            acc_ref[rows, :] += _dot(cmat.astype(bf16), s_b.astype(bf16)) * expand(ers, exp_b)
        if c > 0 or not has_init:
            tail = jnp.exp(rsum[0:1, :] - rsum) * dt
            xw = (xmat * expand(tail, exp_b)).astype(bf16)
            upd = _dot(bmat.T.astype(bf16), xw)
            if s_b is None:
                s_b = upd
            else:
                s_b = s_b * expand(jnp.broadcast_to(ers[0:1, :], (V7X_SUBLANES, V7X_LANES)),
                                   exp_b)[0:1, :] + upd
    if not has_init:
        for hh in range(H):
            sf_ref[0, 0, hh] = s_f[:, hh * SSD_P:(hh + 1) * SSD_P]
            sb_ref[0, 0, hh] = s_b[:, hh * SSD_P:(hh + 1) * SSD_P]
    yg = (acc_ref[...] + xs_ref[...] * dvec_ref[...]) * _silu(z_ref[...].astype(f32))
    yn_ref[g] = yg

    @pl.when(g == SSD_GROUPS - 1)
    def _():
        ssq = None
        for gg in range(SSD_GROUPS):
            y = yn_ref[gg]
            s = jnp.sum(y * y, axis=-1, keepdims=True)
            ssq = s if ssq is None else ssq + s
        inv = lax.rsqrt(ssq * (1.0 / D_SSD) + EPS)
        for gg in range(SSD_GROUPS):
            cols = pl.ds(gg * SSD_GW, SSD_GW)
            y_ref[:, cols] = (yn_ref[gg] * inv * ng_ref[:, cols]).astype(bf16)


def ssd_branch(proj, bufs, row0, n_seq, seq_len, conv_w, conv_b, prm, dvec, ng, init, layer):
    has_init = init is not None
    rb0 = row0 // seq_len
    xoff = OFF_SSD_XBC
    boff = OFF_SSD_XBC + D_SSD
    coff = boff + SSD_GROUPS * SSD_N
    in_specs = [pl.BlockSpec((seq_len, SSD_GW), lambda s, g: (rb0 + s, OFF_SSD_Z // SSD_GW + g)),
                pl.BlockSpec((seq_len, SSD_GW), lambda s, g: (rb0 + s, xoff // SSD_GW + g)),
                pl.BlockSpec((seq_len, SSD_N), lambda s, g: (rb0 + s, boff // SSD_N + g)),
                pl.BlockSpec((seq_len, SSD_N), lambda s, g: (rb0 + s, coff // SSD_N + g)),
                pl.BlockSpec((seq_len, V7X_LANES), lambda s, g: (rb0 + s, OFF_SSD_DT // V7X_LANES + g)),
                pl.BlockSpec((SSD_CONV, SSD_GW), lambda s, g: (0, g)),
                pl.BlockSpec((1, SSD_GW), lambda s, g: (0, g)),
                pl.BlockSpec((SSD_CONV, SSD_N), lambda s, g: (0, D_SSD // SSD_N + g)),
                pl.BlockSpec((1, SSD_N), lambda s, g: (0, D_SSD // SSD_N + g)),
                pl.BlockSpec((SSD_CONV, SSD_N), lambda s, g: (0, D_SSD // SSD_N + SSD_GROUPS + g)),
                pl.BlockSpec((1, SSD_N), lambda s, g: (0, D_SSD // SSD_N + SSD_GROUPS + g)),
                pl.BlockSpec((1, V7X_SUBLANES, V7X_LANES), lambda s, g: (g, 0, 0)),
                pl.BlockSpec((1, SSD_GW), lambda s, g: (0, g)),
                pl.BlockSpec((1, D_SSD), lambda s, g: (0, 0))]
    args = [proj, proj, proj, proj, proj, conv_w, conv_b, conv_w, conv_b, conv_w, conv_b, prm, dvec, ng]
    state_spec = pl.BlockSpec((1, 1, SSD_HPG, SSD_N, SSD_P), lambda s, g: (s, layer, g, 0, 0))
    y_spec = pl.BlockSpec((seq_len, D_SSD), lambda s, g: (rb0 + s, 0))
    if has_init:
        in_specs += [state_spec, state_spec]
        args += [init[0], init[1]]
        out_specs, out_shape = y_spec, _BRANCH_BUF
    else:
        st = jax.ShapeDtypeStruct((n_seq, DEPTH, SSD_HEADS, SSD_N, SSD_P), f32)
        out_specs, out_shape = [y_spec, state_spec, state_spec], [_BRANCH_BUF, st, st]
    scratch = [pltpu.VMEM((seq_len, SSD_GW), f32),
               pltpu.VMEM((seq_len, SSD_N), f32),
               pltpu.VMEM((seq_len, SSD_N), f32),
               pltpu.VMEM((seq_len, SSD_GW), f32),
               pltpu.VMEM((seq_len, V7X_LANES), f32),
               pltpu.VMEM((SSD_GROUPS, seq_len, SSD_GW), f32),
               pltpu.VMEM((SSD_N, SSD_GW), f32)]
    return _inplace_call(functools.partial(_ssd_kernel, seq_len=seq_len, has_init=has_init), bufs,
                         (n_seq, SSD_GROUPS), in_specs, args, out_specs, out_shape, scratch, "ssd_branch")


def _head_rmsnorm(x, g):
    return x * lax.rsqrt(jnp.mean(x * x, axis=-1, keepdims=True) + EPS) * g


CTX_HB = 4
CTX_W = CTX_HB * NA_HD


def _ctx_attn_kernel(q_ref, k_ref, v_ref, qg_ref, kg_ref, y_ref, ko_ref, vo_ref):
    scale = NA_HD ** -0.5
    vo_ref[0, 0] = v_ref[...].astype(f32)
    for h in range(CTX_HB):
        cols = pl.ds(h * NA_HD, NA_HD)
        q = _head_rmsnorm(q_ref[:, cols].astype(f32), qg_ref[...])
        k = _head_rmsnorm(k_ref[:, cols].astype(f32), kg_ref[...])
        ko_ref[0, 0, :, cols] = k
        s = _dot_nt(q.astype(bf16), k.astype(bf16)) * scale
        p = jnp.exp(s - jnp.max(s, axis=-1, keepdims=True))
        o = _dot(p.astype(bf16), v_ref[:, cols]) / jnp.sum(p, axis=-1, keepdims=True)
        y_ref[:, cols] = o.astype(bf16)


def context_attention(proj, bufs, row0, qg, kg, layer):
    rb0 = row0 // SEQ
    spec = lambda off: pl.BlockSpec((SEQ, CTX_W), lambda s, hb: (rb0 + s, off // CTX_W + hb))
    gspec = pl.BlockSpec((1, NA_HD), lambda s, hb: (0, 0))
    kv_spec = pl.BlockSpec((1, 1, SEQ, CTX_W), lambda s, hb: (s, layer, 0, hb))
    kv_shape = jax.ShapeDtypeStruct((BATCH, DEPTH, SEQ, NA_W), f32)
    return _inplace_call(
        _ctx_attn_kernel, bufs, (BATCH, NA_HEADS // CTX_HB),
        [spec(OFF_NA_Q), spec(OFF_NA_K), spec(OFF_NA_V), gspec, gspec], [proj, proj, proj, qg, kg],
        [pl.BlockSpec((SEQ, CTX_W), lambda s, hb: (rb0 + s, hb)), kv_spec, kv_spec],
        [_BRANCH_BUF, kv_shape, kv_shape], [], "context_attention")


NA_ROWS = DEC_SEQ // GRID_W
NA_NK = NA_WR * GRID_W


def _na_row_start(r):
    return min(max(r - NA_WR // 2, 0), NA_ROWS - NA_WR)


def _rope(x, cos, sin_signed):
    lane = lax.broadcasted_iota(jnp.int32, x.shape, 1)
    quarter = NA_HD // 4
    swapped = jnp.where((lane & (2 * quarter - 1)) < quarter,
                        pltpu.roll(x, NA_HD - quarter, axis=1), pltpu.roll(x, quarter, axis=1))
    return x * cos + swapped * sin_signed


def _na_row_groups():
    groups, r = [], 0
    while r < NA_ROWS:
        n = 1
        while r + n < NA_ROWS and _na_row_start(r + n) == _na_row_start(r):
            n += 1
        groups.append((r, n, _na_row_start(r)))
        r += n
    return groups


def _na_attn_kernel(q_ref, k_ref, v_ref, kc_ref, vc_ref, qg_ref, kg_ref, cos_ref, sin_ref,
                    bias_ref, valid_ref, y_ref, qs_ref, ks_ref, s_ref, p_ref, oc_ref, w_ref):
    scale = NA_HD ** -0.5
    cos = cos_ref[...]
    sin = sin_ref[...]
    qs_ref[...] = _rope(_head_rmsnorm(q_ref[...].astype(f32), qg_ref[...]), cos, sin).astype(bf16)
    ks_ref[...] = _rope(_head_rmsnorm(k_ref[...].astype(f32), kg_ref[...]), cos, sin).astype(bf16)
    groups = _na_row_groups()
    valid = valid_ref[...] > 0.0
    for r0, n, rs in groups:
        rows = pl.ds(r0 * GRID_W, n * GRID_W)
        d0 = r0 - rs
        kw = ks_ref[pl.ds(rs * GRID_W, NA_NK), :]
        s = (_dot_nt(qs_ref[rows, :], kw) * scale).reshape(n, GRID_W, NA_NK) + bias_ref[0, d0:d0 + n]
        s_ref[rows, :] = jnp.where(valid[None], s, -1e30).reshape(n * GRID_W, NA_NK)
    s_ctx = _dot_nt(qs_ref[...], kc_ref[0, 0].astype(bf16)) * scale
    s_loc = s_ref[...]
    m = jnp.maximum(jnp.max(s_loc, axis=-1, keepdims=True), jnp.max(s_ctx, axis=-1, keepdims=True))
    p_loc = jnp.exp(s_loc - m)
    p_ctx = jnp.exp(s_ctx - m)
    w_ref[...] = 1.0 / (jnp.sum(p_loc, axis=-1, keepdims=True) + jnp.sum(p_ctx, axis=-1, keepdims=True))
    p_ref[...] = p_loc.astype(bf16)
    oc_ref[...] = _dot(p_ctx.astype(bf16), vc_ref[0, 0].astype(bf16))
    for r0, n, rs in groups:
        rows = pl.ds(r0 * GRID_W, n * GRID_W)
        vw = v_ref[pl.ds(rs * GRID_W, NA_NK), :]
        y_ref[rows, :] = ((_dot(p_ref[rows, :], vw) + oc_ref[rows, :]) * w_ref[rows, :]).astype(bf16)


def neighbourhood_attention(proj, cache_k, cache_v, layer, qg, kg, cos, sin, bias, valid):
    spec = lambda off: pl.BlockSpec((DEC_SEQ, NA_HD), lambda b, h: (b, off // NA_HD + h))
    cspec = pl.BlockSpec((1, 1, PAST_LEN, NA_HD), lambda b, h: (b, layer, 0, h))
    gspec = pl.BlockSpec((1, NA_HD), lambda b, h: (0, 0))
    tspec = pl.BlockSpec((DEC_SEQ, NA_HD), lambda b, h: (0, 0))
    return pl.pallas_call(
        _na_attn_kernel,
        grid=(DEC_BATCH, NA_HEADS),
        in_specs=[spec(OFF_NA_Q), spec(OFF_NA_K), spec(OFF_NA_V), cspec, cspec, gspec, gspec,
                  tspec, tspec,
                  pl.BlockSpec((1, NA_WR, GRID_W, NA_NK), lambda b, h: (h, 0, 0, 0)),
                  pl.BlockSpec((GRID_W, NA_NK), lambda b, h: (0, 0))],
        out_specs=pl.BlockSpec((DEC_SEQ, NA_HD), lambda b, h: (b, h)),
        out_shape=_BRANCH_BUF,
        scratch_shapes=[pltpu.VMEM((DEC_SEQ, NA_HD), bf16), pltpu.VMEM((DEC_SEQ, NA_HD), bf16),
                        pltpu.VMEM((DEC_SEQ, NA_NK), f32), pltpu.VMEM((DEC_SEQ, NA_NK), bf16),
                        pltpu.VMEM((DEC_SEQ, NA_HD), f32), pltpu.VMEM((DEC_SEQ, 1), f32)],
        compiler_params=_params(2), name="neighbourhood_attention",
    )(proj, proj, proj, cache_k, cache_v, qg, kg, cos, sin, bias, valid)


TAIL_TN = 512
N_TAIL_TILES = (D_IN_PAD - HEAD_COLS) // TAIL_TN
N_QKV_TILES = 3 * NA_W // TAIL_TN
DT_W = 2 * SSD_HEADS


def _tail_kernel(a_ref, b_ref, o_ref):
    j = pl.program_id(1)
    lane = lax.broadcasted_iota(jnp.int32, (D_MODEL, TAIL_TN), 1)

    @pl.when(j == 0)
    def _():
        o_ref[0] = a_ref[0]

    @pl.when((j >= 1) & (j <= N_QKV_TILES))
    def _():
        main = pltpu.roll(a_ref[0].astype(f32), TAIL_TN - DT_W, axis=1)
        nxt = pltpu.roll(b_ref[0].astype(f32), V7X_LANES - DT_W, axis=1)
        last_lane = lax.broadcasted_iota(jnp.int32, (D_MODEL, V7X_LANES), 1)
        last = jnp.where(last_lane < V7X_LANES - DT_W, main[:, TAIL_TN - V7X_LANES:], nxt)
        o_ref[0] = jnp.concatenate([main[:, :TAIL_TN - V7X_LANES], last], axis=1).astype(bf16)

    @pl.when(j == N_TAIL_TILES - 1)
    def _():
        a = a_ref[0].astype(f32)
        out = jnp.zeros((D_MODEL, TAIL_TN), f32)
        for gidx in range(SSD_GROUPS):
            base = gidx * V7X_LANES
            for src0, dst0 in ((gidx * SSD_HPG, base), (SSD_HEADS + gidx * SSD_HPG, base + SSD_HPG)):
                shift = (dst0 - src0) % TAIL_TN
                moved = a if shift == 0 else pltpu.roll(a, shift, axis=1)
                out = jnp.where((lane >= dst0) & (lane < dst0 + SSD_HPG), moved, out)
        o_ref[0] = out.astype(bf16)


def _split_w_in(w_in):
    w_in = w_in.astype(bf16)
    dt0 = IN_OFFSETS[7]
    a_tile0, dt_tile, b_tile0 = HEAD_COLS // TAIL_TN, dt0 // TAIL_TN, dt0 // V7X_LANES
    b_per_a = TAIL_TN // V7X_LANES
    last_b = (D_IN - 1) // V7X_LANES
    tail = pl.pallas_call(
        _tail_kernel, grid=(DEPTH, N_TAIL_TILES),
        in_specs=[pl.BlockSpec((1, D_MODEL, TAIL_TN),
                               lambda l, j: (l, 0, jnp.where(j == N_TAIL_TILES - 1, dt_tile, a_tile0 + j))),
                  pl.BlockSpec((1, D_MODEL, V7X_LANES),
                               lambda l, j: (l, 0, jnp.minimum(b_tile0 + b_per_a * j, last_b)))],
        out_specs=pl.BlockSpec((1, D_MODEL, TAIL_TN), lambda l, j: (l, 0, j)),
        out_shape=jax.ShapeDtypeStruct((DEPTH, D_MODEL, D_IN_PAD - HEAD_COLS), bf16),
        compiler_params=_params(2), name="build_w_in_tail",
    )(w_in, w_in)
    return w_in, tail


def _group_lanes(v):
    rows = []
    for gidx in range(SSD_GROUPS):
        sl = slice(gidx * SSD_HPG, (gidx + 1) * SSD_HPG)
        rows.append(jnp.concatenate([v[0, sl], v[1, sl], jnp.zeros((V7X_LANES - 2 * SSD_HPG,), f32)]))
    return jnp.stack(rows)


def _rope_tables():
    t = np.arange(DEC_SEQ)
    quarter = NA_HD // 4
    inv = ROPE_BASE ** (-np.arange(quarter, dtype=np.float32) / quarter)
    ang_r = (t // GRID_W).astype(np.float32)[:, None] * inv
    ang_c = (t % GRID_W).astype(np.float32)[:, None] * inv
    cos = np.concatenate([np.cos(ang_r), np.cos(ang_r), np.cos(ang_c), np.cos(ang_c)], axis=1)
    sin = np.concatenate([-np.sin(ang_r), np.sin(ang_r), -np.sin(ang_c), np.sin(ang_c)], axis=1)
    return jnp.asarray(cos, f32), jnp.asarray(sin, f32)


def _na_tables(rpb):
    cq = np.arange(GRID_W)
    kc = np.tile(np.arange(GRID_W), NA_WR)
    col_start = np.clip(cq - NA_WC // 2, 0, GRID_W - NA_WC)
    valid = (kc[None, :] >= col_start[:, None]) & (kc[None, :] < col_start[:, None] + NA_WC)
    col_off = np.clip(cq[None, :] - cq[:, None], 1 - NA_WC, NA_WC - 1) + NA_WC - 1
    onehot = (col_off[None, :, :] == np.arange(2 * NA_WC - 1)[:, None, None]).astype(np.float32)
    toep = jnp.einsum('hic,cqk->hiqk', rpb.astype(f32), jnp.asarray(onehot), precision=lax.Precision.HIGHEST)
    tables = []
    for d in range(NA_WR):
        rows = toep[:, NA_WR - 1 - d:2 * NA_WR - 1 - d]
        tables.append(rows.transpose(0, 2, 1, 3).reshape(NA_HEADS, GRID_W, NA_NK))
    return jnp.stack(tables, axis=1), jnp.asarray(valid, f32)


def kernel(x_prompt, x_sample, cache_na_k, cache_na_v, state_lru_f, state_lru_b,
           state_ret_f, state_ret_b, state_ssd_f, state_ssd_b, c, c_ctx,
           norm1_g, norm2_g, w_ada, b_ada, w_in, w_gate, b_gate, w_branch, w_out,
           lru_conv_w, lru_conv_b, lru_wa, lru_ba, lru_wx, lru_bx, lru_lambda,
           ret_gn_g, ssd_conv_w, ssd_conv_b, ssd_a_log, ssd_dt_bias, ssd_d, ssd_norm_g,
           na_q_g, na_k_g, na_rpb, ffn_w_up, ffn_conv_w, ffn_conv_b, ffn_w_down):
    xs0 = x_sample.reshape(T_SAMPLE, D_MODEL)
    xp0 = x_prompt.reshape(T_PROMPT, D_MODEL)
    n_ptiles = N_TILES - N_SAMPLE_TILES
    cond = jnp.concatenate([c, c_ctx[None, :], jnp.zeros((N_COND_PAD - N_COND, D_MODEL), f32)], axis=0)
    mod_all = ada_modulation(cond, w_ada, b_ada)
    cos, sin = _rope_tables()
    hh = jnp.arange(RET_HEADS, dtype=f32)
    ret_la = jnp.stack([jnp.log1p(-jnp.exp2(-5.0 - hh)), jnp.log1p(-jnp.exp2(-5.5 - hh))], axis=1)
    cache_k = cache_na_k.reshape(DEC_BATCH, DEPTH, PAST_LEN, NA_W)
    cache_v = cache_na_v.reshape(DEC_BATCH, DEPTH, PAST_LEN, NA_W)
    lru_init = (state_lru_f.reshape(DEC_BATCH, DEPTH, 1, D_RNN), state_lru_b.reshape(DEC_BATCH, DEPTH, 1, D_RNN))

    w_head_b, w_tail_b = _split_w_in(w_in)
    w_branch_b, w_out_b, w_down_b = w_branch.astype(bf16), w_out.astype(bf16), ffn_w_down.astype(bf16)

    new = {k: None for k in ("k", "v", "lru_f", "lru_b", "ret_f", "ret_b", "ssd_f", "ssd_b")}
    for l in range(DEPTH):
        mod = mod_all[l].reshape(N_COND_PAD, 1, N_MOD * D_MODEL)
        g1 = norm1_g[l][None, :]
        if l == 0:
            proj, xn = in_projection(xs0, 0, 0, N_SAMPLE_TILES, g1, mod, w_head_b, w_tail_b, l)
            proj, xn = in_projection(xp0, 0, N_SAMPLE_TILES, n_ptiles, g1, mod, w_head_b, w_tail_b, l, (proj, xn))
        else:
            proj, xn = in_projection(x, 0, 0, N_TILES, g1, mod, w_head_b, w_tail_b, l)

        w4 = jnp.concatenate([lru_wa[l, 0], lru_wx[l, 0], lru_wa[l, 1], lru_wx[l, 1]], axis=-1).astype(bf16)
        b4 = jnp.concatenate([lru_ba[l, 0].reshape(LRU_BLOCKS, 1, LRU_BS), lru_bx[l, 0].reshape(LRU_BLOCKS, 1, LRU_BS),
                              lru_ba[l, 1].reshape(LRU_BLOCKS, 1, LRU_BS), lru_bx[l, 1].reshape(LRU_BLOCKS, 1, LRU_BS)],
                             axis=-1)
        lru_args = (lru_conv_w[l], lru_conv_b[l][None, :], w4, b4, lru_lambda[l])
        y_lru = lru_branch(proj, (None,), 0, DEC_BATCH, DEC_SEQ, *lru_args, lru_init, l)
        y_lru, new["lru_f"], new["lru_b"] = lru_branch(
            proj, (y_lru, new["lru_f"], new["lru_b"]), T_SAMPLE, BATCH, SEQ, *lru_args, None, l)

        gn = ret_gn_g[l][None, :]
        y_ret = retention_branch(proj, (None,), 0, DEC_BATCH, DEC_SEQ, ret_la, gn, (state_ret_f, state_ret_b), l)
        y_ret, new["ret_f"], new["ret_b"] = retention_branch(
            proj, (y_ret, new["ret_f"], new["ret_b"]), T_SAMPLE, BATCH, SEQ, ret_la, gn, None, l)

        prm = jnp.stack([_group_lanes(ssd_dt_bias[l]), _group_lanes(ssd_a_log[l])], axis=1)
        prm = jnp.concatenate([prm, jnp.zeros((SSD_GROUPS, V7X_SUBLANES - 2, V7X_LANES), f32)], axis=1)
        dvec = jnp.repeat(ssd_d[l], SSD_P)[None, :]
        ssd_args = (ssd_conv_w[l], ssd_conv_b[l][None, :], prm, dvec, ssd_norm_g[l][None, :])
        y_ssd = ssd_branch(proj, (None,), 0, DEC_BATCH, DEC_SEQ, *ssd_args, (state_ssd_f, state_ssd_b), l)
        y_ssd, new["ssd_f"], new["ssd_b"] = ssd_branch(
            proj, (y_ssd, new["ssd_f"], new["ssd_b"]), T_SAMPLE, BATCH, SEQ, *ssd_args, None, l)

        qg = na_q_g[l][None, :]
        kg = na_k_g[l][None, :]
        bias, valid = _na_tables(na_rpb[l])
        y_na = neighbourhood_attention(proj, cache_k, cache_v, l, qg, kg, cos, sin, bias, valid)
        y_na, new["k"], new["v"] = context_attention(proj, (y_na, new["k"], new["v"]), T_SAMPLE, qg, kg, l)

        merged = merge_branches(xn, (y_lru, y_ret, y_ssd, y_na), w_gate, b_gate[l][None, :], w_branch_b, l)
        if l == 0:
            x = residual_projection(merged, w_out_b, l, xs0, mod, 2, 0, N_SAMPLE_TILES)
            x = residual_projection(merged, w_out_b, l, xp0, mod, 2, N_SAMPLE_TILES, n_ptiles, x_tile0=0, buf=x)
        else:
            x = residual_projection(merged, w_out_b, l, x, mod, 2)
        hmid = ffn_up(x, norm2_g[l][None, :], mod, ffn_w_up, l, ffn_conv_w[l], ffn_conv_b[l][None, :])
        if l < DEPTH - 1:
            x = residual_projection(hmid, w_down_b, l, x, mod, 5)
        else:
            y_sample = residual_projection(hmid, w_down_b, l, x, mod, 5, 0, N_SAMPLE_TILES, out_rows=T_SAMPLE)
            y_prompt = residual_projection(hmid, w_down_b, l, x, mod, 5, N_SAMPLE_TILES, n_ptiles,
                                           out_tile0=0, out_rows=T_PROMPT)

    kv_shape = (BATCH, DEPTH, SEQ, NA_HEADS, NA_HD)
    return (y_prompt.reshape(BATCH, SEQ, D_MODEL), y_sample.reshape(DEC_BATCH, DEC_SEQ, D_MODEL),
            new["k"].reshape(kv_shape), new["v"].reshape(kv_shape),
            new["lru_f"].reshape(BATCH, DEPTH, D_RNN), new["lru_b"].reshape(BATCH, DEPTH, D_RNN),
            new["ret_f"], new["ret_b"], new["ssd_f"], new["ssd_b"])
```

```python
import functools
import math

import jax
import jax.numpy as jnp
import numpy as np
from jax import lax
from jax.experimental import pallas as pl
from jax.experimental.pallas import tpu as pltpu

D_MODEL = 2048
BATCH = 16
SEQ = 256
DEPTH = 2
DEC_BATCH = 8
DEC_SEQ = 1024
PAST_LEN = 256
GRID_W = 64
EPS = 1e-6
N_BRANCH = 4
BRANCH_W = 1024
N_MOD = 6
D_RNN = 1024
LRU_BLOCKS = 8
LRU_BS = D_RNN // LRU_BLOCKS
LRU_CONV = 4
LRU_C = 8.0
RET_HEADS = 4
RET_DK = 128
RET_DV = 256
SSD_HEADS = 16
SSD_P = 64
SSD_N = 128
SSD_GROUPS = 2
SSD_CONV = 4
D_SSD = SSD_HEADS * SSD_P
SSD_CONV_CH = D_SSD + 2 * SSD_GROUPS * SSD_N
NA_HEADS = 8
NA_HD = 128
NA_W = NA_HEADS * NA_HD
NA_WR = 8
NA_WC = 16
ROPE_BASE = 10000.0
D_FF = 5632
FFN_CONV = 3
IN_SIZES = (D_RNN, D_RNN,
            RET_HEADS * RET_DK, RET_HEADS * RET_DK, RET_HEADS * RET_DV, RET_HEADS * RET_DV,
            D_SSD, SSD_CONV_CH, 2 * SSD_HEADS,
            NA_W, NA_W, NA_W)
D_IN = sum(IN_SIZES)
IN_OFFSETS = tuple(int(s) for s in np.cumsum(IN_SIZES)[:-1])

V7X_LANES = 128
V7X_SUBLANES = 8
V7X_VMEM_LIMIT_BYTES = 56 * 1024 * 1024

TM = 1024
T_SAMPLE = DEC_BATCH * DEC_SEQ
T_PROMPT = BATCH * SEQ
T_ALL = T_SAMPLE + T_PROMPT
N_SAMPLE_TILES = T_SAMPLE // TM
N_TILES = T_ALL // TM
N_COND = DEC_BATCH + 1
N_COND_PAD = 16

SSD_HPG = SSD_HEADS // SSD_GROUPS
OFF_LRU_X = 0
OFF_LRU_G = OFF_LRU_X + D_RNN
OFF_RET_Q = OFF_LRU_G + D_RNN
OFF_RET_K = OFF_RET_Q + RET_HEADS * RET_DK
OFF_RET_V = OFF_RET_K + RET_HEADS * RET_DK
OFF_RET_G = OFF_RET_V + RET_HEADS * RET_DV
OFF_SSD_Z = OFF_RET_G + RET_HEADS * RET_DV
OFF_SSD_XBC = OFF_SSD_Z + D_SSD
OFF_NA_Q = OFF_SSD_XBC + SSD_CONV_CH
OFF_NA_K = OFF_NA_Q + NA_W
OFF_NA_V = OFF_NA_K + NA_W
OFF_SSD_DT = OFF_NA_V + NA_W
PROJ_TN = 1024
D_IN_PAD = -(-(OFF_SSD_DT + SSD_GROUPS * V7X_LANES) // PROJ_TN) * PROJ_TN
N_HEAD_TILES = OFF_NA_Q // PROJ_TN
HEAD_COLS = N_HEAD_TILES * PROJ_TN

RET_CHUNK = 256
SSD_CHUNK = 256
SSD_TB = 128

f32 = jnp.float32
bf16 = jnp.bfloat16

_ARB = "arbitrary"


def _params(n_axes):
    return pltpu.CompilerParams(dimension_semantics=(_ARB,) * n_axes,
                                vmem_limit_bytes=V7X_VMEM_LIMIT_BYTES)


def _mod_spec(k, tile0=0):
    return pl.BlockSpec((1, 1, D_MODEL), lambda i, j: (jnp.minimum(tile0 + i, N_SAMPLE_TILES), 0, k))


def _dot(a, b):
    return jnp.dot(a, b, preferred_element_type=f32)


def _dot_nt(a, b):
    return lax.dot_general(a, b, (((1,), (1,)), ((), ())), preferred_element_type=f32)


def _sigmoid(x):
    return 0.5 * jnp.tanh(0.5 * x) + 0.5


def _silu(x):
    return x * _sigmoid(x)


def _gelu_tanh(x):
    return 0.5 * x * (1.0 + jnp.tanh(math.sqrt(2.0 / math.pi) * (x + 0.044715 * (x * x * x))))


def _shift_rows(v, k):
    n = v.shape[0]
    row = lax.broadcasted_iota(jnp.int32, (n, 1), 0)
    if k == 1:
        return jnp.where(row == 0, 0.0, pltpu.roll(v, 1, axis=0))
    return jnp.where(row == n - 1, 0.0, pltpu.roll(v, n - 1, axis=0))


def _dwconv4(x, cw, cb):
    acc = _shift_rows(x * cw[0:1, :], 1) + x * cw[1:2, :]
    return cb + x * cw[2:3, :] + _shift_rows(acc, 1) + _shift_rows(x * cw[3:4, :], -1)


def _softplus(x):
    return jnp.maximum(x, 0.0) + jnp.log1p(jnp.exp(-jnp.abs(x)))


def _drop_refs(body, n, *refs):
    body(*refs[n:])


def _inplace_call(body, bufs, grid, in_specs, args, out_specs, out_shape, scratch, name):
    held = [(k, b) for k, b in enumerate(bufs) if b is not None]
    if held:
        body = functools.partial(_drop_refs, body, len(held))
        in_specs = [pl.BlockSpec(memory_space=pl.ANY)] * len(held) + list(in_specs)
        args = [b for _, b in held] + list(args)
    aliases = {pos: k for pos, (k, _) in enumerate(held)}
    return pl.pallas_call(
        body, grid=grid, in_specs=in_specs, out_specs=out_specs, out_shape=out_shape,
        scratch_shapes=scratch, input_output_aliases=aliases,
        compiler_params=_params(len(grid)), name=name,
    )(*args)


_BRANCH_BUF = jax.ShapeDtypeStruct((T_ALL, BRANCH_W), bf16)


ADA_TN = 1024


def _ada_kernel(c_ref, w_ref, b_ref, o_ref):
    c = _silu(c_ref[...]).astype(bf16)
    o_ref[0] = _dot(c, w_ref[0].astype(bf16)) + b_ref[0]


def ada_modulation(cond, w_ada, b_ada):
    n = N_MOD * D_MODEL
    return pl.pallas_call(
        _ada_kernel,
        grid=(DEPTH, n // ADA_TN),
        in_specs=[pl.BlockSpec((N_COND_PAD, D_MODEL), lambda l, j: (0, 0)),
                  pl.BlockSpec((1, D_MODEL, ADA_TN), lambda l, j: (l, 0, j)),
                  pl.BlockSpec((1, 1, ADA_TN), lambda l, j: (l, 0, j))],
        out_specs=pl.BlockSpec((1, N_COND_PAD, ADA_TN), lambda l, j: (l, 0, j)),
        out_shape=jax.ShapeDtypeStruct((DEPTH, N_COND_PAD, n), f32),
        compiler_params=_params(2),
        name="ada_modulation",
    )(cond, w_ada, b_ada.reshape(DEPTH, 1, n))


NORM_ROWS = 128


def _modulated_norm(x_ref, g_ref, sc_ref, sh_ref, xn_ref):
    g = g_ref[...]
    sc = 1.0 + sc_ref[0]
    sh = sh_ref[0]

    def body(r, carry):
        rows = pl.ds(pl.multiple_of(r * NORM_ROWS, NORM_ROWS), NORM_ROWS)
        x = x_ref[rows, :]
        y = x * lax.rsqrt(jnp.mean(x * x, axis=-1, keepdims=True) + EPS)
        xn_ref[rows, :] = ((y * g) * sc + sh).astype(bf16)
        return carry

    lax.fori_loop(0, TM // NORM_ROWS, body, 0)


def _in_proj_kernel(x_ref, g_ref, sc_ref, sh_ref, wh_ref, wt_ref, o_ref, xn_ref):
    j = pl.program_id(1)

    @pl.when(j == 0)
    def _():
        _modulated_norm(x_ref, g_ref, sc_ref, sh_ref, xn_ref)

    @pl.when(j < N_HEAD_TILES)
    def _():
        o_ref[...] = _dot(xn_ref[...], wh_ref[0]).astype(bf16)

    @pl.when(j >= N_HEAD_TILES)
    def _():
        o_ref[...] = _dot(xn_ref[...], wt_ref[0]).astype(bf16)


def in_projection(x, x_tile0, tile0, n_tiles, g, mod, w_head, w_tail, layer, bufs=(None, None)):
    n = D_IN_PAD
    n_tail = w_tail.shape[2] // PROJ_TN
    return _inplace_call(
        _in_proj_kernel, bufs, (n_tiles, n // PROJ_TN),
        [pl.BlockSpec((TM, D_MODEL), lambda i, j: (x_tile0 + i, 0)),
         pl.BlockSpec((1, D_MODEL), lambda i, j: (0, 0)),
         _mod_spec(1, tile0), _mod_spec(0, tile0),
         pl.BlockSpec((1, D_MODEL, PROJ_TN), lambda i, j: (layer, 0, jnp.minimum(j, N_HEAD_TILES - 1))),
         pl.BlockSpec((1, D_MODEL, PROJ_TN),
                      lambda i, j: (layer, 0, jnp.clip(j - N_HEAD_TILES, 0, n_tail - 1)))],
        [x, g, mod, mod, w_head, w_tail],
        [pl.BlockSpec((TM, PROJ_TN), lambda i, j: (tile0 + i, j)),
         pl.BlockSpec((TM, D_MODEL), lambda i, j: (tile0 + i, 0))],
        [jax.ShapeDtypeStruct((T_ALL, n), bf16), jax.ShapeDtypeStruct((T_ALL, D_MODEL), bf16)],
        [], "in_projection")


MERGE_TN = 256


def _merge_kernel(xn_ref, *refs):
    y_refs, wg_refs, bg_refs = refs[0:4], refs[4:8], refs[8:12]
    wb_ref, o_ref = refs[12], refs[13]
    xn = xn_ref[...]
    acc = None
    for n in range(N_BRANCH):
        gate = _sigmoid(_dot(xn, wg_refs[n][0].astype(bf16)) + bg_refs[n][...])
        term = gate * _dot(y_refs[n][...], wb_ref[0, n])
        acc = term if acc is None else acc + term
    o_ref[...] = acc.astype(bf16)


def merge_branches(xn, ys, w_gate, b_gate, w_branch, layer):
    nj = D_MODEL // MERGE_TN
    y_spec = pl.BlockSpec((TM, BRANCH_W), lambda i, j: (i, 0))
    wg_specs = [pl.BlockSpec((1, D_MODEL, MERGE_TN), lambda i, j, n=n: (layer, 0, n * nj + j))
                for n in range(N_BRANCH)]
    bg_specs = [pl.BlockSpec((1, MERGE_TN), lambda i, j, n=n: (0, n * nj + j)) for n in range(N_BRANCH)]
    return pl.pallas_call(
        _merge_kernel,
        grid=(N_TILES, nj),
        in_specs=[pl.BlockSpec((TM, D_MODEL), lambda i, j: (i, 0)),
                  y_spec, y_spec, y_spec, y_spec, *wg_specs, *bg_specs,
                  pl.BlockSpec((1, N_BRANCH, BRANCH_W, MERGE_TN), lambda i, j: (layer, 0, 0, j))],
        out_specs=pl.BlockSpec((TM, MERGE_TN), lambda i, j: (i, j)),
        out_shape=jax.ShapeDtypeStruct((T_ALL, D_MODEL), bf16),
        compiler_params=_params(2),
        name="merge_branches",
    )(xn, *ys, *([w_gate] * N_BRANCH), *([b_gate] * N_BRANCH), w_branch)


RES_TN = 512
RES_TN_SHORT_K = 1024


def _residual_kernel(a_ref, w_ref, x_ref, gv_ref, o_ref):
    o_ref[...] = x_ref[...] + gv_ref[0] * _dot(a_ref[...], w_ref[0])


def residual_projection(a, w, layer, x, mod, k_mod, tile0=0, n_tiles=N_TILES, x_tile0=None, out_tile0=None,
                        out_rows=T_ALL, buf=None):
    kdim = a.shape[1]
    tn = RES_TN_SHORT_K if kdim == D_MODEL else RES_TN
    x_tile0 = tile0 if x_tile0 is None else x_tile0
    out_tile0 = tile0 if out_tile0 is None else out_tile0
    return _inplace_call(
        _residual_kernel, (buf,), (n_tiles, D_MODEL // tn),
        [pl.BlockSpec((TM, kdim), lambda i, j: (tile0 + i, 0)),
         pl.BlockSpec((1, kdim, tn), lambda i, j: (layer, 0, j)),
         pl.BlockSpec((TM, tn), lambda i, j: (x_tile0 + i, j)),
         pl.BlockSpec((1, 1, tn),
                      lambda i, j: (jnp.minimum(tile0 + i, N_SAMPLE_TILES), 0, k_mod * (D_MODEL // tn) + j))],
        [a, w, x, mod],
        pl.BlockSpec((TM, tn), lambda i, j: (out_tile0 + i, j)),
        jax.ShapeDtypeStruct((out_rows, D_MODEL), f32), [], "residual_projection")


FFN_TN = 512


def _ffn_up_kernel(x_ref, g_ref, sc_ref, sh_ref, wa_ref, wv_ref, cw_ref, cb_ref, o_ref, xn_ref):
    i = pl.program_id(0)

    @pl.when(pl.program_id(1) == 0)
    def _():
        _modulated_norm(x_ref, g_ref, sc_ref, sh_ref, xn_ref)

    xn = xn_ref[...]
    a = _dot(xn, wa_ref[0].astype(bf16))
    seq_len = jnp.where(i < N_SAMPLE_TILES, DEC_SEQ, SEQ)
    pos = lax.broadcasted_iota(jnp.int32, (TM, 1), 0) & (seq_len - 1)
    prev = jnp.where(pos == 0, 0.0, pltpu.roll(a, 1, axis=0))
    nxt = jnp.where(pos == seq_len - 1, 0.0, pltpu.roll(a, TM - 1, axis=0))
    cw = cw_ref[...]
    conv = cb_ref[...] + prev * cw[0:1, :] + a * cw[1:2, :] + nxt * cw[2:3, :]
    o_ref[...] = (_gelu_tanh(conv) * _dot(xn, wv_ref[0].astype(bf16))).astype(bf16)


def ffn_up(x, g, mod, w_up, layer, conv_w, conv_b):
    nj = D_FF // FFN_TN
    return pl.pallas_call(
        _ffn_up_kernel,
        grid=(N_TILES, nj),
        in_specs=[pl.BlockSpec((TM, D_MODEL), lambda i, j: (i, 0)),
                  pl.BlockSpec((1, D_MODEL), lambda i, j: (0, 0)),
                  _mod_spec(4), _mod_spec(3),
                  pl.BlockSpec((1, D_MODEL, FFN_TN), lambda i, j: (layer, 0, j)),
                  pl.BlockSpec((1, D_MODEL, FFN_TN), lambda i, j: (layer, 0, nj + j)),
                  pl.BlockSpec((FFN_CONV, FFN_TN), lambda i, j: (0, j)),
                  pl.BlockSpec((1, FFN_TN), lambda i, j: (0, j))],
        out_specs=pl.BlockSpec((TM, FFN_TN), lambda i, j: (i, j)),
        out_shape=jax.ShapeDtypeStruct((T_ALL, D_FF), bf16),
        scratch_shapes=[pltpu.VMEM((TM, D_MODEL), bf16)],
        compiler_params=_params(2),
        name="ffn_up",
    )(x, g, mod, mod, w_up, w_up, conv_w, conv_b)


LRU_CB_LONG = D_RNN
LRU_CB_SHORT = D_RNN


def _lru_kernel(*refs, seq_len, has_init, cbw):
    if has_init:
        (x_ref, g_ref, cw_ref, cb_ref, w4_ref, b4_ref, lam_ref, h0f_ref, h0b_ref,
         y_ref, af_ref, uf_ref, ab_ref, ub_ref, hf_ref, hb_ref) = refs
    else:
        (x_ref, g_ref, cw_ref, cb_ref, w4_ref, b4_ref, lam_ref,
         y_ref, sf_ref, sb_ref, af_ref, uf_ref, ab_ref, ub_ref, hf_ref, hb_ref) = refs
    L = seq_len
    S = V7X_SUBLANES
    xc = _dwconv4(x_ref[...].astype(f32), cw_ref[...], cb_ref[...])
    sp = _softplus(-lam_ref[...])
    for n in range(cbw // LRU_BS):
        cols = slice(n * LRU_BS, (n + 1) * LRU_BS)
        xcn = xc[:, cols]
        z = _dot(xcn.astype(bf16), w4_ref[n]) + b4_ref[n]
        for d, (a_ref, u_ref) in enumerate(((af_ref, uf_ref), (ab_ref, ub_ref))):
            r = _sigmoid(z[:, (2 * d) * LRU_BS:(2 * d + 1) * LRU_BS])
            ig = _sigmoid(z[:, (2 * d + 1) * LRU_BS:(2 * d + 2) * LRU_BS])
            log_a = (-LRU_C) * r * sp[d:d + 1, cols]
            a = jnp.exp(log_a)
            a_ref[:, cols] = a
            gain2 = -jnp.tanh(log_a) * (a * a + 1.0)
            gain = jnp.where(gain2 > 0.0, gain2 * lax.rsqrt(gain2), 0.0)
            u_ref[:, cols] = gain * (ig * xcn)

    if has_init:
        h0f = h0f_ref[0, 0]
        h0b = h0b_ref[0, 0]
    else:
        h0f = jnp.zeros((1, cbw), f32)
        h0b = jnp.zeros((1, cbw), f32)

    def step(i, carry):
        hf, hb = carry
        base_f = pl.multiple_of(i * S, S)
        base_b = pl.multiple_of(L - S - i * S, S)
        for r in range(S):
            tf = pl.ds(base_f + r, 1)
            tb = pl.ds(base_b + (S - 1 - r), 1)
            hf = af_ref[tf, :] * hf + uf_ref[tf, :]
            hb = ab_ref[tb, :] * hb + ub_ref[tb, :]
            hf_ref[tf, :] = hf
            hb_ref[tb, :] = hb
        return hf, hb

    hf, hb = lax.fori_loop(0, L // S, step, (h0f, h0b))
    if not has_init:
        sf_ref[0, 0] = hf
        sb_ref[0, 0] = hb
    y_ref[...] = ((hf_ref[...] + hb_ref[...]) * _gelu_tanh(g_ref[...].astype(f32))).astype(bf16)


def lru_branch(proj, bufs, row0, n_seq, seq_len, cw, cb, w4, b4, lam, init, layer):
    has_init = init is not None
    cbw = LRU_CB_LONG if seq_len == DEC_SEQ else LRU_CB_SHORT
    ncb = D_RNN // cbw
    nb = cbw // LRU_BS
    rb0 = row0 // seq_len
    in_specs = [pl.BlockSpec((seq_len, cbw), lambda s, c: (rb0 + s, OFF_LRU_X // cbw + c)),
                pl.BlockSpec((seq_len, cbw), lambda s, c: (rb0 + s, OFF_LRU_G // cbw + c)),
                pl.BlockSpec((LRU_CONV, cbw), lambda s, c: (0, c)),
                pl.BlockSpec((1, cbw), lambda s, c: (0, c)),
                pl.BlockSpec((nb, LRU_BS, 4 * LRU_BS), lambda s, c: (c, 0, 0)),
                pl.BlockSpec((nb, 1, 4 * LRU_BS), lambda s, c: (c, 0, 0)),
                pl.BlockSpec((2, cbw), lambda s, c: (0, c))]
    args = [proj, proj, cw, cb, w4, b4, lam]
    state_spec = pl.BlockSpec((1, 1, 1, cbw), lambda s, c: (s, layer, 0, c))
    y_spec = pl.BlockSpec((seq_len, cbw), lambda s, c: (rb0 + s, c))
    if has_init:
        in_specs += [state_spec, state_spec]
        args += [init[0], init[1]]
        out_specs, out_shape = y_spec, _BRANCH_BUF
    else:
        st = jax.ShapeDtypeStruct((n_seq, DEPTH, 1, D_RNN), f32)
        out_specs, out_shape = [y_spec, state_spec, state_spec], [_BRANCH_BUF, st, st]
    scratch = [pltpu.VMEM((seq_len, cbw), f32) for _ in range(6)]
    return _inplace_call(functools.partial(_lru_kernel, seq_len=seq_len, has_init=has_init, cbw=cbw), bufs,
                         (n_seq, ncb), in_specs, args, out_specs, out_shape, scratch, "lru_branch")


def _ret_kernel(*refs, seq_len, has_init):
    if has_init:
        (la_ref, q_ref, k_ref, v_ref, g_ref, gn_ref, s0f_ref, s0b_ref, y_ref, acc_ref) = refs
    else:
        (la_ref, q_ref, k_ref, v_ref, g_ref, gn_ref, y_ref, sf_ref, sb_ref, acc_ref) = refs
    T = min(RET_CHUNK, seq_len)
    nc = seq_len // T
    tt = lax.broadcasted_iota(jnp.int32, (T, T), 0)
    ss = lax.broadcasted_iota(jnp.int32, (T, T), 1)
    diff = (tt - ss).astype(f32)
    tcol = lax.broadcasted_iota(jnp.int32, (T, 1), 0).astype(f32)
    scale = RET_DK ** -0.5
    for h in range(RET_HEADS):
        la_f = la_ref[h, 0]
        la_b = la_ref[h, 1]
        kcols = pl.ds(h * RET_DK, RET_DK)
        vcols = pl.ds(h * RET_DV, RET_DV)
        dsum = (jnp.where(tt >= ss, jnp.exp(la_f * diff), 0.0)
                + jnp.where(ss >= tt, jnp.exp(-la_b * diff), 0.0))

        def chunk(c):
            rows = pl.ds(c * T, T)
            q = q_ref[rows, kcols].astype(f32)
            ks = k_ref[rows, kcols].astype(f32) * scale
            v = v_ref[rows, vcols].astype(f32)
            return rows, q, ks, v

        s_f = s0f_ref[0, 0, h] if has_init else None
        for c in range(nc):
            rows, q, ks, v = chunk(c)
            scores = _dot_nt(q.astype(bf16), ks.astype(bf16)) * dsum
            y = _dot(scores.astype(bf16), v.astype(bf16))
            if s_f is not None:
                y = y + _dot((q * jnp.exp(la_f * (tcol + 1.0))).astype(bf16), s_f.astype(bf16))
            acc_ref[rows, :] = y
            if c < nc - 1 or not has_init:
                upd = _dot(ks.T.astype(bf16), (v * jnp.exp(la_f * (T - 1.0 - tcol))).astype(bf16))
                s_f = upd if s_f is None else jnp.exp(la_f * T) * s_f + upd
        s_b = s0b_ref[0, 0, h] if has_init else None
        for c in reversed(range(nc)):
            rows, q, ks, v = chunk(c)
            if s_b is not None:
                acc_ref[rows, :] += _dot((q * jnp.exp(la_b * (T - tcol))).astype(bf16), s_b.astype(bf16))
            if c > 0 or not has_init:
                upd = _dot(ks.T.astype(bf16), (v * jnp.exp(la_b * tcol)).astype(bf16))
                s_b = upd if s_b is None else jnp.exp(la_b * T) * s_b + upd
        if not has_init:
            sf_ref[0, 0, h] = s_f
            sb_ref[0, 0, h] = s_b
        y = acc_ref[...]
        mu = jnp.mean(y, axis=-1, keepdims=True)
        yc = y - mu
        var = jnp.mean(yc * yc, axis=-1, keepdims=True)
        y = yc * lax.rsqrt(var + EPS) * gn_ref[:, vcols]
        y_ref[:, vcols] = (y * _silu(g_ref[:, vcols].astype(f32))).astype(bf16)


def retention_branch(proj, bufs, row0, n_seq, seq_len, la, gn, init, layer):
    has_init = init is not None
    rb0 = row0 // seq_len
    qk_w, vg_w = RET_HEADS * RET_DK, RET_HEADS * RET_DV
    in_specs = [pl.BlockSpec(memory_space=pltpu.SMEM),
                pl.BlockSpec((seq_len, qk_w), lambda s: (rb0 + s, OFF_RET_Q // qk_w)),
                pl.BlockSpec((seq_len, qk_w), lambda s: (rb0 + s, OFF_RET_K // qk_w)),
                pl.BlockSpec((seq_len, vg_w), lambda s: (rb0 + s, OFF_RET_V // vg_w)),
                pl.BlockSpec((seq_len, vg_w), lambda s: (rb0 + s, OFF_RET_G // vg_w)),
                pl.BlockSpec((1, vg_w), lambda s: (0, 0))]
    args = [la, proj, proj, proj, proj, gn]
    state_spec = pl.BlockSpec((1, 1, RET_HEADS, RET_DK, RET_DV), lambda s: (s, layer, 0, 0, 0))
    y_spec = pl.BlockSpec((seq_len, vg_w), lambda s: (rb0 + s, 0))
    if has_init:
        in_specs += [state_spec, state_spec]
        args += [init[0], init[1]]
        out_specs, out_shape = y_spec, _BRANCH_BUF
    else:
        st = jax.ShapeDtypeStruct((n_seq, DEPTH, RET_HEADS, RET_DK, RET_DV), f32)
        out_specs, out_shape = [y_spec, state_spec, state_spec], [_BRANCH_BUF, st, st]
    return _inplace_call(functools.partial(_ret_kernel, seq_len=seq_len, has_init=has_init), bufs,
                         (n_seq,), in_specs, args, out_specs, out_shape,
                         [pltpu.VMEM((seq_len, RET_DV), f32)], "retention_branch")


SSD_GW = SSD_HPG * SSD_P


def _split3(x):
    h1 = x.astype(bf16)
    r1 = x - h1.astype(f32)
    h2 = r1.astype(bf16)
    h3 = (r1 - h2.astype(f32)).astype(bf16)
    return h1, h2, h3


def _dot_exact_rhs(m, x):
    h1, h2, h3 = _split3(x)
    return _dot(m, h1) + _dot(m, h2) + _dot(m, h3)


def _dot_exact_lhs(x, m):
    h1, h2, h3 = _split3(x)
    return _dot(h1, m) + _dot(h2, m) + _dot(h3, m)


def _heads_to_lanes(s_ref, st_ref):
    for hh in range(SSD_HPG):
        st_ref[:, pl.ds(hh * SSD_P, SSD_P)] = s_ref[0, 0, hh]
    return st_ref[...]


def _ssd_kernel(*refs, seq_len, has_init):
    if has_init:
        (z_ref, x_ref, b_ref, c_ref, dt_ref, cwx_ref, cbx_ref, cwb_ref, cbb_ref, cwc_ref, cbc_ref,
         prm_ref, dvec_ref, ng_ref, s0f_ref, s0b_ref,
         y_ref, xs_ref, bs_ref, cs_ref, acc_ref, rb_ref, yn_ref, st_ref) = refs
    else:
        (z_ref, x_ref, b_ref, c_ref, dt_ref, cwx_ref, cbx_ref, cwb_ref, cbb_ref, cwc_ref, cbc_ref,
         prm_ref, dvec_ref, ng_ref,
         y_ref, sf_ref, sb_ref, xs_ref, bs_ref, cs_ref, acc_ref, rb_ref, yn_ref, st_ref) = refs
    g = pl.program_id(1)
    L = seq_len
    T = min(SSD_CHUNK, L)
    nc = L // T
    H = SSD_HPG
    xs_ref[...] = _silu(_dwconv4(x_ref[...].astype(f32), cwx_ref[...], cbx_ref[...]))
    bs_ref[...] = _silu(_dwconv4(b_ref[...].astype(f32), cwb_ref[...], cbb_ref[...]))
    cs_ref[...] = _silu(_dwconv4(c_ref[...].astype(f32), cwc_ref[...], cbc_ref[...]))
    prm = prm_ref[0]
    a_neg = -jnp.exp(prm[1:2, :])
    tt = lax.broadcasted_iota(jnp.int32, (T, T), 0)
    ss = lax.broadcasted_iota(jnp.int32, (T, T), 1)
    lower = tt >= ss
    upper = ss >= tt
    tri_l = jnp.where(lower, 1.0, 0.0).astype(bf16)
    tri_u = jnp.where(upper, 1.0, 0.0).astype(bf16)
    lower_b = lower[:SSD_TB, :SSD_TB]
    upper_b = upper[:SSD_TB, :SSD_TB]
    er = lax.broadcasted_iota(jnp.int32, (V7X_LANES, SSD_GW), 0)
    ec = lax.broadcasted_iota(jnp.int32, (V7X_LANES, SSD_GW), 1) // SSD_P
    exp_f = jnp.where(er == ec, 1.0, 0.0).astype(bf16)
    exp_b = jnp.where(er == ec + H, 1.0, 0.0).astype(bf16)

    def expand(w, e):
        hi = w.astype(bf16)
        lo = (w - hi.astype(f32)).astype(bf16)
        return _dot(hi, e) + _dot(lo, e)

    def chunk_terms(c):
        rows = pl.ds(c * T, T)
        dt = _softplus(dt_ref[rows, :].astype(f32) + prm[0:1, :])
        da = dt * a_neg
        return rows, dt, da

    s_f = _heads_to_lanes(s0f_ref, st_ref) if has_init else None
    for c in range(nc):
        rows, dt, da = chunk_terms(c)
        cum = _dot_exact_rhs(tri_l, da)
        rsum = _dot_exact_rhs(tri_u, da)
        rb_ref[rows, :] = rsum
        da_t = da.T
        cum_t = _dot_exact_lhs(da_t, tri_u)
        rsum_t = _dot_exact_lhs(da_t, tri_l)
        dt_t = dt.T
        bmat = bs_ref[rows, :]
        cmat = cs_ref[rows, :]
        xmat = xs_ref[rows, :]
        gmat = _dot_nt(cmat.astype(bf16), bmat.astype(bf16))
        for hh in range(H):
            cf, cf_t, dtf_t = cum[:, hh:hh + 1], cum_t[hh:hh + 1, :], dt_t[hh:hh + 1, :]
            rb, rb_t, dtb_t = rsum[:, H + hh:H + hh + 1], rsum_t[H + hh:H + hh + 1, :], dt_t[H + hh:H + hh + 1, :]
            block_rows = []
            for bi in range(T // SSD_TB):
                rr = slice(bi * SSD_TB, (bi + 1) * SSD_TB)
                blocks = []
                for bj in range(T // SSD_TB):
                    cc = slice(bj * SSD_TB, (bj + 1) * SSD_TB)
                    if bi > bj:
                        blk = jnp.exp(cf[rr] - cf_t[:, cc]) * dtf_t[:, cc]
                    elif bi < bj:
                        blk = jnp.exp(rb[rr] - rb_t[:, cc]) * dtb_t[:, cc]
                    else:
                        blk = (jnp.where(lower_b, jnp.exp(cf[rr] - cf_t[:, cc]), 0.0) * dtf_t[:, cc]
                               + jnp.where(upper_b, jnp.exp(rb[rr] - rb_t[:, cc]), 0.0) * dtb_t[:, cc])
                    blocks.append(gmat[rr, cc] * blk)
                block_rows.append(jnp.concatenate(blocks, axis=1))
            m = jnp.concatenate(block_rows, axis=0)
            xh = xmat[:, hh * SSD_P:(hh + 1) * SSD_P]
            acc_ref[rows, pl.ds(hh * SSD_P, SSD_P)] = _dot(m.astype(bf16), xh.astype(bf16))
        ecum = jnp.exp(cum)
        if s_f is not None:
            acc_ref[rows, :] += _dot(cmat.astype(bf16), s_f.astype(bf16)) * expand(ecum, exp_f)
        if c < nc - 1 or not has_init:
            tail = jnp.exp(cum[T - 1:T, :] - cum) * dt
            xw = (xmat * expand(tail, exp_f)).astype(bf16)
            upd = _dot(bmat.T.astype(bf16), xw)
            if s_f is None:
                s_f = upd
            else:
                s_f = s_f * expand(jnp.broadcast_to(ecum[T - 1:T, :], (V7X_SUBLANES, V7X_LANES)),
                                   exp_f)[0:1, :] + upd
    s_b = _heads_to_lanes(s0b_ref, st_ref) if has_init else None
    for c in reversed(range(nc)):
        rows, dt, da = chunk_terms(c)
        rsum = rb_ref[rows, :]
        ers = jnp.exp(rsum)
        bmat = bs_ref[rows, :]
        cmat = cs_ref[rows, :]
        xmat = xs_ref[rows, :]
        if s_b is not None:
            acc_ref[rows, :] += _dot(cmat.astype(bf16), s_b.astype(bf16)) * expand(ers, exp_b)
        if c > 0 or not has_init:
            tail = jnp.exp(rsum[0:1, :] - rsum) * dt
            xw = (xmat * expand(tail, exp_b)).astype(bf16)
            upd = _dot(bmat.T.astype(bf16), xw)
            if s_b is None:
                s_b = upd
            else:
                s_b = s_b * expand(jnp.broadcast_to(ers[0:1, :], (V7X_SUBLANES, V7X_LANES)),
                                   exp_b)[0:1, :] + upd
    if not has_init:
        for hh in range(H):
            sf_ref[0, 0, hh] = s_f[:, hh * SSD_P:(hh + 1) * SSD_P]
            sb_ref[0, 0, hh] = s_b[:, hh * SSD_P:(hh + 1) * SSD_P]
    yg = (acc_ref[...] + xs_ref[...] * dvec_ref[...]) * _silu(z_ref[...].astype(f32))
    yn_ref[g] = yg

    @pl.when(g == SSD_GROUPS - 1)
    def _():
        ssq = None
        for gg in range(SSD_GROUPS):
            y = yn_ref[gg]
            s = jnp.sum(y * y, axis=-1, keepdims=True)
            ssq = s if ssq is None else ssq + s
        inv = lax.rsqrt(ssq * (1.0 / D_SSD) + EPS)
        for gg in range(SSD_GROUPS):
            cols = pl.ds(gg * SSD_GW, SSD_GW)
            y_ref[:, cols] = (yn_ref[gg] * inv * ng_ref[:, cols]).astype(bf16)


def ssd_branch(proj, bufs, row0, n_seq, seq_len, conv_w, conv_b, prm, dvec, ng, init, layer):
    has_init = init is not None
    rb0 = row0 // seq_len
    xoff = OFF_SSD_XBC
    boff = OFF_SSD_XBC + D_SSD
    coff = boff + SSD_GROUPS * SSD_N
    in_specs = [pl.BlockSpec((seq_len, SSD_GW), lambda s, g: (rb0 + s, OFF_SSD_Z // SSD_GW + g)),
                pl.BlockSpec((seq_len, SSD_GW), lambda s, g: (rb0 + s, xoff // SSD_GW + g)),
                pl.BlockSpec((seq_len, SSD_N), lambda s, g: (rb0 + s, boff // SSD_N + g)),
                pl.BlockSpec((seq_len, SSD_N), lambda s, g: (rb0 + s, coff // SSD_N + g)),
                pl.BlockSpec((seq_len, V7X_LANES), lambda s, g: (rb0 + s, OFF_SSD_DT // V7X_LANES + g)),
                pl.BlockSpec((SSD_CONV, SSD_GW), lambda s, g: (0, g)),
                pl.BlockSpec((1, SSD_GW), lambda s, g: (0, g)),
                pl.BlockSpec((SSD_CONV, SSD_N), lambda s, g: (0, D_SSD // SSD_N + g)),
                pl.BlockSpec((1, SSD_N), lambda s, g: (0, D_SSD // SSD_N + g)),
                pl.BlockSpec((SSD_CONV, SSD_N), lambda s, g: (0, D_SSD // SSD_N + SSD_GROUPS + g)),
                pl.BlockSpec((1, SSD_N), lambda s, g: (0, D_SSD // SSD_N + SSD_GROUPS + g)),
                pl.BlockSpec((1, V7X_SUBLANES, V7X_LANES), lambda s, g: (g, 0, 0)),
                pl.BlockSpec((1, SSD_GW), lambda s, g: (0, g)),
                pl.BlockSpec((1, D_SSD), lambda s, g: (0, 0))]
    args = [proj, proj, proj, proj, proj, conv_w, conv_b, conv_w, conv_b, conv_w, conv_b, prm, dvec, ng]
    state_spec = pl.BlockSpec((1, 1, SSD_HPG, SSD_N, SSD_P), lambda s, g: (s, layer, g, 0, 0))
    y_spec = pl.BlockSpec((seq_len, D_SSD), lambda s, g: (rb0 + s, 0))
    if has_init:
        in_specs += [state_spec, state_spec]
        args += [init[0], init[1]]
        out_specs, out_shape = y_spec, _BRANCH_BUF
    else:
        st = jax.ShapeDtypeStruct((n_seq, DEPTH, SSD_HEADS, SSD_N, SSD_P), f32)
        out_specs, out_shape = [y_spec, state_spec, state_spec], [_BRANCH_BUF, st, st]
    scratch = [pltpu.VMEM((seq_len, SSD_GW), f32),
               pltpu.VMEM((seq_len, SSD_N), f32),
               pltpu.VMEM((seq_len, SSD_N), f32),
               pltpu.VMEM((seq_len, SSD_GW), f32),
               pltpu.VMEM((seq_len, V7X_LANES), f32),
               pltpu.VMEM((SSD_GROUPS, seq_len, SSD_GW), f32),
               pltpu.VMEM((SSD_N, SSD_GW), f32)]
    return _inplace_call(functools.partial(_ssd_kernel, seq_len=seq_len, has_init=has_init), bufs,
                         (n_seq, SSD_GROUPS), in_specs, args, out_specs, out_shape, scratch, "ssd_branch")


def _head_rmsnorm(x, g):
    return x * lax.rsqrt(jnp.mean(x * x, axis=-1, keepdims=True) + EPS) * g


CTX_HB = 4
CTX_W = CTX_HB * NA_HD


def _ctx_attn_kernel(q_ref, k_ref, v_ref, qg_ref, kg_ref, y_ref, ko_ref, vo_ref):
    scale = NA_HD ** -0.5
    vo_ref[0, 0] = v_ref[...].astype(f32)
    for h in range(CTX_HB):
        cols = pl.ds(h * NA_HD, NA_HD)
        q = _head_rmsnorm(q_ref[:, cols].astype(f32), qg_ref[...])
        k = _head_rmsnorm(k_ref[:, cols].astype(f32), kg_ref[...])
        ko_ref[0, 0, :, cols] = k
        s = _dot_nt(q.astype(bf16), k.astype(bf16)) * scale
        p = jnp.exp(s - jnp.max(s, axis=-1, keepdims=True))
        o = _dot(p.astype(bf16), v_ref[:, cols]) / jnp.sum(p, axis=-1, keepdims=True)
        y_ref[:, cols] = o.astype(bf16)


def context_attention(proj, bufs, row0, qg, kg, layer):
    rb0 = row0 // SEQ
    spec = lambda off: pl.BlockSpec((SEQ, CTX_W), lambda s, hb: (rb0 + s, off // CTX_W + hb))
    gspec = pl.BlockSpec((1, NA_HD), lambda s, hb: (0, 0))
    kv_spec = pl.BlockSpec((1, 1, SEQ, CTX_W), lambda s, hb: (s, layer, 0, hb))
    kv_shape = jax.ShapeDtypeStruct((BATCH, DEPTH, SEQ, NA_W), f32)
    return _inplace_call(
        _ctx_attn_kernel, bufs, (BATCH, NA_HEADS // CTX_HB),
        [spec(OFF_NA_Q), spec(OFF_NA_K), spec(OFF_NA_V), gspec, gspec], [proj, proj, proj, qg, kg],
        [pl.BlockSpec((SEQ, CTX_W), lambda s, hb: (rb0 + s, hb)), kv_spec, kv_spec],
        [_BRANCH_BUF, kv_shape, kv_shape], [], "context_attention")


NA_ROWS = DEC_SEQ // GRID_W
NA_NK = NA_WR * GRID_W


def _na_row_start(r):
    return min(max(r - NA_WR // 2, 0), NA_ROWS - NA_WR)


def _rope(x, cos, sin_signed):
    lane = lax.broadcasted_iota(jnp.int32, x.shape, 1)
    quarter = NA_HD // 4
    swapped = jnp.where((lane & (2 * quarter - 1)) < quarter,
                        pltpu.roll(x, NA_HD - quarter, axis=1), pltpu.roll(x, quarter, axis=1))
    return x * cos + swapped * sin_signed


def _na_row_groups():
    groups, r = [], 0
    while r < NA_ROWS:
        n = 1
        while r + n < NA_ROWS and _na_row_start(r + n) == _na_row_start(r):
            n += 1
        groups.append((r, n, _na_row_start(r)))
        r += n
    return groups


def _na_attn_kernel(q_ref, k_ref, v_ref, kc_ref, vc_ref, qg_ref, kg_ref, cos_ref, sin_ref,
                    bias_ref, valid_ref, y_ref, qs_ref, ks_ref, s_ref, p_ref, oc_ref, w_ref):
    scale = NA_HD ** -0.5
    cos = cos_ref[...]
    sin = sin_ref[...]
    qs_ref[...] = _rope(_head_rmsnorm(q_ref[...].astype(f32), qg_ref[...]), cos, sin).astype(bf16)
    ks_ref[...] = _rope(_head_rmsnorm(k_ref[...].astype(f32), kg_ref[...]), cos, sin).astype(bf16)
    groups = _na_row_groups()
    valid = valid_ref[...] > 0.0
    for r0, n, rs in groups:
        rows = pl.ds(r0 * GRID_W, n * GRID_W)
        d0 = r0 - rs
        kw = ks_ref[pl.ds(rs * GRID_W, NA_NK), :]
        s = (_dot_nt(qs_ref[rows, :], kw) * scale).reshape(n, GRID_W, NA_NK) + bias_ref[0, d0:d0 + n]
        s_ref[rows, :] = jnp.where(valid[None], s, -1e30).reshape(n * GRID_W, NA_NK)
    s_ctx = _dot_nt(qs_ref[...], kc_ref[0, 0].astype(bf16)) * scale
    s_loc = s_ref[...]
    m = jnp.maximum(jnp.max(s_loc, axis=-1, keepdims=True), jnp.max(s_ctx, axis=-1, keepdims=True))
    p_loc = jnp.exp(s_loc - m)
    p_ctx = jnp.exp(s_ctx - m)
    w_ref[...] = 1.0 / (jnp.sum(p_loc, axis=-1, keepdims=True) + jnp.sum(p_ctx, axis=-1, keepdims=True))
    p_ref[...] = p_loc.astype(bf16)
    oc_ref[...] = _dot(p_ctx.astype(bf16), vc_ref[0, 0].astype(bf16))
    for r0, n, rs in groups:
        rows = pl.ds(r0 * GRID_W, n * GRID_W)
        vw = v_ref[pl.ds(rs * GRID_W, NA_NK), :]
        y_ref[rows, :] = ((_dot(p_ref[rows, :], vw) + oc_ref[rows, :]) * w_ref[rows, :]).astype(bf16)


def neighbourhood_attention(proj, cache_k, cache_v, layer, qg, kg, cos, sin, bias, valid):
    spec = lambda off: pl.BlockSpec((DEC_SEQ, NA_HD), lambda b, h: (b, off // NA_HD + h))
    cspec = pl.BlockSpec((1, 1, PAST_LEN, NA_HD), lambda b, h: (b, layer, 0, h))
    gspec = pl.BlockSpec((1, NA_HD), lambda b, h: (0, 0))
    tspec = pl.BlockSpec((DEC_SEQ, NA_HD), lambda b, h: (0, 0))
    return pl.pallas_call(
        _na_attn_kernel,
        grid=(DEC_BATCH, NA_HEADS),
        in_specs=[spec(OFF_NA_Q), spec(OFF_NA_K), spec(OFF_NA_V), cspec, cspec, gspec, gspec,
                  tspec, tspec,
                  pl.BlockSpec((1, NA_WR, GRID_W, NA_NK), lambda b, h: (h, 0, 0, 0)),
                  pl.BlockSpec((GRID_W, NA_NK), lambda b, h: (0, 0))],
        out_specs=pl.BlockSpec((DEC_SEQ, NA_HD), lambda b, h: (b, h)),
        out_shape=_BRANCH_BUF,
        scratch_shapes=[pltpu.VMEM((DEC_SEQ, NA_HD), bf16), pltpu.VMEM((DEC_SEQ, NA_HD), bf16),
                        pltpu.VMEM((DEC_SEQ, NA_NK), f32), pltpu.VMEM((DEC_SEQ, NA_NK), bf16),
                        pltpu.VMEM((DEC_SEQ, NA_HD), f32), pltpu.VMEM((DEC_SEQ, 1), f32)],
        compiler_params=_params(2), name="neighbourhood_attention",
    )(proj, proj, proj, cache_k, cache_v, qg, kg, cos, sin, bias, valid)


TAIL_TN = 512
N_TAIL_TILES = (D_IN_PAD - HEAD_COLS) // TAIL_TN
N_QKV_TILES = 3 * NA_W // TAIL_TN
DT_W = 2 * SSD_HEADS


def _tail_kernel(a_ref, b_ref, o_ref):
    j = pl.program_id(1)
    lane = lax.broadcasted_iota(jnp.int32, (D_MODEL, TAIL_TN), 1)

    @pl.when(j == 0)
    def _():
        o_ref[0] = a_ref[0]

    @pl.when((j >= 1) & (j <= N_QKV_TILES))
    def _():
        main = pltpu.roll(a_ref[0].astype(f32), TAIL_TN - DT_W, axis=1)
        nxt = pltpu.roll(b_ref[0].astype(f32), V7X_LANES - DT_W, axis=1)
        last_lane = lax.broadcasted_iota(jnp.int32, (D_MODEL, V7X_LANES), 1)
        last = jnp.where(last_lane < V7X_LANES - DT_W, main[:, TAIL_TN - V7X_LANES:], nxt)
        o_ref[0] = jnp.concatenate([main[:, :TAIL_TN - V7X_LANES], last], axis=1).astype(bf16)

    @pl.when(j == N_TAIL_TILES - 1)
    def _():
        a = a_ref[0].astype(f32)
        out = jnp.zeros((D_MODEL, TAIL_TN), f32)
        for gidx in range(SSD_GROUPS):
            base = gidx * V7X_LANES
            for src0, dst0 in ((gidx * SSD_HPG, base), (SSD_HEADS + gidx * SSD_HPG, base + SSD_HPG)):
                shift = (dst0 - src0) % TAIL_TN
                moved = a if shift == 0 else pltpu.roll(a, shift, axis=1)
                out = jnp.where((lane >= dst0) & (lane < dst0 + SSD_HPG), moved, out)
        o_ref[0] = out.astype(bf16)


def _split_w_in(w_in):
    w_in = w_in.astype(bf16)
    dt0 = IN_OFFSETS[7]
    a_tile0, dt_tile, b_tile0 = HEAD_COLS // TAIL_TN, dt0 // TAIL_TN, dt0 // V7X_LANES
    b_per_a = TAIL_TN // V7X_LANES
    last_b = (D_IN - 1) // V7X_LANES
    tail = pl.pallas_call(
        _tail_kernel, grid=(DEPTH, N_TAIL_TILES),
        in_specs=[pl.BlockSpec((1, D_MODEL, TAIL_TN),
                               lambda l, j: (l, 0, jnp.where(j == N_TAIL_TILES - 1, dt_tile, a_tile0 + j))),
                  pl.BlockSpec((1, D_MODEL, V7X_LANES),
                               lambda l, j: (l, 0, jnp.minimum(b_tile0 + b_per_a * j, last_b)))],
        out_specs=pl.BlockSpec((1, D_MODEL, TAIL_TN), lambda l, j: (l, 0, j)),
        out_shape=jax.ShapeDtypeStruct((DEPTH, D_MODEL, D_IN_PAD - HEAD_COLS), bf16),
        compiler_params=_params(2), name="build_w_in_tail",
    )(w_in, w_in)
    return w_in, tail


def _group_lanes(v):
    rows = []
    for gidx in range(SSD_GROUPS):
        sl = slice(gidx * SSD_HPG, (gidx + 1) * SSD_HPG)
        rows.append(jnp.concatenate([v[0, sl], v[1, sl], jnp.zeros((V7X_LANES - 2 * SSD_HPG,), f32)]))
    return jnp.stack(rows)


def _rope_tables():
    t = np.arange(DEC_SEQ)
    quarter = NA_HD // 4
    inv = ROPE_BASE ** (-np.arange(quarter, dtype=np.float32) / quarter)
    ang_r = (t // GRID_W).astype(np.float32)[:, None] * inv
    ang_c = (t % GRID_W).astype(np.float32)[:, None] * inv
    cos = np.concatenate([np.cos(ang_r), np.cos(ang_r), np.cos(ang_c), np.cos(ang_c)], axis=1)
    sin = np.concatenate([-np.sin(ang_r), np.sin(ang_r), -np.sin(ang_c), np.sin(ang_c)], axis=1)
    return jnp.asarray(cos, f32), jnp.asarray(sin, f32)


def _na_tables(rpb):
    cq = np.arange(GRID_W)
    kc = np.tile(np.arange(GRID_W), NA_WR)
    col_start = np.clip(cq - NA_WC // 2, 0, GRID_W - NA_WC)
    valid = (kc[None, :] >= col_start[:, None]) & (kc[None, :] < col_start[:, None] + NA_WC)
    col_off = np.clip(cq[None, :] - cq[:, None], 1 - NA_WC, NA_WC - 1) + NA_WC - 1
    onehot = (col_off[None, :, :] == np.arange(2 * NA_WC - 1)[:, None, None]).astype(np.float32)
    toep = jnp.einsum('hic,cqk->hiqk', rpb.astype(f32), jnp.asarray(onehot), precision=lax.Precision.HIGHEST)
    tables = []
    for d in range(NA_WR):
        rows = toep[:, NA_WR - 1 - d:2 * NA_WR - 1 - d]
        tables.append(rows.transpose(0, 2, 1, 3).reshape(NA_HEADS, GRID_W, NA_NK))
    return jnp.stack(tables, axis=1), jnp.asarray(valid, f32)


def kernel(x_prompt, x_sample, cache_na_k, cache_na_v, state_lru_f, state_lru_b,
           state_ret_f, state_ret_b, state_ssd_f, state_ssd_b, c, c_ctx,
           norm1_g, norm2_g, w_ada, b_ada, w_in, w_gate, b_gate, w_branch, w_out,
           lru_conv_w, lru_conv_b, lru_wa, lru_ba, lru_wx, lru_bx, lru_lambda,
           ret_gn_g, ssd_conv_w, ssd_conv_b, ssd_a_log, ssd_dt_bias, ssd_d, ssd_norm_g,
           na_q_g, na_k_g, na_rpb, ffn_w_up, ffn_conv_w, ffn_conv_b, ffn_w_down):
    xs0 = x_sample.reshape(T_SAMPLE, D_MODEL)
    xp0 = x_prompt.reshape(T_PROMPT, D_MODEL)
    n_ptiles = N_TILES - N_SAMPLE_TILES
    cond = jnp.concatenate([c, c_ctx[None, :], jnp.zeros((N_COND_PAD - N_COND, D_MODEL), f32)], axis=0)
    mod_all = ada_modulation(cond, w_ada, b_ada)
    cos, sin = _rope_tables()
    hh = jnp.arange(RET_HEADS, dtype=f32)
    ret_la = jnp.stack([jnp.log1p(-jnp.exp2(-5.0 - hh)), jnp.log1p(-jnp.exp2(-5.5 - hh))], axis=1)
    cache_k = cache_na_k.reshape(DEC_BATCH, DEPTH, PAST_LEN, NA_W)
    cache_v = cache_na_v.reshape(DEC_BATCH, DEPTH, PAST_LEN, NA_W)
    lru_init = (state_lru_f.reshape(DEC_BATCH, DEPTH, 1, D_RNN), state_lru_b.reshape(DEC_BATCH, DEPTH, 1, D_RNN))

    w_head_b, w_tail_b = _split_w_in(w_in)
    w_branch_b, w_out_b, w_down_b = w_branch.astype(bf16), w_out.astype(bf16), ffn_w_down.astype(bf16)

    new = {k: None for k in ("k", "v", "lru_f", "lru_b", "ret_f", "ret_b", "ssd_f", "ssd_b")}
    for l in range(DEPTH):
        mod = mod_all[l].reshape(N_COND_PAD, 1, N_MOD * D_MODEL)
        g1 = norm1_g[l][None, :]
        if l == 0:
            proj, xn = in_projection(xs0, 0, 0, N_SAMPLE_TILES, g1, mod, w_head_b, w_tail_b, l)
            proj, xn = in_projection(xp0, 0, N_SAMPLE_TILES, n_ptiles, g1, mod, w_head_b, w_tail_b, l, (proj, xn))
        else:
            proj, xn = in_projection(x, 0, 0, N_TILES, g1, mod, w_head_b, w_tail_b, l)

        w4 = jnp.concatenate([lru_wa[l, 0], lru_wx[l, 0], lru_wa[l, 1], lru_wx[l, 1]], axis=-1).astype(bf16)
        b4 = jnp.concatenate([lru_ba[l, 0].reshape(LRU_BLOCKS, 1, LRU_BS), lru_bx[l, 0].reshape(LRU_BLOCKS, 1, LRU_BS),
                              lru_ba[l, 1].reshape(LRU_BLOCKS, 1, LRU_BS), lru_bx[l, 1].reshape(LRU_BLOCKS, 1, LRU_BS)],
                             axis=-1)
        lru_args = (lru_conv_w[l], lru_conv_b[l][None, :], w4, b4, lru_lambda[l])
        y_lru = lru_branch(proj, (None,), 0, DEC_BATCH, DEC_SEQ, *lru_args, lru_init, l)
        y_lru, new["lru_f"], new["lru_b"] = lru_branch(
            proj, (y_lru, new["lru_f"], new["lru_b"]), T_SAMPLE, BATCH, SEQ, *lru_args, None, l)

        gn = ret_gn_g[l][None, :]
        y_ret = retention_branch(proj, (None,), 0, DEC_BATCH, DEC_SEQ, ret_la, gn, (state_ret_f, state_ret_b), l)
        y_ret, new["ret_f"], new["ret_b"] = retention_branch(
            proj, (y_ret, new["ret_f"], new["ret_b"]), T_SAMPLE, BATCH, SEQ, ret_la, gn, None, l)

        prm = jnp.stack([_group_lanes(ssd_dt_bias[l]), _group_lanes(ssd_a_log[l])], axis=1)
        prm = jnp.concatenate([prm, jnp.zeros((SSD_GROUPS, V7X_SUBLANES - 2, V7X_LANES), f32)], axis=1)
        dvec = jnp.repeat(ssd_d[l], SSD_P)[None, :]
        ssd_args = (ssd_conv_w[l], ssd_conv_b[l][None, :], prm, dvec, ssd_norm_g[l][None, :])
        y_ssd = ssd_branch(proj, (None,), 0, DEC_BATCH, DEC_SEQ, *ssd_args, (state_ssd_f, state_ssd_b), l)
        y_ssd, new["ssd_f"], new["ssd_b"] = ssd_branch(
            proj, (y_ssd, new["ssd_f"], new["ssd_b"]), T_SAMPLE, BATCH, SEQ, *ssd_args, None, l)

        qg = na_q_g[l][None, :]
        kg = na_k_g[l][None, :]
        bias, valid = _na_tables(na_rpb[l])
        y_na = neighbourhood_attention(proj, cache_k, cache_v, l, qg, kg, cos, sin, bias, valid)
        y_na, new["k"], new["v"] = context_attention(proj, (y_na, new["k"], new["v"]), T_SAMPLE, qg, kg, l)

        merged = merge_branches(xn, (y_lru, y_ret, y_ssd, y_na), w_gate, b_gate[l][None, :], w_branch_b, l)
        if l == 0:
            x = residual_projection(merged, w_out_b, l, xs0, mod, 2, 0, N_SAMPLE_TILES)
            x = residual_projection(merged, w_out_b, l, xp0, mod, 2, N_SAMPLE_TILES, n_ptiles, x_tile0=0, buf=x)
        else:
            x = residual_projection(merged, w_out_b, l, x, mod, 2)
        hmid = ffn_up(x, norm2_g[l][None, :], mod, ffn_w_up, l, ffn_conv_w[l], ffn_conv_b[l][None, :])
        if l < DEPTH - 1:
            x = residual_projection(hmid, w_down_b, l, x, mod, 5)
        else:
            y_sample = residual_projection(hmid, w_down_b, l, x, mod, 5, 0, N_SAMPLE_TILES, out_rows=T_SAMPLE)
            y_prompt = residual_projection(hmid, w_down_b, l, x, mod, 5, N_SAMPLE_TILES, n_ptiles,
                                           out_tile0=0, out_rows=T_PROMPT)

    kv_shape = (BATCH, DEPTH, SEQ, NA_HEADS, NA_HD)
    return (y_prompt.reshape(BATCH, SEQ, D_MODEL), y_sample.reshape(DEC_BATCH, DEC_SEQ, D_MODEL),
            new["k"].reshape(kv_shape), new["v"].reshape(kv_shape),
            new["lru_f"].reshape(BATCH, DEPTH, D_RNN), new["lru_b"].reshape(BATCH, DEPTH, D_RNN),
            new["ret_f"], new["ret_b"], new["ssd_f"], new["ssd_b"])
```

```python
import functools
import math

import jax
import jax.numpy as jnp
import numpy as np
from jax import lax
from jax.experimental import pallas as pl
from jax.experimental.pallas import tpu as pltpu

D_MODEL = 2048
BATCH = 16
SEQ = 256
DEPTH = 2
DEC_BATCH = 8
DEC_SEQ = 1024
PAST_LEN = 256
GRID_W = 64
EPS = 1e-6
N_BRANCH = 4
BRANCH_W = 1024
N_MOD = 6
D_RNN = 1024
LRU_BLOCKS = 8
LRU_BS = D_RNN // LRU_BLOCKS
LRU_CONV = 4
LRU_C = 8.0
RET_HEADS = 4
RET_DK = 128
RET_DV = 256
SSD_HEADS = 16
SSD_P = 64
SSD_N = 128
SSD_GROUPS = 2
SSD_CONV = 4
D_SSD = SSD_HEADS * SSD_P
SSD_CONV_CH = D_SSD + 2 * SSD_GROUPS * SSD_N
NA_HEADS = 8
NA_HD = 128
NA_W = NA_HEADS * NA_HD
NA_WR = 8
NA_WC = 16
ROPE_BASE = 10000.0
D_FF = 5632
FFN_CONV = 3
IN_SIZES = (D_RNN, D_RNN,
            RET_HEADS * RET_DK, RET_HEADS * RET_DK, RET_HEADS * RET_DV, RET_HEADS * RET_DV,
            D_SSD, SSD_CONV_CH, 2 * SSD_HEADS,
            NA_W, NA_W, NA_W)
D_IN = sum(IN_SIZES)
IN_OFFSETS = tuple(int(s) for s in np.cumsum(IN_SIZES)[:-1])

V7X_LANES = 128
V7X_SUBLANES = 8
V7X_VMEM_LIMIT_BYTES = 56 * 1024 * 1024

TM = 1024
T_SAMPLE = DEC_BATCH * DEC_SEQ
T_PROMPT = BATCH * SEQ
T_ALL = T_SAMPLE + T_PROMPT
N_SAMPLE_TILES = T_SAMPLE // TM
N_TILES = T_ALL // TM
N_COND = DEC_BATCH + 1
N_COND_PAD = 16

SSD_HPG = SSD_HEADS // SSD_GROUPS
OFF_LRU_X = 0
OFF_LRU_G = OFF_LRU_X + D_RNN
OFF_RET_Q = OFF_LRU_G + D_RNN
OFF_RET_K = OFF_RET_Q + RET_HEADS * RET_DK
OFF_RET_V = OFF_RET_K + RET_HEADS * RET_DK
OFF_RET_G = OFF_RET_V + RET_HEADS * RET_DV
OFF_SSD_Z = OFF_RET_G + RET_HEADS * RET_DV
OFF_SSD_XBC = OFF_SSD_Z + D_SSD
OFF_NA_Q = OFF_SSD_XBC + SSD_CONV_CH
OFF_NA_K = OFF_NA_Q + NA_W
OFF_NA_V = OFF_NA_K + NA_W
OFF_SSD_DT = OFF_NA_V + NA_W
PROJ_TN = 1024
D_IN_PAD = -(-(OFF_SSD_DT + SSD_GROUPS * V7X_LANES) // PROJ_TN) * PROJ_TN
N_HEAD_TILES = OFF_NA_Q // PROJ_TN
HEAD_COLS = N_HEAD_TILES * PROJ_TN

RET_CHUNK = 256
SSD_CHUNK = 256
SSD_TB = 128

f32 = jnp.float32
bf16 = jnp.bfloat16

_ARB = "arbitrary"


def _params(n_axes):
    return pltpu.CompilerParams(dimension_semantics=(_ARB,) * n_axes,
                                vmem_limit_bytes=V7X_VMEM_LIMIT_BYTES)


def _mod_spec(k, tile0=0):
    return pl.BlockSpec((1, 1, D_MODEL), lambda i, j: (jnp.minimum(tile0 + i, N_SAMPLE_TILES), 0, k))


def _dot(a, b):
    return jnp.dot(a, b, preferred_element_type=f32)


def _dot_nt(a, b):
    return lax.dot_general(a, b, (((1,), (1,)), ((), ())), preferred_element_type=f32)


def _sigmoid(x):
    return 0.5 * jnp.tanh(0.5 * x) + 0.5


def _silu(x):
    return x * _sigmoid(x)


def _gelu_tanh(x):
    return 0.5 * x * (1.0 + jnp.tanh(math.sqrt(2.0 / math.pi) * (x + 0.044715 * (x * x * x))))


def _shift_rows(v, k):
    n = v.shape[0]
    row = lax.broadcasted_iota(jnp.int32, (n, 1), 0)
    if k == 1:
        return jnp.where(row == 0, 0.0, pltpu.roll(v, 1, axis=0))
    return jnp.where(row == n - 1, 0.0, pltpu.roll(v, n - 1, axis=0))


def _dwconv4(x, cw, cb):
    acc = _shift_rows(x * cw[0:1, :], 1) + x * cw[1:2, :]
    return cb + x * cw[2:3, :] + _shift_rows(acc, 1) + _shift_rows(x * cw[3:4, :], -1)


def _softplus(x):
    return jnp.maximum(x, 0.0) + jnp.log1p(jnp.exp(-jnp.abs(x)))


def _drop_refs(body, n, *refs):
    body(*refs[n:])


def _inplace_call(body, bufs, grid, in_specs, args, out_specs, out_shape, scratch, name):
    held = [(k, b) for k, b in enumerate(bufs) if b is not None]
    if held:
        body = functools.partial(_drop_refs, body, len(held))
        in_specs = [pl.BlockSpec(memory_space=pl.ANY)] * len(held) + list(in_specs)
        args = [b for _, b in held] + list(args)
    aliases = {pos: k for pos, (k, _) in enumerate(held)}
    return pl.pallas_call(
        body, grid=grid, in_specs=in_specs, out_specs=out_specs, out_shape=out_shape,
        scratch_shapes=scratch, input_output_aliases=aliases,
        compiler_params=_params(len(grid)), name=name,
    )(*args)


_BRANCH_BUF = jax.ShapeDtypeStruct((T_ALL, BRANCH_W), bf16)


ADA_TN = 1024


def _ada_kernel(c_ref, w_ref, b_ref, o_ref):
    c = _silu(c_ref[...]).astype(bf16)
    o_ref[0] = _dot(c, w_ref[0].astype(bf16)) + b_ref[0]


def ada_modulation(cond, w_ada, b_ada):
    n = N_MOD * D_MODEL
    return pl.pallas_call(
        _ada_kernel,
        grid=(DEPTH, n // ADA_TN),
        in_specs=[pl.BlockSpec((N_COND_PAD, D_MODEL), lambda l, j: (0, 0)),
                  pl.BlockSpec((1, D_MODEL, ADA_TN), lambda l, j: (l, 0, j)),
                  pl.BlockSpec((1, 1, ADA_TN), lambda l, j: (l, 0, j))],
        out_specs=pl.BlockSpec((1, N_COND_PAD, ADA_TN), lambda l, j: (l, 0, j)),
        out_shape=jax.ShapeDtypeStruct((DEPTH, N_COND_PAD, n), f32),
        compiler_params=_params(2),
        name="ada_modulation",
    )(cond, w_ada, b_ada.reshape(DEPTH, 1, n))


NORM_ROWS = 128


def _modulated_norm(x_ref, g_ref, sc_ref, sh_ref, xn_ref):
    gs = g_ref[...] * (1.0 + sc_ref[0])
    sh = sh_ref[0]

    def body(r, carry):
        rows = pl.ds(pl.multiple_of(r * NORM_ROWS, NORM_ROWS), NORM_ROWS)
        x = x_ref[rows, :]
        y = x * lax.rsqrt(jnp.mean(x * x, axis=-1, keepdims=True) + EPS)
        xn_ref[rows, :] = (y * gs + sh).astype(bf16)
        return carry

    lax.fori_loop(0, TM // NORM_ROWS, body, 0)


def _in_proj_kernel(x_ref, g_ref, sc_ref, sh_ref, wh_ref, wt_ref, o_ref, xn_ref):
    j = pl.program_id(1)

    @pl.when(j == 0)
    def _():
        _modulated_norm(x_ref, g_ref, sc_ref, sh_ref, xn_ref)

    @pl.when(j < N_HEAD_TILES)
    def _():
        o_ref[...] = _dot(xn_ref[...], wh_ref[0]).astype(bf16)

    @pl.when(j >= N_HEAD_TILES)
    def _():
        o_ref[...] = _dot(xn_ref[...], wt_ref[0]).astype(bf16)


def in_projection(x, x_tile0, tile0, n_tiles, g, mod, w_head, w_tail, layer, bufs=(None, None)):
    n = D_IN_PAD
    n_tail = w_tail.shape[2] // PROJ_TN
    return _inplace_call(
        _in_proj_kernel, bufs, (n_tiles, n // PROJ_TN),
        [pl.BlockSpec((TM, D_MODEL), lambda i, j: (x_tile0 + i, 0)),
         pl.BlockSpec((1, D_MODEL), lambda i, j: (0, 0)),
         _mod_spec(1, tile0), _mod_spec(0, tile0),
         pl.BlockSpec((1, D_MODEL, PROJ_TN), lambda i, j: (layer, 0, jnp.minimum(j, N_HEAD_TILES - 1))),
         pl.BlockSpec((1, D_MODEL, PROJ_TN),
                      lambda i, j: (layer, 0, jnp.clip(j - N_HEAD_TILES, 0, n_tail - 1)))],
        [x, g, mod, mod, w_head, w_tail],
        [pl.BlockSpec((TM, PROJ_TN), lambda i, j: (tile0 + i, j)),
         pl.BlockSpec((TM, D_MODEL), lambda i, j: (tile0 + i, 0))],
        [jax.ShapeDtypeStruct((T_ALL, n), bf16), jax.ShapeDtypeStruct((T_ALL, D_MODEL), bf16)],
        [], "in_projection")


MERGE_TN = 256


def _merge_kernel(xn_ref, *refs):
    y_refs, wg_refs, bg_refs = refs[0:4], refs[4:8], refs[8:12]
    wb_ref, o_ref = refs[12], refs[13]
    xn = xn_ref[...]
    acc = None
    for n in range(N_BRANCH):
        gate = _sigmoid(_dot(xn, wg_refs[n][0].astype(bf16)) + bg_refs[n][...])
        term = gate * _dot(y_refs[n][...], wb_ref[0, n])
        acc = term if acc is None else acc + term
    o_ref[...] = acc.astype(bf16)


def merge_branches(xn, ys, w_gate, b_gate, w_branch, layer):
    nj = D_MODEL // MERGE_TN
    y_spec = pl.BlockSpec((TM, BRANCH_W), lambda i, j: (i, 0))
    wg_specs = [pl.BlockSpec((1, D_MODEL, MERGE_TN), lambda i, j, n=n: (layer, 0, n * nj + j))
                for n in range(N_BRANCH)]
    bg_specs = [pl.BlockSpec((1, MERGE_TN), lambda i, j, n=n: (0, n * nj + j)) for n in range(N_BRANCH)]
    return pl.pallas_call(
        _merge_kernel,
        grid=(N_TILES, nj),
        in_specs=[pl.BlockSpec((TM, D_MODEL), lambda i, j: (i, 0)),
                  y_spec, y_spec, y_spec, y_spec, *wg_specs, *bg_specs,
                  pl.BlockSpec((1, N_BRANCH, BRANCH_W, MERGE_TN), lambda i, j: (layer, 0, 0, j))],
        out_specs=pl.BlockSpec((TM, MERGE_TN), lambda i, j: (i, j)),
        out_shape=jax.ShapeDtypeStruct((T_ALL, D_MODEL), bf16),
        compiler_params=_params(2),
        name="merge_branches",
    )(xn, *ys, *([w_gate] * N_BRANCH), *([b_gate] * N_BRANCH), w_branch)


RES_TN = 512
RES_TN_SHORT_K = 1024


def _residual_kernel(a_ref, w_ref, x_ref, gv_ref, o_ref):
    o_ref[...] = x_ref[...] + gv_ref[0] * _dot(a_ref[...], w_ref[0])


def residual_projection(a, w, layer, x, mod, k_mod, tile0=0, n_tiles=N_TILES, x_tile0=None, out_tile0=None,
                        out_rows=T_ALL, buf=None):
    kdim = a.shape[1]
    tn = RES_TN_SHORT_K if kdim == D_MODEL else RES_TN
    x_tile0 = tile0 if x_tile0 is None else x_tile0
    out_tile0 = tile0 if out_tile0 is None else out_tile0
    return _inplace_call(
        _residual_kernel, (buf,), (n_tiles, D_MODEL // tn),
        [pl.BlockSpec((TM, kdim), lambda i, j: (tile0 + i, 0)),
         pl.BlockSpec((1, kdim, tn), lambda i, j: (layer, 0, j)),
         pl.BlockSpec((TM, tn), lambda i, j: (x_tile0 + i, j)),
         pl.BlockSpec((1, 1, tn),
                      lambda i, j: (jnp.minimum(tile0 + i, N_SAMPLE_TILES), 0, k_mod * (D_MODEL // tn) + j))],
        [a, w, x, mod],
        pl.BlockSpec((TM, tn), lambda i, j: (out_tile0 + i, j)),
        jax.ShapeDtypeStruct((out_rows, D_MODEL), f32), [], "residual_projection")


FFN_TN = 512


def _ffn_up_kernel(x_ref, g_ref, sc_ref, sh_ref, wa_ref, wv_ref, cw_ref, cb_ref, o_ref, xn_ref):
    i = pl.program_id(0)

    @pl.when(pl.program_id(1) == 0)
    def _():
        _modulated_norm(x_ref, g_ref, sc_ref, sh_ref, xn_ref)

    xn = xn_ref[...]
    a = _dot(xn, wa_ref[0].astype(bf16))
    seq_len = jnp.where(i < N_SAMPLE_TILES, DEC_SEQ, SEQ)
    pos = lax.broadcasted_iota(jnp.int32, (TM, 1), 0) & (seq_len - 1)
    prev = jnp.where(pos == 0, 0.0, pltpu.roll(a, 1, axis=0))
    nxt = jnp.where(pos == seq_len - 1, 0.0, pltpu.roll(a, TM - 1, axis=0))
    cw = cw_ref[...]
    conv = cb_ref[...] + prev * cw[0:1, :] + a * cw[1:2, :] + nxt * cw[2:3, :]
    o_ref[...] = (_gelu_tanh(conv) * _dot(xn, wv_ref[0].astype(bf16))).astype(bf16)


def ffn_up(x, g, mod, w_up, layer, conv_w, conv_b):
    nj = D_FF // FFN_TN
    return pl.pallas_call(
        _ffn_up_kernel,
        grid=(N_TILES, nj),
        in_specs=[pl.BlockSpec((TM, D_MODEL), lambda i, j: (i, 0)),
                  pl.BlockSpec((1, D_MODEL), lambda i, j: (0, 0)),
                  _mod_spec(4), _mod_spec(3),
                  pl.BlockSpec((1, D_MODEL, FFN_TN), lambda i, j: (layer, 0, j)),
                  pl.BlockSpec((1, D_MODEL, FFN_TN), lambda i, j: (layer, 0, nj + j)),
                  pl.BlockSpec((FFN_CONV, FFN_TN), lambda i, j: (0, j)),
                  pl.BlockSpec((1, FFN_TN), lambda i, j: (0, j))],
        out_specs=pl.BlockSpec((TM, FFN_TN), lambda i, j: (i, j)),
        out_shape=jax.ShapeDtypeStruct((T_ALL, D_FF), bf16),
        scratch_shapes=[pltpu.VMEM((TM, D_MODEL), bf16)],
        compiler_params=_params(2),
        name="ffn_up",
    )(x, g, mod, mod, w_up, w_up, conv_w, conv_b)


LRU_CB_LONG = D_RNN
LRU_CB_SHORT = D_RNN


def _lru_kernel(*refs, seq_len, has_init, cbw):
    if has_init:
        (x_ref, g_ref, cw_ref, cb_ref, w4_ref, b4_ref, lam_ref, h0f_ref, h0b_ref,
         y_ref, af_ref, uf_ref, ab_ref, ub_ref, hf_ref, hb_ref) = refs
    else:
        (x_ref, g_ref, cw_ref, cb_ref, w4_ref, b4_ref, lam_ref,
         y_ref, sf_ref, sb_ref, af_ref, uf_ref, ab_ref, ub_ref, hf_ref, hb_ref) = refs
    L = seq_len
    S = V7X_SUBLANES
    xc = _dwconv4(x_ref[...].astype(f32), cw_ref[...], cb_ref[...])
    sp = _softplus(-lam_ref[...])
    for n in range(cbw // LRU_BS):
        cols = slice(n * LRU_BS, (n + 1) * LRU_BS)
        xcn = xc[:, cols]
        z = _dot(xcn.astype(bf16), w4_ref[n]) + b4_ref[n]
        for d, (a_ref, u_ref) in enumerate(((af_ref, uf_ref), (ab_ref, ub_ref))):
            r = _sigmoid(z[:, (2 * d) * LRU_BS:(2 * d + 1) * LRU_BS])
            ig = _sigmoid(z[:, (2 * d + 1) * LRU_BS:(2 * d + 2) * LRU_BS])
            log_a = (-LRU_C) * r * sp[d:d + 1, cols]
            a = jnp.exp(log_a)
            a_ref[:, cols] = a
            gain2 = -jnp.tanh(log_a) * (a * a + 1.0)
            gain = jnp.where(gain2 > 0.0, gain2 * lax.rsqrt(gain2), 0.0)
            u_ref[:, cols] = gain * (ig * xcn)

    if has_init:
        h0f = h0f_ref[0, 0]
        h0b = h0b_ref[0, 0]
    else:
        h0f = jnp.zeros((1, cbw), f32)
        h0b = jnp.zeros((1, cbw), f32)

    def step(i, carry):
        hf, hb = carry
        base_f = pl.multiple_of(i * S, S)
        base_b = pl.multiple_of(L - S - i * S, S)
        for r in range(S):
            tf = pl.ds(base_f + r, 1)
            tb = pl.ds(base_b + (S - 1 - r), 1)
            hf = af_ref[tf, :] * hf + uf_ref[tf, :]
            hb = ab_ref[tb, :] * hb + ub_ref[tb, :]
            hf_ref[tf, :] = hf
            hb_ref[tb, :] = hb
        return hf, hb

    hf, hb = lax.fori_loop(0, L // S, step, (h0f, h0b))
    if not has_init:
        sf_ref[0, 0] = hf
        sb_ref[0, 0] = hb
    y_ref[...] = ((hf_ref[...] + hb_ref[...]) * _gelu_tanh(g_ref[...].astype(f32))).astype(bf16)


def lru_branch(proj, bufs, row0, n_seq, seq_len, cw, cb, w4, b4, lam, init, layer):
    has_init = init is not None
    cbw = LRU_CB_LONG if seq_len == DEC_SEQ else LRU_CB_SHORT
    ncb = D_RNN // cbw
    nb = cbw // LRU_BS
    rb0 = row0 // seq_len
    in_specs = [pl.BlockSpec((seq_len, cbw), lambda s, c: (rb0 + s, OFF_LRU_X // cbw + c)),
                pl.BlockSpec((seq_len, cbw), lambda s, c: (rb0 + s, OFF_LRU_G // cbw + c)),
                pl.BlockSpec((LRU_CONV, cbw), lambda s, c: (0, c)),
                pl.BlockSpec((1, cbw), lambda s, c: (0, c)),
                pl.BlockSpec((nb, LRU_BS, 4 * LRU_BS), lambda s, c: (c, 0, 0)),
                pl.BlockSpec((nb, 1, 4 * LRU_BS), lambda s, c: (c, 0, 0)),
                pl.BlockSpec((2, cbw), lambda s, c: (0, c))]
    args = [proj, proj, cw, cb, w4, b4, lam]
    state_spec = pl.BlockSpec((1, 1, 1, cbw), lambda s, c: (s, layer, 0, c))
    y_spec = pl.BlockSpec((seq_len, cbw), lambda s, c: (rb0 + s, c))
    if has_init:
        in_specs += [state_spec, state_spec]
        args += [init[0], init[1]]
        out_specs, out_shape = y_spec, _BRANCH_BUF
    else:
        st = jax.ShapeDtypeStruct((n_seq, DEPTH, 1, D_RNN), f32)
        out_specs, out_shape = [y_spec, state_spec, state_spec], [_BRANCH_BUF, st, st]
    scratch = [pltpu.VMEM((seq_len, cbw), f32) for _ in range(6)]
    return _inplace_call(functools.partial(_lru_kernel, seq_len=seq_len, has_init=has_init, cbw=cbw), bufs,
                         (n_seq, ncb), in_specs, args, out_specs, out_shape, scratch, "lru_branch")


def _ret_kernel(*refs, seq_len, has_init):
    if has_init:
        (la_ref, q_ref, k_ref, v_ref, g_ref, gn_ref, s0f_ref, s0b_ref, y_ref, acc_ref) = refs
    else:
        (la_ref, q_ref, k_ref, v_ref, g_ref, gn_ref, y_ref, sf_ref, sb_ref, acc_ref) = refs
    T = min(RET_CHUNK, seq_len)
    nc = seq_len // T
    tt = lax.broadcasted_iota(jnp.int32, (T, T), 0)
    ss = lax.broadcasted_iota(jnp.int32, (T, T), 1)
    diff = (tt - ss).astype(f32)
    tcol = lax.broadcasted_iota(jnp.int32, (T, 1), 0).astype(f32)
    scale = RET_DK ** -0.5
    for h in range(RET_HEADS):
        la_f = la_ref[h, 0]
        la_b = la_ref[h, 1]
        kcols = pl.ds(h * RET_DK, RET_DK)
        vcols = pl.ds(h * RET_DV, RET_DV)
        dsum = (jnp.where(tt >= ss, jnp.exp(la_f * diff), 0.0)
                + jnp.where(ss >= tt, jnp.exp(-la_b * diff), 0.0))

        def chunk(c):
            rows = pl.ds(c * T, T)
            q = q_ref[rows, kcols].astype(f32)
            ks = k_ref[rows, kcols].astype(f32) * scale
            v = v_ref[rows, vcols].astype(f32)
            return rows, q, ks, v

        s_f = s0f_ref[0, 0, h] if has_init else None
        for c in range(nc):
            rows, q, ks, v = chunk(c)
            scores = _dot_nt(q.astype(bf16), ks.astype(bf16)) * dsum
            y = _dot(scores.astype(bf16), v.astype(bf16))
            if s_f is not None:
                y = y + _dot((q * jnp.exp(la_f * (tcol + 1.0))).astype(bf16), s_f.astype(bf16))
            acc_ref[rows, :] = y
            if c < nc - 1 or not has_init:
                upd = _dot(ks.T.astype(bf16), (v * jnp.exp(la_f * (T - 1.0 - tcol))).astype(bf16))
                s_f = upd if s_f is None else jnp.exp(la_f * T) * s_f + upd
        s_b = s0b_ref[0, 0, h] if has_init else None
        for c in reversed(range(nc)):
            rows, q, ks, v = chunk(c)
            if s_b is not None:
                acc_ref[rows, :] += _dot((q * jnp.exp(la_b * (T - tcol))).astype(bf16), s_b.astype(bf16))
            if c > 0 or not has_init:
                upd = _dot(ks.T.astype(bf16), (v * jnp.exp(la_b * tcol)).astype(bf16))
                s_b = upd if s_b is None else jnp.exp(la_b * T) * s_b + upd
        if not has_init:
            sf_ref[0, 0, h] = s_f
            sb_ref[0, 0, h] = s_b
        y = acc_ref[...]
        mu = jnp.mean(y, axis=-1, keepdims=True)
        yc = y - mu
        var = jnp.mean(yc * yc, axis=-1, keepdims=True)
        y = yc * lax.rsqrt(var + EPS) * gn_ref[:, vcols]
        y_ref[:, vcols] = (y * _silu(g_ref[:, vcols].astype(f32))).astype(bf16)


def retention_branch(proj, bufs, row0, n_seq, seq_len, la, gn, init, layer):
    has_init = init is not None
    rb0 = row0 // seq_len
    qk_w, vg_w = RET_HEADS * RET_DK, RET_HEADS * RET_DV
    in_specs = [pl.BlockSpec(memory_space=pltpu.SMEM),
                pl.BlockSpec((seq_len, qk_w), lambda s: (rb0 + s, OFF_RET_Q // qk_w)),
                pl.BlockSpec((seq_len, qk_w), lambda s: (rb0 + s, OFF_RET_K // qk_w)),
                pl.BlockSpec((seq_len, vg_w), lambda s: (rb0 + s, OFF_RET_V // vg_w)),
                pl.BlockSpec((seq_len, vg_w), lambda s: (rb0 + s, OFF_RET_G // vg_w)),
                pl.BlockSpec((1, vg_w), lambda s: (0, 0))]
    args = [la, proj, proj, proj, proj, gn]
    state_spec = pl.BlockSpec((1, 1, RET_HEADS, RET_DK, RET_DV), lambda s: (s, layer, 0, 0, 0))
    y_spec = pl.BlockSpec((seq_len, vg_w), lambda s: (rb0 + s, 0))
    if has_init:
        in_specs += [state_spec, state_spec]
        args += [init[0], init[1]]
        out_specs, out_shape = y_spec, _BRANCH_BUF
    else:
        st = jax.ShapeDtypeStruct((n_seq, DEPTH, RET_HEADS, RET_DK, RET_DV), f32)
        out_specs, out_shape = [y_spec, state_spec, state_spec], [_BRANCH_BUF, st, st]
    return _inplace_call(functools.partial(_ret_kernel, seq_len=seq_len, has_init=has_init), bufs,
                         (n_seq,), in_specs, args, out_specs, out_shape,
                         [pltpu.VMEM((seq_len, RET_DV), f32)], "retention_branch")


SSD_GW = SSD_HPG * SSD_P


def _split3(x):
    h1 = x.astype(bf16)
    r1 = x - h1.astype(f32)
    h2 = r1.astype(bf16)
    h3 = (r1 - h2.astype(f32)).astype(bf16)
    return h1, h2, h3


def _dot_exact_rhs(m, x):
    h1, h2, h3 = _split3(x)
    return _dot(m, h1) + _dot(m, h2) + _dot(m, h3)


def _dot_exact_lhs(x, m):
    h1, h2, h3 = _split3(x)
    return _dot(h1, m) + _dot(h2, m) + _dot(h3, m)


def _heads_to_lanes(s_ref, st_ref):
    for hh in range(SSD_HPG):
        st_ref[:, pl.ds(hh * SSD_P, SSD_P)] = s_ref[0, 0, hh]
    return st_ref[...]


def _ssd_kernel(*refs, seq_len, has_init):
    if has_init:
        (z_ref, x_ref, b_ref, c_ref, dt_ref, cwx_ref, cbx_ref, cwb_ref, cbb_ref, cwc_ref, cbc_ref,
         prm_ref, dvec_ref, ng_ref, s0f_ref, s0b_ref,
         y_ref, xs_ref, bs_ref, cs_ref, acc_ref, rb_ref, dts_ref, yn_ref, st_ref) = refs
    else:
        (z_ref, x_ref, b_ref, c_ref, dt_ref, cwx_ref, cbx_ref, cwb_ref, cbb_ref, cwc_ref, cbc_ref,
         prm_ref, dvec_ref, ng_ref,
         y_ref, sf_ref, sb_ref, xs_ref, bs_ref, cs_ref, acc_ref, rb_ref, dts_ref, yn_ref, st_ref) = refs
    g = pl.program_id(1)
    L = seq_len
    T = min(SSD_CHUNK, L)
    nc = L // T
    H = SSD_HPG
    xs_ref[...] = _silu(_dwconv4(x_ref[...].astype(f32), cwx_ref[...], cbx_ref[...]))
    bs_ref[...] = _silu(_dwconv4(b_ref[...].astype(f32), cwb_ref[...], cbb_ref[...]))
    cs_ref[...] = _silu(_dwconv4(c_ref[...].astype(f32), cwc_ref[...], cbc_ref[...]))
    prm = prm_ref[0]
    a_neg = -jnp.exp(prm[1:2, :])
    tt = lax.broadcasted_iota(jnp.int32, (T, T), 0)
    ss = lax.broadcasted_iota(jnp.int32, (T, T), 1)
    lower = tt >= ss
    upper = ss >= tt
    tri_l = jnp.where(lower, 1.0, 0.0).astype(bf16)
    tri_u = jnp.where(upper, 1.0, 0.0).astype(bf16)
    lower_b = lower[:SSD_TB, :SSD_TB]
    upper_b = upper[:SSD_TB, :SSD_TB]
    er = lax.broadcasted_iota(jnp.int32, (V7X_LANES, SSD_GW), 0)
    ec = lax.broadcasted_iota(jnp.int32, (V7X_LANES, SSD_GW), 1) // SSD_P
    exp_f = jnp.where(er == ec, 1.0, 0.0).astype(bf16)
    exp_b = jnp.where(er == ec + H, 1.0, 0.0).astype(bf16)

    def expand(w, e):
        hi = w.astype(bf16)
        lo = (w - hi.astype(f32)).astype(bf16)
        return _dot(hi, e) + _dot(lo, e)

    s_f = _heads_to_lanes(s0f_ref, st_ref) if has_init else None
    for c in range(nc):
        rows = pl.ds(c * T, T)
        dt = _softplus(dt_ref[rows, :].astype(f32) + prm[0:1, :])
        dts_ref[rows, :] = dt
        da = dt * a_neg
        cum = _dot_exact_rhs(tri_l, da)
        rsum = (cum[T - 1:T, :] - cum) + da
        rb_ref[rows, :] = rsum
        da_t = da.T
        cum_t = _dot_exact_lhs(da_t, tri_u)
        rsum_t = (cum_t[:, T - 1:T] - cum_t) + da_t
        dt_t = dt.T
        bmat = bs_ref[rows, :]
        cmat = cs_ref[rows, :]
        xmat = xs_ref[rows, :]
        gmat = _dot_nt(cmat.astype(bf16), bmat.astype(bf16))
        for hh in range(H):
            cf, cf_t, dtf_t = cum[:, hh:hh + 1], cum_t[hh:hh + 1, :], dt_t[hh:hh + 1, :]
            rb, rb_t, dtb_t = rsum[:, H + hh:H + hh + 1], rsum_t[H + hh:H + hh + 1, :], dt_t[H + hh:H + hh + 1, :]
            block_rows = []
            for bi in range(T // SSD_TB):
                rr = slice(bi * SSD_TB, (bi + 1) * SSD_TB)
                blocks = []
                for bj in range(T // SSD_TB):
                    cc = slice(bj * SSD_TB, (bj + 1) * SSD_TB)
                    if bi > bj:
                        blk = jnp.exp(cf[rr] - cf_t[:, cc]) * dtf_t[:, cc]
                    elif bi < bj:
                        blk = jnp.exp(rb[rr] - rb_t[:, cc]) * dtb_t[:, cc]
                    else:
                        blk = (jnp.where(lower_b, jnp.exp(cf[rr] - cf_t[:, cc]), 0.0) * dtf_t[:, cc]
                               + jnp.where(upper_b, jnp.exp(rb[rr] - rb_t[:, cc]), 0.0) * dtb_t[:, cc])
                    blocks.append(gmat[rr, cc] * blk)
                block_rows.append(jnp.concatenate(blocks, axis=1))
            m = jnp.concatenate(block_rows, axis=0)
            xh = xmat[:, hh * SSD_P:(hh + 1) * SSD_P]
            acc_ref[rows, pl.ds(hh * SSD_P, SSD_P)] = _dot(m.astype(bf16), xh.astype(bf16))
        ecum = jnp.exp(cum)
        if s_f is not None:
            acc_ref[rows, :] += _dot(cmat.astype(bf16), s_f.astype(bf16)) * expand(ecum, exp_f)
        if c < nc - 1 or not has_init:
            tail = jnp.exp(cum[T - 1:T, :] - cum) * dt
            xw = (xmat * expand(tail, exp_f)).astype(bf16)
            upd = _dot(bmat.T.astype(bf16), xw)
            if s_f is None:
                s_f = upd
            else:
                s_f = s_f * expand(jnp.broadcast_to(ecum[T - 1:T, :], (V7X_SUBLANES, V7X_LANES)),
                                   exp_f)[0:1, :] + upd
    s_b = _heads_to_lanes(s0b_ref, st_ref) if has_init else None
    for c in reversed(range(nc)):
        rows = pl.ds(c * T, T)
        dt = dts_ref[rows, :]
        rsum = rb_ref[rows, :]
        ers = jnp.exp(rsum)
        bmat = bs_ref[rows, :]
        cmat = cs_ref[rows, :]
        xmat = xs_ref[rows, :]
        if s_b is not None:
            acc_ref[rows, :] += _dot(cmat.astype(bf16), s_b.astype(bf16)) * expand(ers, exp_b)
        if c > 0 or not has_init:
            tail = jnp.exp(rsum[0:1, :] - rsum) * dt
            xw = (xmat * expand(tail, exp_b)).astype(bf16)
            upd = _dot(bmat.T.astype(bf16), xw)
            if s_b is None:
                s_b = upd
            else:
                s_b = s_b * expand(jnp.broadcast_to(ers[0:1, :], (V7X_SUBLANES, V7X_LANES)),
                                   exp_b)[0:1, :] + upd
    if not has_init:
        for hh in range(H):
            sf_ref[0, 0, hh] = s_f[:, hh * SSD_P:(hh + 1) * SSD_P]
            sb_ref[0, 0, hh] = s_b[:, hh * SSD_P:(hh + 1) * SSD_P]
    yg = (acc_ref[...] + xs_ref[...] * dvec_ref[...]) * _silu(z_ref[...].astype(f32))
    yn_ref[g] = yg

    @pl.when(g == SSD_GROUPS - 1)
    def _():
        ssq = None
        for gg in range(SSD_GROUPS):
            y = yn_ref[gg]
            s = jnp.sum(y * y, axis=-1, keepdims=True)
            ssq = s if ssq is None else ssq + s
        inv = lax.rsqrt(ssq * (1.0 / D_SSD) + EPS)
        for gg in range(SSD_GROUPS):
            cols = pl.ds(gg * SSD_GW, SSD_GW)
            y_ref[:, cols] = (yn_ref[gg] * inv * ng_ref[:, cols]).astype(bf16)


def ssd_branch(proj, bufs, row0, n_seq, seq_len, conv_w, conv_b, prm, dvec, ng, init, layer):
    has_init = init is not None
    rb0 = row0 // seq_len
    xoff = OFF_SSD_XBC
    boff = OFF_SSD_XBC + D_SSD
    coff = boff + SSD_GROUPS * SSD_N
    in_specs = [pl.BlockSpec((seq_len, SSD_GW), lambda s, g: (rb0 + s, OFF_SSD_Z // SSD_GW + g)),
                pl.BlockSpec((seq_len, SSD_GW), lambda s, g: (rb0 + s, xoff // SSD_GW + g)),
                pl.BlockSpec((seq_len, SSD_N), lambda s, g: (rb0 + s, boff // SSD_N + g)),
                pl.BlockSpec((seq_len, SSD_N), lambda s, g: (rb0 + s, coff // SSD_N + g)),
                pl.BlockSpec((seq_len, V7X_LANES), lambda s, g: (rb0 + s, OFF_SSD_DT // V7X_LANES + g)),
                pl.BlockSpec((SSD_CONV, SSD_GW), lambda s, g: (0, g)),
                pl.BlockSpec((1, SSD_GW), lambda s, g: (0, g)),
                pl.BlockSpec((SSD_CONV, SSD_N), lambda s, g: (0, D_SSD // SSD_N + g)),
                pl.BlockSpec((1, SSD_N), lambda s, g: (0, D_SSD // SSD_N + g)),
                pl.BlockSpec((SSD_CONV, SSD_N), lambda s, g: (0, D_SSD // SSD_N + SSD_GROUPS + g)),
                pl.BlockSpec((1, SSD_N), lambda s, g: (0, D_SSD // SSD_N + SSD_GROUPS + g)),
                pl.BlockSpec((1, V7X_SUBLANES, V7X_LANES), lambda s, g: (g, 0, 0)),
                pl.BlockSpec((1, SSD_GW), lambda s, g: (0, g)),
                pl.BlockSpec((1, D_SSD), lambda s, g: (0, 0))]
    args = [proj, proj, proj, proj, proj, conv_w, conv_b, conv_w, conv_b, conv_w, conv_b, prm, dvec, ng]
    state_spec = pl.BlockSpec((1, 1, SSD_HPG, SSD_N, SSD_P), lambda s, g: (s, layer, g, 0, 0))
    y_spec = pl.BlockSpec((seq_len, D_SSD), lambda s, g: (rb0 + s, 0))
    if has_init:
        in_specs += [state_spec, state_spec]
        args += [init[0], init[1]]
        out_specs, out_shape = y_spec, _BRANCH_BUF
    else:
        st = jax.ShapeDtypeStruct((n_seq, DEPTH, SSD_HEADS, SSD_N, SSD_P), f32)
        out_specs, out_shape = [y_spec, state_spec, state_spec], [_BRANCH_BUF, st, st]
    scratch = [pltpu.VMEM((seq_len, SSD_GW), f32),
               pltpu.VMEM((seq_len, SSD_N), f32),
               pltpu.VMEM((seq_len, SSD_N), f32),
               pltpu.VMEM((seq_len, SSD_GW), f32),
               pltpu.VMEM((seq_len, V7X_LANES), f32),
               pltpu.VMEM((seq_len, V7X_LANES), f32),
               pltpu.VMEM((SSD_GROUPS, seq_len, SSD_GW), f32),
               pltpu.VMEM((SSD_N, SSD_GW), f32)]
    return _inplace_call(functools.partial(_ssd_kernel, seq_len=seq_len, has_init=has_init), bufs,
                         (n_seq, SSD_GROUPS), in_specs, args, out_specs, out_shape, scratch, "ssd_branch")


def _head_rmsnorm(x, g):
    return x * lax.rsqrt(jnp.mean(x * x, axis=-1, keepdims=True) + EPS) * g


CTX_HB = 4
CTX_W = CTX_HB * NA_HD


def _ctx_attn_kernel(q_ref, k_ref, v_ref, qg_ref, kg_ref, y_ref, ko_ref, vo_ref):
    scale = NA_HD ** -0.5
    vo_ref[0, 0] = v_ref[...].astype(f32)
    for h in range(CTX_HB):
        cols = pl.ds(h * NA_HD, NA_HD)
        q = _head_rmsnorm(q_ref[:, cols].astype(f32), qg_ref[...])
        k = _head_rmsnorm(k_ref[:, cols].astype(f32), kg_ref[...])
        ko_ref[0, 0, :, cols] = k
        s = _dot_nt(q.astype(bf16), k.astype(bf16)) * scale
        p = jnp.exp(s - jnp.max(s, axis=-1, keepdims=True))
        o = _dot(p.astype(bf16), v_ref[:, cols]) / jnp.sum(p, axis=-1, keepdims=True)
        y_ref[:, cols] = o.astype(bf16)


def context_attention(proj, bufs, row0, qg, kg, layer):
    rb0 = row0 // SEQ
    spec = lambda off: pl.BlockSpec((SEQ, CTX_W), lambda s, hb: (rb0 + s, off // CTX_W + hb))
    gspec = pl.BlockSpec((1, NA_HD), lambda s, hb: (0, 0))
    kv_spec = pl.BlockSpec((1, 1, SEQ, CTX_W), lambda s, hb: (s, layer, 0, hb))
    kv_shape = jax.ShapeDtypeStruct((BATCH, DEPTH, SEQ, NA_W), f32)
    return _inplace_call(
        _ctx_attn_kernel, bufs, (BATCH, NA_HEADS // CTX_HB),
        [spec(OFF_NA_Q), spec(OFF_NA_K), spec(OFF_NA_V), gspec, gspec], [proj, proj, proj, qg, kg],
        [pl.BlockSpec((SEQ, CTX_W), lambda s, hb: (rb0 + s, hb)), kv_spec, kv_spec],
        [_BRANCH_BUF, kv_shape, kv_shape], [], "context_attention")


NA_ROWS = DEC_SEQ // GRID_W
NA_NK = NA_WR * GRID_W


def _na_row_start(r):
    return min(max(r - NA_WR // 2, 0), NA_ROWS - NA_WR)


def _rope(x, cos, sin_signed):
    lane = lax.broadcasted_iota(jnp.int32, x.shape, 1)
    quarter = NA_HD // 4
    swapped = jnp.where((lane & (2 * quarter - 1)) < quarter,
                        pltpu.roll(x, NA_HD - quarter, axis=1), pltpu.roll(x, quarter, axis=1))
    return x * cos + swapped * sin_signed


def _na_row_groups():
    groups, r = [], 0
    while r < NA_ROWS:
        n = 1
        while r + n < NA_ROWS and _na_row_start(r + n) == _na_row_start(r):
            n += 1
        groups.append((r, n, _na_row_start(r)))
        r += n
    return groups


def _na_attn_kernel(q_ref, k_ref, v_ref, kc_ref, vc_ref, qg_ref, kg_ref, cos_ref, sin_ref,
                    bias_ref, valid_ref, y_ref, qs_ref, ks_ref, s_ref, p_ref, oc_ref, w_ref):
    scale = NA_HD ** -0.5
    cos = cos_ref[...]
    sin = sin_ref[...]
    qs_ref[...] = _rope(_head_rmsnorm(q_ref[...].astype(f32), qg_ref[...]), cos, sin).astype(bf16)
    ks_ref[...] = _rope(_head_rmsnorm(k_ref[...].astype(f32), kg_ref[...]), cos, sin).astype(bf16)
    groups = _na_row_groups()
    valid = valid_ref[...] > 0.0
    for r0, n, rs in groups:
        rows = pl.ds(r0 * GRID_W, n * GRID_W)
        d0 = r0 - rs
        kw = ks_ref[pl.ds(rs * GRID_W, NA_NK), :]
        s = (_dot_nt(qs_ref[rows, :], kw) * scale).reshape(n, GRID_W, NA_NK) + bias_ref[0, d0:d0 + n]
        s_ref[rows, :] = jnp.where(valid[None], s, -1e30).reshape(n * GRID_W, NA_NK)
    s_ctx = _dot_nt(qs_ref[...], kc_ref[0, 0].astype(bf16)) * scale
    s_loc = s_ref[...]
    m = jnp.maximum(jnp.max(s_loc, axis=-1, keepdims=True), jnp.max(s_ctx, axis=-1, keepdims=True))
    p_loc = jnp.exp(s_loc - m)
    p_ctx = jnp.exp(s_ctx - m)
    w_ref[...] = 1.0 / (jnp.sum(p_loc, axis=-1, keepdims=True) + jnp.sum(p_ctx, axis=-1, keepdims=True))
    p_ref[...] = p_loc.astype(bf16)
    oc_ref[...] = _dot(p_ctx.astype(bf16), vc_ref[0, 0].astype(bf16))
    for r0, n, rs in groups:
        rows = pl.ds(r0 * GRID_W, n * GRID_W)
        vw = v_ref[pl.ds(rs * GRID_W, NA_NK), :]
        y_ref[rows, :] = ((_dot(p_ref[rows, :], vw) + oc_ref[rows, :]) * w_ref[rows, :]).astype(bf16)


def neighbourhood_attention(proj, cache_k, cache_v, layer, qg, kg, cos, sin, bias, valid):
    spec = lambda off: pl.BlockSpec((DEC_SEQ, NA_HD), lambda b, h: (b, off // NA_HD + h))
    cspec = pl.BlockSpec((1, 1, PAST_LEN, NA_HD), lambda b, h: (b, layer, 0, h))
    gspec = pl.BlockSpec((1, NA_HD), lambda b, h: (0, 0))
    tspec = pl.BlockSpec((DEC_SEQ, NA_HD), lambda b, h: (0, 0))
    return pl.pallas_call(
        _na_attn_kernel,
        grid=(DEC_BATCH, NA_HEADS),
        in_specs=[spec(OFF_NA_Q), spec(OFF_NA_K), spec(OFF_NA_V), cspec, cspec, gspec, gspec,
                  tspec, tspec,
                  pl.BlockSpec((1, NA_WR, GRID_W, NA_NK), lambda b, h: (h, 0, 0, 0)),
                  pl.BlockSpec((GRID_W, NA_NK), lambda b, h: (0, 0))],
        out_specs=pl.BlockSpec((DEC_SEQ, NA_HD), lambda b, h: (b, h)),
        out_shape=_BRANCH_BUF,
        scratch_shapes=[pltpu.VMEM((DEC_SEQ, NA_HD), bf16), pltpu.VMEM((DEC_SEQ, NA_HD), bf16),
                        pltpu.VMEM((DEC_SEQ, NA_NK), f32), pltpu.VMEM((DEC_SEQ, NA_NK), bf16),
                        pltpu.VMEM((DEC_SEQ, NA_HD), f32), pltpu.VMEM((DEC_SEQ, 1), f32)],
        compiler_params=_params(2), name="neighbourhood_attention",
    )(proj, proj, proj, cache_k, cache_v, qg, kg, cos, sin, bias, valid)


TAIL_TN = 512
N_TAIL_TILES = (D_IN_PAD - HEAD_COLS) // TAIL_TN
N_QKV_TILES = 3 * NA_W // TAIL_TN
DT_W = 2 * SSD_HEADS


def _tail_kernel(a_ref, b_ref, o_ref):
    j = pl.program_id(1)
    lane = lax.broadcasted_iota(jnp.int32, (D_MODEL, TAIL_TN), 1)

    @pl.when(j == 0)
    def _():
        o_ref[0] = a_ref[0]

    @pl.when((j >= 1) & (j <= N_QKV_TILES))
    def _():
        main = pltpu.roll(a_ref[0].astype(f32), TAIL_TN - DT_W, axis=1)
        nxt = pltpu.roll(b_ref[0].astype(f32), V7X_LANES - DT_W, axis=1)
        last_lane = lax.broadcasted_iota(jnp.int32, (D_MODEL, V7X_LANES), 1)
        last = jnp.where(last_lane < V7X_LANES - DT_W, main[:, TAIL_TN - V7X_LANES:], nxt)
        o_ref[0] = jnp.concatenate([main[:, :TAIL_TN - V7X_LANES], last], axis=1).astype(bf16)

    @pl.when(j == N_TAIL_TILES - 1)
    def _():
        a = a_ref[0].astype(f32)
        out = jnp.zeros((D_MODEL, TAIL_TN), f32)
        for gidx in range(SSD_GROUPS):
            base = gidx * V7X_LANES
            for src0, dst0 in ((gidx * SSD_HPG, base), (SSD_HEADS + gidx * SSD_HPG, base + SSD_HPG)):
                shift = (dst0 - src0) % TAIL_TN
                moved = a if shift == 0 else pltpu.roll(a, shift, axis=1)
                out = jnp.where((lane >= dst0) & (lane < dst0 + SSD_HPG), moved, out)
        o_ref[0] = out.astype(bf16)


def _split_w_in(w_in):
    w_in = w_in.astype(bf16)
    dt0 = IN_OFFSETS[7]
    a_tile0, dt_tile, b_tile0 = HEAD_COLS // TAIL_TN, dt0 // TAIL_TN, dt0 // V7X_LANES
    b_per_a = TAIL_TN // V7X_LANES
    last_b = (D_IN - 1) // V7X_LANES
    tail = pl.pallas_call(
        _tail_kernel, grid=(DEPTH, N_TAIL_TILES),
        in_specs=[pl.BlockSpec((1, D_MODEL, TAIL_TN),
                               lambda l, j: (l, 0, jnp.where(j == N_TAIL_TILES - 1, dt_tile, a_tile0 + j))),
                  pl.BlockSpec((1, D_MODEL, V7X_LANES),
                               lambda l, j: (l, 0, jnp.minimum(b_tile0 + b_per_a * j, last_b)))],
        out_specs=pl.BlockSpec((1, D_MODEL, TAIL_TN), lambda l, j: (l, 0, j)),
        out_shape=jax.ShapeDtypeStruct((DEPTH, D_MODEL, D_IN_PAD - HEAD_COLS), bf16),
        compiler_params=_params(2), name="build_w_in_tail",
    )(w_in, w_in)
    return w_in, tail


def _group_lanes(v):
    rows = []
    for gidx in range(SSD_GROUPS):
        sl = slice(gidx * SSD_HPG, (gidx + 1) * SSD_HPG)
        rows.append(jnp.concatenate([v[0, sl], v[1, sl], jnp.zeros((V7X_LANES - 2 * SSD_HPG,), f32)]))
    return jnp.stack(rows)


def _rope_tables():
    t = np.arange(DEC_SEQ)
    quarter = NA_HD // 4
    inv = ROPE_BASE ** (-np.arange(quarter, dtype=np.float32) / quarter)
    ang_r = (t // GRID_W).astype(np.float32)[:, None] * inv
    ang_c = (t % GRID_W).astype(np.float32)[:, None] * inv
    cos = np.concatenate([np.cos(ang_r), np.cos(ang_r), np.cos(ang_c), np.cos(ang_c)], axis=1)
    sin = np.concatenate([-np.sin(ang_r), np.sin(ang_r), -np.sin(ang_c), np.sin(ang_c)], axis=1)
    return jnp.asarray(cos, f32), jnp.asarray(sin, f32)


def _na_tables(rpb):
    cq = np.arange(GRID_W)
    kc = np.tile(np.arange(GRID_W), NA_WR)
    col_start = np.clip(cq - NA_WC // 2, 0, GRID_W - NA_WC)
    valid = (kc[None, :] >= col_start[:, None]) & (kc[None, :] < col_start[:, None] + NA_WC)
    col_off = np.clip(cq[None, :] - cq[:, None], 1 - NA_WC, NA_WC - 1) + NA_WC - 1
    onehot = (col_off[None, :, :] == np.arange(2 * NA_WC - 1)[:, None, None]).astype(np.float32)
    toep = jnp.einsum('hic,cqk->hiqk', rpb.astype(f32), jnp.asarray(onehot), precision=lax.Precision.HIGHEST)
    tables = []
    for d in range(NA_WR):
        rows = toep[:, NA_WR - 1 - d:2 * NA_WR - 1 - d]
        tables.append(rows.transpose(0, 2, 1, 3).reshape(NA_HEADS, GRID_W, NA_NK))
    return jnp.stack(tables, axis=1), jnp.asarray(valid, f32)


def kernel(x_prompt, x_sample, cache_na_k, cache_na_v, state_lru_f, state_lru_b,
           state_ret_f, state_ret_b, state_ssd_f, state_ssd_b, c, c_ctx,
           norm1_g, norm2_g, w_ada, b_ada, w_in, w_gate, b_gate, w_branch, w_out,
           lru_conv_w, lru_conv_b, lru_wa, lru_ba, lru_wx, lru_bx, lru_lambda,
           ret_gn_g, ssd_conv_w, ssd_conv_b, ssd_a_log, ssd_dt_bias, ssd_d, ssd_norm_g,
           na_q_g, na_k_g, na_rpb, ffn_w_up, ffn_conv_w, ffn_conv_b, ffn_w_down):
    xs0 = x_sample.reshape(T_SAMPLE, D_MODEL)
    xp0 = x_prompt.reshape(T_PROMPT, D_MODEL)
    n_ptiles = N_TILES - N_SAMPLE_TILES
    cond = jnp.concatenate([c, c_ctx[None, :], jnp.zeros((N_COND_PAD - N_COND, D_MODEL), f32)], axis=0)
    mod_all = ada_modulation(cond, w_ada, b_ada)
    cos, sin = _rope_tables()
    hh = jnp.arange(RET_HEADS, dtype=f32)
    ret_la = jnp.stack([jnp.log1p(-jnp.exp2(-5.0 - hh)), jnp.log1p(-jnp.exp2(-5.5 - hh))], axis=1)
    cache_k = cache_na_k.reshape(DEC_BATCH, DEPTH, PAST_LEN, NA_W)
    cache_v = cache_na_v.reshape(DEC_BATCH, DEPTH, PAST_LEN, NA_W)
    lru_init = (state_lru_f.reshape(DEC_BATCH, DEPTH, 1, D_RNN), state_lru_b.reshape(DEC_BATCH, DEPTH, 1, D_RNN))

    w_head_b, w_tail_b = _split_w_in(w_in)
    w_branch_b, w_out_b, w_down_b = w_branch.astype(bf16), w_out.astype(bf16), ffn_w_down.astype(bf16)

    new = {k: None for k in ("k", "v", "lru_f", "lru_b", "ret_f", "ret_b", "ssd_f", "ssd_b")}
    for l in range(DEPTH):
        mod = mod_all[l].reshape(N_COND_PAD, 1, N_MOD * D_MODEL)
        g1 = norm1_g[l][None, :]
        if l == 0:
            proj, xn = in_projection(xs0, 0, 0, N_SAMPLE_TILES, g1, mod, w_head_b, w_tail_b, l)
            proj, xn = in_projection(xp0, 0, N_SAMPLE_TILES, n_ptiles, g1, mod, w_head_b, w_tail_b, l, (proj, xn))
        else:
            proj, xn = in_projection(x, 0, 0, N_TILES, g1, mod, w_head_b, w_tail_b, l)

        w4 = jnp.concatenate([lru_wa[l, 0], lru_wx[l, 0], lru_wa[l, 1], lru_wx[l, 1]], axis=-1).astype(bf16)
        b4 = jnp.concatenate([lru_ba[l, 0].reshape(LRU_BLOCKS, 1, LRU_BS), lru_bx[l, 0].reshape(LRU_BLOCKS, 1, LRU_BS),
                              lru_ba[l, 1].reshape(LRU_BLOCKS, 1, LRU_BS), lru_bx[l, 1].reshape(LRU_BLOCKS, 1, LRU_BS)],
                             axis=-1)
        lru_args = (lru_conv_w[l], lru_conv_b[l][None, :], w4, b4, lru_lambda[l])
        y_lru = lru_branch(proj, (None,), 0, DEC_BATCH, DEC_SEQ, *lru_args, lru_init, l)
        y_lru, new["lru_f"], new["lru_b"] = lru_branch(
            proj, (y_lru, new["lru_f"], new["lru_b"]), T_SAMPLE, BATCH, SEQ, *lru_args, None, l)

        gn = ret_gn_g[l][None, :]
        y_ret = retention_branch(proj, (None,), 0, DEC_BATCH, DEC_SEQ, ret_la, gn, (state_ret_f, state_ret_b), l)
        y_ret, new["ret_f"], new["ret_b"] = retention_branch(
            proj, (y_ret, new["ret_f"], new["ret_b"]), T_SAMPLE, BATCH, SEQ, ret_la, gn, None, l)

        prm = jnp.stack([_group_lanes(ssd_dt_bias[l]), _group_lanes(ssd_a_log[l])], axis=1)
        prm = jnp.concatenate([prm, jnp.zeros((SSD_GROUPS, V7X_SUBLANES - 2, V7X_LANES), f32)], axis=1)
        dvec = jnp.repeat(ssd_d[l], SSD_P)[None, :]
        ssd_args = (ssd_conv_w[l], ssd_conv_b[l][None, :], prm, dvec, ssd_norm_g[l][None, :])
        y_ssd = ssd_branch(proj, (None,), 0, DEC_BATCH, DEC_SEQ, *ssd_args, (state_ssd_f, state_ssd_b), l)
        y_ssd, new["ssd_f"], new["ssd_b"] = ssd_branch(
            proj, (y_ssd, new["ssd_f"], new["ssd_b"]), T_SAMPLE, BATCH, SEQ, *ssd_args, None, l)

        qg = na_q_g[l][None, :]
        kg = na_k_g[l][None, :]
        bias, valid = _na_tables(na_rpb[l])
        y_na = neighbourhood_attention(proj, cache_k, cache_v, l, qg, kg, cos, sin, bias, valid)
        y_na, new["k"], new["v"] = context_attention(proj, (y_na, new["k"], new["v"]), T_SAMPLE, qg, kg, l)

        merged = merge_branches(xn, (y_lru, y_ret, y_ssd, y_na), w_gate, b_gate[l][None, :], w_branch_b, l)
        if l == 0:
            x = residual_projection(merged, w_out_b, l, xs0, mod, 2, 0, N_SAMPLE_TILES)
            x = residual_projection(merged, w_out_b, l, xp0, mod, 2, N_SAMPLE_TILES, n_ptiles, x_tile0=0, buf=x)
        else:
            x = residual_projection(merged, w_out_b, l, x, mod, 2)
        hmid = ffn_up(x, norm2_g[l][None, :], mod, ffn_w_up, l, ffn_conv_w[l], ffn_conv_b[l][None, :])
        if l < DEPTH - 1:
            x = residual_projection(hmid, w_down_b, l, x, mod, 5)
        else:
            y_sample = residual_projection(hmid, w_down_b, l, x, mod, 5, 0, N_SAMPLE_TILES, out_rows=T_SAMPLE)
            y_prompt = residual_projection(hmid, w_down_b, l, x, mod, 5, N_SAMPLE_TILES, n_ptiles,
                                           out_tile0=0, out_rows=T_PROMPT)

    kv_shape = (BATCH, DEPTH, SEQ, NA_HEADS, NA_HD)
    return (y_prompt.reshape(BATCH, SEQ, D_MODEL), y_sample.reshape(DEC_BATCH, DEC_SEQ, D_MODEL),
            new["k"].reshape(kv_shape), new["v"].reshape(kv_shape),
            new["lru_f"].reshape(BATCH, DEPTH, D_RNN), new["lru_b"].reshape(BATCH, DEPTH, D_RNN),
            new["ret_f"], new["ret_b"], new["ssd_f"], new["ssd_b"])
```

```python
import functools
import math

import jax
import jax.numpy as jnp
import numpy as np
from jax import lax
from jax.experimental import pallas as pl
from jax.experimental.pallas import tpu as pltpu

D_MODEL = 2048
BATCH = 16
SEQ = 256
DEPTH = 2
DEC_BATCH = 8
DEC_SEQ = 1024
PAST_LEN = 256
GRID_W = 64
EPS = 1e-6
N_BRANCH = 4
BRANCH_W = 1024
N_MOD = 6
D_RNN = 1024
LRU_BLOCKS = 8
LRU_BS = D_RNN // LRU_BLOCKS
LRU_CONV = 4
LRU_C = 8.0
RET_HEADS = 4
RET_DK = 128
RET_DV = 256
SSD_HEADS = 16
SSD_P = 64
SSD_N = 128
SSD_GROUPS = 2
SSD_CONV = 4
D_SSD = SSD_HEADS * SSD_P
SSD_CONV_CH = D_SSD + 2 * SSD_GROUPS * SSD_N
NA_HEADS = 8
NA_HD = 128
NA_W = NA_HEADS * NA_HD
NA_WR = 8
NA_WC = 16
ROPE_BASE = 10000.0
D_FF = 5632
FFN_CONV = 3
IN_SIZES = (D_RNN, D_RNN,
            RET_HEADS * RET_DK, RET_HEADS * RET_DK, RET_HEADS * RET_DV, RET_HEADS * RET_DV,
            D_SSD, SSD_CONV_CH, 2 * SSD_HEADS,
            NA_W, NA_W, NA_W)
D_IN = sum(IN_SIZES)
IN_OFFSETS = tuple(int(s) for s in np.cumsum(IN_SIZES)[:-1])

V7X_LANES = 128
V7X_SUBLANES = 8
V7X_VMEM_LIMIT_BYTES = 56 * 1024 * 1024

TM = 1024
T_SAMPLE = DEC_BATCH * DEC_SEQ
T_PROMPT = BATCH * SEQ
T_ALL = T_SAMPLE + T_PROMPT
N_SAMPLE_TILES = T_SAMPLE // TM
N_TILES = T_ALL // TM
N_COND = DEC_BATCH + 1
N_COND_PAD = 16

SSD_HPG = SSD_HEADS // SSD_GROUPS
OFF_LRU_X = 0
OFF_LRU_G = OFF_LRU_X + D_RNN
OFF_RET_Q = OFF_LRU_G + D_RNN
OFF_RET_K = OFF_RET_Q + RET_HEADS * RET_DK
OFF_RET_V = OFF_RET_K + RET_HEADS * RET_DK
OFF_RET_G = OFF_RET_V + RET_HEADS * RET_DV
OFF_SSD_Z = OFF_RET_G + RET_HEADS * RET_DV
OFF_SSD_XBC = OFF_SSD_Z + D_SSD
OFF_NA_Q = OFF_SSD_XBC + SSD_CONV_CH
OFF_NA_K = OFF_NA_Q + NA_W
OFF_NA_V = OFF_NA_K + NA_W
OFF_SSD_DT = OFF_NA_V + NA_W
PROJ_TN = 1024
D_IN_PAD = -(-(OFF_SSD_DT + SSD_GROUPS * V7X_LANES) // PROJ_TN) * PROJ_TN
N_HEAD_TILES = OFF_NA_Q // PROJ_TN
HEAD_COLS = N_HEAD_TILES * PROJ_TN

RET_CHUNK = 256
SSD_CHUNK = 256
SSD_TB = 128

f32 = jnp.float32
bf16 = jnp.bfloat16

_ARB = "arbitrary"


def _params(n_axes):
    return pltpu.CompilerParams(dimension_semantics=(_ARB,) * n_axes,
                                vmem_limit_bytes=V7X_VMEM_LIMIT_BYTES)


def _mod_spec(k, tile0=0):
    return pl.BlockSpec((1, 1, D_MODEL), lambda i, j: (jnp.minimum(tile0 + i, N_SAMPLE_TILES), 0, k))


def _dot(a, b):
    return jnp.dot(a, b, preferred_element_type=f32)


def _dot_nt(a, b):
    return lax.dot_general(a, b, (((1,), (1,)), ((), ())), preferred_element_type=f32)


def _sigmoid(x):
    return 0.5 * jnp.tanh(0.5 * x) + 0.5


def _silu(x):
    return x * _sigmoid(x)


def _gelu_tanh(x):
    return 0.5 * x * (1.0 + jnp.tanh(math.sqrt(2.0 / math.pi) * (x + 0.044715 * (x * x * x))))


def _shift_rows(v, k):
    n = v.shape[0]
    row = lax.broadcasted_iota(jnp.int32, (n, 1), 0)
    if k == 1:
        return jnp.where(row == 0, 0.0, pltpu.roll(v, 1, axis=0))
    return jnp.where(row == n - 1, 0.0, pltpu.roll(v, n - 1, axis=0))


def _dwconv4(x, cw, cb):
    acc = _shift_rows(x * cw[0:1, :], 1) + x * cw[1:2, :]
    return cb + x * cw[2:3, :] + _shift_rows(acc, 1) + _shift_rows(x * cw[3:4, :], -1)


def _softplus(x):
    return jnp.maximum(x, 0.0) + jnp.log1p(jnp.exp(-jnp.abs(x)))


def _drop_refs(body, n, *refs):
    body(*refs[n:])


def _inplace_call(body, bufs, grid, in_specs, args, out_specs, out_shape, scratch, name):
    held = [(k, b) for k, b in enumerate(bufs) if b is not None]
    if held:
        body = functools.partial(_drop_refs, body, len(held))
        in_specs = [pl.BlockSpec(memory_space=pl.ANY)] * len(held) + list(in_specs)
        args = [b for _, b in held] + list(args)
    aliases = {pos: k for pos, (k, _) in enumerate(held)}
    return pl.pallas_call(
        body, grid=grid, in_specs=in_specs, out_specs=out_specs, out_shape=out_shape,
        scratch_shapes=scratch, input_output_aliases=aliases,
        compiler_params=_params(len(grid)), name=name,
    )(*args)


_BRANCH_BUF = jax.ShapeDtypeStruct((T_ALL, BRANCH_W), bf16)


ADA_TN = 1024


def _ada_kernel(c_ref, w_ref, b_ref, o_ref):
    c = _silu(c_ref[...]).astype(bf16)
    o_ref[0] = _dot(c, w_ref[0].astype(bf16)) + b_ref[0]


def ada_modulation(cond, w_ada, b_ada):
    n = N_MOD * D_MODEL
    return pl.pallas_call(
        _ada_kernel,
        grid=(DEPTH, n // ADA_TN),
        in_specs=[pl.BlockSpec((N_COND_PAD, D_MODEL), lambda l, j: (0, 0)),
                  pl.BlockSpec((1, D_MODEL, ADA_TN), lambda l, j: (l, 0, j)),
                  pl.BlockSpec((1, 1, ADA_TN), lambda l, j: (l, 0, j))],
        out_specs=pl.BlockSpec((1, N_COND_PAD, ADA_TN), lambda l, j: (l, 0, j)),
        out_shape=jax.ShapeDtypeStruct((DEPTH, N_COND_PAD, n), f32),
        compiler_params=_params(2),
        name="ada_modulation",
    )(cond, w_ada, b_ada.reshape(DEPTH, 1, n))


NORM_ROWS = 128
NORM_SPLIT = 2


def _modulated_norm_parts(x_ref, g_ref, sc_ref, sh_ref, xn_ref):
    gs = g_ref[...] * (1.0 + sc_ref[0])
    sh = sh_ref[0]
    part = TM // NORM_SPLIT
    for p in range(NORM_SPLIT):
        pieces = []
        for r in range(part // NORM_ROWS):
            rows = pl.ds(p * part + r * NORM_ROWS, NORM_ROWS)
            x = x_ref[rows, :]
            y = x * lax.rsqrt(jnp.mean(x * x, axis=-1, keepdims=True) + EPS)
            pieces.append((y * gs + sh).astype(bf16))
        xn = jnp.concatenate(pieces, axis=0)
        xn_ref[pl.ds(p * part, part), :] = xn
        yield xn


def _in_proj_kernel(x_ref, g_ref, sc_ref, sh_ref, wh_ref, wt_ref, o_ref, xn_ref):
    j = pl.program_id(1)

    @pl.when(j == 0)
    def _():
        w = wh_ref[0]
        outs = [_dot(xn, w).astype(bf16) for xn in _modulated_norm_parts(x_ref, g_ref, sc_ref, sh_ref, xn_ref)]
        o_ref[...] = jnp.concatenate(outs, axis=0)

    @pl.when((j > 0) & (j < N_HEAD_TILES))
    def _():
        o_ref[...] = _dot(xn_ref[...], wh_ref[0]).astype(bf16)

    @pl.when(j >= N_HEAD_TILES)
    def _():
        o_ref[...] = _dot(xn_ref[...], wt_ref[0]).astype(bf16)


def in_projection(x, x_tile0, tile0, n_tiles, g, mod, w_head, w_tail, layer, bufs=(None, None)):
    n = D_IN_PAD
    n_tail = w_tail.shape[2] // PROJ_TN
    return _inplace_call(
        _in_proj_kernel, bufs, (n_tiles, n // PROJ_TN),
        [pl.BlockSpec((TM, D_MODEL), lambda i, j: (x_tile0 + i, 0)),
         pl.BlockSpec((1, D_MODEL), lambda i, j: (0, 0)),
         _mod_spec(1, tile0), _mod_spec(0, tile0),
         pl.BlockSpec((1, D_MODEL, PROJ_TN), lambda i, j: (layer, 0, jnp.minimum(j, N_HEAD_TILES - 1))),
         pl.BlockSpec((1, D_MODEL, PROJ_TN),
                      lambda i, j: (layer, 0, jnp.clip(j - N_HEAD_TILES, 0, n_tail - 1)))],
        [x, g, mod, mod, w_head, w_tail],
        [pl.BlockSpec((TM, PROJ_TN), lambda i, j: (tile0 + i, j)),
         pl.BlockSpec((TM, D_MODEL), lambda i, j: (tile0 + i, 0))],
        [jax.ShapeDtypeStruct((T_ALL, n), bf16), jax.ShapeDtypeStruct((T_ALL, D_MODEL), bf16)],
        [], "in_projection")


MERGE_TN = 256


def _merge_kernel(xn_ref, *refs):
    y_refs, wg_refs, bg_refs = refs[0:4], refs[4:8], refs[8:12]
    wb_ref, o_ref = refs[12], refs[13]
    xn = xn_ref[...]
    acc = None
    for n in range(N_BRANCH):
        gate = _sigmoid(_dot(xn, wg_refs[n][0].astype(bf16)) + bg_refs[n][...])
        term = gate * _dot(y_refs[n][...], wb_ref[0, n])
        acc = term if acc is None else acc + term
    o_ref[...] = acc.astype(bf16)


def merge_branches(xn, ys, w_gate, b_gate, w_branch, layer):
    nj = D_MODEL // MERGE_TN
    y_spec = pl.BlockSpec((TM, BRANCH_W), lambda i, j: (i, 0))
    wg_specs = [pl.BlockSpec((1, D_MODEL, MERGE_TN), lambda i, j, n=n: (layer, 0, n * nj + j))
                for n in range(N_BRANCH)]
    bg_specs = [pl.BlockSpec((1, MERGE_TN), lambda i, j, n=n: (0, n * nj + j)) for n in range(N_BRANCH)]
    return pl.pallas_call(
        _merge_kernel,
        grid=(N_TILES, nj),
        in_specs=[pl.BlockSpec((TM, D_MODEL), lambda i, j: (i, 0)),
                  y_spec, y_spec, y_spec, y_spec, *wg_specs, *bg_specs,
                  pl.BlockSpec((1, N_BRANCH, BRANCH_W, MERGE_TN), lambda i, j: (layer, 0, 0, j))],
        out_specs=pl.BlockSpec((TM, MERGE_TN), lambda i, j: (i, j)),
        out_shape=jax.ShapeDtypeStruct((T_ALL, D_MODEL), bf16),
        compiler_params=_params(2),
        name="merge_branches",
    )(xn, *ys, *([w_gate] * N_BRANCH), *([b_gate] * N_BRANCH), w_branch)


RES_TN = 512
RES_TN_SHORT_K = 1024


def _residual_kernel(a_ref, w_ref, x_ref, gv_ref, o_ref):
    o_ref[...] = x_ref[...] + gv_ref[0] * _dot(a_ref[...], w_ref[0])


def residual_projection(a, w, layer, x, mod, k_mod, tile0=0, n_tiles=N_TILES, x_tile0=None, out_tile0=None,
                        out_rows=T_ALL, buf=None):
    kdim = a.shape[1]
    tn = RES_TN_SHORT_K if kdim == D_MODEL else RES_TN
    x_tile0 = tile0 if x_tile0 is None else x_tile0
    out_tile0 = tile0 if out_tile0 is None else out_tile0
    return _inplace_call(
        _residual_kernel, (buf,), (n_tiles, D_MODEL // tn),
        [pl.BlockSpec((TM, kdim), lambda i, j: (tile0 + i, 0)),
         pl.BlockSpec((1, kdim, tn), lambda i, j: (layer, 0, j)),
         pl.BlockSpec((TM, tn), lambda i, j: (x_tile0 + i, j)),
         pl.BlockSpec((1, 1, tn),
                      lambda i, j: (jnp.minimum(tile0 + i, N_SAMPLE_TILES), 0, k_mod * (D_MODEL // tn) + j))],
        [a, w, x, mod],
        pl.BlockSpec((TM, tn), lambda i, j: (out_tile0 + i, j)),
        jax.ShapeDtypeStruct((out_rows, D_MODEL), f32), [], "residual_projection")


FFN_TN = 512


def _ffn_up_kernel(x_ref, g_ref, sc_ref, sh_ref, wa_ref, wv_ref, cw_ref, cb_ref, o_ref, xn_ref):
    i = pl.program_id(0)

    def epilogue(a, v):
        seq_len = jnp.where(i < N_SAMPLE_TILES, DEC_SEQ, SEQ)
        pos = lax.broadcasted_iota(jnp.int32, (TM, 1), 0) & (seq_len - 1)
        prev = jnp.where(pos == 0, 0.0, pltpu.roll(a, 1, axis=0))
        nxt = jnp.where(pos == seq_len - 1, 0.0, pltpu.roll(a, TM - 1, axis=0))
        cw = cw_ref[...]
        conv = cb_ref[...] + prev * cw[0:1, :] + a * cw[1:2, :] + nxt * cw[2:3, :]
        o_ref[...] = (_gelu_tanh(conv) * v).astype(bf16)

    @pl.when(pl.program_id(1) == 0)
    def _():
        wa = wa_ref[0].astype(bf16)
        wv = wv_ref[0].astype(bf16)
        parts = [(_dot(xn, wa), _dot(xn, wv)) for xn in _modulated_norm_parts(x_ref, g_ref, sc_ref, sh_ref, xn_ref)]
        epilogue(jnp.concatenate([p[0] for p in parts], axis=0), jnp.concatenate([p[1] for p in parts], axis=0))

    @pl.when(pl.program_id(1) > 0)
    def _():
        xn = xn_ref[...]
        epilogue(_dot(xn, wa_ref[0].astype(bf16)), _dot(xn, wv_ref[0].astype(bf16)))


def ffn_up(x, g, mod, w_up, layer, conv_w, conv_b):
    nj = D_FF // FFN_TN
    return pl.pallas_call(
        _ffn_up_kernel,
        grid=(N_TILES, nj),
        in_specs=[pl.BlockSpec((TM, D_MODEL), lambda i, j: (i, 0)),
                  pl.BlockSpec((1, D_MODEL), lambda i, j: (0, 0)),
                  _mod_spec(4), _mod_spec(3),
                  pl.BlockSpec((1, D_MODEL, FFN_TN), lambda i, j: (layer, 0, j)),
                  pl.BlockSpec((1, D_MODEL, FFN_TN), lambda i, j: (layer, 0, nj + j)),
                  pl.BlockSpec((FFN_CONV, FFN_TN), lambda i, j: (0, j)),
                  pl.BlockSpec((1, FFN_TN), lambda i, j: (0, j))],
        out_specs=pl.BlockSpec((TM, FFN_TN), lambda i, j: (i, j)),
        out_shape=jax.ShapeDtypeStruct((T_ALL, D_FF), bf16),
        scratch_shapes=[pltpu.VMEM((TM, D_MODEL), bf16)],
        compiler_params=_params(2),
        name="ffn_up",
    )(x, g, mod, mod, w_up, w_up, conv_w, conv_b)


LRU_CB_LONG = D_RNN
LRU_CB_SHORT = D_RNN


def _lru_kernel(*refs, seq_len, has_init, cbw):
    if has_init:
        (x_ref, g_ref, cw_ref, cb_ref, w4_ref, b4_ref, lam_ref, h0f_ref, h0b_ref,
         y_ref, af_ref, uf_ref, ab_ref, ub_ref, hf_ref, hb_ref) = refs
    else:
        (x_ref, g_ref, cw_ref, cb_ref, w4_ref, b4_ref, lam_ref,
         y_ref, sf_ref, sb_ref, af_ref, uf_ref, ab_ref, ub_ref, hf_ref, hb_ref) = refs
    L = seq_len
    S = V7X_SUBLANES
    xc = _dwconv4(x_ref[...].astype(f32), cw_ref[...], cb_ref[...])
    sp = _softplus(-lam_ref[...])
    for n in range(cbw // LRU_BS):
        cols = slice(n * LRU_BS, (n + 1) * LRU_BS)
        xcn = xc[:, cols]
        z = _dot(xcn.astype(bf16), w4_ref[n]) + b4_ref[n]
        for d, (a_ref, u_ref) in enumerate(((af_ref, uf_ref), (ab_ref, ub_ref))):
            r = _sigmoid(z[:, (2 * d) * LRU_BS:(2 * d + 1) * LRU_BS])
            ig = _sigmoid(z[:, (2 * d + 1) * LRU_BS:(2 * d + 2) * LRU_BS])
            log_a = (-LRU_C) * r * sp[d:d + 1, cols]
            a = jnp.exp(log_a)
            a_ref[:, cols] = a
            gain2 = -jnp.tanh(log_a) * (a * a + 1.0)
            gain = jnp.where(gain2 > 0.0, gain2 * lax.rsqrt(gain2), 0.0)
            u_ref[:, cols] = gain * (ig * xcn)

    if has_init:
        h0f = h0f_ref[0, 0]
        h0b = h0b_ref[0, 0]
    else:
        h0f = jnp.zeros((1, cbw), f32)
        h0b = jnp.zeros((1, cbw), f32)

    def step(i, carry):
        hf, hb = carry
        base_f = pl.multiple_of(i * S, S)
        base_b = pl.multiple_of(L - S - i * S, S)
        for r in range(S):
            tf = pl.ds(base_f + r, 1)
            tb = pl.ds(base_b + (S - 1 - r), 1)
            hf = af_ref[tf, :] * hf + uf_ref[tf, :]
            hb = ab_ref[tb, :] * hb + ub_ref[tb, :]
            hf_ref[tf, :] = hf
            hb_ref[tb, :] = hb
        return hf, hb

    hf, hb = lax.fori_loop(0, L // S, step, (h0f, h0b))
    if not has_init:
        sf_ref[0, 0] = hf
        sb_ref[0, 0] = hb
    y_ref[...] = ((hf_ref[...] + hb_ref[...]) * _gelu_tanh(g_ref[...].astype(f32))).astype(bf16)


def lru_branch(proj, bufs, row0, n_seq, seq_len, cw, cb, w4, b4, lam, init, layer):
    has_init = init is not None
    cbw = LRU_CB_LONG if seq_len == DEC_SEQ else LRU_CB_SHORT
    ncb = D_RNN // cbw
    nb = cbw // LRU_BS
    rb0 = row0 // seq_len
    in_specs = [pl.BlockSpec((seq_len, cbw), lambda s, c: (rb0 + s, OFF_LRU_X // cbw + c)),
                pl.BlockSpec((seq_len, cbw), lambda s, c: (rb0 + s, OFF_LRU_G // cbw + c)),
                pl.BlockSpec((LRU_CONV, cbw), lambda s, c: (0, c)),
                pl.BlockSpec((1, cbw), lambda s, c: (0, c)),
                pl.BlockSpec((nb, LRU_BS, 4 * LRU_BS), lambda s, c: (c, 0, 0)),
                pl.BlockSpec((nb, 1, 4 * LRU_BS), lambda s, c: (c, 0, 0)),
                pl.BlockSpec((2, cbw), lambda s, c: (0, c))]
    args = [proj, proj, cw, cb, w4, b4, lam]
    state_spec = pl.BlockSpec((1, 1, 1, cbw), lambda s, c: (s, layer, 0, c))
    y_spec = pl.BlockSpec((seq_len, cbw), lambda s, c: (rb0 + s, c))
    if has_init:
        in_specs += [state_spec, state_spec]
        args += [init[0], init[1]]
        out_specs, out_shape = y_spec, _BRANCH_BUF
    else:
        st = jax.ShapeDtypeStruct((n_seq, DEPTH, 1, D_RNN), f32)
        out_specs, out_shape = [y_spec, state_spec, state_spec], [_BRANCH_BUF, st, st]
    scratch = [pltpu.VMEM((seq_len, cbw), f32) for _ in range(6)]
    return _inplace_call(functools.partial(_lru_kernel, seq_len=seq_len, has_init=has_init, cbw=cbw), bufs,
                         (n_seq, ncb), in_specs, args, out_specs, out_shape, scratch, "lru_branch")


def _ret_kernel(*refs, seq_len, has_init):
    if has_init:
        (la_ref, q_ref, k_ref, v_ref, g_ref, gn_ref, s0f_ref, s0b_ref, y_ref, acc_ref) = refs
    else:
        (la_ref, q_ref, k_ref, v_ref, g_ref, gn_ref, y_ref, sf_ref, sb_ref, acc_ref) = refs
    T = min(RET_CHUNK, seq_len)
    nc = seq_len // T
    tt = lax.broadcasted_iota(jnp.int32, (T, T), 0)
    ss = lax.broadcasted_iota(jnp.int32, (T, T), 1)
    diff = (tt - ss).astype(f32)
    tcol = lax.broadcasted_iota(jnp.int32, (T, 1), 0).astype(f32)
    scale = RET_DK ** -0.5
    for h in range(RET_HEADS):
        la_f = la_ref[h, 0]
        la_b = la_ref[h, 1]
        kcols = pl.ds(h * RET_DK, RET_DK)
        vcols = pl.ds(h * RET_DV, RET_DV)
        dsum = (jnp.where(tt >= ss, jnp.exp(la_f * diff), 0.0)
                + jnp.where(ss >= tt, jnp.exp(-la_b * diff), 0.0))

        def chunk(c):
            rows = pl.ds(c * T, T)
            q = q_ref[rows, kcols].astype(f32)
            ks = k_ref[rows, kcols].astype(f32) * scale
            v = v_ref[rows, vcols].astype(f32)
            return rows, q, ks, v

        s_f = s0f_ref[0, 0, h] if has_init else None
        for c in range(nc):
            rows, q, ks, v = chunk(c)
            scores = _dot_nt(q.astype(bf16), ks.astype(bf16)) * dsum
            y = _dot(scores.astype(bf16), v.astype(bf16))
            if s_f is not None:
                y = y + _dot((q * jnp.exp(la_f * (tcol + 1.0))).astype(bf16), s_f.astype(bf16))
            acc_ref[rows, :] = y
            if c < nc - 1 or not has_init:
                upd = _dot(ks.T.astype(bf16), (v * jnp.exp(la_f * (T - 1.0 - tcol))).astype(bf16))
                s_f = upd if s_f is None else jnp.exp(la_f * T) * s_f + upd
        s_b = s0b_ref[0, 0, h] if has_init else None
        for c in reversed(range(nc)):
            rows, q, ks, v = chunk(c)
            if s_b is not None:
                acc_ref[rows, :] += _dot((q * jnp.exp(la_b * (T - tcol))).astype(bf16), s_b.astype(bf16))
            if c > 0 or not has_init:
                upd = _dot(ks.T.astype(bf16), (v * jnp.exp(la_b * tcol)).astype(bf16))
                s_b = upd if s_b is None else jnp.exp(la_b * T) * s_b + upd
        if not has_init:
            sf_ref[0, 0, h] = s_f
            sb_ref[0, 0, h] = s_b
        y = acc_ref[...]
        mu = jnp.mean(y, axis=-1, keepdims=True)
        yc = y - mu
        var = jnp.mean(yc * yc, axis=-1, keepdims=True)
        y = yc * lax.rsqrt(var + EPS) * gn_ref[:, vcols]
        y_ref[:, vcols] = (y * _silu(g_ref[:, vcols].astype(f32))).astype(bf16)


def retention_branch(proj, bufs, row0, n_seq, seq_len, la, gn, init, layer):
    has_init = init is not None
    rb0 = row0 // seq_len
    qk_w, vg_w = RET_HEADS * RET_DK, RET_HEADS * RET_DV
    in_specs = [pl.BlockSpec(memory_space=pltpu.SMEM),
                pl.BlockSpec((seq_len, qk_w), lambda s: (rb0 + s, OFF_RET_Q // qk_w)),
                pl.BlockSpec((seq_len, qk_w), lambda s: (rb0 + s, OFF_RET_K // qk_w)),
                pl.BlockSpec((seq_len, vg_w), lambda s: (rb0 + s, OFF_RET_V // vg_w)),
                pl.BlockSpec((seq_len, vg_w), lambda s: (rb0 + s, OFF_RET_G // vg_w)),
                pl.BlockSpec((1, vg_w), lambda s: (0, 0))]
    args = [la, proj, proj, proj, proj, gn]
    state_spec = pl.BlockSpec((1, 1, RET_HEADS, RET_DK, RET_DV), lambda s: (s, layer, 0, 0, 0))
    y_spec = pl.BlockSpec((seq_len, vg_w), lambda s: (rb0 + s, 0))
    if has_init:
        in_specs += [state_spec, state_spec]
        args += [init[0], init[1]]
        out_specs, out_shape = y_spec, _BRANCH_BUF
    else:
        st = jax.ShapeDtypeStruct((n_seq, DEPTH, RET_HEADS, RET_DK, RET_DV), f32)
        out_specs, out_shape = [y_spec, state_spec, state_spec], [_BRANCH_BUF, st, st]
    return _inplace_call(functools.partial(_ret_kernel, seq_len=seq_len, has_init=has_init), bufs,
                         (n_seq,), in_specs, args, out_specs, out_shape,
                         [pltpu.VMEM((seq_len, RET_DV), f32)], "retention_branch")


SSD_GW = SSD_HPG * SSD_P


def _split3(x):
    h1 = x.astype(bf16)
    r1 = x - h1.astype(f32)
    h2 = r1.astype(bf16)
    h3 = (r1 - h2.astype(f32)).astype(bf16)
    return h1, h2, h3


def _dot_exact_rhs(m, x):
    h1, h2, h3 = _split3(x)
    return _dot(m, h1) + _dot(m, h2) + _dot(m, h3)


def _dot_exact_lhs(x, m):
    h1, h2, h3 = _split3(x)
    return _dot(h1, m) + _dot(h2, m) + _dot(h3, m)


def _heads_to_lanes(s_ref, st_ref):
    for hh in range(SSD_HPG):
        st_ref[:, pl.ds(hh * SSD_P, SSD_P)] = s_ref[0, 0, hh]
    return st_ref[...]


def _ssd_kernel(*refs, seq_len, has_init):
    if has_init:
        (z_ref, x_ref, b_ref, c_ref, dt_ref, cwx_ref, cbx_ref, cwb_ref, cbb_ref, cwc_ref, cbc_ref,
         prm_ref, dvec_ref, ng_ref, s0f_ref, s0b_ref,
         y_ref, xs_ref, bs_ref, cs_ref, acc_ref, rb_ref, dts_ref, yn_ref, st_ref) = refs
    else:
        (z_ref, x_ref, b_ref, c_ref, dt_ref, cwx_ref, cbx_ref, cwb_ref, cbb_ref, cwc_ref, cbc_ref,
         prm_ref, dvec_ref, ng_ref,
         y_ref, sf_ref, sb_ref, xs_ref, bs_ref, cs_ref, acc_ref, rb_ref, dts_ref, yn_ref, st_ref) = refs
    g = pl.program_id(1)
    L = seq_len
    T = min(SSD_CHUNK, L)
    nc = L // T
    H = SSD_HPG
    xs_ref[...] = _silu(_dwconv4(x_ref[...].astype(f32), cwx_ref[...], cbx_ref[...]))
    bs_ref[...] = _silu(_dwconv4(b_ref[...].astype(f32), cwb_ref[...], cbb_ref[...]))
    cs_ref[...] = _silu(_dwconv4(c_ref[...].astype(f32), cwc_ref[...], cbc_ref[...]))
    prm = prm_ref[0]
    a_neg = -jnp.exp(prm[1:2, :])
    tt = lax.broadcasted_iota(jnp.int32, (T, T), 0)
    ss = lax.broadcasted_iota(jnp.int32, (T, T), 1)
    lower = tt >= ss
    upper = ss >= tt
    tri_l = jnp.where(lower, 1.0, 0.0).astype(bf16)
    tri_u = jnp.where(upper, 1.0, 0.0).astype(bf16)
    lower_b = lower[:SSD_TB, :SSD_TB]
    upper_b = upper[:SSD_TB, :SSD_TB]
    er = lax.broadcasted_iota(jnp.int32, (V7X_LANES, SSD_GW), 0)
    ec = lax.broadcasted_iota(jnp.int32, (V7X_LANES, SSD_GW), 1) // SSD_P
    exp_f = jnp.where(er == ec, 1.0, 0.0).astype(bf16)
    exp_b = jnp.where(er == ec + H, 1.0, 0.0).astype(bf16)

    def expand(w, e):
        hi = w.astype(bf16)
        lo = (w - hi.astype(f32)).astype(bf16)
        return _dot(hi, e) + _dot(lo, e)

    s_f = _heads_to_lanes(s0f_ref, st_ref) if has_init else None
    for c in range(nc):
        rows = pl.ds(c * T, T)
        dt = _softplus(dt_ref[rows, :].astype(f32) + prm[0:1, :])
        dts_ref[rows, :] = dt
        da = dt * a_neg
        cum = _dot_exact_rhs(tri_l, da)
        rsum = (cum[T - 1:T, :] - cum) + da
        rb_ref[rows, :] = rsum
        da_t = da.T
        cum_t = _dot_exact_lhs(da_t, tri_u)
        rsum_t = (cum_t[:, T - 1:T] - cum_t) + da_t
        dt_t = dt.T
        bmat = bs_ref[rows, :]
        cmat = cs_ref[rows, :]
        xmat = xs_ref[rows, :]
        gmat = _dot_nt(cmat.astype(bf16), bmat.astype(bf16))
        for hh in range(H):
            cf, cf_t, dtf_t = cum[:, hh:hh + 1], cum_t[hh:hh + 1, :], dt_t[hh:hh + 1, :]
            rb, rb_t, dtb_t = rsum[:, H + hh:H + hh + 1], rsum_t[H + hh:H + hh + 1, :], dt_t[H + hh:H + hh + 1, :]
            block_rows = []
            for bi in range(T // SSD_TB):
                rr = slice(bi * SSD_TB, (bi + 1) * SSD_TB)
                blocks = []
                for bj in range(T // SSD_TB):
                    cc = slice(bj * SSD_TB, (bj + 1) * SSD_TB)
                    if bi > bj:
                        blk = jnp.exp(cf[rr] - cf_t[:, cc]) * dtf_t[:, cc]
                    elif bi < bj:
                        blk = jnp.exp(rb[rr] - rb_t[:, cc]) * dtb_t[:, cc]
                    else:
                        blk = (jnp.where(lower_b, jnp.exp(cf[rr] - cf_t[:, cc]), 0.0) * dtf_t[:, cc]
                               + jnp.where(upper_b, jnp.exp(rb[rr] - rb_t[:, cc]), 0.0) * dtb_t[:, cc])
                    blocks.append(gmat[rr, cc] * blk)
                block_rows.append(jnp.concatenate(blocks, axis=1))
            m = jnp.concatenate(block_rows, axis=0)
            xh = xmat[:, hh * SSD_P:(hh + 1) * SSD_P]
            acc_ref[rows, pl.ds(hh * SSD_P, SSD_P)] = _dot(m.astype(bf16), xh.astype(bf16))
        ecum = jnp.exp(cum)
        if s_f is not None:
            acc_ref[rows, :] += _dot(cmat.astype(bf16), s_f.astype(bf16)) * expand(ecum, exp_f)
        if c < nc - 1 or not has_init:
            tail = jnp.exp(cum[T - 1:T, :] - cum) * dt
            xw = (xmat * expand(tail, exp_f)).astype(bf16)
            upd = _dot(bmat.T.astype(bf16), xw)
            if s_f is None:
                s_f = upd
            else:
                s_f = s_f * expand(jnp.broadcast_to(ecum[T - 1:T, :], (V7X_SUBLANES, V7X_LANES)),
                                   exp_f)[0:1, :] + upd
    s_b = _heads_to_lanes(s0b_ref, st_ref) if has_init else None
    for c in reversed(range(nc)):
        rows = pl.ds(c * T, T)
        dt = dts_ref[rows, :]
        rsum = rb_ref[rows, :]
        ers = jnp.exp(rsum)
        bmat = bs_ref[rows, :]
        cmat = cs_ref[rows, :]
        xmat = xs_ref[rows, :]
        if s_b is not None:
            acc_ref[rows, :] += _dot(cmat.astype(bf16), s_b.astype(bf16)) * expand(ers, exp_b)
        if c > 0 or not has_init:
            tail = jnp.exp(rsum[0:1, :] - rsum) * dt
            xw = (xmat * expand(tail, exp_b)).astype(bf16)
            upd = _dot(bmat.T.astype(bf16), xw)
            if s_b is None:
                s_b = upd
            else:
                s_b = s_b * expand(jnp.broadcast_to(ers[0:1, :], (V7X_SUBLANES, V7X_LANES)),
                                   exp_b)[0:1, :] + upd
    if not has_init:
        for hh in range(H):
            sf_ref[0, 0, hh] = s_f[:, hh * SSD_P:(hh + 1) * SSD_P]
            sb_ref[0, 0, hh] = s_b[:, hh * SSD_P:(hh + 1) * SSD_P]
    yg = (acc_ref[...] + xs_ref[...] * dvec_ref[...]) * _silu(z_ref[...].astype(f32))
    yn_ref[g] = yg

    @pl.when(g == SSD_GROUPS - 1)
    def _():
        ssq = None
        for gg in range(SSD_GROUPS):
            y = yn_ref[gg]
            s = jnp.sum(y * y, axis=-1, keepdims=True)
            ssq = s if ssq is None else ssq + s
        inv = lax.rsqrt(ssq * (1.0 / D_SSD) + EPS)
        for gg in range(SSD_GROUPS):
            cols = pl.ds(gg * SSD_GW, SSD_GW)
            y_ref[:, cols] = (yn_ref[gg] * inv * ng_ref[:, cols]).astype(bf16)


def ssd_branch(proj, bufs, row0, n_seq, seq_len, conv_w, conv_b, prm, dvec, ng, init, layer):
    has_init = init is not None
    rb0 = row0 // seq_len
    xoff = OFF_SSD_XBC
    boff = OFF_SSD_XBC + D_SSD
    coff = boff + SSD_GROUPS * SSD_N
    in_specs = [pl.BlockSpec((seq_len, SSD_GW), lambda s, g: (rb0 + s, OFF_SSD_Z // SSD_GW + g)),
                pl.BlockSpec((seq_len, SSD_GW), lambda s, g: (rb0 + s, xoff // SSD_GW + g)),
                pl.BlockSpec((seq_len, SSD_N), lambda s, g: (rb0 + s, boff // SSD_N + g)),
                pl.BlockSpec((seq_len, SSD_N), lambda s, g: (rb0 + s, coff // SSD_N + g)),
                pl.BlockSpec((seq_len, V7X_LANES), lambda s, g: (rb0 + s, OFF_SSD_DT // V7X_LANES + g)),
                pl.BlockSpec((SSD_CONV, SSD_GW), lambda s, g: (0, g)),
                pl.BlockSpec((1, SSD_GW), lambda s, g: (0, g)),
                pl.BlockSpec((SSD_CONV, SSD_N), lambda s, g: (0, D_SSD // SSD_N + g)),
                pl.BlockSpec((1, SSD_N), lambda s, g: (0, D_SSD // SSD_N + g)),
                pl.BlockSpec((SSD_CONV, SSD_N), lambda s, g: (0, D_SSD // SSD_N + SSD_GROUPS + g)),
                pl.BlockSpec((1, SSD_N), lambda s, g: (0, D_SSD // SSD_N + SSD_GROUPS + g)),
                pl.BlockSpec((1, V7X_SUBLANES, V7X_LANES), lambda s, g: (g, 0, 0)),
                pl.BlockSpec((1, SSD_GW), lambda s, g: (0, g)),
                pl.BlockSpec((1, D_SSD), lambda s, g: (0, 0))]
    args = [proj, proj, proj, proj, proj, conv_w, conv_b, conv_w, conv_b, conv_w, conv_b, prm, dvec, ng]
    state_spec = pl.BlockSpec((1, 1, SSD_HPG, SSD_N, SSD_P), lambda s, g: (s, layer, g, 0, 0))
    y_spec = pl.BlockSpec((seq_len, D_SSD), lambda s, g: (rb0 + s, 0))
    if has_init:
        in_specs += [state_spec, state_spec]
        args += [init[0], init[1]]
        out_specs, out_shape = y_spec, _BRANCH_BUF
    else:
        st = jax.ShapeDtypeStruct((n_seq, DEPTH, SSD_HEADS, SSD_N, SSD_P), f32)
        out_specs, out_shape = [y_spec, state_spec, state_spec], [_BRANCH_BUF, st, st]
    scratch = [pltpu.VMEM((seq_len, SSD_GW), f32),
               pltpu.VMEM((seq_len, SSD_N), f32),
               pltpu.VMEM((seq_len, SSD_N), f32),
               pltpu.VMEM((seq_len, SSD_GW), f32),
               pltpu.VMEM((seq_len, V7X_LANES), f32),
               pltpu.VMEM((seq_len, V7X_LANES), f32),
               pltpu.VMEM((SSD_GROUPS, seq_len, SSD_GW), f32),
               pltpu.VMEM((SSD_N, SSD_GW), f32)]
    return _inplace_call(functools.partial(_ssd_kernel, seq_len=seq_len, has_init=has_init), bufs,
                         (n_seq, SSD_GROUPS), in_specs, args, out_specs, out_shape, scratch, "ssd_branch")


def _head_rmsnorm(x, g):
    return x * lax.rsqrt(jnp.mean(x * x, axis=-1, keepdims=True) + EPS) * g


CTX_HB = 4
CTX_W = CTX_HB * NA_HD


def _ctx_attn_kernel(q_ref, k_ref, v_ref, qg_ref, kg_ref, y_ref, ko_ref, vo_ref):
    scale = NA_HD ** -0.5
    vo_ref[0, 0] = v_ref[...].astype(f32)
    for h in range(CTX_HB):
        cols = pl.ds(h * NA_HD, NA_HD)
        q = _head_rmsnorm(q_ref[:, cols].astype(f32), qg_ref[...])
        k = _head_rmsnorm(k_ref[:, cols].astype(f32), kg_ref[...])
        ko_ref[0, 0, :, cols] = k
        s = _dot_nt(q.astype(bf16), k.astype(bf16)) * scale
        p = jnp.exp(s - jnp.max(s, axis=-1, keepdims=True))
        o = _dot(p.astype(bf16), v_ref[:, cols]) / jnp.sum(p, axis=-1, keepdims=True)
        y_ref[:, cols] = o.astype(bf16)


def context_attention(proj, bufs, row0, qg, kg, layer):
    rb0 = row0 // SEQ
    spec = lambda off: pl.BlockSpec((SEQ, CTX_W), lambda s, hb: (rb0 + s, off // CTX_W + hb))
    gspec = pl.BlockSpec((1, NA_HD), lambda s, hb: (0, 0))
    kv_spec = pl.BlockSpec((1, 1, SEQ, CTX_W), lambda s, hb: (s, layer, 0, hb))
    kv_shape = jax.ShapeDtypeStruct((BATCH, DEPTH, SEQ, NA_W), f32)
    return _inplace_call(
        _ctx_attn_kernel, bufs, (BATCH, NA_HEADS // CTX_HB),
        [spec(OFF_NA_Q), spec(OFF_NA_K), spec(OFF_NA_V), gspec, gspec], [proj, proj, proj, qg, kg],
        [pl.BlockSpec((SEQ, CTX_W), lambda s, hb: (rb0 + s, hb)), kv_spec, kv_spec],
        [_BRANCH_BUF, kv_shape, kv_shape], [], "context_attention")


NA_ROWS = DEC_SEQ // GRID_W
NA_NK = NA_WR * GRID_W


def _na_row_start(r):
    return min(max(r - NA_WR // 2, 0), NA_ROWS - NA_WR)


def _rope(x, cos, sin_signed):
    lane = lax.broadcasted_iota(jnp.int32, x.shape, 1)
    quarter = NA_HD // 4
    swapped = jnp.where((lane & (2 * quarter - 1)) < quarter,
                        pltpu.roll(x, NA_HD - quarter, axis=1), pltpu.roll(x, quarter, axis=1))
    return x * cos + swapped * sin_signed


def _na_row_groups():
    groups, r = [], 0
    while r < NA_ROWS:
        n = 1
        while r + n < NA_ROWS and _na_row_start(r + n) == _na_row_start(r):
            n += 1
        groups.append((r, n, _na_row_start(r)))
        r += n
    return groups


def _na_attn_kernel(q_ref, k_ref, v_ref, kc_ref, vc_ref, qg_ref, kg_ref, cos_ref, sin_ref,
                    bias_ref, valid_ref, y_ref, qs_ref, ks_ref, s_ref, p_ref, oc_ref, w_ref):
    scale = NA_HD ** -0.5
    cos = cos_ref[...]
    sin = sin_ref[...]
    qs_ref[...] = _rope(_head_rmsnorm(q_ref[...].astype(f32), qg_ref[...]), cos, sin).astype(bf16)
    ks_ref[...] = _rope(_head_rmsnorm(k_ref[...].astype(f32), kg_ref[...]), cos, sin).astype(bf16)
    groups = _na_row_groups()
    valid = valid_ref[...] > 0.0
    for r0, n, rs in groups:
        rows = pl.ds(r0 * GRID_W, n * GRID_W)
        d0 = r0 - rs
        kw = ks_ref[pl.ds(rs * GRID_W, NA_NK), :]
        s = (_dot_nt(qs_ref[rows, :], kw) * scale).reshape(n, GRID_W, NA_NK) + bias_ref[0, d0:d0 + n]
        s_ref[rows, :] = jnp.where(valid[None], s, -1e30).reshape(n * GRID_W, NA_NK)
    s_ctx = _dot_nt(qs_ref[...], kc_ref[0, 0].astype(bf16)) * scale
    s_loc = s_ref[...]
    m = jnp.maximum(jnp.max(s_loc, axis=-1, keepdims=True), jnp.max(s_ctx, axis=-1, keepdims=True))
    p_loc = jnp.exp(s_loc - m)
    p_ctx = jnp.exp(s_ctx - m)
    w_ref[...] = 1.0 / (jnp.sum(p_loc, axis=-1, keepdims=True) + jnp.sum(p_ctx, axis=-1, keepdims=True))
    p_ref[...] = p_loc.astype(bf16)
    oc_ref[...] = _dot(p_ctx.astype(bf16), vc_ref[0, 0].astype(bf16))
    for r0, n, rs in groups:
        rows = pl.ds(r0 * GRID_W, n * GRID_W)
        vw = v_ref[pl.ds(rs * GRID_W, NA_NK), :]
        y_ref[rows, :] = ((_dot(p_ref[rows, :], vw) + oc_ref[rows, :]) * w_ref[rows, :]).astype(bf16)


def neighbourhood_attention(proj, cache_k, cache_v, layer, qg, kg, cos, sin, bias, valid):
    spec = lambda off: pl.BlockSpec((DEC_SEQ, NA_HD), lambda b, h: (b, off // NA_HD + h))
    cspec = pl.BlockSpec((1, 1, PAST_LEN, NA_HD), lambda b, h: (b, layer, 0, h))
    gspec = pl.BlockSpec((1, NA_HD), lambda b, h: (0, 0))
    tspec = pl.BlockSpec((DEC_SEQ, NA_HD), lambda b, h: (0, 0))
    return pl.pallas_call(
        _na_attn_kernel,
        grid=(DEC_BATCH, NA_HEADS),
        in_specs=[spec(OFF_NA_Q), spec(OFF_NA_K), spec(OFF_NA_V), cspec, cspec, gspec, gspec,
                  tspec, tspec,
                  pl.BlockSpec((1, NA_WR, GRID_W, NA_NK), lambda b, h: (h, 0, 0, 0)),
                  pl.BlockSpec((GRID_W, NA_NK), lambda b, h: (0, 0))],
        out_specs=pl.BlockSpec((DEC_SEQ, NA_HD), lambda b, h: (b, h)),
        out_shape=_BRANCH_BUF,
        scratch_shapes=[pltpu.VMEM((DEC_SEQ, NA_HD), bf16), pltpu.VMEM((DEC_SEQ, NA_HD), bf16),
                        pltpu.VMEM((DEC_SEQ, NA_NK), f32), pltpu.VMEM((DEC_SEQ, NA_NK), bf16),
                        pltpu.VMEM((DEC_SEQ, NA_HD), f32), pltpu.VMEM((DEC_SEQ, 1), f32)],
        compiler_params=_params(2), name="neighbourhood_attention",
    )(proj, proj, proj, cache_k, cache_v, qg, kg, cos, sin, bias, valid)


TAIL_TN = 512
N_TAIL_TILES = (D_IN_PAD - HEAD_COLS) // TAIL_TN
N_QKV_TILES = 3 * NA_W // TAIL_TN
DT_W = 2 * SSD_HEADS


def _tail_kernel(a_ref, b_ref, o_ref):
    j = pl.program_id(1)
    lane = lax.broadcasted_iota(jnp.int32, (D_MODEL, TAIL_TN), 1)

    @pl.when(j == 0)
    def _():
        o_ref[0] = a_ref[0]

    @pl.when((j >= 1) & (j <= N_QKV_TILES))
    def _():
        main = pltpu.roll(a_ref[0].astype(f32), TAIL_TN - DT_W, axis=1)
        nxt = pltpu.roll(b_ref[0].astype(f32), V7X_LANES - DT_W, axis=1)
        last_lane = lax.broadcasted_iota(jnp.int32, (D_MODEL, V7X_LANES), 1)
        last = jnp.where(last_lane < V7X_LANES - DT_W, main[:, TAIL_TN - V7X_LANES:], nxt)
        o_ref[0] = jnp.concatenate([main[:, :TAIL_TN - V7X_LANES], last], axis=1).astype(bf16)

    @pl.when(j == N_TAIL_TILES - 1)
    def _():
        a = a_ref[0].astype(f32)
        out = jnp.zeros((D_MODEL, TAIL_TN), f32)
        for gidx in range(SSD_GROUPS):
            base = gidx * V7X_LANES
            for src0, dst0 in ((gidx * SSD_HPG, base), (SSD_HEADS + gidx * SSD_HPG, base + SSD_HPG)):
                shift = (dst0 - src0) % TAIL_TN
                moved = a if shift == 0 else pltpu.roll(a, shift, axis=1)
                out = jnp.where((lane >= dst0) & (lane < dst0 + SSD_HPG), moved, out)
        o_ref[0] = out.astype(bf16)


def _split_w_in(w_in):
    w_in = w_in.astype(bf16)
    dt0 = IN_OFFSETS[7]
    a_tile0, dt_tile, b_tile0 = HEAD_COLS // TAIL_TN, dt0 // TAIL_TN, dt0 // V7X_LANES
    b_per_a = TAIL_TN // V7X_LANES
    last_b = (D_IN - 1) // V7X_LANES
    tail = pl.pallas_call(
        _tail_kernel, grid=(DEPTH, N_TAIL_TILES),
        in_specs=[pl.BlockSpec((1, D_MODEL, TAIL_TN),
                               lambda l, j: (l, 0, jnp.where(j == N_TAIL_TILES - 1, dt_tile, a_tile0 + j))),
                  pl.BlockSpec((1, D_MODEL, V7X_LANES),
                               lambda l, j: (l, 0, jnp.minimum(b_tile0 + b_per_a * j, last_b)))],
        out_specs=pl.BlockSpec((1, D_MODEL, TAIL_TN), lambda l, j: (l, 0, j)),
        out_shape=jax.ShapeDtypeStruct((DEPTH, D_MODEL, D_IN_PAD - HEAD_COLS), bf16),
        compiler_params=_params(2), name="build_w_in_tail",
    )(w_in, w_in)
    return w_in, tail


def _group_lanes(v):
    rows = []
    for gidx in range(SSD_GROUPS):
        sl = slice(gidx * SSD_HPG, (gidx + 1) * SSD_HPG)
        rows.append(jnp.concatenate([v[0, sl], v[1, sl], jnp.zeros((V7X_LANES - 2 * SSD_HPG,), f32)]))
    return jnp.stack(rows)


def _rope_tables():
    t = np.arange(DEC_SEQ)
    quarter = NA_HD // 4
    inv = ROPE_BASE ** (-np.arange(quarter, dtype=np.float32) / quarter)
    ang_r = (t // GRID_W).astype(np.float32)[:, None] * inv
    ang_c = (t % GRID_W).astype(np.float32)[:, None] * inv
    cos = np.concatenate([np.cos(ang_r), np.cos(ang_r), np.cos(ang_c), np.cos(ang_c)], axis=1)
    sin = np.concatenate([-np.sin(ang_r), np.sin(ang_r), -np.sin(ang_c), np.sin(ang_c)], axis=1)
    return jnp.asarray(cos, f32), jnp.asarray(sin, f32)


def _na_tables(rpb):
    cq = np.arange(GRID_W)
    kc = np.tile(np.arange(GRID_W), NA_WR)
    col_start = np.clip(cq - NA_WC // 2, 0, GRID_W - NA_WC)
    valid = (kc[None, :] >= col_start[:, None]) & (kc[None, :] < col_start[:, None] + NA_WC)
    col_off = np.clip(cq[None, :] - cq[:, None], 1 - NA_WC, NA_WC - 1) + NA_WC - 1
    onehot = (col_off[None, :, :] == np.arange(2 * NA_WC - 1)[:, None, None]).astype(np.float32)
    toep = jnp.einsum('hic,cqk->hiqk', rpb.astype(f32), jnp.asarray(onehot), precision=lax.Precision.HIGHEST)
    tables = []
    for d in range(NA_WR):
        rows = toep[:, NA_WR - 1 - d:2 * NA_WR - 1 - d]
        tables.append(rows.transpose(0, 2, 1, 3).reshape(NA_HEADS, GRID_W, NA_NK))
    return jnp.stack(tables, axis=1), jnp.asarray(valid, f32)


def kernel(x_prompt, x_sample, cache_na_k, cache_na_v, state_lru_f, state_lru_b,
           state_ret_f, state_ret_b, state_ssd_f, state_ssd_b, c, c_ctx,
           norm1_g, norm2_g, w_ada, b_ada, w_in, w_gate, b_gate, w_branch, w_out,
           lru_conv_w, lru_conv_b, lru_wa, lru_ba, lru_wx, lru_bx, lru_lambda,
           ret_gn_g, ssd_conv_w, ssd_conv_b, ssd_a_log, ssd_dt_bias, ssd_d, ssd_norm_g,
           na_q_g, na_k_g, na_rpb, ffn_w_up, ffn_conv_w, ffn_conv_b, ffn_w_down):
    xs0 = x_sample.reshape(T_SAMPLE, D_MODEL)
    xp0 = x_prompt.reshape(T_PROMPT, D_MODEL)
    n_ptiles = N_TILES - N_SAMPLE_TILES
    cond = jnp.concatenate([c, c_ctx[None, :], jnp.zeros((N_COND_PAD - N_COND, D_MODEL), f32)], axis=0)
    mod_all = ada_modulation(cond, w_ada, b_ada)
    cos, sin = _rope_tables()
    hh = jnp.arange(RET_HEADS, dtype=f32)
    ret_la = jnp.stack([jnp.log1p(-jnp.exp2(-5.0 - hh)), jnp.log1p(-jnp.exp2(-5.5 - hh))], axis=1)
    cache_k = cache_na_k.reshape(DEC_BATCH, DEPTH, PAST_LEN, NA_W)
    cache_v = cache_na_v.reshape(DEC_BATCH, DEPTH, PAST_LEN, NA_W)
    lru_init = (state_lru_f.reshape(DEC_BATCH, DEPTH, 1, D_RNN), state_lru_b.reshape(DEC_BATCH, DEPTH, 1, D_RNN))

    w_head_b, w_tail_b = _split_w_in(w_in)
    w_branch_b, w_out_b, w_down_b = w_branch.astype(bf16), w_out.astype(bf16), ffn_w_down.astype(bf16)

    new = {k: None for k in ("k", "v", "lru_f", "lru_b", "ret_f", "ret_b", "ssd_f", "ssd_b")}
    for l in range(DEPTH):
        mod = mod_all[l].reshape(N_COND_PAD, 1, N_MOD * D_MODEL)
        g1 = norm1_g[l][None, :]
        if l == 0:
            proj, xn = in_projection(xs0, 0, 0, N_SAMPLE_TILES, g1, mod, w_head_b, w_tail_b, l)
            proj, xn = in_projection(xp0, 0, N_SAMPLE_TILES, n_ptiles, g1, mod, w_head_b, w_tail_b, l, (proj, xn))
        else:
            proj, xn = in_projection(x, 0, 0, N_TILES, g1, mod, w_head_b, w_tail_b, l)

        w4 = jnp.concatenate([lru_wa[l, 0], lru_wx[l, 0], lru_wa[l, 1], lru_wx[l, 1]], axis=-1).astype(bf16)
        b4 = jnp.concatenate([lru_ba[l, 0].reshape(LRU_BLOCKS, 1, LRU_BS), lru_bx[l, 0].reshape(LRU_BLOCKS, 1, LRU_BS),
                              lru_ba[l, 1].reshape(LRU_BLOCKS, 1, LRU_BS), lru_bx[l, 1].reshape(LRU_BLOCKS, 1, LRU_BS)],
                             axis=-1)
        lru_args = (lru_conv_w[l], lru_conv_b[l][None, :], w4, b4, lru_lambda[l])
        y_lru = lru_branch(proj, (None,), 0, DEC_BATCH, DEC_SEQ, *lru_args, lru_init, l)
        y_lru, new["lru_f"], new["lru_b"] = lru_branch(
            proj, (y_lru, new["lru_f"], new["lru_b"]), T_SAMPLE, BATCH, SEQ, *lru_args, None, l)

        gn = ret_gn_g[l][None, :]
        y_ret = retention_branch(proj, (None,), 0, DEC_BATCH, DEC_SEQ, ret_la, gn, (state_ret_f, state_ret_b), l)
        y_ret, new["ret_f"], new["ret_b"] = retention_branch(
            proj, (y_ret, new["ret_f"], new["ret_b"]), T_SAMPLE, BATCH, SEQ, ret_la, gn, None, l)

        prm = jnp.stack([_group_lanes(ssd_dt_bias[l]), _group_lanes(ssd_a_log[l])], axis=1)
        prm = jnp.concatenate([prm, jnp.zeros((SSD_GROUPS, V7X_SUBLANES - 2, V7X_LANES), f32)], axis=1)
        dvec = jnp.repeat(ssd_d[l], SSD_P)[None, :]
        ssd_args = (ssd_conv_w[l], ssd_conv_b[l][None, :], prm, dvec, ssd_norm_g[l][None, :])
        y_ssd = ssd_branch(proj, (None,), 0, DEC_BATCH, DEC_SEQ, *ssd_args, (state_ssd_f, state_ssd_b), l)
        y_ssd, new["ssd_f"], new["ssd_b"] = ssd_branch(
            proj, (y_ssd, new["ssd_f"], new["ssd_b"]), T_SAMPLE, BATCH, SEQ, *ssd_args, None, l)

        qg = na_q_g[l][None, :]
        kg = na_k_g[l][None, :]
        bias, valid = _na_tables(na_rpb[l])
        y_na = neighbourhood_attention(proj, cache_k, cache_v, l, qg, kg, cos, sin, bias, valid)
        y_na, new["k"], new["v"] = context_attention(proj, (y_na, new["k"], new["v"]), T_SAMPLE, qg, kg, l)

        merged = merge_branches(xn, (y_lru, y_ret, y_ssd, y_na), w_gate, b_gate[l][None, :], w_branch_b, l)
        if l == 0:
            x = residual_projection(merged, w_out_b, l, xs0, mod, 2, 0, N_SAMPLE_TILES)
            x = residual_projection(merged, w_out_b, l, xp0, mod, 2, N_SAMPLE_TILES, n_ptiles, x_tile0=0, buf=x)
        else:
            x = residual_projection(merged, w_out_b, l, x, mod, 2)
        hmid = ffn_up(x, norm2_g[l][None, :], mod, ffn_w_up, l, ffn_conv_w[l], ffn_conv_b[l][None, :])
        if l < DEPTH - 1:
            x = residual_projection(hmid, w_down_b, l, x, mod, 5)
        else:
            y_sample = residual_projection(hmid, w_down_b, l, x, mod, 5, 0, N_SAMPLE_TILES, out_rows=T_SAMPLE)
            y_prompt = residual_projection(hmid, w_down_b, l, x, mod, 5, N_SAMPLE_TILES, n_ptiles,
                                           out_tile0=0, out_rows=T_PROMPT)

    kv_shape = (BATCH, DEPTH, SEQ, NA_HEADS, NA_HD)
    return (y_prompt.reshape(BATCH, SEQ, D_MODEL), y_sample.reshape(DEC_BATCH, DEC_SEQ, D_MODEL),
            new["k"].reshape(kv_shape), new["v"].reshape(kv_shape),
            new["lru_f"].reshape(BATCH, DEPTH, D_RNN), new["lru_b"].reshape(BATCH, DEPTH, D_RNN),
            new["ret_f"], new["ret_b"], new["ssd_f"], new["ssd_b"])
```

```python
import functools
import math

import jax
import jax.numpy as jnp
import numpy as np
from jax import lax
from jax.experimental import pallas as pl
from jax.experimental.pallas import tpu as pltpu

D_MODEL = 2048
BATCH = 16
SEQ = 256
DEPTH = 2
DEC_BATCH = 8
DEC_SEQ = 1024
PAST_LEN = 256
GRID_W = 64
EPS = 1e-6
N_BRANCH = 4
BRANCH_W = 1024
N_MOD = 6
D_RNN = 1024
LRU_BLOCKS = 8
LRU_BS = D_RNN // LRU_BLOCKS
LRU_CONV = 4
LRU_C = 8.0
RET_HEADS = 4
RET_DK = 128
RET_DV = 256
SSD_HEADS = 16
SSD_P = 64
SSD_N = 128
SSD_GROUPS = 2
SSD_CONV = 4
D_SSD = SSD_HEADS * SSD_P
SSD_CONV_CH = D_SSD + 2 * SSD_GROUPS * SSD_N
NA_HEADS = 8
NA_HD = 128
NA_W = NA_HEADS * NA_HD
NA_WR = 8
NA_WC = 16
ROPE_BASE = 10000.0
D_FF = 5632
FFN_CONV = 3
IN_SIZES = (D_RNN, D_RNN,
            RET_HEADS * RET_DK, RET_HEADS * RET_DK, RET_HEADS * RET_DV, RET_HEADS * RET_DV,
            D_SSD, SSD_CONV_CH, 2 * SSD_HEADS,
            NA_W, NA_W, NA_W)
D_IN = sum(IN_SIZES)
IN_OFFSETS = tuple(int(s) for s in np.cumsum(IN_SIZES)[:-1])

V7X_LANES = 128
V7X_SUBLANES = 8
V7X_VMEM_LIMIT_BYTES = 56 * 1024 * 1024

TM = 1024
T_SAMPLE = DEC_BATCH * DEC_SEQ
T_PROMPT = BATCH * SEQ
T_ALL = T_SAMPLE + T_PROMPT
N_SAMPLE_TILES = T_SAMPLE // TM
N_TILES = T_ALL // TM
N_COND = DEC_BATCH + 1
N_COND_PAD = 16

SSD_HPG = SSD_HEADS // SSD_GROUPS
OFF_LRU_X = 0
OFF_LRU_G = OFF_LRU_X + D_RNN
OFF_RET_Q = OFF_LRU_G + D_RNN
OFF_RET_K = OFF_RET_Q + RET_HEADS * RET_DK
OFF_RET_V = OFF_RET_K + RET_HEADS * RET_DK
OFF_RET_G = OFF_RET_V + RET_HEADS * RET_DV
OFF_SSD_Z = OFF_RET_G + RET_HEADS * RET_DV
OFF_SSD_XBC = OFF_SSD_Z + D_SSD
OFF_NA_Q = OFF_SSD_XBC + SSD_CONV_CH
OFF_NA_K = OFF_NA_Q + NA_W
OFF_NA_V = OFF_NA_K + NA_W
OFF_SSD_DT = OFF_NA_V + NA_W
PROJ_TN = 1024
D_IN_PAD = -(-(OFF_SSD_DT + SSD_GROUPS * V7X_LANES) // PROJ_TN) * PROJ_TN
N_HEAD_TILES = OFF_NA_Q // PROJ_TN
HEAD_COLS = N_HEAD_TILES * PROJ_TN

RET_CHUNK = 256
SSD_CHUNK = 256
SSD_TB = 128

f32 = jnp.float32
bf16 = jnp.bfloat16

_ARB = "arbitrary"


def _params(n_axes):
    return pltpu.CompilerParams(dimension_semantics=(_ARB,) * n_axes,
                                vmem_limit_bytes=V7X_VMEM_LIMIT_BYTES)


def _mod_spec(k, tile0=0):
    return pl.BlockSpec((1, 1, D_MODEL), lambda i, j: (jnp.minimum(tile0 + i, N_SAMPLE_TILES), 0, k))


def _dot(a, b):
    return jnp.dot(a, b, preferred_element_type=f32)


def _dot_nt(a, b):
    return lax.dot_general(a, b, (((1,), (1,)), ((), ())), preferred_element_type=f32)


def _sigmoid(x):
    return 0.5 * jnp.tanh(0.5 * x) + 0.5


def _silu(x):
    return x * _sigmoid(x)


def _gelu_tanh(x):
    return 0.5 * x * (1.0 + jnp.tanh(math.sqrt(2.0 / math.pi) * (x + 0.044715 * (x * x * x))))


def _shift_rows(v, k):
    n = v.shape[0]
    row = lax.broadcasted_iota(jnp.int32, (n, 1), 0)
    if k == 1:
        return jnp.where(row == 0, 0.0, pltpu.roll(v, 1, axis=0))
    return jnp.where(row == n - 1, 0.0, pltpu.roll(v, n - 1, axis=0))


def _dwconv4(x, cw, cb):
    acc = _shift_rows(x * cw[0:1, :], 1) + x * cw[1:2, :]
    return cb + x * cw[2:3, :] + _shift_rows(acc, 1) + _shift_rows(x * cw[3:4, :], -1)


def _softplus(x):
    return jnp.maximum(x, 0.0) + jnp.log1p(jnp.exp(-jnp.abs(x)))


def _drop_refs(body, n, *refs):
    body(*refs[n:])


def _inplace_call(body, bufs, grid, in_specs, args, out_specs, out_shape, scratch, name):
    held = [(k, b) for k, b in enumerate(bufs) if b is not None]
    if held:
        body = functools.partial(_drop_refs, body, len(held))
        in_specs = [pl.BlockSpec(memory_space=pl.ANY)] * len(held) + list(in_specs)
        args = [b for _, b in held] + list(args)
    aliases = {pos: k for pos, (k, _) in enumerate(held)}
    return pl.pallas_call(
        body, grid=grid, in_specs=in_specs, out_specs=out_specs, out_shape=out_shape,
        scratch_shapes=scratch, input_output_aliases=aliases,
        compiler_params=_params(len(grid)), name=name,
    )(*args)


_BRANCH_BUF = jax.ShapeDtypeStruct((T_ALL, BRANCH_W), bf16)


ADA_TN = 1024


def _ada_kernel(c_ref, w_ref, b_ref, o_ref):
    c = _silu(c_ref[...]).astype(bf16)
    o_ref[0] = _dot(c, w_ref[0].astype(bf16)) + b_ref[0]


def ada_modulation(cond, w_ada, b_ada):
    n = N_MOD * D_MODEL
    return pl.pallas_call(
        _ada_kernel,
        grid=(DEPTH, n // ADA_TN),
        in_specs=[pl.BlockSpec((N_COND_PAD, D_MODEL), lambda l, j: (0, 0)),
                  pl.BlockSpec((1, D_MODEL, ADA_TN), lambda l, j: (l, 0, j)),
                  pl.BlockSpec((1, 1, ADA_TN), lambda l, j: (l, 0, j))],
        out_specs=pl.BlockSpec((1, N_COND_PAD, ADA_TN), lambda l, j: (l, 0, j)),
        out_shape=jax.ShapeDtypeStruct((DEPTH, N_COND_PAD, n), f32),
        compiler_params=_params(2),
        name="ada_modulation",
    )(cond, w_ada, b_ada.reshape(DEPTH, 1, n))


NORM_ROWS = 128
NORM_SPLIT = 2


def _modulated_norm_parts(x_ref, g_ref, sc_ref, sh_ref, xn_ref):
    gs = g_ref[...] * (1.0 + sc_ref[0])
    sh = sh_ref[0]
    part = TM // NORM_SPLIT
    for p in range(NORM_SPLIT):
        pieces = []
        for r in range(part // NORM_ROWS):
            rows = pl.ds(p * part + r * NORM_ROWS, NORM_ROWS)
            x = x_ref[rows, :]
            y = x * lax.rsqrt(jnp.mean(x * x, axis=-1, keepdims=True) + EPS)
            pieces.append((y * gs + sh).astype(bf16))
        xn = jnp.concatenate(pieces, axis=0)
        xn_ref[pl.ds(p * part, part), :] = xn
        yield xn


def _in_proj_kernel(x_ref, g_ref, sc_ref, sh_ref, wh_ref, wt_ref, o_ref, xn_ref):
    j = pl.program_id(1)

    @pl.when(j == 0)
    def _():
        w = wh_ref[0]
        outs = [_dot(xn, w).astype(bf16) for xn in _modulated_norm_parts(x_ref, g_ref, sc_ref, sh_ref, xn_ref)]
        o_ref[...] = jnp.concatenate(outs, axis=0)

    @pl.when((j > 0) & (j < N_HEAD_TILES))
    def _():
        o_ref[...] = _dot(xn_ref[...], wh_ref[0]).astype(bf16)

    @pl.when(j >= N_HEAD_TILES)
    def _():
        o_ref[...] = _dot(xn_ref[...], wt_ref[0]).astype(bf16)


def in_projection(x, x_tile0, tile0, n_tiles, g, mod, w_head, w_tail, layer, bufs=(None, None)):
    n = D_IN_PAD
    n_tail = w_tail.shape[2] // PROJ_TN
    return _inplace_call(
        _in_proj_kernel, bufs, (n_tiles, n // PROJ_TN),
        [pl.BlockSpec((TM, D_MODEL), lambda i, j: (x_tile0 + i, 0)),
         pl.BlockSpec((1, D_MODEL), lambda i, j: (0, 0)),
         _mod_spec(1, tile0), _mod_spec(0, tile0),
         pl.BlockSpec((1, D_MODEL, PROJ_TN), lambda i, j: (layer, 0, jnp.minimum(j, N_HEAD_TILES - 1))),
         pl.BlockSpec((1, D_MODEL, PROJ_TN),
                      lambda i, j: (layer, 0, jnp.clip(j - N_HEAD_TILES, 0, n_tail - 1)))],
        [x, g, mod, mod, w_head, w_tail],
        [pl.BlockSpec((TM, PROJ_TN), lambda i, j: (tile0 + i, j)),
         pl.BlockSpec((TM, D_MODEL), lambda i, j: (tile0 + i, 0))],
        [jax.ShapeDtypeStruct((T_ALL, n), bf16), jax.ShapeDtypeStruct((T_ALL, D_MODEL), bf16)],
        [], "in_projection")


MERGE_TN = 256


def _merge_kernel(xn_ref, *refs):
    y_refs, wg_refs, bg_refs = refs[0:4], refs[4:8], refs[8:12]
    wb_ref, o_ref = refs[12], refs[13]
    xn = xn_ref[...]
    acc = None
    for n in range(N_BRANCH):
        gate = _sigmoid(_dot(xn, wg_refs[n][0].astype(bf16)) + bg_refs[n][...])
        term = gate * _dot(y_refs[n][...], wb_ref[0, n])
        acc = term if acc is None else acc + term
    o_ref[...] = acc.astype(bf16)


def merge_branches(xn, ys, w_gate, b_gate, w_branch, layer):
    nj = D_MODEL // MERGE_TN
    y_spec = pl.BlockSpec((TM, BRANCH_W), lambda i, j: (i, 0))
    wg_specs = [pl.BlockSpec((1, D_MODEL, MERGE_TN), lambda i, j, n=n: (layer, 0, n * nj + j))
                for n in range(N_BRANCH)]
    bg_specs = [pl.BlockSpec((1, MERGE_TN), lambda i, j, n=n: (0, n * nj + j)) for n in range(N_BRANCH)]
    return pl.pallas_call(
        _merge_kernel,
        grid=(N_TILES, nj),
        in_specs=[pl.BlockSpec((TM, D_MODEL), lambda i, j: (i, 0)),
                  y_spec, y_spec, y_spec, y_spec, *wg_specs, *bg_specs,
                  pl.BlockSpec((1, N_BRANCH, BRANCH_W, MERGE_TN), lambda i, j: (layer, 0, 0, j))],
        out_specs=pl.BlockSpec((TM, MERGE_TN), lambda i, j: (i, j)),
        out_shape=jax.ShapeDtypeStruct((T_ALL, D_MODEL), bf16),
        compiler_params=_params(2),
        name="merge_branches",
    )(xn, *ys, *([w_gate] * N_BRANCH), *([b_gate] * N_BRANCH), w_branch)


RES_TN = 512
RES_TN_SHORT_K = 1024


def _residual_kernel(a_ref, w_ref, x_ref, gv_ref, o_ref):
    o_ref[...] = x_ref[...] + gv_ref[0] * _dot(a_ref[...], w_ref[0])


def residual_projection(a, w, layer, x, mod, k_mod, tile0=0, n_tiles=N_TILES, x_tile0=None, out_tile0=None,
                        out_rows=T_ALL, buf=None):
    kdim = a.shape[1]
    tn = RES_TN_SHORT_K if kdim == D_MODEL else RES_TN
    x_tile0 = tile0 if x_tile0 is None else x_tile0
    out_tile0 = tile0 if out_tile0 is None else out_tile0
    return _inplace_call(
        _residual_kernel, (buf,), (n_tiles, D_MODEL // tn),
        [pl.BlockSpec((TM, kdim), lambda i, j: (tile0 + i, 0)),
         pl.BlockSpec((1, kdim, tn), lambda i, j: (layer, 0, j)),
         pl.BlockSpec((TM, tn), lambda i, j: (x_tile0 + i, j)),
         pl.BlockSpec((1, 1, tn),
                      lambda i, j: (jnp.minimum(tile0 + i, N_SAMPLE_TILES), 0, k_mod * (D_MODEL // tn) + j))],
        [a, w, x, mod],
        pl.BlockSpec((TM, tn), lambda i, j: (out_tile0 + i, j)),
        jax.ShapeDtypeStruct((out_rows, D_MODEL), f32), [], "residual_projection")


FFN_TN = 512


def _ffn_up_kernel(x_ref, g_ref, sc_ref, sh_ref, wa_ref, wv_ref, cw_ref, cb_ref, o_ref, xn_ref):
    i = pl.program_id(0)

    def epilogue(a, v):
        seq_len = jnp.where(i < N_SAMPLE_TILES, DEC_SEQ, SEQ)
        pos = lax.broadcasted_iota(jnp.int32, (TM, 1), 0) & (seq_len - 1)
        prev = jnp.where(pos == 0, 0.0, pltpu.roll(a, 1, axis=0))
        nxt = jnp.where(pos == seq_len - 1, 0.0, pltpu.roll(a, TM - 1, axis=0))
        cw = cw_ref[...]
        conv = cb_ref[...] + prev * cw[0:1, :] + a * cw[1:2, :] + nxt * cw[2:3, :]
        o_ref[...] = (_gelu_tanh(conv) * v).astype(bf16)

    @pl.when(pl.program_id(1) == 0)
    def _():
        wa = wa_ref[0].astype(bf16)
        wv = wv_ref[0].astype(bf16)
        parts = [(_dot(xn, wa), _dot(xn, wv)) for xn in _modulated_norm_parts(x_ref, g_ref, sc_ref, sh_ref, xn_ref)]
        epilogue(jnp.concatenate([p[0] for p in parts], axis=0), jnp.concatenate([p[1] for p in parts], axis=0))

    @pl.when(pl.program_id(1) > 0)
    def _():
        xn = xn_ref[...]
        epilogue(_dot(xn, wa_ref[0].astype(bf16)), _dot(xn, wv_ref[0].astype(bf16)))


def ffn_up(x, g, mod, w_up, layer, conv_w, conv_b):
    nj = D_FF // FFN_TN
    return pl.pallas_call(
        _ffn_up_kernel,
        grid=(N_TILES, nj),
        in_specs=[pl.BlockSpec((TM, D_MODEL), lambda i, j: (i, 0)),
                  pl.BlockSpec((1, D_MODEL), lambda i, j: (0, 0)),
                  _mod_spec(4), _mod_spec(3),
                  pl.BlockSpec((1, D_MODEL, FFN_TN), lambda i, j: (layer, 0, j)),
                  pl.BlockSpec((1, D_MODEL, FFN_TN), lambda i, j: (layer, 0, nj + j)),
                  pl.BlockSpec((FFN_CONV, FFN_TN), lambda i, j: (0, j)),
                  pl.BlockSpec((1, FFN_TN), lambda i, j: (0, j))],
        out_specs=pl.BlockSpec((TM, FFN_TN), lambda i, j: (i, j)),
        out_shape=jax.ShapeDtypeStruct((T_ALL, D_FF), bf16),
        scratch_shapes=[pltpu.VMEM((TM, D_MODEL), bf16)],
        compiler_params=_params(2),
        name="ffn_up",
    )(x, g, mod, mod, w_up, w_up, conv_w, conv_b)


LRU_CB_LONG = D_RNN
LRU_CB_SHORT = D_RNN


def _lru_kernel(*refs, seq_len, has_init, cbw):
    if has_init:
        (x_ref, g_ref, cw_ref, cb_ref, w4_ref, b4_ref, lam_ref, h0f_ref, h0b_ref,
         y_ref, af_ref, uf_ref, ab_ref, ub_ref, hf_ref, hb_ref) = refs
    else:
        (x_ref, g_ref, cw_ref, cb_ref, w4_ref, b4_ref, lam_ref,
         y_ref, sf_ref, sb_ref, af_ref, uf_ref, ab_ref, ub_ref, hf_ref, hb_ref) = refs
    L = seq_len
    S = V7X_SUBLANES
    xc = _dwconv4(x_ref[...].astype(f32), cw_ref[...], cb_ref[...])
    sp = _softplus(-lam_ref[...])
    for n in range(cbw // LRU_BS):
        cols = slice(n * LRU_BS, (n + 1) * LRU_BS)
        xcn = xc[:, cols]
        z = _dot(xcn.astype(bf16), w4_ref[n]) + b4_ref[n]
        for d, (a_ref, u_ref) in enumerate(((af_ref, uf_ref), (ab_ref, ub_ref))):
            r = _sigmoid(z[:, (2 * d) * LRU_BS:(2 * d + 1) * LRU_BS])
            ig = _sigmoid(z[:, (2 * d + 1) * LRU_BS:(2 * d + 2) * LRU_BS])
            log_a = (-LRU_C) * r * sp[d:d + 1, cols]
            a = jnp.exp(log_a)
            a_ref[:, cols] = a
            gain2 = -jnp.tanh(log_a) * (a * a + 1.0)
            gain = jnp.where(gain2 > 0.0, gain2 * lax.rsqrt(gain2), 0.0)
            u_ref[:, cols] = gain * (ig * xcn)

    if has_init:
        h0f = h0f_ref[0, 0]
        h0b = h0b_ref[0, 0]
    else:
        h0f = jnp.zeros((1, cbw), f32)
        h0b = jnp.zeros((1, cbw), f32)

    def step(i, carry):
        hf, hb = carry
        base_f = pl.multiple_of(i * S, S)
        base_b = pl.multiple_of(L - S - i * S, S)
        for r in range(S):
            tf = pl.ds(base_f + r, 1)
            tb = pl.ds(base_b + (S - 1 - r), 1)
            hf = af_ref[tf, :] * hf + uf_ref[tf, :]
            hb = ab_ref[tb, :] * hb + ub_ref[tb, :]
            hf_ref[tf, :] = hf
            hb_ref[tb, :] = hb
        return hf, hb

    hf, hb = lax.fori_loop(0, L // S, step, (h0f, h0b))
    if not has_init:
        sf_ref[0, 0] = hf
        sb_ref[0, 0] = hb
    y_ref[...] = ((hf_ref[...] + hb_ref[...]) * _gelu_tanh(g_ref[...].astype(f32))).astype(bf16)


def lru_branch(proj, bufs, row0, n_seq, seq_len, cw, cb, w4, b4, lam, init, layer):
    has_init = init is not None
    cbw = LRU_CB_LONG if seq_len == DEC_SEQ else LRU_CB_SHORT
    ncb = D_RNN // cbw
    nb = cbw // LRU_BS
    rb0 = row0 // seq_len
    in_specs = [pl.BlockSpec((seq_len, cbw), lambda s, c: (rb0 + s, OFF_LRU_X // cbw + c)),
                pl.BlockSpec((seq_len, cbw), lambda s, c: (rb0 + s, OFF_LRU_G // cbw + c)),
                pl.BlockSpec((LRU_CONV, cbw), lambda s, c: (0, c)),
                pl.BlockSpec((1, cbw), lambda s, c: (0, c)),
                pl.BlockSpec((nb, LRU_BS, 4 * LRU_BS), lambda s, c: (c, 0, 0)),
                pl.BlockSpec((nb, 1, 4 * LRU_BS), lambda s, c: (c, 0, 0)),
                pl.BlockSpec((2, cbw), lambda s, c: (0, c))]
    args = [proj, proj, cw, cb, w4, b4, lam]
    state_spec = pl.BlockSpec((1, 1, 1, cbw), lambda s, c: (s, layer, 0, c))
    y_spec = pl.BlockSpec((seq_len, cbw), lambda s, c: (rb0 + s, c))
    if has_init:
        in_specs += [state_spec, state_spec]
        args += [init[0], init[1]]
        out_specs, out_shape = y_spec, _BRANCH_BUF
    else:
        st = jax.ShapeDtypeStruct((n_seq, DEPTH, 1, D_RNN), f32)
        out_specs, out_shape = [y_spec, state_spec, state_spec], [_BRANCH_BUF, st, st]
    scratch = [pltpu.VMEM((seq_len, cbw), f32) for _ in range(6)]
    return _inplace_call(functools.partial(_lru_kernel, seq_len=seq_len, has_init=has_init, cbw=cbw), bufs,
                         (n_seq, ncb), in_specs, args, out_specs, out_shape, scratch, "lru_branch")


def _ret_kernel(*refs, seq_len, has_init):
    if has_init:
        (la_ref, q_ref, k_ref, v_ref, g_ref, gn_ref, s0f_ref, s0b_ref, y_ref, acc_ref) = refs
    else:
        (la_ref, q_ref, k_ref, v_ref, g_ref, gn_ref, y_ref, sf_ref, sb_ref, acc_ref) = refs
    T = min(RET_CHUNK, seq_len)
    nc = seq_len // T
    tt = lax.broadcasted_iota(jnp.int32, (T, T), 0)
    ss = lax.broadcasted_iota(jnp.int32, (T, T), 1)
    diff = (tt - ss).astype(f32)
    tcol = lax.broadcasted_iota(jnp.int32, (T, 1), 0).astype(f32)
    scale = RET_DK ** -0.5
    for h in range(RET_HEADS):
        la_f = la_ref[h, 0]
        la_b = la_ref[h, 1]
        kcols = pl.ds(h * RET_DK, RET_DK)
        vcols = pl.ds(h * RET_DV, RET_DV)
        dsum = (jnp.where(tt >= ss, jnp.exp(la_f * diff), 0.0)
                + jnp.where(ss >= tt, jnp.exp(-la_b * diff), 0.0))

        def chunk(c):
            rows = pl.ds(c * T, T)
            q = q_ref[rows, kcols].astype(f32)
            ks = k_ref[rows, kcols].astype(f32) * scale
            v = v_ref[rows, vcols].astype(f32)
            return rows, q, ks, v

        s_f = s0f_ref[0, 0, h] if has_init else None
        for c in range(nc):
            rows, q, ks, v = chunk(c)
            scores = _dot_nt(q.astype(bf16), ks.astype(bf16)) * dsum
            y = _dot(scores.astype(bf16), v.astype(bf16))
            if s_f is not None:
                y = y + _dot((q * jnp.exp(la_f * (tcol + 1.0))).astype(bf16), s_f.astype(bf16))
            acc_ref[rows, :] = y
            if c < nc - 1 or not has_init:
                upd = _dot(ks.T.astype(bf16), (v * jnp.exp(la_f * (T - 1.0 - tcol))).astype(bf16))
                s_f = upd if s_f is None else jnp.exp(la_f * T) * s_f + upd
        s_b = s0b_ref[0, 0, h] if has_init else None
        for c in reversed(range(nc)):
            rows, q, ks, v = chunk(c)
            if s_b is not None:
                acc_ref[rows, :] += _dot((q * jnp.exp(la_b * (T - tcol))).astype(bf16), s_b.astype(bf16))
            if c > 0 or not has_init:
                upd = _dot(ks.T.astype(bf16), (v * jnp.exp(la_b * tcol)).astype(bf16))
                s_b = upd if s_b is None else jnp.exp(la_b * T) * s_b + upd
        if not has_init:
            sf_ref[0, 0, h] = s_f
            sb_ref[0, 0, h] = s_b
        y = acc_ref[...]
        mu = jnp.mean(y, axis=-1, keepdims=True)
        yc = y - mu
        var = jnp.mean(yc * yc, axis=-1, keepdims=True)
        y = yc * lax.rsqrt(var + EPS) * gn_ref[:, vcols]
        y_ref[:, vcols] = (y * _silu(g_ref[:, vcols].astype(f32))).astype(bf16)


def retention_branch(proj, bufs, row0, n_seq, seq_len, la, gn, init, layer):
    has_init = init is not None
    rb0 = row0 // seq_len
    qk_w, vg_w = RET_HEADS * RET_DK, RET_HEADS * RET_DV
    in_specs = [pl.BlockSpec(memory_space=pltpu.SMEM),
                pl.BlockSpec((seq_len, qk_w), lambda s: (rb0 + s, OFF_RET_Q // qk_w)),
                pl.BlockSpec((seq_len, qk_w), lambda s: (rb0 + s, OFF_RET_K // qk_w)),
                pl.BlockSpec((seq_len, vg_w), lambda s: (rb0 + s, OFF_RET_V // vg_w)),
                pl.BlockSpec((seq_len, vg_w), lambda s: (rb0 + s, OFF_RET_G // vg_w)),
                pl.BlockSpec((1, vg_w), lambda s: (0, 0))]
    args = [la, proj, proj, proj, proj, gn]
    state_spec = pl.BlockSpec((1, 1, RET_HEADS, RET_DK, RET_DV), lambda s: (s, layer, 0, 0, 0))
    y_spec = pl.BlockSpec((seq_len, vg_w), lambda s: (rb0 + s, 0))
    if has_init:
        in_specs += [state_spec, state_spec]
        args += [init[0], init[1]]
        out_specs, out_shape = y_spec, _BRANCH_BUF
    else:
        st = jax.ShapeDtypeStruct((n_seq, DEPTH, RET_HEADS, RET_DK, RET_DV), f32)
        out_specs, out_shape = [y_spec, state_spec, state_spec], [_BRANCH_BUF, st, st]
    return _inplace_call(functools.partial(_ret_kernel, seq_len=seq_len, has_init=has_init), bufs,
                         (n_seq,), in_specs, args, out_specs, out_shape,
                         [pltpu.VMEM((seq_len, RET_DV), f32)], "retention_branch")


SSD_GW = SSD_HPG * SSD_P


def _split3(x):
    h1 = x.astype(bf16)
    r1 = x - h1.astype(f32)
    h2 = r1.astype(bf16)
    h3 = (r1 - h2.astype(f32)).astype(bf16)
    return h1, h2, h3


def _dot_exact_rhs(m, x):
    h1, h2, h3 = _split3(x)
    return _dot(m, h1) + _dot(m, h2) + _dot(m, h3)


def _dot_exact_lhs(x, m):
    h1, h2, h3 = _split3(x)
    return _dot(h1, m) + _dot(h2, m) + _dot(h3, m)


def _heads_to_lanes(s_ref, st_ref):
    for hh in range(SSD_HPG):
        st_ref[:, pl.ds(hh * SSD_P, SSD_P)] = s_ref[0, 0, hh]
    return st_ref[...]


def _ssd_kernel(*refs, seq_len, has_init):
    if has_init:
        (z_ref, x_ref, b_ref, c_ref, dt_ref, cwx_ref, cbx_ref, cwb_ref, cbb_ref, cwc_ref, cbc_ref,
         prm_ref, dvec_ref, ng_ref, s0f_ref, s0b_ref,
         y_ref, xs_ref, bs_ref, cs_ref, acc_ref, rb_ref, dts_ref, yn_ref, st_ref) = refs
    else:
        (z_ref, x_ref, b_ref, c_ref, dt_ref, cwx_ref, cbx_ref, cwb_ref, cbb_ref, cwc_ref, cbc_ref,
         prm_ref, dvec_ref, ng_ref,
         y_ref, sf_ref, sb_ref, xs_ref, bs_ref, cs_ref, acc_ref, rb_ref, dts_ref, yn_ref, st_ref) = refs
    g = pl.program_id(1)
    L = seq_len
    T = min(SSD_CHUNK, L)
    nc = L // T
    H = SSD_HPG
    xs_ref[...] = _silu(_dwconv4(x_ref[...].astype(f32), cwx_ref[...], cbx_ref[...]))
    bs_ref[...] = _silu(_dwconv4(b_ref[...].astype(f32), cwb_ref[...], cbb_ref[...]))
    cs_ref[...] = _silu(_dwconv4(c_ref[...].astype(f32), cwc_ref[...], cbc_ref[...]))
    prm = prm_ref[0]
    a_neg = -jnp.exp(prm[1:2, :])
    tt = lax.broadcasted_iota(jnp.int32, (T, T), 0)
    ss = lax.broadcasted_iota(jnp.int32, (T, T), 1)
    lower = tt >= ss
    upper = ss >= tt
    tri_l = jnp.where(lower, 1.0, 0.0).astype(bf16)
    tri_u = jnp.where(upper, 1.0, 0.0).astype(bf16)
    lower_b = lower[:SSD_TB, :SSD_TB]
    upper_b = upper[:SSD_TB, :SSD_TB]
    er = lax.broadcasted_iota(jnp.int32, (V7X_LANES, SSD_GW), 0)
    ec = lax.broadcasted_iota(jnp.int32, (V7X_LANES, SSD_GW), 1) // SSD_P
    exp_f = jnp.where(er == ec, 1.0, 0.0).astype(bf16)
    exp_b = jnp.where(er == ec + H, 1.0, 0.0).astype(bf16)

    def expand(w, e):
        hi = w.astype(bf16)
        lo = (w - hi.astype(f32)).astype(bf16)
        return _dot(hi, e) + _dot(lo, e)

    s_f = _heads_to_lanes(s0f_ref, st_ref) if has_init else None
    for c in range(nc):
        rows = pl.ds(c * T, T)
        dt = _softplus(dt_ref[rows, :].astype(f32) + prm[0:1, :])
        dts_ref[rows, :] = dt
        da = dt * a_neg
        cum = _dot_exact_rhs(tri_l, da)
        rsum = (cum[T - 1:T, :] - cum) + da
        rb_ref[rows, :] = rsum
        da_t = da.T
        cum_t = _dot_exact_lhs(da_t, tri_u)
        rsum_t = (cum_t[:, T - 1:T] - cum_t) + da_t
        dt_t = dt.T
        bmat = bs_ref[rows, :]
        cmat = cs_ref[rows, :]
        xmat = xs_ref[rows, :]
        gmat = _dot_nt(cmat.astype(bf16), bmat.astype(bf16))
        for hh in range(H):
            cf, cf_t, dtf_t = cum[:, hh:hh + 1], cum_t[hh:hh + 1, :], dt_t[hh:hh + 1, :]
            rb, rb_t, dtb_t = rsum[:, H + hh:H + hh + 1], rsum_t[H + hh:H + hh + 1, :], dt_t[H + hh:H + hh + 1, :]
            block_rows = []
            for bi in range(T // SSD_TB):
                rr = slice(bi * SSD_TB, (bi + 1) * SSD_TB)
                blocks = []
                for bj in range(T // SSD_TB):
                    cc = slice(bj * SSD_TB, (bj + 1) * SSD_TB)
                    if bi > bj:
                        blk = jnp.exp(cf[rr] - cf_t[:, cc]) * dtf_t[:, cc]
                    elif bi < bj:
                        blk = jnp.exp(rb[rr] - rb_t[:, cc]) * dtb_t[:, cc]
                    else:
                        blk = (jnp.where(lower_b, jnp.exp(cf[rr] - cf_t[:, cc]), 0.0) * dtf_t[:, cc]
                               + jnp.where(upper_b, jnp.exp(rb[rr] - rb_t[:, cc]), 0.0) * dtb_t[:, cc])
                    blocks.append(gmat[rr, cc] * blk)
                block_rows.append(jnp.concatenate(blocks, axis=1))
            m = jnp.concatenate(block_rows, axis=0)
            xh = xmat[:, hh * SSD_P:(hh + 1) * SSD_P]
            acc_ref[rows, pl.ds(hh * SSD_P, SSD_P)] = _dot(m.astype(bf16), xh.astype(bf16))
        ecum = jnp.exp(cum)
        if s_f is not None:
            acc_ref[rows, :] += _dot(cmat.astype(bf16), s_f.astype(bf16)) * expand(ecum, exp_f)
        if c < nc - 1 or not has_init:
            tail = jnp.exp(cum[T - 1:T, :] - cum) * dt
            xw = (xmat * expand(tail, exp_f)).astype(bf16)
            upd = _dot(bmat.T.astype(bf16), xw)
            if s_f is None:
                s_f = upd
            else:
                s_f = s_f * expand(jnp.broadcast_to(ecum[T - 1:T, :], (V7X_SUBLANES, V7X_LANES)),
                                   exp_f)[0:1, :] + upd
    s_b = _heads_to_lanes(s0b_ref, st_ref) if has_init else None
    for c in reversed(range(nc)):
        rows = pl.ds(c * T, T)
        dt = dts_ref[rows, :]
        rsum = rb_ref[rows, :]
        ers = jnp.exp(rsum)
        bmat = bs_ref[rows, :]
        cmat = cs_ref[rows, :]
        xmat = xs_ref[rows, :]
        if s_b is not None:
            acc_ref[rows, :] += _dot(cmat.astype(bf16), s_b.astype(bf16)) * expand(ers, exp_b)
        if c > 0 or not has_init:
            tail = jnp.exp(rsum[0:1, :] - rsum) * dt
            xw = (xmat * expand(tail, exp_b)).astype(bf16)
            upd = _dot(bmat.T.astype(bf16), xw)
            if s_b is None:
                s_b = upd
            else:
                s_b = s_b * expand(jnp.broadcast_to(ers[0:1, :], (V7X_SUBLANES, V7X_LANES)),
                                   exp_b)[0:1, :] + upd
    if not has_init:
        for hh in range(H):
            sf_ref[0, 0, hh] = s_f[:, hh * SSD_P:(hh + 1) * SSD_P]
            sb_ref[0, 0, hh] = s_b[:, hh * SSD_P:(hh + 1) * SSD_P]
    yg = (acc_ref[...] + xs_ref[...] * dvec_ref[...]) * _silu(z_ref[...].astype(f32))
    yn_ref[g] = yg

    @pl.when(g == SSD_GROUPS - 1)
    def _():
        ssq = None
        for gg in range(SSD_GROUPS):
            y = yn_ref[gg]
            s = jnp.sum(y * y, axis=-1, keepdims=True)
            ssq = s if ssq is None else ssq + s
        inv = lax.rsqrt(ssq * (1.0 / D_SSD) + EPS)
        for gg in range(SSD_GROUPS):
            cols = pl.ds(gg * SSD_GW, SSD_GW)
            y_ref[:, cols] = (yn_ref[gg] * inv * ng_ref[:, cols]).astype(bf16)


def ssd_branch(proj, bufs, row0, n_seq, seq_len, conv_w, conv_b, prm, dvec, ng, init, layer):
    has_init = init is not None
    rb0 = row0 // seq_len
    xoff = OFF_SSD_XBC
    boff = OFF_SSD_XBC + D_SSD
    coff = boff + SSD_GROUPS * SSD_N
    in_specs = [pl.BlockSpec((seq_len, SSD_GW), lambda s, g: (rb0 + s, OFF_SSD_Z // SSD_GW + g)),
                pl.BlockSpec((seq_len, SSD_GW), lambda s, g: (rb0 + s, xoff // SSD_GW + g)),
                pl.BlockSpec((seq_len, SSD_N), lambda s, g: (rb0 + s, boff // SSD_N + g)),
                pl.BlockSpec((seq_len, SSD_N), lambda s, g: (rb0 + s, coff // SSD_N + g)),
                pl.BlockSpec((seq_len, V7X_LANES), lambda s, g: (rb0 + s, OFF_SSD_DT // V7X_LANES + g)),
                pl.BlockSpec((SSD_CONV, SSD_GW), lambda s, g: (0, g)),
                pl.BlockSpec((1, SSD_GW), lambda s, g: (0, g)),
                pl.BlockSpec((SSD_CONV, SSD_N), lambda s, g: (0, D_SSD // SSD_N + g)),
                pl.BlockSpec((1, SSD_N), lambda s, g: (0, D_SSD // SSD_N + g)),
                pl.BlockSpec((SSD_CONV, SSD_N), lambda s, g: (0, D_SSD // SSD_N + SSD_GROUPS + g)),
                pl.BlockSpec((1, SSD_N), lambda s, g: (0, D_SSD // SSD_N + SSD_GROUPS + g)),
                pl.BlockSpec((1, V7X_SUBLANES, V7X_LANES), lambda s, g: (g, 0, 0)),
                pl.BlockSpec((1, SSD_GW), lambda s, g: (0, g)),
                pl.BlockSpec((1, D_SSD), lambda s, g: (0, 0))]
    args = [proj, proj, proj, proj, proj, conv_w, conv_b, conv_w, conv_b, conv_w, conv_b, prm, dvec, ng]
    state_spec = pl.BlockSpec((1, 1, SSD_HPG, SSD_N, SSD_P), lambda s, g: (s, layer, g, 0, 0))
    y_spec = pl.BlockSpec((seq_len, D_SSD), lambda s, g: (rb0 + s, 0))
    if has_init:
        in_specs += [state_spec, state_spec]
        args += [init[0], init[1]]
        out_specs, out_shape = y_spec, _BRANCH_BUF
    else:
        st = jax.ShapeDtypeStruct((n_seq, DEPTH, SSD_HEADS, SSD_N, SSD_P), f32)
        out_specs, out_shape = [y_spec, state_spec, state_spec], [_BRANCH_BUF, st, st]
    scratch = [pltpu.VMEM((seq_len, SSD_GW), f32),
               pltpu.VMEM((seq_len, SSD_N), f32),
               pltpu.VMEM((seq_len, SSD_N), f32),
               pltpu.VMEM((seq_len, SSD_GW), f32),
               pltpu.VMEM((seq_len, V7X_LANES), f32),
               pltpu.VMEM((seq_len, V7X_LANES), f32),
               pltpu.VMEM((SSD_GROUPS, seq_len, SSD_GW), f32),
               pltpu.VMEM((SSD_N, SSD_GW), f32)]
    return _inplace_call(functools.partial(_ssd_kernel, seq_len=seq_len, has_init=has_init), bufs,
                         (n_seq, SSD_GROUPS), in_specs, args, out_specs, out_shape, scratch, "ssd_branch")


def _head_rmsnorm(x, g):
    return x * lax.rsqrt(jnp.mean(x * x, axis=-1, keepdims=True) + EPS) * g


CTX_HB = 4
CTX_W = CTX_HB * NA_HD


def _ctx_attn_kernel(q_ref, k_ref, v_ref, qg_ref, kg_ref, y_ref, ko_ref, vo_ref):
    scale = NA_HD ** -0.5
    vo_ref[0, 0] = v_ref[...].astype(f32)
    for h in range(CTX_HB):
        cols = pl.ds(h * NA_HD, NA_HD)
        q = _head_rmsnorm(q_ref[:, cols].astype(f32), qg_ref[...])
        k = _head_rmsnorm(k_ref[:, cols].astype(f32), kg_ref[...])
        ko_ref[0, 0, :, cols] = k
        s = _dot_nt(q.astype(bf16), k.astype(bf16)) * scale
        p = jnp.exp(s - jnp.max(s, axis=-1, keepdims=True))
        o = _dot(p.astype(bf16), v_ref[:, cols]) / jnp.sum(p, axis=-1, keepdims=True)
        y_ref[:, cols] = o.astype(bf16)


def context_attention(proj, bufs, row0, qg, kg, layer):
    rb0 = row0 // SEQ
    spec = lambda off: pl.BlockSpec((SEQ, CTX_W), lambda s, hb: (rb0 + s, off // CTX_W + hb))
    gspec = pl.BlockSpec((1, NA_HD), lambda s, hb: (0, 0))
    kv_spec = pl.BlockSpec((1, 1, SEQ, CTX_W), lambda s, hb: (s, layer, 0, hb))
    kv_shape = jax.ShapeDtypeStruct((BATCH, DEPTH, SEQ, NA_W), f32)
    return _inplace_call(
        _ctx_attn_kernel, bufs, (BATCH, NA_HEADS // CTX_HB),
        [spec(OFF_NA_Q), spec(OFF_NA_K), spec(OFF_NA_V), gspec, gspec], [proj, proj, proj, qg, kg],
        [pl.BlockSpec((SEQ, CTX_W), lambda s, hb: (rb0 + s, hb)), kv_spec, kv_spec],
        [_BRANCH_BUF, kv_shape, kv_shape], [], "context_attention")


NA_ROWS = DEC_SEQ // GRID_W
NA_NK = NA_WR * GRID_W


def _na_row_start(r):
    return min(max(r - NA_WR // 2, 0), NA_ROWS - NA_WR)


def _rope(x, cos, sin_signed):
    lane = lax.broadcasted_iota(jnp.int32, x.shape, 1)
    quarter = NA_HD // 4
    swapped = jnp.where((lane & (2 * quarter - 1)) < quarter,
                        pltpu.roll(x, NA_HD - quarter, axis=1), pltpu.roll(x, quarter, axis=1))
    return x * cos + swapped * sin_signed


def _na_row_groups():
    groups, r = [], 0
    while r < NA_ROWS:
        n = 1
        while r + n < NA_ROWS and _na_row_start(r + n) == _na_row_start(r):
            n += 1
        groups.append((r, n, _na_row_start(r)))
        r += n
    return groups


def _na_attn_kernel(q_ref, k_ref, v_ref, kc_ref, vc_ref, qg_ref, kg_ref, cos_ref, sin_ref,
                    bias_ref, valid_ref, y_ref, qs_ref, ks_ref, s_ref, p_ref, oc_ref, w_ref):
    scale = NA_HD ** -0.5
    cos = cos_ref[...]
    sin = sin_ref[...]
    qs_ref[...] = _rope(_head_rmsnorm(q_ref[...].astype(f32), qg_ref[...]), cos, sin).astype(bf16)
    ks_ref[...] = _rope(_head_rmsnorm(k_ref[...].astype(f32), kg_ref[...]), cos, sin).astype(bf16)
    groups = _na_row_groups()
    valid = valid_ref[...] > 0.0
    for r0, n, rs in groups:
        rows = pl.ds(r0 * GRID_W, n * GRID_W)
        d0 = r0 - rs
        kw = ks_ref[pl.ds(rs * GRID_W, NA_NK), :]
        s = (_dot_nt(qs_ref[rows, :], kw) * scale).reshape(n, GRID_W, NA_NK) + bias_ref[0, 0, d0:d0 + n]
        s_ref[rows, :] = jnp.where(valid[None], s, -1e30).reshape(n * GRID_W, NA_NK)
    s_ctx = _dot_nt(qs_ref[...], kc_ref[0, 0].astype(bf16)) * scale
    s_loc = s_ref[...]
    m = jnp.maximum(jnp.max(s_loc, axis=-1, keepdims=True), jnp.max(s_ctx, axis=-1, keepdims=True))
    p_loc = jnp.exp(s_loc - m)
    p_ctx = jnp.exp(s_ctx - m)
    w_ref[...] = 1.0 / (jnp.sum(p_loc, axis=-1, keepdims=True) + jnp.sum(p_ctx, axis=-1, keepdims=True))
    p_ref[...] = p_loc.astype(bf16)
    oc_ref[...] = _dot(p_ctx.astype(bf16), vc_ref[0, 0].astype(bf16))
    for r0, n, rs in groups:
        rows = pl.ds(r0 * GRID_W, n * GRID_W)
        vw = v_ref[pl.ds(rs * GRID_W, NA_NK), :]
        y_ref[rows, :] = ((_dot(p_ref[rows, :], vw) + oc_ref[rows, :]) * w_ref[rows, :]).astype(bf16)


def neighbourhood_attention(proj, cache_k, cache_v, layer, qg, kg, cos, sin, bias, valid):
    spec = lambda off: pl.BlockSpec((DEC_SEQ, NA_HD), lambda b, h: (b, off // NA_HD + h))
    cspec = pl.BlockSpec((1, 1, PAST_LEN, NA_HD), lambda b, h: (b, layer, 0, h))
    gspec = pl.BlockSpec((1, NA_HD), lambda b, h: (0, 0))
    tspec = pl.BlockSpec((DEC_SEQ, NA_HD), lambda b, h: (0, 0))
    return pl.pallas_call(
        _na_attn_kernel,
        grid=(DEC_BATCH, NA_HEADS),
        in_specs=[spec(OFF_NA_Q), spec(OFF_NA_K), spec(OFF_NA_V), cspec, cspec, gspec, gspec,
                  tspec, tspec,
                  pl.BlockSpec((1, 1, NA_WR, GRID_W, NA_NK), lambda b, h: (layer, h, 0, 0, 0)),
                  pl.BlockSpec((GRID_W, NA_NK), lambda b, h: (0, 0))],
        out_specs=pl.BlockSpec((DEC_SEQ, NA_HD), lambda b, h: (b, h)),
        out_shape=_BRANCH_BUF,
        scratch_shapes=[pltpu.VMEM((DEC_SEQ, NA_HD), bf16), pltpu.VMEM((DEC_SEQ, NA_HD), bf16),
                        pltpu.VMEM((DEC_SEQ, NA_NK), f32), pltpu.VMEM((DEC_SEQ, NA_NK), bf16),
                        pltpu.VMEM((DEC_SEQ, NA_HD), f32), pltpu.VMEM((DEC_SEQ, 1), f32)],
        compiler_params=_params(2), name="neighbourhood_attention",
    )(proj, proj, proj, cache_k, cache_v, qg, kg, cos, sin, bias, valid)


TAIL_TN = 512
N_TAIL_TILES = (D_IN_PAD - HEAD_COLS) // TAIL_TN
N_QKV_TILES = 3 * NA_W // TAIL_TN
DT_W = 2 * SSD_HEADS


def _tail_kernel(a_ref, b_ref, o_ref):
    j = pl.program_id(1)
    lane = lax.broadcasted_iota(jnp.int32, (D_MODEL, TAIL_TN), 1)

    @pl.when(j == 0)
    def _():
        o_ref[0] = a_ref[0]

    @pl.when((j >= 1) & (j <= N_QKV_TILES))
    def _():
        main = pltpu.roll(a_ref[0].astype(f32), TAIL_TN - DT_W, axis=1)
        nxt = pltpu.roll(b_ref[0].astype(f32), V7X_LANES - DT_W, axis=1)
        last_lane = lax.broadcasted_iota(jnp.int32, (D_MODEL, V7X_LANES), 1)
        last = jnp.where(last_lane < V7X_LANES - DT_W, main[:, TAIL_TN - V7X_LANES:], nxt)
        o_ref[0] = jnp.concatenate([main[:, :TAIL_TN - V7X_LANES], last], axis=1).astype(bf16)

    @pl.when(j == N_TAIL_TILES - 1)
    def _():
        a = a_ref[0].astype(f32)
        out = jnp.zeros((D_MODEL, TAIL_TN), f32)
        for gidx in range(SSD_GROUPS):
            base = gidx * V7X_LANES
            for src0, dst0 in ((gidx * SSD_HPG, base), (SSD_HEADS + gidx * SSD_HPG, base + SSD_HPG)):
                shift = (dst0 - src0) % TAIL_TN
                moved = a if shift == 0 else pltpu.roll(a, shift, axis=1)
                out = jnp.where((lane >= dst0) & (lane < dst0 + SSD_HPG), moved, out)
        o_ref[0] = out.astype(bf16)


def _split_w_in(w_in):
    w_in = w_in.astype(bf16)
    dt0 = IN_OFFSETS[7]
    a_tile0, dt_tile, b_tile0 = HEAD_COLS // TAIL_TN, dt0 // TAIL_TN, dt0 // V7X_LANES
    b_per_a = TAIL_TN // V7X_LANES
    last_b = (D_IN - 1) // V7X_LANES
    tail = pl.pallas_call(
        _tail_kernel, grid=(DEPTH, N_TAIL_TILES),
        in_specs=[pl.BlockSpec((1, D_MODEL, TAIL_TN),
                               lambda l, j: (l, 0, jnp.where(j == N_TAIL_TILES - 1, dt_tile, a_tile0 + j))),
                  pl.BlockSpec((1, D_MODEL, V7X_LANES),
                               lambda l, j: (l, 0, jnp.minimum(b_tile0 + b_per_a * j, last_b)))],
        out_specs=pl.BlockSpec((1, D_MODEL, TAIL_TN), lambda l, j: (l, 0, j)),
        out_shape=jax.ShapeDtypeStruct((DEPTH, D_MODEL, D_IN_PAD - HEAD_COLS), bf16),
        compiler_params=_params(2), name="build_w_in_tail",
    )(w_in, w_in)
    return w_in, tail


def _group_lanes(v):
    rows = []
    for gidx in range(SSD_GROUPS):
        sl = slice(gidx * SSD_HPG, (gidx + 1) * SSD_HPG)
        rows.append(jnp.concatenate([v[0, sl], v[1, sl], jnp.zeros((V7X_LANES - 2 * SSD_HPG,), f32)]))
    return jnp.stack(rows)


def _rope_tables():
    t = np.arange(DEC_SEQ)
    quarter = NA_HD // 4
    inv = ROPE_BASE ** (-np.arange(quarter, dtype=np.float32) / quarter)
    ang_r = (t // GRID_W).astype(np.float32)[:, None] * inv
    ang_c = (t % GRID_W).astype(np.float32)[:, None] * inv
    cos = np.concatenate([np.cos(ang_r), np.cos(ang_r), np.cos(ang_c), np.cos(ang_c)], axis=1)
    sin = np.concatenate([-np.sin(ang_r), np.sin(ang_r), -np.sin(ang_c), np.sin(ang_c)], axis=1)
    return jnp.asarray(cos, f32), jnp.asarray(sin, f32)


def _na_tables(rpb):
    cq = np.arange(GRID_W)
    kc = np.tile(np.arange(GRID_W), NA_WR)
    col_start = np.clip(cq - NA_WC // 2, 0, GRID_W - NA_WC)
    valid = (kc[None, :] >= col_start[:, None]) & (kc[None, :] < col_start[:, None] + NA_WC)
    col_off = np.clip(cq[None, :] - cq[:, None], 1 - NA_WC, NA_WC - 1) + NA_WC - 1
    onehot = (col_off[None, :, :] == np.arange(2 * NA_WC - 1)[:, None, None]).astype(np.float32)
    toep = jnp.einsum('lhic,cqk->lhiqk', rpb.astype(f32), jnp.asarray(onehot), precision=lax.Precision.HIGHEST)
    win = jnp.stack([toep[:, :, NA_WR - 1 - d:2 * NA_WR - 1 - d] for d in range(NA_WR)], axis=2)
    tables = win.transpose(0, 1, 2, 4, 3, 5).reshape(DEPTH, NA_HEADS, NA_WR, GRID_W, NA_NK)
    return tables, jnp.asarray(valid, f32)


def kernel(x_prompt, x_sample, cache_na_k, cache_na_v, state_lru_f, state_lru_b,
           state_ret_f, state_ret_b, state_ssd_f, state_ssd_b, c, c_ctx,
           norm1_g, norm2_g, w_ada, b_ada, w_in, w_gate, b_gate, w_branch, w_out,
           lru_conv_w, lru_conv_b, lru_wa, lru_ba, lru_wx, lru_bx, lru_lambda,
           ret_gn_g, ssd_conv_w, ssd_conv_b, ssd_a_log, ssd_dt_bias, ssd_d, ssd_norm_g,
           na_q_g, na_k_g, na_rpb, ffn_w_up, ffn_conv_w, ffn_conv_b, ffn_w_down):
    xs0 = x_sample.reshape(T_SAMPLE, D_MODEL)
    xp0 = x_prompt.reshape(T_PROMPT, D_MODEL)
    n_ptiles = N_TILES - N_SAMPLE_TILES
    cond = jnp.concatenate([c, c_ctx[None, :], jnp.zeros((N_COND_PAD - N_COND, D_MODEL), f32)], axis=0)
    mod_all = ada_modulation(cond, w_ada, b_ada)
    cos, sin = _rope_tables()
    bias, valid = _na_tables(na_rpb)
    hh = jnp.arange(RET_HEADS, dtype=f32)
    ret_la = jnp.stack([jnp.log1p(-jnp.exp2(-5.0 - hh)), jnp.log1p(-jnp.exp2(-5.5 - hh))], axis=1)
    cache_k = cache_na_k.reshape(DEC_BATCH, DEPTH, PAST_LEN, NA_W)
    cache_v = cache_na_v.reshape(DEC_BATCH, DEPTH, PAST_LEN, NA_W)
    lru_init = (state_lru_f.reshape(DEC_BATCH, DEPTH, 1, D_RNN), state_lru_b.reshape(DEC_BATCH, DEPTH, 1, D_RNN))

    w_head_b, w_tail_b = _split_w_in(w_in)
    w_branch_b, w_out_b, w_down_b = w_branch.astype(bf16), w_out.astype(bf16), ffn_w_down.astype(bf16)

    new = {k: None for k in ("k", "v", "lru_f", "lru_b", "ret_f", "ret_b", "ssd_f", "ssd_b")}
    for l in range(DEPTH):
        mod = mod_all[l].reshape(N_COND_PAD, 1, N_MOD * D_MODEL)
        g1 = norm1_g[l][None, :]
        if l == 0:
            proj, xn = in_projection(xs0, 0, 0, N_SAMPLE_TILES, g1, mod, w_head_b, w_tail_b, l)
            proj, xn = in_projection(xp0, 0, N_SAMPLE_TILES, n_ptiles, g1, mod, w_head_b, w_tail_b, l, (proj, xn))
        else:
            proj, xn = in_projection(x, 0, 0, N_TILES, g1, mod, w_head_b, w_tail_b, l)

        w4 = jnp.concatenate([lru_wa[l, 0], lru_wx[l, 0], lru_wa[l, 1], lru_wx[l, 1]], axis=-1).astype(bf16)
        b4 = jnp.concatenate([lru_ba[l, 0].reshape(LRU_BLOCKS, 1, LRU_BS), lru_bx[l, 0].reshape(LRU_BLOCKS, 1, LRU_BS),
                              lru_ba[l, 1].reshape(LRU_BLOCKS, 1, LRU_BS), lru_bx[l, 1].reshape(LRU_BLOCKS, 1, LRU_BS)],
                             axis=-1)
        lru_args = (lru_conv_w[l], lru_conv_b[l][None, :], w4, b4, lru_lambda[l])
        y_lru = lru_branch(proj, (None,), 0, DEC_BATCH, DEC_SEQ, *lru_args, lru_init, l)
        y_lru, new["lru_f"], new["lru_b"] = lru_branch(
            proj, (y_lru, new["lru_f"], new["lru_b"]), T_SAMPLE, BATCH, SEQ, *lru_args, None, l)

        gn = ret_gn_g[l][None, :]
        y_ret = retention_branch(proj, (None,), 0, DEC_BATCH, DEC_SEQ, ret_la, gn, (state_ret_f, state_ret_b), l)
        y_ret, new["ret_f"], new["ret_b"] = retention_branch(
            proj, (y_ret, new["ret_f"], new["ret_b"]), T_SAMPLE, BATCH, SEQ, ret_la, gn, None, l)

        prm = jnp.stack([_group_lanes(ssd_dt_bias[l]), _group_lanes(ssd_a_log[l])], axis=1)
        prm = jnp.concatenate([prm, jnp.zeros((SSD_GROUPS, V7X_SUBLANES - 2, V7X_LANES), f32)], axis=1)
        dvec = jnp.repeat(ssd_d[l], SSD_P)[None, :]
        ssd_args = (ssd_conv_w[l], ssd_conv_b[l][None, :], prm, dvec, ssd_norm_g[l][None, :])
        y_ssd = ssd_branch(proj, (None,), 0, DEC_BATCH, DEC_SEQ, *ssd_args, (state_ssd_f, state_ssd_b), l)
        y_ssd, new["ssd_f"], new["ssd_b"] = ssd_branch(
            proj, (y_ssd, new["ssd_f"], new["ssd_b"]), T_SAMPLE, BATCH, SEQ, *ssd_args, None, l)

        qg = na_q_g[l][None, :]
        kg = na_k_g[l][None, :]
        y_na = neighbourhood_attention(proj, cache_k, cache_v, l, qg, kg, cos, sin, bias, valid)
        y_na, new["k"], new["v"] = context_attention(proj, (y_na, new["k"], new["v"]), T_SAMPLE, qg, kg, l)

        merged = merge_branches(xn, (y_lru, y_ret, y_ssd, y_na), w_gate, b_gate[l][None, :], w_branch_b, l)
        if l == 0:
            x = residual_projection(merged, w_out_b, l, xs0, mod, 2, 0, N_SAMPLE_TILES)
            x = residual_projection(merged, w_out_b, l, xp0, mod, 2, N_SAMPLE_TILES, n_ptiles, x_tile0=0, buf=x)
        else:
            x = residual_projection(merged, w_out_b, l, x, mod, 2)
        hmid = ffn_up(x, norm2_g[l][None, :], mod, ffn_w_up, l, ffn_conv_w[l], ffn_conv_b[l][None, :])
        if l < DEPTH - 1:
            x = residual_projection(hmid, w_down_b, l, x, mod, 5)
        else:
            y_sample = residual_projection(hmid, w_down_b, l, x, mod, 5, 0, N_SAMPLE_TILES, out_rows=T_SAMPLE)
            y_prompt = residual_projection(hmid, w_down_b, l, x, mod, 5, N_SAMPLE_TILES, n_ptiles,
                                           out_tile0=0, out_rows=T_PROMPT)

    kv_shape = (BATCH, DEPTH, SEQ, NA_HEADS, NA_HD)
    return (y_prompt.reshape(BATCH, SEQ, D_MODEL), y_sample.reshape(DEC_BATCH, DEC_SEQ, D_MODEL),
            new["k"].reshape(kv_shape), new["v"].reshape(kv_shape),
            new["lru_f"].reshape(BATCH, DEPTH, D_RNN), new["lru_b"].reshape(BATCH, DEPTH, D_RNN),
            new["ret_f"], new["ret_b"], new["ssd_f"], new["ssd_b"])
```

```python
import functools
import math

import jax
import jax.numpy as jnp
import numpy as np
from jax import lax
from jax.experimental import pallas as pl
from jax.experimental.pallas import tpu as pltpu

D_MODEL = 2048
BATCH = 16
SEQ = 256
DEPTH = 2
DEC_BATCH = 8
DEC_SEQ = 1024
PAST_LEN = 256
GRID_W = 64
EPS = 1e-6
N_BRANCH = 4
BRANCH_W = 1024
N_MOD = 6
D_RNN = 1024
LRU_BLOCKS = 8
LRU_BS = D_RNN // LRU_BLOCKS
LRU_CONV = 4
LRU_C = 8.0
RET_HEADS = 4
RET_DK = 128
RET_DV = 256
SSD_HEADS = 16
SSD_P = 64
SSD_N = 128
SSD_GROUPS = 2
SSD_CONV = 4
D_SSD = SSD_HEADS * SSD_P
SSD_CONV_CH = D_SSD + 2 * SSD_GROUPS * SSD_N
NA_HEADS = 8
NA_HD = 128
NA_W = NA_HEADS * NA_HD
NA_WR = 8
NA_WC = 16
ROPE_BASE = 10000.0
D_FF = 5632
FFN_CONV = 3
IN_SIZES = (D_RNN, D_RNN,
            RET_HEADS * RET_DK, RET_HEADS * RET_DK, RET_HEADS * RET_DV, RET_HEADS * RET_DV,
            D_SSD, SSD_CONV_CH, 2 * SSD_HEADS,
            NA_W, NA_W, NA_W)
D_IN = sum(IN_SIZES)
IN_OFFSETS = tuple(int(s) for s in np.cumsum(IN_SIZES)[:-1])

V7X_LANES = 128
V7X_SUBLANES = 8
V7X_VMEM_LIMIT_BYTES = 56 * 1024 * 1024

TM = 1024
T_SAMPLE = DEC_BATCH * DEC_SEQ
T_PROMPT = BATCH * SEQ
T_ALL = T_SAMPLE + T_PROMPT
N_SAMPLE_TILES = T_SAMPLE // TM
N_TILES = T_ALL // TM
N_COND = DEC_BATCH + 1
N_COND_PAD = 16

SSD_HPG = SSD_HEADS // SSD_GROUPS
OFF_LRU_X = 0
OFF_LRU_G = OFF_LRU_X + D_RNN
OFF_RET_Q = OFF_LRU_G + D_RNN
OFF_RET_K = OFF_RET_Q + RET_HEADS * RET_DK
OFF_RET_V = OFF_RET_K + RET_HEADS * RET_DK
OFF_RET_G = OFF_RET_V + RET_HEADS * RET_DV
OFF_SSD_Z = OFF_RET_G + RET_HEADS * RET_DV
OFF_SSD_XBC = OFF_SSD_Z + D_SSD
OFF_NA_Q = OFF_SSD_XBC + SSD_CONV_CH
OFF_NA_K = OFF_NA_Q + NA_W
OFF_NA_V = OFF_NA_K + NA_W
OFF_SSD_DT = OFF_NA_V + NA_W
PROJ_TN = 1024
D_IN_PAD = -(-(OFF_SSD_DT + SSD_GROUPS * V7X_LANES) // PROJ_TN) * PROJ_TN
N_HEAD_TILES = OFF_NA_Q // PROJ_TN
HEAD_COLS = N_HEAD_TILES * PROJ_TN

RET_CHUNK = 256
SSD_CHUNK = 256
SSD_TB = 128

f32 = jnp.float32
bf16 = jnp.bfloat16

_ARB = "arbitrary"


def _params(n_axes):
    return pltpu.CompilerParams(dimension_semantics=(_ARB,) * n_axes,
                                vmem_limit_bytes=V7X_VMEM_LIMIT_BYTES)


def _mod_spec(k, tile0=0):
    return pl.BlockSpec((1, 1, D_MODEL), lambda i, j: (jnp.minimum(tile0 + i, N_SAMPLE_TILES), 0, k))


def _dot(a, b):
    return jnp.dot(a, b, preferred_element_type=f32)


def _dot_nt(a, b):
    return lax.dot_general(a, b, (((1,), (1,)), ((), ())), preferred_element_type=f32)


def _sigmoid(x):
    return 0.5 * jnp.tanh(0.5 * x) + 0.5


def _silu(x):
    return x * _sigmoid(x)


def _gelu_tanh(x):
    return 0.5 * x * (1.0 + jnp.tanh(math.sqrt(2.0 / math.pi) * (x + 0.044715 * (x * x * x))))


def _shift_rows(v, k):
    n = v.shape[0]
    row = lax.broadcasted_iota(jnp.int32, (n, 1), 0)
    if k == 1:
        return jnp.where(row == 0, 0.0, pltpu.roll(v, 1, axis=0))
    return jnp.where(row == n - 1, 0.0, pltpu.roll(v, n - 1, axis=0))


def _dwconv4(x, cw, cb):
    acc = _shift_rows(x * cw[0:1, :], 1) + x * cw[1:2, :]
    return cb + x * cw[2:3, :] + _shift_rows(acc, 1) + _shift_rows(x * cw[3:4, :], -1)


def _softplus(x):
    return jnp.maximum(x, 0.0) + jnp.log1p(jnp.exp(-jnp.abs(x)))


def _drop_refs(body, n, *refs):
    body(*refs[n:])


def _inplace_call(body, bufs, grid, in_specs, args, out_specs, out_shape, scratch, name):
    held = [(k, b) for k, b in enumerate(bufs) if b is not None]
    if held:
        body = functools.partial(_drop_refs, body, len(held))
        in_specs = [pl.BlockSpec(memory_space=pl.ANY)] * len(held) + list(in_specs)
        args = [b for _, b in held] + list(args)
    aliases = {pos: k for pos, (k, _) in enumerate(held)}
    return pl.pallas_call(
        body, grid=grid, in_specs=in_specs, out_specs=out_specs, out_shape=out_shape,
        scratch_shapes=scratch, input_output_aliases=aliases,
        compiler_params=_params(len(grid)), name=name,
    )(*args)


_BRANCH_BUF = jax.ShapeDtypeStruct((T_ALL, BRANCH_W), bf16)


ADA_TN = 1024


def _ada_kernel(c_ref, w_ref, b_ref, o_ref):
    c = _silu(c_ref[...]).astype(bf16)
    o_ref[0] = _dot(c, w_ref[0].astype(bf16)) + b_ref[0]


def ada_modulation(cond, w_ada, b_ada):
    n = N_MOD * D_MODEL
    return pl.pallas_call(
        _ada_kernel,
        grid=(DEPTH, n // ADA_TN),
        in_specs=[pl.BlockSpec((N_COND_PAD, D_MODEL), lambda l, j: (0, 0)),
                  pl.BlockSpec((1, D_MODEL, ADA_TN), lambda l, j: (l, 0, j)),
                  pl.BlockSpec((1, 1, ADA_TN), lambda l, j: (l, 0, j))],
        out_specs=pl.BlockSpec((1, N_COND_PAD, ADA_TN), lambda l, j: (l, 0, j)),
        out_shape=jax.ShapeDtypeStruct((DEPTH, N_COND_PAD, n), f32),
        compiler_params=_params(2),
        name="ada_modulation",
    )(cond, w_ada, b_ada.reshape(DEPTH, 1, n))


NORM_ROWS = 128
NORM_SPLIT = 2


def _modulated_norm_parts(x_ref, g_ref, sc_ref, sh_ref, xn_ref):
    gs = g_ref[...] * (1.0 + sc_ref[0])
    sh = sh_ref[0]
    part = TM // NORM_SPLIT
    for p in range(NORM_SPLIT):
        pieces = []
        for r in range(part // NORM_ROWS):
            rows = pl.ds(p * part + r * NORM_ROWS, NORM_ROWS)
            x = x_ref[rows, :]
            y = x * lax.rsqrt(jnp.mean(x * x, axis=-1, keepdims=True) + EPS)
            pieces.append((y * gs + sh).astype(bf16))
        xn = jnp.concatenate(pieces, axis=0)
        xn_ref[pl.ds(p * part, part), :] = xn
        yield xn


def _in_proj_kernel(x_ref, g_ref, sc_ref, sh_ref, wh_ref, wt_ref, o_ref, xn_ref):
    j = pl.program_id(1)

    @pl.when(j == 0)
    def _():
        w = wh_ref[0]
        outs = [_dot(xn, w).astype(bf16) for xn in _modulated_norm_parts(x_ref, g_ref, sc_ref, sh_ref, xn_ref)]
        o_ref[...] = jnp.concatenate(outs, axis=0)

    @pl.when((j > 0) & (j < N_HEAD_TILES))
    def _():
        o_ref[...] = _dot(xn_ref[...], wh_ref[0]).astype(bf16)

    @pl.when(j >= N_HEAD_TILES)
    def _():
        o_ref[...] = _dot(xn_ref[...], wt_ref[0]).astype(bf16)


def in_projection(x, x_tile0, tile0, n_tiles, g, mod, w_head, w_tail, layer, bufs=(None, None)):
    n = D_IN_PAD
    n_tail = w_tail.shape[2] // PROJ_TN
    return _inplace_call(
        _in_proj_kernel, bufs, (n_tiles, n // PROJ_TN),
        [pl.BlockSpec((TM, D_MODEL), lambda i, j: (x_tile0 + i, 0)),
         pl.BlockSpec((1, D_MODEL), lambda i, j: (0, 0)),
         _mod_spec(1, tile0), _mod_spec(0, tile0),
         pl.BlockSpec((1, D_MODEL, PROJ_TN), lambda i, j: (layer, 0, jnp.minimum(j, N_HEAD_TILES - 1))),
         pl.BlockSpec((1, D_MODEL, PROJ_TN),
                      lambda i, j: (layer, 0, jnp.clip(j - N_HEAD_TILES, 0, n_tail - 1)))],
        [x, g, mod, mod, w_head, w_tail],
        [pl.BlockSpec((TM, PROJ_TN), lambda i, j: (tile0 + i, j)),
         pl.BlockSpec((TM, D_MODEL), lambda i, j: (tile0 + i, 0))],
        [jax.ShapeDtypeStruct((T_ALL, n), bf16), jax.ShapeDtypeStruct((T_ALL, D_MODEL), bf16)],
        [], "in_projection")


MERGE_TN = 256


def _merge_kernel(xn_ref, *refs):
    y_refs, wg_refs, bg_refs = refs[0:4], refs[4:8], refs[8:12]
    wb_ref, o_ref = refs[12], refs[13]
    xn = xn_ref[...]
    acc = None
    for n in range(N_BRANCH):
        gate = _sigmoid(_dot(xn, wg_refs[n][0].astype(bf16)) + bg_refs[n][...])
        term = gate * _dot(y_refs[n][...], wb_ref[0, n])
        acc = term if acc is None else acc + term
    o_ref[...] = acc.astype(bf16)


def merge_branches(xn, ys, w_gate, b_gate, w_branch, layer):
    nj = D_MODEL // MERGE_TN
    y_spec = pl.BlockSpec((TM, BRANCH_W), lambda i, j: (i, 0))
    wg_specs = [pl.BlockSpec((1, D_MODEL, MERGE_TN), lambda i, j, n=n: (layer, 0, n * nj + j))
                for n in range(N_BRANCH)]
    bg_specs = [pl.BlockSpec((1, MERGE_TN), lambda i, j, n=n: (0, n * nj + j)) for n in range(N_BRANCH)]
    return pl.pallas_call(
        _merge_kernel,
        grid=(N_TILES, nj),
        in_specs=[pl.BlockSpec((TM, D_MODEL), lambda i, j: (i, 0)),
                  y_spec, y_spec, y_spec, y_spec, *wg_specs, *bg_specs,
                  pl.BlockSpec((1, N_BRANCH, BRANCH_W, MERGE_TN), lambda i, j: (layer, 0, 0, j))],
        out_specs=pl.BlockSpec((TM, MERGE_TN), lambda i, j: (i, j)),
        out_shape=jax.ShapeDtypeStruct((T_ALL, D_MODEL), bf16),
        compiler_params=_params(2),
        name="merge_branches",
    )(xn, *ys, *([w_gate] * N_BRANCH), *([b_gate] * N_BRANCH), w_branch)


RES_TN = 512
RES_TN_SHORT_K = 1024


def _residual_kernel(a_ref, w_ref, x_ref, gv_ref, o_ref):
    o_ref[...] = x_ref[...] + gv_ref[0] * _dot(a_ref[...], w_ref[0])


def residual_projection(a, w, layer, x, mod, k_mod, tile0=0, n_tiles=N_TILES, x_tile0=None, out_tile0=None,
                        out_rows=T_ALL, buf=None):
    kdim = a.shape[1]
    tn = RES_TN_SHORT_K if kdim == D_MODEL else RES_TN
    x_tile0 = tile0 if x_tile0 is None else x_tile0
    out_tile0 = tile0 if out_tile0 is None else out_tile0
    return _inplace_call(
        _residual_kernel, (buf,), (n_tiles, D_MODEL // tn),
        [pl.BlockSpec((TM, kdim), lambda i, j: (tile0 + i, 0)),
         pl.BlockSpec((1, kdim, tn), lambda i, j: (layer, 0, j)),
         pl.BlockSpec((TM, tn), lambda i, j: (x_tile0 + i, j)),
         pl.BlockSpec((1, 1, tn),
                      lambda i, j: (jnp.minimum(tile0 + i, N_SAMPLE_TILES), 0, k_mod * (D_MODEL // tn) + j))],
        [a, w, x, mod],
        pl.BlockSpec((TM, tn), lambda i, j: (out_tile0 + i, j)),
        jax.ShapeDtypeStruct((out_rows, D_MODEL), f32), [], "residual_projection")


FFN_TN = 512


def _ffn_up_kernel(x_ref, g_ref, sc_ref, sh_ref, wa_ref, wv_ref, cw_ref, cb_ref, o_ref, xn_ref):
    i = pl.program_id(0)

    def epilogue(a, v):
        seq_len = jnp.where(i < N_SAMPLE_TILES, DEC_SEQ, SEQ)
        pos = lax.broadcasted_iota(jnp.int32, (TM, 1), 0) & (seq_len - 1)
        prev = jnp.where(pos == 0, 0.0, pltpu.roll(a, 1, axis=0))
        nxt = jnp.where(pos == seq_len - 1, 0.0, pltpu.roll(a, TM - 1, axis=0))
        cw = cw_ref[...]
        conv = cb_ref[...] + prev * cw[0:1, :] + a * cw[1:2, :] + nxt * cw[2:3, :]
        o_ref[...] = (_gelu_tanh(conv) * v).astype(bf16)

    @pl.when(pl.program_id(1) == 0)
    def _():
        wa = wa_ref[0].astype(bf16)
        wv = wv_ref[0].astype(bf16)
        parts = [(_dot(xn, wa), _dot(xn, wv)) for xn in _modulated_norm_parts(x_ref, g_ref, sc_ref, sh_ref, xn_ref)]
        epilogue(jnp.concatenate([p[0] for p in parts], axis=0), jnp.concatenate([p[1] for p in parts], axis=0))

    @pl.when(pl.program_id(1) > 0)
    def _():
        xn = xn_ref[...]
        epilogue(_dot(xn, wa_ref[0].astype(bf16)), _dot(xn, wv_ref[0].astype(bf16)))


def ffn_up(x, g, mod, w_up, layer, conv_w, conv_b):
    nj = D_FF // FFN_TN
    return pl.pallas_call(
        _ffn_up_kernel,
        grid=(N_TILES, nj),
        in_specs=[pl.BlockSpec((TM, D_MODEL), lambda i, j: (i, 0)),
                  pl.BlockSpec((1, D_MODEL), lambda i, j: (0, 0)),
                  _mod_spec(4), _mod_spec(3),
                  pl.BlockSpec((1, D_MODEL, FFN_TN), lambda i, j: (layer, 0, j)),
                  pl.BlockSpec((1, D_MODEL, FFN_TN), lambda i, j: (layer, 0, nj + j)),
                  pl.BlockSpec((FFN_CONV, FFN_TN), lambda i, j: (0, j)),
                  pl.BlockSpec((1, FFN_TN), lambda i, j: (0, j))],
        out_specs=pl.BlockSpec((TM, FFN_TN), lambda i, j: (i, j)),
        out_shape=jax.ShapeDtypeStruct((T_ALL, D_FF), bf16),
        scratch_shapes=[pltpu.VMEM((TM, D_MODEL), bf16)],
        compiler_params=_params(2),
        name="ffn_up",
    )(x, g, mod, mod, w_up, w_up, conv_w, conv_b)


LRU_CB_LONG = D_RNN
LRU_CB_SHORT = D_RNN


def _lru_kernel(*refs, seq_len, has_init, cbw):
    if has_init:
        (x_ref, g_ref, cw_ref, cb_ref, w4_ref, b4_ref, lam_ref, h0f_ref, h0b_ref,
         y_ref, af_ref, uf_ref, ab_ref, ub_ref, hf_ref, hb_ref) = refs
    else:
        (x_ref, g_ref, cw_ref, cb_ref, w4_ref, b4_ref, lam_ref,
         y_ref, sf_ref, sb_ref, af_ref, uf_ref, ab_ref, ub_ref, hf_ref, hb_ref) = refs
    L = seq_len
    S = V7X_SUBLANES
    xc = _dwconv4(x_ref[...].astype(f32), cw_ref[...], cb_ref[...])
    sp = _softplus(-lam_ref[...])
    for n in range(cbw // LRU_BS):
        cols = slice(n * LRU_BS, (n + 1) * LRU_BS)
        xcn = xc[:, cols]
        z = _dot(xcn.astype(bf16), w4_ref[n]) + b4_ref[n]
        for d, (a_ref, u_ref) in enumerate(((af_ref, uf_ref), (ab_ref, ub_ref))):
            r = _sigmoid(z[:, (2 * d) * LRU_BS:(2 * d + 1) * LRU_BS])
            ig = _sigmoid(z[:, (2 * d + 1) * LRU_BS:(2 * d + 2) * LRU_BS])
            log_a = (-LRU_C) * r * sp[d:d + 1, cols]
            a = jnp.exp(log_a)
            a_ref[:, cols] = a
            gain2 = -jnp.tanh(log_a) * (a * a + 1.0)
            gain = jnp.where(gain2 > 0.0, gain2 * lax.rsqrt(gain2), 0.0)
            u_ref[:, cols] = gain * (ig * xcn)

    if has_init:
        h0f = h0f_ref[0, 0]
        h0b = h0b_ref[0, 0]
    else:
        h0f = jnp.zeros((1, cbw), f32)
        h0b = jnp.zeros((1, cbw), f32)

    def step(i, carry):
        hf, hb = carry
        base_f = pl.multiple_of(i * S, S)
        base_b = pl.multiple_of(L - S - i * S, S)
        for r in range(S):
            tf = pl.ds(base_f + r, 1)
            tb = pl.ds(base_b + (S - 1 - r), 1)
            hf = af_ref[tf, :] * hf + uf_ref[tf, :]
            hb = ab_ref[tb, :] * hb + ub_ref[tb, :]
            hf_ref[tf, :] = hf
            hb_ref[tb, :] = hb
        return hf, hb

    hf, hb = lax.fori_loop(0, L // S, step, (h0f, h0b))
    if not has_init:
        sf_ref[0, 0] = hf
        sb_ref[0, 0] = hb
    y_ref[...] = ((hf_ref[...] + hb_ref[...]) * _gelu_tanh(g_ref[...].astype(f32))).astype(bf16)


def lru_branch(proj, bufs, row0, n_seq, seq_len, cw, cb, w4, b4, lam, init, layer):
    has_init = init is not None
    cbw = LRU_CB_LONG if seq_len == DEC_SEQ else LRU_CB_SHORT
    ncb = D_RNN // cbw
    nb = cbw // LRU_BS
    rb0 = row0 // seq_len
    in_specs = [pl.BlockSpec((seq_len, cbw), lambda s, c: (rb0 + s, OFF_LRU_X // cbw + c)),
                pl.BlockSpec((seq_len, cbw), lambda s, c: (rb0 + s, OFF_LRU_G // cbw + c)),
                pl.BlockSpec((LRU_CONV, cbw), lambda s, c: (0, c)),
                pl.BlockSpec((1, cbw), lambda s, c: (0, c)),
                pl.BlockSpec((nb, LRU_BS, 4 * LRU_BS), lambda s, c: (c, 0, 0)),
                pl.BlockSpec((nb, 1, 4 * LRU_BS), lambda s, c: (c, 0, 0)),
                pl.BlockSpec((2, cbw), lambda s, c: (0, c))]
    args = [proj, proj, cw, cb, w4, b4, lam]
    state_spec = pl.BlockSpec((1, 1, 1, cbw), lambda s, c: (s, layer, 0, c))
    y_spec = pl.BlockSpec((seq_len, cbw), lambda s, c: (rb0 + s, c))
    if has_init:
        in_specs += [state_spec, state_spec]
        args += [init[0], init[1]]
        out_specs, out_shape = y_spec, _BRANCH_BUF
    else:
        st = jax.ShapeDtypeStruct((n_seq, DEPTH, 1, D_RNN), f32)
        out_specs, out_shape = [y_spec, state_spec, state_spec], [_BRANCH_BUF, st, st]
    scratch = [pltpu.VMEM((seq_len, cbw), f32) for _ in range(6)]
    return _inplace_call(functools.partial(_lru_kernel, seq_len=seq_len, has_init=has_init, cbw=cbw), bufs,
                         (n_seq, ncb), in_specs, args, out_specs, out_shape, scratch, "lru_branch")


def _ret_kernel(*refs, seq_len, has_init):
    if has_init:
        (la_ref, q_ref, k_ref, v_ref, g_ref, gn_ref, s0f_ref, s0b_ref, y_ref, acc_ref) = refs
    else:
        (la_ref, q_ref, k_ref, v_ref, g_ref, gn_ref, y_ref, sf_ref, sb_ref, acc_ref) = refs
    T = min(RET_CHUNK, seq_len)
    nc = seq_len // T
    tt = lax.broadcasted_iota(jnp.int32, (T, T), 0)
    ss = lax.broadcasted_iota(jnp.int32, (T, T), 1)
    diff = (tt - ss).astype(f32)
    tcol = lax.broadcasted_iota(jnp.int32, (T, 1), 0).astype(f32)
    scale = RET_DK ** -0.5
    for h in range(RET_HEADS):
        la_f = la_ref[h, 0]
        la_b = la_ref[h, 1]
        kcols = pl.ds(h * RET_DK, RET_DK)
        vcols = pl.ds(h * RET_DV, RET_DV)
        dsum = (jnp.where(tt >= ss, jnp.exp(la_f * diff), 0.0)
                + jnp.where(ss >= tt, jnp.exp(-la_b * diff), 0.0))

        def chunk(c):
            rows = pl.ds(c * T, T)
            q = q_ref[rows, kcols].astype(f32)
            ks = k_ref[rows, kcols].astype(f32) * scale
            v = v_ref[rows, vcols].astype(f32)
            return rows, q, ks, v

        s_f = s0f_ref[0, 0, h] if has_init else None
        for c in range(nc):
            rows, q, ks, v = chunk(c)
            scores = _dot_nt(q.astype(bf16), ks.astype(bf16)) * dsum
            y = _dot(scores.astype(bf16), v.astype(bf16))
            if s_f is not None:
                y = y + _dot((q * jnp.exp(la_f * (tcol + 1.0))).astype(bf16), s_f.astype(bf16))
            acc_ref[rows, :] = y
            if c < nc - 1 or not has_init:
                upd = _dot(ks.T.astype(bf16), (v * jnp.exp(la_f * (T - 1.0 - tcol))).astype(bf16))
                s_f = upd if s_f is None else jnp.exp(la_f * T) * s_f + upd
        s_b = s0b_ref[0, 0, h] if has_init else None
        for c in reversed(range(nc)):
            rows, q, ks, v = chunk(c)
            if s_b is not None:
                acc_ref[rows, :] += _dot((q * jnp.exp(la_b * (T - tcol))).astype(bf16), s_b.astype(bf16))
            if c > 0 or not has_init:
                upd = _dot(ks.T.astype(bf16), (v * jnp.exp(la_b * tcol)).astype(bf16))
                s_b = upd if s_b is None else jnp.exp(la_b * T) * s_b + upd
        if not has_init:
            sf_ref[0, 0, h] = s_f
            sb_ref[0, 0, h] = s_b
        y = acc_ref[...]
        mu = jnp.mean(y, axis=-1, keepdims=True)
        yc = y - mu
        var = jnp.mean(yc * yc, axis=-1, keepdims=True)
        y = yc * lax.rsqrt(var + EPS) * gn_ref[:, vcols]
        y_ref[:, vcols] = (y * _silu(g_ref[:, vcols].astype(f32))).astype(bf16)


def retention_branch(proj, bufs, row0, n_seq, seq_len, la, gn, init, layer):
    has_init = init is not None
    rb0 = row0 // seq_len
    qk_w, vg_w = RET_HEADS * RET_DK, RET_HEADS * RET_DV
    in_specs = [pl.BlockSpec(memory_space=pltpu.SMEM),
                pl.BlockSpec((seq_len, qk_w), lambda s: (rb0 + s, OFF_RET_Q // qk_w)),
                pl.BlockSpec((seq_len, qk_w), lambda s: (rb0 + s, OFF_RET_K // qk_w)),
                pl.BlockSpec((seq_len, vg_w), lambda s: (rb0 + s, OFF_RET_V // vg_w)),
                pl.BlockSpec((seq_len, vg_w), lambda s: (rb0 + s, OFF_RET_G // vg_w)),
                pl.BlockSpec((1, vg_w), lambda s: (0, 0))]
    args = [la, proj, proj, proj, proj, gn]
    state_spec = pl.BlockSpec((1, 1, RET_HEADS, RET_DK, RET_DV), lambda s: (s, layer, 0, 0, 0))
    y_spec = pl.BlockSpec((seq_len, vg_w), lambda s: (rb0 + s, 0))
    if has_init:
        in_specs += [state_spec, state_spec]
        args += [init[0], init[1]]
        out_specs, out_shape = y_spec, _BRANCH_BUF
    else:
        st = jax.ShapeDtypeStruct((n_seq, DEPTH, RET_HEADS, RET_DK, RET_DV), f32)
        out_specs, out_shape = [y_spec, state_spec, state_spec], [_BRANCH_BUF, st, st]
    return _inplace_call(functools.partial(_ret_kernel, seq_len=seq_len, has_init=has_init), bufs,
                         (n_seq,), in_specs, args, out_specs, out_shape,
                         [pltpu.VMEM((seq_len, RET_DV), f32)], "retention_branch")


SSD_GW = SSD_HPG * SSD_P


def _split3(x):
    h1 = x.astype(bf16)
    r1 = x - h1.astype(f32)
    h2 = r1.astype(bf16)
    h3 = (r1 - h2.astype(f32)).astype(bf16)
    return h1, h2, h3


def _dot_exact_rhs(m, x):
    h1, h2, h3 = _split3(x)
    return _dot(m, h1) + _dot(m, h2) + _dot(m, h3)


def _dot_exact_lhs(x, m):
    h1, h2, h3 = _split3(x)
    return _dot(h1, m) + _dot(h2, m) + _dot(h3, m)


def _heads_to_lanes(s_ref, st_ref):
    for hh in range(SSD_HPG):
        st_ref[:, pl.ds(hh * SSD_P, SSD_P)] = s_ref[0, 0, hh]
    return st_ref[...]


def _ssd_kernel(*refs, seq_len, has_init):
    if has_init:
        (z_ref, x_ref, b_ref, c_ref, dt_ref, cwx_ref, cbx_ref, cwb_ref, cbb_ref, cwc_ref, cbc_ref,
         prm_ref, dvec_ref, ng_ref, s0f_ref, s0b_ref,
         y_ref, xs_ref, bs_ref, cs_ref, acc_ref, rb_ref, dts_ref, yn_ref, st_ref) = refs
    else:
        (z_ref, x_ref, b_ref, c_ref, dt_ref, cwx_ref, cbx_ref, cwb_ref, cbb_ref, cwc_ref, cbc_ref,
         prm_ref, dvec_ref, ng_ref,
         y_ref, sf_ref, sb_ref, xs_ref, bs_ref, cs_ref, acc_ref, rb_ref, dts_ref, yn_ref, st_ref) = refs
    g = pl.program_id(1)
    L = seq_len
    T = min(SSD_CHUNK, L)
    nc = L // T
    H = SSD_HPG
    xs_ref[...] = _silu(_dwconv4(x_ref[...].astype(f32), cwx_ref[...], cbx_ref[...]))
    bs_ref[...] = _silu(_dwconv4(b_ref[...].astype(f32), cwb_ref[...], cbb_ref[...]))
    cs_ref[...] = _silu(_dwconv4(c_ref[...].astype(f32), cwc_ref[...], cbc_ref[...]))
    prm = prm_ref[0]
    a_neg = -jnp.exp(prm[1:2, :])
    tt = lax.broadcasted_iota(jnp.int32, (T, T), 0)
    ss = lax.broadcasted_iota(jnp.int32, (T, T), 1)
    lower = tt >= ss
    upper = ss >= tt
    tri_l = jnp.where(lower, 1.0, 0.0).astype(bf16)
    tri_u = jnp.where(upper, 1.0, 0.0).astype(bf16)
    lower_b = lower[:SSD_TB, :SSD_TB]
    upper_b = upper[:SSD_TB, :SSD_TB]
    er = lax.broadcasted_iota(jnp.int32, (V7X_LANES, SSD_GW), 0)
    ec = lax.broadcasted_iota(jnp.int32, (V7X_LANES, SSD_GW), 1) // SSD_P
    exp_f = jnp.where(er == ec, 1.0, 0.0).astype(bf16)
    exp_b = jnp.where(er == ec + H, 1.0, 0.0).astype(bf16)

    def expand(w, e):
        hi = w.astype(bf16)
        lo = (w - hi.astype(f32)).astype(bf16)
        return _dot(hi, e) + _dot(lo, e)

    s_f = _heads_to_lanes(s0f_ref, st_ref) if has_init else None
    for c in range(nc):
        rows = pl.ds(c * T, T)
        dt = _softplus(dt_ref[rows, :].astype(f32) + prm[0:1, :])
        dts_ref[rows, :] = dt
        da = dt * a_neg
        cum = _dot_exact_rhs(tri_l, da)
        rsum = (cum[T - 1:T, :] - cum) + da
        rb_ref[rows, :] = rsum
        da_t = da.T
        cum_t = _dot_exact_lhs(da_t, tri_u)
        rsum_t = (cum_t[:, T - 1:T] - cum_t) + da_t
        dt_t = dt.T
        bmat = bs_ref[rows, :]
        cmat = cs_ref[rows, :]
        xmat = xs_ref[rows, :]
        gmat = _dot_nt(cmat.astype(bf16), bmat.astype(bf16))
        for hh in range(H):
            cf, cf_t, dtf_t = cum[:, hh:hh + 1], cum_t[hh:hh + 1, :], dt_t[hh:hh + 1, :]
            rb, rb_t, dtb_t = rsum[:, H + hh:H + hh + 1], rsum_t[H + hh:H + hh + 1, :], dt_t[H + hh:H + hh + 1, :]
            block_rows = []
            for bi in range(T // SSD_TB):
                rr = slice(bi * SSD_TB, (bi + 1) * SSD_TB)
                blocks = []
                for bj in range(T // SSD_TB):
                    cc = slice(bj * SSD_TB, (bj + 1) * SSD_TB)
                    if bi > bj:
                        blk = jnp.exp(cf[rr] - cf_t[:, cc]) * dtf_t[:, cc]
                    elif bi < bj:
                        blk = jnp.exp(rb[rr] - rb_t[:, cc]) * dtb_t[:, cc]
                    else:
                        blk = (jnp.where(lower_b, jnp.exp(cf[rr] - cf_t[:, cc]), 0.0) * dtf_t[:, cc]
                               + jnp.where(upper_b, jnp.exp(rb[rr] - rb_t[:, cc]), 0.0) * dtb_t[:, cc])
                    blocks.append(gmat[rr, cc] * blk)
                block_rows.append(jnp.concatenate(blocks, axis=1))
            m = jnp.concatenate(block_rows, axis=0)
            xh = xmat[:, hh * SSD_P:(hh + 1) * SSD_P]
            acc_ref[rows, pl.ds(hh * SSD_P, SSD_P)] = _dot(m.astype(bf16), xh.astype(bf16))
        ecum = jnp.exp(cum)
        if s_f is not None:
            acc_ref[rows, :] += _dot(cmat.astype(bf16), s_f.astype(bf16)) * expand(ecum, exp_f)
        if c < nc - 1 or not has_init:
            tail = jnp.exp(cum[T - 1:T, :] - cum) * dt
            xw = (xmat * expand(tail, exp_f)).astype(bf16)
            upd = _dot(bmat.T.astype(bf16), xw)
            if s_f is None:
                s_f = upd
            else:
                s_f = s_f * expand(jnp.broadcast_to(ecum[T - 1:T, :], (V7X_SUBLANES, V7X_LANES)),
                                   exp_f)[0:1, :] + upd
    s_b = _heads_to_lanes(s0b_ref, st_ref) if has_init else None
    for c in reversed(range(nc)):
        rows = pl.ds(c * T, T)
        dt = dts_ref[rows, :]
        rsum = rb_ref[rows, :]
        ers = jnp.exp(rsum)
        bmat = bs_ref[rows, :]
        cmat = cs_ref[rows, :]
        xmat = xs_ref[rows, :]
        if s_b is not None:
            acc_ref[rows, :] += _dot(cmat.astype(bf16), s_b.astype(bf16)) * expand(ers, exp_b)
        if c > 0 or not has_init:
            tail = jnp.exp(rsum[0:1, :] - rsum) * dt
            xw = (xmat * expand(tail, exp_b)).astype(bf16)
            upd = _dot(bmat.T.astype(bf16), xw)
            if s_b is None:
                s_b = upd
            else:
                s_b = s_b * expand(jnp.broadcast_to(ers[0:1, :], (V7X_SUBLANES, V7X_LANES)),
                                   exp_b)[0:1, :] + upd
    if not has_init:
        for hh in range(H):
            sf_ref[0, 0, hh] = s_f[:, hh * SSD_P:(hh + 1) * SSD_P]
            sb_ref[0, 0, hh] = s_b[:, hh * SSD_P:(hh + 1) * SSD_P]
    yg = (acc_ref[...] + xs_ref[...] * dvec_ref[...]) * _silu(z_ref[...].astype(f32))
    yn_ref[g] = yg

    @pl.when(g == SSD_GROUPS - 1)
    def _():
        ssq = None
        for gg in range(SSD_GROUPS):
            y = yn_ref[gg]
            s = jnp.sum(y * y, axis=-1, keepdims=True)
            ssq = s if ssq is None else ssq + s
        inv = lax.rsqrt(ssq * (1.0 / D_SSD) + EPS)
        for gg in range(SSD_GROUPS):
            cols = pl.ds(gg * SSD_GW, SSD_GW)
            y_ref[:, cols] = (yn_ref[gg] * inv * ng_ref[:, cols]).astype(bf16)


def ssd_branch(proj, bufs, row0, n_seq, seq_len, conv_w, conv_b, prm, dvec, ng, init, layer):
    has_init = init is not None
    rb0 = row0 // seq_len
    xoff = OFF_SSD_XBC
    boff = OFF_SSD_XBC + D_SSD
    coff = boff + SSD_GROUPS * SSD_N
    in_specs = [pl.BlockSpec((seq_len, SSD_GW), lambda s, g: (rb0 + s, OFF_SSD_Z // SSD_GW + g)),
                pl.BlockSpec((seq_len, SSD_GW), lambda s, g: (rb0 + s, xoff // SSD_GW + g)),
                pl.BlockSpec((seq_len, SSD_N), lambda s, g: (rb0 + s, boff // SSD_N + g)),
                pl.BlockSpec((seq_len, SSD_N), lambda s, g: (rb0 + s, coff // SSD_N + g)),
                pl.BlockSpec((seq_len, V7X_LANES), lambda s, g: (rb0 + s, OFF_SSD_DT // V7X_LANES + g)),
                pl.BlockSpec((SSD_CONV, SSD_GW), lambda s, g: (0, g)),
                pl.BlockSpec((1, SSD_GW), lambda s, g: (0, g)),
                pl.BlockSpec((SSD_CONV, SSD_N), lambda s, g: (0, D_SSD // SSD_N + g)),
                pl.BlockSpec((1, SSD_N), lambda s, g: (0, D_SSD // SSD_N + g)),
                pl.BlockSpec((SSD_CONV, SSD_N), lambda s, g: (0, D_SSD // SSD_N + SSD_GROUPS + g)),
                pl.BlockSpec((1, SSD_N), lambda s, g: (0, D_SSD // SSD_N + SSD_GROUPS + g)),
                pl.BlockSpec((1, V7X_SUBLANES, V7X_LANES), lambda s, g: (g, 0, 0)),
                pl.BlockSpec((1, SSD_GW), lambda s, g: (0, g)),
                pl.BlockSpec((1, D_SSD), lambda s, g: (0, 0))]
    args = [proj, proj, proj, proj, proj, conv_w, conv_b, conv_w, conv_b, conv_w, conv_b, prm, dvec, ng]
    state_spec = pl.BlockSpec((1, 1, SSD_HPG, SSD_N, SSD_P), lambda s, g: (s, layer, g, 0, 0))
    y_spec = pl.BlockSpec((seq_len, D_SSD), lambda s, g: (rb0 + s, 0))
    if has_init:
        in_specs += [state_spec, state_spec]
        args += [init[0], init[1]]
        out_specs, out_shape = y_spec, _BRANCH_BUF
    else:
        st = jax.ShapeDtypeStruct((n_seq, DEPTH, SSD_HEADS, SSD_N, SSD_P), f32)
        out_specs, out_shape = [y_spec, state_spec, state_spec], [_BRANCH_BUF, st, st]
    scratch = [pltpu.VMEM((seq_len, SSD_GW), f32),
               pltpu.VMEM((seq_len, SSD_N), f32),
               pltpu.VMEM((seq_len, SSD_N), f32),
               pltpu.VMEM((seq_len, SSD_GW), f32),
               pltpu.VMEM((seq_len, V7X_LANES), f32),
               pltpu.VMEM((seq_len, V7X_LANES), f32),
               pltpu.VMEM((SSD_GROUPS, seq_len, SSD_GW), f32),
               pltpu.VMEM((SSD_N, SSD_GW), f32)]
    return _inplace_call(functools.partial(_ssd_kernel, seq_len=seq_len, has_init=has_init), bufs,
                         (n_seq, SSD_GROUPS), in_specs, args, out_specs, out_shape, scratch, "ssd_branch")


def _head_rmsnorm(x, g):
    return x * lax.rsqrt(jnp.mean(x * x, axis=-1, keepdims=True) + EPS) * g


CTX_HB = 4
CTX_W = CTX_HB * NA_HD


def _ctx_attn_kernel(q_ref, k_ref, v_ref, qg_ref, kg_ref, y_ref, ko_ref, vo_ref):
    scale = NA_HD ** -0.5
    vo_ref[0, 0] = v_ref[...].astype(f32)
    for h in range(CTX_HB):
        cols = pl.ds(h * NA_HD, NA_HD)
        q = _head_rmsnorm(q_ref[:, cols].astype(f32), qg_ref[...])
        k = _head_rmsnorm(k_ref[:, cols].astype(f32), kg_ref[...])
        ko_ref[0, 0, :, cols] = k
        s = _dot_nt(q.astype(bf16), k.astype(bf16)) * scale
        p = jnp.exp(s - jnp.max(s, axis=-1, keepdims=True))
        o = _dot(p.astype(bf16), v_ref[:, cols]) / jnp.sum(p, axis=-1, keepdims=True)
        y_ref[:, cols] = o.astype(bf16)


def context_attention(proj, bufs, row0, qg, kg, layer):
    rb0 = row0 // SEQ
    spec = lambda off: pl.BlockSpec((SEQ, CTX_W), lambda s, hb: (rb0 + s, off // CTX_W + hb))
    gspec = pl.BlockSpec((1, NA_HD), lambda s, hb: (0, 0))
    kv_spec = pl.BlockSpec((1, 1, SEQ, CTX_W), lambda s, hb: (s, layer, 0, hb))
    kv_shape = jax.ShapeDtypeStruct((BATCH, DEPTH, SEQ, NA_W), f32)
    return _inplace_call(
        _ctx_attn_kernel, bufs, (BATCH, NA_HEADS // CTX_HB),
        [spec(OFF_NA_Q), spec(OFF_NA_K), spec(OFF_NA_V), gspec, gspec], [proj, proj, proj, qg, kg],
        [pl.BlockSpec((SEQ, CTX_W), lambda s, hb: (rb0 + s, hb)), kv_spec, kv_spec],
        [_BRANCH_BUF, kv_shape, kv_shape], [], "context_attention")


NA_ROWS = DEC_SEQ // GRID_W
NA_NK = NA_WR * GRID_W


def _na_row_start(r):
    return min(max(r - NA_WR // 2, 0), NA_ROWS - NA_WR)


def _rope(x, cos, sin_signed):
    lane = lax.broadcasted_iota(jnp.int32, x.shape, 1)
    quarter = NA_HD // 4
    swapped = jnp.where((lane & (2 * quarter - 1)) < quarter,
                        pltpu.roll(x, NA_HD - quarter, axis=1), pltpu.roll(x, quarter, axis=1))
    return x * cos + swapped * sin_signed


def _na_row_groups():
    groups, r = [], 0
    while r < NA_ROWS:
        n = 1
        while r + n < NA_ROWS and _na_row_start(r + n) == _na_row_start(r):
            n += 1
        groups.append((r, n, _na_row_start(r)))
        r += n
    return groups


def _na_attn_kernel(q_ref, k_ref, v_ref, kc_ref, vc_ref, qg_ref, kg_ref, cos_ref, sin_ref,
                    bias_ref, valid_ref, y_ref, qs_ref, ks_ref, s_ref, p_ref, oc_ref, w_ref):
    scale = NA_HD ** -0.5
    cos = cos_ref[...]
    sin = sin_ref[...]
    qs_ref[...] = _rope(_head_rmsnorm(q_ref[...].astype(f32), qg_ref[...]), cos, sin).astype(bf16)
    ks_ref[...] = _rope(_head_rmsnorm(k_ref[...].astype(f32), kg_ref[...]), cos, sin).astype(bf16)
    groups = _na_row_groups()
    valid = valid_ref[...] > 0.0
    for r0, n, rs in groups:
        rows = pl.ds(r0 * GRID_W, n * GRID_W)
        d0 = r0 - rs
        kw = ks_ref[pl.ds(rs * GRID_W, NA_NK), :]
        s = (_dot_nt(qs_ref[rows, :], kw) * scale).reshape(n, GRID_W, NA_NK) + bias_ref[0, 0, d0:d0 + n]
        s_ref[rows, :] = jnp.where(valid[None], s, -1e30).reshape(n * GRID_W, NA_NK)
    s_ctx = _dot_nt(qs_ref[...], kc_ref[0, 0].astype(bf16)) * scale
    s_loc = s_ref[...]
    m = jnp.maximum(jnp.max(s_loc, axis=-1, keepdims=True), jnp.max(s_ctx, axis=-1, keepdims=True))
    p_loc = jnp.exp(s_loc - m)
    p_ctx = jnp.exp(s_ctx - m)
    w_ref[...] = 1.0 / (jnp.sum(p_loc, axis=-1, keepdims=True) + jnp.sum(p_ctx, axis=-1, keepdims=True))
    p_ref[...] = p_loc.astype(bf16)
    oc_ref[...] = _dot(p_ctx.astype(bf16), vc_ref[0, 0].astype(bf16))
    for r0, n, rs in groups:
        rows = pl.ds(r0 * GRID_W, n * GRID_W)
        vw = v_ref[pl.ds(rs * GRID_W, NA_NK), :]
        y_ref[rows, :] = ((_dot(p_ref[rows, :], vw) + oc_ref[rows, :]) * w_ref[rows, :]).astype(bf16)


def neighbourhood_attention(proj, cache_k, cache_v, layer, qg, kg, cos, sin, bias, valid):
    spec = lambda off: pl.BlockSpec((DEC_SEQ, NA_HD), lambda b, h: (b, off // NA_HD + h))
    cspec = pl.BlockSpec((1, 1, PAST_LEN, NA_HD), lambda b, h: (b, layer, 0, h))
    gspec = pl.BlockSpec((1, NA_HD), lambda b, h: (0, 0))
    tspec = pl.BlockSpec((DEC_SEQ, NA_HD), lambda b, h: (0, 0))
    return pl.pallas_call(
        _na_attn_kernel,
        grid=(DEC_BATCH, NA_HEADS),
        in_specs=[spec(OFF_NA_Q), spec(OFF_NA_K), spec(OFF_NA_V), cspec, cspec, gspec, gspec,
                  tspec, tspec,
                  pl.BlockSpec((1, 1, NA_WR, GRID_W, NA_NK), lambda b, h: (layer, h, 0, 0, 0)),
                  pl.BlockSpec((GRID_W, NA_NK), lambda b, h: (0, 0))],
        out_specs=pl.BlockSpec((DEC_SEQ, NA_HD), lambda b, h: (b, h)),
        out_shape=_BRANCH_BUF,
        scratch_shapes=[pltpu.VMEM((DEC_SEQ, NA_HD), bf16), pltpu.VMEM((DEC_SEQ, NA_HD), bf16),
                        pltpu.VMEM((DEC_SEQ, NA_NK), f32), pltpu.VMEM((DEC_SEQ, NA_NK), bf16),
                        pltpu.VMEM((DEC_SEQ, NA_HD), f32), pltpu.VMEM((DEC_SEQ, 1), f32)],
        compiler_params=_params(2), name="neighbourhood_attention",
    )(proj, proj, proj, cache_k, cache_v, qg, kg, cos, sin, bias, valid)


TAIL_TN = 512
N_TAIL_TILES = (D_IN_PAD - HEAD_COLS) // TAIL_TN
N_QKV_TILES = 3 * NA_W // TAIL_TN
DT_W = 2 * SSD_HEADS


def _tail_kernel(a_ref, b_ref, o_ref):
    j = pl.program_id(1)
    lane = lax.broadcasted_iota(jnp.int32, (D_MODEL, TAIL_TN), 1)

    @pl.when(j == 0)
    def _():
        o_ref[0] = a_ref[0]

    @pl.when((j >= 1) & (j <= N_QKV_TILES))
    def _():
        main = pltpu.roll(a_ref[0].astype(f32), TAIL_TN - DT_W, axis=1)
        nxt = pltpu.roll(b_ref[0].astype(f32), V7X_LANES - DT_W, axis=1)
        last_lane = lax.broadcasted_iota(jnp.int32, (D_MODEL, V7X_LANES), 1)
        last = jnp.where(last_lane < V7X_LANES - DT_W, main[:, TAIL_TN - V7X_LANES:], nxt)
        o_ref[0] = jnp.concatenate([main[:, :TAIL_TN - V7X_LANES], last], axis=1).astype(bf16)

    @pl.when(j == N_TAIL_TILES - 1)
    def _():
        a = a_ref[0].astype(f32)
        out = jnp.zeros((D_MODEL, TAIL_TN), f32)
        for gidx in range(SSD_GROUPS):
            base = gidx * V7X_LANES
            for src0, dst0 in ((gidx * SSD_HPG, base), (SSD_HEADS + gidx * SSD_HPG, base + SSD_HPG)):
                shift = (dst0 - src0) % TAIL_TN
                moved = a if shift == 0 else pltpu.roll(a, shift, axis=1)
                out = jnp.where((lane >= dst0) & (lane < dst0 + SSD_HPG), moved, out)
        o_ref[0] = out.astype(bf16)


def _split_w_in(w_in):
    w_in = w_in.astype(bf16)
    dt0 = IN_OFFSETS[7]
    a_tile0, dt_tile, b_tile0 = HEAD_COLS // TAIL_TN, dt0 // TAIL_TN, dt0 // V7X_LANES
    b_per_a = TAIL_TN // V7X_LANES
    last_b = (D_IN - 1) // V7X_LANES
    tail = pl.pallas_call(
        _tail_kernel, grid=(DEPTH, N_TAIL_TILES),
        in_specs=[pl.BlockSpec((1, D_MODEL, TAIL_TN),
                               lambda l, j: (l, 0, jnp.where(j == N_TAIL_TILES - 1, dt_tile, a_tile0 + j))),
                  pl.BlockSpec((1, D_MODEL, V7X_LANES),
                               lambda l, j: (l, 0, jnp.minimum(b_tile0 + b_per_a * j, last_b)))],
        out_specs=pl.BlockSpec((1, D_MODEL, TAIL_TN), lambda l, j: (l, 0, j)),
        out_shape=jax.ShapeDtypeStruct((DEPTH, D_MODEL, D_IN_PAD - HEAD_COLS), bf16),
        compiler_params=_params(2), name="build_w_in_tail",
    )(w_in, w_in)
    return w_in, tail


def _group_lanes(v):
    rows = []
    for gidx in range(SSD_GROUPS):
        sl = slice(gidx * SSD_HPG, (gidx + 1) * SSD_HPG)
        rows.append(jnp.concatenate([v[0, sl], v[1, sl], jnp.zeros((V7X_LANES - 2 * SSD_HPG,), f32)]))
    return jnp.stack(rows)


def _rope_tables():
    t = np.arange(DEC_SEQ)
    quarter = NA_HD // 4
    inv = ROPE_BASE ** (-np.arange(quarter, dtype=np.float32) / quarter)
    ang_r = (t // GRID_W).astype(np.float32)[:, None] * inv
    ang_c = (t % GRID_W).astype(np.float32)[:, None] * inv
    cos = np.concatenate([np.cos(ang_r), np.cos(ang_r), np.cos(ang_c), np.cos(ang_c)], axis=1)
    sin = np.concatenate([-np.sin(ang_r), np.sin(ang_r), -np.sin(ang_c), np.sin(ang_c)], axis=1)
    return jnp.asarray(cos, f32), jnp.asarray(sin, f32)


def _na_tables(rpb):
    cq = np.arange(GRID_W)
    kc = np.tile(np.arange(GRID_W), NA_WR)
    col_start = np.clip(cq - NA_WC // 2, 0, GRID_W - NA_WC)
    valid = (kc[None, :] >= col_start[:, None]) & (kc[None, :] < col_start[:, None] + NA_WC)
    col_off = np.clip(cq[None, :] - cq[:, None], 1 - NA_WC, NA_WC - 1) + NA_WC - 1
    onehot = (col_off[None, :, :] == np.arange(2 * NA_WC - 1)[:, None, None]).astype(np.float32)
    dd, jj = np.arange(NA_WR)[:, None], np.arange(NA_WR)[None, :]
    row_sel = ((jj - dd + NA_WR - 1)[:, :, None] == np.arange(2 * NA_WR - 1)[None, None, :]).astype(np.float32)
    hi = lax.Precision.HIGHEST
    toep = jnp.einsum('lhic,cqk->lhiqk', rpb.astype(f32), jnp.asarray(onehot), precision=hi)
    tables = jnp.einsum('dji,lhiqk->lhdqjk', jnp.asarray(row_sel), toep, precision=hi)
    return tables.reshape(DEPTH, NA_HEADS, NA_WR, GRID_W, NA_NK), jnp.asarray(valid, f32)


def kernel(x_prompt, x_sample, cache_na_k, cache_na_v, state_lru_f, state_lru_b,
           state_ret_f, state_ret_b, state_ssd_f, state_ssd_b, c, c_ctx,
           norm1_g, norm2_g, w_ada, b_ada, w_in, w_gate, b_gate, w_branch, w_out,
           lru_conv_w, lru_conv_b, lru_wa, lru_ba, lru_wx, lru_bx, lru_lambda,
           ret_gn_g, ssd_conv_w, ssd_conv_b, ssd_a_log, ssd_dt_bias, ssd_d, ssd_norm_g,
           na_q_g, na_k_g, na_rpb, ffn_w_up, ffn_conv_w, ffn_conv_b, ffn_w_down):
    xs0 = x_sample.reshape(T_SAMPLE, D_MODEL)
    xp0 = x_prompt.reshape(T_PROMPT, D_MODEL)
    n_ptiles = N_TILES - N_SAMPLE_TILES
    cond = jnp.concatenate([c, c_ctx[None, :], jnp.zeros((N_COND_PAD - N_COND, D_MODEL), f32)], axis=0)
    mod_all = ada_modulation(cond, w_ada, b_ada)
    cos, sin = _rope_tables()
    bias, valid = _na_tables(na_rpb)
    hh = jnp.arange(RET_HEADS, dtype=f32)
    ret_la = jnp.stack([jnp.log1p(-jnp.exp2(-5.0 - hh)), jnp.log1p(-jnp.exp2(-5.5 - hh))], axis=1)
    cache_k = cache_na_k.reshape(DEC_BATCH, DEPTH, PAST_LEN, NA_W)
    cache_v = cache_na_v.reshape(DEC_BATCH, DEPTH, PAST_LEN, NA_W)
    lru_init = (state_lru_f.reshape(DEC_BATCH, DEPTH, 1, D_RNN), state_lru_b.reshape(DEC_BATCH, DEPTH, 1, D_RNN))

    w_head_b, w_tail_b = _split_w_in(w_in)
    w_branch_b, w_out_b, w_down_b = w_branch.astype(bf16), w_out.astype(bf16), ffn_w_down.astype(bf16)

    new = {k: None for k in ("k", "v", "lru_f", "lru_b", "ret_f", "ret_b", "ssd_f", "ssd_b")}
    for l in range(DEPTH):
        mod = mod_all[l].reshape(N_COND_PAD, 1, N_MOD * D_MODEL)
        g1 = norm1_g[l][None, :]
        if l == 0:
            proj, xn = in_projection(xs0, 0, 0, N_SAMPLE_TILES, g1, mod, w_head_b, w_tail_b, l)
            proj, xn = in_projection(xp0, 0, N_SAMPLE_TILES, n_ptiles, g1, mod, w_head_b, w_tail_b, l, (proj, xn))
        else:
            proj, xn = in_projection(x, 0, 0, N_TILES, g1, mod, w_head_b, w_tail_b, l)

        w4 = jnp.concatenate([lru_wa[l, 0], lru_wx[l, 0], lru_wa[l, 1], lru_wx[l, 1]], axis=-1).astype(bf16)
        b4 = jnp.concatenate([lru_ba[l, 0].reshape(LRU_BLOCKS, 1, LRU_BS), lru_bx[l, 0].reshape(LRU_BLOCKS, 1, LRU_BS),
                              lru_ba[l, 1].reshape(LRU_BLOCKS, 1, LRU_BS), lru_bx[l, 1].reshape(LRU_BLOCKS, 1, LRU_BS)],
                             axis=-1)
        lru_args = (lru_conv_w[l], lru_conv_b[l][None, :], w4, b4, lru_lambda[l])
        y_lru = lru_branch(proj, (None,), 0, DEC_BATCH, DEC_SEQ, *lru_args, lru_init, l)
        y_lru, new["lru_f"], new["lru_b"] = lru_branch(
            proj, (y_lru, new["lru_f"], new["lru_b"]), T_SAMPLE, BATCH, SEQ, *lru_args, None, l)

        gn = ret_gn_g[l][None, :]
        y_ret = retention_branch(proj, (None,), 0, DEC_BATCH, DEC_SEQ, ret_la, gn, (state_ret_f, state_ret_b), l)
        y_ret, new["ret_f"], new["ret_b"] = retention_branch(
            proj, (y_ret, new["ret_f"], new["ret_b"]), T_SAMPLE, BATCH, SEQ, ret_la, gn, None, l)

        prm = jnp.stack([_group_lanes(ssd_dt_bias[l]), _group_lanes(ssd_a_log[l])], axis=1)
        prm = jnp.concatenate([prm, jnp.zeros((SSD_GROUPS, V7X_SUBLANES - 2, V7X_LANES), f32)], axis=1)
        dvec = jnp.repeat(ssd_d[l], SSD_P)[None, :]
        ssd_args = (ssd_conv_w[l], ssd_conv_b[l][None, :], prm, dvec, ssd_norm_g[l][None, :])
        y_ssd = ssd_branch(proj, (None,), 0, DEC_BATCH, DEC_SEQ, *ssd_args, (state_ssd_f, state_ssd_b), l)
        y_ssd, new["ssd_f"], new["ssd_b"] = ssd_branch(
            proj, (y_ssd, new["ssd_f"], new["ssd_b"]), T_SAMPLE, BATCH, SEQ, *ssd_args, None, l)

        qg = na_q_g[l][None, :]
        kg = na_k_g[l][None, :]
        y_na = neighbourhood_attention(proj, cache_k, cache_v, l, qg, kg, cos, sin, bias, valid)
        y_na, new["k"], new["v"] = context_attention(proj, (y_na, new["k"], new["v"]), T_SAMPLE, qg, kg, l)

        merged = merge_branches(xn, (y_lru, y_ret, y_ssd, y_na), w_gate, b_gate[l][None, :], w_branch_b, l)
        if l == 0:
            x = residual_projection(merged, w_out_b, l, xs0, mod, 2, 0, N_SAMPLE_TILES)
            x = residual_projection(merged, w_out_b, l, xp0, mod, 2, N_SAMPLE_TILES, n_ptiles, x_tile0=0, buf=x)
        else:
            x = residual_projection(merged, w_out_b, l, x, mod, 2)
        hmid = ffn_up(x, norm2_g[l][None, :], mod, ffn_w_up, l, ffn_conv_w[l], ffn_conv_b[l][None, :])
        if l < DEPTH - 1:
            x = residual_projection(hmid, w_down_b, l, x, mod, 5)
        else:
            y_sample = residual_projection(hmid, w_down_b, l, x, mod, 5, 0, N_SAMPLE_TILES, out_rows=T_SAMPLE)
            y_prompt = residual_projection(hmid, w_down_b, l, x, mod, 5, N_SAMPLE_TILES, n_ptiles,
                                           out_tile0=0, out_rows=T_PROMPT)

    kv_shape = (BATCH, DEPTH, SEQ, NA_HEADS, NA_HD)
    return (y_prompt.reshape(BATCH, SEQ, D_MODEL), y_sample.reshape(DEC_BATCH, DEC_SEQ, D_MODEL),
            new["k"].reshape(kv_shape), new["v"].reshape(kv_shape),
            new["lru_f"].reshape(BATCH, DEPTH, D_RNN), new["lru_b"].reshape(BATCH, DEPTH, D_RNN),
            new["ret_f"], new["ret_b"], new["ssd_f"], new["ssd_b"])
```

```python
import functools
import math

import jax
import jax.numpy as jnp
import numpy as np
from jax import lax
from jax.experimental import pallas as pl
from jax.experimental.pallas import tpu as pltpu

D_MODEL = 2048
BATCH = 16
SEQ = 256
DEPTH = 2
DEC_BATCH = 8
DEC_SEQ = 1024
PAST_LEN = 256
GRID_W = 64
EPS = 1e-6
N_BRANCH = 4
BRANCH_W = 1024
N_MOD = 6
D_RNN = 1024
LRU_BLOCKS = 8
LRU_BS = D_RNN // LRU_BLOCKS
LRU_CONV = 4
LRU_C = 8.0
RET_HEADS = 4
RET_DK = 128
RET_DV = 256
SSD_HEADS = 16
SSD_P = 64
SSD_N = 128
SSD_GROUPS = 2
SSD_CONV = 4
D_SSD = SSD_HEADS * SSD_P
SSD_CONV_CH = D_SSD + 2 * SSD_GROUPS * SSD_N
NA_HEADS = 8
NA_HD = 128
NA_W = NA_HEADS * NA_HD
NA_WR = 8
NA_WC = 16
ROPE_BASE = 10000.0
D_FF = 5632
FFN_CONV = 3
IN_SIZES = (D_RNN, D_RNN,
            RET_HEADS * RET_DK, RET_HEADS * RET_DK, RET_HEADS * RET_DV, RET_HEADS * RET_DV,
            D_SSD, SSD_CONV_CH, 2 * SSD_HEADS,
            NA_W, NA_W, NA_W)
D_IN = sum(IN_SIZES)
IN_OFFSETS = tuple(int(s) for s in np.cumsum(IN_SIZES)[:-1])

V7X_LANES = 128
V7X_SUBLANES = 8
V7X_VMEM_LIMIT_BYTES = 56 * 1024 * 1024

TM = 1024
T_SAMPLE = DEC_BATCH * DEC_SEQ
T_PROMPT = BATCH * SEQ
T_ALL = T_SAMPLE + T_PROMPT
N_SAMPLE_TILES = T_SAMPLE // TM
N_TILES = T_ALL // TM
N_COND = DEC_BATCH + 1
N_COND_PAD = 16

SSD_HPG = SSD_HEADS // SSD_GROUPS
OFF_LRU_X = 0
OFF_LRU_G = OFF_LRU_X + D_RNN
OFF_RET_Q = OFF_LRU_G + D_RNN
OFF_RET_K = OFF_RET_Q + RET_HEADS * RET_DK
OFF_RET_V = OFF_RET_K + RET_HEADS * RET_DK
OFF_RET_G = OFF_RET_V + RET_HEADS * RET_DV
OFF_SSD_Z = OFF_RET_G + RET_HEADS * RET_DV
OFF_SSD_XBC = OFF_SSD_Z + D_SSD
OFF_NA_Q = OFF_SSD_XBC + SSD_CONV_CH
OFF_NA_K = OFF_NA_Q + NA_W
OFF_NA_V = OFF_NA_K + NA_W
OFF_SSD_DT = OFF_NA_V + NA_W
PROJ_TN = 1024
D_IN_PAD = -(-(OFF_SSD_DT + SSD_GROUPS * V7X_LANES) // PROJ_TN) * PROJ_TN
N_HEAD_TILES = OFF_NA_Q // PROJ_TN
HEAD_COLS = N_HEAD_TILES * PROJ_TN

RET_CHUNK = 256
SSD_CHUNK = 256
SSD_TB = 128

f32 = jnp.float32
bf16 = jnp.bfloat16

_ARB = "arbitrary"


def _params(n_axes):
    return pltpu.CompilerParams(dimension_semantics=(_ARB,) * n_axes,
                                vmem_limit_bytes=V7X_VMEM_LIMIT_BYTES)


def _mod_spec(k, tile0=0):
    return pl.BlockSpec((1, 1, D_MODEL), lambda i, j: (jnp.minimum(tile0 + i, N_SAMPLE_TILES), 0, k))


def _dot(a, b):
    return jnp.dot(a, b, preferred_element_type=f32)


def _dot_nt(a, b):
    return lax.dot_general(a, b, (((1,), (1,)), ((), ())), preferred_element_type=f32)


def _sigmoid(x):
    return 0.5 * jnp.tanh(0.5 * x) + 0.5


def _silu(x):
    return x * _sigmoid(x)


def _gelu_tanh(x):
    return 0.5 * x * (1.0 + jnp.tanh(math.sqrt(2.0 / math.pi) * (x + 0.044715 * (x * x * x))))


def _shift_rows(v, k):
    n = v.shape[0]
    row = lax.broadcasted_iota(jnp.int32, (n, 1), 0)
    if k == 1:
        return jnp.where(row == 0, 0.0, pltpu.roll(v, 1, axis=0))
    return jnp.where(row == n - 1, 0.0, pltpu.roll(v, n - 1, axis=0))


def _dwconv4(x, cw, cb):
    acc = _shift_rows(x * cw[0:1, :], 1) + x * cw[1:2, :]
    return cb + x * cw[2:3, :] + _shift_rows(acc, 1) + _shift_rows(x * cw[3:4, :], -1)


def _softplus(x):
    return jnp.maximum(x, 0.0) + jnp.log1p(jnp.exp(-jnp.abs(x)))


def _drop_refs(body, n, *refs):
    body(*refs[n:])


def _inplace_call(body, bufs, grid, in_specs, args, out_specs, out_shape, scratch, name):
    held = [(k, b) for k, b in enumerate(bufs) if b is not None]
    if held:
        body = functools.partial(_drop_refs, body, len(held))
        in_specs = [pl.BlockSpec(memory_space=pl.ANY)] * len(held) + list(in_specs)
        args = [b for _, b in held] + list(args)
    aliases = {pos: k for pos, (k, _) in enumerate(held)}
    return pl.pallas_call(
        body, grid=grid, in_specs=in_specs, out_specs=out_specs, out_shape=out_shape,
        scratch_shapes=scratch, input_output_aliases=aliases,
        compiler_params=_params(len(grid)), name=name,
    )(*args)


_BRANCH_BUF = jax.ShapeDtypeStruct((T_ALL, BRANCH_W), bf16)


ADA_TN = 1024


def _ada_kernel(c_ref, w_ref, b_ref, o_ref):
    c = _silu(c_ref[...]).astype(bf16)
    o_ref[0] = _dot(c, w_ref[0].astype(bf16)) + b_ref[0]


def ada_modulation(cond, w_ada, b_ada):
    n = N_MOD * D_MODEL
    return pl.pallas_call(
        _ada_kernel,
        grid=(DEPTH, n // ADA_TN),
        in_specs=[pl.BlockSpec((N_COND_PAD, D_MODEL), lambda l, j: (0, 0)),
                  pl.BlockSpec((1, D_MODEL, ADA_TN), lambda l, j: (l, 0, j)),
                  pl.BlockSpec((1, 1, ADA_TN), lambda l, j: (l, 0, j))],
        out_specs=pl.BlockSpec((1, N_COND_PAD, ADA_TN), lambda l, j: (l, 0, j)),
        out_shape=jax.ShapeDtypeStruct((DEPTH, N_COND_PAD, n), f32),
        compiler_params=_params(2),
        name="ada_modulation",
    )(cond, w_ada, b_ada.reshape(DEPTH, 1, n))


NORM_ROWS = 128
NORM_SPLIT = 2


def _modulated_norm_parts(x_ref, g_ref, sc_ref, sh_ref, xn_ref):
    gs = g_ref[...] * (1.0 + sc_ref[0])
    sh = sh_ref[0]
    part = TM // NORM_SPLIT
    for p in range(NORM_SPLIT):
        pieces = []
        for r in range(part // NORM_ROWS):
            rows = pl.ds(p * part + r * NORM_ROWS, NORM_ROWS)
            x = x_ref[rows, :]
            y = x * lax.rsqrt(jnp.mean(x * x, axis=-1, keepdims=True) + EPS)
            pieces.append((y * gs + sh).astype(bf16))
        xn = jnp.concatenate(pieces, axis=0)
        xn_ref[pl.ds(p * part, part), :] = xn
        yield xn


def _in_proj_kernel(x_ref, g_ref, sc_ref, sh_ref, wh_ref, wt_ref, o_ref, xn_ref):
    j = pl.program_id(1)

    @pl.when(j == 0)
    def _():
        w = wh_ref[0]
        outs = [_dot(xn, w).astype(bf16) for xn in _modulated_norm_parts(x_ref, g_ref, sc_ref, sh_ref, xn_ref)]
        o_ref[...] = jnp.concatenate(outs, axis=0)

    @pl.when((j > 0) & (j < N_HEAD_TILES))
    def _():
        o_ref[...] = _dot(xn_ref[...], wh_ref[0]).astype(bf16)

    @pl.when(j >= N_HEAD_TILES)
    def _():
        o_ref[...] = _dot(xn_ref[...], wt_ref[0]).astype(bf16)


def in_projection(x, x_tile0, tile0, n_tiles, g, mod, w_head, w_tail, layer, bufs=(None, None)):
    n = D_IN_PAD
    n_tail = w_tail.shape[2] // PROJ_TN
    return _inplace_call(
        _in_proj_kernel, bufs, (n_tiles, n // PROJ_TN),
        [pl.BlockSpec((TM, D_MODEL), lambda i, j: (x_tile0 + i, 0)),
         pl.BlockSpec((1, D_MODEL), lambda i, j: (0, 0)),
         _mod_spec(1, tile0), _mod_spec(0, tile0),
         pl.BlockSpec((1, D_MODEL, PROJ_TN), lambda i, j: (layer, 0, jnp.minimum(j, N_HEAD_TILES - 1))),
         pl.BlockSpec((1, D_MODEL, PROJ_TN),
                      lambda i, j: (layer, 0, jnp.clip(j - N_HEAD_TILES, 0, n_tail - 1)))],
        [x, g, mod, mod, w_head, w_tail],
        [pl.BlockSpec((TM, PROJ_TN), lambda i, j: (tile0 + i, j)),
         pl.BlockSpec((TM, D_MODEL), lambda i, j: (tile0 + i, 0))],
        [jax.ShapeDtypeStruct((T_ALL, n), bf16), jax.ShapeDtypeStruct((T_ALL, D_MODEL), bf16)],
        [], "in_projection")


MERGE_TN = 256


def _merge_kernel(xn_ref, *refs):
    y_refs, wg_refs, bg_refs = refs[0:4], refs[4:8], refs[8:12]
    wb_ref, o_ref = refs[12], refs[13]
    xn = xn_ref[...]
    acc = None
    for n in range(N_BRANCH):
        gate = _sigmoid(_dot(xn, wg_refs[n][0].astype(bf16)) + bg_refs[n][...])
        term = gate * _dot(y_refs[n][...], wb_ref[0, n])
        acc = term if acc is None else acc + term
    o_ref[...] = acc.astype(bf16)


def merge_branches(xn, ys, w_gate, b_gate, w_branch, layer):
    nj = D_MODEL // MERGE_TN
    y_spec = pl.BlockSpec((TM, BRANCH_W), lambda i, j: (i, 0))
    wg_specs = [pl.BlockSpec((1, D_MODEL, MERGE_TN), lambda i, j, n=n: (layer, 0, n * nj + j))
                for n in range(N_BRANCH)]
    bg_specs = [pl.BlockSpec((1, MERGE_TN), lambda i, j, n=n: (0, n * nj + j)) for n in range(N_BRANCH)]
    return pl.pallas_call(
        _merge_kernel,
        grid=(N_TILES, nj),
        in_specs=[pl.BlockSpec((TM, D_MODEL), lambda i, j: (i, 0)),
                  y_spec, y_spec, y_spec, y_spec, *wg_specs, *bg_specs,
                  pl.BlockSpec((1, N_BRANCH, BRANCH_W, MERGE_TN), lambda i, j: (layer, 0, 0, j))],
        out_specs=pl.BlockSpec((TM, MERGE_TN), lambda i, j: (i, j)),
        out_shape=jax.ShapeDtypeStruct((T_ALL, D_MODEL), bf16),
        compiler_params=_params(2),
        name="merge_branches",
    )(xn, *ys, *([w_gate] * N_BRANCH), *([b_gate] * N_BRANCH), w_branch)


RES_TN = 512
RES_TN_SHORT_K = 1024


def _residual_kernel(a_ref, w_ref, x_ref, gv_ref, o_ref):
    o_ref[...] = x_ref[...] + gv_ref[0] * _dot(a_ref[...], w_ref[0])


def residual_projection(a, w, layer, x, mod, k_mod, tile0=0, n_tiles=N_TILES, x_tile0=None, out_tile0=None,
                        out_rows=T_ALL, buf=None):
    kdim = a.shape[1]
    tn = RES_TN_SHORT_K if kdim == D_MODEL else RES_TN
    x_tile0 = tile0 if x_tile0 is None else x_tile0
    out_tile0 = tile0 if out_tile0 is None else out_tile0
    return _inplace_call(
        _residual_kernel, (buf,), (n_tiles, D_MODEL // tn),
        [pl.BlockSpec((TM, kdim), lambda i, j: (tile0 + i, 0)),
         pl.BlockSpec((1, kdim, tn), lambda i, j: (layer, 0, j)),
         pl.BlockSpec((TM, tn), lambda i, j: (x_tile0 + i, j)),
         pl.BlockSpec((1, 1, tn),
                      lambda i, j: (jnp.minimum(tile0 + i, N_SAMPLE_TILES), 0, k_mod * (D_MODEL // tn) + j))],
        [a, w, x, mod],
        pl.BlockSpec((TM, tn), lambda i, j: (out_tile0 + i, j)),
        jax.ShapeDtypeStruct((out_rows, D_MODEL), f32), [], "residual_projection")


FFN_TN = 512


def _ffn_up_kernel(x_ref, g_ref, sc_ref, sh_ref, wa_ref, wv_ref, cw_ref, cb_ref, o_ref, xn_ref):
    i = pl.program_id(0)

    def epilogue(a, v):
        seq_len = jnp.where(i < N_SAMPLE_TILES, DEC_SEQ, SEQ)
        pos = lax.broadcasted_iota(jnp.int32, (TM, 1), 0) & (seq_len - 1)
        prev = jnp.where(pos == 0, 0.0, pltpu.roll(a, 1, axis=0))
        nxt = jnp.where(pos == seq_len - 1, 0.0, pltpu.roll(a, TM - 1, axis=0))
        cw = cw_ref[...]
        conv = cb_ref[...] + prev * cw[0:1, :] + a * cw[1:2, :] + nxt * cw[2:3, :]
        o_ref[...] = (_gelu_tanh(conv) * v).astype(bf16)

    @pl.when(pl.program_id(1) == 0)
    def _():
        wa = wa_ref[0].astype(bf16)
        wv = wv_ref[0].astype(bf16)
        parts = [(_dot(xn, wa), _dot(xn, wv)) for xn in _modulated_norm_parts(x_ref, g_ref, sc_ref, sh_ref, xn_ref)]
        epilogue(jnp.concatenate([p[0] for p in parts], axis=0), jnp.concatenate([p[1] for p in parts], axis=0))

    @pl.when(pl.program_id(1) > 0)
    def _():
        xn = xn_ref[...]
        epilogue(_dot(xn, wa_ref[0].astype(bf16)), _dot(xn, wv_ref[0].astype(bf16)))


def ffn_up(x, g, mod, w_up, layer, conv_w, conv_b):
    nj = D_FF // FFN_TN
    return pl.pallas_call(
        _ffn_up_kernel,
        grid=(N_TILES, nj),
        in_specs=[pl.BlockSpec((TM, D_MODEL), lambda i, j: (i, 0)),
                  pl.BlockSpec((1, D_MODEL), lambda i, j: (0, 0)),
                  _mod_spec(4), _mod_spec(3),
                  pl.BlockSpec((1, D_MODEL, FFN_TN), lambda i, j: (layer, 0, j)),
                  pl.BlockSpec((1, D_MODEL, FFN_TN), lambda i, j: (layer, 0, nj + j)),
                  pl.BlockSpec((FFN_CONV, FFN_TN), lambda i, j: (0, j)),
                  pl.BlockSpec((1, FFN_TN), lambda i, j: (0, j))],
        out_specs=pl.BlockSpec((TM, FFN_TN), lambda i, j: (i, j)),
        out_shape=jax.ShapeDtypeStruct((T_ALL, D_FF), bf16),
        scratch_shapes=[pltpu.VMEM((TM, D_MODEL), bf16)],
        compiler_params=_params(2),
        name="ffn_up",
    )(x, g, mod, mod, w_up, w_up, conv_w, conv_b)


LRU_CB_LONG = D_RNN
LRU_CB_SHORT = D_RNN


def _lru_kernel(*refs, seq_len, has_init, cbw):
    if has_init:
        (x_ref, g_ref, cw_ref, cb_ref, w4_ref, b4_ref, lam_ref, h0f_ref, h0b_ref,
         y_ref, af_ref, uf_ref, ab_ref, ub_ref, hf_ref, hb_ref) = refs
    else:
        (x_ref, g_ref, cw_ref, cb_ref, w4_ref, b4_ref, lam_ref,
         y_ref, sf_ref, sb_ref, af_ref, uf_ref, ab_ref, ub_ref, hf_ref, hb_ref) = refs
    L = seq_len
    S = V7X_SUBLANES
    sp = _softplus(-lam_ref[...])
    for n in range(cbw // LRU_BS):
        cols = slice(n * LRU_BS, (n + 1) * LRU_BS)
        xcn = _dwconv4(x_ref[:, cols].astype(f32), cw_ref[:, cols], cb_ref[:, cols])
        z = _dot(xcn.astype(bf16), w4_ref[n]) + b4_ref[n]
        for d, (a_ref, u_ref) in enumerate(((af_ref, uf_ref), (ab_ref, ub_ref))):
            r = _sigmoid(z[:, (2 * d) * LRU_BS:(2 * d + 1) * LRU_BS])
            ig = _sigmoid(z[:, (2 * d + 1) * LRU_BS:(2 * d + 2) * LRU_BS])
            log_a = (-LRU_C) * r * sp[d:d + 1, cols]
            a = jnp.exp(log_a)
            a_ref[:, cols] = a
            gain2 = -jnp.tanh(log_a) * (a * a + 1.0)
            gain = jnp.where(gain2 > 0.0, gain2 * lax.rsqrt(gain2), 0.0)
            u_ref[:, cols] = gain * (ig * xcn)

    if has_init:
        h0f = h0f_ref[0, 0]
        h0b = h0b_ref[0, 0]
    else:
        h0f = jnp.zeros((1, cbw), f32)
        h0b = jnp.zeros((1, cbw), f32)

    def step(i, carry):
        hf, hb = carry
        base_f = pl.multiple_of(i * S, S)
        base_b = pl.multiple_of(L - S - i * S, S)
        for r in range(S):
            tf = pl.ds(base_f + r, 1)
            tb = pl.ds(base_b + (S - 1 - r), 1)
            hf = af_ref[tf, :] * hf + uf_ref[tf, :]
            hb = ab_ref[tb, :] * hb + ub_ref[tb, :]
            hf_ref[tf, :] = hf
            hb_ref[tb, :] = hb
        return hf, hb

    hf, hb = lax.fori_loop(0, L // S, step, (h0f, h0b))
    if not has_init:
        sf_ref[0, 0] = hf
        sb_ref[0, 0] = hb
    for n in range(cbw // LRU_BS):
        cols = slice(n * LRU_BS, (n + 1) * LRU_BS)
        y_ref[:, cols] = ((hf_ref[:, cols] + hb_ref[:, cols]) * _gelu_tanh(g_ref[:, cols].astype(f32))).astype(bf16)


def lru_branch(proj, bufs, row0, n_seq, seq_len, cw, cb, w4, b4, lam, init, layer):
    has_init = init is not None
    cbw = LRU_CB_LONG if seq_len == DEC_SEQ else LRU_CB_SHORT
    ncb = D_RNN // cbw
    nb = cbw // LRU_BS
    rb0 = row0 // seq_len
    in_specs = [pl.BlockSpec((seq_len, cbw), lambda s, c: (rb0 + s, OFF_LRU_X // cbw + c)),
                pl.BlockSpec((seq_len, cbw), lambda s, c: (rb0 + s, OFF_LRU_G // cbw + c)),
                pl.BlockSpec((LRU_CONV, cbw), lambda s, c: (0, c)),
                pl.BlockSpec((1, cbw), lambda s, c: (0, c)),
                pl.BlockSpec((nb, LRU_BS, 4 * LRU_BS), lambda s, c: (c, 0, 0)),
                pl.BlockSpec((nb, 1, 4 * LRU_BS), lambda s, c: (c, 0, 0)),
                pl.BlockSpec((2, cbw), lambda s, c: (0, c))]
    args = [proj, proj, cw, cb, w4, b4, lam]
    state_spec = pl.BlockSpec((1, 1, 1, cbw), lambda s, c: (s, layer, 0, c))
    y_spec = pl.BlockSpec((seq_len, cbw), lambda s, c: (rb0 + s, c))
    if has_init:
        in_specs += [state_spec, state_spec]
        args += [init[0], init[1]]
        out_specs, out_shape = y_spec, _BRANCH_BUF
    else:
        st = jax.ShapeDtypeStruct((n_seq, DEPTH, 1, D_RNN), f32)
        out_specs, out_shape = [y_spec, state_spec, state_spec], [_BRANCH_BUF, st, st]
    scratch = [pltpu.VMEM((seq_len, cbw), f32) for _ in range(6)]
    return _inplace_call(functools.partial(_lru_kernel, seq_len=seq_len, has_init=has_init, cbw=cbw), bufs,
                         (n_seq, ncb), in_specs, args, out_specs, out_shape, scratch, "lru_branch")


def _ret_kernel(*refs, seq_len, has_init):
    if has_init:
        (la_ref, q_ref, k_ref, v_ref, g_ref, gn_ref, s0f_ref, s0b_ref, y_ref, acc_ref) = refs
    else:
        (la_ref, q_ref, k_ref, v_ref, g_ref, gn_ref, y_ref, sf_ref, sb_ref, acc_ref) = refs
    T = min(RET_CHUNK, seq_len)
    nc = seq_len // T
    tt = lax.broadcasted_iota(jnp.int32, (T, T), 0)
    ss = lax.broadcasted_iota(jnp.int32, (T, T), 1)
    diff = (tt - ss).astype(f32)
    tcol = lax.broadcasted_iota(jnp.int32, (T, 1), 0).astype(f32)
    scale = RET_DK ** -0.5
    for h in range(RET_HEADS):
        la_f = la_ref[h, 0]
        la_b = la_ref[h, 1]
        kcols = pl.ds(h * RET_DK, RET_DK)
        vcols = pl.ds(h * RET_DV, RET_DV)
        dsum = (jnp.where(tt >= ss, jnp.exp(la_f * diff), 0.0)
                + jnp.where(ss >= tt, jnp.exp(-la_b * diff), 0.0))

        def chunk(c):
            rows = pl.ds(c * T, T)
            q = q_ref[rows, kcols].astype(f32)
            ks = k_ref[rows, kcols].astype(f32) * scale
            v = v_ref[rows, vcols].astype(f32)
            return rows, q, ks, v

        s_f = s0f_ref[0, 0, h] if has_init else None
        for c in range(nc):
            rows, q, ks, v = chunk(c)
            scores = _dot_nt(q.astype(bf16), ks.astype(bf16)) * dsum
            y = _dot(scores.astype(bf16), v.astype(bf16))
            if s_f is not None:
                y = y + _dot((q * jnp.exp(la_f * (tcol + 1.0))).astype(bf16), s_f.astype(bf16))
            acc_ref[rows, :] = y
            if c < nc - 1 or not has_init:
                upd = _dot(ks.T.astype(bf16), (v * jnp.exp(la_f * (T - 1.0 - tcol))).astype(bf16))
                s_f = upd if s_f is None else jnp.exp(la_f * T) * s_f + upd
        s_b = s0b_ref[0, 0, h] if has_init else None
        for c in reversed(range(nc)):
            rows, q, ks, v = chunk(c)
            if s_b is not None:
                acc_ref[rows, :] += _dot((q * jnp.exp(la_b * (T - tcol))).astype(bf16), s_b.astype(bf16))
            if c > 0 or not has_init:
                upd = _dot(ks.T.astype(bf16), (v * jnp.exp(la_b * tcol)).astype(bf16))
                s_b = upd if s_b is None else jnp.exp(la_b * T) * s_b + upd
        if not has_init:
            sf_ref[0, 0, h] = s_f
            sb_ref[0, 0, h] = s_b
        y = acc_ref[...]
        mu = jnp.mean(y, axis=-1, keepdims=True)
        yc = y - mu
        var = jnp.mean(yc * yc, axis=-1, keepdims=True)
        y = yc * lax.rsqrt(var + EPS) * gn_ref[:, vcols]
        y_ref[:, vcols] = (y * _silu(g_ref[:, vcols].astype(f32))).astype(bf16)


def retention_branch(proj, bufs, row0, n_seq, seq_len, la, gn, init, layer):
    has_init = init is not None
    rb0 = row0 // seq_len
    qk_w, vg_w = RET_HEADS * RET_DK, RET_HEADS * RET_DV
    in_specs = [pl.BlockSpec(memory_space=pltpu.SMEM),
                pl.BlockSpec((seq_len, qk_w), lambda s: (rb0 + s, OFF_RET_Q // qk_w)),
                pl.BlockSpec((seq_len, qk_w), lambda s: (rb0 + s, OFF_RET_K // qk_w)),
                pl.BlockSpec((seq_len, vg_w), lambda s: (rb0 + s, OFF_RET_V // vg_w)),
                pl.BlockSpec((seq_len, vg_w), lambda s: (rb0 + s, OFF_RET_G // vg_w)),
                pl.BlockSpec((1, vg_w), lambda s: (0, 0))]
    args = [la, proj, proj, proj, proj, gn]
    state_spec = pl.BlockSpec((1, 1, RET_HEADS, RET_DK, RET_DV), lambda s: (s, layer, 0, 0, 0))
    y_spec = pl.BlockSpec((seq_len, vg_w), lambda s: (rb0 + s, 0))
    if has_init:
        in_specs += [state_spec, state_spec]
        args += [init[0], init[1]]
        out_specs, out_shape = y_spec, _BRANCH_BUF
    else:
        st = jax.ShapeDtypeStruct((n_seq, DEPTH, RET_HEADS, RET_DK, RET_DV), f32)
        out_specs, out_shape = [y_spec, state_spec, state_spec], [_BRANCH_BUF, st, st]
    return _inplace_call(functools.partial(_ret_kernel, seq_len=seq_len, has_init=has_init), bufs,
                         (n_seq,), in_specs, args, out_specs, out_shape,
                         [pltpu.VMEM((seq_len, RET_DV), f32)], "retention_branch")


SSD_GW = SSD_HPG * SSD_P


def _split3(x):
    h1 = x.astype(bf16)
    r1 = x - h1.astype(f32)
    h2 = r1.astype(bf16)
    h3 = (r1 - h2.astype(f32)).astype(bf16)
    return h1, h2, h3


def _dot_exact_rhs(m, x):
    h1, h2, h3 = _split3(x)
    return _dot(m, h1) + _dot(m, h2) + _dot(m, h3)


def _dot_exact_lhs(x, m):
    h1, h2, h3 = _split3(x)
    return _dot(h1, m) + _dot(h2, m) + _dot(h3, m)


def _heads_to_lanes(s_ref, st_ref):
    for hh in range(SSD_HPG):
        st_ref[:, pl.ds(hh * SSD_P, SSD_P)] = s_ref[0, 0, hh]
    return st_ref[...]


def _ssd_kernel(*refs, seq_len, has_init):
    if has_init:
        (z_ref, x_ref, b_ref, c_ref, dt_ref, cwx_ref, cbx_ref, cwb_ref, cbb_ref, cwc_ref, cbc_ref,
         prm_ref, dvec_ref, ng_ref, s0f_ref, s0b_ref,
         y_ref, xs_ref, bs_ref, cs_ref, acc_ref, rb_ref, dts_ref, yn_ref, st_ref) = refs
    else:
        (z_ref, x_ref, b_ref, c_ref, dt_ref, cwx_ref, cbx_ref, cwb_ref, cbb_ref, cwc_ref, cbc_ref,
         prm_ref, dvec_ref, ng_ref,
         y_ref, sf_ref, sb_ref, xs_ref, bs_ref, cs_ref, acc_ref, rb_ref, dts_ref, yn_ref, st_ref) = refs
    g = pl.program_id(1)
    L = seq_len
    T = min(SSD_CHUNK, L)
    nc = L // T
    H = SSD_HPG
    xs_ref[...] = _silu(_dwconv4(x_ref[...].astype(f32), cwx_ref[...], cbx_ref[...]))
    bs_ref[...] = _silu(_dwconv4(b_ref[...].astype(f32), cwb_ref[...], cbb_ref[...]))
    cs_ref[...] = _silu(_dwconv4(c_ref[...].astype(f32), cwc_ref[...], cbc_ref[...]))
    prm = prm_ref[0]
    a_neg = -jnp.exp(prm[1:2, :])
    tt = lax.broadcasted_iota(jnp.int32, (T, T), 0)
    ss = lax.broadcasted_iota(jnp.int32, (T, T), 1)
    lower = tt >= ss
    upper = ss >= tt
    tri_l = jnp.where(lower, 1.0, 0.0).astype(bf16)
    tri_u = jnp.where(upper, 1.0, 0.0).astype(bf16)
    lower_b = lower[:SSD_TB, :SSD_TB]
    upper_b = upper[:SSD_TB, :SSD_TB]
    er = lax.broadcasted_iota(jnp.int32, (V7X_LANES, SSD_GW), 0)
    ec = lax.broadcasted_iota(jnp.int32, (V7X_LANES, SSD_GW), 1) // SSD_P
    exp_f = jnp.where(er == ec, 1.0, 0.0).astype(bf16)
    exp_b = jnp.where(er == ec + H, 1.0, 0.0).astype(bf16)

    def expand(w, e):
        hi = w.astype(bf16)
        lo = (w - hi.astype(f32)).astype(bf16)
        return _dot(hi, e) + _dot(lo, e)

    s_f = _heads_to_lanes(s0f_ref, st_ref) if has_init else None
    for c in range(nc):
        rows = pl.ds(c * T, T)
        dt = _softplus(dt_ref[rows, :].astype(f32) + prm[0:1, :])
        dts_ref[rows, :] = dt
        da = dt * a_neg
        cum = _dot_exact_rhs(tri_l, da)
        rsum = (cum[T - 1:T, :] - cum) + da
        rb_ref[rows, :] = rsum
        da_t = da.T
        cum_t = _dot_exact_lhs(da_t, tri_u)
        rsum_t = (cum_t[:, T - 1:T] - cum_t) + da_t
        dt_t = dt.T
        bmat = bs_ref[rows, :]
        cmat = cs_ref[rows, :]
        xmat = xs_ref[rows, :]
        gmat = _dot_nt(cmat.astype(bf16), bmat.astype(bf16))
        for hh in range(H):
            cf, cf_t, dtf_t = cum[:, hh:hh + 1], cum_t[hh:hh + 1, :], dt_t[hh:hh + 1, :]
            rb, rb_t, dtb_t = rsum[:, H + hh:H + hh + 1], rsum_t[H + hh:H + hh + 1, :], dt_t[H + hh:H + hh + 1, :]
            block_rows = []
            for bi in range(T // SSD_TB):
                rr = slice(bi * SSD_TB, (bi + 1) * SSD_TB)
                blocks = []
                for bj in range(T // SSD_TB):
                    cc = slice(bj * SSD_TB, (bj + 1) * SSD_TB)
                    if bi > bj:
                        blk = jnp.exp(cf[rr] - cf_t[:, cc]) * dtf_t[:, cc]
                    elif bi < bj:
                        blk = jnp.exp(rb[rr] - rb_t[:, cc]) * dtb_t[:, cc]
                    else:
                        blk = (jnp.where(lower_b, jnp.exp(cf[rr] - cf_t[:, cc]), 0.0) * dtf_t[:, cc]
                               + jnp.where(upper_b, jnp.exp(rb[rr] - rb_t[:, cc]), 0.0) * dtb_t[:, cc])
                    blocks.append(gmat[rr, cc] * blk)
                block_rows.append(jnp.concatenate(blocks, axis=1))
            m = jnp.concatenate(block_rows, axis=0)
            xh = xmat[:, hh * SSD_P:(hh + 1) * SSD_P]
            acc_ref[rows, pl.ds(hh * SSD_P, SSD_P)] = _dot(m.astype(bf16), xh.astype(bf16))
        ecum = jnp.exp(cum)
        if s_f is not None:
            acc_ref[rows, :] += _dot(cmat.astype(bf16), s_f.astype(bf16)) * expand(ecum, exp_f)
        if c < nc - 1 or not has_init:
            tail = jnp.exp(cum[T - 1:T, :] - cum) * dt
            xw = (xmat * expand(tail, exp_f)).astype(bf16)
            upd = _dot(bmat.T.astype(bf16), xw)
            if s_f is None:
                s_f = upd
            else:
                s_f = s_f * expand(jnp.broadcast_to(ecum[T - 1:T, :], (V7X_SUBLANES, V7X_LANES)),
                                   exp_f)[0:1, :] + upd
    s_b = _heads_to_lanes(s0b_ref, st_ref) if has_init else None
    for c in reversed(range(nc)):
        rows = pl.ds(c * T, T)
        dt = dts_ref[rows, :]
        rsum = rb_ref[rows, :]
        ers = jnp.exp(rsum)
        bmat = bs_ref[rows, :]
        cmat = cs_ref[rows, :]
        xmat = xs_ref[rows, :]
        if s_b is not None:
            acc_ref[rows, :] += _dot(cmat.astype(bf16), s_b.astype(bf16)) * expand(ers, exp_b)
        if c > 0 or not has_init:
            tail = jnp.exp(rsum[0:1, :] - rsum) * dt
            xw = (xmat * expand(tail, exp_b)).astype(bf16)
            upd = _dot(bmat.T.astype(bf16), xw)
            if s_b is None:
                s_b = upd
            else:
                s_b = s_b * expand(jnp.broadcast_to(ers[0:1, :], (V7X_SUBLANES, V7X_LANES)),
                                   exp_b)[0:1, :] + upd
    if not has_init:
        for hh in range(H):
            sf_ref[0, 0, hh] = s_f[:, hh * SSD_P:(hh + 1) * SSD_P]
            sb_ref[0, 0, hh] = s_b[:, hh * SSD_P:(hh + 1) * SSD_P]
    yg = (acc_ref[...] + xs_ref[...] * dvec_ref[...]) * _silu(z_ref[...].astype(f32))
    yn_ref[g] = yg

    @pl.when(g == SSD_GROUPS - 1)
    def _():
        ssq = None
        for gg in range(SSD_GROUPS):
            y = yn_ref[gg]
            s = jnp.sum(y * y, axis=-1, keepdims=True)
            ssq = s if ssq is None else ssq + s
        inv = lax.rsqrt(ssq * (1.0 / D_SSD) + EPS)
        for gg in range(SSD_GROUPS):
            cols = pl.ds(gg * SSD_GW, SSD_GW)
            y_ref[:, cols] = (yn_ref[gg] * inv * ng_ref[:, cols]).astype(bf16)


def ssd_branch(proj, bufs, row0, n_seq, seq_len, conv_w, conv_b, prm, dvec, ng, init, layer):
    has_init = init is not None
    rb0 = row0 // seq_len
    xoff = OFF_SSD_XBC
    boff = OFF_SSD_XBC + D_SSD
    coff = boff + SSD_GROUPS * SSD_N
    in_specs = [pl.BlockSpec((seq_len, SSD_GW), lambda s, g: (rb0 + s, OFF_SSD_Z // SSD_GW + g)),
                pl.BlockSpec((seq_len, SSD_GW), lambda s, g: (rb0 + s, xoff // SSD_GW + g)),
                pl.BlockSpec((seq_len, SSD_N), lambda s, g: (rb0 + s, boff // SSD_N + g)),
                pl.BlockSpec((seq_len, SSD_N), lambda s, g: (rb0 + s, coff // SSD_N + g)),
                pl.BlockSpec((seq_len, V7X_LANES), lambda s, g: (rb0 + s, OFF_SSD_DT // V7X_LANES + g)),
                pl.BlockSpec((SSD_CONV, SSD_GW), lambda s, g: (0, g)),
                pl.BlockSpec((1, SSD_GW), lambda s, g: (0, g)),
                pl.BlockSpec((SSD_CONV, SSD_N), lambda s, g: (0, D_SSD // SSD_N + g)),
                pl.BlockSpec((1, SSD_N), lambda s, g: (0, D_SSD // SSD_N + g)),
                pl.BlockSpec((SSD_CONV, SSD_N), lambda s, g: (0, D_SSD // SSD_N + SSD_GROUPS + g)),
                pl.BlockSpec((1, SSD_N), lambda s, g: (0, D_SSD // SSD_N + SSD_GROUPS + g)),
                pl.BlockSpec((1, V7X_SUBLANES, V7X_LANES), lambda s, g: (g, 0, 0)),
                pl.BlockSpec((1, SSD_GW), lambda s, g: (0, g)),
                pl.BlockSpec((1, D_SSD), lambda s, g: (0, 0))]
    args = [proj, proj, proj, proj, proj, conv_w, conv_b, conv_w, conv_b, conv_w, conv_b, prm, dvec, ng]
    state_spec = pl.BlockSpec((1, 1, SSD_HPG, SSD_N, SSD_P), lambda s, g: (s, layer, g, 0, 0))
    y_spec = pl.BlockSpec((seq_len, D_SSD), lambda s, g: (rb0 + s, 0))
    if has_init:
        in_specs += [state_spec, state_spec]
        args += [init[0], init[1]]
        out_specs, out_shape = y_spec, _BRANCH_BUF
    else:
        st = jax.ShapeDtypeStruct((n_seq, DEPTH, SSD_HEADS, SSD_N, SSD_P), f32)
        out_specs, out_shape = [y_spec, state_spec, state_spec], [_BRANCH_BUF, st, st]
    scratch = [pltpu.VMEM((seq_len, SSD_GW), f32),
               pltpu.VMEM((seq_len, SSD_N), f32),
               pltpu.VMEM((seq_len, SSD_N), f32),
               pltpu.VMEM((seq_len, SSD_GW), f32),
               pltpu.VMEM((seq_len, V7X_LANES), f32),
               pltpu.VMEM((seq_len, V7X_LANES), f32),
               pltpu.VMEM((SSD_GROUPS, seq_len, SSD_GW), f32),
               pltpu.VMEM((SSD_N, SSD_GW), f32)]
    return _inplace_call(functools.partial(_ssd_kernel, seq_len=seq_len, has_init=has_init), bufs,
                         (n_seq, SSD_GROUPS), in_specs, args, out_specs, out_shape, scratch, "ssd_branch")


def _head_rmsnorm(x, g):
    return x * lax.rsqrt(jnp.mean(x * x, axis=-1, keepdims=True) + EPS) * g


CTX_HB = 4
CTX_W = CTX_HB * NA_HD


def _ctx_attn_kernel(q_ref, k_ref, v_ref, qg_ref, kg_ref, y_ref, ko_ref, vo_ref):
    scale = NA_HD ** -0.5
    vo_ref[0, 0] = v_ref[...].astype(f32)
    for h in range(CTX_HB):
        cols = pl.ds(h * NA_HD, NA_HD)
        q = _head_rmsnorm(q_ref[:, cols].astype(f32), qg_ref[...])
        k = _head_rmsnorm(k_ref[:, cols].astype(f32), kg_ref[...])
        ko_ref[0, 0, :, cols] = k
        s = _dot_nt(q.astype(bf16), k.astype(bf16)) * scale
        p = jnp.exp(s - jnp.max(s, axis=-1, keepdims=True))
        o = _dot(p.astype(bf16), v_ref[:, cols]) / jnp.sum(p, axis=-1, keepdims=True)
        y_ref[:, cols] = o.astype(bf16)


def context_attention(proj, bufs, row0, qg, kg, layer):
    rb0 = row0 // SEQ
    spec = lambda off: pl.BlockSpec((SEQ, CTX_W), lambda s, hb: (rb0 + s, off // CTX_W + hb))
    gspec = pl.BlockSpec((1, NA_HD), lambda s, hb: (0, 0))
    kv_spec = pl.BlockSpec((1, 1, SEQ, CTX_W), lambda s, hb: (s, layer, 0, hb))
    kv_shape = jax.ShapeDtypeStruct((BATCH, DEPTH, SEQ, NA_W), f32)
    return _inplace_call(
        _ctx_attn_kernel, bufs, (BATCH, NA_HEADS // CTX_HB),
        [spec(OFF_NA_Q), spec(OFF_NA_K), spec(OFF_NA_V), gspec, gspec], [proj, proj, proj, qg, kg],
        [pl.BlockSpec((SEQ, CTX_W), lambda s, hb: (rb0 + s, hb)), kv_spec, kv_spec],
        [_BRANCH_BUF, kv_shape, kv_shape], [], "context_attention")


NA_ROWS = DEC_SEQ // GRID_W
NA_NK = NA_WR * GRID_W


def _na_row_start(r):
    return min(max(r - NA_WR // 2, 0), NA_ROWS - NA_WR)


def _rope(x, cos, sin_signed):
    lane = lax.broadcasted_iota(jnp.int32, x.shape, 1)
    quarter = NA_HD // 4
    swapped = jnp.where((lane & (2 * quarter - 1)) < quarter,
                        pltpu.roll(x, NA_HD - quarter, axis=1), pltpu.roll(x, quarter, axis=1))
    return x * cos + swapped * sin_signed


def _na_row_groups():
    groups, r = [], 0
    while r < NA_ROWS:
        n = 1
        while r + n < NA_ROWS and _na_row_start(r + n) == _na_row_start(r):
            n += 1
        groups.append((r, n, _na_row_start(r)))
        r += n
    return groups


def _na_attn_kernel(q_ref, k_ref, v_ref, kc_ref, vc_ref, qg_ref, kg_ref, cos_ref, sin_ref,
                    bias_ref, valid_ref, y_ref, qs_ref, ks_ref, s_ref, p_ref, oc_ref, w_ref):
    scale = NA_HD ** -0.5
    cos = cos_ref[...]
    sin = sin_ref[...]
    qs_ref[...] = _rope(_head_rmsnorm(q_ref[...].astype(f32), qg_ref[...]), cos, sin).astype(bf16)
    ks_ref[...] = _rope(_head_rmsnorm(k_ref[...].astype(f32), kg_ref[...]), cos, sin).astype(bf16)
    groups = _na_row_groups()
    valid = valid_ref[...] > 0.0
    for r0, n, rs in groups:
        rows = pl.ds(r0 * GRID_W, n * GRID_W)
        d0 = r0 - rs
        kw = ks_ref[pl.ds(rs * GRID_W, NA_NK), :]
        s = (_dot_nt(qs_ref[rows, :], kw) * scale).reshape(n, GRID_W, NA_NK) + bias_ref[0, 0, d0:d0 + n]
        s_ref[rows, :] = jnp.where(valid[None], s, -1e30).reshape(n * GRID_W, NA_NK)
    s_ctx = _dot_nt(qs_ref[...], kc_ref[0, 0].astype(bf16)) * scale
    s_loc = s_ref[...]
    m = jnp.maximum(jnp.max(s_loc, axis=-1, keepdims=True), jnp.max(s_ctx, axis=-1, keepdims=True))
    p_loc = jnp.exp(s_loc - m)
    p_ctx = jnp.exp(s_ctx - m)
    w_ref[...] = 1.0 / (jnp.sum(p_loc, axis=-1, keepdims=True) + jnp.sum(p_ctx, axis=-1, keepdims=True))
    p_ref[...] = p_loc.astype(bf16)
    oc_ref[...] = _dot(p_ctx.astype(bf16), vc_ref[0, 0].astype(bf16))
    for r0, n, rs in groups:
        rows = pl.ds(r0 * GRID_W, n * GRID_W)
        vw = v_ref[pl.ds(rs * GRID_W, NA_NK), :]
        y_ref[rows, :] = ((_dot(p_ref[rows, :], vw) + oc_ref[rows, :]) * w_ref[rows, :]).astype(bf16)


def neighbourhood_attention(proj, cache_k, cache_v, layer, qg, kg, cos, sin, bias, valid):
    spec = lambda off: pl.BlockSpec((DEC_SEQ, NA_HD), lambda b, h: (b, off // NA_HD + h))
    cspec = pl.BlockSpec((1, 1, PAST_LEN, NA_HD), lambda b, h: (b, layer, 0, h))
    gspec = pl.BlockSpec((1, NA_HD), lambda b, h: (0, 0))
    tspec = pl.BlockSpec((DEC_SEQ, NA_HD), lambda b, h: (0, 0))
    return pl.pallas_call(
        _na_attn_kernel,
        grid=(DEC_BATCH, NA_HEADS),
        in_specs=[spec(OFF_NA_Q), spec(OFF_NA_K), spec(OFF_NA_V), cspec, cspec, gspec, gspec,
                  tspec, tspec,
                  pl.BlockSpec((1, 1, NA_WR, GRID_W, NA_NK), lambda b, h: (layer, h, 0, 0, 0)),
                  pl.BlockSpec((GRID_W, NA_NK), lambda b, h: (0, 0))],
        out_specs=pl.BlockSpec((DEC_SEQ, NA_HD), lambda b, h: (b, h)),
        out_shape=_BRANCH_BUF,
        scratch_shapes=[pltpu.VMEM((DEC_SEQ, NA_HD), bf16), pltpu.VMEM((DEC_SEQ, NA_HD), bf16),
                        pltpu.VMEM((DEC_SEQ, NA_NK), f32), pltpu.VMEM((DEC_SEQ, NA_NK), bf16),
                        pltpu.VMEM((DEC_SEQ, NA_HD), f32), pltpu.VMEM((DEC_SEQ, 1), f32)],
        compiler_params=_params(2), name="neighbourhood_attention",
    )(proj, proj, proj, cache_k, cache_v, qg, kg, cos, sin, bias, valid)


TAIL_TN = 512
N_TAIL_TILES = (D_IN_PAD - HEAD_COLS) // TAIL_TN
N_QKV_TILES = 3 * NA_W // TAIL_TN
DT_W = 2 * SSD_HEADS


def _tail_kernel(a_ref, b_ref, o_ref):
    j = pl.program_id(1)
    lane = lax.broadcasted_iota(jnp.int32, (D_MODEL, TAIL_TN), 1)

    @pl.when(j == 0)
    def _():
        o_ref[0] = a_ref[0]

    @pl.when((j >= 1) & (j <= N_QKV_TILES))
    def _():
        main = pltpu.roll(a_ref[0].astype(f32), TAIL_TN - DT_W, axis=1)
        nxt = pltpu.roll(b_ref[0].astype(f32), V7X_LANES - DT_W, axis=1)
        last_lane = lax.broadcasted_iota(jnp.int32, (D_MODEL, V7X_LANES), 1)
        last = jnp.where(last_lane < V7X_LANES - DT_W, main[:, TAIL_TN - V7X_LANES:], nxt)
        o_ref[0] = jnp.concatenate([main[:, :TAIL_TN - V7X_LANES], last], axis=1).astype(bf16)

    @pl.when(j == N_TAIL_TILES - 1)
    def _():
        a = a_ref[0].astype(f32)
        out = jnp.zeros((D_MODEL, TAIL_TN), f32)
        for gidx in range(SSD_GROUPS):
            base = gidx * V7X_LANES
            for src0, dst0 in ((gidx * SSD_HPG, base), (SSD_HEADS + gidx * SSD_HPG, base + SSD_HPG)):
                shift = (dst0 - src0) % TAIL_TN
                moved = a if shift == 0 else pltpu.roll(a, shift, axis=1)
                out = jnp.where((lane >= dst0) & (lane < dst0 + SSD_HPG), moved, out)
        o_ref[0] = out.astype(bf16)


def _split_w_in(w_in):
    w_in = w_in.astype(bf16)
    dt0 = IN_OFFSETS[7]
    a_tile0, dt_tile, b_tile0 = HEAD_COLS // TAIL_TN, dt0 // TAIL_TN, dt0 // V7X_LANES
    b_per_a = TAIL_TN // V7X_LANES
    last_b = (D_IN - 1) // V7X_LANES
    tail = pl.pallas_call(
        _tail_kernel, grid=(DEPTH, N_TAIL_TILES),
        in_specs=[pl.BlockSpec((1, D_MODEL, TAIL_TN),
                               lambda l, j: (l, 0, jnp.where(j == N_TAIL_TILES - 1, dt_tile, a_tile0 + j))),
                  pl.BlockSpec((1, D_MODEL, V7X_LANES),
                               lambda l, j: (l, 0, jnp.minimum(b_tile0 + b_per_a * j, last_b)))],
        out_specs=pl.BlockSpec((1, D_MODEL, TAIL_TN), lambda l, j: (l, 0, j)),
        out_shape=jax.ShapeDtypeStruct((DEPTH, D_MODEL, D_IN_PAD - HEAD_COLS), bf16),
        compiler_params=_params(2), name="build_w_in_tail",
    )(w_in, w_in)
    return w_in, tail


def _group_lanes(v):
    rows = []
    for gidx in range(SSD_GROUPS):
        sl = slice(gidx * SSD_HPG, (gidx + 1) * SSD_HPG)
        rows.append(jnp.concatenate([v[0, sl], v[1, sl], jnp.zeros((V7X_LANES - 2 * SSD_HPG,), f32)]))
    return jnp.stack(rows)


def _rope_tables():
    t = np.arange(DEC_SEQ)
    quarter = NA_HD // 4
    inv = ROPE_BASE ** (-np.arange(quarter, dtype=np.float32) / quarter)
    ang_r = (t // GRID_W).astype(np.float32)[:, None] * inv
    ang_c = (t % GRID_W).astype(np.float32)[:, None] * inv
    cos = np.concatenate([np.cos(ang_r), np.cos(ang_r), np.cos(ang_c), np.cos(ang_c)], axis=1)
    sin = np.concatenate([-np.sin(ang_r), np.sin(ang_r), -np.sin(ang_c), np.sin(ang_c)], axis=1)
    return jnp.asarray(cos, f32), jnp.asarray(sin, f32)


def _na_tables(rpb):
    cq = np.arange(GRID_W)
    kc = np.tile(np.arange(GRID_W), NA_WR)
    col_start = np.clip(cq - NA_WC // 2, 0, GRID_W - NA_WC)
    valid = (kc[None, :] >= col_start[:, None]) & (kc[None, :] < col_start[:, None] + NA_WC)
    col_off = np.clip(cq[None, :] - cq[:, None], 1 - NA_WC, NA_WC - 1) + NA_WC - 1
    onehot = (col_off[None, :, :] == np.arange(2 * NA_WC - 1)[:, None, None]).astype(np.float32)
    toep = jnp.einsum('lhic,cqk->lhiqk', rpb.astype(f32), jnp.asarray(onehot), precision=lax.Precision.HIGHEST)
    win = jnp.stack([toep[:, :, NA_WR - 1 - d:2 * NA_WR - 1 - d] for d in range(NA_WR)], axis=2)
    tables = win.transpose(0, 1, 2, 4, 3, 5).reshape(DEPTH, NA_HEADS, NA_WR, GRID_W, NA_NK)
    return tables, jnp.asarray(valid, f32)


def kernel(x_prompt, x_sample, cache_na_k, cache_na_v, state_lru_f, state_lru_b,
           state_ret_f, state_ret_b, state_ssd_f, state_ssd_b, c, c_ctx,
           norm1_g, norm2_g, w_ada, b_ada, w_in, w_gate, b_gate, w_branch, w_out,
           lru_conv_w, lru_conv_b, lru_wa, lru_ba, lru_wx, lru_bx, lru_lambda,
           ret_gn_g, ssd_conv_w, ssd_conv_b, ssd_a_log, ssd_dt_bias, ssd_d, ssd_norm_g,
           na_q_g, na_k_g, na_rpb, ffn_w_up, ffn_conv_w, ffn_conv_b, ffn_w_down):
    xs0 = x_sample.reshape(T_SAMPLE, D_MODEL)
    xp0 = x_prompt.reshape(T_PROMPT, D_MODEL)
    n_ptiles = N_TILES - N_SAMPLE_TILES
    cond = jnp.concatenate([c, c_ctx[None, :], jnp.zeros((N_COND_PAD - N_COND, D_MODEL), f32)], axis=0)
    mod_all = ada_modulation(cond, w_ada, b_ada)
    cos, sin = _rope_tables()
    bias, valid = _na_tables(na_rpb)
    hh = jnp.arange(RET_HEADS, dtype=f32)
    ret_la = jnp.stack([jnp.log1p(-jnp.exp2(-5.0 - hh)), jnp.log1p(-jnp.exp2(-5.5 - hh))], axis=1)
    cache_k = cache_na_k.reshape(DEC_BATCH, DEPTH, PAST_LEN, NA_W)
    cache_v = cache_na_v.reshape(DEC_BATCH, DEPTH, PAST_LEN, NA_W)
    lru_init = (state_lru_f.reshape(DEC_BATCH, DEPTH, 1, D_RNN), state_lru_b.reshape(DEC_BATCH, DEPTH, 1, D_RNN))

    w_head_b, w_tail_b = _split_w_in(w_in)
    w_branch_b, w_out_b, w_down_b = w_branch.astype(bf16), w_out.astype(bf16), ffn_w_down.astype(bf16)

    new = {k: None for k in ("k", "v", "lru_f", "lru_b", "ret_f", "ret_b", "ssd_f", "ssd_b")}
    for l in range(DEPTH):
        mod = mod_all[l].reshape(N_COND_PAD, 1, N_MOD * D_MODEL)
        g1 = norm1_g[l][None, :]
        if l == 0:
            proj, xn = in_projection(xs0, 0, 0, N_SAMPLE_TILES, g1, mod, w_head_b, w_tail_b, l)
            proj, xn = in_projection(xp0, 0, N_SAMPLE_TILES, n_ptiles, g1, mod, w_head_b, w_tail_b, l, (proj, xn))
        else:
            proj, xn = in_projection(x, 0, 0, N_TILES, g1, mod, w_head_b, w_tail_b, l)

        w4 = jnp.concatenate([lru_wa[l, 0], lru_wx[l, 0], lru_wa[l, 1], lru_wx[l, 1]], axis=-1).astype(bf16)
        b4 = jnp.concatenate([lru_ba[l, 0].reshape(LRU_BLOCKS, 1, LRU_BS), lru_bx[l, 0].reshape(LRU_BLOCKS, 1, LRU_BS),
                              lru_ba[l, 1].reshape(LRU_BLOCKS, 1, LRU_BS), lru_bx[l, 1].reshape(LRU_BLOCKS, 1, LRU_BS)],
                             axis=-1)
        lru_args = (lru_conv_w[l], lru_conv_b[l][None, :], w4, b4, lru_lambda[l])
        y_lru = lru_branch(proj, (None,), 0, DEC_BATCH, DEC_SEQ, *lru_args, lru_init, l)
        y_lru, new["lru_f"], new["lru_b"] = lru_branch(
            proj, (y_lru, new["lru_f"], new["lru_b"]), T_SAMPLE, BATCH, SEQ, *lru_args, None, l)

        gn = ret_gn_g[l][None, :]
        y_ret = retention_branch(proj, (None,), 0, DEC_BATCH, DEC_SEQ, ret_la, gn, (state_ret_f, state_ret_b), l)
        y_ret, new["ret_f"], new["ret_b"] = retention_branch(
            proj, (y_ret, new["ret_f"], new["ret_b"]), T_SAMPLE, BATCH, SEQ, ret_la, gn, None, l)

        prm = jnp.stack([_group_lanes(ssd_dt_bias[l]), _group_lanes(ssd_a_log[l])], axis=1)
        prm = jnp.concatenate([prm, jnp.zeros((SSD_GROUPS, V7X_SUBLANES - 2, V7X_LANES), f32)], axis=1)
        dvec = jnp.repeat(ssd_d[l], SSD_P)[None, :]
        ssd_args = (ssd_conv_w[l], ssd_conv_b[l][None, :], prm, dvec, ssd_norm_g[l][None, :])
        y_ssd = ssd_branch(proj, (None,), 0, DEC_BATCH, DEC_SEQ, *ssd_args, (state_ssd_f, state_ssd_b), l)
        y_ssd, new["ssd_f"], new["ssd_b"] = ssd_branch(
            proj, (y_ssd, new["ssd_f"], new["ssd_b"]), T_SAMPLE, BATCH, SEQ, *ssd_args, None, l)

        qg = na_q_g[l][None, :]
        kg = na_k_g[l][None, :]
        y_na = neighbourhood_attention(proj, cache_k, cache_v, l, qg, kg, cos, sin, bias, valid)
        y_na, new["k"], new["v"] = context_attention(proj, (y_na, new["k"], new["v"]), T_SAMPLE, qg, kg, l)

        merged = merge_branches(xn, (y_lru, y_ret, y_ssd, y_na), w_gate, b_gate[l][None, :], w_branch_b, l)
        if l == 0:
            x = residual_projection(merged, w_out_b, l, xs0, mod, 2, 0, N_SAMPLE_TILES)
            x = residual_projection(merged, w_out_b, l, xp0, mod, 2, N_SAMPLE_TILES, n_ptiles, x_tile0=0, buf=x)
        else:
            x = residual_projection(merged, w_out_b, l, x, mod, 2)
        hmid = ffn_up(x, norm2_g[l][None, :], mod, ffn_w_up, l, ffn_conv_w[l], ffn_conv_b[l][None, :])
        if l < DEPTH - 1:
            x = residual_projection(hmid, w_down_b, l, x, mod, 5)
        else:
            y_sample = residual_projection(hmid, w_down_b, l, x, mod, 5, 0, N_SAMPLE_TILES, out_rows=T_SAMPLE)
            y_prompt = residual_projection(hmid, w_down_b, l, x, mod, 5, N_SAMPLE_TILES, n_ptiles,
                                           out_tile0=0, out_rows=T_PROMPT)

    kv_shape = (BATCH, DEPTH, SEQ, NA_HEADS, NA_HD)
    return (y_prompt.reshape(BATCH, SEQ, D_MODEL), y_sample.reshape(DEC_BATCH, DEC_SEQ, D_MODEL),
            new["k"].reshape(kv_shape), new["v"].reshape(kv_shape),
            new["lru_f"].reshape(BATCH, DEPTH, D_RNN), new["lru_b"].reshape(BATCH, DEPTH, D_RNN),
            new["ret_f"], new["ret_b"], new["ssd_f"], new["ssd_b"])
```

```python
import functools
import math

import jax
import jax.numpy as jnp
import numpy as np
from jax import lax
from jax.experimental import pallas as pl
from jax.experimental.pallas import tpu as pltpu

D_MODEL = 2048
BATCH = 16
SEQ = 256
DEPTH = 2
DEC_BATCH = 8
DEC_SEQ = 1024
PAST_LEN = 256
GRID_W = 64
EPS = 1e-6
N_BRANCH = 4
BRANCH_W = 1024
N_MOD = 6
D_RNN = 1024
LRU_BLOCKS = 8
LRU_BS = D_RNN // LRU_BLOCKS
LRU_CONV = 4
LRU_C = 8.0
RET_HEADS = 4
RET_DK = 128
RET_DV = 256
SSD_HEADS = 16
SSD_P = 64
SSD_N = 128
SSD_GROUPS = 2
SSD_CONV = 4
D_SSD = SSD_HEADS * SSD_P
SSD_CONV_CH = D_SSD + 2 * SSD_GROUPS * SSD_N
NA_HEADS = 8
NA_HD = 128
NA_W = NA_HEADS * NA_HD
NA_WR = 8
NA_WC = 16
ROPE_BASE = 10000.0
D_FF = 5632
FFN_CONV = 3
IN_SIZES = (D_RNN, D_RNN,
            RET_HEADS * RET_DK, RET_HEADS * RET_DK, RET_HEADS * RET_DV, RET_HEADS * RET_DV,
            D_SSD, SSD_CONV_CH, 2 * SSD_HEADS,
            NA_W, NA_W, NA_W)
D_IN = sum(IN_SIZES)
IN_OFFSETS = tuple(int(s) for s in np.cumsum(IN_SIZES)[:-1])

V7X_LANES = 128
V7X_SUBLANES = 8
V7X_VMEM_LIMIT_BYTES = 56 * 1024 * 1024

TM = 1024
T_SAMPLE = DEC_BATCH * DEC_SEQ
T_PROMPT = BATCH * SEQ
T_ALL = T_SAMPLE + T_PROMPT
N_SAMPLE_TILES = T_SAMPLE // TM
N_TILES = T_ALL // TM
N_COND = DEC_BATCH + 1
N_COND_PAD = 16

SSD_HPG = SSD_HEADS // SSD_GROUPS
OFF_LRU_X = 0
OFF_LRU_G = OFF_LRU_X + D_RNN
OFF_RET_Q = OFF_LRU_G + D_RNN
OFF_RET_K = OFF_RET_Q + RET_HEADS * RET_DK
OFF_RET_V = OFF_RET_K + RET_HEADS * RET_DK
OFF_RET_G = OFF_RET_V + RET_HEADS * RET_DV
OFF_SSD_Z = OFF_RET_G + RET_HEADS * RET_DV
OFF_SSD_XBC = OFF_SSD_Z + D_SSD
OFF_NA_Q = OFF_SSD_XBC + SSD_CONV_CH
OFF_NA_K = OFF_NA_Q + NA_W
OFF_NA_V = OFF_NA_K + NA_W
OFF_SSD_DT = OFF_NA_V + NA_W
PROJ_TN = 1024
D_IN_PAD = -(-(OFF_SSD_DT + SSD_GROUPS * V7X_LANES) // PROJ_TN) * PROJ_TN
N_HEAD_TILES = OFF_NA_Q // PROJ_TN
HEAD_COLS = N_HEAD_TILES * PROJ_TN

RET_CHUNK = 256
SSD_CHUNK = 256
SSD_TB = 128

f32 = jnp.float32
bf16 = jnp.bfloat16

_ARB = "arbitrary"


def _params(n_axes):
    return pltpu.CompilerParams(dimension_semantics=(_ARB,) * n_axes,
                                vmem_limit_bytes=V7X_VMEM_LIMIT_BYTES)


def _mod_spec(k, tile0=0):
    return pl.BlockSpec((1, 1, D_MODEL), lambda i, j: (jnp.minimum(tile0 + i, N_SAMPLE_TILES), 0, k))


def _dot(a, b):
    return jnp.dot(a, b, preferred_element_type=f32)


def _dot_nt(a, b):
    return lax.dot_general(a, b, (((1,), (1,)), ((), ())), preferred_element_type=f32)


def _sigmoid(x):
    return 0.5 * jnp.tanh(0.5 * x) + 0.5


def _silu(x):
    return x * _sigmoid(x)


def _gelu_tanh(x):
    return 0.5 * x * (1.0 + jnp.tanh(math.sqrt(2.0 / math.pi) * (x + 0.044715 * (x * x * x))))


def _shift_rows(v, k):
    n = v.shape[0]
    row = lax.broadcasted_iota(jnp.int32, (n, 1), 0)
    if k == 1:
        return jnp.where(row == 0, 0.0, pltpu.roll(v, 1, axis=0))
    return jnp.where(row == n - 1, 0.0, pltpu.roll(v, n - 1, axis=0))


def _dwconv4(x, cw, cb):
    acc = _shift_rows(x * cw[0:1, :], 1) + x * cw[1:2, :]
    return cb + x * cw[2:3, :] + _shift_rows(acc, 1) + _shift_rows(x * cw[3:4, :], -1)


def _softplus(x):
    return jnp.maximum(x, 0.0) + jnp.log1p(jnp.exp(-jnp.abs(x)))


def _drop_refs(body, n, *refs):
    body(*refs[n:])


def _inplace_call(body, bufs, grid, in_specs, args, out_specs, out_shape, scratch, name):
    held = [(k, b) for k, b in enumerate(bufs) if b is not None]
    if held:
        body = functools.partial(_drop_refs, body, len(held))
        in_specs = [pl.BlockSpec(memory_space=pl.ANY)] * len(held) + list(in_specs)
        args = [b for _, b in held] + list(args)
    aliases = {pos: k for pos, (k, _) in enumerate(held)}
    return pl.pallas_call(
        body, grid=grid, in_specs=in_specs, out_specs=out_specs, out_shape=out_shape,
        scratch_shapes=scratch, input_output_aliases=aliases,
        compiler_params=_params(len(grid)), name=name,
    )(*args)


_BRANCH_BUF = jax.ShapeDtypeStruct((T_ALL, BRANCH_W), bf16)


ADA_TN = 1024


def _ada_kernel(c_ref, w_ref, b_ref, o_ref):
    c = _silu(c_ref[...]).astype(bf16)
    o_ref[0] = _dot(c, w_ref[0].astype(bf16)) + b_ref[0]


def ada_modulation(cond, w_ada, b_ada):
    n = N_MOD * D_MODEL
    return pl.pallas_call(
        _ada_kernel,
        grid=(DEPTH, n // ADA_TN),
        in_specs=[pl.BlockSpec((N_COND_PAD, D_MODEL), lambda l, j: (0, 0)),
                  pl.BlockSpec((1, D_MODEL, ADA_TN), lambda l, j: (l, 0, j)),
                  pl.BlockSpec((1, 1, ADA_TN), lambda l, j: (l, 0, j))],
        out_specs=pl.BlockSpec((1, N_COND_PAD, ADA_TN), lambda l, j: (l, 0, j)),
        out_shape=jax.ShapeDtypeStruct((DEPTH, N_COND_PAD, n), f32),
        compiler_params=_params(2),
        name="ada_modulation",
    )(cond, w_ada, b_ada.reshape(DEPTH, 1, n))


NORM_ROWS = 128
NORM_SPLIT = 2


def _modulated_norm_parts(x_ref, g_ref, sc_ref, sh_ref, xn_ref):
    gs = g_ref[...] * (1.0 + sc_ref[0])
    sh = sh_ref[0]
    part = TM // NORM_SPLIT
    for p in range(NORM_SPLIT):
        pieces = []
        for r in range(part // NORM_ROWS):
            rows = pl.ds(p * part + r * NORM_ROWS, NORM_ROWS)
            x = x_ref[rows, :]
            y = x * lax.rsqrt(jnp.mean(x * x, axis=-1, keepdims=True) + EPS)
            pieces.append((y * gs + sh).astype(bf16))
        xn = jnp.concatenate(pieces, axis=0)
        xn_ref[pl.ds(p * part, part), :] = xn
        yield xn


def _in_proj_kernel(x_ref, g_ref, sc_ref, sh_ref, wh_ref, wt_ref, o_ref, xn_ref):
    j = pl.program_id(1)

    @pl.when(j == 0)
    def _():
        w = wh_ref[0]
        outs = [_dot(xn, w).astype(bf16) for xn in _modulated_norm_parts(x_ref, g_ref, sc_ref, sh_ref, xn_ref)]
        o_ref[...] = jnp.concatenate(outs, axis=0)

    @pl.when((j > 0) & (j < N_HEAD_TILES))
    def _():
        o_ref[...] = _dot(xn_ref[...], wh_ref[0]).astype(bf16)

    @pl.when(j >= N_HEAD_TILES)
    def _():
        o_ref[...] = _dot(xn_ref[...], wt_ref[0]).astype(bf16)


def in_projection(x, x_tile0, tile0, n_tiles, g, mod, w_head, w_tail, layer, bufs=(None, None)):
    n = D_IN_PAD
    n_tail = w_tail.shape[2] // PROJ_TN
    return _inplace_call(
        _in_proj_kernel, bufs, (n_tiles, n // PROJ_TN),
        [pl.BlockSpec((TM, D_MODEL), lambda i, j: (x_tile0 + i, 0)),
         pl.BlockSpec((1, D_MODEL), lambda i, j: (0, 0)),
         _mod_spec(1, tile0), _mod_spec(0, tile0),
         pl.BlockSpec((1, D_MODEL, PROJ_TN), lambda i, j: (layer, 0, jnp.minimum(j, N_HEAD_TILES - 1))),
         pl.BlockSpec((1, D_MODEL, PROJ_TN),
                      lambda i, j: (layer, 0, jnp.clip(j - N_HEAD_TILES, 0, n_tail - 1)))],
        [x, g, mod, mod, w_head, w_tail],
        [pl.BlockSpec((TM, PROJ_TN), lambda i, j: (tile0 + i, j)),
         pl.BlockSpec((TM, D_MODEL), lambda i, j: (tile0 + i, 0))],
        [jax.ShapeDtypeStruct((T_ALL, n), bf16), jax.ShapeDtypeStruct((T_ALL, D_MODEL), bf16)],
        [], "in_projection")


MERGE_TN = 256


def _merge_kernel(xn_ref, *refs):
    y_refs, wg_refs, bg_refs = refs[0:4], refs[4:8], refs[8:12]
    wb_ref, o_ref = refs[12], refs[13]
    xn = xn_ref[...]
    acc = None
    for n in range(N_BRANCH):
        gate = _sigmoid(_dot(xn, wg_refs[n][0].astype(bf16)) + bg_refs[n][...])
        term = gate * _dot(y_refs[n][...], wb_ref[0, n])
        acc = term if acc is None else acc + term
    o_ref[...] = acc.astype(bf16)


def merge_branches(xn, ys, w_gate, b_gate, w_branch, layer):
    nj = D_MODEL // MERGE_TN
    y_spec = pl.BlockSpec((TM, BRANCH_W), lambda i, j: (i, 0))
    wg_specs = [pl.BlockSpec((1, D_MODEL, MERGE_TN), lambda i, j, n=n: (layer, 0, n * nj + j))
                for n in range(N_BRANCH)]
    bg_specs = [pl.BlockSpec((1, MERGE_TN), lambda i, j, n=n: (0, n * nj + j)) for n in range(N_BRANCH)]
    return pl.pallas_call(
        _merge_kernel,
        grid=(N_TILES, nj),
        in_specs=[pl.BlockSpec((TM, D_MODEL), lambda i, j: (i, 0)),
                  y_spec, y_spec, y_spec, y_spec, *wg_specs, *bg_specs,
                  pl.BlockSpec((1, N_BRANCH, BRANCH_W, MERGE_TN), lambda i, j: (layer, 0, 0, j))],
        out_specs=pl.BlockSpec((TM, MERGE_TN), lambda i, j: (i, j)),
        out_shape=jax.ShapeDtypeStruct((T_ALL, D_MODEL), bf16),
        compiler_params=_params(2),
        name="merge_branches",
    )(xn, *ys, *([w_gate] * N_BRANCH), *([b_gate] * N_BRANCH), w_branch)


RES_TN = 512
RES_TN_SHORT_K = 1024


def _residual_kernel(a_ref, w_ref, x_ref, gv_ref, o_ref):
    o_ref[...] = x_ref[...] + gv_ref[0] * _dot(a_ref[...], w_ref[0])


def residual_projection(a, w, layer, x, mod, k_mod, tile0=0, n_tiles=N_TILES, x_tile0=None, out_tile0=None,
                        out_rows=T_ALL, buf=None):
    kdim = a.shape[1]
    tn = RES_TN_SHORT_K if kdim == D_MODEL else RES_TN
    x_tile0 = tile0 if x_tile0 is None else x_tile0
    out_tile0 = tile0 if out_tile0 is None else out_tile0
    return _inplace_call(
        _residual_kernel, (buf,), (n_tiles, D_MODEL // tn),
        [pl.BlockSpec((TM, kdim), lambda i, j: (tile0 + i, 0)),
         pl.BlockSpec((1, kdim, tn), lambda i, j: (layer, 0, j)),
         pl.BlockSpec((TM, tn), lambda i, j: (x_tile0 + i, j)),
         pl.BlockSpec((1, 1, tn),
                      lambda i, j: (jnp.minimum(tile0 + i, N_SAMPLE_TILES), 0, k_mod * (D_MODEL // tn) + j))],
        [a, w, x, mod],
        pl.BlockSpec((TM, tn), lambda i, j: (out_tile0 + i, j)),
        jax.ShapeDtypeStruct((out_rows, D_MODEL), f32), [], "residual_projection")


FFN_TN = 512


def _ffn_up_kernel(x_ref, g_ref, sc_ref, sh_ref, wa_ref, wv_ref, cw_ref, cb_ref, o_ref, xn_ref):
    i = pl.program_id(0)

    def epilogue(a, v):
        seq_len = jnp.where(i < N_SAMPLE_TILES, DEC_SEQ, SEQ)
        pos = lax.broadcasted_iota(jnp.int32, (TM, 1), 0) & (seq_len - 1)
        prev = jnp.where(pos == 0, 0.0, pltpu.roll(a, 1, axis=0))
        nxt = jnp.where(pos == seq_len - 1, 0.0, pltpu.roll(a, TM - 1, axis=0))
        cw = cw_ref[...]
        conv = cb_ref[...] + prev * cw[0:1, :] + a * cw[1:2, :] + nxt * cw[2:3, :]
        o_ref[...] = (_gelu_tanh(conv) * v).astype(bf16)

    @pl.when(pl.program_id(1) == 0)
    def _():
        wa = wa_ref[0].astype(bf16)
        wv = wv_ref[0].astype(bf16)
        parts = [(_dot(xn, wa), _dot(xn, wv)) for xn in _modulated_norm_parts(x_ref, g_ref, sc_ref, sh_ref, xn_ref)]
        epilogue(jnp.concatenate([p[0] for p in parts], axis=0), jnp.concatenate([p[1] for p in parts], axis=0))

    @pl.when(pl.program_id(1) > 0)
    def _():
        xn = xn_ref[...]
        epilogue(_dot(xn, wa_ref[0].astype(bf16)), _dot(xn, wv_ref[0].astype(bf16)))


def ffn_up(x, g, mod, w_up, layer, conv_w, conv_b):
    nj = D_FF // FFN_TN
    return pl.pallas_call(
        _ffn_up_kernel,
        grid=(N_TILES, nj),
        in_specs=[pl.BlockSpec((TM, D_MODEL), lambda i, j: (i, 0)),
                  pl.BlockSpec((1, D_MODEL), lambda i, j: (0, 0)),
                  _mod_spec(4), _mod_spec(3),
                  pl.BlockSpec((1, D_MODEL, FFN_TN), lambda i, j: (layer, 0, j)),
                  pl.BlockSpec((1, D_MODEL, FFN_TN), lambda i, j: (layer, 0, nj + j)),
                  pl.BlockSpec((FFN_CONV, FFN_TN), lambda i, j: (0, j)),
                  pl.BlockSpec((1, FFN_TN), lambda i, j: (0, j))],
        out_specs=pl.BlockSpec((TM, FFN_TN), lambda i, j: (i, j)),
        out_shape=jax.ShapeDtypeStruct((T_ALL, D_FF), bf16),
        scratch_shapes=[pltpu.VMEM((TM, D_MODEL), bf16)],
        compiler_params=_params(2),
        name="ffn_up",
    )(x, g, mod, mod, w_up, w_up, conv_w, conv_b)


LRU_CB_LONG = D_RNN
LRU_CB_SHORT = D_RNN


def _lru_kernel(*refs, seq_len, has_init, cbw):
    if has_init:
        (x_ref, g_ref, cw_ref, cb_ref, w4_ref, b4_ref, lam_ref, h0f_ref, h0b_ref,
         y_ref, af_ref, uf_ref, ab_ref, ub_ref, hf_ref, hb_ref) = refs
    else:
        (x_ref, g_ref, cw_ref, cb_ref, w4_ref, b4_ref, lam_ref,
         y_ref, sf_ref, sb_ref, af_ref, uf_ref, ab_ref, ub_ref, hf_ref, hb_ref) = refs
    L = seq_len
    S = V7X_SUBLANES
    xc = _dwconv4(x_ref[...].astype(f32), cw_ref[...], cb_ref[...])
    sp = _softplus(-lam_ref[...])
    for n in range(cbw // LRU_BS):
        cols = slice(n * LRU_BS, (n + 1) * LRU_BS)
        xcn = xc[:, cols]
        z = _dot(xcn.astype(bf16), w4_ref[n]) + b4_ref[n]
        for d, (a_ref, u_ref) in enumerate(((af_ref, uf_ref), (ab_ref, ub_ref))):
            r = 0.5 * jnp.tanh(z[:, (2 * d) * LRU_BS:(2 * d + 1) * LRU_BS]) + 0.5
            ig = 0.5 * jnp.tanh(z[:, (2 * d + 1) * LRU_BS:(2 * d + 2) * LRU_BS]) + 0.5
            log_a = (-LRU_C) * r * sp[d:d + 1, cols]
            a = jnp.exp(log_a)
            a_ref[:, cols] = a
            gain2 = -jnp.tanh(log_a) * (a * a + 1.0)
            gain = jnp.where(gain2 > 0.0, gain2 * lax.rsqrt(gain2), 0.0)
            u_ref[:, cols] = gain * (ig * xcn)

    if has_init:
        h0f = h0f_ref[0, 0]
        h0b = h0b_ref[0, 0]
    else:
        h0f = jnp.zeros((1, cbw), f32)
        h0b = jnp.zeros((1, cbw), f32)

    def step(i, carry):
        hf, hb = carry
        base_f = pl.multiple_of(i * S, S)
        base_b = pl.multiple_of(L - S - i * S, S)
        for r in range(S):
            tf = pl.ds(base_f + r, 1)
            tb = pl.ds(base_b + (S - 1 - r), 1)
            hf = af_ref[tf, :] * hf + uf_ref[tf, :]
            hb = ab_ref[tb, :] * hb + ub_ref[tb, :]
            hf_ref[tf, :] = hf
            hb_ref[tb, :] = hb
        return hf, hb

    hf, hb = lax.fori_loop(0, L // S, step, (h0f, h0b))
    if not has_init:
        sf_ref[0, 0] = hf
        sb_ref[0, 0] = hb
    y_ref[...] = ((hf_ref[...] + hb_ref[...]) * _gelu_tanh(g_ref[...].astype(f32))).astype(bf16)


def lru_branch(proj, bufs, row0, n_seq, seq_len, cw, cb, w4, b4, lam, init, layer):
    has_init = init is not None
    cbw = LRU_CB_LONG if seq_len == DEC_SEQ else LRU_CB_SHORT
    ncb = D_RNN // cbw
    nb = cbw // LRU_BS
    rb0 = row0 // seq_len
    in_specs = [pl.BlockSpec((seq_len, cbw), lambda s, c: (rb0 + s, OFF_LRU_X // cbw + c)),
                pl.BlockSpec((seq_len, cbw), lambda s, c: (rb0 + s, OFF_LRU_G // cbw + c)),
                pl.BlockSpec((LRU_CONV, cbw), lambda s, c: (0, c)),
                pl.BlockSpec((1, cbw), lambda s, c: (0, c)),
                pl.BlockSpec((nb, LRU_BS, 4 * LRU_BS), lambda s, c: (c, 0, 0)),
                pl.BlockSpec((nb, 1, 4 * LRU_BS), lambda s, c: (c, 0, 0)),
                pl.BlockSpec((2, cbw), lambda s, c: (0, c))]
    args = [proj, proj, cw, cb, w4, b4, lam]
    state_spec = pl.BlockSpec((1, 1, 1, cbw), lambda s, c: (s, layer, 0, c))
    y_spec = pl.BlockSpec((seq_len, cbw), lambda s, c: (rb0 + s, c))
    if has_init:
        in_specs += [state_spec, state_spec]
        args += [init[0], init[1]]
        out_specs, out_shape = y_spec, _BRANCH_BUF
    else:
        st = jax.ShapeDtypeStruct((n_seq, DEPTH, 1, D_RNN), f32)
        out_specs, out_shape = [y_spec, state_spec, state_spec], [_BRANCH_BUF, st, st]
    scratch = [pltpu.VMEM((seq_len, cbw), f32) for _ in range(6)]
    return _inplace_call(functools.partial(_lru_kernel, seq_len=seq_len, has_init=has_init, cbw=cbw), bufs,
                         (n_seq, ncb), in_specs, args, out_specs, out_shape, scratch, "lru_branch")


def _ret_kernel(*refs, seq_len, has_init):
    if has_init:
        (la_ref, q_ref, k_ref, v_ref, g_ref, gn_ref, s0f_ref, s0b_ref, y_ref, acc_ref) = refs
    else:
        (la_ref, q_ref, k_ref, v_ref, g_ref, gn_ref, y_ref, sf_ref, sb_ref, acc_ref) = refs
    T = min(RET_CHUNK, seq_len)
    nc = seq_len // T
    tt = lax.broadcasted_iota(jnp.int32, (T, T), 0)
    ss = lax.broadcasted_iota(jnp.int32, (T, T), 1)
    diff = (tt - ss).astype(f32)
    tcol = lax.broadcasted_iota(jnp.int32, (T, 1), 0).astype(f32)
    scale = RET_DK ** -0.5
    for h in range(RET_HEADS):
        la_f = la_ref[h, 0]
        la_b = la_ref[h, 1]
        kcols = pl.ds(h * RET_DK, RET_DK)
        vcols = pl.ds(h * RET_DV, RET_DV)
        dsum = (jnp.where(tt >= ss, jnp.exp(la_f * diff), 0.0)
                + jnp.where(ss >= tt, jnp.exp(-la_b * diff), 0.0))

        def chunk(c):
            rows = pl.ds(c * T, T)
            q = q_ref[rows, kcols].astype(f32)
            ks = k_ref[rows, kcols].astype(f32) * scale
            v = v_ref[rows, vcols].astype(f32)
            return rows, q, ks, v

        s_f = s0f_ref[0, 0, h] if has_init else None
        for c in range(nc):
            rows, q, ks, v = chunk(c)
            scores = _dot_nt(q.astype(bf16), ks.astype(bf16)) * dsum
            y = _dot(scores.astype(bf16), v.astype(bf16))
            if s_f is not None:
                y = y + _dot((q * jnp.exp(la_f * (tcol + 1.0))).astype(bf16), s_f.astype(bf16))
            acc_ref[rows, :] = y
            if c < nc - 1 or not has_init:
                upd = _dot(ks.T.astype(bf16), (v * jnp.exp(la_f * (T - 1.0 - tcol))).astype(bf16))
                s_f = upd if s_f is None else jnp.exp(la_f * T) * s_f + upd
        s_b = s0b_ref[0, 0, h] if has_init else None
        for c in reversed(range(nc)):
            rows, q, ks, v = chunk(c)
            if s_b is not None:
                acc_ref[rows, :] += _dot((q * jnp.exp(la_b * (T - tcol))).astype(bf16), s_b.astype(bf16))
            if c > 0 or not has_init:
                upd = _dot(ks.T.astype(bf16), (v * jnp.exp(la_b * tcol)).astype(bf16))
                s_b = upd if s_b is None else jnp.exp(la_b * T) * s_b + upd
        if not has_init:
            sf_ref[0, 0, h] = s_f
            sb_ref[0, 0, h] = s_b
        y = acc_ref[...]
        mu = jnp.mean(y, axis=-1, keepdims=True)
        yc = y - mu
        var = jnp.mean(yc * yc, axis=-1, keepdims=True)
        y = yc * lax.rsqrt(var + EPS) * gn_ref[:, vcols]
        y_ref[:, vcols] = (y * _silu(g_ref[:, vcols].astype(f32))).astype(bf16)


def retention_branch(proj, bufs, row0, n_seq, seq_len, la, gn, init, layer):
    has_init = init is not None
    rb0 = row0 // seq_len
    qk_w, vg_w = RET_HEADS * RET_DK, RET_HEADS * RET_DV
    in_specs = [pl.BlockSpec(memory_space=pltpu.SMEM),
                pl.BlockSpec((seq_len, qk_w), lambda s: (rb0 + s, OFF_RET_Q // qk_w)),
                pl.BlockSpec((seq_len, qk_w), lambda s: (rb0 + s, OFF_RET_K // qk_w)),
                pl.BlockSpec((seq_len, vg_w), lambda s: (rb0 + s, OFF_RET_V // vg_w)),
                pl.BlockSpec((seq_len, vg_w), lambda s: (rb0 + s, OFF_RET_G // vg_w)),
                pl.BlockSpec((1, vg_w), lambda s: (0, 0))]
    args = [la, proj, proj, proj, proj, gn]
    state_spec = pl.BlockSpec((1, 1, RET_HEADS, RET_DK, RET_DV), lambda s: (s, layer, 0, 0, 0))
    y_spec = pl.BlockSpec((seq_len, vg_w), lambda s: (rb0 + s, 0))
    if has_init:
        in_specs += [state_spec, state_spec]
        args += [init[0], init[1]]
        out_specs, out_shape = y_spec, _BRANCH_BUF
    else:
        st = jax.ShapeDtypeStruct((n_seq, DEPTH, RET_HEADS, RET_DK, RET_DV), f32)
        out_specs, out_shape = [y_spec, state_spec, state_spec], [_BRANCH_BUF, st, st]
    return _inplace_call(functools.partial(_ret_kernel, seq_len=seq_len, has_init=has_init), bufs,
                         (n_seq,), in_specs, args, out_specs, out_shape,
                         [pltpu.VMEM((seq_len, RET_DV), f32)], "retention_branch")


SSD_GW = SSD_HPG * SSD_P


def _split3(x):
    h1 = x.astype(bf16)
    r1 = x - h1.astype(f32)
    h2 = r1.astype(bf16)
    h3 = (r1 - h2.astype(f32)).astype(bf16)
    return h1, h2, h3


def _dot_exact_rhs(m, x):
    h1, h2, h3 = _split3(x)
    return _dot(m, h1) + _dot(m, h2) + _dot(m, h3)


def _dot_exact_lhs(x, m):
    h1, h2, h3 = _split3(x)
    return _dot(h1, m) + _dot(h2, m) + _dot(h3, m)


def _heads_to_lanes(s_ref, st_ref):
    for hh in range(SSD_HPG):
        st_ref[:, pl.ds(hh * SSD_P, SSD_P)] = s_ref[0, 0, hh]
    return st_ref[...]


def _ssd_kernel(*refs, seq_len, has_init):
    if has_init:
        (z_ref, x_ref, b_ref, c_ref, dt_ref, cwx_ref, cbx_ref, cwb_ref, cbb_ref, cwc_ref, cbc_ref,
         prm_ref, dvec_ref, ng_ref, s0f_ref, s0b_ref,
         y_ref, xs_ref, bs_ref, cs_ref, acc_ref, rb_ref, dts_ref, yn_ref, st_ref) = refs
    else:
        (z_ref, x_ref, b_ref, c_ref, dt_ref, cwx_ref, cbx_ref, cwb_ref, cbb_ref, cwc_ref, cbc_ref,
         prm_ref, dvec_ref, ng_ref,
         y_ref, sf_ref, sb_ref, xs_ref, bs_ref, cs_ref, acc_ref, rb_ref, dts_ref, yn_ref, st_ref) = refs
    g = pl.program_id(1)
    L = seq_len
    T = min(SSD_CHUNK, L)
    nc = L // T
    H = SSD_HPG
    xs_ref[...] = _silu(_dwconv4(x_ref[...].astype(f32), cwx_ref[...], cbx_ref[...]))
    bs_ref[...] = _silu(_dwconv4(b_ref[...].astype(f32), cwb_ref[...], cbb_ref[...]))
    cs_ref[...] = _silu(_dwconv4(c_ref[...].astype(f32), cwc_ref[...], cbc_ref[...]))
    prm = prm_ref[0]
    a_neg = -jnp.exp(prm[1:2, :])
    tt = lax.broadcasted_iota(jnp.int32, (T, T), 0)
    ss = lax.broadcasted_iota(jnp.int32, (T, T), 1)
    lower = tt >= ss
    upper = ss >= tt
    tri_l = jnp.where(lower, 1.0, 0.0).astype(bf16)
    tri_u = jnp.where(upper, 1.0, 0.0).astype(bf16)
    lower_b = lower[:SSD_TB, :SSD_TB]
    upper_b = upper[:SSD_TB, :SSD_TB]
    er = lax.broadcasted_iota(jnp.int32, (V7X_LANES, SSD_GW), 0)
    ec = lax.broadcasted_iota(jnp.int32, (V7X_LANES, SSD_GW), 1) // SSD_P
    exp_f = jnp.where(er == ec, 1.0, 0.0).astype(bf16)
    exp_b = jnp.where(er == ec + H, 1.0, 0.0).astype(bf16)

    def expand(w, e):
        hi = w.astype(bf16)
        lo = (w - hi.astype(f32)).astype(bf16)
        return _dot(hi, e) + _dot(lo, e)

    s_f = _heads_to_lanes(s0f_ref, st_ref) if has_init else None
    for c in range(nc):
        rows = pl.ds(c * T, T)
        dt = _softplus(dt_ref[rows, :].astype(f32) + prm[0:1, :])
        dts_ref[rows, :] = dt
        da = dt * a_neg
        cum = _dot_exact_rhs(tri_l, da)
        rsum = (cum[T - 1:T, :] - cum) + da
        rb_ref[rows, :] = rsum
        da_t = da.T
        cum_t = _dot_exact_lhs(da_t, tri_u)
        rsum_t = (cum_t[:, T - 1:T] - cum_t) + da_t
        dt_t = dt.T
        bmat = bs_ref[rows, :]
        cmat = cs_ref[rows, :]
        xmat = xs_ref[rows, :]
        gmat = _dot_nt(cmat.astype(bf16), bmat.astype(bf16))
        for hh in range(H):
            cf, cf_t, dtf_t = cum[:, hh:hh + 1], cum_t[hh:hh + 1, :], dt_t[hh:hh + 1, :]
            rb, rb_t, dtb_t = rsum[:, H + hh:H + hh + 1], rsum_t[H + hh:H + hh + 1, :], dt_t[H + hh:H + hh + 1, :]
            block_rows = []
            for bi in range(T // SSD_TB):
                rr = slice(bi * SSD_TB, (bi + 1) * SSD_TB)
                blocks = []
                for bj in range(T // SSD_TB):
                    cc = slice(bj * SSD_TB, (bj + 1) * SSD_TB)
                    if bi > bj:
                        blk = jnp.exp(cf[rr] - cf_t[:, cc]) * dtf_t[:, cc]
                    elif bi < bj:
                        blk = jnp.exp(rb[rr] - rb_t[:, cc]) * dtb_t[:, cc]
                    else:
                        blk = (jnp.where(lower_b, jnp.exp(cf[rr] - cf_t[:, cc]), 0.0) * dtf_t[:, cc]
                               + jnp.where(upper_b, jnp.exp(rb[rr] - rb_t[:, cc]), 0.0) * dtb_t[:, cc])
                    blocks.append(gmat[rr, cc] * blk)
                block_rows.append(jnp.concatenate(blocks, axis=1))
            m = jnp.concatenate(block_rows, axis=0)
            xh = xmat[:, hh * SSD_P:(hh + 1) * SSD_P]
            acc_ref[rows, pl.ds(hh * SSD_P, SSD_P)] = _dot(m.astype(bf16), xh.astype(bf16))
        ecum = jnp.exp(cum)
        if s_f is not None:
            acc_ref[rows, :] += _dot(cmat.astype(bf16), s_f.astype(bf16)) * expand(ecum, exp_f)
        if c < nc - 1 or not has_init:
            tail = jnp.exp(cum[T - 1:T, :] - cum) * dt
            xw = (xmat * expand(tail, exp_f)).astype(bf16)
            upd = _dot(bmat.T.astype(bf16), xw)
            if s_f is None:
                s_f = upd
            else:
                s_f = s_f * expand(jnp.broadcast_to(ecum[T - 1:T, :], (V7X_SUBLANES, V7X_LANES)),
                                   exp_f)[0:1, :] + upd
    s_b = _heads_to_lanes(s0b_ref, st_ref) if has_init else None
    for c in reversed(range(nc)):
        rows = pl.ds(c * T, T)
        dt = dts_ref[rows, :]
        rsum = rb_ref[rows, :]
        ers = jnp.exp(rsum)
        bmat = bs_ref[rows, :]
        cmat = cs_ref[rows, :]
        xmat = xs_ref[rows, :]
        if s_b is not None:
            acc_ref[rows, :] += _dot(cmat.astype(bf16), s_b.astype(bf16)) * expand(ers, exp_b)
        if c > 0 or not has_init:
            tail = jnp.exp(rsum[0:1, :] - rsum) * dt
            xw = (xmat * expand(tail, exp_b)).astype(bf16)
            upd = _dot(bmat.T.astype(bf16), xw)
            if s_b is None:
                s_b = upd
            else:
                s_b = s_b * expand(jnp.broadcast_to(ers[0:1, :], (V7X_SUBLANES, V7X_LANES)),
                                   exp_b)[0:1, :] + upd
    if not has_init:
        for hh in range(H):
            sf_ref[0, 0, hh] = s_f[:, hh * SSD_P:(hh + 1) * SSD_P]
            sb_ref[0, 0, hh] = s_b[:, hh * SSD_P:(hh + 1) * SSD_P]
    yg = (acc_ref[...] + xs_ref[...] * dvec_ref[...]) * _silu(z_ref[...].astype(f32))
    yn_ref[g] = yg

    @pl.when(g == SSD_GROUPS - 1)
    def _():
        ssq = None
        for gg in range(SSD_GROUPS):
            y = yn_ref[gg]
            s = jnp.sum(y * y, axis=-1, keepdims=True)
            ssq = s if ssq is None else ssq + s
        inv = lax.rsqrt(ssq * (1.0 / D_SSD) + EPS)
        for gg in range(SSD_GROUPS):
            cols = pl.ds(gg * SSD_GW, SSD_GW)
            y_ref[:, cols] = (yn_ref[gg] * inv * ng_ref[:, cols]).astype(bf16)


def ssd_branch(proj, bufs, row0, n_seq, seq_len, conv_w, conv_b, prm, dvec, ng, init, layer):
    has_init = init is not None
    rb0 = row0 // seq_len
    xoff = OFF_SSD_XBC
    boff = OFF_SSD_XBC + D_SSD
    coff = boff + SSD_GROUPS * SSD_N
    in_specs = [pl.BlockSpec((seq_len, SSD_GW), lambda s, g: (rb0 + s, OFF_SSD_Z // SSD_GW + g)),
                pl.BlockSpec((seq_len, SSD_GW), lambda s, g: (rb0 + s, xoff // SSD_GW + g)),
                pl.BlockSpec((seq_len, SSD_N), lambda s, g: (rb0 + s, boff // SSD_N + g)),
                pl.BlockSpec((seq_len, SSD_N), lambda s, g: (rb0 + s, coff // SSD_N + g)),
                pl.BlockSpec((seq_len, V7X_LANES), lambda s, g: (rb0 + s, OFF_SSD_DT // V7X_LANES + g)),
                pl.BlockSpec((SSD_CONV, SSD_GW), lambda s, g: (0, g)),
                pl.BlockSpec((1, SSD_GW), lambda s, g: (0, g)),
                pl.BlockSpec((SSD_CONV, SSD_N), lambda s, g: (0, D_SSD // SSD_N + g)),
                pl.BlockSpec((1, SSD_N), lambda s, g: (0, D_SSD // SSD_N + g)),
                pl.BlockSpec((SSD_CONV, SSD_N), lambda s, g: (0, D_SSD // SSD_N + SSD_GROUPS + g)),
                pl.BlockSpec((1, SSD_N), lambda s, g: (0, D_SSD // SSD_N + SSD_GROUPS + g)),
                pl.BlockSpec((1, V7X_SUBLANES, V7X_LANES), lambda s, g: (g, 0, 0)),
                pl.BlockSpec((1, SSD_GW), lambda s, g: (0, g)),
                pl.BlockSpec((1, D_SSD), lambda s, g: (0, 0))]
    args = [proj, proj, proj, proj, proj, conv_w, conv_b, conv_w, conv_b, conv_w, conv_b, prm, dvec, ng]
    state_spec = pl.BlockSpec((1, 1, SSD_HPG, SSD_N, SSD_P), lambda s, g: (s, layer, g, 0, 0))
    y_spec = pl.BlockSpec((seq_len, D_SSD), lambda s, g: (rb0 + s, 0))
    if has_init:
        in_specs += [state_spec, state_spec]
        args += [init[0], init[1]]
        out_specs, out_shape = y_spec, _BRANCH_BUF
    else:
        st = jax.ShapeDtypeStruct((n_seq, DEPTH, SSD_HEADS, SSD_N, SSD_P), f32)
        out_specs, out_shape = [y_spec, state_spec, state_spec], [_BRANCH_BUF, st, st]
    scratch = [pltpu.VMEM((seq_len, SSD_GW), f32),
               pltpu.VMEM((seq_len, SSD_N), f32),
               pltpu.VMEM((seq_len, SSD_N), f32),
               pltpu.VMEM((seq_len, SSD_GW), f32),
               pltpu.VMEM((seq_len, V7X_LANES), f32),
               pltpu.VMEM((seq_len, V7X_LANES), f32),
               pltpu.VMEM((SSD_GROUPS, seq_len, SSD_GW), f32),
               pltpu.VMEM((SSD_N, SSD_GW), f32)]
    return _inplace_call(functools.partial(_ssd_kernel, seq_len=seq_len, has_init=has_init), bufs,
                         (n_seq, SSD_GROUPS), in_specs, args, out_specs, out_shape, scratch, "ssd_branch")


def _head_rmsnorm(x, g):
    return x * lax.rsqrt(jnp.mean(x * x, axis=-1, keepdims=True) + EPS) * g


CTX_HB = 4
CTX_W = CTX_HB * NA_HD


def _ctx_attn_kernel(q_ref, k_ref, v_ref, qg_ref, kg_ref, y_ref, ko_ref, vo_ref):
    scale = NA_HD ** -0.5
    vo_ref[0, 0] = v_ref[...].astype(f32)
    for h in range(CTX_HB):
        cols = pl.ds(h * NA_HD, NA_HD)
        q = _head_rmsnorm(q_ref[:, cols].astype(f32), qg_ref[...])
        k = _head_rmsnorm(k_ref[:, cols].astype(f32), kg_ref[...])
        ko_ref[0, 0, :, cols] = k
        s = _dot_nt(q.astype(bf16), k.astype(bf16)) * scale
        p = jnp.exp(s - jnp.max(s, axis=-1, keepdims=True))
        o = _dot(p.astype(bf16), v_ref[:, cols]) / jnp.sum(p, axis=-1, keepdims=True)
        y_ref[:, cols] = o.astype(bf16)


def context_attention(proj, bufs, row0, qg, kg, layer):
    rb0 = row0 // SEQ
    spec = lambda off: pl.BlockSpec((SEQ, CTX_W), lambda s, hb: (rb0 + s, off // CTX_W + hb))
    gspec = pl.BlockSpec((1, NA_HD), lambda s, hb: (0, 0))
    kv_spec = pl.BlockSpec((1, 1, SEQ, CTX_W), lambda s, hb: (s, layer, 0, hb))
    kv_shape = jax.ShapeDtypeStruct((BATCH, DEPTH, SEQ, NA_W), f32)
    return _inplace_call(
        _ctx_attn_kernel, bufs, (BATCH, NA_HEADS // CTX_HB),
        [spec(OFF_NA_Q), spec(OFF_NA_K), spec(OFF_NA_V), gspec, gspec], [proj, proj, proj, qg, kg],
        [pl.BlockSpec((SEQ, CTX_W), lambda s, hb: (rb0 + s, hb)), kv_spec, kv_spec],
        [_BRANCH_BUF, kv_shape, kv_shape], [], "context_attention")


NA_ROWS = DEC_SEQ // GRID_W
NA_NK = NA_WR * GRID_W


def _na_row_start(r):
    return min(max(r - NA_WR // 2, 0), NA_ROWS - NA_WR)


def _rope(x, cos, sin_signed):
    lane = lax.broadcasted_iota(jnp.int32, x.shape, 1)
    quarter = NA_HD // 4
    swapped = jnp.where((lane & (2 * quarter - 1)) < quarter,
                        pltpu.roll(x, NA_HD - quarter, axis=1), pltpu.roll(x, quarter, axis=1))
    return x * cos + swapped * sin_signed


def _na_row_groups():
    groups, r = [], 0
    while r < NA_ROWS:
        n = 1
        while r + n < NA_ROWS and _na_row_start(r + n) == _na_row_start(r):
            n += 1
        groups.append((r, n, _na_row_start(r)))
        r += n
    return groups


def _na_attn_kernel(q_ref, k_ref, v_ref, kc_ref, vc_ref, qg_ref, kg_ref, cos_ref, sin_ref,
                    bias_ref, valid_ref, y_ref, qs_ref, ks_ref, s_ref, p_ref, oc_ref, w_ref):
    scale = NA_HD ** -0.5
    cos = cos_ref[...]
    sin = sin_ref[...]
    qs_ref[...] = _rope(_head_rmsnorm(q_ref[...].astype(f32), qg_ref[...]), cos, sin).astype(bf16)
    ks_ref[...] = _rope(_head_rmsnorm(k_ref[...].astype(f32), kg_ref[...]), cos, sin).astype(bf16)
    groups = _na_row_groups()
    valid = valid_ref[...] > 0.0
    for r0, n, rs in groups:
        rows = pl.ds(r0 * GRID_W, n * GRID_W)
        d0 = r0 - rs
        kw = ks_ref[pl.ds(rs * GRID_W, NA_NK), :]
        s = (_dot_nt(qs_ref[rows, :], kw) * scale).reshape(n, GRID_W, NA_NK) + bias_ref[0, 0, d0:d0 + n]
        s_ref[rows, :] = jnp.where(valid[None], s, -1e30).reshape(n * GRID_W, NA_NK)
    s_ctx = _dot_nt(qs_ref[...], kc_ref[0, 0].astype(bf16)) * scale
    s_loc = s_ref[...]
    m = jnp.maximum(jnp.max(s_loc, axis=-1, keepdims=True), jnp.max(s_ctx, axis=-1, keepdims=True))
    p_loc = jnp.exp(s_loc - m)
    p_ctx = jnp.exp(s_ctx - m)
    w_ref[...] = 1.0 / (jnp.sum(p_loc, axis=-1, keepdims=True) + jnp.sum(p_ctx, axis=-1, keepdims=True))
    p_ref[...] = p_loc.astype(bf16)
    oc_ref[...] = _dot(p_ctx.astype(bf16), vc_ref[0, 0].astype(bf16))
    for r0, n, rs in groups:
        rows = pl.ds(r0 * GRID_W, n * GRID_W)
        vw = v_ref[pl.ds(rs * GRID_W, NA_NK), :]
        y_ref[rows, :] = ((_dot(p_ref[rows, :], vw) + oc_ref[rows, :]) * w_ref[rows, :]).astype(bf16)


def neighbourhood_attention(proj, cache_k, cache_v, layer, qg, kg, cos, sin, bias, valid):
    spec = lambda off: pl.BlockSpec((DEC_SEQ, NA_HD), lambda b, h: (b, off // NA_HD + h))
    cspec = pl.BlockSpec((1, 1, PAST_LEN, NA_HD), lambda b, h: (b, layer, 0, h))
    gspec = pl.BlockSpec((1, NA_HD), lambda b, h: (0, 0))
    tspec = pl.BlockSpec((DEC_SEQ, NA_HD), lambda b, h: (0, 0))
    return pl.pallas_call(
        _na_attn_kernel,
        grid=(DEC_BATCH, NA_HEADS),
        in_specs=[spec(OFF_NA_Q), spec(OFF_NA_K), spec(OFF_NA_V), cspec, cspec, gspec, gspec,
                  tspec, tspec,
                  pl.BlockSpec((1, 1, NA_WR, GRID_W, NA_NK), lambda b, h: (layer, h, 0, 0, 0)),
                  pl.BlockSpec((GRID_W, NA_NK), lambda b, h: (0, 0))],
        out_specs=pl.BlockSpec((DEC_SEQ, NA_HD), lambda b, h: (b, h)),
        out_shape=_BRANCH_BUF,
        scratch_shapes=[pltpu.VMEM((DEC_SEQ, NA_HD), bf16), pltpu.VMEM((DEC_SEQ, NA_HD), bf16),
                        pltpu.VMEM((DEC_SEQ, NA_NK), f32), pltpu.VMEM((DEC_SEQ, NA_NK), bf16),
                        pltpu.VMEM((DEC_SEQ, NA_HD), f32), pltpu.VMEM((DEC_SEQ, 1), f32)],
        compiler_params=_params(2), name="neighbourhood_attention",
    )(proj, proj, proj, cache_k, cache_v, qg, kg, cos, sin, bias, valid)


TAIL_TN = 512
N_TAIL_TILES = (D_IN_PAD - HEAD_COLS) // TAIL_TN
N_QKV_TILES = 3 * NA_W // TAIL_TN
DT_W = 2 * SSD_HEADS


def _tail_kernel(a_ref, b_ref, o_ref):
    j = pl.program_id(1)
    lane = lax.broadcasted_iota(jnp.int32, (D_MODEL, TAIL_TN), 1)

    @pl.when(j == 0)
    def _():
        o_ref[0] = a_ref[0]

    @pl.when((j >= 1) & (j <= N_QKV_TILES))
    def _():
        main = pltpu.roll(a_ref[0].astype(f32), TAIL_TN - DT_W, axis=1)
        nxt = pltpu.roll(b_ref[0].astype(f32), V7X_LANES - DT_W, axis=1)
        last_lane = lax.broadcasted_iota(jnp.int32, (D_MODEL, V7X_LANES), 1)
        last = jnp.where(last_lane < V7X_LANES - DT_W, main[:, TAIL_TN - V7X_LANES:], nxt)
        o_ref[0] = jnp.concatenate([main[:, :TAIL_TN - V7X_LANES], last], axis=1).astype(bf16)

    @pl.when(j == N_TAIL_TILES - 1)
    def _():
        a = a_ref[0].astype(f32)
        out = jnp.zeros((D_MODEL, TAIL_TN), f32)
        for gidx in range(SSD_GROUPS):
            base = gidx * V7X_LANES
            for src0, dst0 in ((gidx * SSD_HPG, base), (SSD_HEADS + gidx * SSD_HPG, base + SSD_HPG)):
                shift = (dst0 - src0) % TAIL_TN
                moved = a if shift == 0 else pltpu.roll(a, shift, axis=1)
                out = jnp.where((lane >= dst0) & (lane < dst0 + SSD_HPG), moved, out)
        o_ref[0] = out.astype(bf16)


def _split_w_in(w_in):
    w_in = w_in.astype(bf16)
    dt0 = IN_OFFSETS[7]
    a_tile0, dt_tile, b_tile0 = HEAD_COLS // TAIL_TN, dt0 // TAIL_TN, dt0 // V7X_LANES
    b_per_a = TAIL_TN // V7X_LANES
    last_b = (D_IN - 1) // V7X_LANES
    tail = pl.pallas_call(
        _tail_kernel, grid=(DEPTH, N_TAIL_TILES),
        in_specs=[pl.BlockSpec((1, D_MODEL, TAIL_TN),
                               lambda l, j: (l, 0, jnp.where(j == N_TAIL_TILES - 1, dt_tile, a_tile0 + j))),
                  pl.BlockSpec((1, D_MODEL, V7X_LANES),
                               lambda l, j: (l, 0, jnp.minimum(b_tile0 + b_per_a * j, last_b)))],
        out_specs=pl.BlockSpec((1, D_MODEL, TAIL_TN), lambda l, j: (l, 0, j)),
        out_shape=jax.ShapeDtypeStruct((DEPTH, D_MODEL, D_IN_PAD - HEAD_COLS), bf16),
        compiler_params=_params(2), name="build_w_in_tail",
    )(w_in, w_in)
    return w_in, tail


def _group_lanes(v):
    rows = []
    for gidx in range(SSD_GROUPS):
        sl = slice(gidx * SSD_HPG, (gidx + 1) * SSD_HPG)
        rows.append(jnp.concatenate([v[0, sl], v[1, sl], jnp.zeros((V7X_LANES - 2 * SSD_HPG,), f32)]))
    return jnp.stack(rows)


def _rope_tables():
    t = np.arange(DEC_SEQ)
    quarter = NA_HD // 4
    inv = ROPE_BASE ** (-np.arange(quarter, dtype=np.float32) / quarter)
    ang_r = (t // GRID_W).astype(np.float32)[:, None] * inv
    ang_c = (t % GRID_W).astype(np.float32)[:, None] * inv
    cos = np.concatenate([np.cos(ang_r), np.cos(ang_r), np.cos(ang_c), np.cos(ang_c)], axis=1)
    sin = np.concatenate([-np.sin(ang_r), np.sin(ang_r), -np.sin(ang_c), np.sin(ang_c)], axis=1)
    return jnp.asarray(cos, f32), jnp.asarray(sin, f32)


def _na_tables(rpb):
    cq = np.arange(GRID_W)
    kc = np.tile(np.arange(GRID_W), NA_WR)
    col_start = np.clip(cq - NA_WC // 2, 0, GRID_W - NA_WC)
    valid = (kc[None, :] >= col_start[:, None]) & (kc[None, :] < col_start[:, None] + NA_WC)
    col_off = np.clip(cq[None, :] - cq[:, None], 1 - NA_WC, NA_WC - 1) + NA_WC - 1
    onehot = (col_off[None, :, :] == np.arange(2 * NA_WC - 1)[:, None, None]).astype(np.float32)
    toep = jnp.einsum('lhic,cqk->lhiqk', rpb.astype(f32), jnp.asarray(onehot), precision=lax.Precision.HIGHEST)
    win = jnp.stack([toep[:, :, NA_WR - 1 - d:2 * NA_WR - 1 - d] for d in range(NA_WR)], axis=2)
    tables = win.transpose(0, 1, 2, 4, 3, 5).reshape(DEPTH, NA_HEADS, NA_WR, GRID_W, NA_NK)
    return tables, jnp.asarray(valid, f32)


def kernel(x_prompt, x_sample, cache_na_k, cache_na_v, state_lru_f, state_lru_b,
           state_ret_f, state_ret_b, state_ssd_f, state_ssd_b, c, c_ctx,
           norm1_g, norm2_g, w_ada, b_ada, w_in, w_gate, b_gate, w_branch, w_out,
           lru_conv_w, lru_conv_b, lru_wa, lru_ba, lru_wx, lru_bx, lru_lambda,
           ret_gn_g, ssd_conv_w, ssd_conv_b, ssd_a_log, ssd_dt_bias, ssd_d, ssd_norm_g,
           na_q_g, na_k_g, na_rpb, ffn_w_up, ffn_conv_w, ffn_conv_b, ffn_w_down):
    xs0 = x_sample.reshape(T_SAMPLE, D_MODEL)
    xp0 = x_prompt.reshape(T_PROMPT, D_MODEL)
    n_ptiles = N_TILES - N_SAMPLE_TILES
    cond = jnp.concatenate([c, c_ctx[None, :], jnp.zeros((N_COND_PAD - N_COND, D_MODEL), f32)], axis=0)
    mod_all = ada_modulation(cond, w_ada, b_ada)
    cos, sin = _rope_tables()
    bias, valid = _na_tables(na_rpb)
    hh = jnp.arange(RET_HEADS, dtype=f32)
    ret_la = jnp.stack([jnp.log1p(-jnp.exp2(-5.0 - hh)), jnp.log1p(-jnp.exp2(-5.5 - hh))], axis=1)
    cache_k = cache_na_k.reshape(DEC_BATCH, DEPTH, PAST_LEN, NA_W)
    cache_v = cache_na_v.reshape(DEC_BATCH, DEPTH, PAST_LEN, NA_W)
    lru_init = (state_lru_f.reshape(DEC_BATCH, DEPTH, 1, D_RNN), state_lru_b.reshape(DEC_BATCH, DEPTH, 1, D_RNN))

    w_head_b, w_tail_b = _split_w_in(w_in)
    w_branch_b, w_out_b, w_down_b = w_branch.astype(bf16), w_out.astype(bf16), ffn_w_down.astype(bf16)

    new = {k: None for k in ("k", "v", "lru_f", "lru_b", "ret_f", "ret_b", "ssd_f", "ssd_b")}
    for l in range(DEPTH):
        mod = mod_all[l].reshape(N_COND_PAD, 1, N_MOD * D_MODEL)
        g1 = norm1_g[l][None, :]
        if l == 0:
            proj, xn = in_projection(xs0, 0, 0, N_SAMPLE_TILES, g1, mod, w_head_b, w_tail_b, l)
            proj, xn = in_projection(xp0, 0, N_SAMPLE_TILES, n_ptiles, g1, mod, w_head_b, w_tail_b, l, (proj, xn))
        else:
            proj, xn = in_projection(x, 0, 0, N_TILES, g1, mod, w_head_b, w_tail_b, l)

        w4 = (0.5 * jnp.concatenate([lru_wa[l, 0], lru_wx[l, 0], lru_wa[l, 1], lru_wx[l, 1]], axis=-1)).astype(bf16)
        b4 = 0.5 * jnp.concatenate([lru_ba[l, 0].reshape(LRU_BLOCKS, 1, LRU_BS), lru_bx[l, 0].reshape(LRU_BLOCKS, 1, LRU_BS),
                                    lru_ba[l, 1].reshape(LRU_BLOCKS, 1, LRU_BS), lru_bx[l, 1].reshape(LRU_BLOCKS, 1, LRU_BS)],
                                   axis=-1)
        lru_args = (lru_conv_w[l], lru_conv_b[l][None, :], w4, b4, lru_lambda[l])
        y_lru = lru_branch(proj, (None,), 0, DEC_BATCH, DEC_SEQ, *lru_args, lru_init, l)
        y_lru, new["lru_f"], new["lru_b"] = lru_branch(
            proj, (y_lru, new["lru_f"], new["lru_b"]), T_SAMPLE, BATCH, SEQ, *lru_args, None, l)

        gn = ret_gn_g[l][None, :]
        y_ret = retention_branch(proj, (None,), 0, DEC_BATCH, DEC_SEQ, ret_la, gn, (state_ret_f, state_ret_b), l)
        y_ret, new["ret_f"], new["ret_b"] = retention_branch(
            proj, (y_ret, new["ret_f"], new["ret_b"]), T_SAMPLE, BATCH, SEQ, ret_la, gn, None, l)

        prm = jnp.stack([_group_lanes(ssd_dt_bias[l]), _group_lanes(ssd_a_log[l])], axis=1)
        prm = jnp.concatenate([prm, jnp.zeros((SSD_GROUPS, V7X_SUBLANES - 2, V7X_LANES), f32)], axis=1)
        dvec = jnp.repeat(ssd_d[l], SSD_P)[None, :]
        ssd_args = (ssd_conv_w[l], ssd_conv_b[l][None, :], prm, dvec, ssd_norm_g[l][None, :])
        y_ssd = ssd_branch(proj, (None,), 0, DEC_BATCH, DEC_SEQ, *ssd_args, (state_ssd_f, state_ssd_b), l)
        y_ssd, new["ssd_f"], new["ssd_b"] = ssd_branch(
            proj, (y_ssd, new["ssd_f"], new["ssd_b"]), T_SAMPLE, BATCH, SEQ, *ssd_args, None, l)

        qg = na_q_g[l][None, :]
        kg = na_k_g[l][None, :]
        y_na = neighbourhood_attention(proj, cache_k, cache_v, l, qg, kg, cos, sin, bias, valid)
        y_na, new["k"], new["v"] = context_attention(proj, (y_na, new["k"], new["v"]), T_SAMPLE, qg, kg, l)

        merged = merge_branches(xn, (y_lru, y_ret, y_ssd, y_na), w_gate, b_gate[l][None, :], w_branch_b, l)
        if l == 0:
            x = residual_projection(merged, w_out_b, l, xs0, mod, 2, 0, N_SAMPLE_TILES)
            x = residual_projection(merged, w_out_b, l, xp0, mod, 2, N_SAMPLE_TILES, n_ptiles, x_tile0=0, buf=x)
        else:
            x = residual_projection(merged, w_out_b, l, x, mod, 2)
        hmid = ffn_up(x, norm2_g[l][None, :], mod, ffn_w_up, l, ffn_conv_w[l], ffn_conv_b[l][None, :])
        if l < DEPTH - 1:
            x = residual_projection(hmid, w_down_b, l, x, mod, 5)
        else:
            y_sample = residual_projection(hmid, w_down_b, l, x, mod, 5, 0, N_SAMPLE_TILES, out_rows=T_SAMPLE)
            y_prompt = residual_projection(hmid, w_down_b, l, x, mod, 5, N_SAMPLE_TILES, n_ptiles,
                                           out_tile0=0, out_rows=T_PROMPT)

    kv_shape = (BATCH, DEPTH, SEQ, NA_HEADS, NA_HD)
    return (y_prompt.reshape(BATCH, SEQ, D_MODEL), y_sample.reshape(DEC_BATCH, DEC_SEQ, D_MODEL),
            new["k"].reshape(kv_shape), new["v"].reshape(kv_shape),
            new["lru_f"].reshape(BATCH, DEPTH, D_RNN), new["lru_b"].reshape(BATCH, DEPTH, D_RNN),
            new["ret_f"], new["ret_b"], new["ssd_f"], new["ssd_b"])
```
